```python
import math
import jax, jax.numpy as jnp
from jax import lax
import numpy as np

D_MODEL = 1024
BATCH = 8
SEQ = 4096
DEPTH = 2

RMS_EPS = 1e-6
ROPE_THETA = 10000.0
ATTN_Q_BLOCK = 128

MLA_HEADS = 4
MLA_NOPE_DIM = 128
MLA_ROPE_DIM = 64
MLA_V_DIM = 128
MLA_QK_DIM = MLA_NOPE_DIM + MLA_ROPE_DIM
MLA_Q_RANK = 384
MLA_KV_RANK = 256
MLA_WIDTH = MLA_HEADS * MLA_V_DIM

DN_HEADS = 4
DN_HEAD_DIM = 128
DN_WIDTH = DN_HEADS * DN_HEAD_DIM
DN_CONV = 4
DN_CHUNK = 64

DIL_WINDOWS = (128, 512, 2048)
DIL_DILATIONS = (1, 4, 16)
DIL_GROUPS = 3
DIL_HEADS_PER_GROUP = 4
DIL_HEAD_DIM = 128
DIL_QKV_WIDTH = DIL_GROUPS * DIL_HEADS_PER_GROUP * DIL_HEAD_DIM
DIL_WIDTH = DIL_HEADS_PER_GROUP * DIL_HEAD_DIM
DIL_BLOCK = 128

N_BRANCHES = 3
BRANCH_WIDTH = 512
IN_SPLITS = (MLA_Q_RANK, MLA_KV_RANK + MLA_ROPE_DIM, MLA_WIDTH,
             3 * DN_WIDTH, DN_HEADS, DN_HEADS, DN_WIDTH,
             3 * DIL_QKV_WIDTH, DIL_WIDTH,
             N_BRANCHES * D_MODEL)
IN_WIDTH = (MLA_Q_RANK + MLA_KV_RANK + MLA_ROPE_DIM + MLA_WIDTH
            + 3 * DN_WIDTH + 2 * DN_HEADS + DN_WIDTH
            + 3 * DIL_QKV_WIDTH + DIL_WIDTH + N_BRANCHES * D_MODEL)

kernel_name = "hybrid_mla_gdn_dilated_gated_merge"


def rms_norm(x, g):
    xf = x.astype(jnp.float32)
    y = xf * lax.rsqrt(jnp.mean(xf * xf, axis=-1, keepdims=True) + RMS_EPS)
    return (y * g.astype(jnp.float32)).astype(x.dtype)


def l2_normalize(x):
    return x * lax.rsqrt(jnp.sum(x * x, axis=-1, keepdims=True) + 1e-6)


def rope_tables(positions, dim):
    inv_freq = 1.0 / (ROPE_THETA ** (jnp.arange(0, dim, 2, dtype=jnp.float32) / dim))
    ang = positions.astype(jnp.float32)[..., None] * inv_freq
    return jnp.cos(ang), jnp.sin(ang)


def apply_rope(x, cos, sin):
    half = x.shape[-1] // 2
    xf = x.astype(jnp.float32)
    x1, x2 = xf[..., :half], xf[..., half:]
    c, s = cos[:, :, None, :], sin[:, :, None, :]
    return jnp.concatenate([x1 * c - x2 * s, x2 * c + x1 * s], axis=-1).astype(x.dtype)


def split_last(t, sizes):
    cuts, acc = [], 0
    for size in sizes[:-1]:
        acc += size
        cuts.append(acc)
    return jnp.split(t, cuts, axis=-1)


def causal_depthwise_conv(x, w):
    k = w.shape[0]
    return lax.conv_general_dilated(
        x, w[:, None, :].astype(x.dtype), window_strides=(1,), padding=[(k - 1, 0)],
        dimension_numbers=('NWC', 'WIO', 'NWC'), feature_group_count=x.shape[-1])


def causal_block_attention(q, k, v, scale):
    B, S, H, dk = q.shape
    dv = v.shape[-1]
    nb = S // ATTN_Q_BLOCK
    q_blocks = jnp.moveaxis(q.reshape(B, nb, ATTN_Q_BLOCK, H, dk), 1, 0)
    k_pos = jnp.arange(S)

    def one_block(args):
        q_blk, blk = args
        s = jnp.einsum('bqhd,bkhd->bhqk', q_blk, k, preferred_element_type=jnp.float32) * scale
        q_pos = blk * ATTN_Q_BLOCK + jnp.arange(ATTN_Q_BLOCK)
        s = jnp.where(k_pos[None, :] <= q_pos[:, None], s, -jnp.inf)
        p = jax.nn.softmax(s, axis=-1)
        return jnp.einsum('bhqk,bkhd->bqhd', p.astype(v.dtype), v)

    o = lax.map(one_block, (q_blocks, jnp.arange(nb)))
    return jnp.moveaxis(o, 0, 1).reshape(B, S, H, dv)


def mla_branch(q_lat, kv_lat, cos, sin, q_a_norm_g, w_q_b, kv_a_norm_g, w_kv_b, q_norm_g, k_norm_g):
    B, S, _ = q_lat.shape
    q = jnp.einsum('bsr,re->bse', rms_norm(q_lat, q_a_norm_g), w_q_b).reshape(B, S, MLA_HEADS, MLA_QK_DIM)
    c_kv, k_pe = kv_lat[..., :MLA_KV_RANK], kv_lat[..., MLA_KV_RANK:]
    kv = jnp.einsum('bsr,re->bse', rms_norm(c_kv, kv_a_norm_g), w_kv_b).reshape(B, S, MLA_HEADS, MLA_NOPE_DIM + MLA_V_DIM)
    k_nope, v = kv[..., :MLA_NOPE_DIM], kv[..., MLA_NOPE_DIM:]
    q_nope = rms_norm(q[..., :MLA_NOPE_DIM], q_norm_g[:MLA_NOPE_DIM])
    q_pe = apply_rope(rms_norm(q[..., MLA_NOPE_DIM:], q_norm_g[MLA_NOPE_DIM:]), cos, sin)
    k_nope = rms_norm(k_nope, k_norm_g[:MLA_NOPE_DIM])
    k_pe = apply_rope(rms_norm(k_pe, k_norm_g[MLA_NOPE_DIM:])[:, :, None, :], cos, sin)
    q_full = jnp.concatenate([q_nope, q_pe], axis=-1)
    k_full = jnp.concatenate([k_nope, jnp.broadcast_to(k_pe, (B, S, MLA_HEADS, MLA_ROPE_DIM))], axis=-1)
    o = causal_block_attention(q_full, k_full, v, MLA_QK_DIM ** -0.5)
    return o.reshape(B, S, MLA_WIDTH)


def chunk_gated_delta_rule(q, k, v, g, beta):
    B, S, H, dk = q.shape
    dv = v.shape[-1]
    C = DN_CHUNK
    N = S // C

    def chunks(t):
        t = jnp.moveaxis(t, 2, 1)
        return t.reshape(B, H, N, C, *t.shape[3:])

    q, k, v, g, beta = (chunks(t) for t in (q, k, v, g, beta))
    gc = jnp.cumsum(g, axis=-1)
    idx = jnp.arange(C)
    incl = idx[:, None] >= idx[None, :]
    strict = idx[:, None] > idx[None, :]
    decay = jnp.exp(jnp.where(incl, gc[..., :, None] - gc[..., None, :], -jnp.inf))
    k_beta = k * beta[..., None]
    lower = jnp.where(strict, jnp.einsum('bhncd,bhnmd->bhncm', k_beta, k) * decay, 0.0)
    rhs = jnp.concatenate([v * beta[..., None], k_beta * jnp.exp(gc)[..., None]], axis=-1)
    sol = lax.linalg.triangular_solve(lower + jnp.eye(C, dtype=lower.dtype), rhs, left_side=True, lower=True)
    u, w = sol[..., :dv], sol[..., dv:]
    qk = jnp.einsum('bhncd,bhnmd->bhncm', q, k) * decay

    def step(state, inp):
        q_i, k_i, u_i, w_i, gc_i, qk_i = inp
        v_new = u_i - jnp.einsum('bhcd,bhde->bhce', w_i, state)
        o_i = (jnp.einsum('bhcd,bhde->bhce', q_i * jnp.exp(gc_i)[..., None], state)
               + jnp.einsum('bhcm,bhme->bhce', qk_i, v_new))
        g_last = gc_i[..., -1:]
        state = (state * jnp.exp(g_last)[..., None]
                 + jnp.einsum('bhcd,bhce->bhde', k_i * jnp.exp(g_last - gc_i)[..., None], v_new))
        return state, o_i

    xs = tuple(jnp.moveaxis(t, 2, 0) for t in (q, k, u, w, gc, qk))
    state0 = jnp.zeros((B, H, dk, dv), jnp.float32)
    _, o = lax.scan(step, state0, xs)
    o = jnp.moveaxis(o, 0, 2).reshape(B, H, S, dv)
    return jnp.moveaxis(o, 1, 2)


def gated_deltanet_branch(qkv, a, b, conv_w, a_log, dt_bias, out_norm_g):
    B, S, _ = qkv.shape
    mixed = jax.nn.silu(causal_depthwise_conv(qkv, conv_w)).astype(jnp.float32)
    q, k, v = jnp.split(mixed, 3, axis=-1)
    q = l2_normalize(q.reshape(B, S, DN_HEADS, DN_HEAD_DIM)) * (DN_HEAD_DIM ** -0.5)
    k = l2_normalize(k.reshape(B, S, DN_HEADS, DN_HEAD_DIM))
    v = v.reshape(B, S, DN_HEADS, DN_HEAD_DIM)
    beta = jax.nn.sigmoid(b.astype(jnp.float32))
    g = -jnp.exp(a_log.astype(jnp.float32)) * jax.nn.softplus(a.astype(jnp.float32) + dt_bias.astype(jnp.float32))
    o = chunk_gated_delta_rule(q, k, v, g, beta)
    return rms_norm(o, out_norm_g).astype(qkv.dtype).reshape(B, S, DN_WIDTH)


def dilated_window_attention(q, k, v, window, dilation):
    B, S, H, hd = q.shape
    reach = window // dilation
    L = S // dilation
    nb = -(-L // DIL_BLOCK)
    Lp = nb * DIL_BLOCK

    def to_blocks(t):
        t = jnp.swapaxes(t.reshape(B, L, dilation, H, hd), 1, 2)
        t = jnp.pad(t, ((0, 0), (0, 0), (0, Lp - L), (0, 0), (0, 0)))
        return t.reshape(B, dilation, nb, DIL_BLOCK, H, hd)

    def with_previous(t):
        prev = jnp.pad(t[:, :, :-1], ((0, 0), (0, 0), (1, 0), (0, 0), (0, 0), (0, 0)))
        return jnp.concatenate([prev, t], axis=3)

    qb = to_blocks(q)
    kb = with_previous(to_blocks(k))
    vb = with_previous(to_blocks(v))
    s = jnp.einsum('bgnqhd,bgnkhd->bgnhqk', qb, kb, preferred_element_type=jnp.float32) * (hd ** -0.5)
    qi = jnp.arange(DIL_BLOCK)[:, None]
    kc = jnp.arange(2 * DIL_BLOCK)[None, :]
    dist = DIL_BLOCK + qi - kc
    band = (dist >= 0) & (dist <= reach)
    has_prev = (jnp.arange(nb) > 0)[:, None, None] | (kc >= DIL_BLOCK)[None]
    valid = band[None] & has_prev
    s = jnp.where(valid[:, None], s, -jnp.inf)
    m = jnp.max(s, axis=-1, keepdims=True)
    e = jnp.exp(s - m)
    den = jnp.sum(e, axis=-1)
    o = jnp.einsum('bgnhqk,bgnkhd->bgnqhd', e, vb.astype(jnp.float32)) / jnp.swapaxes(den, -1, -2)[..., None]
    lse = jnp.swapaxes(m[..., 0] + jnp.log(den), -1, -2)

    def from_blocks(t):
        t = t.reshape(B, dilation, Lp, *t.shape[4:])[:, :, :L]
        return jnp.swapaxes(t, 1, 2).reshape(B, S, *t.shape[3:])

    return from_blocks(o), from_blocks(lse)


def dilated_branch(qkv, cos, sin, q_norm_g, k_norm_g):
    B, S, _ = qkv.shape
    n_heads = DIL_GROUPS * DIL_HEADS_PER_GROUP
    q, k, v = (t.reshape(B, S, n_heads, DIL_HEAD_DIM) for t in jnp.split(qkv, 3, axis=-1))
    q = apply_rope(rms_norm(q, q_norm_g), cos, sin)
    k = apply_rope(rms_norm(k, k_norm_g), cos, sin)
    grp = (B, S, DIL_GROUPS, DIL_HEADS_PER_GROUP, DIL_HEAD_DIM)
    q, k, v = q.reshape(grp), k.reshape(grp), v.reshape(grp)
    outs, lses = [], []
    for gi in range(DIL_GROUPS):
        o_g, lse_g = dilated_window_attention(q[:, :, gi], k[:, :, gi], v[:, :, gi], DIL_WINDOWS[gi], DIL_DILATIONS[gi])
        outs.append(o_g)
        lses.append(lse_g)
    wts = jax.nn.softmax(jnp.stack(lses, axis=0), axis=0)
    o = jnp.sum(wts[..., None] * jnp.stack(outs, axis=0), axis=0)
    return o.reshape(B, S, DIL_WIDTH).astype(qkv.dtype)


def hybrid_layer(x, cos_r, sin_r, cos_h, sin_h, norm_g, w_in, mla_q_a_norm_g, mla_w_q_b,
                 mla_kv_a_norm_g, mla_w_kv_b, mla_q_norm_g, mla_k_norm_g, dn_conv_w, dn_a_log,
                 dn_dt_bias, dn_out_norm_g, dil_q_norm_g, dil_k_norm_g, w_branch, w_out):
    B, S, _ = x.shape
    h = rms_norm(x, norm_g)
    proj = jnp.einsum('bsd,de->bse', h, w_in)
    (q_lat, kv_lat, z_a, dn_qkv, dn_a, dn_b, z_b, dil_qkv, z_c, gate_logits) = split_last(proj, IN_SPLITS)
    y_a = mla_branch(q_lat, kv_lat, cos_r, sin_r, mla_q_a_norm_g, mla_w_q_b, mla_kv_a_norm_g,
                     mla_w_kv_b, mla_q_norm_g, mla_k_norm_g)
    y_b = gated_deltanet_branch(dn_qkv, dn_a, dn_b, dn_conv_w, dn_a_log, dn_dt_bias, dn_out_norm_g)
    y_c = dilated_branch(dil_qkv, cos_h, sin_h, dil_q_norm_g, dil_k_norm_g)
    ys = jnp.stack([y_a * jax.nn.silu(z_a), y_b * jax.nn.silu(z_b), y_c * jax.nn.silu(z_c)], axis=2)
    branch_out = jnp.einsum('bsnc,ncd->bsnd', ys, w_branch)
    gates = jax.nn.sigmoid(gate_logits.reshape(B, S, N_BRANCHES, D_MODEL))
    mixed = jnp.sum(gates * branch_out, axis=2)
    return x + jnp.einsum('bsd,de->bse', mixed, w_out)


def _fwd_setup_inputs(seed: int = 0) -> dict:
    key = jax.random.key(seed)
    ks = jax.random.split(key, 24)
    f32 = jnp.float32

    def normal(k, shape, fan_in):
        return jax.random.normal(k, shape, f32) * (fan_in ** -0.5)

    def gain(k, shape):
        return 1.0 + 0.02 * jax.random.normal(k, shape, f32)

    x = jax.random.normal(ks[0], (BATCH, SEQ, D_MODEL), f32)
    offsets = jax.random.randint(ks[1], (BATCH, 1), 0, 1024, dtype=jnp.int32)
    positions = jnp.arange(SEQ, dtype=jnp.int32)[None, :] + offsets
    dt = jnp.exp(jax.random.uniform(ks[14], (DEPTH, DN_HEADS), f32, math.log(1e-3), math.log(1e-1)))
    return {
        "x": x,
        "positions": positions,
        "norm_g": gain(ks[2], (DEPTH, D_MODEL)),
        "w_in": normal(ks[3], (DEPTH, D_MODEL, IN_WIDTH), D_MODEL),
        "mla_q_a_norm_g": gain(ks[4], (DEPTH, MLA_Q_RANK)),
        "mla_w_q_b": normal(ks[5], (DEPTH, MLA_Q_RANK, MLA_HEADS * MLA_QK_DIM), MLA_Q_RANK),
        "mla_kv_a_norm_g": gain(ks[6], (DEPTH, MLA_KV_RANK)),
        "mla_w_kv_b": normal(ks[7], (DEPTH, MLA_KV_RANK, MLA_HEADS * (MLA_NOPE_DIM + MLA_V_DIM)), MLA_KV_RANK),
        "mla_q_norm_g": gain(ks[8], (DEPTH, MLA_QK_DIM)),
        "mla_k_norm_g": gain(ks[9], (DEPTH, MLA_QK_DIM)),
        "dn_conv_w": normal(ks[10], (DEPTH, DN_CONV, 3 * DN_WIDTH), DN_CONV),
        "dn_a_log": jnp.log(jax.random.uniform(ks[11], (DEPTH, DN_HEADS), f32, 1.0, 16.0)),
        "dn_dt_bias": dt + jnp.log(-jnp.expm1(-dt)),
        "dn_out_norm_g": gain(ks[12], (DEPTH, DN_HEAD_DIM)),
        "dil_q_norm_g": gain(ks[13], (DEPTH, DIL_HEAD_DIM)),
        "dil_k_norm_g": gain(ks[15], (DEPTH, DIL_HEAD_DIM)),
        "w_branch": normal(ks[16], (DEPTH, N_BRANCHES, BRANCH_WIDTH, D_MODEL), BRANCH_WIDTH),
        "w_out": normal(ks[17], (DEPTH, D_MODEL, D_MODEL), D_MODEL),
    }


def _fwd_reference(x, positions, norm_g, w_in, mla_q_a_norm_g, mla_w_q_b, mla_kv_a_norm_g, mla_w_kv_b,
              mla_q_norm_g, mla_k_norm_g, dn_conv_w, dn_a_log, dn_dt_bias, dn_out_norm_g,
              dil_q_norm_g, dil_k_norm_g, w_branch, w_out):
    cos_r, sin_r = rope_tables(positions, MLA_ROPE_DIM)
    cos_h, sin_h = rope_tables(positions, DIL_HEAD_DIM)
    for layer in range(DEPTH):
        x = hybrid_layer(x, cos_r, sin_r, cos_h, sin_h, norm_g[layer], w_in[layer],
                         mla_q_a_norm_g[layer], mla_w_q_b[layer], mla_kv_a_norm_g[layer],
                         mla_w_kv_b[layer], mla_q_norm_g[layer], mla_k_norm_g[layer],
                         dn_conv_w[layer], dn_a_log[layer], dn_dt_bias[layer], dn_out_norm_g[layer],
                         dil_q_norm_g[layer], dil_k_norm_g[layer], w_branch[layer], w_out[layer])
    return x


import jax as _jax
import jax.numpy as _jnp

TWIN_FORMAT = 'train_step'
FWD_PARAMS = ['x', 'positions', 'norm_g', 'w_in', 'mla_q_a_norm_g', 'mla_w_q_b', 'mla_kv_a_norm_g', 'mla_w_kv_b', 'mla_q_norm_g', 'mla_k_norm_g', 'dn_conv_w', 'dn_a_log', 'dn_dt_bias', 'dn_out_norm_g', 'dil_q_norm_g', 'dil_k_norm_g', 'w_branch', 'w_out']
TWIN_WEIGHTS = ['norm_g', 'w_in', 'mla_q_a_norm_g', 'mla_w_q_b', 'mla_kv_a_norm_g', 'mla_w_kv_b', 'mla_q_norm_g', 'mla_k_norm_g', 'dn_conv_w', 'dn_a_log', 'dn_dt_bias', 'dn_out_norm_g', 'dil_q_norm_g', 'dil_k_norm_g', 'w_branch', 'w_out']
TWIN_DIFF_INPUT = 'x'
TWIN_INPUTS = ['x', 'positions', 'norm_g', 'w_in', 'mla_q_a_norm_g', 'mla_w_q_b', 'mla_kv_a_norm_g', 'mla_w_kv_b', 'mla_q_norm_g', 'mla_k_norm_g', 'dn_conv_w', 'dn_a_log', 'dn_dt_bias', 'dn_out_norm_g', 'dil_q_norm_g', 'dil_k_norm_g', 'w_branch', 'w_out', 'loss_target', 'm_norm_g', 'm_w_in', 'm_mla_q_a_norm_g', 'm_mla_w_q_b', 'm_mla_kv_a_norm_g', 'm_mla_w_kv_b', 'm_mla_q_norm_g', 'm_mla_k_norm_g', 'm_dn_conv_w', 'm_dn_a_log', 'm_dn_dt_bias', 'm_dn_out_norm_g', 'm_dil_q_norm_g', 'm_dil_k_norm_g', 'm_w_branch', 'm_w_out', 'v_norm_g', 'v_w_in', 'v_mla_q_a_norm_g', 'v_mla_w_q_b', 'v_mla_kv_a_norm_g', 'v_mla_w_kv_b', 'v_mla_q_norm_g', 'v_mla_k_norm_g', 'v_dn_conv_w', 'v_dn_a_log', 'v_dn_dt_bias', 'v_dn_out_norm_g', 'v_dil_q_norm_g', 'v_dil_k_norm_g', 'v_w_branch', 'v_w_out']
TWIN_OUTPUTS = ['loss', 'grad_x', 'grad_norm_g', 'grad_w_in', 'grad_mla_q_a_norm_g', 'grad_mla_w_q_b', 'grad_mla_kv_a_norm_g', 'grad_mla_w_kv_b', 'grad_mla_q_norm_g', 'grad_mla_k_norm_g', 'grad_dn_conv_w', 'grad_dn_a_log', 'grad_dn_dt_bias', 'grad_dn_out_norm_g', 'grad_dil_q_norm_g', 'grad_dil_k_norm_g', 'grad_w_branch', 'grad_w_out', 'delta_norm_g', 'delta_w_in', 'delta_mla_q_a_norm_g', 'delta_mla_w_q_b', 'delta_mla_kv_a_norm_g', 'delta_mla_w_kv_b', 'delta_mla_q_norm_g', 'delta_mla_k_norm_g', 'delta_dn_conv_w', 'delta_dn_a_log', 'delta_dn_dt_bias', 'delta_dn_out_norm_g', 'delta_dil_q_norm_g', 'delta_dil_k_norm_g', 'delta_w_branch', 'delta_w_out', 'new_m_norm_g', 'new_m_w_in', 'new_m_mla_q_a_norm_g', 'new_m_mla_w_q_b', 'new_m_mla_kv_a_norm_g', 'new_m_mla_w_kv_b', 'new_m_mla_q_norm_g', 'new_m_mla_k_norm_g', 'new_m_dn_conv_w', 'new_m_dn_a_log', 'new_m_dn_dt_bias', 'new_m_dn_out_norm_g', 'new_m_dil_q_norm_g', 'new_m_dil_k_norm_g', 'new_m_w_branch', 'new_m_w_out', 'new_v_norm_g', 'new_v_w_in', 'new_v_mla_q_a_norm_g', 'new_v_mla_w_q_b', 'new_v_mla_kv_a_norm_g', 'new_v_mla_w_kv_b', 'new_v_mla_q_norm_g', 'new_v_mla_k_norm_g', 'new_v_dn_conv_w', 'new_v_dn_a_log', 'new_v_dn_dt_bias', 'new_v_dn_out_norm_g', 'new_v_dil_q_norm_g', 'new_v_dil_k_norm_g', 'new_v_w_branch', 'new_v_w_out']
TWIN_LEAF_KINDS = {'loss': 'loss', 'grad_x': 'grad_x', 'grad_norm_g': 'grad_w', 'grad_w_in': 'grad_w', 'grad_mla_q_a_norm_g': 'grad_w', 'grad_mla_w_q_b': 'grad_w', 'grad_mla_kv_a_norm_g': 'grad_w', 'grad_mla_w_kv_b': 'grad_w', 'grad_mla_q_norm_g': 'grad_w', 'grad_mla_k_norm_g': 'grad_w', 'grad_dn_conv_w': 'grad_w', 'grad_dn_a_log': 'grad_w', 'grad_dn_dt_bias': 'grad_w', 'grad_dn_out_norm_g': 'grad_w', 'grad_dil_q_norm_g': 'grad_w', 'grad_dil_k_norm_g': 'grad_w', 'grad_w_branch': 'grad_w', 'grad_w_out': 'grad_w', 'delta_norm_g': 'delta_w', 'delta_w_in': 'delta_w', 'delta_mla_q_a_norm_g': 'delta_w', 'delta_mla_w_q_b': 'delta_w', 'delta_mla_kv_a_norm_g': 'delta_w', 'delta_mla_w_kv_b': 'delta_w', 'delta_mla_q_norm_g': 'delta_w', 'delta_mla_k_norm_g': 'delta_w', 'delta_dn_conv_w': 'delta_w', 'delta_dn_a_log': 'delta_w', 'delta_dn_dt_bias': 'delta_w', 'delta_dn_out_norm_g': 'delta_w', 'delta_dil_q_norm_g': 'delta_w', 'delta_dil_k_norm_g': 'delta_w', 'delta_w_branch': 'delta_w', 'delta_w_out': 'delta_w', 'new_m_norm_g': 'new_m', 'new_m_w_in': 'new_m', 'new_m_mla_q_a_norm_g': 'new_m', 'new_m_mla_w_q_b': 'new_m', 'new_m_mla_kv_a_norm_g': 'new_m', 'new_m_mla_w_kv_b': 'new_m', 'new_m_mla_q_norm_g': 'new_m', 'new_m_mla_k_norm_g': 'new_m', 'new_m_dn_conv_w': 'new_m', 'new_m_dn_a_log': 'new_m', 'new_m_dn_dt_bias': 'new_m', 'new_m_dn_out_norm_g': 'new_m', 'new_m_dil_q_norm_g': 'new_m', 'new_m_dil_k_norm_g': 'new_m', 'new_m_w_branch': 'new_m', 'new_m_w_out': 'new_m', 'new_v_norm_g': 'new_v', 'new_v_w_in': 'new_v', 'new_v_mla_q_a_norm_g': 'new_v', 'new_v_mla_w_q_b': 'new_v', 'new_v_mla_kv_a_norm_g': 'new_v', 'new_v_mla_w_kv_b': 'new_v', 'new_v_mla_q_norm_g': 'new_v', 'new_v_mla_k_norm_g': 'new_v', 'new_v_dn_conv_w': 'new_v', 'new_v_dn_a_log': 'new_v', 'new_v_dn_dt_bias': 'new_v', 'new_v_dn_out_norm_g': 'new_v', 'new_v_dil_q_norm_g': 'new_v', 'new_v_dil_k_norm_g': 'new_v', 'new_v_w_branch': 'new_v', 'new_v_w_out': 'new_v'}


def _forward(args):
    return _fwd_reference(*[args[k] for k in FWD_PARAMS])


def _output_shape():
    out = _jax.eval_shape(lambda: _forward(_fwd_setup_inputs(0)))
    return out.shape, out.dtype

N_MICROBATCH = 1
ADAM_LR = 0.001
ADAM_B1 = 0.9
ADAM_B2 = 0.999
ADAM_EPS = 1e-08
ADAM_WD = 0.01
ADAM_STEP = 10
PER_EXAMPLE_BATCH_AXIS = {'x': 0, 'positions': 0, 'loss_target': 0}
SHARED_INPUTS = []
_WEIGHT_DTYPES = {'norm_g': _jnp.float32, 'w_in': _jnp.float32, 'mla_q_a_norm_g': _jnp.float32, 'mla_w_q_b': _jnp.float32, 'mla_kv_a_norm_g': _jnp.float32, 'mla_w_kv_b': _jnp.float32, 'mla_q_norm_g': _jnp.float32, 'mla_k_norm_g': _jnp.float32, 'dn_conv_w': _jnp.float32, 'dn_a_log': _jnp.float32, 'dn_dt_bias': _jnp.float32, 'dn_out_norm_g': _jnp.float32, 'dil_q_norm_g': _jnp.float32, 'dil_k_norm_g': _jnp.float32, 'w_branch': _jnp.float32, 'w_out': _jnp.float32}
MOMENT_SCALE = {'norm_g': 4.157259e+00, 'w_in': 1.002754e-01, 'mla_q_a_norm_g': 3.252106e-02, 'mla_w_q_b': 2.372906e-02, 'mla_kv_a_norm_g': 1.421321e-01, 'mla_w_kv_b': 3.231512e-02, 'mla_q_norm_g': 1.019675e-01, 'mla_k_norm_g': 1.026950e-01, 'dn_conv_w': 2.751556e-01, 'dn_a_log': 9.696857e+00, 'dn_dt_bias': 8.987505e+00, 'dn_out_norm_g': 2.999822e+01, 'dil_q_norm_g': 1.210883e-01, 'dil_k_norm_g': 1.222933e-01, 'w_branch': 2.390206e-01, 'w_out': 3.457909e-01}


def _to_microbatches(a, axis):
    t = _jnp.moveaxis(a, axis, 0)
    t = t.reshape((N_MICROBATCH, t.shape[0] // N_MICROBATCH) + t.shape[1:])
    return _jnp.moveaxis(t, 1, axis + 1)


def setup_inputs(seed: int = 0) -> dict:
    inp = _fwd_setup_inputs(seed)
    key = _jax.random.fold_in(_jax.random.key(seed), 7919)
    shape, _ = _output_shape()
    out = dict(inp)
    out["loss_target"] = _jax.random.normal(_jax.random.fold_in(key, 0), shape, _jnp.float32)
    for i, name in enumerate(TWIN_WEIGHTS):
        w = inp[name].astype(_jnp.float32)
        if MOMENT_SCALE is None:
            s = _jnp.sqrt(_jnp.mean(_jnp.square(w)) + 1e-30)
        else:
            s = MOMENT_SCALE[name]
        km, kv = _jax.random.split(_jax.random.fold_in(key, i + 1))
        out[name] = w
        out["m_" + name] = s * _jax.random.normal(km, w.shape, _jnp.float32)
        out["v_" + name] = (s * s) * _jax.random.uniform(kv, w.shape, _jnp.float32, 0.5, 1.5)
    if N_MICROBATCH > 1:
        for name, axis in PER_EXAMPLE_BATCH_AXIS.items():
            out[name] = _to_microbatches(out[name], axis)
    return {'x': out['x'], 'positions': out['positions'], 'norm_g': out['norm_g'], 'w_in': out['w_in'], 'mla_q_a_norm_g': out['mla_q_a_norm_g'], 'mla_w_q_b': out['mla_w_q_b'], 'mla_kv_a_norm_g': out['mla_kv_a_norm_g'], 'mla_w_kv_b': out['mla_w_kv_b'], 'mla_q_norm_g': out['mla_q_norm_g'], 'mla_k_norm_g': out['mla_k_norm_g'], 'dn_conv_w': out['dn_conv_w'], 'dn_a_log': out['dn_a_log'], 'dn_dt_bias': out['dn_dt_bias'], 'dn_out_norm_g': out['dn_out_norm_g'], 'dil_q_norm_g': out['dil_q_norm_g'], 'dil_k_norm_g': out['dil_k_norm_g'], 'w_branch': out['w_branch'], 'w_out': out['w_out'], 'loss_target': out['loss_target'], 'm_norm_g': out['m_norm_g'], 'm_w_in': out['m_w_in'], 'm_mla_q_a_norm_g': out['m_mla_q_a_norm_g'], 'm_mla_w_q_b': out['m_mla_w_q_b'], 'm_mla_kv_a_norm_g': out['m_mla_kv_a_norm_g'], 'm_mla_w_kv_b': out['m_mla_w_kv_b'], 'm_mla_q_norm_g': out['m_mla_q_norm_g'], 'm_mla_k_norm_g': out['m_mla_k_norm_g'], 'm_dn_conv_w': out['m_dn_conv_w'], 'm_dn_a_log': out['m_dn_a_log'], 'm_dn_dt_bias': out['m_dn_dt_bias'], 'm_dn_out_norm_g': out['m_dn_out_norm_g'], 'm_dil_q_norm_g': out['m_dil_q_norm_g'], 'm_dil_k_norm_g': out['m_dil_k_norm_g'], 'm_w_branch': out['m_w_branch'], 'm_w_out': out['m_w_out'], 'v_norm_g': out['v_norm_g'], 'v_w_in': out['v_w_in'], 'v_mla_q_a_norm_g': out['v_mla_q_a_norm_g'], 'v_mla_w_q_b': out['v_mla_w_q_b'], 'v_mla_kv_a_norm_g': out['v_mla_kv_a_norm_g'], 'v_mla_w_kv_b': out['v_mla_w_kv_b'], 'v_mla_q_norm_g': out['v_mla_q_norm_g'], 'v_mla_k_norm_g': out['v_mla_k_norm_g'], 'v_dn_conv_w': out['v_dn_conv_w'], 'v_dn_a_log': out['v_dn_a_log'], 'v_dn_dt_bias': out['v_dn_dt_bias'], 'v_dn_out_norm_g': out['v_dn_out_norm_g'], 'v_dil_q_norm_g': out['v_dil_q_norm_g'], 'v_dil_k_norm_g': out['v_dil_k_norm_g'], 'v_w_branch': out['v_w_branch'], 'v_w_out': out['v_w_out']}


def _loss(weights, diff, rest, loss_target):
    with _jax.named_scope("forward"):
        args = {**rest, TWIN_DIFF_INPUT: diff, **{k: w.astype(_WEIGHT_DTYPES[k]) for k, w in weights.items()}}
        y = _forward(args)
    with _jax.named_scope("loss_head"):
        err = _jnp.square(y.astype(_jnp.float32) - loss_target)
        return 0.5 * _jnp.sum(_jnp.mean(err, axis=-1)) if err.ndim else 0.5 * err


def _adamw(w, g, m, v):
    m = ADAM_B1 * m + (1.0 - ADAM_B1) * g
    v = ADAM_B2 * v + (1.0 - ADAM_B2) * _jnp.square(g)
    m_hat = m / (1.0 - ADAM_B1 ** ADAM_STEP)
    v_hat = v / (1.0 - ADAM_B2 ** ADAM_STEP)
    delta = -ADAM_LR * (m_hat / (_jnp.sqrt(v_hat) + ADAM_EPS) + ADAM_WD * w)
    return delta, m, v


def reference(x, positions, norm_g, w_in, mla_q_a_norm_g, mla_w_q_b, mla_kv_a_norm_g, mla_w_kv_b, mla_q_norm_g, mla_k_norm_g, dn_conv_w, dn_a_log, dn_dt_bias, dn_out_norm_g, dil_q_norm_g, dil_k_norm_g, w_branch, w_out, loss_target, m_norm_g, m_w_in, m_mla_q_a_norm_g, m_mla_w_q_b, m_mla_kv_a_norm_g, m_mla_w_kv_b, m_mla_q_norm_g, m_mla_k_norm_g, m_dn_conv_w, m_dn_a_log, m_dn_dt_bias, m_dn_out_norm_g, m_dil_q_norm_g, m_dil_k_norm_g, m_w_branch, m_w_out, v_norm_g, v_w_in, v_mla_q_a_norm_g, v_mla_w_q_b, v_mla_kv_a_norm_g, v_mla_w_kv_b, v_mla_q_norm_g, v_mla_k_norm_g, v_dn_conv_w, v_dn_a_log, v_dn_dt_bias, v_dn_out_norm_g, v_dil_q_norm_g, v_dil_k_norm_g, v_w_branch, v_w_out):
    given = dict(x=x, positions=positions, norm_g=norm_g, w_in=w_in, mla_q_a_norm_g=mla_q_a_norm_g, mla_w_q_b=mla_w_q_b, mla_kv_a_norm_g=mla_kv_a_norm_g, mla_w_kv_b=mla_w_kv_b, mla_q_norm_g=mla_q_norm_g, mla_k_norm_g=mla_k_norm_g, dn_conv_w=dn_conv_w, dn_a_log=dn_a_log, dn_dt_bias=dn_dt_bias, dn_out_norm_g=dn_out_norm_g, dil_q_norm_g=dil_q_norm_g, dil_k_norm_g=dil_k_norm_g, w_branch=w_branch, w_out=w_out, loss_target=loss_target, m_norm_g=m_norm_g, m_w_in=m_w_in, m_mla_q_a_norm_g=m_mla_q_a_norm_g, m_mla_w_q_b=m_mla_w_q_b, m_mla_kv_a_norm_g=m_mla_kv_a_norm_g, m_mla_w_kv_b=m_mla_w_kv_b, m_mla_q_norm_g=m_mla_q_norm_g, m_mla_k_norm_g=m_mla_k_norm_g, m_dn_conv_w=m_dn_conv_w, m_dn_a_log=m_dn_a_log, m_dn_dt_bias=m_dn_dt_bias, m_dn_out_norm_g=m_dn_out_norm_g, m_dil_q_norm_g=m_dil_q_norm_g, m_dil_k_norm_g=m_dil_k_norm_g, m_w_branch=m_w_branch, m_w_out=m_w_out, v_norm_g=v_norm_g, v_w_in=v_w_in, v_mla_q_a_norm_g=v_mla_q_a_norm_g, v_mla_w_q_b=v_mla_w_q_b, v_mla_kv_a_norm_g=v_mla_kv_a_norm_g, v_mla_w_kv_b=v_mla_w_kv_b, v_mla_q_norm_g=v_mla_q_norm_g, v_mla_k_norm_g=v_mla_k_norm_g, v_dn_conv_w=v_dn_conv_w, v_dn_a_log=v_dn_a_log, v_dn_dt_bias=v_dn_dt_bias, v_dn_out_norm_g=v_dn_out_norm_g, v_dil_q_norm_g=v_dil_q_norm_g, v_dil_k_norm_g=v_dil_k_norm_g, v_w_branch=v_w_branch, v_w_out=v_w_out)
    weights = {n: given[n] for n in TWIN_WEIGHTS}
    shared = {n: given[n] for n in SHARED_INPUTS}
    per_example = {n: given[n] for n in ['x', 'positions']}
    grad_fn = _jax.value_and_grad(_loss, argnums=(0, 1))

    def one_microbatch(ex, loss_target):
        ex = dict(ex)
        diff = ex.pop(TWIN_DIFF_INPUT)
        return grad_fn(weights, diff, {**shared, **ex}, loss_target)

    if N_MICROBATCH == 1:
        loss, (grad_w, grad_x) = one_microbatch(per_example, given["loss_target"])
    else:
        def body(carry, xs):
            loss_sum, grad_sum = carry
            l_k, (gw_k, gx_k) = one_microbatch(xs[0], xs[1])
            with _jax.named_scope("update"):
                return (loss_sum + l_k, _jax.tree.map(_jnp.add, grad_sum, gw_k)), gx_k

        init = (_jnp.zeros((), _jnp.float32), _jax.tree.map(_jnp.zeros_like, weights))
        (loss, grad_w), grad_x = _jax.lax.scan(body, init, (per_example, given["loss_target"]))
    with _jax.named_scope("update"):
        delta_w, new_m, new_v = {}, {}, {}
        for n in TWIN_WEIGHTS:
            delta_w[n], new_m[n], new_v[n] = _adamw(weights[n], grad_w[n], given["m_" + n], given["v_" + n])
    return (loss, grad_x, *[grad_w[n] for n in TWIN_WEIGHTS], *[delta_w[n] for n in TWIN_WEIGHTS],
            *[new_m[n] for n in TWIN_WEIGHTS], *[new_v[n] for n in TWIN_WEIGHTS])
```

```python
import functools
import math

import jax
import jax.numpy as jnp
from jax import lax
from jax.experimental import pallas as pl
from jax.experimental.pallas import tpu as pltpu

F32 = jnp.float32
BF16 = jnp.bfloat16
HI = lax.Precision.HIGHEST
MESH = pl.DeviceIdType.MESH

N_DEV = 8
D_MODEL = 1024
DEPTH = 2
RMS_EPS = 1e-6
ROPE_THETA = 10000.0
LANES = 128
MLA_HEADS = 4
MLA_ROPE = 64
MLA_QK = 192
MLA_Q_RANK = 384
MLA_KV_RANK = 256
DN_HEADS = 4
DN_CHUNK = 64
DN_CONV = 4
DIL_HEADS = 12
DIL_GROUP_HEADS = 4
DIL_DILATIONS = (1, 4, 16)
DIL_BLOCK = 128
BRANCH_W = 512
IN_WIDTH = 11464
NEG = -1e30
VMEM_LIMIT = 56 * 1024 * 1024

ADAM_LR, ADAM_B1, ADAM_B2, ADAM_EPS, ADAM_WD, ADAM_STEP = 0.001, 0.9, 0.999, 1e-08, 0.01, 10

_SEG = {}
_off = 0
for _n, _w in (("q_lat", 384), ("c_kv", 256), ("k_pe", 64), ("z_a", 512), ("dn_qkv", 1536), ("dn_a", 4), ("dn_b", 4),
               ("z_b", 512), ("dil_qkv", 4608), ("z_c", 512), ("gate", 3072)):
    _SEG[_n] = (_off, _w)
    _off += _w
assert _off == IN_WIDTH


def _pcall(body, **kw):
    return pl.pallas_call(body, **kw)


def _params(sem=None):
    return pltpu.CompilerParams(dimension_semantics=sem, vmem_limit_bytes=VMEM_LIMIT)


def _tile(n, target, mult):
    t = (min(n, target) // mult) * mult
    while t >= mult:
        if n % t == 0:
            return t
        t -= mult
    return n


def _mm(name, a, b, mode, out_dtype=F32, acc=None, tm=1024, tn=512, tk=1024):
    if mode == "nn":
        (M, K), (_, N) = a.shape, b.shape
    elif mode == "nt":
        (M, K), (N, _) = a.shape, b.shape
    else:
        (K, M), (_, N) = a.shape, b.shape
    tm, tn, tk = _tile(M, tm, LANES), _tile(N, tn, LANES), _tile(K, tk, LANES)
    nk = K // tk
    dims = {"nn": (((1,), (0,)), ((), ())), "nt": (((1,), (1,)), ((), ())), "tn": (((0,), (0,)), ((), ()))}[mode]
    a_spec = pl.BlockSpec((tk, tm), lambda i, j, k: (k, i)) if mode == "tn" else pl.BlockSpec((tm, tk), lambda i, j, k: (i, k))
    b_spec = pl.BlockSpec((tn, tk), lambda i, j, k: (j, k)) if mode == "nt" else pl.BlockSpec((tk, tn), lambda i, j, k: (k, j))
    o_spec = pl.BlockSpec((tm, tn), lambda i, j, k: (i, j))
    has_acc = acc is not None

    def body(*refs):
        if has_acc:
            a_ref, b_ref, c_ref, o_ref, acc_ref = refs
        else:
            a_ref, b_ref, o_ref, acc_ref = refs
        k = pl.program_id(2)

        @pl.when(k == 0)
        def _():
            acc_ref[...] = c_ref[...].astype(F32) if has_acc else jnp.zeros_like(acc_ref)

        acc_ref[...] += lax.dot_general(a_ref[...].astype(BF16), b_ref[...].astype(BF16), dims, preferred_element_type=F32)

        @pl.when(k == nk - 1)
        def _():
            o_ref[...] = acc_ref[...].astype(out_dtype)

    ins = [a, b] + ([acc] if has_acc else [])
    in_specs = [a_spec, b_spec] + ([o_spec] if has_acc else [])
    return _pcall(body, name=name, grid=(M // tm, N // tn, nk), in_specs=in_specs, out_specs=o_spec,
                  out_shape=jax.ShapeDtypeStruct((M, N), out_dtype), scratch_shapes=[pltpu.VMEM((tm, tn), F32)],
                  compiler_params=_params(("parallel", "parallel", "arbitrary")))(*ins)


def _make_multi_linear(name, n):
    @jax.custom_vjp
    def op(h, ws):
        return tuple(_mm(f"{name}_fwd{i}", h, w, "nn") for i, w in enumerate(ws))

    def fwd(h, ws):
        return op(h, ws), (h, ws)

    def bwd(res, douts):
        h, ws = res
        dh = None
        for i, (w, d) in enumerate(zip(ws, douts)):
            dh = _mm(f"{name}_dh{i}", d, w, "nt", acc=dh)
        dws = tuple(_mm(f"{name}_dw{i}", h, d, "tn", out_dtype=w.dtype) for i, (w, d) in enumerate(zip(ws, douts)))
        return dh.astype(h.dtype), dws

    op.defvjp(fwd, bwd)
    return op


def _make_resid_linear(name):
    @jax.custom_vjp
    def op(x, a, w):
        return _mm(f"{name}_fwd", a, w, "nn", acc=x)

    def fwd(x, a, w):
        return op(x, a, w), (a, w)

    def bwd(res, dy):
        a, w = res
        return dy, _mm(f"{name}_da", dy, w, "nt"), _mm(f"{name}_dw", a, dy, "tn", out_dtype=w.dtype)

    op.defvjp(fwd, bwd)
    return op


def _make_rowwise(name, f, tile):
    def specs(rows, aux, params, consts, t):
        row = [pl.BlockSpec((t, a.shape[1]), lambda i: (i, 0)) for a in (*rows, *aux)]
        full = [pl.BlockSpec(p.shape, lambda i: (0, 0)) for p in (*params, *consts)]
        return row, full

    def fwd_call(rows, aux, params, consts):
        S = rows[0].shape[0]
        t = min(tile, S)
        n_in = len(rows) + len(aux) + len(params) + len(consts)
        shp = lambda a: jax.ShapeDtypeStruct((t, a.shape[1]), a.dtype)
        outs = jax.eval_shape(f, *[shp(a) for a in (*rows, *aux)], *params, *consts)
        row_specs, full_specs = specs(rows, aux, params, consts, t)

        def body(*refs):
            res = f(*[r[...] for r in refs[:n_in]])
            for o_ref, o in zip(refs[n_in:], res):
                o_ref[...] = o

        return _pcall(body, name=f"{name}_fwd", grid=(S // t,), in_specs=row_specs + full_specs,
                      out_specs=[pl.BlockSpec((t, o.shape[1]), lambda i: (i, 0)) for o in outs],
                      out_shape=[jax.ShapeDtypeStruct((S, o.shape[1]), o.dtype) for o in outs],
                      compiler_params=_params(("parallel",)))(*rows, *aux, *params, *consts)

    def bwd_call(rows, aux, params, consts, douts):
        S = rows[0].shape[0]
        t = min(tile, S)
        nr, na, npar, nc, nd = len(rows), len(aux), len(params), len(consts), len(douts)
        row_specs, full_specs = specs(rows, aux, params, consts, t)

        def body(*refs):
            vals = [r[...] for r in refs[:nr + na + npar + nc]]
            rv, av = vals[:nr], vals[nr:nr + na]
            pv, cv = vals[nr + na:nr + na + npar], vals[nr + na + npar:]
            dv = tuple(r[...] for r in refs[nr + na + npar + nc:nr + na + npar + nc + nd])
            out_refs = refs[nr + na + npar + nc + nd:]
            _, vjp = jax.vjp(lambda *rp: f(*rp[:nr], *av, *rp[nr:], *cv), *rv, *pv)
            grads = vjp(dv)
            for o_ref, g in zip(out_refs[:nr], grads[:nr]):
                o_ref[...] = g
            first = pl.program_id(0) == 0
            for o_ref, g in zip(out_refs[nr:], grads[nr:]):
                @pl.when(first)
                def _(o_ref=o_ref):
                    o_ref[...] = jnp.zeros_like(o_ref)
                o_ref[...] += g

        res = _pcall(body, name=f"{name}_bwd", grid=(S // t,),
                     in_specs=row_specs + full_specs + [pl.BlockSpec((t, d.shape[1]), lambda i: (i, 0)) for d in douts],
                     out_specs=[pl.BlockSpec((t, a.shape[1]), lambda i: (i, 0)) for a in rows]
                     + [pl.BlockSpec(p.shape, lambda i: (0, 0)) for p in params],
                     out_shape=[jax.ShapeDtypeStruct(a.shape, a.dtype) for a in (*rows, *params)],
                     compiler_params=_params(("arbitrary",)))(*rows, *aux, *params, *consts, *douts)
        return tuple(res[:nr]), tuple(res[nr:])

    @jax.custom_vjp
    def op(rows, aux, params, consts):
        return tuple(fwd_call(rows, aux, params, consts))

    def fwd(rows, aux, params, consts):
        return op(rows, aux, params, consts), (rows, aux, params, consts)

    def bwd(res, douts):
        rows, aux, params, consts = res
        drows, dparams = bwd_call(rows, aux, params, consts, tuple(douts))
        zeros = lambda xs: tuple(jnp.zeros_like(a) for a in xs)
        return drows, zeros(aux), dparams, zeros(consts)

    op.defvjp(fwd, bwd)
    return op


@jax.custom_vjp
def _swap_halves(x):
    return pltpu.roll(x, LANES // 2, 1)


_swap_halves.defvjp(lambda x: (_swap_halves(x), None), lambda _, g: (_swap_halves(g),))


def _rope(x, cos_t, sin_t):
    return x * cos_t + _swap_halves(x) * sin_t


def _rms(x, g, n=None):
    n = x.shape[-1] if n is None else n
    return x * lax.rsqrt(jnp.sum(x * x, axis=-1, keepdims=True) * (1.0 / n) + RMS_EPS) * g


def _heads(x):
    return [x[:, i * LANES:(i + 1) * LANES] for i in range(x.shape[1] // LANES)]


def _cat(xs):
    return jnp.concatenate(xs, axis=1)


def _silu(x):
    return x * jax.nn.sigmoid(x)


def _f_norm(x, g):
    return (_rms(x, g),)


def _f_mla_a(q_lat, c_kv, kpe, cos_p, sin_p, qa_g, kva_g, kpe_g):
    kp = _rope(_rms(kpe, kpe_g, MLA_ROPE), cos_p, sin_p)
    return _rms(q_lat, qa_g), _rms(c_kv, kva_g), _cat([kp] * MLA_HEADS)


def _f_mla_b(q8, kn_raw, cos_p, sin_p, qn_g, qp_g, kn_g):
    hs = _heads(q8)
    qn = _cat([_rms(h, qn_g) for h in hs[:MLA_HEADS]])
    qp = _cat([_rope(_rms(h, qp_g, MLA_ROPE), cos_p, sin_p) for h in hs[MLA_HEADS:]])
    kn = _cat([_rms(h, kn_g) for h in _heads(kn_raw)])
    return qn, qp, kn


def _softplus(x):
    return jnp.maximum(x, 0.0) + jnp.log(1.0 + jnp.exp(-jnp.abs(x)))


def _l2n(x):
    return x * lax.rsqrt(jnp.sum(x * x, axis=-1, keepdims=True) + 1e-6)


def _f_dn_pre(mixed, ab, alog_f, dtb_f, e_a, e_b):
    hs = _heads(mixed)
    q = _cat([_l2n(h) * (LANES ** -0.5) for h in hs[:DN_HEADS]])
    k = _cat([_l2n(h) for h in hs[DN_HEADS:2 * DN_HEADS]])
    v = _cat(hs[2 * DN_HEADS:])
    a_f = jnp.dot(ab, e_a, precision=HI, preferred_element_type=F32)
    b_f = jnp.dot(ab, e_b, precision=HI, preferred_element_type=F32)
    g = -jnp.exp(alog_f) * _softplus(a_f + dtb_f)
    return q, k, v, g, jax.nn.sigmoid(b_f)


def _f_dil_pre(qkv, cos_h, sin_h, q_g, k_g):
    hs = _heads(qkv)
    q = _cat([_rope(_rms(h, q_g), cos_h, sin_h) for h in hs[:DIL_HEADS]])
    k = _cat([_rope(_rms(h, k_g), cos_h, sin_h) for h in hs[DIL_HEADS:2 * DIL_HEADS]])
    return q, k, _cat(hs[2 * DIL_HEADS:])


def _f_merge_a(y_a, z_a, o_dn, z_b, o_dil, lse, z_c, out_g):
    y_b = _cat([_rms(h, out_g) for h in _heads(o_dn)])
    os_, ls = _heads(o_dil), _heads(lse)
    y_c = []
    for j in range(DIL_GROUP_HEADS):
        l3 = [ls[g * DIL_GROUP_HEADS + j] for g in range(3)]
        m = jnp.maximum(jnp.maximum(l3[0], l3[1]), l3[2])
        e3 = [jnp.exp(l - m) for l in l3]
        den = e3[0] + e3[1] + e3[2]
        y_c.append(sum(e3[g] * os_[g * DIL_GROUP_HEADS + j] for g in range(3)) / den)
    return y_a * _silu(z_a), y_b * _silu(z_b), _cat(y_c) * _silu(z_c)


def _f_merge_b(b0, b1, b2, gl):
    gs = [jax.nn.sigmoid(gl[:, i * D_MODEL:(i + 1) * D_MODEL]) for i in range(3)]
    return (gs[0] * b0 + gs[1] * b1 + gs[2] * b2,)


def _rope_tables(pos, inv_sign):
    S = pos.shape[0]
    t = min(S, 1024)

    def body(p_ref, c_ref, cp, sp, ch, sh):
        p = p_ref[...].astype(F32)
        c = c_ref[...]
        ang_p, ang_h = p * c[0:1], p * c[2:3]
        cp[...] = jnp.cos(ang_p) * jnp.abs(c[1:2])
        sp[...] = jnp.sin(ang_p) * c[1:2]
        ch[...] = jnp.cos(ang_h)
        sh[...] = jnp.sin(ang_h) * c[3:4]

    row = pl.BlockSpec((t, LANES), lambda i: (i, 0))
    return _pcall(body, name="rope_tables", grid=(S // t,),
                  in_specs=[pl.BlockSpec((t, 1), lambda i: (i, 0)), pl.BlockSpec((4, LANES), lambda i: (0, 0))],
                  out_specs=[row] * 4, out_shape=[jax.ShapeDtypeStruct((S, LANES), F32)] * 4,
                  compiler_params=_params(("parallel",)))(pos, inv_sign)


def _rope_consts():
    half_p, half_h = MLA_ROPE // 2, LANES // 2
    inv_p = 1.0 / (ROPE_THETA ** (jnp.arange(0, MLA_ROPE, 2, dtype=F32) / MLA_ROPE))
    inv_h = 1.0 / (ROPE_THETA ** (jnp.arange(0, LANES, 2, dtype=F32) / LANES))
    z = jnp.zeros((half_p,), F32)
    o = jnp.ones((half_p,), F32)
    return jnp.stack([jnp.concatenate([inv_p, z, inv_p, z]), jnp.concatenate([-o, z, o, z]),
                      jnp.concatenate([inv_h, inv_h]), jnp.concatenate([-jnp.ones((half_h,), F32), jnp.ones((half_h,), F32)])])


def _shift_rows(x, s, up):
    n = x.shape[0]
    r = lax.broadcasted_iota(jnp.int32, x.shape, 0)
    if up:
        return jnp.where(r < n - s, pltpu.roll(x, n - s, 0), 0.0)
    return jnp.where(r >= s, pltpu.roll(x, s, 0), 0.0)


def _make_shift(s):
    @jax.custom_vjp
    def sh(x):
        return _shift_rows(x, s, False)

    sh.defvjp(lambda x: (sh(x), None), lambda _, g: (_shift_rows(g, s, True),))
    return sh


def _f_conv(x, w):
    y = x * w[DN_CONV - 1:DN_CONV]
    for j in range(DN_CONV - 1):
        y = y + _make_shift(DN_CONV - 1 - j)(x) * w[j:j + 1]
    return _silu(y)


def _make_conv(name):
    def call(x, w, dy=None):
        S, C = x.shape
        col = pl.BlockSpec((S, LANES), lambda i: (0, i))
        wsp = pl.BlockSpec((DN_CONV, LANES), lambda i: (0, i))
        if dy is None:
            def body(x_ref, w_ref, o_ref):
                o_ref[...] = _f_conv(x_ref[...], w_ref[...])
            return _pcall(body, name=f"{name}_fwd", grid=(C // LANES,), in_specs=[col, wsp], out_specs=col,
                          out_shape=jax.ShapeDtypeStruct(x.shape, F32), compiler_params=_params(("parallel",)))(x, w)

        def body(x_ref, w_ref, dy_ref, dx_ref, dw_ref):
            _, vjp = jax.vjp(_f_conv, x_ref[...], w_ref[...])
            dx_ref[...], dw_ref[...] = vjp(dy_ref[...])
        return _pcall(body, name=f"{name}_bwd", grid=(C // LANES,), in_specs=[col, wsp, col], out_specs=[col, wsp],
                      out_shape=[jax.ShapeDtypeStruct(x.shape, F32), jax.ShapeDtypeStruct(w.shape, F32)],
                      compiler_params=_params(("parallel",)))(x, w, dy)

    @jax.custom_vjp
    def op(x, w):
        return call(x, w)

    op.defvjp(lambda x, w: (op(x, w), (x, w)), lambda res, dy: tuple(call(*res, dy)))
    return op


def _dot_nt(a, b):
    return lax.dot_general(a.astype(BF16), b.astype(BF16), (((1,), (1,)), ((), ())), preferred_element_type=F32)


def _dot_nn(a, b):
    return jnp.dot(a.astype(BF16), b.astype(BF16), preferred_element_type=F32)


def _dot_tn(a, b):
    return lax.dot_general(a.astype(BF16), b.astype(BF16), (((0,), (0,)), ((), ())), preferred_element_type=F32)


def _causal_mask(qi, ki, t):
    r = lax.broadcasted_iota(jnp.int32, (t, t), 0) + qi * t
    c = lax.broadcasted_iota(jnp.int32, (t, t), 1) + ki * t
    return c <= r


def _make_mla_attn(name):
    scale = MLA_QK ** -0.5

    def fwd_call(qn, qp, kn, kp, v):
        S = qn.shape[0]
        t = min(S, 512)
        n = S // t
        qs = pl.BlockSpec((t, LANES), lambda h, i, j: (i, h))
        ks = pl.BlockSpec((t, LANES), lambda h, i, j: (jnp.minimum(j, i), h))

        def body(qn_r, qp_r, kn_r, kp_r, v_r, o_r, lse_r, m_s, l_s, acc_s):
            qi, ki = pl.program_id(1), pl.program_id(2)

            @pl.when(ki == 0)
            def _():
                m_s[...] = jnp.full_like(m_s, NEG)
                l_s[...] = jnp.zeros_like(l_s)
                acc_s[...] = jnp.zeros_like(acc_s)

            @pl.when(ki <= qi)
            def _():
                s = (_dot_nt(qn_r[...], kn_r[...]) + _dot_nt(qp_r[...], kp_r[...])) * scale
                s = jnp.where(_causal_mask(qi, ki, t), s, NEG)
                m_old = m_s[...]
                m_new = jnp.maximum(m_old, jnp.max(s, axis=-1, keepdims=True))
                p = jnp.exp(s - m_new[:, :1])
                alpha = jnp.exp(m_old - m_new)
                l_s[...] = alpha * l_s[...] + jnp.sum(p, axis=-1, keepdims=True)
                acc_s[...] = alpha * acc_s[...] + _dot_nn(p, v_r[...])
                m_s[...] = m_new

            @pl.when(ki == n - 1)
            def _():
                o_r[...] = acc_s[...] / l_s[...]
                lse_r[...] = m_s[...] + jnp.log(l_s[...])

        return _pcall(body, name=f"{name}_fwd", grid=(MLA_HEADS, n, n), in_specs=[qs, qs, ks, ks, ks], out_specs=[qs, qs],
                      out_shape=[jax.ShapeDtypeStruct((S, MLA_HEADS * LANES), F32)] * 2,
                      scratch_shapes=[pltpu.VMEM((t, LANES), F32)] * 3,
                      compiler_params=_params(("parallel", "parallel", "arbitrary")))(qn, qp, kn, kp, v)

    def dq_call(qn, qp, kn, kp, v, o, lse, do):
        S = qn.shape[0]
        t = min(S, 512)
        n = S // t
        qs = pl.BlockSpec((t, LANES), lambda h, i, j: (i, h))
        ks = pl.BlockSpec((t, LANES), lambda h, i, j: (jnp.minimum(j, i), h))

        def body(qn_r, qp_r, kn_r, kp_r, v_r, o_r, lse_r, do_r, dqn_r, dqp_r, dl_r, dqn_s, dqp_s):
            qi, ki = pl.program_id(1), pl.program_id(2)

            @pl.when(ki == 0)
            def _():
                dqn_s[...] = jnp.zeros_like(dqn_s)
                dqp_s[...] = jnp.zeros_like(dqp_s)
                dl_r[...] = jnp.broadcast_to(jnp.sum(do_r[...] * o_r[...], axis=-1, keepdims=True), dl_r.shape)

            @pl.when(ki <= qi)
            def _():
                s = (_dot_nt(qn_r[...], kn_r[...]) + _dot_nt(qp_r[...], kp_r[...])) * scale
                s = jnp.where(_causal_mask(qi, ki, t), s, NEG)
                p = jnp.exp(s - lse_r[...][:, :1])
                ds = p * (_dot_nt(do_r[...], v_r[...]) - dl_r[...][:, :1]) * scale
                dqn_s[...] += _dot_nn(ds, kn_r[...])
                dqp_s[...] += _dot_nn(ds, kp_r[...])

            @pl.when(ki == n - 1)
            def _():
                dqn_r[...] = dqn_s[...]
                dqp_r[...] = dqp_s[...]

        return _pcall(body, name=f"{name}_dq", grid=(MLA_HEADS, n, n), in_specs=[qs, qs, ks, ks, ks, qs, qs, qs],
                      out_specs=[qs, qs, qs], out_shape=[jax.ShapeDtypeStruct((S, MLA_HEADS * LANES), F32)] * 3,
                      scratch_shapes=[pltpu.VMEM((t, LANES), F32)] * 2,
                      compiler_params=_params(("parallel", "parallel", "arbitrary")))(qn, qp, kn, kp, v, o, lse, do)

    def dkv_call(qn, qp, kn, kp, v, lse, do, dl):
        S = qn.shape[0]
        t = min(S, 512)
        n = S // t
        ks = pl.BlockSpec((t, LANES), lambda h, j, i: (j, h))
        qs = pl.BlockSpec((t, LANES), lambda h, j, i: (jnp.maximum(i, j), h))

        def body(qn_r, qp_r, kn_r, kp_r, v_r, lse_r, do_r, dl_r, dkn_r, dkp_r, dv_r, dkn_s, dkp_s, dv_s):
            ki, qi = pl.program_id(1), pl.program_id(2)

            @pl.when(qi == 0)
            def _():
                dkn_s[...] = jnp.zeros_like(dkn_s)
                dkp_s[...] = jnp.zeros_like(dkp_s)
                dv_s[...] = jnp.zeros_like(dv_s)

            @pl.when(qi >= ki)
            def _():
                s = (_dot_nt(qn_r[...], kn_r[...]) + _dot_nt(qp_r[...], kp_r[...])) * scale
                s = jnp.where(_causal_mask(qi, ki, t), s, NEG)
                p = jnp.exp(s - lse_r[...][:, :1])
                ds = p * (_dot_nt(do_r[...], v_r[...]) - dl_r[...][:, :1]) * scale
                dv_s[...] += _dot_tn(p, do_r[...])
                dkn_s[...] += _dot_tn(ds, qn_r[...])
                dkp_s[...] += _dot_tn(ds, qp_r[...])

            @pl.when(qi == n - 1)
            def _():
                dkn_r[...] = dkn_s[...]
                dkp_r[...] = dkp_s[...]
                dv_r[...] = dv_s[...]

        return _pcall(body, name=f"{name}_dkv", grid=(MLA_HEADS, n, n), in_specs=[qs, qs, ks, ks, ks, qs, qs, qs],
                      out_specs=[ks, ks, ks], out_shape=[jax.ShapeDtypeStruct((S, MLA_HEADS * LANES), F32)] * 3,
                      scratch_shapes=[pltpu.VMEM((t, LANES), F32)] * 3,
                      compiler_params=_params(("parallel", "parallel", "arbitrary")))(qn, qp, kn, kp, v, lse, do, dl)

    @jax.custom_vjp
    def op(qn, qp, kn, kp, v):
        return fwd_call(qn, qp, kn, kp, v)[0]

    def fwd(qn, qp, kn, kp, v):
        o, lse = fwd_call(qn, qp, kn, kp, v)
        return o, (qn, qp, kn, kp, v, o, lse)

    def bwd(res, do):
        qn, qp, kn, kp, v, o, lse = res
        dqn, dqp, dl = dq_call(qn, qp, kn, kp, v, o, lse, do)
        dkn, dkp, dv = dkv_call(qn, qp, kn, kp, v, lse, do, dl)
        return dqn, dqp, dkn, dkp, dv

    op.defvjp(fwd, bwd)
    return op


def _dil_block(q, kp, kc, vp, vc, has_prev):
    scale = LANES ** -0.5
    r = lax.broadcasted_iota(jnp.int32, (DIL_BLOCK, DIL_BLOCK), 0)
    c = lax.broadcasted_iota(jnp.int32, (DIL_BLOCK, DIL_BLOCK), 1)
    s_c = jnp.where(c <= r, _dot_nt(q, kc) * scale, NEG)
    s_p = jnp.where((c >= r) & has_prev, _dot_nt(q, kp) * scale, NEG)
    m = jnp.maximum(jnp.max(s_c, axis=-1, keepdims=True), jnp.max(s_p, axis=-1, keepdims=True))
    e_c, e_p = jnp.exp(s_c - m), jnp.exp(s_p - m)
    den = jnp.sum(e_c, axis=-1, keepdims=True) + jnp.sum(e_p, axis=-1, keepdims=True)
    o = (_dot_nn(e_c, vc) + _dot_nn(e_p, vp)) / den
    return o, jnp.broadcast_to(m + jnp.log(den), o.shape)


def _make_dil_attn(name):
    def call(q, k, v, cts=None):
        S = q.shape[0]
        nb = S // DIL_BLOCK
        cur = pl.BlockSpec((DIL_BLOCK, LANES), lambda h, b: (b, h))
        prev = pl.BlockSpec((DIL_BLOCK, LANES), lambda h, b: (jnp.maximum(b - 1, 0), h))

        def has_prev():
            h, b = pl.program_id(0), pl.program_id(1)
            per_seq = lax.shift_right_logical(jnp.int32(nb), 2 * (h // DIL_GROUP_HEADS))
            return (b & (per_seq - 1)) != 0

        if cts is None:
            def body(q_r, kp_r, kc_r, vp_r, vc_r, o_r, lse_r):
                o_r[...], lse_r[...] = _dil_block(q_r[...], kp_r[...], kc_r[...], vp_r[...], vc_r[...], has_prev())
            return _pcall(body, name=f"{name}_fwd", grid=(DIL_HEADS, nb), in_specs=[cur, prev, cur, prev, cur],
                          out_specs=[cur, cur], out_shape=[jax.ShapeDtypeStruct(q.shape, F32)] * 2,
                          compiler_params=_params(("parallel", "parallel")))(q, k, k, v, v)

        def body(q_r, kp_r, kc_r, vp_r, vc_r, do_r, dl_r, dq_r, dkp_r, dkc_r, dvp_r, dvc_r):
            hp = has_prev()
            _, vjp = jax.vjp(lambda *a: _dil_block(*a, hp), q_r[...], kp_r[...], kc_r[...], vp_r[...], vc_r[...])
            dq_r[...], dkp_r[...], dkc_r[...], dvp_r[...], dvc_r[...] = vjp((do_r[...], dl_r[...]))
        return _pcall(body, name=f"{name}_bwd", grid=(DIL_HEADS, nb), in_specs=[cur, prev, cur, prev, cur, cur, cur],
                      out_specs=[cur] * 5, out_shape=[jax.ShapeDtypeStruct(q.shape, F32)] * 5,
                      compiler_params=_params(("parallel", "parallel")))(q, k, k, v, v, *cts)

    @jax.custom_vjp
    def op(q, k, v):
        return tuple(call(q, k, v))

    def bwd(res, cts):
        dq, dkp, dkc, dvp, dvc = call(*res, cts=cts)
        up = lambda a: jnp.concatenate([a[DIL_BLOCK:], jnp.zeros((DIL_BLOCK, a.shape[1]), a.dtype)], axis=0)
        return dq, dkc + up(dkp), dvc + up(dvp)

    op.defvjp(lambda q, k, v: (op(q, k, v), (q, k, v)), bwd)
    return op


def _to_residues(a):
    S = a.shape[0]
    w = DIL_GROUP_HEADS * LANES
    parts = []
    for g, d in enumerate(DIL_DILATIONS):
        p = a[:, g * w:(g + 1) * w]
        parts.append(p.reshape(S // d, d, w).transpose(1, 0, 2).reshape(S, w))
    return jnp.concatenate(parts, axis=1)


def _from_residues(a):
    S = a.shape[0]
    w = DIL_GROUP_HEADS * LANES
    parts = []
    for g, d in enumerate(DIL_DILATIONS):
        p = a[:, g * w:(g + 1) * w]
        parts.append(p.reshape(d, S // d, w).transpose(1, 0, 2).reshape(S, w))
    return jnp.concatenate(parts, axis=1)


def _hdot(a, b, dims):
    return lax.dot_general(a, b, (dims, ((), ())), precision=HI, preferred_element_type=F32)


def _dn_chunk(q, k, v, g, b, state):
    C = DN_CHUNK
    r = lax.broadcasted_iota(jnp.int32, (C, C), 0)
    c = lax.broadcasted_iota(jnp.int32, (C, C), 1)
    incl, strict = r >= c, r > c
    low = incl.astype(F32)
    eye = (r == c).astype(F32)
    avg = jnp.full((C, LANES), 1.0 / LANES, F32)
    outs, states = [], []
    for h in range(DN_HEADS):
        sl = slice(h * LANES, (h + 1) * LANES)
        qh, kh, vh, gh, bh, sh = q[:, sl], k[:, sl], v[:, sl], g[:, sl], b[:, sl], state[sl, :]
        gc = _hdot(low, gh, ((1,), (0,)))
        gc_i = _hdot(gc, avg, ((1,), (1,)))
        gc_j = _hdot(avg, gc, ((1,), (1,)))
        decay = jnp.exp(jnp.where(incl, gc_i - gc_j, NEG))
        kb = kh * bh
        a = jnp.where(strict, _hdot(kb, kh, ((1,), (1,))) * decay, 0.0)
        inv, pw = eye - a, a
        for _ in range(5):
            pw = _hdot(pw, pw, ((1,), (0,)))
            inv = inv + _hdot(inv, pw, ((1,), (0,)))
        eg = jnp.exp(gc)
        u = _hdot(inv, vh * bh, ((1,), (0,)))
        w = _hdot(inv, kb * eg, ((1,), (0,)))
        qk = _hdot(qh, kh, ((1,), (1,))) * decay
        v_new = u - _hdot(w, sh, ((1,), (0,)))
        outs.append(_hdot(qh * eg, sh, ((1,), (0,))) + _hdot(qk, v_new, ((1,), (0,))))
        g_last = gc[C - 1:C, :]
        states.append(sh * jnp.exp(g_last) + _hdot(kh * jnp.exp(g_last - gc), v_new, ((0,), (0,))))
    return _cat(outs), jnp.concatenate(states, axis=0)


def _make_delta_rule(name):
    W = DN_HEADS * LANES

    def fwd_call(q, k, v, g, b):
        S = q.shape[0]
        n = S // DN_CHUNK
        row = pl.BlockSpec((DN_CHUNK, W), lambda i: (i, 0))
        st = pl.BlockSpec((1, W, LANES), lambda i: (i, 0, 0))

        def body(q_r, k_r, v_r, g_r, b_r, o_r, st_r, s_s):
            @pl.when(pl.program_id(0) == 0)
            def _():
                s_s[...] = jnp.zeros_like(s_s)
            st_r[0] = s_s[...]
            o_r[...], s_s[...] = _dn_chunk(q_r[...], k_r[...], v_r[...], g_r[...], b_r[...], s_s[...])

        return _pcall(body, name=f"{name}_fwd", grid=(n,), in_specs=[row] * 5, out_specs=[row, st],
                      out_shape=[jax.ShapeDtypeStruct((S, W), F32), jax.ShapeDtypeStruct((n, W, LANES), F32)],
                      scratch_shapes=[pltpu.VMEM((W, LANES), F32)], compiler_params=_params(("arbitrary",)))(q, k, v, g, b)

    def bwd_call(q, k, v, g, b, states, do):
        S = q.shape[0]
        n = S // DN_CHUNK
        row = pl.BlockSpec((DN_CHUNK, W), lambda i: (n - 1 - i, 0))
        st = pl.BlockSpec((1, W, LANES), lambda i: (n - 1 - i, 0, 0))

        def body(q_r, k_r, v_r, g_r, b_r, st_r, do_r, dq_r, dk_r, dv_r, dg_r, db_r, ds_s):
            @pl.when(pl.program_id(0) == 0)
            def _():
                ds_s[...] = jnp.zeros_like(ds_s)
            _, vjp = jax.vjp(_dn_chunk, q_r[...], k_r[...], v_r[...], g_r[...], b_r[...], st_r[0])
            dq_r[...], dk_r[...], dv_r[...], dg_r[...], db_r[...], ds_s[...] = vjp((do_r[...], ds_s[...]))

        return _pcall(body, name=f"{name}_bwd", grid=(n,), in_specs=[row] * 5 + [st, row], out_specs=[row] * 5,
                      out_shape=[jax.ShapeDtypeStruct((S, W), F32)] * 5, scratch_shapes=[pltpu.VMEM((W, LANES), F32)],
                      compiler_params=_params(("arbitrary",)))(q, k, v, g, b, states, do)

    @jax.custom_vjp
    def op(q, k, v, g, b):
        return fwd_call(q, k, v, g, b)[0]

    def fwd(q, k, v, g, b):
        o, states = fwd_call(q, k, v, g, b)
        return o, (q, k, v, g, b, states)

    op.defvjp(fwd, lambda res, do: tuple(bwd_call(*res, do)))
    return op


def _loss_call(y, target):
    S, D = y.shape
    t = min(S, 512)
    n = S // t
    row = pl.BlockSpec((t, D), lambda i: (i, 0))

    def body(y_r, t_r, loss_r, dy_r, acc_s):
        i = pl.program_id(0)

        @pl.when(i == 0)
        def _():
            acc_s[...] = jnp.zeros_like(acc_s)
        err = y_r[...] - t_r[...]
        dy_r[...] = err * (1.0 / D)
        acc_s[...] += jnp.sum(err * err, axis=0, keepdims=True)

        @pl.when(i == n - 1)
        def _():
            loss_r[...] = jnp.broadcast_to(jnp.sum(acc_s[...], axis=1, keepdims=True) * (0.5 / D), loss_r.shape)

    return _pcall(body, name="loss_head", grid=(n,), in_specs=[row, row],
                  out_specs=[pl.BlockSpec((8, LANES), lambda i: (0, 0)), row],
                  out_shape=[jax.ShapeDtypeStruct((8, LANES), F32), jax.ShapeDtypeStruct((S, D), F32)],
                  scratch_shapes=[pltpu.VMEM((1, D), F32)], compiler_params=_params(("arbitrary",)))(y, target)


ADAM_ROWS = 512


def _adamw_call(name, parts, w, m, v):
    R = w.shape[0]
    t = min(R, ADAM_ROWS)
    row = pl.BlockSpec((t, LANES), lambda i: (i, 0))

    def body(p_r, w_r, m_r, v_r, g_r, d_r, nm_r, nv_r):
        g = p_r[0].astype(F32)
        for s in range(1, N_DEV):
            g = g + p_r[s].astype(F32)
        m_new = ADAM_B1 * m_r[...] + (1.0 - ADAM_B1) * g
        v_new = ADAM_B2 * v_r[...] + (1.0 - ADAM_B2) * (g * g)
        m_hat = m_new / (1.0 - ADAM_B1 ** ADAM_STEP)
        v_hat = v_new / (1.0 - ADAM_B2 ** ADAM_STEP)
        g_r[...] = g
        d_r[...] = -ADAM_LR * (m_hat / (jnp.sqrt(v_hat) + ADAM_EPS) + ADAM_WD * w_r[...])
        nm_r[...] = m_new
        nv_r[...] = v_new

    return _pcall(body, name=name, grid=(R // t,), in_specs=[pl.BlockSpec((N_DEV, t, LANES), lambda i: (0, i, 0)), row, row, row],
                  out_specs=[row] * 4, out_shape=[jax.ShapeDtypeStruct((R, LANES), F32)] * 4,
                  compiler_params=_params(("parallel",)))(parts, w, m, v)


def _my_place():
    x, y, c = lax.axis_index("x"), lax.axis_index("y"), lax.axis_index("c")
    return x, y, c


def _index(x, y, c):
    return 4 * x + 2 * y + c


def _all_gather(v):
    R = v.shape[0]

    def body(v_ref, out_ref, send_sems, recv_sems, local_sem):
        x, y, c = _my_place()
        me, sibling = (x, y, c), (x, y, 1 - c)
        chips = [(1 - x, y), (x, 1 - y), (1 - x, 1 - y)]

        def copy(k, block, to, src=None):
            rows = out_ref.at[_index(*block)]
            return pltpu.make_async_remote_copy(src_ref=rows if src is None else src, dst_ref=rows, send_sem=send_sems.at[k],
                                                recv_sem=recv_sems.at[k], device_id=to, device_id_type=MESH)

        mine = pltpu.make_async_copy(v_ref, out_ref.at[_index(*me)], local_sem)
        mine.start()
        first = [copy(0, me, sibling, src=v_ref)] + [copy(1 + j, me, (*chip, c), src=v_ref) for j, chip in enumerate(chips)]
        for cp in first:
            cp.start()
        passed = [copy(4 + j, (*chip, c), sibling) for j, chip in enumerate(chips)]
        for j, chip in enumerate(chips):
            copy(1 + j, (*chip, c), me).wait_recv()
            passed[j].start()
        copy(0, sibling, me).wait_recv()
        for j, chip in enumerate(chips):
            copy(4 + j, (*chip, 1 - c), me).wait_recv()
        for cp in first + passed:
            cp.wait_send()
        mine.wait()

    any_ = pl.BlockSpec(memory_space=pl.ANY)
    return _pcall(body, name="gather_weights", in_specs=[any_], out_specs=any_,
                  out_shape=jax.ShapeDtypeStruct((N_DEV, R, LANES), v.dtype),
                  scratch_shapes=[pltpu.SemaphoreType.DMA((7,)), pltpu.SemaphoreType.DMA((7,)), pltpu.SemaphoreType.DMA(())])(v)


def _all_to_all(v):
    R = v.shape[1]

    def body(v_ref, out_ref, send_sems, recv_sems, local_sem):
        x, y, c = _my_place()
        me = _index(x, y, c)
        mine = pltpu.make_async_copy(v_ref.at[me], out_ref.at[me], local_sem)
        mine.start()
        copies = []
        for k in range(1, N_DEV):
            px = 1 - x if k & 4 else x
            py = 1 - y if k & 2 else y
            pc = 1 - c if k & 1 else c
            cp = pltpu.make_async_remote_copy(src_ref=v_ref.at[_index(px, py, pc)], dst_ref=out_ref.at[me],
                                              send_sem=send_sems.at[k - 1], recv_sem=recv_sems.at[k - 1],
                                              device_id=(px, py, pc), device_id_type=MESH)
            cp.start()
            copies.append(cp)
        for cp in copies:
            cp.wait()
        mine.wait()

    any_ = pl.BlockSpec(memory_space=pl.ANY)
    return _pcall(body, name="exchange_grads", in_specs=[any_], out_specs=any_,
                  out_shape=jax.ShapeDtypeStruct((N_DEV, R, LANES), v.dtype),
                  scratch_shapes=[pltpu.SemaphoreType.DMA((7,)), pltpu.SemaphoreType.DMA((7,)), pltpu.SemaphoreType.DMA(())])(v)


def _pe_pad(a):
    h = MLA_ROPE // 2
    z = jnp.zeros(a.shape[:-1] + (h,), a.dtype)
    return jnp.concatenate([a[..., :h], z, a[..., h:], z], axis=-1)


def _seg(w_in, name):
    o, w = _SEG[name]
    return w_in[:, o:o + w]


def _layer(tag, x, tables, W):
    cos_p, sin_p, cos_h, sin_h = tables
    w_in = W["w_in"]
    pad_ab = jnp.zeros((D_MODEL, LANES - 2 * DN_HEADS), w_in.dtype)
    w_segs = (_seg(w_in, "q_lat"), _seg(w_in, "c_kv"), _pe_pad(_seg(w_in, "k_pe")), _seg(w_in, "z_a"), _seg(w_in, "dn_qkv"),
              jnp.concatenate([_seg(w_in, "dn_a"), _seg(w_in, "dn_b"), pad_ab], axis=1), _seg(w_in, "z_b"),
              _seg(w_in, "dil_qkv"), _seg(w_in, "z_c"), _seg(w_in, "gate"))
    row = lambda a: a[None, :]

    (h,) = _make_rowwise(f"{tag}_norm", _f_norm, 512)((x,), (), (row(W["norm_g"]),), ())
    q_lat, c_kv, kpe, z_a, dn_qkv, ab, z_b, dil_qkv, z_c, gl = _make_multi_linear(f"{tag}_inproj", 10)(h, w_segs)

    qn_lat, ckvn, kp = _make_rowwise(f"{tag}_mla_a", _f_mla_a, 512)(
        (q_lat, c_kv, kpe), (cos_p, sin_p),
        (row(W["mla_q_a_norm_g"]), row(W["mla_kv_a_norm_g"]), row(_pe_pad(W["mla_k_norm_g"][LANES:]))), ())
    wq = W["mla_w_q_b"].reshape(MLA_Q_RANK, MLA_HEADS, MLA_QK)
    wq = jnp.concatenate([wq[:, :, :LANES].reshape(MLA_Q_RANK, -1), _pe_pad(wq[:, :, LANES:]).reshape(MLA_Q_RANK, -1)], axis=1)
    wkv = W["mla_w_kv_b"].reshape(MLA_KV_RANK, MLA_HEADS, 2 * LANES)
    (q8,) = _make_multi_linear(f"{tag}_qb", 1)(qn_lat, (wq,))
    kn_raw, v_mla = _make_multi_linear(f"{tag}_kvb", 2)(
        ckvn, (wkv[:, :, :LANES].reshape(MLA_KV_RANK, -1), wkv[:, :, LANES:].reshape(MLA_KV_RANK, -1)))
    qn, qp, kn = _make_rowwise(f"{tag}_mla_b", _f_mla_b, 512)(
        (q8, kn_raw), (cos_p, sin_p),
        (row(W["mla_q_norm_g"][:LANES]), row(_pe_pad(W["mla_q_norm_g"][LANES:])), row(W["mla_k_norm_g"][:LANES])), ())
    y_a = _make_mla_attn(f"{tag}_mla")(qn, qp, kn, kp, v_mla)

    mixed = _make_conv(f"{tag}_conv")(dn_qkv, W["dn_conv_w"])
    lane_head = jnp.arange(DN_HEADS * LANES) // LANES
    e_a = (jnp.arange(LANES)[:, None] == lane_head[None, :]).astype(F32)
    e_b = (jnp.arange(LANES)[:, None] == lane_head[None, :] + DN_HEADS).astype(F32)
    q_dn, k_dn, v_dn, g_dn, b_dn = _make_rowwise(f"{tag}_dn_pre", _f_dn_pre, 512)(
        (mixed, ab), (), (row(jnp.repeat(W["dn_a_log"], LANES)), row(jnp.repeat(W["dn_dt_bias"], LANES))), (e_a, e_b))
    o_dn = _make_delta_rule(f"{tag}_dn")(q_dn, k_dn, v_dn, g_dn, b_dn)

    q_dil, k_dil, v_dil = _make_rowwise(f"{tag}_dil_pre", _f_dil_pre, 256)(
        (dil_qkv,), (cos_h, sin_h), (row(W["dil_q_norm_g"]), row(W["dil_k_norm_g"])), ())
    o_dil, lse_dil = _make_dil_attn(f"{tag}_dil")(_to_residues(q_dil), _to_residues(k_dil), _to_residues(v_dil))
    o_dil, lse_dil = _from_residues(o_dil), _from_residues(lse_dil)

    ya, yb, yc = _make_rowwise(f"{tag}_merge_a", _f_merge_a, 256)(
        (y_a, z_a, o_dn, z_b, o_dil, lse_dil, z_c), (), (row(W["dn_out_norm_g"]),), ())
    (b0,) = _make_multi_linear(f"{tag}_br0", 1)(ya, (W["w_branch"][0],))
    (b1,) = _make_multi_linear(f"{tag}_br1", 1)(yb, (W["w_branch"][1],))
    (b2,) = _make_multi_linear(f"{tag}_br2", 1)(yc, (W["w_branch"][2],))
    (mix,) = _make_rowwise(f"{tag}_merge_b", _f_merge_b, 256)((b0, b1, b2, gl), (), (), ())
    return _make_resid_linear(f"{tag}_out")(x, mix, W["w_out"])


PACK_ALIGN = 16 * LANES

BIG = (("w_in", (D_MODEL, IN_WIDTH // N_DEV)), ("mla_w_q_b", (MLA_Q_RANK, MLA_HEADS * MLA_QK // N_DEV)),
       ("mla_w_kv_b", (MLA_KV_RANK, MLA_HEADS * 2 * LANES // N_DEV)), ("w_branch", (3, BRANCH_W, D_MODEL // N_DEV)),
       ("w_out", (D_MODEL // N_DEV, D_MODEL)))
CONV = ("dn_conv_w", (DN_CONV, 3 * DN_HEADS * LANES // N_DEV))
SMALL = (("norm_g", D_MODEL), ("mla_q_a_norm_g", MLA_Q_RANK), ("mla_kv_a_norm_g", MLA_KV_RANK), ("mla_q_norm_g", MLA_QK),
         ("mla_k_norm_g", MLA_QK), ("dn_a_log", DN_HEADS), ("dn_dt_bias", DN_HEADS), ("dn_out_norm_g", LANES),
         ("dil_q_norm_g", LANES), ("dil_k_norm_g", LANES))
WEIGHTS = ("norm_g", "w_in", "mla_q_a_norm_g", "mla_w_q_b", "mla_kv_a_norm_g", "mla_w_kv_b", "mla_q_norm_g", "mla_k_norm_g",
           "dn_conv_w", "dn_a_log", "dn_dt_bias", "dn_out_norm_g", "dil_q_norm_g", "dil_k_norm_g", "w_branch", "w_out")


def _round_up(n, m):
    return -(-n // m) * m


def _pack(pieces, total_mult):
    out, n = [], 0
    for p in pieces:
        pad = _round_up(p.shape[-1], PACK_ALIGN) - p.shape[-1]
        out.append(jnp.pad(p, [(0, 0)] * (p.ndim - 1) + [(0, pad)]))
        n += p.shape[-1] + pad
    tail = _round_up(n, total_mult) - n
    if tail:
        out.append(jnp.zeros(pieces[0].shape[:-1] + (tail,), pieces[0].dtype))
    return jnp.concatenate(out, axis=-1)


def _unpack(flat, sizes):
    out, off = [], 0
    for n in sizes:
        out.append(flat[..., off:off + n])
        off += _round_up(n, PACK_ALIGN)
    return out


def _as_bf16_bits(a):
    return lax.bitcast_convert_type(a, BF16).reshape(a.shape[:-1] + (2 * a.shape[-1],))


def _from_bf16_bits(a):
    return lax.bitcast_convert_type(a.reshape(a.shape[:-1] + (a.shape[-1] // 2, 2)), F32)


def _size(shape):
    return math.prod(shape)


_BIG_SIZES = [_size(s) for _ in range(DEPTH) for _, s in BIG]
_F32_SIZES = [n for _ in range(DEPTH) for n in (_size(CONV[1]), *[k for _, k in SMALL])]


def _whole_weights(g, small):
    W = dict(small)
    W["w_in"] = g["w_in"].transpose(1, 0, 2).reshape(D_MODEL, IN_WIDTH)
    W["mla_w_q_b"] = g["mla_w_q_b"].transpose(1, 0, 2).reshape(MLA_Q_RANK, -1)
    W["mla_w_kv_b"] = g["mla_w_kv_b"].transpose(1, 0, 2).reshape(MLA_KV_RANK, -1)
    W["w_branch"] = g["w_branch"].transpose(1, 2, 0, 3).reshape(3, BRANCH_W, D_MODEL)
    W["w_out"] = g["w_out"].reshape(D_MODEL, D_MODEL)
    W["dn_conv_w"] = g["dn_conv_w"].transpose(1, 0, 2).reshape(DN_CONV, -1)
    return W


def kernel(x, positions, norm_g, w_in, mla_q_a_norm_g, mla_w_q_b, mla_kv_a_norm_g, mla_w_kv_b, mla_q_norm_g, mla_k_norm_g, dn_conv_w, dn_a_log, dn_dt_bias, dn_out_norm_g, dil_q_norm_g, dil_k_norm_g, w_branch, w_out, loss_target, m_norm_g, m_w_in, m_mla_q_a_norm_g, m_mla_w_q_b, m_mla_kv_a_norm_g, m_mla_w_kv_b, m_mla_q_norm_g, m_mla_k_norm_g, m_dn_conv_w, m_dn_a_log, m_dn_dt_bias, m_dn_out_norm_g, m_dil_q_norm_g, m_dil_k_norm_g, m_w_branch, m_w_out, v_norm_g, v_w_in, v_mla_q_a_norm_g, v_mla_w_q_b, v_mla_kv_a_norm_g, v_mla_w_kv_b, v_mla_q_norm_g, v_mla_k_norm_g, v_dn_conv_w, v_dn_a_log, v_dn_dt_bias, v_dn_out_norm_g, v_dil_q_norm_g, v_dil_k_norm_g, v_w_branch, v_w_out):
    w = dict(norm_g=norm_g, w_in=w_in, mla_q_a_norm_g=mla_q_a_norm_g, mla_w_q_b=mla_w_q_b, mla_kv_a_norm_g=mla_kv_a_norm_g,
             mla_w_kv_b=mla_w_kv_b, mla_q_norm_g=mla_q_norm_g, mla_k_norm_g=mla_k_norm_g, dn_conv_w=dn_conv_w, dn_a_log=dn_a_log,
             dn_dt_bias=dn_dt_bias, dn_out_norm_g=dn_out_norm_g, dil_q_norm_g=dil_q_norm_g, dil_k_norm_g=dil_k_norm_g,
             w_branch=w_branch, w_out=w_out)
    m = dict(norm_g=m_norm_g, w_in=m_w_in, mla_q_a_norm_g=m_mla_q_a_norm_g, mla_w_q_b=m_mla_w_q_b, mla_kv_a_norm_g=m_mla_kv_a_norm_g,
             mla_w_kv_b=m_mla_w_kv_b, mla_q_norm_g=m_mla_q_norm_g, mla_k_norm_g=m_mla_k_norm_g, dn_conv_w=m_dn_conv_w,
             dn_a_log=m_dn_a_log, dn_dt_bias=m_dn_dt_bias, dn_out_norm_g=m_dn_out_norm_g, dil_q_norm_g=m_dil_q_norm_g,
             dil_k_norm_g=m_dil_k_norm_g, w_branch=m_w_branch, w_out=m_w_out)
    v = dict(norm_g=v_norm_g, w_in=v_w_in, mla_q_a_norm_g=v_mla_q_a_norm_g, mla_w_q_b=v_mla_w_q_b, mla_kv_a_norm_g=v_mla_kv_a_norm_g,
             mla_w_kv_b=v_mla_w_kv_b, mla_q_norm_g=v_mla_q_norm_g, mla_k_norm_g=v_mla_k_norm_g, dn_conv_w=v_dn_conv_w,
             dn_a_log=v_dn_a_log, dn_dt_bias=v_dn_dt_bias, dn_out_norm_g=v_dn_out_norm_g, dil_q_norm_g=v_dil_q_norm_g,
             dil_k_norm_g=v_dil_k_norm_g, w_branch=v_w_branch, w_out=v_w_out)
    x2, target = x[0], loss_target[0]
    pos = positions[0][:, None]

    def shard_pieces(t, cast):
        big = [cast(t[n][l].reshape(-1)) for l in range(DEPTH) for n, _ in BIG]
        f32 = [t[n][l].reshape(-1) for l in range(DEPTH) for n in (CONV[0], *[k for k, _ in SMALL])]
        return big, f32

    big_w, f32_w = shard_pieces(w, lambda a: a.astype(BF16))
    conv_bits = [_as_bf16_bits(w[CONV[0]][l].reshape(-1)) for l in range(DEPTH)]
    sent = _pack(big_w + conv_bits, ADAM_ROWS * LANES).reshape(-1, LANES)
    got = _all_gather(sent).reshape(N_DEV, -1)
    pieces = _unpack(got, _BIG_SIZES + [2 * _size(CONV[1])] * DEPTH)
    gathered, small = [], []
    for l in range(DEPTH):
        g = {n: pieces[l * len(BIG) + i].reshape((N_DEV,) + s) for i, (n, s) in enumerate(BIG)}
        g[CONV[0]] = _from_bf16_bits(pieces[DEPTH * len(BIG) + l]).reshape((N_DEV,) + CONV[1])
        gathered.append(g)
        small.append({n: w[n][l] for n, _ in SMALL})

    tables = _rope_tables(pos, _rope_consts())

    def forward(gathered, small, x2):
        y = x2
        for l in range(DEPTH):
            y = _layer(f"l{l}", y, tables, _whole_weights(gathered[l], small[l]))
        return y

    y, vjp = jax.vjp(forward, gathered, small, x2)
    loss_splat, dy = _loss_call(y, target)
    g_gathered, g_small, g_x = vjp(dy)
    loss = lax.psum(loss_splat[0, 0], ("x", "y", "c"))

    g_big = [g_gathered[l][n].reshape(N_DEV, -1) for l in range(DEPTH) for n, _ in BIG]
    g_f32 = []
    for l in range(DEPTH):
        g_f32.append(g_gathered[l][CONV[0]].reshape(N_DEV, -1))
        g_f32 += [jnp.broadcast_to(g_small[l][n][None], (N_DEV, k)) for n, k in SMALL]
    big_part = _pack(g_big, ADAM_ROWS * LANES)
    f32_part = _pack(g_f32, 8 * LANES)
    rows_big, rows_f32 = big_part.shape[1] // LANES, f32_part.shape[1] // LANES
    parts = _all_to_all(jnp.concatenate([big_part, _as_bf16_bits(f32_part)], axis=1).reshape(N_DEV, -1, LANES))
    parts_f32 = _from_bf16_bits(parts[:, rows_big:].reshape(N_DEV, -1)).reshape(N_DEV, rows_f32, LANES)

    def state(t):
        big, f32 = shard_pieces(t, lambda a: a)
        return _pack(big, ADAM_ROWS * LANES).reshape(-1, LANES), _pack(f32, 8 * LANES).reshape(-1, LANES)

    (w_b, w_s), (m_b, m_s), (v_b, v_s) = state(w), state(m), state(v)
    out_big = _adamw_call("adamw_matrices", parts, w_b, m_b, v_b)
    out_f32 = _adamw_call("adamw_vectors", parts_f32, w_s, m_s, v_s)

    results = []
    for ob, os_ in zip(out_big, out_f32):
        pb = _unpack(ob.reshape(-1), _BIG_SIZES)
        ps = _unpack(os_.reshape(-1), _F32_SIZES)
        t = {}
        for i, (n, s) in enumerate(BIG):
            t[n] = jnp.stack([pb[l * len(BIG) + i].reshape(s) for l in range(DEPTH)])
        per = 1 + len(SMALL)
        t[CONV[0]] = jnp.stack([ps[l * per].reshape(CONV[1]) for l in range(DEPTH)])
        for i, (n, k) in enumerate(SMALL):
            t[n] = jnp.stack([ps[l * per + 1 + i] for l in range(DEPTH)])
        results.append(t)
    grads, deltas, new_m, new_v = results
    return (loss, g_x[None], *[grads[n] for n in WEIGHTS], *[deltas[n] for n in WEIGHTS],
            *[new_m[n] for n in WEIGHTS], *[new_v[n] for n in WEIGHTS])
```

```python
import functools
import math

import jax
import jax.numpy as jnp
from jax import lax
from jax.experimental import pallas as pl
from jax.experimental.pallas import tpu as pltpu

F32 = jnp.float32
BF16 = jnp.bfloat16
HI = lax.Precision.HIGHEST
MESH = pl.DeviceIdType.MESH

N_DEV = 8
D_MODEL = 1024
DEPTH = 2
RMS_EPS = 1e-6
ROPE_THETA = 10000.0
LANES = 128
MLA_HEADS = 4
MLA_ROPE = 64
MLA_QK = 192
MLA_Q_RANK = 384
MLA_KV_RANK = 256
DN_HEADS = 4
DN_CHUNK = 64
DN_CONV = 4
DIL_HEADS = 12
DIL_GROUP_HEADS = 4
DIL_DILATIONS = (1, 4, 16)
DIL_BLOCK = 128
BRANCH_W = 512
IN_WIDTH = 11464
NEG = -1e30
VMEM_LIMIT = 56 * 1024 * 1024

ADAM_LR, ADAM_B1, ADAM_B2, ADAM_EPS, ADAM_WD, ADAM_STEP = 0.001, 0.9, 0.999, 1e-08, 0.01, 10

_SEG = {}
_off = 0
for _n, _w in (("q_lat", 384), ("c_kv", 256), ("k_pe", 64), ("z_a", 512), ("dn_qkv", 1536), ("dn_a", 4), ("dn_b", 4),
               ("z_b", 512), ("dil_qkv", 4608), ("z_c", 512), ("gate", 3072)):
    _SEG[_n] = (_off, _w)
    _off += _w
assert _off == IN_WIDTH


def _pcall(body, **kw):
    return pl.pallas_call(body, **kw)


def _params(sem=None):
    return pltpu.CompilerParams(dimension_semantics=sem, vmem_limit_bytes=VMEM_LIMIT)


def _tile(n, target, mult):
    t = (min(n, target) // mult) * mult
    while t >= mult:
        if n % t == 0:
            return t
        t -= mult
    return n


def _mm(name, a, b, mode, out_dtype=F32, acc=None, tm=1024, tn=512, tk=1024):
    if mode == "nn":
        (M, K), (_, N) = a.shape, b.shape
    elif mode == "nt":
        (M, K), (N, _) = a.shape, b.shape
    else:
        (K, M), (_, N) = a.shape, b.shape
    tm, tn, tk = _tile(M, tm, LANES), _tile(N, tn, LANES), _tile(K, tk, LANES)
    nk = K // tk
    dims = {"nn": (((1,), (0,)), ((), ())), "nt": (((1,), (1,)), ((), ())), "tn": (((0,), (0,)), ((), ()))}[mode]
    a_spec = pl.BlockSpec((tk, tm), lambda i, j, k: (k, i)) if mode == "tn" else pl.BlockSpec((tm, tk), lambda i, j, k: (i, k))
    b_spec = pl.BlockSpec((tn, tk), lambda i, j, k: (j, k)) if mode == "nt" else pl.BlockSpec((tk, tn), lambda i, j, k: (k, j))
    o_spec = pl.BlockSpec((tm, tn), lambda i, j, k: (i, j))
    has_acc = acc is not None

    def body(*refs):
        if has_acc:
            a_ref, b_ref, c_ref, o_ref, acc_ref = refs
        else:
            a_ref, b_ref, o_ref, acc_ref = refs
        k = pl.program_id(2)

        @pl.when(k == 0)
        def _():
            acc_ref[...] = c_ref[...].astype(F32) if has_acc else jnp.zeros_like(acc_ref)

        acc_ref[...] += lax.dot_general(a_ref[...].astype(BF16), b_ref[...].astype(BF16), dims, preferred_element_type=F32)

        @pl.when(k == nk - 1)
        def _():
            o_ref[...] = acc_ref[...].astype(out_dtype)

    ins = [a, b] + ([acc] if has_acc else [])
    in_specs = [a_spec, b_spec] + ([o_spec] if has_acc else [])
    return _pcall(body, name=name, grid=(M // tm, N // tn, nk), in_specs=in_specs, out_specs=o_spec,
                  out_shape=jax.ShapeDtypeStruct((M, N), out_dtype), scratch_shapes=[pltpu.VMEM((tm, tn), F32)],
                  compiler_params=_params(("parallel", "parallel", "arbitrary")))(*ins)


def _make_multi_linear(name, n):
    @jax.custom_vjp
    def op(h, ws):
        return tuple(_mm(f"{name}_fwd{i}", h, w, "nn") for i, w in enumerate(ws))

    def fwd(h, ws):
        return op(h, ws), (h, ws)

    def bwd(res, douts):
        h, ws = res
        dh = None
        for i, (w, d) in enumerate(zip(ws, douts)):
            dh = _mm(f"{name}_dh{i}", d, w, "nt", acc=dh)
        dws = tuple(_mm(f"{name}_dw{i}", h, d, "tn", out_dtype=w.dtype) for i, (w, d) in enumerate(zip(ws, douts)))
        return dh.astype(h.dtype), dws

    op.defvjp(fwd, bwd)
    return op


def _make_resid_linear(name):
    @jax.custom_vjp
    def op(x, a, w):
        return _mm(f"{name}_fwd", a, w, "nn", acc=x)

    def fwd(x, a, w):
        return op(x, a, w), (a, w)

    def bwd(res, dy):
        a, w = res
        return dy, _mm(f"{name}_da", dy, w, "nt"), _mm(f"{name}_dw", a, dy, "tn", out_dtype=w.dtype)

    op.defvjp(fwd, bwd)
    return op


def _make_rowwise(name, f, tile):
    def specs(rows, aux, params, consts, t):
        row = [pl.BlockSpec((t, a.shape[1]), lambda i: (i, 0)) for a in (*rows, *aux)]
        full = [pl.BlockSpec(p.shape, lambda i: (0, 0)) for p in (*params, *consts)]
        return row, full

    def fwd_call(rows, aux, params, consts):
        S = rows[0].shape[0]
        t = min(tile, S)
        n_in = len(rows) + len(aux) + len(params) + len(consts)
        shp = lambda a: jax.ShapeDtypeStruct((t, a.shape[1]), a.dtype)
        outs = jax.eval_shape(f, *[shp(a) for a in (*rows, *aux)], *params, *consts)
        row_specs, full_specs = specs(rows, aux, params, consts, t)

        def body(*refs):
            res = f(*[r[...] for r in refs[:n_in]])
            for o_ref, o in zip(refs[n_in:], res):
                o_ref[...] = o

        return _pcall(body, name=f"{name}_fwd", grid=(S // t,), in_specs=row_specs + full_specs,
                      out_specs=[pl.BlockSpec((t, o.shape[1]), lambda i: (i, 0)) for o in outs],
                      out_shape=[jax.ShapeDtypeStruct((S, o.shape[1]), o.dtype) for o in outs],
                      compiler_params=_params(("parallel",)))(*rows, *aux, *params, *consts)

    def bwd_call(rows, aux, params, consts, douts):
        S = rows[0].shape[0]
        t = min(tile, S)
        nr, na, npar, nc, nd = len(rows), len(aux), len(params), len(consts), len(douts)
        row_specs, full_specs = specs(rows, aux, params, consts, t)

        def body(*refs):
            vals = [r[...] for r in refs[:nr + na + npar + nc]]
            rv, av = vals[:nr], vals[nr:nr + na]
            pv, cv = vals[nr + na:nr + na + npar], vals[nr + na + npar:]
            dv = tuple(r[...] for r in refs[nr + na + npar + nc:nr + na + npar + nc + nd])
            out_refs = refs[nr + na + npar + nc + nd:]
            _, vjp = jax.vjp(lambda *rp: f(*rp[:nr], *av, *rp[nr:], *cv), *rv, *pv)
            grads = vjp(dv)
            for o_ref, g in zip(out_refs[:nr], grads[:nr]):
                o_ref[...] = g
            first = pl.program_id(0) == 0
            for o_ref, g in zip(out_refs[nr:], grads[nr:]):
                @pl.when(first)
                def _(o_ref=o_ref):
                    o_ref[...] = jnp.zeros_like(o_ref)
                o_ref[...] += g

        res = _pcall(body, name=f"{name}_bwd", grid=(S // t,),
                     in_specs=row_specs + full_specs + [pl.BlockSpec((t, d.shape[1]), lambda i: (i, 0)) for d in douts],
                     out_specs=[pl.BlockSpec((t, a.shape[1]), lambda i: (i, 0)) for a in rows]
                     + [pl.BlockSpec(p.shape, lambda i: (0, 0)) for p in params],
                     out_shape=[jax.ShapeDtypeStruct(a.shape, a.dtype) for a in (*rows, *params)],
                     compiler_params=_params(("arbitrary",)))(*rows, *aux, *params, *consts, *douts)
        return tuple(res[:nr]), tuple(res[nr:])

    @jax.custom_vjp
    def op(rows, aux, params, consts):
        return tuple(fwd_call(rows, aux, params, consts))

    def fwd(rows, aux, params, consts):
        return op(rows, aux, params, consts), (rows, aux, params, consts)

    def bwd(res, douts):
        rows, aux, params, consts = res
        drows, dparams = bwd_call(rows, aux, params, consts, tuple(douts))
        zeros = lambda xs: tuple(jnp.zeros_like(a) for a in xs)
        return drows, zeros(aux), dparams, zeros(consts)

    op.defvjp(fwd, bwd)
    return op


@jax.custom_vjp
def _swap_halves(x):
    return pltpu.roll(x, LANES // 2, 1)


_swap_halves.defvjp(lambda x: (_swap_halves(x), None), lambda _, g: (_swap_halves(g),))


def _rope(x, cos_t, sin_t):
    return x * cos_t + _swap_halves(x) * sin_t


def _rms(x, g, n=None):
    n = x.shape[-1] if n is None else n
    return x * lax.rsqrt(jnp.sum(x * x, axis=-1, keepdims=True) * (1.0 / n) + RMS_EPS) * g


def _heads(x):
    return [x[:, i * LANES:(i + 1) * LANES] for i in range(x.shape[1] // LANES)]


def _cat(xs):
    return jnp.concatenate(xs, axis=1)


def _silu(x):
    return x * jax.nn.sigmoid(x)


def _f_norm(x, g):
    return (_rms(x, g),)


def _f_mla_a(q_lat, c_kv, kpe, cos_p, sin_p, qa_g, kva_g, kpe_g):
    kp = _rope(_rms(kpe, kpe_g, MLA_ROPE), cos_p, sin_p)
    return _rms(q_lat, qa_g), _rms(c_kv, kva_g), _cat([kp] * MLA_HEADS)


def _f_mla_b(q8, kn_raw, cos_p, sin_p, qn_g, qp_g, kn_g):
    hs = _heads(q8)
    qn = _cat([_rms(h, qn_g) for h in hs[:MLA_HEADS]])
    qp = _cat([_rope(_rms(h, qp_g, MLA_ROPE), cos_p, sin_p) for h in hs[MLA_HEADS:]])
    kn = _cat([_rms(h, kn_g) for h in _heads(kn_raw)])
    return qn, qp, kn


def _softplus(x):
    return jnp.maximum(x, 0.0) + jnp.log(1.0 + jnp.exp(-jnp.abs(x)))


def _l2n(x):
    return x * lax.rsqrt(jnp.sum(x * x, axis=-1, keepdims=True) + 1e-6)


def _f_dn_pre(mixed, ab, alog_f, dtb_f, e_a, e_b):
    hs = _heads(mixed)
    q = _cat([_l2n(h) * (LANES ** -0.5) for h in hs[:DN_HEADS]])
    k = _cat([_l2n(h) for h in hs[DN_HEADS:2 * DN_HEADS]])
    v = _cat(hs[2 * DN_HEADS:])
    a_f = jnp.dot(ab, e_a, precision=HI, preferred_element_type=F32)
    b_f = jnp.dot(ab, e_b, precision=HI, preferred_element_type=F32)
    g = -jnp.exp(alog_f) * _softplus(a_f + dtb_f)
    return q, k, v, g, jax.nn.sigmoid(b_f)


def _f_dil_pre(qkv, cos_h, sin_h, q_g, k_g):
    hs = _heads(qkv)
    q = _cat([_rope(_rms(h, q_g), cos_h, sin_h) for h in hs[:DIL_HEADS]])
    k = _cat([_rope(_rms(h, k_g), cos_h, sin_h) for h in hs[DIL_HEADS:2 * DIL_HEADS]])
    return q, k, _cat(hs[2 * DIL_HEADS:])


def _f_merge_a(y_a, z_a, o_dn, z_b, o_dil, lse, z_c, out_g):
    y_b = _cat([_rms(h, out_g) for h in _heads(o_dn)])
    os_, ls = _heads(o_dil), _heads(lse)
    y_c = []
    for j in range(DIL_GROUP_HEADS):
        l3 = [ls[g * DIL_GROUP_HEADS + j] for g in range(3)]
        m = jnp.maximum(jnp.maximum(l3[0], l3[1]), l3[2])
        e3 = [jnp.exp(l - m) for l in l3]
        den = e3[0] + e3[1] + e3[2]
        y_c.append(sum(e3[g] * os_[g * DIL_GROUP_HEADS + j] for g in range(3)) / den)
    return y_a * _silu(z_a), y_b * _silu(z_b), _cat(y_c) * _silu(z_c)


def _f_merge_b(b0, b1, b2, gl):
    gs = [jax.nn.sigmoid(gl[:, i * D_MODEL:(i + 1) * D_MODEL]) for i in range(3)]
    return (gs[0] * b0 + gs[1] * b1 + gs[2] * b2,)


def _rope_tables(pos, inv_sign):
    S = pos.shape[0]
    t = min(S, 1024)

    def body(p_ref, c_ref, cp, sp, ch, sh):
        p = p_ref[...].astype(F32)
        c = c_ref[...]
        ang_p, ang_h = p * c[0:1], p * c[2:3]
        cp[...] = jnp.cos(ang_p) * jnp.abs(c[1:2])
        sp[...] = jnp.sin(ang_p) * c[1:2]
        ch[...] = jnp.cos(ang_h)
        sh[...] = jnp.sin(ang_h) * c[3:4]

    row = pl.BlockSpec((t, LANES), lambda i: (i, 0))
    return _pcall(body, name="rope_tables", grid=(S // t,),
                  in_specs=[pl.BlockSpec((t, 1), lambda i: (i, 0)), pl.BlockSpec((4, LANES), lambda i: (0, 0))],
                  out_specs=[row] * 4, out_shape=[jax.ShapeDtypeStruct((S, LANES), F32)] * 4,
                  compiler_params=_params(("parallel",)))(pos, inv_sign)


def _rope_consts():
    half_p, half_h = MLA_ROPE // 2, LANES // 2
    inv_p = 1.0 / (ROPE_THETA ** (jnp.arange(0, MLA_ROPE, 2, dtype=F32) / MLA_ROPE))
    inv_h = 1.0 / (ROPE_THETA ** (jnp.arange(0, LANES, 2, dtype=F32) / LANES))
    z = jnp.zeros((half_p,), F32)
    o = jnp.ones((half_p,), F32)
    return jnp.stack([jnp.concatenate([inv_p, z, inv_p, z]), jnp.concatenate([-o, z, o, z]),
                      jnp.concatenate([inv_h, inv_h]), jnp.concatenate([-jnp.ones((half_h,), F32), jnp.ones((half_h,), F32)])])


def _shift_rows(x, s, up):
    n = x.shape[0]
    r = lax.broadcasted_iota(jnp.int32, x.shape, 0)
    if up:
        return jnp.where(r < n - s, pltpu.roll(x, n - s, 0), 0.0)
    return jnp.where(r >= s, pltpu.roll(x, s, 0), 0.0)


def _make_shift(s):
    @jax.custom_vjp
    def sh(x):
        return _shift_rows(x, s, False)

    sh.defvjp(lambda x: (sh(x), None), lambda _, g: (_shift_rows(g, s, True),))
    return sh


def _f_conv(x, w):
    y = x * w[DN_CONV - 1:DN_CONV]
    for j in range(DN_CONV - 1):
        y = y + _make_shift(DN_CONV - 1 - j)(x) * w[j:j + 1]
    return _silu(y)


def _make_conv(name):
    def call(x, w, dy=None):
        S, C = x.shape
        col = pl.BlockSpec((S, LANES), lambda i: (0, i))
        wsp = pl.BlockSpec((DN_CONV, LANES), lambda i: (0, i))
        if dy is None:
            def body(x_ref, w_ref, o_ref):
                o_ref[...] = _f_conv(x_ref[...], w_ref[...])
            return _pcall(body, name=f"{name}_fwd", grid=(C // LANES,), in_specs=[col, wsp], out_specs=col,
                          out_shape=jax.ShapeDtypeStruct(x.shape, F32), compiler_params=_params(("parallel",)))(x, w)

        def body(x_ref, w_ref, dy_ref, dx_ref, dw_ref):
            _, vjp = jax.vjp(_f_conv, x_ref[...], w_ref[...])
            dx_ref[...], dw_ref[...] = vjp(dy_ref[...])
        return _pcall(body, name=f"{name}_bwd", grid=(C // LANES,), in_specs=[col, wsp, col], out_specs=[col, wsp],
                      out_shape=[jax.ShapeDtypeStruct(x.shape, F32), jax.ShapeDtypeStruct(w.shape, F32)],
                      compiler_params=_params(("parallel",)))(x, w, dy)

    @jax.custom_vjp
    def op(x, w):
        return call(x, w)

    op.defvjp(lambda x, w: (op(x, w), (x, w)), lambda res, dy: tuple(call(*res, dy)))
    return op


def _dot_nt(a, b):
    return lax.dot_general(a.astype(BF16), b.astype(BF16), (((1,), (1,)), ((), ())), preferred_element_type=F32)


def _dot_nn(a, b):
    return jnp.dot(a.astype(BF16), b.astype(BF16), preferred_element_type=F32)


def _dot_tn(a, b):
    return lax.dot_general(a.astype(BF16), b.astype(BF16), (((0,), (0,)), ((), ())), preferred_element_type=F32)


def _causal_mask(qi, ki, t):
    r = lax.broadcasted_iota(jnp.int32, (t, t), 0) + qi * t
    c = lax.broadcasted_iota(jnp.int32, (t, t), 1) + ki * t
    return c <= r


def _make_mla_attn(name):
    scale = MLA_QK ** -0.5

    def fwd_call(qn, qp, kn, kp, v):
        S = qn.shape[0]
        t = min(S, 512)
        n = S // t
        qs = pl.BlockSpec((t, LANES), lambda h, i, j: (i, h))
        ks = pl.BlockSpec((t, LANES), lambda h, i, j: (jnp.minimum(j, i), h))

        def body(qn_r, qp_r, kn_r, kp_r, v_r, o_r, lse_r, m_s, l_s, acc_s):
            qi, ki = pl.program_id(1), pl.program_id(2)

            @pl.when(ki == 0)
            def _():
                m_s[...] = jnp.full_like(m_s, NEG)
                l_s[...] = jnp.zeros_like(l_s)
                acc_s[...] = jnp.zeros_like(acc_s)

            @pl.when(ki <= qi)
            def _():
                s = (_dot_nt(qn_r[...], kn_r[...]) + _dot_nt(qp_r[...], kp_r[...])) * scale
                s = jnp.where(_causal_mask(qi, ki, t), s, NEG)
                m_old = m_s[...]
                m_new = jnp.maximum(m_old, jnp.max(s, axis=-1, keepdims=True))
                p = jnp.exp(s - m_new[:, :1])
                alpha = jnp.exp(m_old - m_new)
                l_s[...] = alpha * l_s[...] + jnp.sum(p, axis=-1, keepdims=True)
                acc_s[...] = alpha * acc_s[...] + _dot_nn(p, v_r[...])
                m_s[...] = m_new

            @pl.when(ki == n - 1)
            def _():
                o_r[...] = acc_s[...] / l_s[...]
                lse_r[...] = m_s[...] + jnp.log(l_s[...])

        return _pcall(body, name=f"{name}_fwd", grid=(MLA_HEADS, n, n), in_specs=[qs, qs, ks, ks, ks], out_specs=[qs, qs],
                      out_shape=[jax.ShapeDtypeStruct((S, MLA_HEADS * LANES), F32)] * 2,
                      scratch_shapes=[pltpu.VMEM((t, LANES), F32)] * 3,
                      compiler_params=_params(("parallel", "parallel", "arbitrary")))(qn, qp, kn, kp, v)

    def dq_call(qn, qp, kn, kp, v, o, lse, do):
        S = qn.shape[0]
        t = min(S, 512)
        n = S // t
        qs = pl.BlockSpec((t, LANES), lambda h, i, j: (i, h))
        ks = pl.BlockSpec((t, LANES), lambda h, i, j: (jnp.minimum(j, i), h))

        def body(qn_r, qp_r, kn_r, kp_r, v_r, o_r, lse_r, do_r, dqn_r, dqp_r, dl_r, dqn_s, dqp_s):
            qi, ki = pl.program_id(1), pl.program_id(2)

            @pl.when(ki == 0)
            def _():
                dqn_s[...] = jnp.zeros_like(dqn_s)
                dqp_s[...] = jnp.zeros_like(dqp_s)
                dl_r[...] = jnp.broadcast_to(jnp.sum(do_r[...] * o_r[...], axis=-1, keepdims=True), dl_r.shape)

            @pl.when(ki <= qi)
            def _():
                s = (_dot_nt(qn_r[...], kn_r[...]) + _dot_nt(qp_r[...], kp_r[...])) * scale
                s = jnp.where(_causal_mask(qi, ki, t), s, NEG)
                p = jnp.exp(s - lse_r[...][:, :1])
                ds = p * (_dot_nt(do_r[...], v_r[...]) - dl_r[...][:, :1]) * scale
                dqn_s[...] += _dot_nn(ds, kn_r[...])
                dqp_s[...] += _dot_nn(ds, kp_r[...])

            @pl.when(ki == n - 1)
            def _():
                dqn_r[...] = dqn_s[...]
                dqp_r[...] = dqp_s[...]

        return _pcall(body, name=f"{name}_dq", grid=(MLA_HEADS, n, n), in_specs=[qs, qs, ks, ks, ks, qs, qs, qs],
                      out_specs=[qs, qs, qs], out_shape=[jax.ShapeDtypeStruct((S, MLA_HEADS * LANES), F32)] * 3,
                      scratch_shapes=[pltpu.VMEM((t, LANES), F32)] * 2,
                      compiler_params=_params(("parallel", "parallel", "arbitrary")))(qn, qp, kn, kp, v, o, lse, do)

    def dkv_call(qn, qp, kn, kp, v, lse, do, dl):
        S = qn.shape[0]
        t = min(S, 512)
        n = S // t
        ks = pl.BlockSpec((t, LANES), lambda h, j, i: (j, h))
        qs = pl.BlockSpec((t, LANES), lambda h, j, i: (jnp.maximum(i, j), h))

        def body(qn_r, qp_r, kn_r, kp_r, v_r, lse_r, do_r, dl_r, dkn_r, dkp_r, dv_r, dkn_s, dkp_s, dv_s):
            ki, qi = pl.program_id(1), pl.program_id(2)

            @pl.when(qi == 0)
            def _():
                dkn_s[...] = jnp.zeros_like(dkn_s)
                dkp_s[...] = jnp.zeros_like(dkp_s)
                dv_s[...] = jnp.zeros_like(dv_s)

            @pl.when(qi >= ki)
            def _():
                s = (_dot_nt(qn_r[...], kn_r[...]) + _dot_nt(qp_r[...], kp_r[...])) * scale
                s = jnp.where(_causal_mask(qi, ki, t), s, NEG)
                p = jnp.exp(s - lse_r[...][:, :1])
                ds = p * (_dot_nt(do_r[...], v_r[...]) - dl_r[...][:, :1]) * scale
                dv_s[...] += _dot_tn(p, do_r[...])
                dkn_s[...] += _dot_tn(ds, qn_r[...])
                dkp_s[...] += _dot_tn(ds, qp_r[...])

            @pl.when(qi == n - 1)
            def _():
                dkn_r[...] = dkn_s[...]
                dkp_r[...] = dkp_s[...]
                dv_r[...] = dv_s[...]

        return _pcall(body, name=f"{name}_dkv", grid=(MLA_HEADS, n, n), in_specs=[qs, qs, ks, ks, ks, qs, qs, qs],
                      out_specs=[ks, ks, ks], out_shape=[jax.ShapeDtypeStruct((S, MLA_HEADS * LANES), F32)] * 3,
                      scratch_shapes=[pltpu.VMEM((t, LANES), F32)] * 3,
                      compiler_params=_params(("parallel", "parallel", "arbitrary")))(qn, qp, kn, kp, v, lse, do, dl)

    @jax.custom_vjp
    def op(qn, qp, kn, kp, v):
        return fwd_call(qn, qp, kn, kp, v)[0]

    def fwd(qn, qp, kn, kp, v):
        o, lse = fwd_call(qn, qp, kn, kp, v)
        return o, (qn, qp, kn, kp, v, o, lse)

    def bwd(res, do):
        qn, qp, kn, kp, v, o, lse = res
        dqn, dqp, dl = dq_call(qn, qp, kn, kp, v, o, lse, do)
        dkn, dkp, dv = dkv_call(qn, qp, kn, kp, v, lse, do, dl)
        return dqn, dqp, dkn, dkp, dv

    op.defvjp(fwd, bwd)
    return op


def _dil_block(q, kp, kc, vp, vc, has_prev):
    scale = LANES ** -0.5
    r = lax.broadcasted_iota(jnp.int32, (DIL_BLOCK, DIL_BLOCK), 0)
    c = lax.broadcasted_iota(jnp.int32, (DIL_BLOCK, DIL_BLOCK), 1)
    s_c = jnp.where(c <= r, _dot_nt(q, kc) * scale, NEG)
    s_p = jnp.where((c >= r) & has_prev, _dot_nt(q, kp) * scale, NEG)
    m = jnp.maximum(jnp.max(s_c, axis=-1, keepdims=True), jnp.max(s_p, axis=-1, keepdims=True))
    e_c, e_p = jnp.exp(s_c - m), jnp.exp(s_p - m)
    den = jnp.sum(e_c, axis=-1, keepdims=True) + jnp.sum(e_p, axis=-1, keepdims=True)
    o = (_dot_nn(e_c, vc) + _dot_nn(e_p, vp)) / den
    return o, jnp.broadcast_to(m + jnp.log(den), o.shape)


DIL_STEP_BLOCKS = 8


def _make_dil_attn(name):
    def call(q, k, v, cts=None):
        S, W = q.shape
        nb = S // DIL_BLOCK
        G = math.gcd(DIL_STEP_BLOCKS, nb)
        grp = pl.BlockSpec((G * DIL_BLOCK, LANES), lambda h, i: (i, h))
        one = pl.BlockSpec((DIL_BLOCK, LANES), lambda h, i: (i, h))
        before = pl.BlockSpec((DIL_BLOCK, LANES), lambda h, i: (jnp.maximum(G * i - 1, 0), h))
        rows = lambda j: slice(j * DIL_BLOCK, (j + 1) * DIL_BLOCK)

        def has_prev(j):
            h, b = pl.program_id(0), G * pl.program_id(1) + j
            per_seq = lax.shift_right_logical(jnp.int32(nb), 2 * (h // DIL_GROUP_HEADS))
            return (b & (per_seq - 1)) != 0

        def block_inputs(j, q_r, kb_r, k_r, vb_r, v_r):
            kp = kb_r[...] if j == 0 else k_r[rows(j - 1), :]
            vp = vb_r[...] if j == 0 else v_r[rows(j - 1), :]
            return q_r[rows(j), :], kp, k_r[rows(j), :], vp, v_r[rows(j), :]

        if cts is None:
            def body(q_r, kb_r, k_r, vb_r, v_r, o_r, lse_r):
                for j in range(G):
                    o_r[rows(j), :], lse_r[rows(j), :] = _dil_block(*block_inputs(j, q_r, kb_r, k_r, vb_r, v_r), has_prev(j))
            return _pcall(body, name=f"{name}_fwd", grid=(DIL_HEADS, nb // G), in_specs=[grp, before, grp, before, grp],
                          out_specs=[grp, grp], out_shape=[jax.ShapeDtypeStruct(q.shape, F32)] * 2,
                          compiler_params=_params(("parallel", "parallel")))(q, k, k, v, v)

        def body(q_r, kb_r, k_r, vb_r, v_r, do_r, dl_r, dq_r, dk_r, dv_r, dkb_r, dvb_r):
            for j in range(G):
                hp = has_prev(j)
                _, vjp = jax.vjp(lambda *a: _dil_block(*a, hp), *block_inputs(j, q_r, kb_r, k_r, vb_r, v_r))
                dq_r[rows(j), :], dkp, dk_r[rows(j), :], dvp, dv_r[rows(j), :] = vjp((do_r[rows(j), :], dl_r[rows(j), :]))
                if j == 0:
                    dkb_r[...], dvb_r[...] = dkp, dvp
                else:
                    dk_r[rows(j - 1), :] += dkp
                    dv_r[rows(j - 1), :] += dvp
        small = jax.ShapeDtypeStruct((S // G, W), F32)
        return _pcall(body, name=f"{name}_bwd", grid=(DIL_HEADS, nb // G), in_specs=[grp, before, grp, before, grp, grp, grp],
                      out_specs=[grp, grp, grp, one, one], out_shape=[jax.ShapeDtypeStruct(q.shape, F32)] * 3 + [small, small],
                      compiler_params=_params(("parallel", "parallel")))(q, k, k, v, v, *cts)

    @jax.custom_vjp
    def op(q, k, v):
        return tuple(call(q, k, v))

    def bwd(res, cts):
        S, W = res[0].shape
        dq, dk, dv, dkb, dvb = call(*res, cts=cts)
        G = S // dkb.shape[0]

        def placed(a):
            a = a.reshape(-1, DIL_BLOCK, W)
            a = jnp.concatenate([a[1:], jnp.zeros((1, DIL_BLOCK, W), a.dtype)], axis=0)
            return jnp.pad(a, ((0, 0), ((G - 1) * DIL_BLOCK, 0), (0, 0))).reshape(S, W)
        return dq, dk + placed(dkb), dv + placed(dvb)

    op.defvjp(lambda q, k, v: (op(q, k, v), (q, k, v)), bwd)
    return op


def _to_residues(a):
    S = a.shape[0]
    w = DIL_GROUP_HEADS * LANES
    parts = []
    for g, d in enumerate(DIL_DILATIONS):
        p = a[:, g * w:(g + 1) * w]
        parts.append(p.reshape(S // d, d, w).transpose(1, 0, 2).reshape(S, w))
    return jnp.concatenate(parts, axis=1)


def _from_residues(a):
    S = a.shape[0]
    w = DIL_GROUP_HEADS * LANES
    parts = []
    for g, d in enumerate(DIL_DILATIONS):
        p = a[:, g * w:(g + 1) * w]
        parts.append(p.reshape(d, S // d, w).transpose(1, 0, 2).reshape(S, w))
    return jnp.concatenate(parts, axis=1)


def _pdot(a, b, dims):
    return lax.dot_general(a, b, (dims, ((), ())), precision=lax.Precision.HIGH, preferred_element_type=F32)


DN_LOCAL_CHUNKS = 4


def _dn_local(q, k, v, g, b):
    C = DN_CHUNK
    r = lax.broadcasted_iota(jnp.int32, (C, C), 0)
    c = lax.broadcasted_iota(jnp.int32, (C, C), 1)
    incl, strict = r >= c, r > c
    eye = (r == c).astype(F32)
    avg = jnp.full((C, LANES), 1.0 / LANES, F32)
    gc_all = _pdot(incl.astype(F32), g, ((1,), (0,)))
    pad = jnp.zeros((C, LANES - C), F32)
    us, ws, qes, kds, qks, egs = [], [], [], [], [], []
    for h in range(DN_HEADS):
        sl = slice(h * LANES, (h + 1) * LANES)
        qh, kh, vh, bh, gc = q[:, sl], k[:, sl], v[:, sl], b[:, sl], gc_all[:, sl]
        gc_j = _pdot(avg, gc, ((1,), (1,)))
        decay = jnp.exp(jnp.where(incl, gc[:, :C] - gc_j, NEG))
        kb = kh * bh
        kk = _pdot(jnp.concatenate([kb, qh], axis=0), kh, ((1,), (1,)))
        a = jnp.where(strict, kk[:C] * decay, 0.0)
        inv, pw = eye - a, a
        for _ in range(5):
            pw = _pdot(pw, pw, ((1,), (0,)))
            inv = inv + _pdot(inv, pw, ((1,), (0,)))
        eg = jnp.exp(gc)
        uw = _pdot(inv, jnp.concatenate([vh * bh, kb * eg], axis=1), ((1,), (0,)))
        us.append(uw[:, :LANES])
        ws.append(uw[:, LANES:])
        qes.append(qh * eg)
        g_last = gc[C - 1:C, :]
        kds.append(kh * jnp.exp(g_last - gc))
        qks.append(jnp.concatenate([kk[C:] * decay, pad], axis=1))
        egs.append(jnp.broadcast_to(jnp.exp(g_last), (8, LANES)))
    return _cat(us), _cat(ws), _cat(qes), _cat(kds), _cat(qks), _cat(egs)


def _dn_scan(u, w, qe, kd, qk, egl, state):
    C = DN_CHUNK
    outs, states = [], []
    for h in range(DN_HEADS):
        sl = slice(h * LANES, (h + 1) * LANES)
        sh = state[sl, :]
        ws = _pdot(jnp.concatenate([w[:, sl], qe[:, sl]], axis=0), sh, ((1,), (0,)))
        v_new = u[:, sl] - ws[:C]
        outs.append(ws[C:] + _pdot(qk[:, sl][:, :C], v_new, ((1,), (0,))))
        states.append(sh * egl[0:1, sl] + _pdot(kd[:, sl], v_new, ((0,), (0,))))
    return _cat(outs), jnp.concatenate(states, axis=0)


def _make_delta_rule(name):
    W = DN_HEADS * LANES

    def local_call(ins, cts=None):
        S = ins[0].shape[0]
        n = S // DN_CHUNK
        per = math.gcd(DN_LOCAL_CHUNKS, n)
        row = pl.BlockSpec((per * DN_CHUNK, W), lambda i: (i, 0))
        eg = pl.BlockSpec((per, 8, W), lambda i: (i, 0, 0))
        rows = lambda j: slice(j * DN_CHUNK, (j + 1) * DN_CHUNK)

        if cts is None:
            def body(*refs):
                for j in range(per):
                    res = _dn_local(*[r[rows(j), :] for r in refs[:5]])
                    for o_r, o in zip(refs[5:10], res[:5]):
                        o_r[rows(j), :] = o
                    refs[10][j] = res[5]
            return _pcall(body, name=f"{name}_local_fwd", grid=(n // per,), in_specs=[row] * 5, out_specs=[row] * 5 + [eg],
                          out_shape=[jax.ShapeDtypeStruct((S, W), F32)] * 5 + [jax.ShapeDtypeStruct((n, 8, W), F32)],
                          compiler_params=_params(("parallel",)))(*ins)

        def body(*refs):
            for j in range(per):
                _, vjp = jax.vjp(_dn_local, *[r[rows(j), :] for r in refs[:5]])
                grads = vjp(tuple(r[rows(j), :] for r in refs[5:10]) + (refs[10][j],))
                for o_r, o in zip(refs[11:], grads):
                    o_r[rows(j), :] = o
        return _pcall(body, name=f"{name}_local_bwd", grid=(n // per,), in_specs=[row] * 10 + [eg], out_specs=[row] * 5,
                      out_shape=[jax.ShapeDtypeStruct((S, W), F32)] * 5, compiler_params=_params(("parallel",)))(*ins, *cts)

    def scan_call(ins, saved=None, do=None):
        S = ins[0].shape[0]
        n = S // DN_CHUNK
        at = (lambda i: i) if do is None else (lambda i: n - 1 - i)
        row = pl.BlockSpec((DN_CHUNK, W), lambda i: (at(i), 0))
        eg = pl.BlockSpec((None, 8, W), lambda i: (at(i), 0, 0))
        st = pl.BlockSpec((None, W, LANES), lambda i: (at(i), 0, 0))

        if do is None:
            def body(*refs):
                o_r, st_r, s_s = refs[6:]

                @pl.when(pl.program_id(0) == 0)
                def _():
                    s_s[...] = jnp.zeros_like(s_s)
                st_r[...] = s_s[...]
                o_r[...], s_s[...] = _dn_scan(*[r[...] for r in refs[:6]], s_s[...])
            return _pcall(body, name=f"{name}_scan_fwd", grid=(n,), in_specs=[row] * 5 + [eg], out_specs=[row, st],
                          out_shape=[jax.ShapeDtypeStruct((S, W), F32), jax.ShapeDtypeStruct((n, W, LANES), F32)],
                          scratch_shapes=[pltpu.VMEM((W, LANES), F32)], compiler_params=_params(("arbitrary",)))(*ins)

        def body(*refs):
            st_r, do_r = refs[6:8]
            outs, ds_s = refs[8:14], refs[14]

            @pl.when(pl.program_id(0) == 0)
            def _():
                ds_s[...] = jnp.zeros_like(ds_s)
            _, vjp = jax.vjp(_dn_scan, *[r[...] for r in refs[:6]], st_r[...])
            *grads, ds = vjp((do_r[...], ds_s[...]))
            for o_r, gval in zip(outs, grads):
                o_r[...] = gval
            ds_s[...] = ds
        return _pcall(body, name=f"{name}_scan_bwd", grid=(n,), in_specs=[row] * 5 + [eg, st, row], out_specs=[row] * 5 + [eg],
                      out_shape=[jax.ShapeDtypeStruct((S, W), F32)] * 5 + [jax.ShapeDtypeStruct((n, 8, W), F32)],
                      scratch_shapes=[pltpu.VMEM((W, LANES), F32)], compiler_params=_params(("arbitrary",)))(*ins, saved, do)

    @jax.custom_vjp
    def local(q, k, v, g, b):
        return tuple(local_call((q, k, v, g, b)))

    local.defvjp(lambda *a: (local(*a), a), lambda res, cts: tuple(local_call(res, tuple(cts))))

    @jax.custom_vjp
    def scan(u, w, qe, kd, qk, egl):
        return scan_call((u, w, qe, kd, qk, egl))[0]

    def scan_fwd(*a):
        o, states = scan_call(a)
        return o, (a, states)

    scan.defvjp(scan_fwd, lambda res, do: tuple(scan_call(res[0], res[1], do)))
    return lambda q, k, v, g, b: scan(*local(q, k, v, g, b))


def _loss_call(y, target):
    S, D = y.shape
    t = min(S, 512)
    n = S // t
    row = pl.BlockSpec((t, D), lambda i: (i, 0))

    def body(y_r, t_r, loss_r, dy_r, acc_s):
        i = pl.program_id(0)

        @pl.when(i == 0)
        def _():
            acc_s[...] = jnp.zeros_like(acc_s)
        err = y_r[...] - t_r[...]
        dy_r[...] = err * (1.0 / D)
        acc_s[...] += jnp.sum(err * err, axis=0, keepdims=True)

        @pl.when(i == n - 1)
        def _():
            loss_r[...] = jnp.broadcast_to(jnp.sum(acc_s[...], axis=1, keepdims=True) * (0.5 / D), loss_r.shape)

    return _pcall(body, name="loss_head", grid=(n,), in_specs=[row, row],
                  out_specs=[pl.BlockSpec((8, LANES), lambda i: (0, 0)), row],
                  out_shape=[jax.ShapeDtypeStruct((8, LANES), F32), jax.ShapeDtypeStruct((S, D), F32)],
                  scratch_shapes=[pltpu.VMEM((1, D), F32)], compiler_params=_params(("arbitrary",)))(y, target)


def _adamw_call(name, parts, w, m, v, rows=128):
    L, R, C = w.shape
    t = _tile(R, rows, 8)
    row = pl.BlockSpec((None, t, C), lambda l, i: (l, i, 0))

    def body(p_r, w_r, m_r, v_r, g_r, d_r, nm_r, nv_r):
        g = p_r[0].astype(F32)
        for s in range(1, N_DEV):
            g = g + p_r[s].astype(F32)
        m_new = ADAM_B1 * m_r[...] + (1.0 - ADAM_B1) * g
        v_new = ADAM_B2 * v_r[...] + (1.0 - ADAM_B2) * (g * g)
        m_hat = m_new / (1.0 - ADAM_B1 ** ADAM_STEP)
        v_hat = v_new / (1.0 - ADAM_B2 ** ADAM_STEP)
        g_r[...] = g
        d_r[...] = -ADAM_LR * (m_hat / (jnp.sqrt(v_hat) + ADAM_EPS) + ADAM_WD * w_r[...])
        nm_r[...] = m_new
        nv_r[...] = v_new

    return _pcall(body, name=name, grid=(L, R // t),
                  in_specs=[pl.BlockSpec((N_DEV, None, t, C), lambda l, i: (0, l, i, 0)), row, row, row],
                  out_specs=[row] * 4, out_shape=[jax.ShapeDtypeStruct((L, R, C), F32)] * 4,
                  compiler_params=_params(("parallel", "parallel")))(parts, w, m, v)


def _my_place():
    x, y, c = lax.axis_index("x"), lax.axis_index("y"), lax.axis_index("c")
    return x, y, c


def _index(x, y, c):
    return 4 * x + 2 * y + c


def _all_gather(vs):
    n = len(vs)

    def body(*refs):
        v_refs, out_refs = refs[:n], refs[n:2 * n]
        send_sems, recv_sems, local_sems = refs[2 * n:]
        x, y, c = _my_place()
        me, sibling = (x, y, c), (x, y, 1 - c)
        chips = [(1 - x, y), (x, 1 - y), (1 - x, 1 - y)]

        def copy(a, k, block, to, src=None):
            rows = out_refs[a].at[_index(*block)]
            return pltpu.make_async_remote_copy(src_ref=rows if src is None else src, dst_ref=rows, send_sem=send_sems.at[a, k],
                                                recv_sem=recv_sems.at[a, k], device_id=to, device_id_type=MESH)

        mine = [pltpu.make_async_copy(v_refs[a], out_refs[a].at[_index(*me)], local_sems.at[a]) for a in range(n)]
        first, passed = [], []
        for a in range(n):
            mine[a].start()
            first += [copy(a, 0, me, sibling, src=v_refs[a])]
            first += [copy(a, 1 + j, me, (*chip, c), src=v_refs[a]) for j, chip in enumerate(chips)]
        for cp in first:
            cp.start()
        for j, chip in enumerate(chips):
            for a in range(n):
                copy(a, 1 + j, (*chip, c), me).wait_recv()
                passed.append(copy(a, 4 + j, (*chip, c), sibling))
                passed[-1].start()
        for a in range(n):
            copy(a, 0, sibling, me).wait_recv()
            for j, chip in enumerate(chips):
                copy(a, 4 + j, (*chip, 1 - c), me).wait_recv()
        for cp in first + passed:
            cp.wait_send()
        for a in range(n):
            mine[a].wait()

    any_ = pl.BlockSpec(memory_space=pl.ANY)
    return _pcall(body, name="gather_weights", in_specs=[any_] * n, out_specs=[any_] * n,
                  out_shape=[jax.ShapeDtypeStruct((N_DEV,) + v.shape, v.dtype) for v in vs],
                  scratch_shapes=[pltpu.SemaphoreType.DMA((n, 7)), pltpu.SemaphoreType.DMA((n, 7)), pltpu.SemaphoreType.DMA((n,))])(*vs)


def _all_to_all(vs):
    n = len(vs)

    def body(*refs):
        v_refs, out_refs = refs[:n], refs[n:2 * n]
        send_sems, recv_sems, local_sems = refs[2 * n:]
        x, y, c = _my_place()
        me = _index(x, y, c)
        mine = [pltpu.make_async_copy(v_refs[a].at[me], out_refs[a].at[me], local_sems.at[a]) for a in range(n)]
        copies = []
        for a in range(n):
            mine[a].start()
        for k in range(1, N_DEV):
            px = 1 - x if k & 4 else x
            py = 1 - y if k & 2 else y
            pc = 1 - c if k & 1 else c
            for a in range(n):
                cp = pltpu.make_async_remote_copy(src_ref=v_refs[a].at[_index(px, py, pc)], dst_ref=out_refs[a].at[me],
                                                  send_sem=send_sems.at[a, k - 1], recv_sem=recv_sems.at[a, k - 1],
                                                  device_id=(px, py, pc), device_id_type=MESH)
                cp.start()
                copies.append(cp)
        for cp in copies:
            cp.wait()
        for a in range(n):
            mine[a].wait()

    any_ = pl.BlockSpec(memory_space=pl.ANY)
    return _pcall(body, name="exchange_grads", in_specs=[any_] * n, out_specs=[any_] * n,
                  out_shape=[jax.ShapeDtypeStruct(v.shape, v.dtype) for v in vs],
                  scratch_shapes=[pltpu.SemaphoreType.DMA((n, 7)), pltpu.SemaphoreType.DMA((n, 7)), pltpu.SemaphoreType.DMA((n,))])(*vs)


W_IN_SHARD = IN_WIDTH // N_DEV
SEG_ORDER = ("q_lat", "c_kv", "k_pe", "z_a", "dn_qkv", "dn_ab", "z_b", "dil_qkv", "z_c", "gate")
SEG_WIDTH = (384, 256, LANES, 512, 1536, LANES, 512, 4608, 512, 3072)


def _w_in_plan():
    plan = []

    def add(seg, c0, c1, dst):
        while c0 < c1:
            d = c0 // W_IN_SHARD
            e = min(c1, (d + 1) * W_IN_SHARD)
            plan.append((seg, dst, d, c0 - d * W_IN_SHARD, e - c0))
            dst += e - c0
            c0 = e

    half = MLA_ROPE // 2
    for i, name in enumerate(SEG_ORDER):
        if name == "k_pe":
            o = _SEG["k_pe"][0]
            add(i, o, o + half, 0)
            add(i, o + half, o + 2 * half, LANES // 2)
        elif name == "dn_ab":
            o = _SEG["dn_a"][0]
            add(i, o, o + 2 * DN_HEADS, 0)
        else:
            o, w = _SEG[name]
            add(i, o, o + w, 0)
    return plan


def _make_w_in_segments(name):
    plan = _w_in_plan()
    nseg = len(SEG_ORDER)
    t = 256

    def fwd_call(g):
        L = g.shape[1]

        def body(g_ref, *o_refs):
            for i in (SEG_ORDER.index("k_pe"), SEG_ORDER.index("dn_ab")):
                o_refs[i][...] = jnp.zeros_like(o_refs[i])
            for seg, dst, d, src, n in plan:
                o_refs[seg][:, dst:dst + n] = g_ref[d, :, src:src + n]

        return _pcall(body, name=f"{name}_fwd", grid=(L, D_MODEL // t),
                      in_specs=[pl.BlockSpec((N_DEV, None, t, W_IN_SHARD), lambda l, i: (0, l, i, 0))],
                      out_specs=[pl.BlockSpec((None, t, w), lambda l, i: (l, i, 0)) for w in SEG_WIDTH],
                      out_shape=[jax.ShapeDtypeStruct((L, D_MODEL, w), g.dtype) for w in SEG_WIDTH],
                      compiler_params=_params(("parallel", "parallel")))(g)

    def bwd_call(ds):
        L = ds[0].shape[0]

        def body(*refs):
            d_refs, g_ref = refs[:nseg], refs[nseg]
            for seg, dst, d, src, n in plan:
                g_ref[d, :, src:src + n] = d_refs[seg][:, dst:dst + n]

        return _pcall(body, name=f"{name}_bwd", grid=(L, D_MODEL // t),
                      in_specs=[pl.BlockSpec((None, t, w), lambda l, i: (l, i, 0)) for w in SEG_WIDTH],
                      out_specs=pl.BlockSpec((N_DEV, None, t, W_IN_SHARD), lambda l, i: (0, l, i, 0)),
                      out_shape=jax.ShapeDtypeStruct((N_DEV, L, D_MODEL, W_IN_SHARD), ds[0].dtype),
                      compiler_params=_params(("parallel", "parallel")))(*ds)

    @jax.custom_vjp
    def op(g):
        return tuple(fwd_call(g))

    op.defvjp(lambda g: (op(g), None), lambda _, ds: (bwd_call(tuple(ds)),))
    return op


def _pe_pad(a):
    h = MLA_ROPE // 2
    z = jnp.zeros(a.shape[:-1] + (h,), a.dtype)
    return jnp.concatenate([a[..., :h], z, a[..., h:], z], axis=-1)


def _layer(tag, x, tables, W):
    cos_p, sin_p, cos_h, sin_h = tables
    w_segs = W["w_in_segments"]
    row = lambda a: a[None, :]

    (h,) = _make_rowwise(f"{tag}_norm", _f_norm, 512)((x,), (), (row(W["norm_g"]),), ())
    q_lat, c_kv, kpe, z_a, dn_qkv, ab, z_b, dil_qkv, z_c, gl = _make_multi_linear(f"{tag}_inproj", 10)(h, w_segs)

    qn_lat, ckvn, kp = _make_rowwise(f"{tag}_mla_a", _f_mla_a, 512)(
        (q_lat, c_kv, kpe), (cos_p, sin_p),
        (row(W["mla_q_a_norm_g"]), row(W["mla_kv_a_norm_g"]), row(_pe_pad(W["mla_k_norm_g"][LANES:]))), ())
    wq = W["mla_w_q_b"].reshape(MLA_Q_RANK, MLA_HEADS, MLA_QK)
    wq = jnp.concatenate([wq[:, :, :LANES].reshape(MLA_Q_RANK, -1), _pe_pad(wq[:, :, LANES:]).reshape(MLA_Q_RANK, -1)], axis=1)
    wkv = W["mla_w_kv_b"].reshape(MLA_KV_RANK, MLA_HEADS, 2 * LANES)
    (q8,) = _make_multi_linear(f"{tag}_qb", 1)(qn_lat, (wq,))
    kn_raw, v_mla = _make_multi_linear(f"{tag}_kvb", 2)(
        ckvn, (wkv[:, :, :LANES].reshape(MLA_KV_RANK, -1), wkv[:, :, LANES:].reshape(MLA_KV_RANK, -1)))
    qn, qp, kn = _make_rowwise(f"{tag}_mla_b", _f_mla_b, 512)(
        (q8, kn_raw), (cos_p, sin_p),
        (row(W["mla_q_norm_g"][:LANES]), row(_pe_pad(W["mla_q_norm_g"][LANES:])), row(W["mla_k_norm_g"][:LANES])), ())
    y_a = _make_mla_attn(f"{tag}_mla")(qn, qp, kn, kp, v_mla)

    mixed = _make_conv(f"{tag}_conv")(dn_qkv, W["dn_conv_w"])
    lane_head = jnp.arange(DN_HEADS * LANES) // LANES
    e_a = (jnp.arange(LANES)[:, None] == lane_head[None, :]).astype(F32)
    e_b = (jnp.arange(LANES)[:, None] == lane_head[None, :] + DN_HEADS).astype(F32)
    q_dn, k_dn, v_dn, g_dn, b_dn = _make_rowwise(f"{tag}_dn_pre", _f_dn_pre, 512)(
        (mixed, ab), (), (row(jnp.repeat(W["dn_a_log"], LANES)), row(jnp.repeat(W["dn_dt_bias"], LANES))), (e_a, e_b))
    o_dn = _make_delta_rule(f"{tag}_dn")(q_dn, k_dn, v_dn, g_dn, b_dn)

    q_dil, k_dil, v_dil = _make_rowwise(f"{tag}_dil_pre", _f_dil_pre, 256)(
        (dil_qkv,), (cos_h, sin_h), (row(W["dil_q_norm_g"]), row(W["dil_k_norm_g"])), ())
    o_dil, lse_dil = _make_dil_attn(f"{tag}_dil")(_to_residues(q_dil), _to_residues(k_dil), _to_residues(v_dil))
    o_dil, lse_dil = _from_residues(o_dil), _from_residues(lse_dil)

    ya, yb, yc = _make_rowwise(f"{tag}_merge_a", _f_merge_a, 256)(
        (y_a, z_a, o_dn, z_b, o_dil, lse_dil, z_c), (), (row(W["dn_out_norm_g"]),), ())
    (b0,) = _make_multi_linear(f"{tag}_br0", 1)(ya, (W["w_branch"][0],))
    (b1,) = _make_multi_linear(f"{tag}_br1", 1)(yb, (W["w_branch"][1],))
    (b2,) = _make_multi_linear(f"{tag}_br2", 1)(yc, (W["w_branch"][2],))
    (mix,) = _make_rowwise(f"{tag}_merge_b", _f_merge_b, 256)((b0, b1, b2, gl), (), (), ())
    return _make_resid_linear(f"{tag}_out")(x, mix, W["w_out"])


SHARDED = (("w_in", (D_MODEL, W_IN_SHARD)), ("mla_w_q_b", (MLA_Q_RANK, MLA_HEADS * MLA_QK // N_DEV)),
           ("mla_w_kv_b", (MLA_KV_RANK, MLA_HEADS * 2 * LANES // N_DEV)), ("w_branch", (3 * BRANCH_W, D_MODEL // N_DEV)),
           ("w_out", (D_MODEL // N_DEV, D_MODEL)), ("dn_conv_w", (DN_CONV, 3 * DN_HEADS * LANES // N_DEV)))
SMALL = (("norm_g", D_MODEL), ("mla_q_a_norm_g", MLA_Q_RANK), ("mla_kv_a_norm_g", MLA_KV_RANK), ("mla_q_norm_g", MLA_QK),
         ("mla_k_norm_g", MLA_QK), ("dn_a_log", DN_HEADS), ("dn_dt_bias", DN_HEADS), ("dn_out_norm_g", LANES),
         ("dil_q_norm_g", LANES), ("dil_k_norm_g", LANES))
WEIGHTS = ("norm_g", "w_in", "mla_q_a_norm_g", "mla_w_q_b", "mla_kv_a_norm_g", "mla_w_kv_b", "mla_q_norm_g", "mla_k_norm_g",
           "dn_conv_w", "dn_a_log", "dn_dt_bias", "dn_out_norm_g", "dil_q_norm_g", "dil_k_norm_g", "w_branch", "w_out")


def _round_up(n, m):
    return -(-n // m) * m


def _pack_vectors(pieces):
    return jnp.concatenate([jnp.pad(p, (0, _round_up(p.shape[0], LANES) - p.shape[0])) for p in pieces]).reshape(-1, LANES)


def _unpack_vectors(flat, sizes):
    out, off = [], 0
    flat = flat.reshape(-1)
    for n in sizes:
        out.append(flat[off:off + n])
        off += _round_up(n, LANES)
    return out


def _whole_weights(l, segs, g, small):
    W = dict(small)
    W["w_in_segments"] = tuple(s[l] for s in segs)
    W["mla_w_q_b"] = g["mla_w_q_b"][:, l].transpose(1, 0, 2).reshape(MLA_Q_RANK, -1)
    W["mla_w_kv_b"] = g["mla_w_kv_b"][:, l].transpose(1, 0, 2).reshape(MLA_KV_RANK, -1)
    W["w_branch"] = g["w_branch"][:, l].reshape(N_DEV, 3, BRANCH_W, -1).transpose(1, 2, 0, 3).reshape(3, BRANCH_W, D_MODEL)
    W["w_out"] = g["w_out"][:, l].reshape(D_MODEL, D_MODEL)
    W["dn_conv_w"] = g["dn_conv_w"][:, l].transpose(1, 0, 2).reshape(DN_CONV, -1)
    return W


def kernel(x, positions, norm_g, w_in, mla_q_a_norm_g, mla_w_q_b, mla_kv_a_norm_g, mla_w_kv_b, mla_q_norm_g, mla_k_norm_g, dn_conv_w, dn_a_log, dn_dt_bias, dn_out_norm_g, dil_q_norm_g, dil_k_norm_g, w_branch, w_out, loss_target, m_norm_g, m_w_in, m_mla_q_a_norm_g, m_mla_w_q_b, m_mla_kv_a_norm_g, m_mla_w_kv_b, m_mla_q_norm_g, m_mla_k_norm_g, m_dn_conv_w, m_dn_a_log, m_dn_dt_bias, m_dn_out_norm_g, m_dil_q_norm_g, m_dil_k_norm_g, m_w_branch, m_w_out, v_norm_g, v_w_in, v_mla_q_a_norm_g, v_mla_w_q_b, v_mla_kv_a_norm_g, v_mla_w_kv_b, v_mla_q_norm_g, v_mla_k_norm_g, v_dn_conv_w, v_dn_a_log, v_dn_dt_bias, v_dn_out_norm_g, v_dil_q_norm_g, v_dil_k_norm_g, v_w_branch, v_w_out):
    w = dict(norm_g=norm_g, w_in=w_in, mla_q_a_norm_g=mla_q_a_norm_g, mla_w_q_b=mla_w_q_b, mla_kv_a_norm_g=mla_kv_a_norm_g,
             mla_w_kv_b=mla_w_kv_b, mla_q_norm_g=mla_q_norm_g, mla_k_norm_g=mla_k_norm_g, dn_conv_w=dn_conv_w, dn_a_log=dn_a_log,
             dn_dt_bias=dn_dt_bias, dn_out_norm_g=dn_out_norm_g, dil_q_norm_g=dil_q_norm_g, dil_k_norm_g=dil_k_norm_g,
             w_branch=w_branch, w_out=w_out)
    m = dict(norm_g=m_norm_g, w_in=m_w_in, mla_q_a_norm_g=m_mla_q_a_norm_g, mla_w_q_b=m_mla_w_q_b, mla_kv_a_norm_g=m_mla_kv_a_norm_g,
             mla_w_kv_b=m_mla_w_kv_b, mla_q_norm_g=m_mla_q_norm_g, mla_k_norm_g=m_mla_k_norm_g, dn_conv_w=m_dn_conv_w,
             dn_a_log=m_dn_a_log, dn_dt_bias=m_dn_dt_bias, dn_out_norm_g=m_dn_out_norm_g, dil_q_norm_g=m_dil_q_norm_g,
             dil_k_norm_g=m_dil_k_norm_g, w_branch=m_w_branch, w_out=m_w_out)
    v = dict(norm_g=v_norm_g, w_in=v_w_in, mla_q_a_norm_g=v_mla_q_a_norm_g, mla_w_q_b=v_mla_w_q_b, mla_kv_a_norm_g=v_mla_kv_a_norm_g,
             mla_w_kv_b=v_mla_w_kv_b, mla_q_norm_g=v_mla_q_norm_g, mla_k_norm_g=v_mla_k_norm_g, dn_conv_w=v_dn_conv_w,
             dn_a_log=v_dn_a_log, dn_dt_bias=v_dn_dt_bias, dn_out_norm_g=v_dn_out_norm_g, dil_q_norm_g=v_dil_q_norm_g,
             dil_k_norm_g=v_dil_k_norm_g, w_branch=v_w_branch, w_out=v_w_out)
    x2, target = x[0], loss_target[0]
    pos = positions[0][:, None]

    view = lambda t, n, s: t[n].reshape((DEPTH,) + s)
    shards = [view(w, n, s) if n == "dn_conv_w" else view(w, n, s).astype(BF16) for n, s in SHARDED]
    gathered = dict(zip([n for n, _ in SHARDED], _all_gather(shards)))
    small = [{n: w[n][l] for n, _ in SMALL} for l in range(DEPTH)]
    tables = _rope_tables(pos, _rope_consts())

    def forward(gathered, small, x2):
        segs = _make_w_in_segments("w_in_segments")(gathered["w_in"])
        y = x2
        for l in range(DEPTH):
            y = _layer(f"l{l}", y, tables, _whole_weights(l, segs, gathered, small[l]))
        return y

    y, vjp = jax.vjp(forward, gathered, small, x2)
    loss_splat, dy = _loss_call(y, target)
    g_gathered, g_small, g_x = vjp(dy)
    loss = lax.psum(loss_splat[0, 0], ("x", "y", "c"))

    sizes = [k for _ in range(DEPTH) for _, k in SMALL]
    g_vec = _pack_vectors([g_small[l][n] for l in range(DEPTH) for n, _ in SMALL])
    sent = [g_gathered[n] for n, _ in SHARDED] + [jnp.broadcast_to(g_vec[None, None], (N_DEV, 1) + g_vec.shape)]
    parts = _all_to_all(sent)

    vec = lambda t: _pack_vectors([t[n][l] for l in range(DEPTH) for n, _ in SMALL])[None]
    outs = {}
    for (n, s), p in zip(SHARDED, parts[:-1]):
        outs[n] = [o.reshape(w[n].shape) for o in _adamw_call(f"adamw_{n}", p, view(w, n, s), view(m, n, s), view(v, n, s))]
    vec_outs = [_unpack_vectors(o, sizes) for o in _adamw_call("adamw_vectors", parts[-1], vec(w), vec(m), vec(v))]
    for i, (n, _) in enumerate(SMALL):
        outs[n] = [jnp.stack([o[l * len(SMALL) + i] for l in range(DEPTH)]) for o in vec_outs]
    return (loss, g_x[None], *[outs[n][k] for k in range(4) for n in WEIGHTS])
```

```python
import functools
import math

import jax
import jax.numpy as jnp
from jax import lax
from jax.experimental import pallas as pl
from jax.experimental.pallas import tpu as pltpu

F32 = jnp.float32
BF16 = jnp.bfloat16
HI = lax.Precision.HIGHEST
MESH = pl.DeviceIdType.MESH

N_DEV = 8
D_MODEL = 1024
DEPTH = 2
RMS_EPS = 1e-6
ROPE_THETA = 10000.0
LANES = 128
MLA_HEADS = 4
MLA_ROPE = 64
MLA_QK = 192
MLA_Q_RANK = 384
MLA_KV_RANK = 256
DN_HEADS = 4
DN_CHUNK = 64
DN_CONV = 4
DIL_HEADS = 12
DIL_GROUP_HEADS = 4
DIL_DILATIONS = (1, 4, 16)
DIL_BLOCK = 128
BRANCH_W = 512
IN_WIDTH = 11464
NEG = -1e30
VMEM_LIMIT = 56 * 1024 * 1024

ADAM_LR, ADAM_B1, ADAM_B2, ADAM_EPS, ADAM_WD, ADAM_STEP = 0.001, 0.9, 0.999, 1e-08, 0.01, 10

_SEG = {}
_off = 0
for _n, _w in (("q_lat", 384), ("c_kv", 256), ("k_pe", 64), ("z_a", 512), ("dn_qkv", 1536), ("dn_a", 4), ("dn_b", 4),
               ("z_b", 512), ("dil_qkv", 4608), ("z_c", 512), ("gate", 3072)):
    _SEG[_n] = (_off, _w)
    _off += _w
assert _off == IN_WIDTH


def _pcall(body, **kw):
    return pl.pallas_call(body, **kw)


def _params(sem=None):
    return pltpu.CompilerParams(dimension_semantics=sem, vmem_limit_bytes=VMEM_LIMIT)


def _tile(n, target, mult):
    t = (min(n, target) // mult) * mult
    while t >= mult:
        if n % t == 0:
            return t
        t -= mult
    return n


def _mm(name, a, b, mode, out_dtype=F32, acc=None, tm=1024, tn=512, tk=1024):
    if mode == "nn":
        (M, K), (_, N) = a.shape, b.shape
    elif mode == "nt":
        (M, K), (N, _) = a.shape, b.shape
    else:
        (K, M), (_, N) = a.shape, b.shape
    tm, tn, tk = _tile(M, tm, LANES), _tile(N, tn, LANES), _tile(K, tk, LANES)
    nk = K // tk
    dims = {"nn": (((1,), (0,)), ((), ())), "nt": (((1,), (1,)), ((), ())), "tn": (((0,), (0,)), ((), ()))}[mode]
    a_spec = pl.BlockSpec((tk, tm), lambda i, j, k: (k, i)) if mode == "tn" else pl.BlockSpec((tm, tk), lambda i, j, k: (i, k))
    b_spec = pl.BlockSpec((tn, tk), lambda i, j, k: (j, k)) if mode == "nt" else pl.BlockSpec((tk, tn), lambda i, j, k: (k, j))
    o_spec = pl.BlockSpec((tm, tn), lambda i, j, k: (i, j))
    has_acc = acc is not None

    def body(*refs):
        if has_acc:
            a_ref, b_ref, c_ref, o_ref, acc_ref = refs
        else:
            a_ref, b_ref, o_ref, acc_ref = refs
        k = pl.program_id(2)

        @pl.when(k == 0)
        def _():
            acc_ref[...] = c_ref[...].astype(F32) if has_acc else jnp.zeros_like(acc_ref)

        acc_ref[...] += lax.dot_general(a_ref[...].astype(BF16), b_ref[...].astype(BF16), dims, preferred_element_type=F32)

        @pl.when(k == nk - 1)
        def _():
            o_ref[...] = acc_ref[...].astype(out_dtype)

    ins = [a, b] + ([acc] if has_acc else [])
    in_specs = [a_spec, b_spec] + ([o_spec] if has_acc else [])
    return _pcall(body, name=name, grid=(M // tm, N // tn, nk), in_specs=in_specs, out_specs=o_spec,
                  out_shape=jax.ShapeDtypeStruct((M, N), out_dtype), scratch_shapes=[pltpu.VMEM((tm, tn), F32)],
                  compiler_params=_params(("parallel", "parallel", "arbitrary")))(*ins)


def _make_multi_linear(name, n):
    @jax.custom_vjp
    def op(h, ws):
        return tuple(_mm(f"{name}_fwd{i}", h, w, "nn") for i, w in enumerate(ws))

    def fwd(h, ws):
        return op(h, ws), (h, ws)

    def bwd(res, douts):
        h, ws = res
        dh = None
        for i, (w, d) in enumerate(zip(ws, douts)):
            dh = _mm(f"{name}_dh{i}", d, w, "nt", acc=dh)
        dws = tuple(_mm(f"{name}_dw{i}", h, d, "tn", out_dtype=w.dtype) for i, (w, d) in enumerate(zip(ws, douts)))
        return dh.astype(h.dtype), dws

    op.defvjp(fwd, bwd)
    return op


def _make_resid_linear(name):
    @jax.custom_vjp
    def op(x, a, w):
        return _mm(f"{name}_fwd", a, w, "nn", acc=x)

    def fwd(x, a, w):
        return op(x, a, w), (a, w)

    def bwd(res, dy):
        a, w = res
        return dy, _mm(f"{name}_da", dy, w, "nt"), _mm(f"{name}_dw", a, dy, "tn", out_dtype=w.dtype)

    op.defvjp(fwd, bwd)
    return op


def _make_rowwise(name, f, tile):
    def specs(rows, aux, params, consts, t):
        row = [pl.BlockSpec((t, a.shape[1]), lambda i: (i, 0)) for a in (*rows, *aux)]
        full = [pl.BlockSpec(p.shape, lambda i: (0, 0)) for p in (*params, *consts)]
        return row, full

    def fwd_call(rows, aux, params, consts):
        S = rows[0].shape[0]
        t = min(tile, S)
        n_in = len(rows) + len(aux) + len(params) + len(consts)
        shp = lambda a: jax.ShapeDtypeStruct((t, a.shape[1]), a.dtype)
        outs = jax.eval_shape(f, *[shp(a) for a in (*rows, *aux)], *params, *consts)
        row_specs, full_specs = specs(rows, aux, params, consts, t)

        def body(*refs):
            res = f(*[r[...] for r in refs[:n_in]])
            for o_ref, o in zip(refs[n_in:], res):
                o_ref[...] = o

        return _pcall(body, name=f"{name}_fwd", grid=(S // t,), in_specs=row_specs + full_specs,
                      out_specs=[pl.BlockSpec((t, o.shape[1]), lambda i: (i, 0)) for o in outs],
                      out_shape=[jax.ShapeDtypeStruct((S, o.shape[1]), o.dtype) for o in outs],
                      compiler_params=_params(("parallel",)))(*rows, *aux, *params, *consts)

    def bwd_call(rows, aux, params, consts, douts):
        S = rows[0].shape[0]
        t = min(tile, S)
        nr, na, npar, nc, nd = len(rows), len(aux), len(params), len(consts), len(douts)
        row_specs, full_specs = specs(rows, aux, params, consts, t)

        def body(*refs):
            vals = [r[...] for r in refs[:nr + na + npar + nc]]
            rv, av = vals[:nr], vals[nr:nr + na]
            pv, cv = vals[nr + na:nr + na + npar], vals[nr + na + npar:]
            dv = tuple(r[...] for r in refs[nr + na + npar + nc:nr + na + npar + nc + nd])
            out_refs = refs[nr + na + npar + nc + nd:]
            _, vjp = jax.vjp(lambda *rp: f(*rp[:nr], *av, *rp[nr:], *cv), *rv, *pv)
            grads = vjp(dv)
            for o_ref, g in zip(out_refs[:nr], grads[:nr]):
                o_ref[...] = g
            first = pl.program_id(0) == 0
            for o_ref, g in zip(out_refs[nr:], grads[nr:]):
                @pl.when(first)
                def _(o_ref=o_ref):
                    o_ref[...] = jnp.zeros_like(o_ref)
                o_ref[...] += g

        res = _pcall(body, name=f"{name}_bwd", grid=(S // t,),
                     in_specs=row_specs + full_specs + [pl.BlockSpec((t, d.shape[1]), lambda i: (i, 0)) for d in douts],
                     out_specs=[pl.BlockSpec((t, a.shape[1]), lambda i: (i, 0)) for a in rows]
                     + [pl.BlockSpec(p.shape, lambda i: (0, 0)) for p in params],
                     out_shape=[jax.ShapeDtypeStruct(a.shape, a.dtype) for a in (*rows, *params)],
                     compiler_params=_params(("arbitrary",)))(*rows, *aux, *params, *consts, *douts)
        return tuple(res[:nr]), tuple(res[nr:])

    @jax.custom_vjp
    def op(rows, aux, params, consts):
        return tuple(fwd_call(rows, aux, params, consts))

    def fwd(rows, aux, params, consts):
        return op(rows, aux, params, consts), (rows, aux, params, consts)

    def bwd(res, douts):
        rows, aux, params, consts = res
        drows, dparams = bwd_call(rows, aux, params, consts, tuple(douts))
        zeros = lambda xs: tuple(jnp.zeros_like(a) for a in xs)
        return drows, zeros(aux), dparams, zeros(consts)

    op.defvjp(fwd, bwd)
    return op


@jax.custom_vjp
def _swap_halves(x):
    return pltpu.roll(x, LANES // 2, 1)


_swap_halves.defvjp(lambda x: (_swap_halves(x), None), lambda _, g: (_swap_halves(g),))


def _rope(x, cos_t, sin_t):
    return x * cos_t + _swap_halves(x) * sin_t


def _rms(x, g, n=None):
    n = x.shape[-1] if n is None else n
    return x * lax.rsqrt(jnp.sum(x * x, axis=-1, keepdims=True) * (1.0 / n) + RMS_EPS) * g


def _heads(x):
    return [x[:, i * LANES:(i + 1) * LANES] for i in range(x.shape[1] // LANES)]


def _cat(xs):
    return jnp.concatenate(xs, axis=1)


def _silu(x):
    return x * jax.nn.sigmoid(x)


def _f_norm(x, g):
    return (_rms(x, g),)


def _f_mla_a(q_lat, c_kv, kpe, cos_p, sin_p, qa_g, kva_g, kpe_g):
    kp = _rope(_rms(kpe, kpe_g, MLA_ROPE), cos_p, sin_p)
    return _rms(q_lat, qa_g), _rms(c_kv, kva_g), _cat([kp] * MLA_HEADS)


def _f_mla_b(q8, kn_raw, cos_p, sin_p, qn_g, qp_g, kn_g):
    hs = _heads(q8)
    qn = _cat([_rms(h, qn_g) for h in hs[:MLA_HEADS]])
    qp = _cat([_rope(_rms(h, qp_g, MLA_ROPE), cos_p, sin_p) for h in hs[MLA_HEADS:]])
    kn = _cat([_rms(h, kn_g) for h in _heads(kn_raw)])
    return qn, qp, kn


def _softplus(x):
    return jnp.maximum(x, 0.0) + jnp.log(1.0 + jnp.exp(-jnp.abs(x)))


def _l2n(x):
    return x * lax.rsqrt(jnp.sum(x * x, axis=-1, keepdims=True) + 1e-6)


def _f_dn_pre(mixed, ab, alog_f, dtb_f, e_a, e_b):
    hs = _heads(mixed)
    q = _cat([_l2n(h) * (LANES ** -0.5) for h in hs[:DN_HEADS]])
    k = _cat([_l2n(h) for h in hs[DN_HEADS:2 * DN_HEADS]])
    v = _cat(hs[2 * DN_HEADS:])
    a_f = jnp.dot(ab, e_a, precision=HI, preferred_element_type=F32)
    b_f = jnp.dot(ab, e_b, precision=HI, preferred_element_type=F32)
    g = -jnp.exp(alog_f) * _softplus(a_f + dtb_f)
    return q, k, v, g, jax.nn.sigmoid(b_f)


def _f_dil_pre(qkv, cos_h, sin_h, q_g, k_g):
    hs = _heads(qkv)
    q = [_rope(_rms(h, q_g), cos_h, sin_h) for h in hs[:DIL_HEADS]]
    k = [_rope(_rms(h, k_g), cos_h, sin_h) for h in hs[DIL_HEADS:2 * DIL_HEADS]]
    v = hs[2 * DIL_HEADS:]
    group = lambda xs, g: _cat(xs[g * DIL_GROUP_HEADS:(g + 1) * DIL_GROUP_HEADS])
    return tuple(group(xs, g) for xs in (q, k, v) for g in range(len(DIL_DILATIONS)))


def _f_merge_a(y_a, z_a, o_dn, z_b, o0, o1, o2, l0, l1, l2, z_c, out_g):
    y_b = _cat([_rms(h, out_g) for h in _heads(o_dn)])
    os_, ls = [_heads(o) for o in (o0, o1, o2)], [_heads(l) for l in (l0, l1, l2)]
    y_c = []
    for j in range(DIL_GROUP_HEADS):
        l3 = [ls[g][j] for g in range(3)]
        m = jnp.maximum(jnp.maximum(l3[0], l3[1]), l3[2])
        e3 = [jnp.exp(l - m) for l in l3]
        den = e3[0] + e3[1] + e3[2]
        y_c.append(sum(e3[g] * os_[g][j] for g in range(3)) / den)
    return y_a * _silu(z_a), y_b * _silu(z_b), _cat(y_c) * _silu(z_c)


def _f_merge_b(b0, b1, b2, gl):
    gs = [jax.nn.sigmoid(gl[:, i * D_MODEL:(i + 1) * D_MODEL]) for i in range(3)]
    return (gs[0] * b0 + gs[1] * b1 + gs[2] * b2,)


def _rope_tables(pos, inv_sign):
    S = pos.shape[0]
    t = min(S, 1024)

    def body(p_ref, c_ref, cp, sp, ch, sh):
        p = p_ref[...].astype(F32)
        c = c_ref[...]
        ang_p, ang_h = p * c[0:1], p * c[2:3]
        cp[...] = jnp.cos(ang_p) * jnp.abs(c[1:2])
        sp[...] = jnp.sin(ang_p) * c[1:2]
        ch[...] = jnp.cos(ang_h)
        sh[...] = jnp.sin(ang_h) * c[3:4]

    row = pl.BlockSpec((t, LANES), lambda i: (i, 0))
    return _pcall(body, name="rope_tables", grid=(S // t,),
                  in_specs=[pl.BlockSpec((t, 1), lambda i: (i, 0)), pl.BlockSpec((4, LANES), lambda i: (0, 0))],
                  out_specs=[row] * 4, out_shape=[jax.ShapeDtypeStruct((S, LANES), F32)] * 4,
                  compiler_params=_params(("parallel",)))(pos, inv_sign)


def _rope_consts():
    half_p, half_h = MLA_ROPE // 2, LANES // 2
    inv_p = 1.0 / (ROPE_THETA ** (jnp.arange(0, MLA_ROPE, 2, dtype=F32) / MLA_ROPE))
    inv_h = 1.0 / (ROPE_THETA ** (jnp.arange(0, LANES, 2, dtype=F32) / LANES))
    z = jnp.zeros((half_p,), F32)
    o = jnp.ones((half_p,), F32)
    return jnp.stack([jnp.concatenate([inv_p, z, inv_p, z]), jnp.concatenate([-o, z, o, z]),
                      jnp.concatenate([inv_h, inv_h]), jnp.concatenate([-jnp.ones((half_h,), F32), jnp.ones((half_h,), F32)])])


def _shift_rows(x, s, up):
    n = x.shape[0]
    r = lax.broadcasted_iota(jnp.int32, x.shape, 0)
    if up:
        return jnp.where(r < n - s, pltpu.roll(x, n - s, 0), 0.0)
    return jnp.where(r >= s, pltpu.roll(x, s, 0), 0.0)


def _make_shift(s):
    @jax.custom_vjp
    def sh(x):
        return _shift_rows(x, s, False)

    sh.defvjp(lambda x: (sh(x), None), lambda _, g: (_shift_rows(g, s, True),))
    return sh


def _f_conv(x, w):
    y = x * w[DN_CONV - 1:DN_CONV]
    for j in range(DN_CONV - 1):
        y = y + _make_shift(DN_CONV - 1 - j)(x) * w[j:j + 1]
    return _silu(y)


def _make_conv(name):
    def call(x, w, dy=None):
        S, C = x.shape
        col = pl.BlockSpec((S, LANES), lambda i: (0, i))
        wsp = pl.BlockSpec((DN_CONV, LANES), lambda i: (0, i))
        if dy is None:
            def body(x_ref, w_ref, o_ref):
                o_ref[...] = _f_conv(x_ref[...], w_ref[...])
            return _pcall(body, name=f"{name}_fwd", grid=(C // LANES,), in_specs=[col, wsp], out_specs=col,
                          out_shape=jax.ShapeDtypeStruct(x.shape, F32), compiler_params=_params(("parallel",)))(x, w)

        def body(x_ref, w_ref, dy_ref, dx_ref, dw_ref):
            _, vjp = jax.vjp(_f_conv, x_ref[...], w_ref[...])
            dx_ref[...], dw_ref[...] = vjp(dy_ref[...])
        return _pcall(body, name=f"{name}_bwd", grid=(C // LANES,), in_specs=[col, wsp, col], out_specs=[col, wsp],
                      out_shape=[jax.ShapeDtypeStruct(x.shape, F32), jax.ShapeDtypeStruct(w.shape, F32)],
                      compiler_params=_params(("parallel",)))(x, w, dy)

    @jax.custom_vjp
    def op(x, w):
        return call(x, w)

    op.defvjp(lambda x, w: (op(x, w), (x, w)), lambda res, dy: tuple(call(*res, dy)))
    return op


def _dot_nt(a, b):
    return lax.dot_general(a.astype(BF16), b.astype(BF16), (((1,), (1,)), ((), ())), preferred_element_type=F32)


def _dot_nn(a, b):
    return jnp.dot(a.astype(BF16), b.astype(BF16), preferred_element_type=F32)


def _dot_tn(a, b):
    return lax.dot_general(a.astype(BF16), b.astype(BF16), (((0,), (0,)), ((), ())), preferred_element_type=F32)


def _mla_scores(qn_r, qp_r, kn_r, kp_r, diagonal):
    scale = MLA_QK ** -0.5
    s = _dot_nt(qn_r[...] * scale, kn_r[...]) + _dot_nt(qp_r[...] * scale, kp_r[...])
    if diagonal:
        r = lax.broadcasted_iota(jnp.int32, s.shape, 0)
        c = lax.broadcasted_iota(jnp.int32, s.shape, 1)
        s = jnp.where(c <= r, s, NEG)
    return s


def _on_causal_pairs(qi, ki, step):
    @pl.when(ki < qi)
    def _():
        step(False)

    @pl.when(ki == qi)
    def _():
        step(True)


def _make_mla_attn(name):
    scale = MLA_QK ** -0.5

    def fwd_call(qn, qp, kn, kp, v):
        S = qn.shape[0]
        t = min(S, 512)
        n = S // t
        qs = pl.BlockSpec((t, LANES), lambda h, i, j: (i, h))
        ks = pl.BlockSpec((t, LANES), lambda h, i, j: (jnp.minimum(j, i), h))

        def body(qn_r, qp_r, kn_r, kp_r, v_r, o_r, lse_r, m_s, l_s, acc_s):
            qi, ki = pl.program_id(1), pl.program_id(2)

            @pl.when(ki == 0)
            def _():
                m_s[...] = jnp.full_like(m_s, NEG)
                l_s[...] = jnp.zeros_like(l_s)
                acc_s[...] = jnp.zeros_like(acc_s)

            def step(diagonal):
                s = _mla_scores(qn_r, qp_r, kn_r, kp_r, diagonal)
                m_old = m_s[...]
                m_new = jnp.maximum(m_old, jnp.max(s, axis=-1, keepdims=True))
                p = jnp.exp(s - m_new[:, :1])
                alpha = jnp.exp(m_old - m_new)
                l_s[...] = alpha * l_s[...] + jnp.sum(p, axis=-1, keepdims=True)
                acc_s[...] = alpha * acc_s[...] + _dot_nn(p, v_r[...])
                m_s[...] = m_new
            _on_causal_pairs(qi, ki, step)

            @pl.when(ki == n - 1)
            def _():
                o_r[...] = acc_s[...] / l_s[...]
                lse_r[...] = m_s[...] + jnp.log(l_s[...])

        return _pcall(body, name=f"{name}_fwd", grid=(MLA_HEADS, n, n), in_specs=[qs, qs, ks, ks, ks], out_specs=[qs, qs],
                      out_shape=[jax.ShapeDtypeStruct((S, MLA_HEADS * LANES), F32)] * 2,
                      scratch_shapes=[pltpu.VMEM((t, LANES), F32)] * 3,
                      compiler_params=_params(("parallel", "parallel", "arbitrary")))(qn, qp, kn, kp, v)

    def dq_call(qn, qp, kn, kp, v, o, lse, do):
        S = qn.shape[0]
        t = min(S, 512)
        n = S // t
        qs = pl.BlockSpec((t, LANES), lambda h, i, j: (i, h))
        ks = pl.BlockSpec((t, LANES), lambda h, i, j: (jnp.minimum(j, i), h))

        def body(qn_r, qp_r, kn_r, kp_r, v_r, o_r, lse_r, do_r, dqn_r, dqp_r, dl_r, dqn_s, dqp_s):
            qi, ki = pl.program_id(1), pl.program_id(2)

            @pl.when(ki == 0)
            def _():
                dqn_s[...] = jnp.zeros_like(dqn_s)
                dqp_s[...] = jnp.zeros_like(dqp_s)
                dl_r[...] = jnp.broadcast_to(jnp.sum(do_r[...] * o_r[...], axis=-1, keepdims=True), dl_r.shape)

            def step(diagonal):
                p = jnp.exp(_mla_scores(qn_r, qp_r, kn_r, kp_r, diagonal) - lse_r[...][:, :1])
                ds = p * (_dot_nt(do_r[...], v_r[...]) - dl_r[...][:, :1])
                dqn_s[...] += _dot_nn(ds, kn_r[...])
                dqp_s[...] += _dot_nn(ds, kp_r[...])
            _on_causal_pairs(qi, ki, step)

            @pl.when(ki == n - 1)
            def _():
                dqn_r[...] = dqn_s[...] * scale
                dqp_r[...] = dqp_s[...] * scale

        return _pcall(body, name=f"{name}_dq", grid=(MLA_HEADS, n, n), in_specs=[qs, qs, ks, ks, ks, qs, qs, qs],
                      out_specs=[qs, qs, qs], out_shape=[jax.ShapeDtypeStruct((S, MLA_HEADS * LANES), F32)] * 3,
                      scratch_shapes=[pltpu.VMEM((t, LANES), F32)] * 2,
                      compiler_params=_params(("parallel", "parallel", "arbitrary")))(qn, qp, kn, kp, v, o, lse, do)

    def dkv_call(qn, qp, kn, kp, v, lse, do, dl):
        S = qn.shape[0]
        t = min(S, 512)
        n = S // t
        ks = pl.BlockSpec((t, LANES), lambda h, j, i: (j, h))
        qs = pl.BlockSpec((t, LANES), lambda h, j, i: (jnp.maximum(i, j), h))

        def body(qn_r, qp_r, kn_r, kp_r, v_r, lse_r, do_r, dl_r, dkn_r, dkp_r, dv_r, dkn_s, dkp_s, dv_s):
            ki, qi = pl.program_id(1), pl.program_id(2)

            @pl.when(qi == 0)
            def _():
                dkn_s[...] = jnp.zeros_like(dkn_s)
                dkp_s[...] = jnp.zeros_like(dkp_s)
                dv_s[...] = jnp.zeros_like(dv_s)

            def step(diagonal):
                p = jnp.exp(_mla_scores(qn_r, qp_r, kn_r, kp_r, diagonal) - lse_r[...][:, :1])
                ds = p * (_dot_nt(do_r[...], v_r[...]) - dl_r[...][:, :1])
                dv_s[...] += _dot_tn(p, do_r[...])
                dkn_s[...] += _dot_tn(ds, qn_r[...] * scale)
                dkp_s[...] += _dot_tn(ds, qp_r[...] * scale)
            _on_causal_pairs(qi, ki, step)

            @pl.when(qi == n - 1)
            def _():
                dkn_r[...] = dkn_s[...]
                dkp_r[...] = dkp_s[...]
                dv_r[...] = dv_s[...]

        return _pcall(body, name=f"{name}_dkv", grid=(MLA_HEADS, n, n), in_specs=[qs, qs, ks, ks, ks, qs, qs, qs],
                      out_specs=[ks, ks, ks], out_shape=[jax.ShapeDtypeStruct((S, MLA_HEADS * LANES), F32)] * 3,
                      scratch_shapes=[pltpu.VMEM((t, LANES), F32)] * 3,
                      compiler_params=_params(("parallel", "parallel", "arbitrary")))(qn, qp, kn, kp, v, lse, do, dl)

    @jax.custom_vjp
    def op(qn, qp, kn, kp, v):
        return fwd_call(qn, qp, kn, kp, v)[0]

    def fwd(qn, qp, kn, kp, v):
        o, lse = fwd_call(qn, qp, kn, kp, v)
        return o, (qn, qp, kn, kp, v, o, lse)

    def bwd(res, do):
        qn, qp, kn, kp, v, o, lse = res
        dqn, dqp, dl = dq_call(qn, qp, kn, kp, v, o, lse, do)
        dkn, dkp, dv = dkv_call(qn, qp, kn, kp, v, lse, do, dl)
        return dqn, dqp, dkn, dkp, dv

    op.defvjp(fwd, bwd)
    return op


def _dil_block(q, kp, kc, vp, vc, has_prev):
    scale = LANES ** -0.5
    r = lax.broadcasted_iota(jnp.int32, (DIL_BLOCK, DIL_BLOCK), 0)
    c = lax.broadcasted_iota(jnp.int32, (DIL_BLOCK, DIL_BLOCK), 1)
    s_c = jnp.where(c <= r, _dot_nt(q, kc) * scale, NEG)
    s_p = jnp.where((c >= r) & has_prev, _dot_nt(q, kp) * scale, NEG)
    m = jnp.maximum(jnp.max(s_c, axis=-1, keepdims=True), jnp.max(s_p, axis=-1, keepdims=True))
    e_c, e_p = jnp.exp(s_c - m), jnp.exp(s_p - m)
    den = jnp.sum(e_c, axis=-1, keepdims=True) + jnp.sum(e_p, axis=-1, keepdims=True)
    o = (_dot_nn(e_c, vc) + _dot_nn(e_p, vp)) / den
    return o, jnp.broadcast_to(m + jnp.log(den), o.shape)


DIL_TILE_ROWS = (1024, 1024, 2048)


def _make_dil_attn(name, d, tile_rows):
    def call(q, k, v, cts=None):
        S = q.shape[0]
        span = DIL_BLOCK * d
        G = max(1, min(tile_rows, S) // span)
        n = S // (G * span)
        at = (lambda i: i) if cts is None else (lambda i: n - 1 - i)
        tile = pl.BlockSpec((G * span, LANES), lambda h, i: (at(i), h))
        before = pl.BlockSpec((span, LANES), lambda h, i: (jnp.maximum(at(i) * G - 1, 0), h))

        def rows(r, j):
            return pl.ds(j * DIL_BLOCK, DIL_BLOCK) if d == 1 else pl.ds(r + j * span, DIL_BLOCK, stride=d)

        def over_residues(fn):
            if d == 1:
                fn(0)
            else:
                lax.fori_loop(0, d, lambda r, c: (fn(r), c)[1], 0)

        def block_inputs(r, j, q_r, kb_r, k_r, vb_r, v_r):
            kp = kb_r[rows(r, 0), :] if j == 0 else k_r[rows(r, j - 1), :]
            vp = vb_r[rows(r, 0), :] if j == 0 else v_r[rows(r, j - 1), :]
            return q_r[rows(r, j), :], kp, k_r[rows(r, j), :], vp, v_r[rows(r, j), :]

        if cts is None:
            def body(q_r, kb_r, k_r, vb_r, v_r, o_r, lse_r):
                first = at(pl.program_id(1)) * G

                def residue(r):
                    for j in range(G):
                        o_r[rows(r, j), :], lse_r[rows(r, j), :] = _dil_block(
                            *block_inputs(r, j, q_r, kb_r, k_r, vb_r, v_r), first + j > 0)
                over_residues(residue)
            return _pcall(body, name=f"{name}_fwd", grid=(DIL_GROUP_HEADS, n), in_specs=[tile, before, tile, before, tile],
                          out_specs=[tile, tile], out_shape=[jax.ShapeDtypeStruct(q.shape, F32)] * 2,
                          compiler_params=_params(("parallel", "parallel")))(q, k, k, v, v)

        def body(q_r, kb_r, k_r, vb_r, v_r, do_r, dl_r, dq_r, dk_r, dv_r, ck_s, cv_s):
            first = at(pl.program_id(1)) * G

            @pl.when(pl.program_id(1) == 0)
            def _():
                ck_s[...] = jnp.zeros_like(ck_s)
                cv_s[...] = jnp.zeros_like(cv_s)

            def residue(r):
                owed = None
                for j in range(G):
                    hp = first + j > 0
                    _, vjp = jax.vjp(lambda *a: _dil_block(*a, hp), *block_inputs(r, j, q_r, kb_r, k_r, vb_r, v_r))
                    dq, dkp, dkc, dvp, dvc = vjp((do_r[rows(r, j), :], dl_r[rows(r, j), :]))
                    dq_r[rows(r, j), :] = dq
                    if j == G - 1:
                        dkc, dvc = dkc + ck_s[rows(r, 0), :], dvc + cv_s[rows(r, 0), :]
                    dk_r[rows(r, j), :], dv_r[rows(r, j), :] = dkc, dvc
                    if j == 0:
                        owed = (dkp, dvp)
                    else:
                        dk_r[rows(r, j - 1), :] += dkp
                        dv_r[rows(r, j - 1), :] += dvp
                ck_s[rows(r, 0), :], cv_s[rows(r, 0), :] = owed
            over_residues(residue)
        return _pcall(body, name=f"{name}_bwd", grid=(DIL_GROUP_HEADS, n), in_specs=[tile, before, tile, before, tile, tile, tile],
                      out_specs=[tile] * 3, out_shape=[jax.ShapeDtypeStruct(q.shape, F32)] * 3,
                      scratch_shapes=[pltpu.VMEM((span, LANES), F32)] * 2,
                      compiler_params=_params(("parallel", "arbitrary")))(q, k, k, v, v, *cts)

    @jax.custom_vjp
    def op(q, k, v):
        return tuple(call(q, k, v))

    op.defvjp(lambda q, k, v: (op(q, k, v), (q, k, v)), lambda res, cts: tuple(call(*res, cts=cts)))
    return op


def _pdot(a, b, dims):
    return lax.dot_general(a, b, (dims, ((), ())), precision=lax.Precision.HIGH, preferred_element_type=F32)


DN_LOCAL_CHUNKS = 4


def _dn_local(q, k, v, g, b):
    C = DN_CHUNK
    r = lax.broadcasted_iota(jnp.int32, (C, C), 0)
    c = lax.broadcasted_iota(jnp.int32, (C, C), 1)
    incl, strict = r >= c, r > c
    eye = (r == c).astype(F32)
    avg = jnp.full((C, LANES), 1.0 / LANES, F32)
    gc_all = _pdot(incl.astype(F32), g, ((1,), (0,)))
    pad = jnp.zeros((C, LANES - C), F32)
    us, ws, qes, kds, qks, egs = [], [], [], [], [], []
    for h in range(DN_HEADS):
        sl = slice(h * LANES, (h + 1) * LANES)
        qh, kh, vh, bh, gc = q[:, sl], k[:, sl], v[:, sl], b[:, sl], gc_all[:, sl]
        gc_j = _pdot(avg, gc, ((1,), (1,)))
        decay = jnp.exp(jnp.where(incl, gc[:, :C] - gc_j, NEG))
        kb = kh * bh
        kk = _pdot(jnp.concatenate([kb, qh], axis=0), kh, ((1,), (1,)))
        a = jnp.where(strict, kk[:C] * decay, 0.0)
        inv, pw = eye - a, a
        for _ in range(5):
            pw = _pdot(pw, pw, ((1,), (0,)))
            inv = inv + _pdot(inv, pw, ((1,), (0,)))
        eg = jnp.exp(gc)
        uw = _pdot(inv, jnp.concatenate([vh * bh, kb * eg], axis=1), ((1,), (0,)))
        us.append(uw[:, :LANES])
        ws.append(uw[:, LANES:])
        qes.append(qh * eg)
        g_last = gc[C - 1:C, :]
        kds.append(kh * jnp.exp(g_last - gc))
        qks.append(jnp.concatenate([kk[C:] * decay, pad], axis=1))
        egs.append(jnp.broadcast_to(jnp.exp(g_last), (8, LANES)))
    return _cat(us), _cat(ws), _cat(qes), _cat(kds), _cat(qks), _cat(egs)


def _dn_scan(u, w, qe, kd, qk, egl, state):
    C = DN_CHUNK
    outs, states = [], []
    for h in range(DN_HEADS):
        sl = slice(h * LANES, (h + 1) * LANES)
        sh = state[sl, :]
        ws = _pdot(jnp.concatenate([w[:, sl], qe[:, sl]], axis=0), sh, ((1,), (0,)))
        v_new = u[:, sl] - ws[:C]
        outs.append(ws[C:] + _pdot(qk[:, sl][:, :C], v_new, ((1,), (0,))))
        states.append(sh * egl[0:1, sl] + _pdot(kd[:, sl], v_new, ((0,), (0,))))
    return _cat(outs), jnp.concatenate(states, axis=0)


def _make_delta_rule(name):
    W = DN_HEADS * LANES

    def local_call(ins, cts=None):
        S = ins[0].shape[0]
        n = S // DN_CHUNK
        per = math.gcd(DN_LOCAL_CHUNKS, n)
        row = pl.BlockSpec((per * DN_CHUNK, W), lambda i: (i, 0))
        eg = pl.BlockSpec((per, 8, W), lambda i: (i, 0, 0))
        rows = lambda j: slice(j * DN_CHUNK, (j + 1) * DN_CHUNK)

        if cts is None:
            def body(*refs):
                for j in range(per):
                    res = _dn_local(*[r[rows(j), :] for r in refs[:5]])
                    for o_r, o in zip(refs[5:10], res[:5]):
                        o_r[rows(j), :] = o
                    refs[10][j] = res[5]
            return _pcall(body, name=f"{name}_local_fwd", grid=(n // per,), in_specs=[row] * 5, out_specs=[row] * 5 + [eg],
                          out_shape=[jax.ShapeDtypeStruct((S, W), F32)] * 5 + [jax.ShapeDtypeStruct((n, 8, W), F32)],
                          compiler_params=_params(("parallel",)))(*ins)

        def body(*refs):
            for j in range(per):
                _, vjp = jax.vjp(_dn_local, *[r[rows(j), :] for r in refs[:5]])
                grads = vjp(tuple(r[rows(j), :] for r in refs[5:10]) + (refs[10][j],))
                for o_r, o in zip(refs[11:], grads):
                    o_r[rows(j), :] = o
        return _pcall(body, name=f"{name}_local_bwd", grid=(n // per,), in_specs=[row] * 10 + [eg], out_specs=[row] * 5,
                      out_shape=[jax.ShapeDtypeStruct((S, W), F32)] * 5, compiler_params=_params(("parallel",)))(*ins, *cts)

    def scan_call(ins, saved=None, do=None):
        S = ins[0].shape[0]
        n = S // DN_CHUNK
        at = (lambda i: i) if do is None else (lambda i: n - 1 - i)
        row = pl.BlockSpec((DN_CHUNK, W), lambda i: (at(i), 0))
        eg = pl.BlockSpec((None, 8, W), lambda i: (at(i), 0, 0))
        st = pl.BlockSpec((None, W, LANES), lambda i: (at(i), 0, 0))

        if do is None:
            def body(*refs):
                o_r, st_r, s_s = refs[6:]

                @pl.when(pl.program_id(0) == 0)
                def _():
                    s_s[...] = jnp.zeros_like(s_s)
                st_r[...] = s_s[...]
                o_r[...], s_s[...] = _dn_scan(*[r[...] for r in refs[:6]], s_s[...])
            return _pcall(body, name=f"{name}_scan_fwd", grid=(n,), in_specs=[row] * 5 + [eg], out_specs=[row, st],
                          out_shape=[jax.ShapeDtypeStruct((S, W), F32), jax.ShapeDtypeStruct((n, W, LANES), F32)],
                          scratch_shapes=[pltpu.VMEM((W, LANES), F32)], compiler_params=_params(("arbitrary",)))(*ins)

        def body(*refs):
            st_r, do_r = refs[6:8]
            outs, ds_s = refs[8:14], refs[14]

            @pl.when(pl.program_id(0) == 0)
            def _():
                ds_s[...] = jnp.zeros_like(ds_s)
            _, vjp = jax.vjp(_dn_scan, *[r[...] for r in refs[:6]], st_r[...])
            *grads, ds = vjp((do_r[...], ds_s[...]))
            for o_r, gval in zip(outs, grads):
                o_r[...] = gval
            ds_s[...] = ds
        return _pcall(body, name=f"{name}_scan_bwd", grid=(n,), in_specs=[row] * 5 + [eg, st, row], out_specs=[row] * 5 + [eg],
                      out_shape=[jax.ShapeDtypeStruct((S, W), F32)] * 5 + [jax.ShapeDtypeStruct((n, 8, W), F32)],
                      scratch_shapes=[pltpu.VMEM((W, LANES), F32)], compiler_params=_params(("arbitrary",)))(*ins, saved, do)

    @jax.custom_vjp
    def local(q, k, v, g, b):
        return tuple(local_call((q, k, v, g, b)))

    local.defvjp(lambda *a: (local(*a), a), lambda res, cts: tuple(local_call(res, tuple(cts))))

    @jax.custom_vjp
    def scan(u, w, qe, kd, qk, egl):
        return scan_call((u, w, qe, kd, qk, egl))[0]

    def scan_fwd(*a):
        o, states = scan_call(a)
        return o, (a, states)

    scan.defvjp(scan_fwd, lambda res, do: tuple(scan_call(res[0], res[1], do)))
    return lambda q, k, v, g, b: scan(*local(q, k, v, g, b))


def _loss_call(y, target):
    S, D = y.shape
    t = min(S, 512)
    n = S // t
    row = pl.BlockSpec((t, D), lambda i: (i, 0))

    def body(y_r, t_r, loss_r, dy_r, acc_s):
        i = pl.program_id(0)

        @pl.when(i == 0)
        def _():
            acc_s[...] = jnp.zeros_like(acc_s)
        err = y_r[...] - t_r[...]
        dy_r[...] = err * (1.0 / D)
        acc_s[...] += jnp.sum(err * err, axis=0, keepdims=True)

        @pl.when(i == n - 1)
        def _():
            loss_r[...] = jnp.broadcast_to(jnp.sum(acc_s[...], axis=1, keepdims=True) * (0.5 / D), loss_r.shape)

    return _pcall(body, name="loss_head", grid=(n,), in_specs=[row, row],
                  out_specs=[pl.BlockSpec((8, LANES), lambda i: (0, 0)), row],
                  out_shape=[jax.ShapeDtypeStruct((8, LANES), F32), jax.ShapeDtypeStruct((S, D), F32)],
                  scratch_shapes=[pltpu.VMEM((1, D), F32)], compiler_params=_params(("arbitrary",)))(y, target)


def _adamw_call(name, parts, w, m, v, rows=128):
    L, R, C = w.shape
    t = _tile(R, rows, 8)
    row = pl.BlockSpec((None, t, C), lambda l, i: (l, i, 0))

    def body(p_r, w_r, m_r, v_r, g_r, d_r, nm_r, nv_r):
        g = p_r[0].astype(F32)
        for s in range(1, N_DEV):
            g = g + p_r[s].astype(F32)
        m_new = ADAM_B1 * m_r[...] + (1.0 - ADAM_B1) * g
        v_new = ADAM_B2 * v_r[...] + (1.0 - ADAM_B2) * (g * g)
        m_hat = m_new / (1.0 - ADAM_B1 ** ADAM_STEP)
        v_hat = v_new / (1.0 - ADAM_B2 ** ADAM_STEP)
        g_r[...] = g
        d_r[...] = -ADAM_LR * (m_hat / (jnp.sqrt(v_hat) + ADAM_EPS) + ADAM_WD * w_r[...])
        nm_r[...] = m_new
        nv_r[...] = v_new

    return _pcall(body, name=name, grid=(L, R // t),
                  in_specs=[pl.BlockSpec((N_DEV, None, t, C), lambda l, i: (0, l, i, 0)), row, row, row],
                  out_specs=[row] * 4, out_shape=[jax.ShapeDtypeStruct((L, R, C), F32)] * 4,
                  compiler_params=_params(("parallel", "parallel")))(parts, w, m, v)


def _my_place():
    x, y, c = lax.axis_index("x"), lax.axis_index("y"), lax.axis_index("c")
    return x, y, c


def _index(x, y, c):
    return 4 * x + 2 * y + c


def _all_gather(vs):
    n = len(vs)

    def body(*refs):
        v_refs, out_refs = refs[:n], refs[n:2 * n]
        send_sems, recv_sems, local_sems = refs[2 * n:]
        x, y, c = _my_place()
        me, sibling = (x, y, c), (x, y, 1 - c)
        chips = [(1 - x, y), (x, 1 - y), (1 - x, 1 - y)]

        def copy(a, k, block, to, src=None):
            rows = out_refs[a].at[_index(*block)]
            return pltpu.make_async_remote_copy(src_ref=rows if src is None else src, dst_ref=rows, send_sem=send_sems.at[a, k],
                                                recv_sem=recv_sems.at[a, k], device_id=to, device_id_type=MESH)

        mine = [pltpu.make_async_copy(v_refs[a], out_refs[a].at[_index(*me)], local_sems.at[a]) for a in range(n)]
        first, passed = [], []
        for a in range(n):
            mine[a].start()
            first += [copy(a, 0, me, sibling, src=v_refs[a])]
            first += [copy(a, 1 + j, me, (*chip, c), src=v_refs[a]) for j, chip in enumerate(chips)]
        for cp in first:
            cp.start()
        for j, chip in enumerate(chips):
            for a in range(n):
                copy(a, 1 + j, (*chip, c), me).wait_recv()
                passed.append(copy(a, 4 + j, (*chip, c), sibling))
                passed[-1].start()
        for a in range(n):
            copy(a, 0, sibling, me).wait_recv()
            for j, chip in enumerate(chips):
                copy(a, 4 + j, (*chip, 1 - c), me).wait_recv()
        for cp in first + passed:
            cp.wait_send()
        for a in range(n):
            mine[a].wait()

    any_ = pl.BlockSpec(memory_space=pl.ANY)
    return _pcall(body, name="gather_weights", in_specs=[any_] * n, out_specs=[any_] * n,
                  out_shape=[jax.ShapeDtypeStruct((N_DEV,) + v.shape, v.dtype) for v in vs],
                  scratch_shapes=[pltpu.SemaphoreType.DMA((n, 7)), pltpu.SemaphoreType.DMA((n, 7)), pltpu.SemaphoreType.DMA((n,))])(*vs)


def _all_to_all(vs):
    n = len(vs)

    def body(*refs):
        v_refs, out_refs = refs[:n], refs[n:2 * n]
        send_sems, recv_sems, local_sems = refs[2 * n:]
        x, y, c = _my_place()
        me = _index(x, y, c)
        mine = [pltpu.make_async_copy(v_refs[a].at[me], out_refs[a].at[me], local_sems.at[a]) for a in range(n)]
        copies = []
        for a in range(n):
            mine[a].start()
        for k in range(1, N_DEV):
            px = 1 - x if k & 4 else x
            py = 1 - y if k & 2 else y
            pc = 1 - c if k & 1 else c
            for a in range(n):
                cp = pltpu.make_async_remote_copy(src_ref=v_refs[a].at[_index(px, py, pc)], dst_ref=out_refs[a].at[me],
                                                  send_sem=send_sems.at[a, k - 1], recv_sem=recv_sems.at[a, k - 1],
                                                  device_id=(px, py, pc), device_id_type=MESH)
                cp.start()
                copies.append(cp)
        for cp in copies:
            cp.wait()
        for a in range(n):
            mine[a].wait()

    any_ = pl.BlockSpec(memory_space=pl.ANY)
    return _pcall(body, name="exchange_grads", in_specs=[any_] * n, out_specs=[any_] * n,
                  out_shape=[jax.ShapeDtypeStruct(v.shape, v.dtype) for v in vs],
                  scratch_shapes=[pltpu.SemaphoreType.DMA((n, 7)), pltpu.SemaphoreType.DMA((n, 7)), pltpu.SemaphoreType.DMA((n,))])(*vs)


W_IN_SHARD = IN_WIDTH // N_DEV
SEG_ORDER = ("q_lat", "c_kv", "k_pe", "z_a", "dn_qkv", "dn_ab", "z_b", "dil_qkv", "z_c", "gate")
SEG_WIDTH = (384, 256, LANES, 512, 1536, LANES, 512, 4608, 512, 3072)


def _w_in_plan():
    plan = []

    def add(seg, c0, c1, dst):
        while c0 < c1:
            d = c0 // W_IN_SHARD
            e = min(c1, (d + 1) * W_IN_SHARD)
            plan.append((seg, dst, d, c0 - d * W_IN_SHARD, e - c0))
            dst += e - c0
            c0 = e

    half = MLA_ROPE // 2
    for i, name in enumerate(SEG_ORDER):
        if name == "k_pe":
            o = _SEG["k_pe"][0]
            add(i, o, o + half, 0)
            add(i, o + half, o + 2 * half, LANES // 2)
        elif name == "dn_ab":
            o = _SEG["dn_a"][0]
            add(i, o, o + 2 * DN_HEADS, 0)
        else:
            o, w = _SEG[name]
            add(i, o, o + w, 0)
    return plan


def _make_w_in_segments(name):
    plan = _w_in_plan()
    nseg = len(SEG_ORDER)
    t = 256

    def fwd_call(g):
        L = g.shape[1]

        def body(g_ref, *o_refs):
            for i in (SEG_ORDER.index("k_pe"), SEG_ORDER.index("dn_ab")):
                o_refs[i][...] = jnp.zeros_like(o_refs[i])
            for seg, dst, d, src, n in plan:
                o_refs[seg][:, dst:dst + n] = g_ref[d, :, src:src + n]

        return _pcall(body, name=f"{name}_fwd", grid=(L, D_MODEL // t),
                      in_specs=[pl.BlockSpec((N_DEV, None, t, W_IN_SHARD), lambda l, i: (0, l, i, 0))],
                      out_specs=[pl.BlockSpec((None, t, w), lambda l, i: (l, i, 0)) for w in SEG_WIDTH],
                      out_shape=[jax.ShapeDtypeStruct((L, D_MODEL, w), g.dtype) for w in SEG_WIDTH],
                      compiler_params=_params(("parallel", "parallel")))(g)

    def bwd_call(ds):
        L = ds[0].shape[0]

        def body(*refs):
            d_refs, g_ref = refs[:nseg], refs[nseg]
            for seg, dst, d, src, n in plan:
                g_ref[d, :, src:src + n] = d_refs[seg][:, dst:dst + n]

        return _pcall(body, name=f"{name}_bwd", grid=(L, D_MODEL // t),
                      in_specs=[pl.BlockSpec((None, t, w), lambda l, i: (l, i, 0)) for w in SEG_WIDTH],
                      out_specs=pl.BlockSpec((N_DEV, None, t, W_IN_SHARD), lambda l, i: (0, l, i, 0)),
                      out_shape=jax.ShapeDtypeStruct((N_DEV, L, D_MODEL, W_IN_SHARD), ds[0].dtype),
                      compiler_params=_params(("parallel", "parallel")))(*ds)

    @jax.custom_vjp
    def op(g):
        return tuple(fwd_call(g))

    op.defvjp(lambda g: (op(g), None), lambda _, ds: (bwd_call(tuple(ds)),))
    return op


def _pe_pad(a):
    h = MLA_ROPE // 2
    z = jnp.zeros(a.shape[:-1] + (h,), a.dtype)
    return jnp.concatenate([a[..., :h], z, a[..., h:], z], axis=-1)


def _layer(tag, x, tables, W):
    cos_p, sin_p, cos_h, sin_h = tables
    w_segs = W["w_in_segments"]
    row = lambda a: a[None, :]

    (h,) = _make_rowwise(f"{tag}_norm", _f_norm, 512)((x,), (), (row(W["norm_g"]),), ())
    q_lat, c_kv, kpe, z_a, dn_qkv, ab, z_b, dil_qkv, z_c, gl = _make_multi_linear(f"{tag}_inproj", 10)(h, w_segs)

    qn_lat, ckvn, kp = _make_rowwise(f"{tag}_mla_a", _f_mla_a, 512)(
        (q_lat, c_kv, kpe), (cos_p, sin_p),
        (row(W["mla_q_a_norm_g"]), row(W["mla_kv_a_norm_g"]), row(_pe_pad(W["mla_k_norm_g"][LANES:]))), ())
    wq = W["mla_w_q_b"].reshape(MLA_Q_RANK, MLA_HEADS, MLA_QK)
    wq = jnp.concatenate([wq[:, :, :LANES].reshape(MLA_Q_RANK, -1), _pe_pad(wq[:, :, LANES:]).reshape(MLA_Q_RANK, -1)], axis=1)
    wkv = W["mla_w_kv_b"].reshape(MLA_KV_RANK, MLA_HEADS, 2 * LANES)
    (q8,) = _make_multi_linear(f"{tag}_qb", 1)(qn_lat, (wq,))
    kn_raw, v_mla = _make_multi_linear(f"{tag}_kvb", 2)(
        ckvn, (wkv[:, :, :LANES].reshape(MLA_KV_RANK, -1), wkv[:, :, LANES:].reshape(MLA_KV_RANK, -1)))
    qn, qp, kn = _make_rowwise(f"{tag}_mla_b", _f_mla_b, 512)(
        (q8, kn_raw), (cos_p, sin_p),
        (row(W["mla_q_norm_g"][:LANES]), row(_pe_pad(W["mla_q_norm_g"][LANES:])), row(W["mla_k_norm_g"][:LANES])), ())
    y_a = _make_mla_attn(f"{tag}_mla")(qn, qp, kn, kp, v_mla)

    mixed = _make_conv(f"{tag}_conv")(dn_qkv, W["dn_conv_w"])
    lane_head = jnp.arange(DN_HEADS * LANES) // LANES
    e_a = (jnp.arange(LANES)[:, None] == lane_head[None, :]).astype(F32)
    e_b = (jnp.arange(LANES)[:, None] == lane_head[None, :] + DN_HEADS).astype(F32)
    q_dn, k_dn, v_dn, g_dn, b_dn = _make_rowwise(f"{tag}_dn_pre", _f_dn_pre, 512)(
        (mixed, ab), (), (row(jnp.repeat(W["dn_a_log"], LANES)), row(jnp.repeat(W["dn_dt_bias"], LANES))), (e_a, e_b))
    o_dn = _make_delta_rule(f"{tag}_dn")(q_dn, k_dn, v_dn, g_dn, b_dn)

    qkv_dil = _make_rowwise(f"{tag}_dil_pre", _f_dil_pre, 256)(
        (dil_qkv,), (cos_h, sin_h), (row(W["dil_q_norm_g"]), row(W["dil_k_norm_g"])), ())
    n_groups = len(DIL_DILATIONS)
    o_lse = [_make_dil_attn(f"{tag}_dil{g}", d, DIL_TILE_ROWS[g])(qkv_dil[g], qkv_dil[n_groups + g], qkv_dil[2 * n_groups + g])
             for g, d in enumerate(DIL_DILATIONS)]

    ya, yb, yc = _make_rowwise(f"{tag}_merge_a", _f_merge_a, 256)(
        (y_a, z_a, o_dn, z_b, *[o for o, _ in o_lse], *[l for _, l in o_lse], z_c), (), (row(W["dn_out_norm_g"]),), ())
    (b0,) = _make_multi_linear(f"{tag}_br0", 1)(ya, (W["w_branch"][0],))
    (b1,) = _make_multi_linear(f"{tag}_br1", 1)(yb, (W["w_branch"][1],))
    (b2,) = _make_multi_linear(f"{tag}_br2", 1)(yc, (W["w_branch"][2],))
    (mix,) = _make_rowwise(f"{tag}_merge_b", _f_merge_b, 256)((b0, b1, b2, gl), (), (), ())
    return _make_resid_linear(f"{tag}_out")(x, mix, W["w_out"])


SHARDED = (("w_in", (D_MODEL, W_IN_SHARD)), ("mla_w_q_b", (MLA_Q_RANK, MLA_HEADS * MLA_QK // N_DEV)),
           ("mla_w_kv_b", (MLA_KV_RANK, MLA_HEADS * 2 * LANES // N_DEV)), ("w_branch", (3 * BRANCH_W, D_MODEL // N_DEV)),
           ("w_out", (D_MODEL // N_DEV, D_MODEL)), ("dn_conv_w", (DN_CONV, 3 * DN_HEADS * LANES // N_DEV)))
SMALL = (("norm_g", D_MODEL), ("mla_q_a_norm_g", MLA_Q_RANK), ("mla_kv_a_norm_g", MLA_KV_RANK), ("mla_q_norm_g", MLA_QK),
         ("mla_k_norm_g", MLA_QK), ("dn_a_log", DN_HEADS), ("dn_dt_bias", DN_HEADS), ("dn_out_norm_g", LANES),
         ("dil_q_norm_g", LANES), ("dil_k_norm_g", LANES))
WEIGHTS = ("norm_g", "w_in", "mla_q_a_norm_g", "mla_w_q_b", "mla_kv_a_norm_g", "mla_w_kv_b", "mla_q_norm_g", "mla_k_norm_g",
           "dn_conv_w", "dn_a_log", "dn_dt_bias", "dn_out_norm_g", "dil_q_norm_g", "dil_k_norm_g", "w_branch", "w_out")


def _round_up(n, m):
    return -(-n // m) * m


def _pack_vectors(pieces):
    return jnp.concatenate([jnp.pad(p, (0, _round_up(p.shape[0], LANES) - p.shape[0])) for p in pieces]).reshape(-1, LANES)


def _unpack_vectors(flat, sizes):
    out, off = [], 0
    flat = flat.reshape(-1)
    for n in sizes:
        out.append(flat[off:off + n])
        off += _round_up(n, LANES)
    return out


def _whole_weights(l, segs, g, small):
    W = dict(small)
    W["w_in_segments"] = tuple(s[l] for s in segs)
    W["mla_w_q_b"] = g["mla_w_q_b"][:, l].transpose(1, 0, 2).reshape(MLA_Q_RANK, -1)
    W["mla_w_kv_b"] = g["mla_w_kv_b"][:, l].transpose(1, 0, 2).reshape(MLA_KV_RANK, -1)
    W["w_branch"] = g["w_branch"][:, l].reshape(N_DEV, 3, BRANCH_W, -1).transpose(1, 2, 0, 3).reshape(3, BRANCH_W, D_MODEL)
    W["w_out"] = g["w_out"][:, l].reshape(D_MODEL, D_MODEL)
    W["dn_conv_w"] = g["dn_conv_w"][:, l].transpose(1, 0, 2).reshape(DN_CONV, -1)
    return W


def kernel(x, positions, norm_g, w_in, mla_q_a_norm_g, mla_w_q_b, mla_kv_a_norm_g, mla_w_kv_b, mla_q_norm_g, mla_k_norm_g, dn_conv_w, dn_a_log, dn_dt_bias, dn_out_norm_g, dil_q_norm_g, dil_k_norm_g, w_branch, w_out, loss_target, m_norm_g, m_w_in, m_mla_q_a_norm_g, m_mla_w_q_b, m_mla_kv_a_norm_g, m_mla_w_kv_b, m_mla_q_norm_g, m_mla_k_norm_g, m_dn_conv_w, m_dn_a_log, m_dn_dt_bias, m_dn_out_norm_g, m_dil_q_norm_g, m_dil_k_norm_g, m_w_branch, m_w_out, v_norm_g, v_w_in, v_mla_q_a_norm_g, v_mla_w_q_b, v_mla_kv_a_norm_g, v_mla_w_kv_b, v_mla_q_norm_g, v_mla_k_norm_g, v_dn_conv_w, v_dn_a_log, v_dn_dt_bias, v_dn_out_norm_g, v_dil_q_norm_g, v_dil_k_norm_g, v_w_branch, v_w_out):
    w = dict(norm_g=norm_g, w_in=w_in, mla_q_a_norm_g=mla_q_a_norm_g, mla_w_q_b=mla_w_q_b, mla_kv_a_norm_g=mla_kv_a_norm_g,
             mla_w_kv_b=mla_w_kv_b, mla_q_norm_g=mla_q_norm_g, mla_k_norm_g=mla_k_norm_g, dn_conv_w=dn_conv_w, dn_a_log=dn_a_log,
             dn_dt_bias=dn_dt_bias, dn_out_norm_g=dn_out_norm_g, dil_q_norm_g=dil_q_norm_g, dil_k_norm_g=dil_k_norm_g,
             w_branch=w_branch, w_out=w_out)
    m = dict(norm_g=m_norm_g, w_in=m_w_in, mla_q_a_norm_g=m_mla_q_a_norm_g, mla_w_q_b=m_mla_w_q_b, mla_kv_a_norm_g=m_mla_kv_a_norm_g,
             mla_w_kv_b=m_mla_w_kv_b, mla_q_norm_g=m_mla_q_norm_g, mla_k_norm_g=m_mla_k_norm_g, dn_conv_w=m_dn_conv_w,
             dn_a_log=m_dn_a_log, dn_dt_bias=m_dn_dt_bias, dn_out_norm_g=m_dn_out_norm_g, dil_q_norm_g=m_dil_q_norm_g,
             dil_k_norm_g=m_dil_k_norm_g, w_branch=m_w_branch, w_out=m_w_out)
    v = dict(norm_g=v_norm_g, w_in=v_w_in, mla_q_a_norm_g=v_mla_q_a_norm_g, mla_w_q_b=v_mla_w_q_b, mla_kv_a_norm_g=v_mla_kv_a_norm_g,
             mla_w_kv_b=v_mla_w_kv_b, mla_q_norm_g=v_mla_q_norm_g, mla_k_norm_g=v_mla_k_norm_g, dn_conv_w=v_dn_conv_w,
             dn_a_log=v_dn_a_log, dn_dt_bias=v_dn_dt_bias, dn_out_norm_g=v_dn_out_norm_g, dil_q_norm_g=v_dil_q_norm_g,
             dil_k_norm_g=v_dil_k_norm_g, w_branch=v_w_branch, w_out=v_w_out)
    x2, target = x[0], loss_target[0]
    pos = positions[0][:, None]

    view = lambda t, n, s: t[n].reshape((DEPTH,) + s)
    shards = [view(w, n, s) if n == "dn_conv_w" else view(w, n, s).astype(BF16) for n, s in SHARDED]
    gathered = dict(zip([n for n, _ in SHARDED], _all_gather(shards)))
    small = [{n: w[n][l] for n, _ in SMALL} for l in range(DEPTH)]
    tables = _rope_tables(pos, _rope_consts())

    def forward(gathered, small, x2):
        segs = _make_w_in_segments("w_in_segments")(gathered["w_in"])
        y = x2
        for l in range(DEPTH):
            y = _layer(f"l{l}", y, tables, _whole_weights(l, segs, gathered, small[l]))
        return y

    y, vjp = jax.vjp(forward, gathered, small, x2)
    loss_splat, dy = _loss_call(y, target)
    g_gathered, g_small, g_x = vjp(dy)
    loss = lax.psum(loss_splat[0, 0], ("x", "y", "c"))

    sizes = [k for _ in range(DEPTH) for _, k in SMALL]
    g_vec = _pack_vectors([g_small[l][n] for l in range(DEPTH) for n, _ in SMALL])
    sent = [g_gathered[n] for n, _ in SHARDED] + [jnp.broadcast_to(g_vec[None, None], (N_DEV, 1) + g_vec.shape)]
    parts = _all_to_all(sent)

    vec = lambda t: _pack_vectors([t[n][l] for l in range(DEPTH) for n, _ in SMALL])[None]
    outs = {}
    for (n, s), p in zip(SHARDED, parts[:-1]):
        outs[n] = [o.reshape(w[n].shape) for o in _adamw_call(f"adamw_{n}", p, view(w, n, s), view(m, n, s), view(v, n, s))]
    vec_outs = [_unpack_vectors(o, sizes) for o in _adamw_call("adamw_vectors", parts[-1], vec(w), vec(m), vec(v))]
    for i, (n, _) in enumerate(SMALL):
        outs[n] = [jnp.stack([o[l * len(SMALL) + i] for l in range(DEPTH)]) for o in vec_outs]
    return (loss, g_x[None], *[outs[n][k] for k in range(4) for n in WEIGHTS])
```

```python
import functools
import math

import jax
import jax.numpy as jnp
from jax import lax
from jax.experimental import pallas as pl
from jax.experimental.pallas import tpu as pltpu

F32 = jnp.float32
BF16 = jnp.bfloat16
HI = lax.Precision.HIGHEST
MESH = pl.DeviceIdType.MESH

N_DEV = 8
D_MODEL = 1024
DEPTH = 2
RMS_EPS = 1e-6
ROPE_THETA = 10000.0
LANES = 128
MLA_HEADS = 4
MLA_ROPE = 64
MLA_QK = 192
MLA_Q_RANK = 384
MLA_KV_RANK = 256
DN_HEADS = 4
DN_CHUNK = 64
DN_CONV = 4
DIL_HEADS = 12
DIL_GROUP_HEADS = 4
DIL_DILATIONS = (1, 4, 16)
DIL_BLOCK = 128
BRANCH_W = 512
IN_WIDTH = 11464
NEG = -1e30
VMEM_LIMIT = 56 * 1024 * 1024

ADAM_LR, ADAM_B1, ADAM_B2, ADAM_EPS, ADAM_WD, ADAM_STEP = 0.001, 0.9, 0.999, 1e-08, 0.01, 10

_SEG = {}
_off = 0
for _n, _w in (("q_lat", 384), ("c_kv", 256), ("k_pe", 64), ("z_a", 512), ("dn_qkv", 1536), ("dn_a", 4), ("dn_b", 4),
               ("z_b", 512), ("dil_qkv", 4608), ("z_c", 512), ("gate", 3072)):
    _SEG[_n] = (_off, _w)
    _off += _w
assert _off == IN_WIDTH


def _pcall(body, **kw):
    return pl.pallas_call(body, **kw)


def _params(sem=None):
    return pltpu.CompilerParams(dimension_semantics=sem, vmem_limit_bytes=VMEM_LIMIT)


def _tile(n, target, mult):
    t = (min(n, target) // mult) * mult
    while t >= mult:
        if n % t == 0:
            return t
        t -= mult
    return n


def _mm(name, a, b, mode, out_dtype=F32, acc=None, tm=1024, tn=512, tk=1024):
    if mode == "nn":
        (M, K), (_, N) = a.shape, b.shape
    elif mode == "nt":
        (M, K), (N, _) = a.shape, b.shape
    else:
        (K, M), (_, N) = a.shape, b.shape
    tm, tn, tk = _tile(M, tm, LANES), _tile(N, tn, LANES), _tile(K, tk, LANES)
    nk = K // tk
    dims = {"nn": (((1,), (0,)), ((), ())), "nt": (((1,), (1,)), ((), ())), "tn": (((0,), (0,)), ((), ()))}[mode]
    a_spec = pl.BlockSpec((tk, tm), lambda i, j, k: (k, i)) if mode == "tn" else pl.BlockSpec((tm, tk), lambda i, j, k: (i, k))
    b_spec = pl.BlockSpec((tn, tk), lambda i, j, k: (j, k)) if mode == "nt" else pl.BlockSpec((tk, tn), lambda i, j, k: (k, j))
    o_spec = pl.BlockSpec((tm, tn), lambda i, j, k: (i, j))
    has_acc = acc is not None

    def body(*refs):
        a_ref, b_ref = refs[:2]
        c_ref = refs[2] if has_acc else None
        o_ref = refs[3] if has_acc else refs[2]
        prod = lax.dot_general(a_ref[...].astype(BF16), b_ref[...].astype(BF16), dims, preferred_element_type=F32)
        if nk == 1:
            o_ref[...] = (prod + c_ref[...].astype(F32) if has_acc else prod).astype(out_dtype)
            return
        acc_ref = refs[-1]
        k = pl.program_id(2)

        @pl.when(k == 0)
        def _():
            acc_ref[...] = prod + c_ref[...].astype(F32) if has_acc else prod

        @pl.when(k > 0)
        def _():
            acc_ref[...] += prod

        @pl.when(k == nk - 1)
        def _():
            o_ref[...] = acc_ref[...].astype(out_dtype)

    ins = [a, b] + ([acc] if has_acc else [])
    in_specs = [a_spec, b_spec] + ([o_spec] if has_acc else [])
    return _pcall(body, name=name, grid=(M // tm, N // tn, nk), in_specs=in_specs, out_specs=o_spec,
                  out_shape=jax.ShapeDtypeStruct((M, N), out_dtype), scratch_shapes=[pltpu.VMEM((tm, tn), F32)] if nk > 1 else [],
                  compiler_params=_params(("parallel", "parallel", "arbitrary")))(*ins)


def _make_multi_linear(name, n):
    @jax.custom_vjp
    def op(h, ws):
        return tuple(_mm(f"{name}_fwd{i}", h, w, "nn") for i, w in enumerate(ws))

    def fwd(h, ws):
        return op(h, ws), (h, ws)

    def bwd(res, douts):
        h, ws = res
        dh = None
        for i, (w, d) in enumerate(zip(ws, douts)):
            dh = _mm(f"{name}_dh{i}", d, w, "nt", acc=dh, out_dtype=h.dtype if i == len(ws) - 1 else F32)
        dws = tuple(_mm(f"{name}_dw{i}", h, d, "tn", out_dtype=w.dtype) for i, (w, d) in enumerate(zip(ws, douts)))
        return dh, dws

    op.defvjp(fwd, bwd)
    return op


def _make_resid_linear(name):
    @jax.custom_vjp
    def op(x, a, w):
        return _mm(f"{name}_fwd", a, w, "nn", acc=x)

    def fwd(x, a, w):
        return op(x, a, w), (a, w)

    def bwd(res, dy):
        a, w = res
        return dy, _mm(f"{name}_da", dy, w, "nt", out_dtype=a.dtype), _mm(f"{name}_dw", a, dy, "tn", out_dtype=w.dtype)

    op.defvjp(fwd, bwd)
    return op


def _make_rowwise(name, f, tile):
    def specs(rows, aux, params, consts, t):
        row = [pl.BlockSpec((t, a.shape[1]), lambda i: (i, 0)) for a in (*rows, *aux)]
        full = [pl.BlockSpec(p.shape, lambda i: (0, 0)) for p in (*params, *consts)]
        return row, full

    def fwd_call(rows, aux, params, consts):
        S = rows[0].shape[0]
        t = min(tile, S)
        n_in = len(rows) + len(aux) + len(params) + len(consts)
        shp = lambda a: jax.ShapeDtypeStruct((t, a.shape[1]), a.dtype)
        outs = jax.eval_shape(f, *[shp(a) for a in (*rows, *aux)], *params, *consts)
        row_specs, full_specs = specs(rows, aux, params, consts, t)

        def body(*refs):
            res = f(*[r[...] for r in refs[:n_in]])
            for o_ref, o in zip(refs[n_in:], res):
                o_ref[...] = o

        return _pcall(body, name=f"{name}_fwd", grid=(S // t,), in_specs=row_specs + full_specs,
                      out_specs=[pl.BlockSpec((t, o.shape[1]), lambda i: (i, 0)) for o in outs],
                      out_shape=[jax.ShapeDtypeStruct((S, o.shape[1]), o.dtype) for o in outs],
                      compiler_params=_params(("parallel",)))(*rows, *aux, *params, *consts)

    def bwd_call(rows, aux, params, consts, douts):
        S = rows[0].shape[0]
        t = min(tile, S)
        nr, na, npar, nc, nd = len(rows), len(aux), len(params), len(consts), len(douts)
        row_specs, full_specs = specs(rows, aux, params, consts, t)

        def body(*refs):
            vals = [r[...] for r in refs[:nr + na + npar + nc]]
            rv, av = vals[:nr], vals[nr:nr + na]
            pv, cv = vals[nr + na:nr + na + npar], vals[nr + na + npar:]
            dv = tuple(r[...] for r in refs[nr + na + npar + nc:nr + na + npar + nc + nd])
            out_refs = refs[nr + na + npar + nc + nd:]
            _, vjp = jax.vjp(lambda *rp: f(*rp[:nr], *av, *rp[nr:], *cv), *rv, *pv)
            grads = vjp(dv)
            for o_ref, g in zip(out_refs[:nr], grads[:nr]):
                o_ref[...] = g
            first = pl.program_id(0) == 0
            for o_ref, g in zip(out_refs[nr:], grads[nr:]):
                @pl.when(first)
                def _(o_ref=o_ref):
                    o_ref[...] = jnp.zeros_like(o_ref)
                o_ref[...] += g

        res = _pcall(body, name=f"{name}_bwd", grid=(S // t,),
                     in_specs=row_specs + full_specs + [pl.BlockSpec((t, d.shape[1]), lambda i: (i, 0)) for d in douts],
                     out_specs=[pl.BlockSpec((t, a.shape[1]), lambda i: (i, 0)) for a in rows]
                     + [pl.BlockSpec(p.shape, lambda i: (0, 0)) for p in params],
                     out_shape=[jax.ShapeDtypeStruct(a.shape, a.dtype) for a in (*rows, *params)],
                     compiler_params=_params(("arbitrary",)))(*rows, *aux, *params, *consts, *douts)
        return tuple(res[:nr]), tuple(res[nr:])

    @jax.custom_vjp
    def op(rows, aux, params, consts):
        return tuple(fwd_call(rows, aux, params, consts))

    def fwd(rows, aux, params, consts):
        return op(rows, aux, params, consts), (rows, aux, params, consts)

    def bwd(res, douts):
        rows, aux, params, consts = res
        drows, dparams = bwd_call(rows, aux, params, consts, tuple(douts))
        zeros = lambda xs: tuple(jnp.zeros_like(a) for a in xs)
        return drows, zeros(aux), dparams, zeros(consts)

    op.defvjp(fwd, bwd)
    return op


@jax.custom_vjp
def _swap_halves(x):
    return pltpu.roll(x, LANES // 2, 1)


_swap_halves.defvjp(lambda x: (_swap_halves(x), None), lambda _, g: (_swap_halves(g),))


def _rope(x, cos_t, sin_t):
    return x * cos_t + _swap_halves(x) * sin_t


def _rms(x, g, n=None):
    n = x.shape[-1] if n is None else n
    return x * lax.rsqrt(jnp.sum(x * x, axis=-1, keepdims=True) * (1.0 / n) + RMS_EPS) * g


def _heads(x):
    return [x[:, i * LANES:(i + 1) * LANES] for i in range(x.shape[1] // LANES)]


def _cat(xs):
    return jnp.concatenate(xs, axis=1)


def _silu(x):
    return x * jax.nn.sigmoid(x)


def _f_norm(x, g):
    return (_rms(x, g).astype(BF16),)


def _f_mla_a(q_lat, c_kv, kpe, cos_p, sin_p, qa_g, kva_g, kpe_g):
    kp = _rope(_rms(kpe, kpe_g, MLA_ROPE), cos_p, sin_p)
    return _rms(q_lat, qa_g).astype(BF16), _rms(c_kv, kva_g).astype(BF16), _cat([kp] * MLA_HEADS)


def _f_mla_b(q8, kn_raw, cos_p, sin_p, qn_g, qp_g, kn_g):
    hs = _heads(q8)
    qn = _cat([_rms(h, qn_g) for h in hs[:MLA_HEADS]])
    qp = _cat([_rope(_rms(h, qp_g, MLA_ROPE), cos_p, sin_p) for h in hs[MLA_HEADS:]])
    kn = _cat([_rms(h, kn_g) for h in _heads(kn_raw)])
    return qn, qp, kn


def _softplus(x):
    return jnp.maximum(x, 0.0) + jnp.log(1.0 + jnp.exp(-jnp.abs(x)))


def _l2n(x):
    return x * lax.rsqrt(jnp.sum(x * x, axis=-1, keepdims=True) + 1e-6)


def _f_dn_pre(mixed, ab, alog_f, dtb_f, e_a, e_b):
    hs = _heads(mixed)
    q = _cat([_l2n(h) * (LANES ** -0.5) for h in hs[:DN_HEADS]])
    k = _cat([_l2n(h) for h in hs[DN_HEADS:2 * DN_HEADS]])
    v = _cat(hs[2 * DN_HEADS:])
    a_f = jnp.dot(ab, e_a, precision=HI, preferred_element_type=F32)
    b_f = jnp.dot(ab, e_b, precision=HI, preferred_element_type=F32)
    g = -jnp.exp(alog_f) * _softplus(a_f + dtb_f)
    return q, k, v, g, jax.nn.sigmoid(b_f)


def _f_dil_pre(qkv, cos_h, sin_h, q_g, k_g):
    hs = _heads(qkv)
    q = [_rope(_rms(h, q_g), cos_h, sin_h) for h in hs[:DIL_HEADS]]
    k = [_rope(_rms(h, k_g), cos_h, sin_h) for h in hs[DIL_HEADS:2 * DIL_HEADS]]
    v = hs[2 * DIL_HEADS:]
    group = lambda xs, g: _cat(xs[g * DIL_GROUP_HEADS:(g + 1) * DIL_GROUP_HEADS])
    return tuple(group(xs, g) for xs in (q, k, v) for g in range(len(DIL_DILATIONS)))


def _f_merge_a(y_a, z_a, o_dn, z_b, o0, o1, o2, l0, l1, l2, z_c, out_g):
    y_b = _cat([_rms(h, out_g) for h in _heads(o_dn)])
    os_, ls = [_heads(o) for o in (o0, o1, o2)], [_heads(l) for l in (l0, l1, l2)]
    y_c = []
    for j in range(DIL_GROUP_HEADS):
        l3 = [ls[g][j] for g in range(3)]
        m = jnp.maximum(jnp.maximum(l3[0], l3[1]), l3[2])
        e3 = [jnp.exp(l - m) for l in l3]
        den = e3[0] + e3[1] + e3[2]
        y_c.append(sum(e3[g] * os_[g][j] for g in range(3)) / den)
    return tuple(t.astype(BF16) for t in (y_a * _silu(z_a), y_b * _silu(z_b), _cat(y_c) * _silu(z_c)))


def _f_merge_b(b0, b1, b2, gl):
    gs = [jax.nn.sigmoid(gl[:, i * D_MODEL:(i + 1) * D_MODEL]) for i in range(3)]
    return ((gs[0] * b0 + gs[1] * b1 + gs[2] * b2).astype(BF16),)


def _rope_tables(pos, inv_sign):
    S = pos.shape[0]
    t = min(S, 1024)

    def body(p_ref, c_ref, cp, sp, ch, sh):
        p = p_ref[...].astype(F32)
        c = c_ref[...]
        ang_p, ang_h = p * c[0:1], p * c[2:3]
        cp[...] = jnp.cos(ang_p) * jnp.abs(c[1:2])
        sp[...] = jnp.sin(ang_p) * c[1:2]
        ch[...] = jnp.cos(ang_h)
        sh[...] = jnp.sin(ang_h) * c[3:4]

    row = pl.BlockSpec((t, LANES), lambda i: (i, 0))
    return _pcall(body, name="rope_tables", grid=(S // t,),
                  in_specs=[pl.BlockSpec((t, 1), lambda i: (i, 0)), pl.BlockSpec((4, LANES), lambda i: (0, 0))],
                  out_specs=[row] * 4, out_shape=[jax.ShapeDtypeStruct((S, LANES), F32)] * 4,
                  compiler_params=_params(("parallel",)))(pos, inv_sign)


def _rope_consts():
    half_p, half_h = MLA_ROPE // 2, LANES // 2
    inv_p = 1.0 / (ROPE_THETA ** (jnp.arange(0, MLA_ROPE, 2, dtype=F32) / MLA_ROPE))
    inv_h = 1.0 / (ROPE_THETA ** (jnp.arange(0, LANES, 2, dtype=F32) / LANES))
    z = jnp.zeros((half_p,), F32)
    o = jnp.ones((half_p,), F32)
    return jnp.stack([jnp.concatenate([inv_p, z, inv_p, z]), jnp.concatenate([-o, z, o, z]),
                      jnp.concatenate([inv_h, inv_h]), jnp.concatenate([-jnp.ones((half_h,), F32), jnp.ones((half_h,), F32)])])


def _shift_rows(x, s, up):
    n = x.shape[0]
    r = lax.broadcasted_iota(jnp.int32, x.shape, 0)
    if up:
        return jnp.where(r < n - s, pltpu.roll(x, n - s, 0), 0.0)
    return jnp.where(r >= s, pltpu.roll(x, s, 0), 0.0)


def _make_shift(s):
    @jax.custom_vjp
    def sh(x):
        return _shift_rows(x, s, False)

    sh.defvjp(lambda x: (sh(x), None), lambda _, g: (_shift_rows(g, s, True),))
    return sh


def _f_conv(x, w):
    y = x * w[DN_CONV - 1:DN_CONV]
    for j in range(DN_CONV - 1):
        y = y + _make_shift(DN_CONV - 1 - j)(x) * w[j:j + 1]
    return _silu(y)


def _make_conv(name):
    def call(x, w, dy=None):
        S, C = x.shape
        col = pl.BlockSpec((S, LANES), lambda i: (0, i))
        wsp = pl.BlockSpec((DN_CONV, LANES), lambda i: (0, i))
        if dy is None:
            def body(x_ref, w_ref, o_ref):
                o_ref[...] = _f_conv(x_ref[...], w_ref[...])
            return _pcall(body, name=f"{name}_fwd", grid=(C // LANES,), in_specs=[col, wsp], out_specs=col,
                          out_shape=jax.ShapeDtypeStruct(x.shape, F32), compiler_params=_params(("parallel",)))(x, w)

        def body(x_ref, w_ref, dy_ref, dx_ref, dw_ref):
            _, vjp = jax.vjp(_f_conv, x_ref[...], w_ref[...])
            dx_ref[...], dw_ref[...] = vjp(dy_ref[...])
        return _pcall(body, name=f"{name}_bwd", grid=(C // LANES,), in_specs=[col, wsp, col], out_specs=[col, wsp],
                      out_shape=[jax.ShapeDtypeStruct(x.shape, F32), jax.ShapeDtypeStruct(w.shape, F32)],
                      compiler_params=_params(("parallel",)))(x, w, dy)

    @jax.custom_vjp
    def op(x, w):
        return call(x, w)

    op.defvjp(lambda x, w: (op(x, w), (x, w)), lambda res, dy: tuple(call(*res, dy)))
    return op


def _dot_nt(a, b):
    return lax.dot_general(a.astype(BF16), b.astype(BF16), (((1,), (1,)), ((), ())), preferred_element_type=F32)


def _dot_nn(a, b):
    return jnp.dot(a.astype(BF16), b.astype(BF16), preferred_element_type=F32)


def _dot_tn(a, b):
    return lax.dot_general(a.astype(BF16), b.astype(BF16), (((0,), (0,)), ((), ())), preferred_element_type=F32)


def _mla_scores(qn_r, qp_r, kn_r, kp_r, diagonal):
    scale = MLA_QK ** -0.5
    s = _dot_nt(qn_r[...] * scale, kn_r[...]) + _dot_nt(qp_r[...] * scale, kp_r[...])
    if diagonal:
        r = lax.broadcasted_iota(jnp.int32, s.shape, 0)
        c = lax.broadcasted_iota(jnp.int32, s.shape, 1)
        s = jnp.where(c <= r, s, NEG)
    return s


def _on_causal_pairs(qi, ki, step):
    @pl.when(ki < qi)
    def _():
        step(False)

    @pl.when(ki == qi)
    def _():
        step(True)


def _make_mla_attn(name):
    scale = MLA_QK ** -0.5

    def fwd_call(qn, qp, kn, kp, v):
        S = qn.shape[0]
        t = min(S, 512)
        n = S // t
        qs = pl.BlockSpec((t, LANES), lambda h, i, j: (i, h))
        ks = pl.BlockSpec((t, LANES), lambda h, i, j: (jnp.minimum(j, i), h))

        def body(qn_r, qp_r, kn_r, kp_r, v_r, o_r, lse_r, m_s, l_s, acc_s):
            qi, ki = pl.program_id(1), pl.program_id(2)

            @pl.when(ki == 0)
            def _():
                m_s[...] = jnp.full_like(m_s, NEG)
                l_s[...] = jnp.zeros_like(l_s)
                acc_s[...] = jnp.zeros_like(acc_s)

            def step(diagonal):
                s = _mla_scores(qn_r, qp_r, kn_r, kp_r, diagonal)
                m_old = m_s[...]
                m_new = jnp.maximum(m_old, jnp.max(s, axis=-1, keepdims=True))
                p = jnp.exp(s - m_new[:, :1])
                alpha = jnp.exp(m_old - m_new)
                l_s[...] = alpha * l_s[...] + jnp.sum(p, axis=-1, keepdims=True)
                acc_s[...] = alpha * acc_s[...] + _dot_nn(p, v_r[...])
                m_s[...] = m_new
            _on_causal_pairs(qi, ki, step)

            @pl.when(ki == n - 1)
            def _():
                o_r[...] = acc_s[...] / l_s[...]
                lse_r[...] = m_s[...] + jnp.log(l_s[...])

        return _pcall(body, name=f"{name}_fwd", grid=(MLA_HEADS, n, n), in_specs=[qs, qs, ks, ks, ks], out_specs=[qs, qs],
                      out_shape=[jax.ShapeDtypeStruct((S, MLA_HEADS * LANES), F32)] * 2,
                      scratch_shapes=[pltpu.VMEM((t, LANES), F32)] * 3,
                      compiler_params=_params(("parallel", "parallel", "arbitrary")))(qn, qp, kn, kp, v)

    def dq_call(qn, qp, kn, kp, v, o, lse, do):
        S = qn.shape[0]
        t = min(S, 512)
        n = S // t
        qs = pl.BlockSpec((t, LANES), lambda h, i, j: (i, h))
        ks = pl.BlockSpec((t, LANES), lambda h, i, j: (jnp.minimum(j, i), h))

        def body(qn_r, qp_r, kn_r, kp_r, v_r, o_r, lse_r, do_r, dqn_r, dqp_r, dl_r, dqn_s, dqp_s):
            qi, ki = pl.program_id(1), pl.program_id(2)

            @pl.when(ki == 0)
            def _():
                dqn_s[...] = jnp.zeros_like(dqn_s)
                dqp_s[...] = jnp.zeros_like(dqp_s)
                dl_r[...] = jnp.broadcast_to(jnp.sum(do_r[...] * o_r[...], axis=-1, keepdims=True), dl_r.shape)

            def step(diagonal):
                p = jnp.exp(_mla_scores(qn_r, qp_r, kn_r, kp_r, diagonal) - lse_r[...][:, :1])
                ds = p * (_dot_nt(do_r[...], v_r[...]) - dl_r[...][:, :1])
                dqn_s[...] += _dot_nn(ds, kn_r[...])
                dqp_s[...] += _dot_nn(ds, kp_r[...])
            _on_causal_pairs(qi, ki, step)

            @pl.when(ki == n - 1)
            def _():
                dqn_r[...] = dqn_s[...] * scale
                dqp_r[...] = dqp_s[...] * scale

        return _pcall(body, name=f"{name}_dq", grid=(MLA_HEADS, n, n), in_specs=[qs, qs, ks, ks, ks, qs, qs, qs],
                      out_specs=[qs, qs, qs], out_shape=[jax.ShapeDtypeStruct((S, MLA_HEADS * LANES), F32)] * 3,
                      scratch_shapes=[pltpu.VMEM((t, LANES), F32)] * 2,
                      compiler_params=_params(("parallel", "parallel", "arbitrary")))(qn, qp, kn, kp, v, o, lse, do)

    def dkv_call(qn, qp, kn, kp, v, lse, do, dl):
        S = qn.shape[0]
        t = min(S, 512)
        n = S // t
        ks = pl.BlockSpec((t, LANES), lambda h, j, i: (j, h))
        qs = pl.BlockSpec((t, LANES), lambda h, j, i: (jnp.maximum(i, j), h))

        def body(qn_r, qp_r, kn_r, kp_r, v_r, lse_r, do_r, dl_r, dkn_r, dkp_r, dv_r, dkn_s, dkp_s, dv_s):
            ki, qi = pl.program_id(1), pl.program_id(2)

            @pl.when(qi == 0)
            def _():
                dkn_s[...] = jnp.zeros_like(dkn_s)
                dkp_s[...] = jnp.zeros_like(dkp_s)
                dv_s[...] = jnp.zeros_like(dv_s)

            def step(diagonal):
                p = jnp.exp(_mla_scores(qn_r, qp_r, kn_r, kp_r, diagonal) - lse_r[...][:, :1])
                ds = p * (_dot_nt(do_r[...], v_r[...]) - dl_r[...][:, :1])
                dv_s[...] += _dot_tn(p, do_r[...])
                dkn_s[...] += _dot_tn(ds, qn_r[...] * scale)
                dkp_s[...] += _dot_tn(ds, qp_r[...] * scale)
            _on_causal_pairs(qi, ki, step)

            @pl.when(qi == n - 1)
            def _():
                dkn_r[...] = dkn_s[...]
                dkp_r[...] = dkp_s[...]
                dv_r[...] = dv_s[...]

        return _pcall(body, name=f"{name}_dkv", grid=(MLA_HEADS, n, n), in_specs=[qs, qs, ks, ks, ks, qs, qs, qs],
                      out_specs=[ks, ks, ks], out_shape=[jax.ShapeDtypeStruct((S, MLA_HEADS * LANES), F32)] * 3,
                      scratch_shapes=[pltpu.VMEM((t, LANES), F32)] * 3,
                      compiler_params=_params(("parallel", "parallel", "arbitrary")))(qn, qp, kn, kp, v, lse, do, dl)

    @jax.custom_vjp
    def op(qn, qp, kn, kp, v):
        return fwd_call(qn, qp, kn, kp, v)[0]

    def fwd(qn, qp, kn, kp, v):
        o, lse = fwd_call(qn, qp, kn, kp, v)
        return o, (qn, qp, kn, kp, v, o, lse)

    def bwd(res, do):
        qn, qp, kn, kp, v, o, lse = res
        dqn, dqp, dl = dq_call(qn, qp, kn, kp, v, o, lse, do)
        dkn, dkp, dv = dkv_call(qn, qp, kn, kp, v, lse, do, dl)
        return dqn, dqp, dkn, dkp, dv

    op.defvjp(fwd, bwd)
    return op


def _dil_block(q, kp, kc, vp, vc, has_prev):
    scale = LANES ** -0.5
    r = lax.broadcasted_iota(jnp.int32, (DIL_BLOCK, DIL_BLOCK), 0)
    c = lax.broadcasted_iota(jnp.int32, (DIL_BLOCK, DIL_BLOCK), 1)
    s_c = jnp.where(c <= r, _dot_nt(q, kc) * scale, NEG)
    s_p = jnp.where((c >= r) & has_prev, _dot_nt(q, kp) * scale, NEG)
    m = jnp.maximum(jnp.max(s_c, axis=-1, keepdims=True), jnp.max(s_p, axis=-1, keepdims=True))
    e_c, e_p = jnp.exp(s_c - m), jnp.exp(s_p - m)
    den = jnp.sum(e_c, axis=-1, keepdims=True) + jnp.sum(e_p, axis=-1, keepdims=True)
    o = (_dot_nn(e_c, vc) + _dot_nn(e_p, vp)) / den
    return o, jnp.broadcast_to(m + jnp.log(den), o.shape)


DIL_TILE_ROWS = (1024, 1024, 2048)


def _make_dil_attn(name, d, tile_rows):
    def call(q, k, v, cts=None):
        S = q.shape[0]
        span = DIL_BLOCK * d
        G = max(1, min(tile_rows, S) // span)
        n = S // (G * span)
        at = (lambda i: i) if cts is None else (lambda i: n - 1 - i)
        tile = pl.BlockSpec((G * span, LANES), lambda h, i: (at(i), h))
        before = pl.BlockSpec((span, LANES), lambda h, i: (jnp.maximum(at(i) * G - 1, 0), h))

        def rows(r, j):
            return pl.ds(j * DIL_BLOCK, DIL_BLOCK) if d == 1 else pl.ds(r + j * span, DIL_BLOCK, stride=d)

        def over_residues(fn):
            if d == 1:
                fn(0)
            else:
                lax.fori_loop(0, d, lambda r, c: (fn(r), c)[1], 0)

        def block_inputs(r, j, q_r, kb_r, k_r, vb_r, v_r):
            kp = kb_r[rows(r, 0), :] if j == 0 else k_r[rows(r, j - 1), :]
            vp = vb_r[rows(r, 0), :] if j == 0 else v_r[rows(r, j - 1), :]
            return q_r[rows(r, j), :], kp, k_r[rows(r, j), :], vp, v_r[rows(r, j), :]

        if cts is None:
            def body(q_r, kb_r, k_r, vb_r, v_r, o_r, lse_r):
                first = at(pl.program_id(1)) * G

                def residue(r):
                    for j in range(G):
                        o_r[rows(r, j), :], lse_r[rows(r, j), :] = _dil_block(
                            *block_inputs(r, j, q_r, kb_r, k_r, vb_r, v_r), first + j > 0)
                over_residues(residue)
            return _pcall(body, name=f"{name}_fwd", grid=(DIL_GROUP_HEADS, n), in_specs=[tile, before, tile, before, tile],
                          out_specs=[tile, tile], out_shape=[jax.ShapeDtypeStruct(q.shape, F32)] * 2,
                          compiler_params=_params(("parallel", "parallel")))(q, k, k, v, v)

        def body(q_r, kb_r, k_r, vb_r, v_r, do_r, dl_r, dq_r, dk_r, dv_r, ck_s, cv_s):
            first = at(pl.program_id(1)) * G

            @pl.when(pl.program_id(1) == 0)
            def _():
                ck_s[...] = jnp.zeros_like(ck_s)
                cv_s[...] = jnp.zeros_like(cv_s)

            def residue(r):
                owed = None
                for j in range(G):
                    hp = first + j > 0
                    _, vjp = jax.vjp(lambda *a: _dil_block(*a, hp), *block_inputs(r, j, q_r, kb_r, k_r, vb_r, v_r))
                    dq, dkp, dkc, dvp, dvc = vjp((do_r[rows(r, j), :], dl_r[rows(r, j), :]))
                    dq_r[rows(r, j), :] = dq
                    if j == G - 1:
                        dkc, dvc = dkc + ck_s[rows(r, 0), :], dvc + cv_s[rows(r, 0), :]
                    dk_r[rows(r, j), :], dv_r[rows(r, j), :] = dkc, dvc
                    if j == 0:
                        owed = (dkp, dvp)
                    else:
                        dk_r[rows(r, j - 1), :] += dkp
                        dv_r[rows(r, j - 1), :] += dvp
                ck_s[rows(r, 0), :], cv_s[rows(r, 0), :] = owed
            over_residues(residue)
        return _pcall(body, name=f"{name}_bwd", grid=(DIL_GROUP_HEADS, n), in_specs=[tile, before, tile, before, tile, tile, tile],
                      out_specs=[tile] * 3, out_shape=[jax.ShapeDtypeStruct(q.shape, F32)] * 3,
                      scratch_shapes=[pltpu.VMEM((span, LANES), F32)] * 2,
                      compiler_params=_params(("parallel", "arbitrary")))(q, k, k, v, v, *cts)

    @jax.custom_vjp
    def op(q, k, v):
        return tuple(call(q, k, v))

    op.defvjp(lambda q, k, v: (op(q, k, v), (q, k, v)), lambda res, cts: tuple(call(*res, cts=cts)))
    return op


def _pdot(a, b, dims):
    return lax.dot_general(a, b, (dims, ((), ())), precision=lax.Precision.HIGH, preferred_element_type=F32)


DN_LOCAL_CHUNKS = 4


DN_BLOCK_HEADS = 4
DN_BLOCK = DN_BLOCK_HEADS * DN_CHUNK


def _dn_local(q, k, v, g, b):
    C, R = DN_CHUNK, DN_BLOCK
    r = lax.broadcasted_iota(jnp.int32, (R, R), 0)
    c = lax.broadcasted_iota(jnp.int32, (R, R), 1)
    same_head = (r // C) == (c // C)
    incl, strict = same_head & (r >= c), same_head & (r > c)
    eye = (r == c).astype(F32)
    avg = jnp.full((R, LANES), 1.0 / LANES, F32)
    rc = lax.broadcasted_iota(jnp.int32, (C, C), 0) >= lax.broadcasted_iota(jnp.int32, (C, C), 1)
    gc_lanes = _pdot(rc.astype(F32), g, ((1,), (0,)))
    us, ws, qes, kds, qks = [], [], [], [], []
    for first in range(0, DN_HEADS, DN_BLOCK_HEADS):
        stack = lambda x: jnp.concatenate(_heads(x)[first:first + DN_BLOCK_HEADS], axis=0)
        unstack = lambda x: [x[p * C:(p + 1) * C] for p in range(DN_BLOCK_HEADS)]
        gc, q_s, k_s, v_s, b_s = (stack(x) for x in (gc_lanes, q, k, v, b))
        gc_j = _pdot(avg, gc, ((1,), (1,)))
        decay = jnp.exp(jnp.where(incl, _cat([gc] * (R // LANES)) - gc_j, NEG))
        kb = k_s * b_s
        kk = _pdot(jnp.concatenate([kb, q_s], axis=0), k_s, ((1,), (1,)))
        a = jnp.where(strict, kk[:R] * decay, 0.0)
        inv, pw = eye - a, a
        for _ in range(5):
            pw = _pdot(pw, pw, ((1,), (0,)))
            inv = inv + _pdot(inv, pw, ((1,), (0,)))
        eg = jnp.exp(gc)
        uw = _pdot(inv, _cat([v_s * b_s, kb * eg]), ((1,), (0,)))
        g_last = jnp.concatenate([jnp.broadcast_to(x[C - 1:C], (C, LANES)) for x in unstack(gc)], axis=0)
        us += unstack(uw[:, :LANES])
        ws += unstack(uw[:, LANES:])
        qes += unstack(q_s * eg)
        kds += unstack(k_s * jnp.exp(g_last - gc))
        qks.append(kk[R:] * decay)
    egl = jnp.broadcast_to(jnp.exp(gc_lanes[C - 1:C]), (8, DN_HEADS * LANES))
    return _cat(us), _cat(ws), _cat(qes), _cat(kds), jnp.concatenate(qks, axis=0), egl


def _dn_scan(u, w, qe, kd, qk, egl, state):
    C = DN_CHUNK
    heads = [slice(h * LANES, (h + 1) * LANES) for h in range(DN_HEADS)]
    ws = [_pdot(jnp.concatenate([w[:, sl], qe[:, sl]], axis=0), state[sl, :], ((1,), (0,))) for sl in heads]
    v_new = [u[:, sl] - x[:C] for sl, x in zip(heads, ws)]
    local = []
    for i, first in enumerate(range(0, DN_HEADS, DN_BLOCK_HEADS)):
        y = _pdot(qk[i * DN_BLOCK:(i + 1) * DN_BLOCK], jnp.concatenate(v_new[first:first + DN_BLOCK_HEADS], axis=0), ((1,), (0,)))
        local += [y[p * C:(p + 1) * C] for p in range(DN_BLOCK_HEADS)]
    o = _cat([x[C:] + y for x, y in zip(ws, local)])
    states = [state[sl, :] * egl[0:1, sl] + _pdot(kd[:, sl], vn, ((0,), (0,))) for sl, vn in zip(heads, v_new)]
    return o, jnp.concatenate(states, axis=0)


def _make_delta_rule(name):
    W = DN_HEADS * LANES
    QK = DN_HEADS * DN_CHUNK

    def local_call(ins, cts=None):
        S = ins[0].shape[0]
        n = S // DN_CHUNK
        per = math.gcd(DN_LOCAL_CHUNKS, n)
        row = pl.BlockSpec((per * DN_CHUNK, W), lambda i: (i, 0))
        qkb = pl.BlockSpec((per * QK, DN_BLOCK), lambda i: (i, 0))
        eg = pl.BlockSpec((per, 8, W), lambda i: (i, 0, 0))
        rows = lambda j: slice(j * DN_CHUNK, (j + 1) * DN_CHUNK)
        qk_rows = lambda j: slice(j * QK, (j + 1) * QK)
        out_rows = [rows, rows, rows, rows, qk_rows]

        if cts is None:
            def body(*refs):
                for j in range(per):
                    res = _dn_local(*[r[rows(j), :] for r in refs[:5]])
                    for o_r, o, at_ in zip(refs[5:10], res[:5], out_rows):
                        o_r[at_(j), :] = o
                    refs[10][j] = res[5]
            return _pcall(body, name=f"{name}_local_fwd", grid=(n // per,), in_specs=[row] * 5, out_specs=[row] * 4 + [qkb, eg],
                          out_shape=[jax.ShapeDtypeStruct((S, W), F32)] * 4
                          + [jax.ShapeDtypeStruct((n * QK, DN_BLOCK), F32), jax.ShapeDtypeStruct((n, 8, W), F32)],
                          compiler_params=_params(("parallel",)))(*ins)

        def body(*refs):
            for j in range(per):
                _, vjp = jax.vjp(_dn_local, *[r[rows(j), :] for r in refs[:5]])
                grads = vjp(tuple(r[at_(j), :] for r, at_ in zip(refs[5:10], out_rows)) + (refs[10][j],))
                for o_r, o in zip(refs[11:], grads):
                    o_r[rows(j), :] = o
        return _pcall(body, name=f"{name}_local_bwd", grid=(n // per,), in_specs=[row] * 9 + [qkb, eg], out_specs=[row] * 5,
                      out_shape=[jax.ShapeDtypeStruct((S, W), F32)] * 5, compiler_params=_params(("parallel",)))(*ins, *cts)

    def scan_call(ins, saved=None, do=None):
        S = ins[0].shape[0]
        n = S // DN_CHUNK
        at = (lambda i: i) if do is None else (lambda i: n - 1 - i)
        row = pl.BlockSpec((DN_CHUNK, W), lambda i: (at(i), 0))
        qkb = pl.BlockSpec((QK, DN_BLOCK), lambda i: (at(i), 0))
        eg = pl.BlockSpec((None, 8, W), lambda i: (at(i), 0, 0))
        st = pl.BlockSpec((None, W, LANES), lambda i: (at(i), 0, 0))

        if do is None:
            def body(*refs):
                o_r, st_r, s_s = refs[6:]

                @pl.when(pl.program_id(0) == 0)
                def _():
                    s_s[...] = jnp.zeros_like(s_s)
                st_r[...] = s_s[...]
                o_r[...], s_s[...] = _dn_scan(*[r[...] for r in refs[:6]], s_s[...])
            return _pcall(body, name=f"{name}_scan_fwd", grid=(n,), in_specs=[row] * 4 + [qkb, eg], out_specs=[row, st],
                          out_shape=[jax.ShapeDtypeStruct((S, W), F32), jax.ShapeDtypeStruct((n, W, LANES), F32)],
                          scratch_shapes=[pltpu.VMEM((W, LANES), F32)], compiler_params=_params(("arbitrary",)))(*ins)

        def body(*refs):
            st_r, do_r = refs[6:8]
            outs, ds_s = refs[8:14], refs[14]

            @pl.when(pl.program_id(0) == 0)
            def _():
                ds_s[...] = jnp.zeros_like(ds_s)
            _, vjp = jax.vjp(_dn_scan, *[r[...] for r in refs[:6]], st_r[...])
            *grads, ds = vjp((do_r[...], ds_s[...]))
            for o_r, gval in zip(outs, grads):
                o_r[...] = gval
            ds_s[...] = ds
        return _pcall(body, name=f"{name}_scan_bwd", grid=(n,), in_specs=[row] * 4 + [qkb, eg, st, row], out_specs=[row] * 4 + [qkb, eg],
                      out_shape=[jax.ShapeDtypeStruct((S, W), F32)] * 4
                      + [jax.ShapeDtypeStruct((n * QK, DN_BLOCK), F32), jax.ShapeDtypeStruct((n, 8, W), F32)],
                      scratch_shapes=[pltpu.VMEM((W, LANES), F32)], compiler_params=_params(("arbitrary",)))(*ins, saved, do)

    @jax.custom_vjp
    def local(q, k, v, g, b):
        return tuple(local_call((q, k, v, g, b)))

    local.defvjp(lambda *a: (local(*a), a), lambda res, cts: tuple(local_call(res, tuple(cts))))

    @jax.custom_vjp
    def scan(u, w, qe, kd, qk, egl):
        return scan_call((u, w, qe, kd, qk, egl))[0]

    def scan_fwd(*a):
        o, states = scan_call(a)
        return o, (a, states)

    scan.defvjp(scan_fwd, lambda res, do: tuple(scan_call(res[0], res[1], do)))
    return lambda q, k, v, g, b: scan(*local(q, k, v, g, b))


def _loss_call(y, target):
    S, D = y.shape
    t = min(S, 512)
    n = S // t
    row = pl.BlockSpec((t, D), lambda i: (i, 0))

    def body(y_r, t_r, loss_r, dy_r, acc_s):
        i = pl.program_id(0)

        @pl.when(i == 0)
        def _():
            acc_s[...] = jnp.zeros_like(acc_s)
        err = y_r[...] - t_r[...]
        dy_r[...] = err * (1.0 / D)
        acc_s[...] += jnp.sum(err * err, axis=0, keepdims=True)

        @pl.when(i == n - 1)
        def _():
            loss_r[...] = jnp.broadcast_to(jnp.sum(acc_s[...], axis=1, keepdims=True) * (0.5 / D), loss_r.shape)

    return _pcall(body, name="loss_head", grid=(n,), in_specs=[row, row],
                  out_specs=[pl.BlockSpec((8, LANES), lambda i: (0, 0)), row],
                  out_shape=[jax.ShapeDtypeStruct((8, LANES), F32), jax.ShapeDtypeStruct((S, D), F32)],
                  scratch_shapes=[pltpu.VMEM((1, D), F32)], compiler_params=_params(("arbitrary",)))(y, target)


def _adamw_call(name, parts, w, m, v, rows=128):
    L, R, C = w.shape
    t = _tile(R, rows, 8)
    row = pl.BlockSpec((None, t, C), lambda l, i: (l, i, 0))

    def body(p_r, w_r, m_r, v_r, g_r, d_r, nm_r, nv_r):
        g = p_r[0].astype(F32)
        for s in range(1, N_DEV):
            g = g + p_r[s].astype(F32)
        m_new = ADAM_B1 * m_r[...] + (1.0 - ADAM_B1) * g
        v_new = ADAM_B2 * v_r[...] + (1.0 - ADAM_B2) * (g * g)
        m_hat = m_new / (1.0 - ADAM_B1 ** ADAM_STEP)
        v_hat = v_new / (1.0 - ADAM_B2 ** ADAM_STEP)
        g_r[...] = g
        d_r[...] = -ADAM_LR * (m_hat / (jnp.sqrt(v_hat) + ADAM_EPS) + ADAM_WD * w_r[...])
        nm_r[...] = m_new
        nv_r[...] = v_new

    return _pcall(body, name=name, grid=(L, R // t),
                  in_specs=[pl.BlockSpec((N_DEV, None, t, C), lambda l, i: (0, l, i, 0)), row, row, row],
                  out_specs=[row] * 4, out_shape=[jax.ShapeDtypeStruct((L, R, C), F32)] * 4,
                  compiler_params=_params(("parallel", "parallel")))(parts, w, m, v)


def _my_place():
    x, y, c = lax.axis_index("x"), lax.axis_index("y"), lax.axis_index("c")
    return x, y, c


def _index(x, y, c):
    return 4 * x + 2 * y + c


def _all_gather(vs):
    n = len(vs)

    def body(*refs):
        v_refs, out_refs = refs[:n], refs[n:2 * n]
        send_sems, recv_sems, local_sems = refs[2 * n:]
        x, y, c = _my_place()
        me, sibling = (x, y, c), (x, y, 1 - c)
        chips = [(1 - x, y), (x, 1 - y), (1 - x, 1 - y)]

        def copy(a, k, block, to, src=None):
            rows = out_refs[a].at[_index(*block)]
            return pltpu.make_async_remote_copy(src_ref=rows if src is None else src, dst_ref=rows, send_sem=send_sems.at[a, k],
                                                recv_sem=recv_sems.at[a, k], device_id=to, device_id_type=MESH)

        mine = [pltpu.make_async_copy(v_refs[a], out_refs[a].at[_index(*me)], local_sems.at[a]) for a in range(n)]
        first, passed = [], []
        for a in range(n):
            mine[a].start()
            first += [copy(a, 0, me, sibling, src=v_refs[a])]
            first += [copy(a, 1 + j, me, (*chip, c), src=v_refs[a]) for j, chip in enumerate(chips)]
        for cp in first:
            cp.start()
        for j, chip in enumerate(chips):
            for a in range(n):
                copy(a, 1 + j, (*chip, c), me).wait_recv()
                passed.append(copy(a, 4 + j, (*chip, c), sibling))
                passed[-1].start()
        for a in range(n):
            copy(a, 0, sibling, me).wait_recv()
            for j, chip in enumerate(chips):
                copy(a, 4 + j, (*chip, 1 - c), me).wait_recv()
        for cp in first + passed:
            cp.wait_send()
        for a in range(n):
            mine[a].wait()

    any_ = pl.BlockSpec(memory_space=pl.ANY)
    return _pcall(body, name="gather_weights", in_specs=[any_] * n, out_specs=[any_] * n,
                  out_shape=[jax.ShapeDtypeStruct((N_DEV,) + v.shape, v.dtype) for v in vs],
                  scratch_shapes=[pltpu.SemaphoreType.DMA((n, 7)), pltpu.SemaphoreType.DMA((n, 7)), pltpu.SemaphoreType.DMA((n,))])(*vs)


def _all_to_all(vs):
    n = len(vs)

    def body(*refs):
        v_refs, out_refs = refs[:n], refs[n:2 * n]
        send_sems, recv_sems, local_sems = refs[2 * n:]
        x, y, c = _my_place()
        me = _index(x, y, c)
        mine = [pltpu.make_async_copy(v_refs[a].at[me], out_refs[a].at[me], local_sems.at[a]) for a in range(n)]
        copies = []
        for a in range(n):
            mine[a].start()
        for k in range(1, N_DEV):
            px = 1 - x if k & 4 else x
            py = 1 - y if k & 2 else y
            pc = 1 - c if k & 1 else c
            for a in range(n):
                cp = pltpu.make_async_remote_copy(src_ref=v_refs[a].at[_index(px, py, pc)], dst_ref=out_refs[a].at[me],
                                                  send_sem=send_sems.at[a, k - 1], recv_sem=recv_sems.at[a, k - 1],
                                                  device_id=(px, py, pc), device_id_type=MESH)
                cp.start()
                copies.append(cp)
        for cp in copies:
            cp.wait()
        for a in range(n):
            mine[a].wait()

    any_ = pl.BlockSpec(memory_space=pl.ANY)
    return _pcall(body, name="exchange_grads", in_specs=[any_] * n, out_specs=[any_] * n,
                  out_shape=[jax.ShapeDtypeStruct(v.shape, v.dtype) for v in vs],
                  scratch_shapes=[pltpu.SemaphoreType.DMA((n, 7)), pltpu.SemaphoreType.DMA((n, 7)), pltpu.SemaphoreType.DMA((n,))])(*vs)


W_IN_SHARD = IN_WIDTH // N_DEV
SEG_ORDER = ("q_lat", "c_kv", "k_pe", "z_a", "dn_qkv", "dn_ab", "z_b", "dil_qkv", "z_c", "gate")
SEG_WIDTH = (384, 256, LANES, 512, 1536, LANES, 512, 4608, 512, 3072)


def _w_in_plan():
    plan = []

    def add(seg, c0, c1, dst):
        while c0 < c1:
            d = c0 // W_IN_SHARD
            e = min(c1, (d + 1) * W_IN_SHARD)
            plan.append((seg, dst, d, c0 - d * W_IN_SHARD, e - c0))
            dst += e - c0
            c0 = e

    half = MLA_ROPE // 2
    for i, name in enumerate(SEG_ORDER):
        if name == "k_pe":
            o = _SEG["k_pe"][0]
            add(i, o, o + half, 0)
            add(i, o + half, o + 2 * half, LANES // 2)
        elif name == "dn_ab":
            o = _SEG["dn_a"][0]
            add(i, o, o + 2 * DN_HEADS, 0)
        else:
            o, w = _SEG[name]
            add(i, o, o + w, 0)
    return plan


def _make_w_in_segments(name):
    plan = _w_in_plan()
    nseg = len(SEG_ORDER)
    t = 256

    def fwd_call(g):
        L = g.shape[1]

        def body(g_ref, *o_refs):
            for i in (SEG_ORDER.index("k_pe"), SEG_ORDER.index("dn_ab")):
                o_refs[i][...] = jnp.zeros_like(o_refs[i])
            for seg, dst, d, src, n in plan:
                o_refs[seg][:, dst:dst + n] = g_ref[d, :, src:src + n]

        return _pcall(body, name=f"{name}_fwd", grid=(L, D_MODEL // t),
                      in_specs=[pl.BlockSpec((N_DEV, None, t, W_IN_SHARD), lambda l, i: (0, l, i, 0))],
                      out_specs=[pl.BlockSpec((None, t, w), lambda l, i: (l, i, 0)) for w in SEG_WIDTH],
                      out_shape=[jax.ShapeDtypeStruct((L, D_MODEL, w), g.dtype) for w in SEG_WIDTH],
                      compiler_params=_params(("parallel", "parallel")))(g)

    def bwd_call(ds):
        L = ds[0].shape[0]

        def body(*refs):
            d_refs, g_ref = refs[:nseg], refs[nseg]
            for seg, dst, d, src, n in plan:
                g_ref[d, :, src:src + n] = d_refs[seg][:, dst:dst + n]

        return _pcall(body, name=f"{name}_bwd", grid=(L, D_MODEL // t),
                      in_specs=[pl.BlockSpec((None, t, w), lambda l, i: (l, i, 0)) for w in SEG_WIDTH],
                      out_specs=pl.BlockSpec((N_DEV, None, t, W_IN_SHARD), lambda l, i: (0, l, i, 0)),
                      out_shape=jax.ShapeDtypeStruct((N_DEV, L, D_MODEL, W_IN_SHARD), ds[0].dtype),
                      compiler_params=_params(("parallel", "parallel")))(*ds)

    @jax.custom_vjp
    def op(g):
        return tuple(fwd_call(g))

    op.defvjp(lambda g: (op(g), None), lambda _, ds: (bwd_call(tuple(ds)),))
    return op


def _pe_pad(a):
    h = MLA_ROPE // 2
    z = jnp.zeros(a.shape[:-1] + (h,), a.dtype)
    return jnp.concatenate([a[..., :h], z, a[..., h:], z], axis=-1)


def _layer(tag, x, tables, W):
    cos_p, sin_p, cos_h, sin_h = tables
    w_segs = W["w_in_segments"]
    row = lambda a: a[None, :]

    (h,) = _make_rowwise(f"{tag}_norm", _f_norm, 512)((x,), (), (row(W["norm_g"]),), ())
    q_lat, c_kv, kpe, z_a, dn_qkv, ab, z_b, dil_qkv, z_c, gl = _make_multi_linear(f"{tag}_inproj", 10)(h, w_segs)

    qn_lat, ckvn, kp = _make_rowwise(f"{tag}_mla_a", _f_mla_a, 512)(
        (q_lat, c_kv, kpe), (cos_p, sin_p),
        (row(W["mla_q_a_norm_g"]), row(W["mla_kv_a_norm_g"]), row(_pe_pad(W["mla_k_norm_g"][LANES:]))), ())
    wq = W["mla_w_q_b"].reshape(MLA_Q_RANK, MLA_HEADS, MLA_QK)
    wq = jnp.concatenate([wq[:, :, :LANES].reshape(MLA_Q_RANK, -1), _pe_pad(wq[:, :, LANES:]).reshape(MLA_Q_RANK, -1)], axis=1)
    wkv = W["mla_w_kv_b"].reshape(MLA_KV_RANK, MLA_HEADS, 2 * LANES)
    (q8,) = _make_multi_linear(f"{tag}_qb", 1)(qn_lat, (wq,))
    kn_raw, v_mla = _make_multi_linear(f"{tag}_kvb", 2)(
        ckvn, (wkv[:, :, :LANES].reshape(MLA_KV_RANK, -1), wkv[:, :, LANES:].reshape(MLA_KV_RANK, -1)))
    qn, qp, kn = _make_rowwise(f"{tag}_mla_b", _f_mla_b, 512)(
        (q8, kn_raw), (cos_p, sin_p),
        (row(W["mla_q_norm_g"][:LANES]), row(_pe_pad(W["mla_q_norm_g"][LANES:])), row(W["mla_k_norm_g"][:LANES])), ())
    y_a = _make_mla_attn(f"{tag}_mla")(qn, qp, kn, kp, v_mla)

    mixed = _make_conv(f"{tag}_conv")(dn_qkv, W["dn_conv_w"])
    lane_head = jnp.arange(DN_HEADS * LANES) // LANES
    e_a = (jnp.arange(LANES)[:, None] == lane_head[None, :]).astype(F32)
    e_b = (jnp.arange(LANES)[:, None] == lane_head[None, :] + DN_HEADS).astype(F32)
    q_dn, k_dn, v_dn, g_dn, b_dn = _make_rowwise(f"{tag}_dn_pre", _f_dn_pre, 512)(
        (mixed, ab), (), (row(jnp.repeat(W["dn_a_log"], LANES)), row(jnp.repeat(W["dn_dt_bias"], LANES))), (e_a, e_b))
    o_dn = _make_delta_rule(f"{tag}_dn")(q_dn, k_dn, v_dn, g_dn, b_dn)

    qkv_dil = _make_rowwise(f"{tag}_dil_pre", _f_dil_pre, 256)(
        (dil_qkv,), (cos_h, sin_h), (row(W["dil_q_norm_g"]), row(W["dil_k_norm_g"])), ())
    n_groups = len(DIL_DILATIONS)
    o_lse = [_make_dil_attn(f"{tag}_dil{g}", d, DIL_TILE_ROWS[g])(qkv_dil[g], qkv_dil[n_groups + g], qkv_dil[2 * n_groups + g])
             for g, d in enumerate(DIL_DILATIONS)]

    ya, yb, yc = _make_rowwise(f"{tag}_merge_a", _f_merge_a, 256)(
        (y_a, z_a, o_dn, z_b, *[o for o, _ in o_lse], *[l for _, l in o_lse], z_c), (), (row(W["dn_out_norm_g"]),), ())
    (b0,) = _make_multi_linear(f"{tag}_br0", 1)(ya, (W["w_branch"][0],))
    (b1,) = _make_multi_linear(f"{tag}_br1", 1)(yb, (W["w_branch"][1],))
    (b2,) = _make_multi_linear(f"{tag}_br2", 1)(yc, (W["w_branch"][2],))
    (mix,) = _make_rowwise(f"{tag}_merge_b", _f_merge_b, 256)((b0, b1, b2, gl), (), (), ())
    return _make_resid_linear(f"{tag}_out")(x, mix, W["w_out"])


SHARDED = (("w_in", (D_MODEL, W_IN_SHARD)), ("mla_w_q_b", (MLA_Q_RANK, MLA_HEADS * MLA_QK // N_DEV)),
           ("mla_w_kv_b", (MLA_KV_RANK, MLA_HEADS * 2 * LANES // N_DEV)), ("w_branch", (3 * BRANCH_W, D_MODEL // N_DEV)),
           ("w_out", (D_MODEL // N_DEV, D_MODEL)), ("dn_conv_w", (DN_CONV, 3 * DN_HEADS * LANES // N_DEV)))
SMALL = (("norm_g", D_MODEL), ("mla_q_a_norm_g", MLA_Q_RANK), ("mla_kv_a_norm_g", MLA_KV_RANK), ("mla_q_norm_g", MLA_QK),
         ("mla_k_norm_g", MLA_QK), ("dn_a_log", DN_HEADS), ("dn_dt_bias", DN_HEADS), ("dn_out_norm_g", LANES),
         ("dil_q_norm_g", LANES), ("dil_k_norm_g", LANES))
WEIGHTS = ("norm_g", "w_in", "mla_q_a_norm_g", "mla_w_q_b", "mla_kv_a_norm_g", "mla_w_kv_b", "mla_q_norm_g", "mla_k_norm_g",
           "dn_conv_w", "dn_a_log", "dn_dt_bias", "dn_out_norm_g", "dil_q_norm_g", "dil_k_norm_g", "w_branch", "w_out")


def _round_up(n, m):
    return -(-n // m) * m


def _pack_vectors(pieces):
    return jnp.concatenate([jnp.pad(p, (0, _round_up(p.shape[0], LANES) - p.shape[0])) for p in pieces]).reshape(-1, LANES)


def _unpack_vectors(flat, sizes):
    out, off = [], 0
    flat = flat.reshape(-1)
    for n in sizes:
        out.append(flat[off:off + n])
        off += _round_up(n, LANES)
    return out


def _whole_weights(l, segs, g, small):
    W = dict(small)
    W["w_in_segments"] = tuple(s[l] for s in segs)
    W["mla_w_q_b"] = g["mla_w_q_b"][:, l].transpose(1, 0, 2).reshape(MLA_Q_RANK, -1)
    W["mla_w_kv_b"] = g["mla_w_kv_b"][:, l].transpose(1, 0, 2).reshape(MLA_KV_RANK, -1)
    W["w_branch"] = g["w_branch"][:, l].reshape(N_DEV, 3, BRANCH_W, -1).transpose(1, 2, 0, 3).reshape(3, BRANCH_W, D_MODEL)
    W["w_out"] = g["w_out"][:, l].reshape(D_MODEL, D_MODEL)
    W["dn_conv_w"] = g["dn_conv_w"][:, l].transpose(1, 0, 2).reshape(DN_CONV, -1)
    return W


def kernel(x, positions, norm_g, w_in, mla_q_a_norm_g, mla_w_q_b, mla_kv_a_norm_g, mla_w_kv_b, mla_q_norm_g, mla_k_norm_g, dn_conv_w, dn_a_log, dn_dt_bias, dn_out_norm_g, dil_q_norm_g, dil_k_norm_g, w_branch, w_out, loss_target, m_norm_g, m_w_in, m_mla_q_a_norm_g, m_mla_w_q_b, m_mla_kv_a_norm_g, m_mla_w_kv_b, m_mla_q_norm_g, m_mla_k_norm_g, m_dn_conv_w, m_dn_a_log, m_dn_dt_bias, m_dn_out_norm_g, m_dil_q_norm_g, m_dil_k_norm_g, m_w_branch, m_w_out, v_norm_g, v_w_in, v_mla_q_a_norm_g, v_mla_w_q_b, v_mla_kv_a_norm_g, v_mla_w_kv_b, v_mla_q_norm_g, v_mla_k_norm_g, v_dn_conv_w, v_dn_a_log, v_dn_dt_bias, v_dn_out_norm_g, v_dil_q_norm_g, v_dil_k_norm_g, v_w_branch, v_w_out):
    w = dict(norm_g=norm_g, w_in=w_in, mla_q_a_norm_g=mla_q_a_norm_g, mla_w_q_b=mla_w_q_b, mla_kv_a_norm_g=mla_kv_a_norm_g,
             mla_w_kv_b=mla_w_kv_b, mla_q_norm_g=mla_q_norm_g, mla_k_norm_g=mla_k_norm_g, dn_conv_w=dn_conv_w, dn_a_log=dn_a_log,
             dn_dt_bias=dn_dt_bias, dn_out_norm_g=dn_out_norm_g, dil_q_norm_g=dil_q_norm_g, dil_k_norm_g=dil_k_norm_g,
             w_branch=w_branch, w_out=w_out)
    m = dict(norm_g=m_norm_g, w_in=m_w_in, mla_q_a_norm_g=m_mla_q_a_norm_g, mla_w_q_b=m_mla_w_q_b, mla_kv_a_norm_g=m_mla_kv_a_norm_g,
             mla_w_kv_b=m_mla_w_kv_b, mla_q_norm_g=m_mla_q_norm_g, mla_k_norm_g=m_mla_k_norm_g, dn_conv_w=m_dn_conv_w,
             dn_a_log=m_dn_a_log, dn_dt_bias=m_dn_dt_bias, dn_out_norm_g=m_dn_out_norm_g, dil_q_norm_g=m_dil_q_norm_g,
             dil_k_norm_g=m_dil_k_norm_g, w_branch=m_w_branch, w_out=m_w_out)
    v = dict(norm_g=v_norm_g, w_in=v_w_in, mla_q_a_norm_g=v_mla_q_a_norm_g, mla_w_q_b=v_mla_w_q_b, mla_kv_a_norm_g=v_mla_kv_a_norm_g,
             mla_w_kv_b=v_mla_w_kv_b, mla_q_norm_g=v_mla_q_norm_g, mla_k_norm_g=v_mla_k_norm_g, dn_conv_w=v_dn_conv_w,
             dn_a_log=v_dn_a_log, dn_dt_bias=v_dn_dt_bias, dn_out_norm_g=v_dn_out_norm_g, dil_q_norm_g=v_dil_q_norm_g,
             dil_k_norm_g=v_dil_k_norm_g, w_branch=v_w_branch, w_out=v_w_out)
    x2, target = x[0], loss_target[0]
    pos = positions[0][:, None]

    view = lambda t, n, s: t[n].reshape((DEPTH,) + s)
    shards = [view(w, n, s) if n == "dn_conv_w" else view(w, n, s).astype(BF16) for n, s in SHARDED]
    gathered = dict(zip([n for n, _ in SHARDED], _all_gather(shards)))
    small = [{n: w[n][l] for n, _ in SMALL} for l in range(DEPTH)]
    tables = _rope_tables(pos, _rope_consts())

    def forward(gathered, small, x2):
        segs = _make_w_in_segments("w_in_segments")(gathered["w_in"])
        y = x2
        for l in range(DEPTH):
            y = _layer(f"l{l}", y, tables, _whole_weights(l, segs, gathered, small[l]))
        return y

    y, vjp = jax.vjp(forward, gathered, small, x2)
    loss_splat, dy = _loss_call(y, target)
    g_gathered, g_small, g_x = vjp(dy)
    loss = lax.psum(loss_splat[0, 0], ("x", "y", "c"))

    sizes = [k for _ in range(DEPTH) for _, k in SMALL]
    g_vec = _pack_vectors([g_small[l][n] for l in range(DEPTH) for n, _ in SMALL])
    sent = [g_gathered[n] for n, _ in SHARDED] + [jnp.broadcast_to(g_vec[None, None], (N_DEV, 1) + g_vec.shape)]
    parts = _all_to_all(sent)

    vec = lambda t: _pack_vectors([t[n][l] for l in range(DEPTH) for n, _ in SMALL])[None]
    outs = {}
    for (n, s), p in zip(SHARDED, parts[:-1]):
        outs[n] = [o.reshape(w[n].shape) for o in _adamw_call(f"adamw_{n}", p, view(w, n, s), view(m, n, s), view(v, n, s))]
    vec_outs = [_unpack_vectors(o, sizes) for o in _adamw_call("adamw_vectors", parts[-1], vec(w), vec(m), vec(v))]
    for i, (n, _) in enumerate(SMALL):
        outs[n] = [jnp.stack([o[l * len(SMALL) + i] for l in range(DEPTH)]) for o in vec_outs]
    return (loss, g_x[None], *[outs[n][k] for k in range(4) for n in WEIGHTS])
```

```python
import functools
import math

import jax
import jax.numpy as jnp
from jax import lax
from jax.experimental import pallas as pl
from jax.experimental.pallas import tpu as pltpu

F32 = jnp.float32
BF16 = jnp.bfloat16
HI = lax.Precision.HIGHEST
MESH = pl.DeviceIdType.MESH

N_DEV = 8
D_MODEL = 1024
DEPTH = 2
RMS_EPS = 1e-6
ROPE_THETA = 10000.0
LANES = 128
MLA_HEADS = 4
MLA_ROPE = 64
MLA_QK = 192
MLA_Q_RANK = 384
MLA_KV_RANK = 256
DN_HEADS = 4
DN_CHUNK = 64
DN_CONV = 4
DIL_HEADS = 12
DIL_GROUP_HEADS = 4
DIL_DILATIONS = (1, 4, 16)
DIL_BLOCK = 128
BRANCH_W = 512
IN_WIDTH = 11464
NEG = -1e30
VMEM_LIMIT = 56 * 1024 * 1024

ADAM_LR, ADAM_B1, ADAM_B2, ADAM_EPS, ADAM_WD, ADAM_STEP = 0.001, 0.9, 0.999, 1e-08, 0.01, 10

_SEG = {}
_off = 0
for _n, _w in (("q_lat", 384), ("c_kv", 256), ("k_pe", 64), ("z_a", 512), ("dn_qkv", 1536), ("dn_a", 4), ("dn_b", 4),
               ("z_b", 512), ("dil_qkv", 4608), ("z_c", 512), ("gate", 3072)):
    _SEG[_n] = (_off, _w)
    _off += _w
assert _off == IN_WIDTH


def _pcall(body, **kw):
    return pl.pallas_call(body, **kw)


def _params(sem=None):
    return pltpu.CompilerParams(dimension_semantics=sem, vmem_limit_bytes=VMEM_LIMIT)


def _tile(n, target, mult):
    t = (min(n, target) // mult) * mult
    while t >= mult:
        if n % t == 0:
            return t
        t -= mult
    return n


def _mm(name, a, b, mode, out_dtype=F32, acc=None, tm=1024, tn=512, tk=1024):
    if mode == "nn":
        (M, K), (_, N) = a.shape, b.shape
    elif mode == "nt":
        (M, K), (N, _) = a.shape, b.shape
    else:
        (K, M), (_, N) = a.shape, b.shape
    tm, tn, tk = _tile(M, tm, LANES), _tile(N, tn, LANES), _tile(K, tk, LANES)
    nk = K // tk
    dims = {"nn": (((1,), (0,)), ((), ())), "nt": (((1,), (1,)), ((), ())), "tn": (((0,), (0,)), ((), ()))}[mode]
    a_spec = pl.BlockSpec((tk, tm), lambda i, j, k: (k, i)) if mode == "tn" else pl.BlockSpec((tm, tk), lambda i, j, k: (i, k))
    b_spec = pl.BlockSpec((tn, tk), lambda i, j, k: (j, k)) if mode == "nt" else pl.BlockSpec((tk, tn), lambda i, j, k: (k, j))
    o_spec = pl.BlockSpec((tm, tn), lambda i, j, k: (i, j))
    has_acc = acc is not None

    def body(*refs):
        a_ref, b_ref = refs[:2]
        c_ref = refs[2] if has_acc else None
        o_ref = refs[3] if has_acc else refs[2]
        prod = lax.dot_general(a_ref[...].astype(BF16), b_ref[...].astype(BF16), dims, preferred_element_type=F32)
        if nk == 1:
            o_ref[...] = (prod + c_ref[...].astype(F32) if has_acc else prod).astype(out_dtype)
            return
        acc_ref = refs[-1]
        k = pl.program_id(2)

        @pl.when(k == 0)
        def _():
            acc_ref[...] = prod + c_ref[...].astype(F32) if has_acc else prod

        @pl.when(k > 0)
        def _():
            acc_ref[...] += prod

        @pl.when(k == nk - 1)
        def _():
            o_ref[...] = acc_ref[...].astype(out_dtype)

    ins = [a, b] + ([acc] if has_acc else [])
    in_specs = [a_spec, b_spec] + ([o_spec] if has_acc else [])
    return _pcall(body, name=name, grid=(M // tm, N // tn, nk), in_specs=in_specs, out_specs=o_spec,
                  out_shape=jax.ShapeDtypeStruct((M, N), out_dtype), scratch_shapes=[pltpu.VMEM((tm, tn), F32)] if nk > 1 else [],
                  compiler_params=_params(("parallel", "parallel", "arbitrary")))(*ins)


def _make_multi_linear(name, n):
    @jax.custom_vjp
    def op(h, ws):
        return tuple(_mm(f"{name}_fwd{i}", h, w, "nn") for i, w in enumerate(ws))

    def fwd(h, ws):
        return op(h, ws), (h, ws)

    def bwd(res, douts):
        h, ws = res
        dh = None
        for i, (w, d) in enumerate(zip(ws, douts)):
            dh = _mm(f"{name}_dh{i}", d, w, "nt", acc=dh, out_dtype=h.dtype if i == len(ws) - 1 else F32)
        dws = tuple(_mm(f"{name}_dw{i}", h, d, "tn", out_dtype=w.dtype) for i, (w, d) in enumerate(zip(ws, douts)))
        return dh, dws

    op.defvjp(fwd, bwd)
    return op


def _make_resid_linear(name):
    @jax.custom_vjp
    def op(x, a, w):
        return _mm(f"{name}_fwd", a, w, "nn", acc=x)

    def fwd(x, a, w):
        return op(x, a, w), (a, w)

    def bwd(res, dy):
        a, w = res
        return dy, _mm(f"{name}_da", dy, w, "nt", out_dtype=a.dtype), _mm(f"{name}_dw", a, dy, "tn", out_dtype=w.dtype)

    op.defvjp(fwd, bwd)
    return op


def _make_rowwise(name, f, tile):
    def specs(rows, aux, params, consts, t):
        row = [pl.BlockSpec((t, a.shape[1]), lambda i: (i, 0)) for a in (*rows, *aux)]
        full = [pl.BlockSpec(p.shape, lambda i: (0, 0)) for p in (*params, *consts)]
        return row, full

    def fwd_call(rows, aux, params, consts):
        S = rows[0].shape[0]
        t = min(tile, S)
        n_in = len(rows) + len(aux) + len(params) + len(consts)
        shp = lambda a: jax.ShapeDtypeStruct((t, a.shape[1]), a.dtype)
        outs = jax.eval_shape(f, *[shp(a) for a in (*rows, *aux)], *params, *consts)
        row_specs, full_specs = specs(rows, aux, params, consts, t)

        def body(*refs):
            res = f(*[r[...] for r in refs[:n_in]])
            for o_ref, o in zip(refs[n_in:], res):
                o_ref[...] = o

        return _pcall(body, name=f"{name}_fwd", grid=(S // t,), in_specs=row_specs + full_specs,
                      out_specs=[pl.BlockSpec((t, o.shape[1]), lambda i: (i, 0)) for o in outs],
                      out_shape=[jax.ShapeDtypeStruct((S, o.shape[1]), o.dtype) for o in outs],
                      compiler_params=_params(("parallel",)))(*rows, *aux, *params, *consts)

    def bwd_call(rows, aux, params, consts, douts):
        S = rows[0].shape[0]
        t = min(tile, S)
        nr, na, npar, nc, nd = len(rows), len(aux), len(params), len(consts), len(douts)
        row_specs, full_specs = specs(rows, aux, params, consts, t)

        def body(*refs):
            vals = [r[...] for r in refs[:nr + na + npar + nc]]
            rv, av = vals[:nr], vals[nr:nr + na]
            pv, cv = vals[nr + na:nr + na + npar], vals[nr + na + npar:]
            dv = tuple(r[...] for r in refs[nr + na + npar + nc:nr + na + npar + nc + nd])
            out_refs = refs[nr + na + npar + nc + nd:]
            _, vjp = jax.vjp(lambda *rp: f(*rp[:nr], *av, *rp[nr:], *cv), *rv, *pv)
            grads = vjp(dv)
            for o_ref, g in zip(out_refs[:nr], grads[:nr]):
                o_ref[...] = g
            first = pl.program_id(0) == 0
            for o_ref, g in zip(out_refs[nr:], grads[nr:]):
                @pl.when(first)
                def _(o_ref=o_ref):
                    o_ref[...] = jnp.zeros_like(o_ref)
                o_ref[...] += g

        res = _pcall(body, name=f"{name}_bwd", grid=(S // t,),
                     in_specs=row_specs + full_specs + [pl.BlockSpec((t, d.shape[1]), lambda i: (i, 0)) for d in douts],
                     out_specs=[pl.BlockSpec((t, a.shape[1]), lambda i: (i, 0)) for a in rows]
                     + [pl.BlockSpec(p.shape, lambda i: (0, 0)) for p in params],
                     out_shape=[jax.ShapeDtypeStruct(a.shape, a.dtype) for a in (*rows, *params)],
                     compiler_params=_params(("arbitrary",)))(*rows, *aux, *params, *consts, *douts)
        return tuple(res[:nr]), tuple(res[nr:])

    @jax.custom_vjp
    def op(rows, aux, params, consts):
        return tuple(fwd_call(rows, aux, params, consts))

    def fwd(rows, aux, params, consts):
        return op(rows, aux, params, consts), (rows, aux, params, consts)

    def bwd(res, douts):
        rows, aux, params, consts = res
        drows, dparams = bwd_call(rows, aux, params, consts, tuple(douts))
        zeros = lambda xs: tuple(jnp.zeros_like(a) for a in xs)
        return drows, zeros(aux), dparams, zeros(consts)

    op.defvjp(fwd, bwd)
    return op


@jax.custom_vjp
def _swap_halves(x):
    return pltpu.roll(x, LANES // 2, 1)


_swap_halves.defvjp(lambda x: (_swap_halves(x), None), lambda _, g: (_swap_halves(g),))


def _rope(x, cos_t, sin_t):
    return x * cos_t + _swap_halves(x) * sin_t


def _rms(x, g, n=None):
    n = x.shape[-1] if n is None else n
    return x * lax.rsqrt(jnp.sum(x * x, axis=-1, keepdims=True) * (1.0 / n) + RMS_EPS) * g


def _heads(x):
    return [x[:, i * LANES:(i + 1) * LANES] for i in range(x.shape[1] // LANES)]


def _cat(xs):
    return jnp.concatenate(xs, axis=1)


def _silu(x):
    return x * jax.nn.sigmoid(x)


def _f_norm(x, g):
    return (_rms(x, g).astype(BF16),)


def _f_mla_a(q_lat, c_kv, kpe, cos_p, sin_p, qa_g, kva_g, kpe_g):
    kp = _rope(_rms(kpe, kpe_g, MLA_ROPE), cos_p, sin_p)
    return _rms(q_lat, qa_g).astype(BF16), _rms(c_kv, kva_g).astype(BF16), _cat([kp] * MLA_HEADS)


def _f_mla_b(q8, kn_raw, cos_p, sin_p, qn_g, qp_g, kn_g):
    hs = _heads(q8)
    qn = _cat([_rms(h, qn_g) for h in hs[:MLA_HEADS]])
    qp = _cat([_rope(_rms(h, qp_g, MLA_ROPE), cos_p, sin_p) for h in hs[MLA_HEADS:]])
    kn = _cat([_rms(h, kn_g) for h in _heads(kn_raw)])
    return qn, qp, kn


def _softplus(x):
    return jnp.maximum(x, 0.0) + jnp.log(1.0 + jnp.exp(-jnp.abs(x)))


def _l2n(x):
    return x * lax.rsqrt(jnp.sum(x * x, axis=-1, keepdims=True) + 1e-6)


def _f_dn_pre(mixed, ab, alog_f, dtb_f, e_a, e_b):
    hs = _heads(mixed)
    q = _cat([_l2n(h) * (LANES ** -0.5) for h in hs[:DN_HEADS]])
    k = _cat([_l2n(h) for h in hs[DN_HEADS:2 * DN_HEADS]])
    v = _cat(hs[2 * DN_HEADS:])
    a_f = jnp.dot(ab, e_a, precision=HI, preferred_element_type=F32)
    b_f = jnp.dot(ab, e_b, precision=HI, preferred_element_type=F32)
    g = -jnp.exp(alog_f) * _softplus(a_f + dtb_f)
    return q, k, v, g, jax.nn.sigmoid(b_f)


def _f_dil_pre(qkv, cos_h, sin_h, q_g, k_g):
    hs = _heads(qkv)
    q = [_rope(_rms(h, q_g), cos_h, sin_h) for h in hs[:DIL_HEADS]]
    k = [_rope(_rms(h, k_g), cos_h, sin_h) for h in hs[DIL_HEADS:2 * DIL_HEADS]]
    v = hs[2 * DIL_HEADS:]
    group = lambda xs, g: _cat(xs[g * DIL_GROUP_HEADS:(g + 1) * DIL_GROUP_HEADS])
    return tuple(group(xs, g) for xs in (q, k, v) for g in range(len(DIL_DILATIONS)))


def _f_merge_a(y_a, z_a, o_dn, z_b, o0, o1, o2, l0, l1, l2, z_c, out_g):
    y_b = _cat([_rms(h, out_g) for h in _heads(o_dn)])
    os_, ls = [_heads(o) for o in (o0, o1, o2)], [_heads(l) for l in (l0, l1, l2)]
    y_c = []
    for j in range(DIL_GROUP_HEADS):
        l3 = [ls[g][j] for g in range(3)]
        m = jnp.maximum(jnp.maximum(l3[0], l3[1]), l3[2])
        e3 = [jnp.exp(l - m) for l in l3]
        den = e3[0] + e3[1] + e3[2]
        y_c.append(sum(e3[g] * os_[g][j] for g in range(3)) / den)
    return tuple(t.astype(BF16) for t in (y_a * _silu(z_a), y_b * _silu(z_b), _cat(y_c) * _silu(z_c)))


def _f_merge_b(b0, b1, b2, gl):
    gs = [jax.nn.sigmoid(gl[:, i * D_MODEL:(i + 1) * D_MODEL]) for i in range(3)]
    return ((gs[0] * b0 + gs[1] * b1 + gs[2] * b2).astype(BF16),)


def _rope_tables(pos, inv_sign):
    S = pos.shape[0]
    t = min(S, 1024)

    def body(p_ref, c_ref, cp, sp, ch, sh):
        p = p_ref[...].astype(F32)
        c = c_ref[...]
        ang_p, ang_h = p * c[0:1], p * c[2:3]
        cp[...] = jnp.cos(ang_p) * jnp.abs(c[1:2])
        sp[...] = jnp.sin(ang_p) * c[1:2]
        ch[...] = jnp.cos(ang_h)
        sh[...] = jnp.sin(ang_h) * c[3:4]

    row = pl.BlockSpec((t, LANES), lambda i: (i, 0))
    return _pcall(body, name="rope_tables", grid=(S // t,),
                  in_specs=[pl.BlockSpec((t, 1), lambda i: (i, 0)), pl.BlockSpec((4, LANES), lambda i: (0, 0))],
                  out_specs=[row] * 4, out_shape=[jax.ShapeDtypeStruct((S, LANES), F32)] * 4,
                  compiler_params=_params(("parallel",)))(pos, inv_sign)


def _rope_consts():
    half_p, half_h = MLA_ROPE // 2, LANES // 2
    inv_p = 1.0 / (ROPE_THETA ** (jnp.arange(0, MLA_ROPE, 2, dtype=F32) / MLA_ROPE))
    inv_h = 1.0 / (ROPE_THETA ** (jnp.arange(0, LANES, 2, dtype=F32) / LANES))
    z = jnp.zeros((half_p,), F32)
    o = jnp.ones((half_p,), F32)
    return jnp.stack([jnp.concatenate([inv_p, z, inv_p, z]), jnp.concatenate([-o, z, o, z]),
                      jnp.concatenate([inv_h, inv_h]), jnp.concatenate([-jnp.ones((half_h,), F32), jnp.ones((half_h,), F32)])])


def _shift_rows(x, s, up):
    n = x.shape[0]
    r = lax.broadcasted_iota(jnp.int32, x.shape, 0)
    if up:
        return jnp.where(r < n - s, pltpu.roll(x, n - s, 0), 0.0)
    return jnp.where(r >= s, pltpu.roll(x, s, 0), 0.0)


def _make_shift(s):
    @jax.custom_vjp
    def sh(x):
        return _shift_rows(x, s, False)

    sh.defvjp(lambda x: (sh(x), None), lambda _, g: (_shift_rows(g, s, True),))
    return sh


def _f_conv(x, w):
    y = x * w[DN_CONV - 1:DN_CONV]
    for j in range(DN_CONV - 1):
        y = y + _make_shift(DN_CONV - 1 - j)(x) * w[j:j + 1]
    return _silu(y)


def _make_conv(name):
    def call(x, w, dy=None):
        S, C = x.shape
        col = pl.BlockSpec((S, LANES), lambda i: (0, i))
        wsp = pl.BlockSpec((DN_CONV, LANES), lambda i: (0, i))
        if dy is None:
            def body(x_ref, w_ref, o_ref):
                o_ref[...] = _f_conv(x_ref[...], w_ref[...])
            return _pcall(body, name=f"{name}_fwd", grid=(C // LANES,), in_specs=[col, wsp], out_specs=col,
                          out_shape=jax.ShapeDtypeStruct(x.shape, F32), compiler_params=_params(("parallel",)))(x, w)

        def body(x_ref, w_ref, dy_ref, dx_ref, dw_ref):
            _, vjp = jax.vjp(_f_conv, x_ref[...], w_ref[...])
            dx_ref[...], dw_ref[...] = vjp(dy_ref[...])
        return _pcall(body, name=f"{name}_bwd", grid=(C // LANES,), in_specs=[col, wsp, col], out_specs=[col, wsp],
                      out_shape=[jax.ShapeDtypeStruct(x.shape, F32), jax.ShapeDtypeStruct(w.shape, F32)],
                      compiler_params=_params(("parallel",)))(x, w, dy)

    @jax.custom_vjp
    def op(x, w):
        return call(x, w)

    op.defvjp(lambda x, w: (op(x, w), (x, w)), lambda res, dy: tuple(call(*res, dy)))
    return op


def _dot_nt(a, b):
    return lax.dot_general(a.astype(BF16), b.astype(BF16), (((1,), (1,)), ((), ())), preferred_element_type=F32)


def _dot_nn(a, b):
    return jnp.dot(a.astype(BF16), b.astype(BF16), preferred_element_type=F32)


def _dot_tn(a, b):
    return lax.dot_general(a.astype(BF16), b.astype(BF16), (((0,), (0,)), ((), ())), preferred_element_type=F32)


def _mla_scores(qn_r, qp_r, kn_r, kp_r, diagonal):
    scale = MLA_QK ** -0.5
    s = _dot_nt(qn_r[...] * scale, kn_r[...]) + _dot_nt(qp_r[...] * scale, kp_r[...])
    if diagonal:
        r = lax.broadcasted_iota(jnp.int32, s.shape, 0)
        c = lax.broadcasted_iota(jnp.int32, s.shape, 1)
        s = jnp.where(c <= r, s, NEG)
    return s


def _on_causal_pairs(qi, ki, step):
    @pl.when(ki < qi)
    def _():
        step(False)

    @pl.when(ki == qi)
    def _():
        step(True)


def _make_mla_attn(name):
    scale = MLA_QK ** -0.5

    def fwd_call(qn, qp, kn, kp, v):
        S = qn.shape[0]
        t = min(S, 512)
        n = S // t
        qs = pl.BlockSpec((t, LANES), lambda h, i, j: (i, h))
        ks = pl.BlockSpec((t, LANES), lambda h, i, j: (jnp.minimum(j, i), h))

        def body(qn_r, qp_r, kn_r, kp_r, v_r, o_r, lse_r, m_s, l_s, acc_s):
            qi, ki = pl.program_id(1), pl.program_id(2)

            @pl.when(ki == 0)
            def _():
                m_s[...] = jnp.full_like(m_s, NEG)
                l_s[...] = jnp.zeros_like(l_s)
                acc_s[...] = jnp.zeros_like(acc_s)

            def step(diagonal):
                s = _mla_scores(qn_r, qp_r, kn_r, kp_r, diagonal)
                m_old = m_s[...]
                m_new = jnp.maximum(m_old, jnp.max(s, axis=-1, keepdims=True))
                p = jnp.exp(s - m_new[:, :1])
                alpha = jnp.exp(m_old - m_new)
                l_s[...] = alpha * l_s[...] + jnp.sum(p, axis=-1, keepdims=True)
                acc_s[...] = alpha * acc_s[...] + _dot_nn(p, v_r[...])
                m_s[...] = m_new
            _on_causal_pairs(qi, ki, step)

            @pl.when(ki == n - 1)
            def _():
                o_r[...] = acc_s[...] / l_s[...]
                lse_r[...] = m_s[...] + jnp.log(l_s[...])

        return _pcall(body, name=f"{name}_fwd", grid=(MLA_HEADS, n, n), in_specs=[qs, qs, ks, ks, ks], out_specs=[qs, qs],
                      out_shape=[jax.ShapeDtypeStruct((S, MLA_HEADS * LANES), F32)] * 2,
                      scratch_shapes=[pltpu.VMEM((t, LANES), F32)] * 3,
                      compiler_params=_params(("parallel", "parallel", "arbitrary")))(qn, qp, kn, kp, v)

    def dq_call(qn, qp, kn, kp, v, o, lse, do):
        S = qn.shape[0]
        t = min(S, 512)
        n = S // t
        qs = pl.BlockSpec((t, LANES), lambda h, i, j: (i, h))
        ks = pl.BlockSpec((t, LANES), lambda h, i, j: (jnp.minimum(j, i), h))

        def body(qn_r, qp_r, kn_r, kp_r, v_r, o_r, lse_r, do_r, dqn_r, dqp_r, dl_r, dqn_s, dqp_s):
            qi, ki = pl.program_id(1), pl.program_id(2)

            @pl.when(ki == 0)
            def _():
                dqn_s[...] = jnp.zeros_like(dqn_s)
                dqp_s[...] = jnp.zeros_like(dqp_s)
                dl_r[...] = jnp.broadcast_to(jnp.sum(do_r[...] * o_r[...], axis=-1, keepdims=True), dl_r.shape)

            def step(diagonal):
                p = jnp.exp(_mla_scores(qn_r, qp_r, kn_r, kp_r, diagonal) - lse_r[...][:, :1])
                ds = p * (_dot_nt(do_r[...], v_r[...]) - dl_r[...][:, :1])
                dqn_s[...] += _dot_nn(ds, kn_r[...])
                dqp_s[...] += _dot_nn(ds, kp_r[...])
            _on_causal_pairs(qi, ki, step)

            @pl.when(ki == n - 1)
            def _():
                dqn_r[...] = dqn_s[...] * scale
                dqp_r[...] = dqp_s[...] * scale

        return _pcall(body, name=f"{name}_dq", grid=(MLA_HEADS, n, n), in_specs=[qs, qs, ks, ks, ks, qs, qs, qs],
                      out_specs=[qs, qs, qs], out_shape=[jax.ShapeDtypeStruct((S, MLA_HEADS * LANES), F32)] * 3,
                      scratch_shapes=[pltpu.VMEM((t, LANES), F32)] * 2,
                      compiler_params=_params(("parallel", "parallel", "arbitrary")))(qn, qp, kn, kp, v, o, lse, do)

    def dkv_call(qn, qp, kn, kp, v, lse, do, dl):
        S = qn.shape[0]
        t = min(S, 512)
        n = S // t
        ks = pl.BlockSpec((t, LANES), lambda h, j, i: (j, h))
        qs = pl.BlockSpec((t, LANES), lambda h, j, i: (jnp.maximum(i, j), h))

        def body(qn_r, qp_r, kn_r, kp_r, v_r, lse_r, do_r, dl_r, dkn_r, dkp_r, dv_r, dkn_s, dkp_s, dv_s):
            ki, qi = pl.program_id(1), pl.program_id(2)

            @pl.when(qi == 0)
            def _():
                dkn_s[...] = jnp.zeros_like(dkn_s)
                dkp_s[...] = jnp.zeros_like(dkp_s)
                dv_s[...] = jnp.zeros_like(dv_s)

            def step(diagonal):
                p = jnp.exp(_mla_scores(qn_r, qp_r, kn_r, kp_r, diagonal) - lse_r[...][:, :1])
                ds = p * (_dot_nt(do_r[...], v_r[...]) - dl_r[...][:, :1])
                dv_s[...] += _dot_tn(p, do_r[...])
                dkn_s[...] += _dot_tn(ds, qn_r[...] * scale)
                dkp_s[...] += _dot_tn(ds, qp_r[...] * scale)
            _on_causal_pairs(qi, ki, step)

            @pl.when(qi == n - 1)
            def _():
                dkn_r[...] = dkn_s[...]
                dkp_r[...] = dkp_s[...]
                dv_r[...] = dv_s[...]

        return _pcall(body, name=f"{name}_dkv", grid=(MLA_HEADS, n, n), in_specs=[qs, qs, ks, ks, ks, qs, qs, qs],
                      out_specs=[ks, ks, ks], out_shape=[jax.ShapeDtypeStruct((S, MLA_HEADS * LANES), F32)] * 3,
                      scratch_shapes=[pltpu.VMEM((t, LANES), F32)] * 3,
                      compiler_params=_params(("parallel", "parallel", "arbitrary")))(qn, qp, kn, kp, v, lse, do, dl)

    @jax.custom_vjp
    def op(qn, qp, kn, kp, v):
        return fwd_call(qn, qp, kn, kp, v)[0]

    def fwd(qn, qp, kn, kp, v):
        o, lse = fwd_call(qn, qp, kn, kp, v)
        return o, (qn, qp, kn, kp, v, o, lse)

    def bwd(res, do):
        qn, qp, kn, kp, v, o, lse = res
        dqn, dqp, dl = dq_call(qn, qp, kn, kp, v, o, lse, do)
        dkn, dkp, dv = dkv_call(qn, qp, kn, kp, v, lse, do, dl)
        return dqn, dqp, dkn, dkp, dv

    op.defvjp(fwd, bwd)
    return op


def _dil_block(q, kp, kc, vp, vc, has_prev):
    scale = LANES ** -0.5
    r = lax.broadcasted_iota(jnp.int32, (DIL_BLOCK, DIL_BLOCK), 0)
    c = lax.broadcasted_iota(jnp.int32, (DIL_BLOCK, DIL_BLOCK), 1)
    s_c = jnp.where(c <= r, _dot_nt(q, kc) * scale, NEG)
    s_p = jnp.where((c >= r) & has_prev, _dot_nt(q, kp) * scale, NEG)
    m = jnp.maximum(jnp.max(s_c, axis=-1, keepdims=True), jnp.max(s_p, axis=-1, keepdims=True))
    e_c, e_p = jnp.exp(s_c - m), jnp.exp(s_p - m)
    den = jnp.sum(e_c, axis=-1, keepdims=True) + jnp.sum(e_p, axis=-1, keepdims=True)
    o = (_dot_nn(e_c, vc) + _dot_nn(e_p, vp)) / den
    return o, jnp.broadcast_to(m + jnp.log(den), o.shape)


DIL_TILE_ROWS = (1024, 1024, 2048)


def _make_dil_attn(name, d, tile_rows):
    def call(q, k, v, cts=None):
        S = q.shape[0]
        span = DIL_BLOCK * d
        G = max(1, min(tile_rows, S) // span)
        n = S // (G * span)
        at = (lambda i: i) if cts is None else (lambda i: n - 1 - i)
        tile = pl.BlockSpec((G * span, LANES), lambda h, i: (at(i), h))
        before = pl.BlockSpec((span, LANES), lambda h, i: (jnp.maximum(at(i) * G - 1, 0), h))

        def rows(r, j):
            return pl.ds(j * DIL_BLOCK, DIL_BLOCK) if d == 1 else pl.ds(r + j * span, DIL_BLOCK, stride=d)

        def over_residues(fn):
            if d == 1:
                fn(0)
            else:
                lax.fori_loop(0, d, lambda r, c: (fn(r), c)[1], 0)

        def block_inputs(r, j, q_r, kb_r, k_r, vb_r, v_r):
            kp = kb_r[rows(r, 0), :] if j == 0 else k_r[rows(r, j - 1), :]
            vp = vb_r[rows(r, 0), :] if j == 0 else v_r[rows(r, j - 1), :]
            return q_r[rows(r, j), :], kp, k_r[rows(r, j), :], vp, v_r[rows(r, j), :]

        if cts is None:
            def body(q_r, kb_r, k_r, vb_r, v_r, o_r, lse_r):
                first = at(pl.program_id(1)) * G

                def residue(r):
                    for j in range(G):
                        o_r[rows(r, j), :], lse_r[rows(r, j), :] = _dil_block(
                            *block_inputs(r, j, q_r, kb_r, k_r, vb_r, v_r), first + j > 0)
                over_residues(residue)
            return _pcall(body, name=f"{name}_fwd", grid=(DIL_GROUP_HEADS, n), in_specs=[tile, before, tile, before, tile],
                          out_specs=[tile, tile], out_shape=[jax.ShapeDtypeStruct(q.shape, F32)] * 2,
                          compiler_params=_params(("parallel", "parallel")))(q, k, k, v, v)

        def body(q_r, kb_r, k_r, vb_r, v_r, do_r, dl_r, dq_r, dk_r, dv_r, ck_s, cv_s):
            first = at(pl.program_id(1)) * G

            @pl.when(pl.program_id(1) == 0)
            def _():
                ck_s[...] = jnp.zeros_like(ck_s)
                cv_s[...] = jnp.zeros_like(cv_s)

            def residue(r):
                owed = None
                for j in range(G):
                    hp = first + j > 0
                    _, vjp = jax.vjp(lambda *a: _dil_block(*a, hp), *block_inputs(r, j, q_r, kb_r, k_r, vb_r, v_r))
                    dq, dkp, dkc, dvp, dvc = vjp((do_r[rows(r, j), :], dl_r[rows(r, j), :]))
                    dq_r[rows(r, j), :] = dq
                    if j == G - 1:
                        dkc, dvc = dkc + ck_s[rows(r, 0), :], dvc + cv_s[rows(r, 0), :]
                    dk_r[rows(r, j), :], dv_r[rows(r, j), :] = dkc, dvc
                    if j == 0:
                        owed = (dkp, dvp)
                    else:
                        dk_r[rows(r, j - 1), :] += dkp
                        dv_r[rows(r, j - 1), :] += dvp
                ck_s[rows(r, 0), :], cv_s[rows(r, 0), :] = owed
            over_residues(residue)
        return _pcall(body, name=f"{name}_bwd", grid=(DIL_GROUP_HEADS, n), in_specs=[tile, before, tile, before, tile, tile, tile],
                      out_specs=[tile] * 3, out_shape=[jax.ShapeDtypeStruct(q.shape, F32)] * 3,
                      scratch_shapes=[pltpu.VMEM((span, LANES), F32)] * 2,
                      compiler_params=_params(("parallel", "arbitrary")))(q, k, k, v, v, *cts)

    @jax.custom_vjp
    def op(q, k, v):
        return tuple(call(q, k, v))

    op.defvjp(lambda q, k, v: (op(q, k, v), (q, k, v)), lambda res, cts: tuple(call(*res, cts=cts)))
    return op


def _pdot(a, b, dims):
    return lax.dot_general(a, b, (dims, ((), ())), precision=lax.Precision.HIGH, preferred_element_type=F32)


DN_LOCAL_CHUNKS = 4


DN_BLOCK_HEADS = 4
DN_BLOCK = DN_BLOCK_HEADS * DN_CHUNK


def _dn_local(q, k, v, g, b):
    C, R = DN_CHUNK, DN_BLOCK
    r = lax.broadcasted_iota(jnp.int32, (R, R), 0)
    c = lax.broadcasted_iota(jnp.int32, (R, R), 1)
    same_head = (r // C) == (c // C)
    incl, strict = same_head & (r >= c), same_head & (r > c)
    eye = (r == c).astype(F32)
    avg = jnp.full((R, LANES), 1.0 / LANES, F32)
    rc = lax.broadcasted_iota(jnp.int32, (C, C), 0) >= lax.broadcasted_iota(jnp.int32, (C, C), 1)
    gc_lanes = _pdot(rc.astype(F32), g, ((1,), (0,)))
    us, ws, qes, kds, qks = [], [], [], [], []
    for first in range(0, DN_HEADS, DN_BLOCK_HEADS):
        stack = lambda x: jnp.concatenate(_heads(x)[first:first + DN_BLOCK_HEADS], axis=0)
        unstack = lambda x: [x[p * C:(p + 1) * C] for p in range(DN_BLOCK_HEADS)]
        gc, q_s, k_s, v_s, b_s = (stack(x) for x in (gc_lanes, q, k, v, b))
        gc_j = _pdot(avg, gc, ((1,), (1,)))
        decay = jnp.exp(jnp.where(incl, _cat([gc] * (R // LANES)) - gc_j, NEG))
        kb = k_s * b_s
        kk = _pdot(jnp.concatenate([kb, q_s], axis=0), k_s, ((1,), (1,)))
        a = jnp.where(strict, kk[:R] * decay, 0.0)
        inv, pw = eye - a, a
        for _ in range(5):
            pw = _pdot(pw, pw, ((1,), (0,)))
            inv = inv + _pdot(inv, pw, ((1,), (0,)))
        eg = jnp.exp(gc)
        uw = _pdot(inv, _cat([v_s * b_s, kb * eg]), ((1,), (0,)))
        g_last = jnp.concatenate([jnp.broadcast_to(x[C - 1:C], (C, LANES)) for x in unstack(gc)], axis=0)
        us += unstack(uw[:, :LANES])
        ws += unstack(uw[:, LANES:])
        qes += unstack(q_s * eg)
        kds += unstack(k_s * jnp.exp(g_last - gc))
        qks.append(kk[R:] * decay)
    egl = jnp.broadcast_to(jnp.exp(gc_lanes[C - 1:C]), (8, DN_HEADS * LANES))
    return _cat(us), _cat(ws), _cat(qes), _cat(kds), jnp.concatenate(qks, axis=0), egl


def _dn_scan(u, w, qe, kd, qk, egl, state):
    C = DN_CHUNK
    heads = [slice(h * LANES, (h + 1) * LANES) for h in range(DN_HEADS)]
    ws = [_pdot(jnp.concatenate([w[:, sl], qe[:, sl]], axis=0), state[sl, :], ((1,), (0,))) for sl in heads]
    v_new = [u[:, sl] - x[:C] for sl, x in zip(heads, ws)]
    local = []
    for i, first in enumerate(range(0, DN_HEADS, DN_BLOCK_HEADS)):
        y = _pdot(qk[i * DN_BLOCK:(i + 1) * DN_BLOCK], jnp.concatenate(v_new[first:first + DN_BLOCK_HEADS], axis=0), ((1,), (0,)))
        local += [y[p * C:(p + 1) * C] for p in range(DN_BLOCK_HEADS)]
    o = _cat([x[C:] + y for x, y in zip(ws, local)])
    states = [state[sl, :] * egl[0:1, sl] + _pdot(kd[:, sl], vn, ((0,), (0,))) for sl, vn in zip(heads, v_new)]
    return o, jnp.concatenate(states, axis=0)


def _make_delta_rule(name):
    W = DN_HEADS * LANES
    QK = DN_HEADS * DN_CHUNK

    def local_call(ins, cts=None):
        S = ins[0].shape[0]
        n = S // DN_CHUNK
        per = math.gcd(DN_LOCAL_CHUNKS, n)
        row = pl.BlockSpec((per * DN_CHUNK, W), lambda i: (i, 0))
        qkb = pl.BlockSpec((per * QK, DN_BLOCK), lambda i: (i, 0))
        eg = pl.BlockSpec((per, 8, W), lambda i: (i, 0, 0))
        rows = lambda j: slice(j * DN_CHUNK, (j + 1) * DN_CHUNK)
        qk_rows = lambda j: slice(j * QK, (j + 1) * QK)
        out_rows = [rows, rows, rows, rows, qk_rows]

        if cts is None:
            def body(*refs):
                for j in range(per):
                    res = _dn_local(*[r[rows(j), :] for r in refs[:5]])
                    for o_r, o, at_ in zip(refs[5:10], res[:5], out_rows):
                        o_r[at_(j), :] = o
                    refs[10][j] = res[5]
            return _pcall(body, name=f"{name}_local_fwd", grid=(n // per,), in_specs=[row] * 5, out_specs=[row] * 4 + [qkb, eg],
                          out_shape=[jax.ShapeDtypeStruct((S, W), F32)] * 4
                          + [jax.ShapeDtypeStruct((n * QK, DN_BLOCK), F32), jax.ShapeDtypeStruct((n, 8, W), F32)],
                          compiler_params=_params(("parallel",)))(*ins)

        def body(*refs):
            for j in range(per):
                _, vjp = jax.vjp(_dn_local, *[r[rows(j), :] for r in refs[:5]])
                grads = vjp(tuple(r[at_(j), :] for r, at_ in zip(refs[5:10], out_rows)) + (refs[10][j],))
                for o_r, o in zip(refs[11:], grads):
                    o_r[rows(j), :] = o
        return _pcall(body, name=f"{name}_local_bwd", grid=(n // per,), in_specs=[row] * 9 + [qkb, eg], out_specs=[row] * 5,
                      out_shape=[jax.ShapeDtypeStruct((S, W), F32)] * 5, compiler_params=_params(("parallel",)))(*ins, *cts)

    def scan_call(ins, saved=None, do=None):
        S = ins[0].shape[0]
        n = S // DN_CHUNK
        at = (lambda i: i) if do is None else (lambda i: n - 1 - i)
        row = pl.BlockSpec((DN_CHUNK, W), lambda i: (at(i), 0))
        qkb = pl.BlockSpec((QK, DN_BLOCK), lambda i: (at(i), 0))
        eg = pl.BlockSpec((None, 8, W), lambda i: (at(i), 0, 0))
        st = pl.BlockSpec((None, W, LANES), lambda i: (at(i), 0, 0))

        if do is None:
            def body(*refs):
                o_r, st_r, s_s = refs[6:]

                @pl.when(pl.program_id(0) == 0)
                def _():
                    s_s[...] = jnp.zeros_like(s_s)
                st_r[...] = s_s[...]
                o_r[...], s_s[...] = _dn_scan(*[r[...] for r in refs[:6]], s_s[...])
            return _pcall(body, name=f"{name}_scan_fwd", grid=(n,), in_specs=[row] * 4 + [qkb, eg], out_specs=[row, st],
                          out_shape=[jax.ShapeDtypeStruct((S, W), F32), jax.ShapeDtypeStruct((n, W, LANES), F32)],
                          scratch_shapes=[pltpu.VMEM((W, LANES), F32)], compiler_params=_params(("arbitrary",)))(*ins)

        def body(*refs):
            st_r, do_r = refs[6:8]
            outs, ds_s = refs[8:14], refs[14]

            @pl.when(pl.program_id(0) == 0)
            def _():
                ds_s[...] = jnp.zeros_like(ds_s)
            _, vjp = jax.vjp(_dn_scan, *[r[...] for r in refs[:6]], st_r[...])
            *grads, ds = vjp((do_r[...], ds_s[...]))
            for o_r, gval in zip(outs, grads):
                o_r[...] = gval
            ds_s[...] = ds
        return _pcall(body, name=f"{name}_scan_bwd", grid=(n,), in_specs=[row] * 4 + [qkb, eg, st, row], out_specs=[row] * 4 + [qkb, eg],
                      out_shape=[jax.ShapeDtypeStruct((S, W), F32)] * 4
                      + [jax.ShapeDtypeStruct((n * QK, DN_BLOCK), F32), jax.ShapeDtypeStruct((n, 8, W), F32)],
                      scratch_shapes=[pltpu.VMEM((W, LANES), F32)], compiler_params=_params(("arbitrary",)))(*ins, saved, do)

    @jax.custom_vjp
    def local(q, k, v, g, b):
        return tuple(local_call((q, k, v, g, b)))

    local.defvjp(lambda *a: (local(*a), a), lambda res, cts: tuple(local_call(res, tuple(cts))))

    @jax.custom_vjp
    def scan(u, w, qe, kd, qk, egl):
        return scan_call((u, w, qe, kd, qk, egl))[0]

    def scan_fwd(*a):
        o, states = scan_call(a)
        return o, (a, states)

    scan.defvjp(scan_fwd, lambda res, do: tuple(scan_call(res[0], res[1], do)))
    return lambda q, k, v, g, b: scan(*local(q, k, v, g, b))


def _loss_call(y, target):
    S, D = y.shape
    t = min(S, 512)
    n = S // t
    row = pl.BlockSpec((t, D), lambda i: (i, 0))

    def body(y_r, t_r, loss_r, dy_r, acc_s):
        i = pl.program_id(0)

        @pl.when(i == 0)
        def _():
            acc_s[...] = jnp.zeros_like(acc_s)
        err = y_r[...] - t_r[...]
        dy_r[...] = err * (1.0 / D)
        acc_s[...] += jnp.sum(err * err, axis=0, keepdims=True)

        @pl.when(i == n - 1)
        def _():
            loss_r[...] = jnp.broadcast_to(jnp.sum(acc_s[...], axis=1, keepdims=True) * (0.5 / D), loss_r.shape)

    return _pcall(body, name="loss_head", grid=(n,), in_specs=[row, row],
                  out_specs=[pl.BlockSpec((8, LANES), lambda i: (0, 0)), row],
                  out_shape=[jax.ShapeDtypeStruct((8, LANES), F32), jax.ShapeDtypeStruct((S, D), F32)],
                  scratch_shapes=[pltpu.VMEM((1, D), F32)], compiler_params=_params(("arbitrary",)))(y, target)


def _adamw_call(name, parts, w, m, v, rows=128):
    L, R, C = w.shape
    assert len(parts) == L
    t = _tile(R, rows, 8)
    row = pl.BlockSpec((None, t, C), lambda l, i: (l, i, 0))
    part = lambda k: pl.BlockSpec((N_DEV, t, C), lambda l, i: (0, jnp.where(l == k, i, 0), 0))

    def body(*refs):
        p_refs, (w_r, m_r, v_r, g_r, d_r, nm_r, nv_r) = refs[:L], refs[L:]

        def update(p_r):
            g = p_r[0].astype(F32)
            for s in range(1, N_DEV):
                g = g + p_r[s].astype(F32)
            m_new = ADAM_B1 * m_r[...] + (1.0 - ADAM_B1) * g
            v_new = ADAM_B2 * v_r[...] + (1.0 - ADAM_B2) * (g * g)
            m_hat = m_new / (1.0 - ADAM_B1 ** ADAM_STEP)
            v_hat = v_new / (1.0 - ADAM_B2 ** ADAM_STEP)
            g_r[...] = g
            d_r[...] = -ADAM_LR * (m_hat / (jnp.sqrt(v_hat) + ADAM_EPS) + ADAM_WD * w_r[...])
            nm_r[...] = m_new
            nv_r[...] = v_new

        for k in range(L):
            pl.when(pl.program_id(0) == k)(functools.partial(update, p_refs[k]))

    return _pcall(body, name=name, grid=(L, R // t), in_specs=[part(k) for k in range(L)] + [row, row, row],
                  out_specs=[row] * 4, out_shape=[jax.ShapeDtypeStruct((L, R, C), F32)] * 4,
                  compiler_params=_params(("arbitrary", "arbitrary")))(*parts, w, m, v)


def _my_place():
    x, y, c = lax.axis_index("x"), lax.axis_index("y"), lax.axis_index("c")
    return x, y, c


def _index(x, y, c):
    return 4 * x + 2 * y + c


def _all_gather(vs):
    n = len(vs)

    def body(*refs):
        v_refs, out_refs = refs[:n], refs[n:2 * n]
        send_sems, recv_sems, local_sems = refs[2 * n:]
        x, y, c = _my_place()
        me, sibling = (x, y, c), (x, y, 1 - c)
        chips = [(1 - x, y), (x, 1 - y), (1 - x, 1 - y)]

        def copy(a, k, block, to, src=None):
            rows = out_refs[a].at[_index(*block)]
            return pltpu.make_async_remote_copy(src_ref=rows if src is None else src, dst_ref=rows, send_sem=send_sems.at[a, k],
                                                recv_sem=recv_sems.at[a, k], device_id=to, device_id_type=MESH)

        mine = [pltpu.make_async_copy(v_refs[a], out_refs[a].at[_index(*me)], local_sems.at[a]) for a in range(n)]
        first, passed = [], []
        for a in range(n):
            mine[a].start()
            first += [copy(a, 0, me, sibling, src=v_refs[a])]
            first += [copy(a, 1 + j, me, (*chip, c), src=v_refs[a]) for j, chip in enumerate(chips)]
        for cp in first:
            cp.start()
        for j, chip in enumerate(chips):
            for a in range(n):
                copy(a, 1 + j, (*chip, c), me).wait_recv()
                passed.append(copy(a, 4 + j, (*chip, c), sibling))
                passed[-1].start()
        for a in range(n):
            copy(a, 0, sibling, me).wait_recv()
            for j, chip in enumerate(chips):
                copy(a, 4 + j, (*chip, 1 - c), me).wait_recv()
        for cp in first + passed:
            cp.wait_send()
        for a in range(n):
            mine[a].wait()

    any_ = pl.BlockSpec(memory_space=pl.ANY)
    return _pcall(body, name="gather_weights", in_specs=[any_] * n, out_specs=[any_] * n,
                  out_shape=[jax.ShapeDtypeStruct((N_DEV,) + v.shape, v.dtype) for v in vs],
                  scratch_shapes=[pltpu.SemaphoreType.DMA((n, 7)), pltpu.SemaphoreType.DMA((n, 7)), pltpu.SemaphoreType.DMA((n,))])(*vs)


def _all_to_all(vs):
    n = len(vs)

    def body(*refs):
        v_refs, out_refs = refs[:n], refs[n:2 * n]
        send_sems, recv_sems, local_sems = refs[2 * n:]
        x, y, c = _my_place()
        me = _index(x, y, c)
        mine = [pltpu.make_async_copy(v_refs[a].at[me], out_refs[a].at[me], local_sems.at[a]) for a in range(n)]
        copies = []
        for a in range(n):
            mine[a].start()
        for k in range(1, N_DEV):
            px = 1 - x if k & 4 else x
            py = 1 - y if k & 2 else y
            pc = 1 - c if k & 1 else c
            for a in range(n):
                cp = pltpu.make_async_remote_copy(src_ref=v_refs[a].at[_index(px, py, pc)], dst_ref=out_refs[a].at[me],
                                                  send_sem=send_sems.at[a, k - 1], recv_sem=recv_sems.at[a, k - 1],
                                                  device_id=(px, py, pc), device_id_type=MESH)
                cp.start()
                copies.append(cp)
        for cp in copies:
            cp.wait()
        for a in range(n):
            mine[a].wait()

    any_ = pl.BlockSpec(memory_space=pl.ANY)
    return _pcall(body, name="exchange_grads", in_specs=[any_] * n, out_specs=[any_] * n,
                  out_shape=[jax.ShapeDtypeStruct(v.shape, v.dtype) for v in vs],
                  scratch_shapes=[pltpu.SemaphoreType.DMA((n, 7)), pltpu.SemaphoreType.DMA((n, 7)), pltpu.SemaphoreType.DMA((n,))])(*vs)


_HBM = pl.BlockSpec(memory_space=pltpu.HBM)
_SEM = pl.BlockSpec(memory_space=pltpu.SEMAPHORE)
_EFFECT = pltpu.SideEffectType.DATAFLOW_SIDE_EFFECTING


def _direct_copies(gather, v_refs, land_refs, send_sems, recv_sems, local_sems):
    x, y, c = _my_place()
    me = _index(x, y, c)
    local, remote = [], []
    for a, (v_ref, land_ref) in enumerate(zip(v_refs, land_refs)):
        local.append(pltpu.make_async_copy(v_ref if gather else v_ref.at[me], land_ref.at[me], local_sems.at[a]))
    for k in range(1, N_DEV):
        px = 1 - x if k & 4 else x
        py = 1 - y if k & 2 else y
        pc = 1 - c if k & 1 else c
        for a, (v_ref, land_ref) in enumerate(zip(v_refs, land_refs)):
            sem = a * (N_DEV - 1) + k - 1
            remote.append(pltpu.make_async_remote_copy(
                src_ref=v_ref if gather else v_ref.at[_index(px, py, pc)], dst_ref=land_ref.at[me], send_sem=send_sems.at[sem],
                recv_sem=recv_sems.at[sem], device_id=(px, py, pc), device_id_type=MESH))
    return local, remote


def _exchange_start(name, vs, gather, thru):
    n = len(vs)
    lands = [lax.empty((N_DEV,) + v.shape if gather else v.shape, v.dtype) for v in vs]

    def body(*refs):
        v_refs, land_refs = refs[:n], refs[n:2 * n]
        send_sems, recv_sems, local_sems = refs[2 * n + 1:2 * n + 4]
        local, remote = _direct_copies(gather, v_refs, land_refs, send_sems, recv_sems, local_sems)
        for cp in local + remote:
            cp.start()

    hbm = lambda a: pltpu.HBM(a.shape, a.dtype)
    res = _pcall(body, name=name,
                 out_shape=(pltpu.SemaphoreType.DMA((n * (N_DEV - 1),)), pltpu.SemaphoreType.DMA((n * (N_DEV - 1),)), pltpu.SemaphoreType.DMA((n,)),
                            *[hbm(a) for a in (*vs, *lands, thru)]),
                 in_specs=[_HBM] * (2 * n + 1), out_specs=(_SEM, _SEM, _SEM, *[_HBM] * (2 * n + 1)),
                 input_output_aliases={i: 3 + i for i in range(2 * n + 1)},
                 compiler_params=pltpu.CompilerParams(has_side_effects=_EFFECT))(
        *[pltpu.with_memory_space_constraint(a, pltpu.HBM) for a in (*vs, *lands, thru)])
    return (gather, res[:3], res[3:3 + n], res[3 + n:3 + 2 * n]), res[3 + 2 * n]


def _exchange_wait(name, started, after):
    gather, sems, vs, lands = started
    n = len(vs)

    def body(*refs):
        v_refs, land_refs = refs[:n], refs[n:2 * n]
        send_sems, recv_sems, local_sems = refs[2 * n:2 * n + 3]
        local, remote = _direct_copies(gather, v_refs, land_refs, send_sems, recv_sems, local_sems)
        for cp in local:
            cp.wait()
        for cp in remote:
            cp.wait_send()
            cp.wait_recv()

    hbm = lambda a: pltpu.HBM(a.shape, a.dtype)
    res = _pcall(body, name=name, out_shape=tuple(hbm(a) for a in (*vs, *lands)),
                 in_specs=[_HBM] * (2 * n) + [_SEM] * 3 + [pl.BlockSpec(memory_space=pl.ANY)], out_specs=tuple([_HBM] * (2 * n)),
                 input_output_aliases={i: i for i in range(2 * n)},
                 compiler_params=pltpu.CompilerParams(has_side_effects=_EFFECT))(*vs, *lands, *sems, after)
    return list(res[n:])


W_IN_SHARD = IN_WIDTH // N_DEV
SEG_ORDER = ("q_lat", "c_kv", "k_pe", "z_a", "dn_qkv", "dn_ab", "z_b", "dil_qkv", "z_c", "gate")
SEG_WIDTH = (384, 256, LANES, 512, 1536, LANES, 512, 4608, 512, 3072)


def _w_in_plan():
    plan = []

    def add(seg, c0, c1, dst):
        while c0 < c1:
            d = c0 // W_IN_SHARD
            e = min(c1, (d + 1) * W_IN_SHARD)
            plan.append((seg, dst, d, c0 - d * W_IN_SHARD, e - c0))
            dst += e - c0
            c0 = e

    half = MLA_ROPE // 2
    for i, name in enumerate(SEG_ORDER):
        if name == "k_pe":
            o = _SEG["k_pe"][0]
            add(i, o, o + half, 0)
            add(i, o + half, o + 2 * half, LANES // 2)
        elif name == "dn_ab":
            o = _SEG["dn_a"][0]
            add(i, o, o + 2 * DN_HEADS, 0)
        else:
            o, w = _SEG[name]
            add(i, o, o + w, 0)
    return plan


def _make_w_in_segments(name):
    plan = _w_in_plan()
    nseg = len(SEG_ORDER)
    t = 256

    def fwd_call(g):
        L = g.shape[1]

        def body(g_ref, *o_refs):
            for i in (SEG_ORDER.index("k_pe"), SEG_ORDER.index("dn_ab")):
                o_refs[i][...] = jnp.zeros_like(o_refs[i])
            for seg, dst, d, src, n in plan:
                o_refs[seg][:, dst:dst + n] = g_ref[d, :, src:src + n]

        return _pcall(body, name=f"{name}_fwd", grid=(L, D_MODEL // t),
                      in_specs=[pl.BlockSpec((N_DEV, None, t, W_IN_SHARD), lambda l, i: (0, l, i, 0))],
                      out_specs=[pl.BlockSpec((None, t, w), lambda l, i: (l, i, 0)) for w in SEG_WIDTH],
                      out_shape=[jax.ShapeDtypeStruct((L, D_MODEL, w), g.dtype) for w in SEG_WIDTH],
                      compiler_params=_params(("parallel", "parallel")))(g)

    def bwd_call(ds):
        L = ds[0].shape[0]

        def body(*refs):
            d_refs, g_ref = refs[:nseg], refs[nseg]
            for seg, dst, d, src, n in plan:
                g_ref[d, :, src:src + n] = d_refs[seg][:, dst:dst + n]

        return _pcall(body, name=f"{name}_bwd", grid=(L, D_MODEL // t),
                      in_specs=[pl.BlockSpec((None, t, w), lambda l, i: (l, i, 0)) for w in SEG_WIDTH],
                      out_specs=pl.BlockSpec((N_DEV, None, t, W_IN_SHARD), lambda l, i: (0, l, i, 0)),
                      out_shape=jax.ShapeDtypeStruct((N_DEV, L, D_MODEL, W_IN_SHARD), ds[0].dtype),
                      compiler_params=_params(("parallel", "parallel")))(*ds)

    @jax.custom_vjp
    def op(g):
        return tuple(fwd_call(g))

    op.defvjp(lambda g: (op(g), None), lambda _, ds: (bwd_call(tuple(ds)),))
    return op


def _pe_pad(a):
    h = MLA_ROPE // 2
    z = jnp.zeros(a.shape[:-1] + (h,), a.dtype)
    return jnp.concatenate([a[..., :h], z, a[..., h:], z], axis=-1)


def _layer(tag, x, tables, W):
    cos_p, sin_p, cos_h, sin_h = tables
    w_segs = W["w_in_segments"]
    row = lambda a: a[None, :]

    (h,) = _make_rowwise(f"{tag}_norm", _f_norm, 512)((x,), (), (row(W["norm_g"]),), ())
    q_lat, c_kv, kpe, z_a, dn_qkv, ab, z_b, dil_qkv, z_c, gl = _make_multi_linear(f"{tag}_inproj", 10)(h, w_segs)

    qn_lat, ckvn, kp = _make_rowwise(f"{tag}_mla_a", _f_mla_a, 512)(
        (q_lat, c_kv, kpe), (cos_p, sin_p),
        (row(W["mla_q_a_norm_g"]), row(W["mla_kv_a_norm_g"]), row(_pe_pad(W["mla_k_norm_g"][LANES:]))), ())
    wq = W["mla_w_q_b"].reshape(MLA_Q_RANK, MLA_HEADS, MLA_QK)
    wq = jnp.concatenate([wq[:, :, :LANES].reshape(MLA_Q_RANK, -1), _pe_pad(wq[:, :, LANES:]).reshape(MLA_Q_RANK, -1)], axis=1)
    wkv = W["mla_w_kv_b"].reshape(MLA_KV_RANK, MLA_HEADS, 2 * LANES)
    (q8,) = _make_multi_linear(f"{tag}_qb", 1)(qn_lat, (wq,))
    kn_raw, v_mla = _make_multi_linear(f"{tag}_kvb", 2)(
        ckvn, (wkv[:, :, :LANES].reshape(MLA_KV_RANK, -1), wkv[:, :, LANES:].reshape(MLA_KV_RANK, -1)))
    qn, qp, kn = _make_rowwise(f"{tag}_mla_b", _f_mla_b, 512)(
        (q8, kn_raw), (cos_p, sin_p),
        (row(W["mla_q_norm_g"][:LANES]), row(_pe_pad(W["mla_q_norm_g"][LANES:])), row(W["mla_k_norm_g"][:LANES])), ())
    y_a = _make_mla_attn(f"{tag}_mla")(qn, qp, kn, kp, v_mla)

    mixed = _make_conv(f"{tag}_conv")(dn_qkv, W["dn_conv_w"])
    lane_head = jnp.arange(DN_HEADS * LANES) // LANES
    e_a = (jnp.arange(LANES)[:, None] == lane_head[None, :]).astype(F32)
    e_b = (jnp.arange(LANES)[:, None] == lane_head[None, :] + DN_HEADS).astype(F32)
    q_dn, k_dn, v_dn, g_dn, b_dn = _make_rowwise(f"{tag}_dn_pre", _f_dn_pre, 512)(
        (mixed, ab), (), (row(jnp.repeat(W["dn_a_log"], LANES)), row(jnp.repeat(W["dn_dt_bias"], LANES))), (e_a, e_b))
    o_dn = _make_delta_rule(f"{tag}_dn")(q_dn, k_dn, v_dn, g_dn, b_dn)

    qkv_dil = _make_rowwise(f"{tag}_dil_pre", _f_dil_pre, 256)(
        (dil_qkv,), (cos_h, sin_h), (row(W["dil_q_norm_g"]), row(W["dil_k_norm_g"])), ())
    n_groups = len(DIL_DILATIONS)
    o_lse = [_make_dil_attn(f"{tag}_dil{g}", d, DIL_TILE_ROWS[g])(qkv_dil[g], qkv_dil[n_groups + g], qkv_dil[2 * n_groups + g])
             for g, d in enumerate(DIL_DILATIONS)]

    ya, yb, yc = _make_rowwise(f"{tag}_merge_a", _f_merge_a, 256)(
        (y_a, z_a, o_dn, z_b, *[o for o, _ in o_lse], *[l for _, l in o_lse], z_c), (), (row(W["dn_out_norm_g"]),), ())
    (b0,) = _make_multi_linear(f"{tag}_br0", 1)(ya, (W["w_branch"][0],))
    (b1,) = _make_multi_linear(f"{tag}_br1", 1)(yb, (W["w_branch"][1],))
    (b2,) = _make_multi_linear(f"{tag}_br2", 1)(yc, (W["w_branch"][2],))
    (mix,) = _make_rowwise(f"{tag}_merge_b", _f_merge_b, 256)((b0, b1, b2, gl), (), (), ())
    return _make_resid_linear(f"{tag}_out")(x, mix, W["w_out"])


SHARDED = (("w_in", (D_MODEL, W_IN_SHARD)), ("mla_w_q_b", (MLA_Q_RANK, MLA_HEADS * MLA_QK // N_DEV)),
           ("mla_w_kv_b", (MLA_KV_RANK, MLA_HEADS * 2 * LANES // N_DEV)), ("w_branch", (3 * BRANCH_W, D_MODEL // N_DEV)),
           ("w_out", (D_MODEL // N_DEV, D_MODEL)), ("dn_conv_w", (DN_CONV, 3 * DN_HEADS * LANES // N_DEV)))
SMALL = (("norm_g", D_MODEL), ("mla_q_a_norm_g", MLA_Q_RANK), ("mla_kv_a_norm_g", MLA_KV_RANK), ("mla_q_norm_g", MLA_QK),
         ("mla_k_norm_g", MLA_QK), ("dn_a_log", DN_HEADS), ("dn_dt_bias", DN_HEADS), ("dn_out_norm_g", LANES),
         ("dil_q_norm_g", LANES), ("dil_k_norm_g", LANES))
WEIGHTS = ("norm_g", "w_in", "mla_q_a_norm_g", "mla_w_q_b", "mla_kv_a_norm_g", "mla_w_kv_b", "mla_q_norm_g", "mla_k_norm_g",
           "dn_conv_w", "dn_a_log", "dn_dt_bias", "dn_out_norm_g", "dil_q_norm_g", "dil_k_norm_g", "w_branch", "w_out")


def _round_up(n, m):
    return -(-n // m) * m


def _pack_vectors(pieces):
    return jnp.concatenate([jnp.pad(p, (0, _round_up(p.shape[0], LANES) - p.shape[0])) for p in pieces]).reshape(-1, LANES)


def _unpack_vectors(flat, sizes):
    out, off = [], 0
    flat = flat.reshape(-1)
    for n in sizes:
        out.append(flat[off:off + n])
        off += _round_up(n, LANES)
    return out


def _whole_weights(tag, g, small):
    W = dict(small)
    W["w_in_segments"] = tuple(s[0] for s in _make_w_in_segments(f"{tag}_w_in_segments")(g["w_in"][:, None]))
    W["mla_w_q_b"] = g["mla_w_q_b"].transpose(1, 0, 2).reshape(MLA_Q_RANK, -1)
    W["mla_w_kv_b"] = g["mla_w_kv_b"].transpose(1, 0, 2).reshape(MLA_KV_RANK, -1)
    W["w_branch"] = g["w_branch"].reshape(N_DEV, 3, BRANCH_W, -1).transpose(1, 2, 0, 3).reshape(3, BRANCH_W, D_MODEL)
    W["w_out"] = g["w_out"].reshape(D_MODEL, D_MODEL)
    W["dn_conv_w"] = g["dn_conv_w"].transpose(1, 0, 2).reshape(DN_CONV, -1)
    return W


def kernel(x, positions, norm_g, w_in, mla_q_a_norm_g, mla_w_q_b, mla_kv_a_norm_g, mla_w_kv_b, mla_q_norm_g, mla_k_norm_g, dn_conv_w, dn_a_log, dn_dt_bias, dn_out_norm_g, dil_q_norm_g, dil_k_norm_g, w_branch, w_out, loss_target, m_norm_g, m_w_in, m_mla_q_a_norm_g, m_mla_w_q_b, m_mla_kv_a_norm_g, m_mla_w_kv_b, m_mla_q_norm_g, m_mla_k_norm_g, m_dn_conv_w, m_dn_a_log, m_dn_dt_bias, m_dn_out_norm_g, m_dil_q_norm_g, m_dil_k_norm_g, m_w_branch, m_w_out, v_norm_g, v_w_in, v_mla_q_a_norm_g, v_mla_w_q_b, v_mla_kv_a_norm_g, v_mla_w_kv_b, v_mla_q_norm_g, v_mla_k_norm_g, v_dn_conv_w, v_dn_a_log, v_dn_dt_bias, v_dn_out_norm_g, v_dil_q_norm_g, v_dil_k_norm_g, v_w_branch, v_w_out):
    w = dict(norm_g=norm_g, w_in=w_in, mla_q_a_norm_g=mla_q_a_norm_g, mla_w_q_b=mla_w_q_b, mla_kv_a_norm_g=mla_kv_a_norm_g,
             mla_w_kv_b=mla_w_kv_b, mla_q_norm_g=mla_q_norm_g, mla_k_norm_g=mla_k_norm_g, dn_conv_w=dn_conv_w, dn_a_log=dn_a_log,
             dn_dt_bias=dn_dt_bias, dn_out_norm_g=dn_out_norm_g, dil_q_norm_g=dil_q_norm_g, dil_k_norm_g=dil_k_norm_g,
             w_branch=w_branch, w_out=w_out)
    m = dict(norm_g=m_norm_g, w_in=m_w_in, mla_q_a_norm_g=m_mla_q_a_norm_g, mla_w_q_b=m_mla_w_q_b, mla_kv_a_norm_g=m_mla_kv_a_norm_g,
             mla_w_kv_b=m_mla_w_kv_b, mla_q_norm_g=m_mla_q_norm_g, mla_k_norm_g=m_mla_k_norm_g, dn_conv_w=m_dn_conv_w,
             dn_a_log=m_dn_a_log, dn_dt_bias=m_dn_dt_bias, dn_out_norm_g=m_dn_out_norm_g, dil_q_norm_g=m_dil_q_norm_g,
             dil_k_norm_g=m_dil_k_norm_g, w_branch=m_w_branch, w_out=m_w_out)
    v = dict(norm_g=v_norm_g, w_in=v_w_in, mla_q_a_norm_g=v_mla_q_a_norm_g, mla_w_q_b=v_mla_w_q_b, mla_kv_a_norm_g=v_mla_kv_a_norm_g,
             mla_w_kv_b=v_mla_w_kv_b, mla_q_norm_g=v_mla_q_norm_g, mla_k_norm_g=v_mla_k_norm_g, dn_conv_w=v_dn_conv_w,
             dn_a_log=v_dn_a_log, dn_dt_bias=v_dn_dt_bias, dn_out_norm_g=v_dn_out_norm_g, dil_q_norm_g=v_dil_q_norm_g,
             dil_k_norm_g=v_dil_k_norm_g, w_branch=v_w_branch, w_out=v_w_out)
    x2, target = x[0], loss_target[0]
    pos = positions[0][:, None]

    names = [n for n, _ in SHARDED]
    view = lambda t, n, s: t[n].reshape((DEPTH,) + s)
    shards = [[(view(w, n, s) if n == "dn_conv_w" else view(w, n, s).astype(BF16))[l] for n, s in SHARDED] for l in range(DEPTH)]
    small = [{n: w[n][l] for n, _ in SMALL} for l in range(DEPTH)]
    gathered0 = dict(zip(names, _all_gather(shards[0])))
    gathering1, pos = _exchange_start("gather_layer1_start", shards[1], True, pos)
    tables = _rope_tables(pos, _rope_consts())

    def layer_fn(l):
        return lambda g, s, h: _layer(f"l{l}", h, tables, _whole_weights(f"l{l}", g, s))

    y0, vjp0 = jax.vjp(layer_fn(0), gathered0, small[0], x2)
    gathered1 = dict(zip(names, _exchange_wait("gather_layer1_wait", gathering1, y0)))
    y1, vjp1 = jax.vjp(layer_fn(1), gathered1, small[1], y0)
    loss_splat, dy = _loss_call(y1, target)
    loss = lax.psum(loss_splat[0, 0], ("x", "y", "c"))

    g_gathered1, g_small1, d_y0 = vjp1(dy)
    exchanging1, d_y0 = _exchange_start("exchange_layer1_start", [g_gathered1[n] for n in names], False, d_y0)
    g_gathered0, g_small0, g_x = vjp0(d_y0)
    g_small = (g_small0, g_small1)
    sizes = [k for _ in range(DEPTH) for _, k in SMALL]
    g_vec = _pack_vectors([g_small[l][n] for l in range(DEPTH) for n, _ in SMALL])
    parts0 = _all_to_all([g_gathered0[n] for n in names] + [jnp.broadcast_to(g_vec[None], (N_DEV,) + g_vec.shape)])
    parts1 = _exchange_wait("exchange_layer1_wait", exchanging1, parts0[-1])

    vec = lambda t: _pack_vectors([t[n][l] for l in range(DEPTH) for n, _ in SMALL])[None]
    outs = {}
    for i, (n, s) in enumerate(SHARDED):
        res = _adamw_call(f"adamw_{n}", [parts0[i], parts1[i]], view(w, n, s), view(m, n, s), view(v, n, s))
        outs[n] = [o.reshape(w[n].shape) for o in res]
    vec_outs = [_unpack_vectors(o, sizes) for o in _adamw_call("adamw_vectors", [parts0[-1]], vec(w), vec(m), vec(v))]
    for i, (n, _) in enumerate(SMALL):
        outs[n] = [jnp.stack([o[l * len(SMALL) + i] for l in range(DEPTH)]) for o in vec_outs]
    return (loss, g_x[None], *[outs[n][k] for k in range(4) for n in WEIGHTS])
```

```python
import functools
import math

import jax
import jax.numpy as jnp
from jax import lax
from jax.experimental import pallas as pl
from jax.experimental.pallas import tpu as pltpu

F32 = jnp.float32
BF16 = jnp.bfloat16
HI = lax.Precision.HIGHEST
MESH = pl.DeviceIdType.MESH

N_DEV = 8
D_MODEL = 1024
DEPTH = 2
RMS_EPS = 1e-6
ROPE_THETA = 10000.0
LANES = 128
MLA_HEADS = 4
MLA_ROPE = 64
MLA_QK = 192
MLA_Q_RANK = 384
MLA_KV_RANK = 256
DN_HEADS = 4
DN_CHUNK = 64
DN_CONV = 4
DIL_HEADS = 12
DIL_GROUP_HEADS = 4
DIL_DILATIONS = (1, 4, 16)
DIL_BLOCK = 128
BRANCH_W = 512
IN_WIDTH = 11464
NEG = -1e30
VMEM_LIMIT = 56 * 1024 * 1024

ADAM_LR, ADAM_B1, ADAM_B2, ADAM_EPS, ADAM_WD, ADAM_STEP = 0.001, 0.9, 0.999, 1e-08, 0.01, 10

_SEG = {}
_off = 0
for _n, _w in (("q_lat", 384), ("c_kv", 256), ("k_pe", 64), ("z_a", 512), ("dn_qkv", 1536), ("dn_a", 4), ("dn_b", 4),
               ("z_b", 512), ("dil_qkv", 4608), ("z_c", 512), ("gate", 3072)):
    _SEG[_n] = (_off, _w)
    _off += _w
assert _off == IN_WIDTH


def _pcall(body, **kw):
    return pl.pallas_call(body, **kw)


def _params(sem=None):
    return pltpu.CompilerParams(dimension_semantics=sem, vmem_limit_bytes=VMEM_LIMIT)


def _tile(n, target, mult):
    t = (min(n, target) // mult) * mult
    while t >= mult:
        if n % t == 0:
            return t
        t -= mult
    return n


def _mm(name, a, b, mode, out_dtype=F32, acc=None, tm=1024, tn=512, tk=1024):
    if mode == "nn":
        (M, K), (_, N) = a.shape, b.shape
    elif mode == "nt":
        (M, K), (N, _) = a.shape, b.shape
    else:
        (K, M), (_, N) = a.shape, b.shape
    tm, tn, tk = _tile(M, tm, LANES), _tile(N, tn, LANES), _tile(K, tk, LANES)
    nk = K // tk
    dims = {"nn": (((1,), (0,)), ((), ())), "nt": (((1,), (1,)), ((), ())), "tn": (((0,), (0,)), ((), ()))}[mode]
    a_spec = pl.BlockSpec((tk, tm), lambda i, j, k: (k, i)) if mode == "tn" else pl.BlockSpec((tm, tk), lambda i, j, k: (i, k))
    b_spec = pl.BlockSpec((tn, tk), lambda i, j, k: (j, k)) if mode == "nt" else pl.BlockSpec((tk, tn), lambda i, j, k: (k, j))
    o_spec = pl.BlockSpec((tm, tn), lambda i, j, k: (i, j))
    has_acc = acc is not None

    def body(*refs):
        a_ref, b_ref = refs[:2]
        c_ref = refs[2] if has_acc else None
        o_ref = refs[3] if has_acc else refs[2]
        prod = lax.dot_general(a_ref[...].astype(BF16), b_ref[...].astype(BF16), dims, preferred_element_type=F32)
        if nk == 1:
            o_ref[...] = (prod + c_ref[...].astype(F32) if has_acc else prod).astype(out_dtype)
            return
        acc_ref = refs[-1]
        k = pl.program_id(2)

        @pl.when(k == 0)
        def _():
            acc_ref[...] = prod + c_ref[...].astype(F32) if has_acc else prod

        @pl.when(k > 0)
        def _():
            acc_ref[...] += prod

        @pl.when(k == nk - 1)
        def _():
            o_ref[...] = acc_ref[...].astype(out_dtype)

    ins = [a, b] + ([acc] if has_acc else [])
    in_specs = [a_spec, b_spec] + ([o_spec] if has_acc else [])
    return _pcall(body, name=name, grid=(M // tm, N // tn, nk), in_specs=in_specs, out_specs=o_spec,
                  out_shape=jax.ShapeDtypeStruct((M, N), out_dtype), scratch_shapes=[pltpu.VMEM((tm, tn), F32)] if nk > 1 else [],
                  compiler_params=_params(("parallel", "parallel", "arbitrary")))(*ins)


def _make_multi_linear(name, n):
    @jax.custom_vjp
    def op(h, ws):
        return tuple(_mm(f"{name}_fwd{i}", h, w, "nn") for i, w in enumerate(ws))

    def fwd(h, ws):
        return op(h, ws), (h, ws)

    def bwd(res, douts):
        h, ws = res
        dh = None
        for i, (w, d) in enumerate(zip(ws, douts)):
            dh = _mm(f"{name}_dh{i}", d, w, "nt", acc=dh, out_dtype=h.dtype if i == len(ws) - 1 else F32)
        dws = tuple(_mm(f"{name}_dw{i}", h, d, "tn", out_dtype=w.dtype) for i, (w, d) in enumerate(zip(ws, douts)))
        return dh, dws

    op.defvjp(fwd, bwd)
    return op


def _make_resid_linear(name):
    @jax.custom_vjp
    def op(x, a, w):
        return _mm(f"{name}_fwd", a, w, "nn", acc=x)

    def fwd(x, a, w):
        return op(x, a, w), (a, w)

    def bwd(res, dy):
        a, w = res
        return dy, _mm(f"{name}_da", dy, w, "nt", out_dtype=a.dtype), _mm(f"{name}_dw", a, dy, "tn", out_dtype=w.dtype)

    op.defvjp(fwd, bwd)
    return op


def _make_rowwise(name, f, tile):
    def specs(rows, aux, params, consts, t):
        row = [pl.BlockSpec((t, a.shape[1]), lambda i: (i, 0)) for a in (*rows, *aux)]
        full = [pl.BlockSpec(p.shape, lambda i: (0, 0)) for p in (*params, *consts)]
        return row, full

    def fwd_call(rows, aux, params, consts):
        S = rows[0].shape[0]
        t = min(tile, S)
        n_in = len(rows) + len(aux) + len(params) + len(consts)
        shp = lambda a: jax.ShapeDtypeStruct((t, a.shape[1]), a.dtype)
        outs = jax.eval_shape(f, *[shp(a) for a in (*rows, *aux)], *params, *consts)
        row_specs, full_specs = specs(rows, aux, params, consts, t)

        def body(*refs):
            res = f(*[r[...] for r in refs[:n_in]])
            for o_ref, o in zip(refs[n_in:], res):
                o_ref[...] = o

        return _pcall(body, name=f"{name}_fwd", grid=(S // t,), in_specs=row_specs + full_specs,
                      out_specs=[pl.BlockSpec((t, o.shape[1]), lambda i: (i, 0)) for o in outs],
                      out_shape=[jax.ShapeDtypeStruct((S, o.shape[1]), o.dtype) for o in outs],
                      compiler_params=_params(("parallel",)))(*rows, *aux, *params, *consts)

    def bwd_call(rows, aux, params, consts, douts):
        S = rows[0].shape[0]
        t = min(tile, S)
        nr, na, npar, nc, nd = len(rows), len(aux), len(params), len(consts), len(douts)
        row_specs, full_specs = specs(rows, aux, params, consts, t)

        def body(*refs):
            vals = [r[...] for r in refs[:nr + na + npar + nc]]
            rv, av = vals[:nr], vals[nr:nr + na]
            pv, cv = vals[nr + na:nr + na + npar], vals[nr + na + npar:]
            dv = tuple(r[...] for r in refs[nr + na + npar + nc:nr + na + npar + nc + nd])
            out_refs = refs[nr + na + npar + nc + nd:]
            _, vjp = jax.vjp(lambda *rp: f(*rp[:nr], *av, *rp[nr:], *cv), *rv, *pv)
            grads = vjp(dv)
            for o_ref, g in zip(out_refs[:nr], grads[:nr]):
                o_ref[...] = g
            first = pl.program_id(0) == 0
            for o_ref, g in zip(out_refs[nr:], grads[nr:]):
                @pl.when(first)
                def _(o_ref=o_ref):
                    o_ref[...] = jnp.zeros_like(o_ref)
                o_ref[...] += g

        res = _pcall(body, name=f"{name}_bwd", grid=(S // t,),
                     in_specs=row_specs + full_specs + [pl.BlockSpec((t, d.shape[1]), lambda i: (i, 0)) for d in douts],
                     out_specs=[pl.BlockSpec((t, a.shape[1]), lambda i: (i, 0)) for a in rows]
                     + [pl.BlockSpec(p.shape, lambda i: (0, 0)) for p in params],
                     out_shape=[jax.ShapeDtypeStruct(a.shape, a.dtype) for a in (*rows, *params)],
                     compiler_params=_params(("arbitrary",)))(*rows, *aux, *params, *consts, *douts)
        return tuple(res[:nr]), tuple(res[nr:])

    @jax.custom_vjp
    def op(rows, aux, params, consts):
        return tuple(fwd_call(rows, aux, params, consts))

    def fwd(rows, aux, params, consts):
        return op(rows, aux, params, consts), (rows, aux, params, consts)

    def bwd(res, douts):
        rows, aux, params, consts = res
        drows, dparams = bwd_call(rows, aux, params, consts, tuple(douts))
        zeros = lambda xs: tuple(jnp.zeros_like(a) for a in xs)
        return drows, zeros(aux), dparams, zeros(consts)

    op.defvjp(fwd, bwd)
    return op


@jax.custom_vjp
def _swap_halves(x):
    return pltpu.roll(x, LANES // 2, 1)


_swap_halves.defvjp(lambda x: (_swap_halves(x), None), lambda _, g: (_swap_halves(g),))


def _rope(x, cos_t, sin_t):
    return x * cos_t + _swap_halves(x) * sin_t


def _rms(x, g, n=None):
    n = x.shape[-1] if n is None else n
    return x * lax.rsqrt(jnp.sum(x * x, axis=-1, keepdims=True) * (1.0 / n) + RMS_EPS) * g


def _heads(x):
    return [x[:, i * LANES:(i + 1) * LANES] for i in range(x.shape[1] // LANES)]


def _cat(xs):
    return jnp.concatenate(xs, axis=1)


def _silu(x):
    return x * jax.nn.sigmoid(x)


def _f_norm(x, g):
    return (_rms(x, g).astype(BF16),)


def _f_mla_a(q_lat, c_kv, kpe, cos_p, sin_p, qa_g, kva_g, kpe_g):
    kp = _rope(_rms(kpe, kpe_g, MLA_ROPE), cos_p, sin_p)
    return _rms(q_lat, qa_g).astype(BF16), _rms(c_kv, kva_g).astype(BF16), _cat([kp] * MLA_HEADS)


def _f_mla_b(q8, kn_raw, cos_p, sin_p, qn_g, qp_g, kn_g):
    hs = _heads(q8)
    qn = _cat([_rms(h, qn_g) for h in hs[:MLA_HEADS]])
    qp = _cat([_rope(_rms(h, qp_g, MLA_ROPE), cos_p, sin_p) for h in hs[MLA_HEADS:]])
    kn = _cat([_rms(h, kn_g) for h in _heads(kn_raw)])
    return qn, qp, kn


def _softplus(x):
    return jnp.maximum(x, 0.0) + jnp.log(1.0 + jnp.exp(-jnp.abs(x)))


def _l2n(x):
    return x * lax.rsqrt(jnp.sum(x * x, axis=-1, keepdims=True) + 1e-6)


def _f_dn_pre(mixed, ab, alog_f, dtb_f, e_a, e_b):
    hs = _heads(mixed)
    q = _cat([_l2n(h) * (LANES ** -0.5) for h in hs[:DN_HEADS]])
    k = _cat([_l2n(h) for h in hs[DN_HEADS:2 * DN_HEADS]])
    v = _cat(hs[2 * DN_HEADS:])
    a_f = jnp.dot(ab, e_a, precision=HI, preferred_element_type=F32)
    b_f = jnp.dot(ab, e_b, precision=HI, preferred_element_type=F32)
    g = -jnp.exp(alog_f) * _softplus(a_f + dtb_f)
    return q, k, v, g, jax.nn.sigmoid(b_f)


def _f_dil_pre(qkv, cos_h, sin_h, q_g, k_g):
    hs = _heads(qkv)
    q = [_rope(_rms(h, q_g), cos_h, sin_h) for h in hs[:DIL_HEADS]]
    k = [_rope(_rms(h, k_g), cos_h, sin_h) for h in hs[DIL_HEADS:2 * DIL_HEADS]]
    v = hs[2 * DIL_HEADS:]
    group = lambda xs, g: _cat(xs[g * DIL_GROUP_HEADS:(g + 1) * DIL_GROUP_HEADS])
    return tuple(group(xs, g) for xs in (q, k, v) for g in range(len(DIL_DILATIONS)))


def _f_merge_a(y_a, z_a, o_dn, z_b, o0, o1, o2, l0, l1, l2, z_c, out_g):
    y_b = _cat([_rms(h, out_g) for h in _heads(o_dn)])
    os_, ls = [_heads(o) for o in (o0, o1, o2)], [_heads(l) for l in (l0, l1, l2)]
    y_c = []
    for j in range(DIL_GROUP_HEADS):
        l3 = [ls[g][j] for g in range(3)]
        m = jnp.maximum(jnp.maximum(l3[0], l3[1]), l3[2])
        e3 = [jnp.exp(l - m) for l in l3]
        den = e3[0] + e3[1] + e3[2]
        y_c.append(sum(e3[g] * os_[g][j] for g in range(3)) / den)
    return tuple(t.astype(BF16) for t in (y_a * _silu(z_a), y_b * _silu(z_b), _cat(y_c) * _silu(z_c)))


def _f_merge_b(b0, b1, b2, gl):
    gs = [jax.nn.sigmoid(gl[:, i * D_MODEL:(i + 1) * D_MODEL]) for i in range(3)]
    return ((gs[0] * b0 + gs[1] * b1 + gs[2] * b2).astype(BF16),)


def _rope_tables(pos, inv_sign):
    S = pos.shape[0]
    t = min(S, 1024)

    def body(p_ref, c_ref, cp, sp, ch, sh):
        p = p_ref[...].astype(F32)
        c = c_ref[...]
        ang_p, ang_h = p * c[0:1], p * c[2:3]
        cp[...] = jnp.cos(ang_p) * jnp.abs(c[1:2])
        sp[...] = jnp.sin(ang_p) * c[1:2]
        ch[...] = jnp.cos(ang_h)
        sh[...] = jnp.sin(ang_h) * c[3:4]

    row = pl.BlockSpec((t, LANES), lambda i: (i, 0))
    return _pcall(body, name="rope_tables", grid=(S // t,),
                  in_specs=[pl.BlockSpec((t, 1), lambda i: (i, 0)), pl.BlockSpec((4, LANES), lambda i: (0, 0))],
                  out_specs=[row] * 4, out_shape=[jax.ShapeDtypeStruct((S, LANES), F32)] * 4,
                  compiler_params=_params(("parallel",)))(pos, inv_sign)


def _rope_consts():
    half_p, half_h = MLA_ROPE // 2, LANES // 2
    inv_p = 1.0 / (ROPE_THETA ** (jnp.arange(0, MLA_ROPE, 2, dtype=F32) / MLA_ROPE))
    inv_h = 1.0 / (ROPE_THETA ** (jnp.arange(0, LANES, 2, dtype=F32) / LANES))
    z = jnp.zeros((half_p,), F32)
    o = jnp.ones((half_p,), F32)
    return jnp.stack([jnp.concatenate([inv_p, z, inv_p, z]), jnp.concatenate([-o, z, o, z]),
                      jnp.concatenate([inv_h, inv_h]), jnp.concatenate([-jnp.ones((half_h,), F32), jnp.ones((half_h,), F32)])])


def _shift_rows(x, s, up):
    n = x.shape[0]
    r = lax.broadcasted_iota(jnp.int32, x.shape, 0)
    if up:
        return jnp.where(r < n - s, pltpu.roll(x, n - s, 0), 0.0)
    return jnp.where(r >= s, pltpu.roll(x, s, 0), 0.0)


def _make_shift(s):
    @jax.custom_vjp
    def sh(x):
        return _shift_rows(x, s, False)

    sh.defvjp(lambda x: (sh(x), None), lambda _, g: (_shift_rows(g, s, True),))
    return sh


def _f_conv(x, w):
    y = x * w[DN_CONV - 1:DN_CONV]
    for j in range(DN_CONV - 1):
        y = y + _make_shift(DN_CONV - 1 - j)(x) * w[j:j + 1]
    return _silu(y)


def _make_conv(name):
    def call(x, w, dy=None):
        S, C = x.shape
        col = pl.BlockSpec((S, LANES), lambda i: (0, i))
        wsp = pl.BlockSpec((DN_CONV, LANES), lambda i: (0, i))
        if dy is None:
            def body(x_ref, w_ref, o_ref):
                o_ref[...] = _f_conv(x_ref[...], w_ref[...])
            return _pcall(body, name=f"{name}_fwd", grid=(C // LANES,), in_specs=[col, wsp], out_specs=col,
                          out_shape=jax.ShapeDtypeStruct(x.shape, F32), compiler_params=_params(("parallel",)))(x, w)

        def body(x_ref, w_ref, dy_ref, dx_ref, dw_ref):
            _, vjp = jax.vjp(_f_conv, x_ref[...], w_ref[...])
            dx_ref[...], dw_ref[...] = vjp(dy_ref[...])
        return _pcall(body, name=f"{name}_bwd", grid=(C // LANES,), in_specs=[col, wsp, col], out_specs=[col, wsp],
                      out_shape=[jax.ShapeDtypeStruct(x.shape, F32), jax.ShapeDtypeStruct(w.shape, F32)],
                      compiler_params=_params(("parallel",)))(x, w, dy)

    @jax.custom_vjp
    def op(x, w):
        return call(x, w)

    op.defvjp(lambda x, w: (op(x, w), (x, w)), lambda res, dy: tuple(call(*res, dy)))
    return op


def _dot_nt(a, b):
    return lax.dot_general(a.astype(BF16), b.astype(BF16), (((1,), (1,)), ((), ())), preferred_element_type=F32)


def _dot_nn(a, b):
    return jnp.dot(a.astype(BF16), b.astype(BF16), preferred_element_type=F32)


def _dot_tn(a, b):
    return lax.dot_general(a.astype(BF16), b.astype(BF16), (((0,), (0,)), ((), ())), preferred_element_type=F32)


def _mla_scores(qn_r, qp_r, kn_r, kp_r, diagonal):
    scale = MLA_QK ** -0.5
    s = _dot_nt(qn_r[...] * scale, kn_r[...]) + _dot_nt(qp_r[...] * scale, kp_r[...])
    if diagonal:
        r = lax.broadcasted_iota(jnp.int32, s.shape, 0)
        c = lax.broadcasted_iota(jnp.int32, s.shape, 1)
        s = jnp.where(c <= r, s, NEG)
    return s


def _on_causal_pairs(qi, ki, step):
    @pl.when(ki < qi)
    def _():
        step(False)

    @pl.when(ki == qi)
    def _():
        step(True)


def _make_mla_attn(name):
    scale = MLA_QK ** -0.5

    def fwd_call(qn, qp, kn, kp, v):
        S = qn.shape[0]
        t = min(S, 512)
        n = S // t
        qs = pl.BlockSpec((t, LANES), lambda h, i, j: (i, h))
        ks = pl.BlockSpec((t, LANES), lambda h, i, j: (jnp.minimum(j, i), h))

        def body(qn_r, qp_r, kn_r, kp_r, v_r, o_r, lse_r, m_s, l_s, acc_s):
            qi, ki = pl.program_id(1), pl.program_id(2)

            @pl.when(ki == 0)
            def _():
                m_s[...] = jnp.full_like(m_s, NEG)
                l_s[...] = jnp.zeros_like(l_s)
                acc_s[...] = jnp.zeros_like(acc_s)

            def step(diagonal):
                s = _mla_scores(qn_r, qp_r, kn_r, kp_r, diagonal)
                m_old = m_s[...]
                m_new = jnp.maximum(m_old, jnp.max(s, axis=-1, keepdims=True))
                p = jnp.exp(s - m_new[:, :1])
                alpha = jnp.exp(m_old - m_new)
                l_s[...] = alpha * l_s[...] + jnp.sum(p, axis=-1, keepdims=True)
                acc_s[...] = alpha * acc_s[...] + _dot_nn(p, v_r[...])
                m_s[...] = m_new
            _on_causal_pairs(qi, ki, step)

            @pl.when(ki == n - 1)
            def _():
                o_r[...] = acc_s[...] / l_s[...]
                lse_r[...] = m_s[...] + jnp.log(l_s[...])

        return _pcall(body, name=f"{name}_fwd", grid=(MLA_HEADS, n, n), in_specs=[qs, qs, ks, ks, ks], out_specs=[qs, qs],
                      out_shape=[jax.ShapeDtypeStruct((S, MLA_HEADS * LANES), F32)] * 2,
                      scratch_shapes=[pltpu.VMEM((t, LANES), F32)] * 3,
                      compiler_params=_params(("parallel", "parallel", "arbitrary")))(qn, qp, kn, kp, v)

    def dq_call(qn, qp, kn, kp, v, o, lse, do):
        S = qn.shape[0]
        t = min(S, 512)
        n = S // t
        qs = pl.BlockSpec((t, LANES), lambda h, i, j: (i, h))
        ks = pl.BlockSpec((t, LANES), lambda h, i, j: (jnp.minimum(j, i), h))

        def body(qn_r, qp_r, kn_r, kp_r, v_r, o_r, lse_r, do_r, dqn_r, dqp_r, dl_r, dqn_s, dqp_s):
            qi, ki = pl.program_id(1), pl.program_id(2)

            @pl.when(ki == 0)
            def _():
                dqn_s[...] = jnp.zeros_like(dqn_s)
                dqp_s[...] = jnp.zeros_like(dqp_s)
                dl_r[...] = jnp.broadcast_to(jnp.sum(do_r[...] * o_r[...], axis=-1, keepdims=True), dl_r.shape)

            def step(diagonal):
                p = jnp.exp(_mla_scores(qn_r, qp_r, kn_r, kp_r, diagonal) - lse_r[...][:, :1])
                ds = p * (_dot_nt(do_r[...], v_r[...]) - dl_r[...][:, :1])
                dqn_s[...] += _dot_nn(ds, kn_r[...])
                dqp_s[...] += _dot_nn(ds, kp_r[...])
            _on_causal_pairs(qi, ki, step)

            @pl.when(ki == n - 1)
            def _():
                dqn_r[...] = dqn_s[...] * scale
                dqp_r[...] = dqp_s[...] * scale

        return _pcall(body, name=f"{name}_dq", grid=(MLA_HEADS, n, n), in_specs=[qs, qs, ks, ks, ks, qs, qs, qs],
                      out_specs=[qs, qs, qs], out_shape=[jax.ShapeDtypeStruct((S, MLA_HEADS * LANES), F32)] * 3,
                      scratch_shapes=[pltpu.VMEM((t, LANES), F32)] * 2,
                      compiler_params=_params(("parallel", "parallel", "arbitrary")))(qn, qp, kn, kp, v, o, lse, do)

    def dkv_call(qn, qp, kn, kp, v, lse, do, dl):
        S = qn.shape[0]
        t = min(S, 512)
        n = S // t
        ks = pl.BlockSpec((t, LANES), lambda h, j, i: (j, h))
        qs = pl.BlockSpec((t, LANES), lambda h, j, i: (jnp.maximum(i, j), h))

        def body(qn_r, qp_r, kn_r, kp_r, v_r, lse_r, do_r, dl_r, dkn_r, dkp_r, dv_r, dkn_s, dkp_s, dv_s):
            ki, qi = pl.program_id(1), pl.program_id(2)

            @pl.when(qi == 0)
            def _():
                dkn_s[...] = jnp.zeros_like(dkn_s)
                dkp_s[...] = jnp.zeros_like(dkp_s)
                dv_s[...] = jnp.zeros_like(dv_s)

            def step(diagonal):
                p = jnp.exp(_mla_scores(qn_r, qp_r, kn_r, kp_r, diagonal) - lse_r[...][:, :1])
                ds = p * (_dot_nt(do_r[...], v_r[...]) - dl_r[...][:, :1])
                dv_s[...] += _dot_tn(p, do_r[...])
                dkn_s[...] += _dot_tn(ds, qn_r[...] * scale)
                dkp_s[...] += _dot_tn(ds, qp_r[...] * scale)
            _on_causal_pairs(qi, ki, step)

            @pl.when(qi == n - 1)
            def _():
                dkn_r[...] = dkn_s[...]
                dkp_r[...] = dkp_s[...]
                dv_r[...] = dv_s[...]

        return _pcall(body, name=f"{name}_dkv", grid=(MLA_HEADS, n, n), in_specs=[qs, qs, ks, ks, ks, qs, qs, qs],
                      out_specs=[ks, ks, ks], out_shape=[jax.ShapeDtypeStruct((S, MLA_HEADS * LANES), F32)] * 3,
                      scratch_shapes=[pltpu.VMEM((t, LANES), F32)] * 3,
                      compiler_params=_params(("parallel", "parallel", "arbitrary")))(qn, qp, kn, kp, v, lse, do, dl)

    @jax.custom_vjp
    def op(qn, qp, kn, kp, v):
        return fwd_call(qn, qp, kn, kp, v)[0]

    def fwd(qn, qp, kn, kp, v):
        o, lse = fwd_call(qn, qp, kn, kp, v)
        return o, (qn, qp, kn, kp, v, o, lse)

    def bwd(res, do):
        qn, qp, kn, kp, v, o, lse = res
        dqn, dqp, dl = dq_call(qn, qp, kn, kp, v, o, lse, do)
        dkn, dkp, dv = dkv_call(qn, qp, kn, kp, v, lse, do, dl)
        return dqn, dqp, dkn, dkp, dv

    op.defvjp(fwd, bwd)
    return op


def _dil_block(q, kp, kc, vp, vc, has_prev):
    scale = LANES ** -0.5
    r = lax.broadcasted_iota(jnp.int32, (DIL_BLOCK, DIL_BLOCK), 0)
    c = lax.broadcasted_iota(jnp.int32, (DIL_BLOCK, DIL_BLOCK), 1)
    s_c = jnp.where(c <= r, _dot_nt(q, kc) * scale, NEG)
    s_p = jnp.where((c >= r) & has_prev, _dot_nt(q, kp) * scale, NEG)
    m = jnp.maximum(jnp.max(s_c, axis=-1, keepdims=True), jnp.max(s_p, axis=-1, keepdims=True))
    e_c, e_p = jnp.exp(s_c - m), jnp.exp(s_p - m)
    den = jnp.sum(e_c, axis=-1, keepdims=True) + jnp.sum(e_p, axis=-1, keepdims=True)
    o = (_dot_nn(e_c, vc) + _dot_nn(e_p, vp)) / den
    return o, jnp.broadcast_to(m + jnp.log(den), o.shape)


DIL_TILE_ROWS = (1024, 1024, 2048)


def _make_dil_attn(name, d, tile_rows):
    def call(q, k, v, cts=None):
        S = q.shape[0]
        span = DIL_BLOCK * d
        G = max(1, min(tile_rows, S) // span)
        n = S // (G * span)
        at = (lambda i: i) if cts is None else (lambda i: n - 1 - i)
        tile = pl.BlockSpec((G * span, LANES), lambda h, i: (at(i), h))
        before = pl.BlockSpec((span, LANES), lambda h, i: (jnp.maximum(at(i) * G - 1, 0), h))

        def rows(r, j):
            return pl.ds(j * DIL_BLOCK, DIL_BLOCK) if d == 1 else pl.ds(r + j * span, DIL_BLOCK, stride=d)

        def over_residues(fn):
            if d == 1:
                fn(0)
            else:
                lax.fori_loop(0, d, lambda r, c: (fn(r), c)[1], 0)

        def block_inputs(r, j, q_r, kb_r, k_r, vb_r, v_r):
            kp = kb_r[rows(r, 0), :] if j == 0 else k_r[rows(r, j - 1), :]
            vp = vb_r[rows(r, 0), :] if j == 0 else v_r[rows(r, j - 1), :]
            return q_r[rows(r, j), :], kp, k_r[rows(r, j), :], vp, v_r[rows(r, j), :]

        if cts is None:
            def body(q_r, kb_r, k_r, vb_r, v_r, o_r, lse_r):
                first = at(pl.program_id(1)) * G

                def residue(r):
                    for j in range(G):
                        o_r[rows(r, j), :], lse_r[rows(r, j), :] = _dil_block(
                            *block_inputs(r, j, q_r, kb_r, k_r, vb_r, v_r), first + j > 0)
                over_residues(residue)
            return _pcall(body, name=f"{name}_fwd", grid=(DIL_GROUP_HEADS, n), in_specs=[tile, before, tile, before, tile],
                          out_specs=[tile, tile], out_shape=[jax.ShapeDtypeStruct(q.shape, F32)] * 2,
                          compiler_params=_params(("parallel", "parallel")))(q, k, k, v, v)

        def body(q_r, kb_r, k_r, vb_r, v_r, do_r, dl_r, dq_r, dk_r, dv_r, ck_s, cv_s):
            first = at(pl.program_id(1)) * G

            @pl.when(pl.program_id(1) == 0)
            def _():
                ck_s[...] = jnp.zeros_like(ck_s)
                cv_s[...] = jnp.zeros_like(cv_s)

            def residue(r):
                owed = None
                for j in range(G):
                    hp = first + j > 0
                    _, vjp = jax.vjp(lambda *a: _dil_block(*a, hp), *block_inputs(r, j, q_r, kb_r, k_r, vb_r, v_r))
                    dq, dkp, dkc, dvp, dvc = vjp((do_r[rows(r, j), :], dl_r[rows(r, j), :]))
                    dq_r[rows(r, j), :] = dq
                    if j == G - 1:
                        dkc, dvc = dkc + ck_s[rows(r, 0), :], dvc + cv_s[rows(r, 0), :]
                    dk_r[rows(r, j), :], dv_r[rows(r, j), :] = dkc, dvc
                    if j == 0:
                        owed = (dkp, dvp)
                    else:
                        dk_r[rows(r, j - 1), :] += dkp
                        dv_r[rows(r, j - 1), :] += dvp
                ck_s[rows(r, 0), :], cv_s[rows(r, 0), :] = owed
            over_residues(residue)
        return _pcall(body, name=f"{name}_bwd", grid=(DIL_GROUP_HEADS, n), in_specs=[tile, before, tile, before, tile, tile, tile],
                      out_specs=[tile] * 3, out_shape=[jax.ShapeDtypeStruct(q.shape, F32)] * 3,
                      scratch_shapes=[pltpu.VMEM((span, LANES), F32)] * 2,
                      compiler_params=_params(("parallel", "arbitrary")))(q, k, k, v, v, *cts)

    @jax.custom_vjp
    def op(q, k, v):
        return tuple(call(q, k, v))

    op.defvjp(lambda q, k, v: (op(q, k, v), (q, k, v)), lambda res, cts: tuple(call(*res, cts=cts)))
    return op


def _pdot(a, b, dims):
    return lax.dot_general(a, b, (dims, ((), ())), precision=lax.Precision.HIGH, preferred_element_type=F32)


DN_LOCAL_CHUNKS = 4


DN_BLOCK_HEADS = 4
DN_BLOCK = DN_BLOCK_HEADS * DN_CHUNK


def _dn_local(q, k, v, g, b):
    C, R = DN_CHUNK, DN_BLOCK
    r = lax.broadcasted_iota(jnp.int32, (R, R), 0)
    c = lax.broadcasted_iota(jnp.int32, (R, R), 1)
    same_head = (r // C) == (c // C)
    incl, strict = same_head & (r >= c), same_head & (r > c)
    eye = (r == c).astype(F32)
    avg = jnp.full((R, LANES), 1.0 / LANES, F32)
    rc = lax.broadcasted_iota(jnp.int32, (C, C), 0) >= lax.broadcasted_iota(jnp.int32, (C, C), 1)
    gc_lanes = _pdot(rc.astype(F32), g, ((1,), (0,)))
    us, ws, qes, kds, qks = [], [], [], [], []
    for first in range(0, DN_HEADS, DN_BLOCK_HEADS):
        stack = lambda x: jnp.concatenate(_heads(x)[first:first + DN_BLOCK_HEADS], axis=0)
        unstack = lambda x: [x[p * C:(p + 1) * C] for p in range(DN_BLOCK_HEADS)]
        gc, q_s, k_s, v_s, b_s = (stack(x) for x in (gc_lanes, q, k, v, b))
        gc_j = _pdot(avg, gc, ((1,), (1,)))
        decay = jnp.exp(jnp.where(incl, _cat([gc] * (R // LANES)) - gc_j, NEG))
        kb = k_s * b_s
        kk = _pdot(jnp.concatenate([kb, q_s], axis=0), k_s, ((1,), (1,)))
        a = jnp.where(strict, kk[:R] * decay, 0.0)
        inv, pw = eye - a, a
        for _ in range(5):
            pw = _pdot(pw, pw, ((1,), (0,)))
            inv = inv + _pdot(inv, pw, ((1,), (0,)))
        eg = jnp.exp(gc)
        uw = _pdot(inv, _cat([v_s * b_s, kb * eg]), ((1,), (0,)))
        g_last = jnp.concatenate([jnp.broadcast_to(x[C - 1:C], (C, LANES)) for x in unstack(gc)], axis=0)
        us += unstack(uw[:, :LANES])
        ws += unstack(uw[:, LANES:])
        qes += unstack(q_s * eg)
        kds += unstack(k_s * jnp.exp(g_last - gc))
        qks.append(kk[R:] * decay)
    egl = jnp.broadcast_to(jnp.exp(gc_lanes[C - 1:C]), (8, DN_HEADS * LANES))
    return _cat(us), _cat(ws), _cat(qes), _cat(kds), jnp.concatenate(qks, axis=0), egl


def _dn_scan(u, w, qe, kd, qk, egl, state):
    C = DN_CHUNK
    heads = [slice(h * LANES, (h + 1) * LANES) for h in range(DN_HEADS)]
    ws = [_pdot(jnp.concatenate([w[:, sl], qe[:, sl]], axis=0), state[sl, :], ((1,), (0,))) for sl in heads]
    v_new = [u[:, sl] - x[:C] for sl, x in zip(heads, ws)]
    local = []
    for i, first in enumerate(range(0, DN_HEADS, DN_BLOCK_HEADS)):
        y = _pdot(qk[i * DN_BLOCK:(i + 1) * DN_BLOCK], jnp.concatenate(v_new[first:first + DN_BLOCK_HEADS], axis=0), ((1,), (0,)))
        local += [y[p * C:(p + 1) * C] for p in range(DN_BLOCK_HEADS)]
    o = _cat([x[C:] + y for x, y in zip(ws, local)])
    states = [state[sl, :] * egl[0:1, sl] + _pdot(kd[:, sl], vn, ((0,), (0,))) for sl, vn in zip(heads, v_new)]
    return o, jnp.concatenate(states, axis=0)


def _make_delta_rule(name):
    W = DN_HEADS * LANES
    QK = DN_HEADS * DN_CHUNK

    def local_call(ins, cts=None):
        S = ins[0].shape[0]
        n = S // DN_CHUNK
        per = math.gcd(DN_LOCAL_CHUNKS, n)
        row = pl.BlockSpec((per * DN_CHUNK, W), lambda i: (i, 0))
        qkb = pl.BlockSpec((per * QK, DN_BLOCK), lambda i: (i, 0))
        eg = pl.BlockSpec((per, 8, W), lambda i: (i, 0, 0))
        rows = lambda j: slice(j * DN_CHUNK, (j + 1) * DN_CHUNK)
        qk_rows = lambda j: slice(j * QK, (j + 1) * QK)
        out_rows = [rows, rows, rows, rows, qk_rows]

        if cts is None:
            def body(*refs):
                for j in range(per):
                    res = _dn_local(*[r[rows(j), :] for r in refs[:5]])
                    for o_r, o, at_ in zip(refs[5:10], res[:5], out_rows):
                        o_r[at_(j), :] = o
                    refs[10][j] = res[5]
            return _pcall(body, name=f"{name}_local_fwd", grid=(n // per,), in_specs=[row] * 5, out_specs=[row] * 4 + [qkb, eg],
                          out_shape=[jax.ShapeDtypeStruct((S, W), F32)] * 4
                          + [jax.ShapeDtypeStruct((n * QK, DN_BLOCK), F32), jax.ShapeDtypeStruct((n, 8, W), F32)],
                          compiler_params=_params(("parallel",)))(*ins)

        def body(*refs):
            for j in range(per):
                _, vjp = jax.vjp(_dn_local, *[r[rows(j), :] for r in refs[:5]])
                grads = vjp(tuple(r[at_(j), :] for r, at_ in zip(refs[5:10], out_rows)) + (refs[10][j],))
                for o_r, o in zip(refs[11:], grads):
                    o_r[rows(j), :] = o
        return _pcall(body, name=f"{name}_local_bwd", grid=(n // per,), in_specs=[row] * 9 + [qkb, eg], out_specs=[row] * 5,
                      out_shape=[jax.ShapeDtypeStruct((S, W), F32)] * 5, compiler_params=_params(("parallel",)))(*ins, *cts)

    def scan_call(ins, saved=None, do=None):
        S = ins[0].shape[0]
        n = S // DN_CHUNK
        at = (lambda i: i) if do is None else (lambda i: n - 1 - i)
        row = pl.BlockSpec((DN_CHUNK, W), lambda i: (at(i), 0))
        qkb = pl.BlockSpec((QK, DN_BLOCK), lambda i: (at(i), 0))
        eg = pl.BlockSpec((None, 8, W), lambda i: (at(i), 0, 0))
        st = pl.BlockSpec((None, W, LANES), lambda i: (at(i), 0, 0))

        if do is None:
            def body(*refs):
                o_r, st_r, s_s = refs[6:]

                @pl.when(pl.program_id(0) == 0)
                def _():
                    s_s[...] = jnp.zeros_like(s_s)
                st_r[...] = s_s[...]
                o_r[...], s_s[...] = _dn_scan(*[r[...] for r in refs[:6]], s_s[...])
            return _pcall(body, name=f"{name}_scan_fwd", grid=(n,), in_specs=[row] * 4 + [qkb, eg], out_specs=[row, st],
                          out_shape=[jax.ShapeDtypeStruct((S, W), F32), jax.ShapeDtypeStruct((n, W, LANES), F32)],
                          scratch_shapes=[pltpu.VMEM((W, LANES), F32)], compiler_params=_params(("arbitrary",)))(*ins)

        def body(*refs):
            st_r, do_r = refs[6:8]
            outs, ds_s = refs[8:14], refs[14]

            @pl.when(pl.program_id(0) == 0)
            def _():
                ds_s[...] = jnp.zeros_like(ds_s)
            _, vjp = jax.vjp(_dn_scan, *[r[...] for r in refs[:6]], st_r[...])
            *grads, ds = vjp((do_r[...], ds_s[...]))
            for o_r, gval in zip(outs, grads):
                o_r[...] = gval
            ds_s[...] = ds
        return _pcall(body, name=f"{name}_scan_bwd", grid=(n,), in_specs=[row] * 4 + [qkb, eg, st, row], out_specs=[row] * 4 + [qkb, eg],
                      out_shape=[jax.ShapeDtypeStruct((S, W), F32)] * 4
                      + [jax.ShapeDtypeStruct((n * QK, DN_BLOCK), F32), jax.ShapeDtypeStruct((n, 8, W), F32)],
                      scratch_shapes=[pltpu.VMEM((W, LANES), F32)], compiler_params=_params(("arbitrary",)))(*ins, saved, do)

    @jax.custom_vjp
    def local(q, k, v, g, b):
        return tuple(local_call((q, k, v, g, b)))

    local.defvjp(lambda *a: (local(*a), a), lambda res, cts: tuple(local_call(res, tuple(cts))))

    @jax.custom_vjp
    def scan(u, w, qe, kd, qk, egl):
        return scan_call((u, w, qe, kd, qk, egl))[0]

    def scan_fwd(*a):
        o, states = scan_call(a)
        return o, (a, states)

    scan.defvjp(scan_fwd, lambda res, do: tuple(scan_call(res[0], res[1], do)))
    return lambda q, k, v, g, b: scan(*local(q, k, v, g, b))


def _loss_call(y, target):
    S, D = y.shape
    t = min(S, 512)
    n = S // t
    row = pl.BlockSpec((t, D), lambda i: (i, 0))

    def body(y_r, t_r, loss_r, dy_r, acc_s):
        i = pl.program_id(0)

        @pl.when(i == 0)
        def _():
            acc_s[...] = jnp.zeros_like(acc_s)
        err = y_r[...] - t_r[...]
        dy_r[...] = err * (1.0 / D)
        acc_s[...] += jnp.sum(err * err, axis=0, keepdims=True)

        @pl.when(i == n - 1)
        def _():
            loss_r[...] = jnp.broadcast_to(jnp.sum(acc_s[...], axis=1, keepdims=True) * (0.5 / D), loss_r.shape)

    return _pcall(body, name="loss_head", grid=(n,), in_specs=[row, row],
                  out_specs=[pl.BlockSpec((8, LANES), lambda i: (0, 0)), row],
                  out_shape=[jax.ShapeDtypeStruct((8, LANES), F32), jax.ShapeDtypeStruct((S, D), F32)],
                  scratch_shapes=[pltpu.VMEM((1, D), F32)], compiler_params=_params(("arbitrary",)))(y, target)


def _adamw_call(name, parts, w, m, v, rows=128):
    L, R, C = w.shape
    assert len(parts) == L
    t = _tile(R, rows, 8)
    row = pl.BlockSpec((None, t, C), lambda l, i: (l, i, 0))
    part = lambda k: pl.BlockSpec((N_DEV, t, C), lambda l, i: (0, jnp.where(l == k, i, 0), 0))

    def body(*refs):
        p_refs, (w_r, m_r, v_r, g_r, d_r, nm_r, nv_r) = refs[:L], refs[L:]

        def update(p_r):
            g = p_r[0].astype(F32)
            for s in range(1, N_DEV):
                g = g + p_r[s].astype(F32)
            m_new = ADAM_B1 * m_r[...] + (1.0 - ADAM_B1) * g
            v_new = ADAM_B2 * v_r[...] + (1.0 - ADAM_B2) * (g * g)
            m_hat = m_new / (1.0 - ADAM_B1 ** ADAM_STEP)
            v_hat = v_new / (1.0 - ADAM_B2 ** ADAM_STEP)
            g_r[...] = g
            d_r[...] = -ADAM_LR * (m_hat / (jnp.sqrt(v_hat) + ADAM_EPS) + ADAM_WD * w_r[...])
            nm_r[...] = m_new
            nv_r[...] = v_new

        for k in range(L):
            pl.when(pl.program_id(0) == k)(functools.partial(update, p_refs[k]))

    return _pcall(body, name=name, grid=(L, R // t), in_specs=[part(k) for k in range(L)] + [row, row, row],
                  out_specs=[row] * 4, out_shape=[jax.ShapeDtypeStruct((L, R, C), F32)] * 4,
                  compiler_params=_params(("arbitrary", "arbitrary")))(*parts, w, m, v)


def _my_place():
    x, y, c = lax.axis_index("x"), lax.axis_index("y"), lax.axis_index("c")
    return x, y, c


def _index(x, y, c):
    return 4 * x + 2 * y + c


def _all_gather(vs):
    n = len(vs)

    def body(*refs):
        v_refs, out_refs = refs[:n], refs[n:2 * n]
        send_sems, recv_sems, local_sems = refs[2 * n:]
        x, y, c = _my_place()
        me, sibling = (x, y, c), (x, y, 1 - c)
        chips = [(1 - x, y), (x, 1 - y), (1 - x, 1 - y)]

        def copy(a, k, block, to, src=None):
            rows = out_refs[a].at[_index(*block)]
            return pltpu.make_async_remote_copy(src_ref=rows if src is None else src, dst_ref=rows, send_sem=send_sems.at[a, k],
                                                recv_sem=recv_sems.at[a, k], device_id=to, device_id_type=MESH)

        mine = [pltpu.make_async_copy(v_refs[a], out_refs[a].at[_index(*me)], local_sems.at[a]) for a in range(n)]
        first, passed = [], []
        for a in range(n):
            mine[a].start()
            first += [copy(a, 0, me, sibling, src=v_refs[a])]
            first += [copy(a, 1 + j, me, (*chip, c), src=v_refs[a]) for j, chip in enumerate(chips)]
        for cp in first:
            cp.start()
        for j, chip in enumerate(chips):
            for a in range(n):
                copy(a, 1 + j, (*chip, c), me).wait_recv()
                passed.append(copy(a, 4 + j, (*chip, c), sibling))
                passed[-1].start()
        for a in range(n):
            copy(a, 0, sibling, me).wait_recv()
            for j, chip in enumerate(chips):
                copy(a, 4 + j, (*chip, 1 - c), me).wait_recv()
        for cp in first + passed:
            cp.wait_send()
        for a in range(n):
            mine[a].wait()

    any_ = pl.BlockSpec(memory_space=pl.ANY)
    return _pcall(body, name="gather_weights", in_specs=[any_] * n, out_specs=[any_] * n,
                  out_shape=[jax.ShapeDtypeStruct((N_DEV,) + v.shape, v.dtype) for v in vs],
                  scratch_shapes=[pltpu.SemaphoreType.DMA((n, 7)), pltpu.SemaphoreType.DMA((n, 7)), pltpu.SemaphoreType.DMA((n,))])(*vs)


def _all_to_all(vs):
    n = len(vs)

    def body(*refs):
        v_refs, out_refs = refs[:n], refs[n:2 * n]
        send_sems, recv_sems, local_sems = refs[2 * n:]
        x, y, c = _my_place()
        me = _index(x, y, c)
        mine = [pltpu.make_async_copy(v_refs[a].at[me], out_refs[a].at[me], local_sems.at[a]) for a in range(n)]
        copies = []
        for a in range(n):
            mine[a].start()
        for k in range(1, N_DEV):
            px = 1 - x if k & 4 else x
            py = 1 - y if k & 2 else y
            pc = 1 - c if k & 1 else c
            for a in range(n):
                cp = pltpu.make_async_remote_copy(src_ref=v_refs[a].at[_index(px, py, pc)], dst_ref=out_refs[a].at[me],
                                                  send_sem=send_sems.at[a, k - 1], recv_sem=recv_sems.at[a, k - 1],
                                                  device_id=(px, py, pc), device_id_type=MESH)
                cp.start()
                copies.append(cp)
        for cp in copies:
            cp.wait()
        for a in range(n):
            mine[a].wait()

    any_ = pl.BlockSpec(memory_space=pl.ANY)
    return _pcall(body, name="exchange_grads", in_specs=[any_] * n, out_specs=[any_] * n,
                  out_shape=[jax.ShapeDtypeStruct(v.shape, v.dtype) for v in vs],
                  scratch_shapes=[pltpu.SemaphoreType.DMA((n, 7)), pltpu.SemaphoreType.DMA((n, 7)), pltpu.SemaphoreType.DMA((n,))])(*vs)


_HBM = pl.BlockSpec(memory_space=pltpu.HBM)
_SEM = pl.BlockSpec(memory_space=pltpu.SEMAPHORE)
_EFFECT = pltpu.SideEffectType.DATAFLOW_SIDE_EFFECTING


def _direct_copies(gather, v_refs, land_refs, send_sems, recv_sems, local_sems):
    x, y, c = _my_place()
    me = _index(x, y, c)
    local, remote = [], []
    for a, (v_ref, land_ref) in enumerate(zip(v_refs, land_refs)):
        local.append(pltpu.make_async_copy(v_ref if gather else v_ref.at[me], land_ref.at[me], local_sems.at[a]))
    for k in range(1, N_DEV):
        px = 1 - x if k & 4 else x
        py = 1 - y if k & 2 else y
        pc = 1 - c if k & 1 else c
        for a, (v_ref, land_ref) in enumerate(zip(v_refs, land_refs)):
            sem = a * (N_DEV - 1) + k - 1
            remote.append(pltpu.make_async_remote_copy(
                src_ref=v_ref if gather else v_ref.at[_index(px, py, pc)], dst_ref=land_ref.at[me], send_sem=send_sems.at[sem],
                recv_sem=recv_sems.at[sem], device_id=(px, py, pc), device_id_type=MESH))
    return local, remote


def _exchange_start(name, vs, gather, thru):
    n = len(vs)
    lands = [lax.empty((N_DEV,) + v.shape if gather else v.shape, v.dtype) for v in vs]

    def body(*refs):
        v_refs, land_refs = refs[:n], refs[n:2 * n]
        send_sems, recv_sems, local_sems = refs[2 * n + 1:2 * n + 4]
        local, remote = _direct_copies(gather, v_refs, land_refs, send_sems, recv_sems, local_sems)
        for cp in local + remote:
            cp.start()

    hbm = lambda a: pltpu.HBM(a.shape, a.dtype)
    res = _pcall(body, name=name,
                 out_shape=(pltpu.SemaphoreType.DMA((n * (N_DEV - 1),)), pltpu.SemaphoreType.DMA((n * (N_DEV - 1),)), pltpu.SemaphoreType.DMA((n,)),
                            *[hbm(a) for a in (*vs, *lands, thru)]),
                 in_specs=[_HBM] * (2 * n + 1), out_specs=(_SEM, _SEM, _SEM, *[_HBM] * (2 * n + 1)),
                 input_output_aliases={i: 3 + i for i in range(2 * n + 1)},
                 compiler_params=pltpu.CompilerParams(has_side_effects=_EFFECT))(
        *[pltpu.with_memory_space_constraint(a, pltpu.HBM) for a in (*vs, *lands, thru)])
    return (gather, res[:3], res[3:3 + n], res[3 + n:3 + 2 * n]), res[3 + 2 * n]


def _exchange_wait(name, started, after):
    gather, sems, vs, lands = started
    n = len(vs)

    def body(*refs):
        v_refs, land_refs = refs[:n], refs[n:2 * n]
        send_sems, recv_sems, local_sems = refs[2 * n:2 * n + 3]
        local, remote = _direct_copies(gather, v_refs, land_refs, send_sems, recv_sems, local_sems)
        for cp in local:
            cp.wait()
        for cp in remote:
            cp.wait_send()
            cp.wait_recv()

    hbm = lambda a: pltpu.HBM(a.shape, a.dtype)
    res = _pcall(body, name=name, out_shape=tuple(hbm(a) for a in (*vs, *lands)),
                 in_specs=[_HBM] * (2 * n) + [_SEM] * 3 + [pl.BlockSpec(memory_space=pl.ANY)], out_specs=tuple([_HBM] * (2 * n)),
                 input_output_aliases={i: i for i in range(2 * n)},
                 compiler_params=pltpu.CompilerParams(has_side_effects=_EFFECT))(*vs, *lands, *sems, after)
    return list(res[n:])


W_IN_SHARD = IN_WIDTH // N_DEV
SEG_ORDER = ("q_lat", "c_kv", "k_pe", "z_a", "dn_qkv", "dn_ab", "z_b", "dil_qkv", "z_c", "gate")
SEG_WIDTH = (384, 256, LANES, 512, 1536, LANES, 512, 4608, 512, 3072)


def _w_in_plan():
    plan = []

    def add(seg, c0, c1, dst):
        while c0 < c1:
            d = c0 // W_IN_SHARD
            e = min(c1, (d + 1) * W_IN_SHARD)
            plan.append((seg, dst, d, c0 - d * W_IN_SHARD, e - c0))
            dst += e - c0
            c0 = e

    half = MLA_ROPE // 2
    for i, name in enumerate(SEG_ORDER):
        if name == "k_pe":
            o = _SEG["k_pe"][0]
            add(i, o, o + half, 0)
            add(i, o + half, o + 2 * half, LANES // 2)
        elif name == "dn_ab":
            o = _SEG["dn_a"][0]
            add(i, o, o + 2 * DN_HEADS, 0)
        else:
            o, w = _SEG[name]
            add(i, o, o + w, 0)
    return plan


def _make_w_in_segments(name):
    plan = _w_in_plan()
    nseg = len(SEG_ORDER)
    t = 256

    def fwd_call(g):
        L = g.shape[1]

        def body(g_ref, *o_refs):
            for i in (SEG_ORDER.index("k_pe"), SEG_ORDER.index("dn_ab")):
                o_refs[i][...] = jnp.zeros_like(o_refs[i])
            for seg, dst, d, src, n in plan:
                o_refs[seg][:, dst:dst + n] = g_ref[d, :, src:src + n]

        return _pcall(body, name=f"{name}_fwd", grid=(L, D_MODEL // t),
                      in_specs=[pl.BlockSpec((N_DEV, None, t, W_IN_SHARD), lambda l, i: (0, l, i, 0))],
                      out_specs=[pl.BlockSpec((None, t, w), lambda l, i: (l, i, 0)) for w in SEG_WIDTH],
                      out_shape=[jax.ShapeDtypeStruct((L, D_MODEL, w), g.dtype) for w in SEG_WIDTH],
                      compiler_params=_params(("parallel", "parallel")))(g)

    def bwd_call(ds):
        L = ds[0].shape[0]

        def body(*refs):
            d_refs, g_ref = refs[:nseg], refs[nseg]
            for seg, dst, d, src, n in plan:
                g_ref[d, :, src:src + n] = d_refs[seg][:, dst:dst + n]

        return _pcall(body, name=f"{name}_bwd", grid=(L, D_MODEL // t),
                      in_specs=[pl.BlockSpec((None, t, w), lambda l, i: (l, i, 0)) for w in SEG_WIDTH],
                      out_specs=pl.BlockSpec((N_DEV, None, t, W_IN_SHARD), lambda l, i: (0, l, i, 0)),
                      out_shape=jax.ShapeDtypeStruct((N_DEV, L, D_MODEL, W_IN_SHARD), ds[0].dtype),
                      compiler_params=_params(("parallel", "parallel")))(*ds)

    @jax.custom_vjp
    def op(g):
        return tuple(fwd_call(g))

    op.defvjp(lambda g: (op(g), None), lambda _, ds: (bwd_call(tuple(ds)),))
    return op


def _pe_pad(a):
    h = MLA_ROPE // 2
    z = jnp.zeros(a.shape[:-1] + (h,), a.dtype)
    return jnp.concatenate([a[..., :h], z, a[..., h:], z], axis=-1)


def _layer_norm(tag, x, norm_g):
    return _make_rowwise(f"{tag}_norm", _f_norm, 512)((x,), (), (norm_g[None, :],), ())[0]


def _layer(tag, x, tables, W):
    h = _layer_norm(tag, x, W["norm_g"])
    return _layer_tail(tag, x, _make_multi_linear(f"{tag}_inproj", 10)(h, W["w_in_segments"]), tables, W)


def _layer_tail(tag, x, segments, tables, W):
    cos_p, sin_p, cos_h, sin_h = tables
    row = lambda a: a[None, :]
    q_lat, c_kv, kpe, z_a, dn_qkv, ab, z_b, dil_qkv, z_c, gl = segments

    qn_lat, ckvn, kp = _make_rowwise(f"{tag}_mla_a", _f_mla_a, 512)(
        (q_lat, c_kv, kpe), (cos_p, sin_p),
        (row(W["mla_q_a_norm_g"]), row(W["mla_kv_a_norm_g"]), row(_pe_pad(W["mla_k_norm_g"][LANES:]))), ())
    wq = W["mla_w_q_b"].reshape(MLA_Q_RANK, MLA_HEADS, MLA_QK)
    wq = jnp.concatenate([wq[:, :, :LANES].reshape(MLA_Q_RANK, -1), _pe_pad(wq[:, :, LANES:]).reshape(MLA_Q_RANK, -1)], axis=1)
    wkv = W["mla_w_kv_b"].reshape(MLA_KV_RANK, MLA_HEADS, 2 * LANES)
    (q8,) = _make_multi_linear(f"{tag}_qb", 1)(qn_lat, (wq,))
    kn_raw, v_mla = _make_multi_linear(f"{tag}_kvb", 2)(
        ckvn, (wkv[:, :, :LANES].reshape(MLA_KV_RANK, -1), wkv[:, :, LANES:].reshape(MLA_KV_RANK, -1)))
    qn, qp, kn = _make_rowwise(f"{tag}_mla_b", _f_mla_b, 512)(
        (q8, kn_raw), (cos_p, sin_p),
        (row(W["mla_q_norm_g"][:LANES]), row(_pe_pad(W["mla_q_norm_g"][LANES:])), row(W["mla_k_norm_g"][:LANES])), ())
    y_a = _make_mla_attn(f"{tag}_mla")(qn, qp, kn, kp, v_mla)

    mixed = _make_conv(f"{tag}_conv")(dn_qkv, W["dn_conv_w"])
    lane_head = jnp.arange(DN_HEADS * LANES) // LANES
    e_a = (jnp.arange(LANES)[:, None] == lane_head[None, :]).astype(F32)
    e_b = (jnp.arange(LANES)[:, None] == lane_head[None, :] + DN_HEADS).astype(F32)
    q_dn, k_dn, v_dn, g_dn, b_dn = _make_rowwise(f"{tag}_dn_pre", _f_dn_pre, 512)(
        (mixed, ab), (), (row(jnp.repeat(W["dn_a_log"], LANES)), row(jnp.repeat(W["dn_dt_bias"], LANES))), (e_a, e_b))
    o_dn = _make_delta_rule(f"{tag}_dn")(q_dn, k_dn, v_dn, g_dn, b_dn)

    qkv_dil = _make_rowwise(f"{tag}_dil_pre", _f_dil_pre, 256)(
        (dil_qkv,), (cos_h, sin_h), (row(W["dil_q_norm_g"]), row(W["dil_k_norm_g"])), ())
    n_groups = len(DIL_DILATIONS)
    o_lse = [_make_dil_attn(f"{tag}_dil{g}", d, DIL_TILE_ROWS[g])(qkv_dil[g], qkv_dil[n_groups + g], qkv_dil[2 * n_groups + g])
             for g, d in enumerate(DIL_DILATIONS)]

    ya, yb, yc = _make_rowwise(f"{tag}_merge_a", _f_merge_a, 256)(
        (y_a, z_a, o_dn, z_b, *[o for o, _ in o_lse], *[l for _, l in o_lse], z_c), (), (row(W["dn_out_norm_g"]),), ())
    (b0,) = _make_multi_linear(f"{tag}_br0", 1)(ya, (W["w_branch"][0],))
    (b1,) = _make_multi_linear(f"{tag}_br1", 1)(yb, (W["w_branch"][1],))
    (b2,) = _make_multi_linear(f"{tag}_br2", 1)(yc, (W["w_branch"][2],))
    (mix,) = _make_rowwise(f"{tag}_merge_b", _f_merge_b, 256)((b0, b1, b2, gl), (), (), ())
    return _make_resid_linear(f"{tag}_out")(x, mix, W["w_out"])


SHARDED = (("w_in", (D_MODEL, W_IN_SHARD)), ("mla_w_q_b", (MLA_Q_RANK, MLA_HEADS * MLA_QK // N_DEV)),
           ("mla_w_kv_b", (MLA_KV_RANK, MLA_HEADS * 2 * LANES // N_DEV)), ("w_branch", (3 * BRANCH_W, D_MODEL // N_DEV)),
           ("w_out", (D_MODEL // N_DEV, D_MODEL)), ("dn_conv_w", (DN_CONV, 3 * DN_HEADS * LANES // N_DEV)))
SMALL = (("norm_g", D_MODEL), ("mla_q_a_norm_g", MLA_Q_RANK), ("mla_kv_a_norm_g", MLA_KV_RANK), ("mla_q_norm_g", MLA_QK),
         ("mla_k_norm_g", MLA_QK), ("dn_a_log", DN_HEADS), ("dn_dt_bias", DN_HEADS), ("dn_out_norm_g", LANES),
         ("dil_q_norm_g", LANES), ("dil_k_norm_g", LANES))
WEIGHTS = ("norm_g", "w_in", "mla_q_a_norm_g", "mla_w_q_b", "mla_kv_a_norm_g", "mla_w_kv_b", "mla_q_norm_g", "mla_k_norm_g",
           "dn_conv_w", "dn_a_log", "dn_dt_bias", "dn_out_norm_g", "dil_q_norm_g", "dil_k_norm_g", "w_branch", "w_out")


def _round_up(n, m):
    return -(-n // m) * m


def _pack_vectors(pieces):
    return jnp.concatenate([jnp.pad(p, (0, _round_up(p.shape[0], LANES) - p.shape[0])) for p in pieces]).reshape(-1, LANES)


def _unpack_vectors(flat, sizes):
    out, off = [], 0
    flat = flat.reshape(-1)
    for n in sizes:
        out.append(flat[off:off + n])
        off += _round_up(n, LANES)
    return out


def _whole_weights(g, small):
    W = dict(small)
    W["mla_w_q_b"] = g["mla_w_q_b"].transpose(1, 0, 2).reshape(MLA_Q_RANK, -1)
    W["mla_w_kv_b"] = g["mla_w_kv_b"].transpose(1, 0, 2).reshape(MLA_KV_RANK, -1)
    W["w_branch"] = g["w_branch"].reshape(N_DEV, 3, BRANCH_W, -1).transpose(1, 2, 0, 3).reshape(3, BRANCH_W, D_MODEL)
    W["w_out"] = g["w_out"].reshape(D_MODEL, D_MODEL)
    W["dn_conv_w"] = g["dn_conv_w"].transpose(1, 0, 2).reshape(DN_CONV, -1)
    return W


def kernel(x, positions, norm_g, w_in, mla_q_a_norm_g, mla_w_q_b, mla_kv_a_norm_g, mla_w_kv_b, mla_q_norm_g, mla_k_norm_g, dn_conv_w, dn_a_log, dn_dt_bias, dn_out_norm_g, dil_q_norm_g, dil_k_norm_g, w_branch, w_out, loss_target, m_norm_g, m_w_in, m_mla_q_a_norm_g, m_mla_w_q_b, m_mla_kv_a_norm_g, m_mla_w_kv_b, m_mla_q_norm_g, m_mla_k_norm_g, m_dn_conv_w, m_dn_a_log, m_dn_dt_bias, m_dn_out_norm_g, m_dil_q_norm_g, m_dil_k_norm_g, m_w_branch, m_w_out, v_norm_g, v_w_in, v_mla_q_a_norm_g, v_mla_w_q_b, v_mla_kv_a_norm_g, v_mla_w_kv_b, v_mla_q_norm_g, v_mla_k_norm_g, v_dn_conv_w, v_dn_a_log, v_dn_dt_bias, v_dn_out_norm_g, v_dil_q_norm_g, v_dil_k_norm_g, v_w_branch, v_w_out):
    w = dict(norm_g=norm_g, w_in=w_in, mla_q_a_norm_g=mla_q_a_norm_g, mla_w_q_b=mla_w_q_b, mla_kv_a_norm_g=mla_kv_a_norm_g,
             mla_w_kv_b=mla_w_kv_b, mla_q_norm_g=mla_q_norm_g, mla_k_norm_g=mla_k_norm_g, dn_conv_w=dn_conv_w, dn_a_log=dn_a_log,
             dn_dt_bias=dn_dt_bias, dn_out_norm_g=dn_out_norm_g, dil_q_norm_g=dil_q_norm_g, dil_k_norm_g=dil_k_norm_g,
             w_branch=w_branch, w_out=w_out)
    m = dict(norm_g=m_norm_g, w_in=m_w_in, mla_q_a_norm_g=m_mla_q_a_norm_g, mla_w_q_b=m_mla_w_q_b, mla_kv_a_norm_g=m_mla_kv_a_norm_g,
             mla_w_kv_b=m_mla_w_kv_b, mla_q_norm_g=m_mla_q_norm_g, mla_k_norm_g=m_mla_k_norm_g, dn_conv_w=m_dn_conv_w,
             dn_a_log=m_dn_a_log, dn_dt_bias=m_dn_dt_bias, dn_out_norm_g=m_dn_out_norm_g, dil_q_norm_g=m_dil_q_norm_g,
             dil_k_norm_g=m_dil_k_norm_g, w_branch=m_w_branch, w_out=m_w_out)
    v = dict(norm_g=v_norm_g, w_in=v_w_in, mla_q_a_norm_g=v_mla_q_a_norm_g, mla_w_q_b=v_mla_w_q_b, mla_kv_a_norm_g=v_mla_kv_a_norm_g,
             mla_w_kv_b=v_mla_w_kv_b, mla_q_norm_g=v_mla_q_norm_g, mla_k_norm_g=v_mla_k_norm_g, dn_conv_w=v_dn_conv_w,
             dn_a_log=v_dn_a_log, dn_dt_bias=v_dn_dt_bias, dn_out_norm_g=v_dn_out_norm_g, dil_q_norm_g=v_dil_q_norm_g,
             dil_k_norm_g=v_dil_k_norm_g, w_branch=v_w_branch, w_out=v_w_out)
    x2, target = x[0], loss_target[0]
    pos = positions[0][:, None]

    names = [n for n, _ in SHARDED]
    view = lambda t, n, s: t[n].reshape((DEPTH,) + s)
    shards = [[(view(w, n, s) if n == "dn_conv_w" else view(w, n, s).astype(BF16))[l] for n, s in SHARDED] for l in range(DEPTH)]
    small = [{n: w[n][l] for n, _ in SMALL} for l in range(DEPTH)]
    gathered0 = dict(zip(names, _all_gather(shards[0])))
    gathering1, pos = _exchange_start("gather_layer1_start", shards[1], True, pos)
    tables = _rope_tables(pos, _rope_consts())

    def layer(l, g, small_l, x_l):
        tag = f"l{l}"
        seg_op = _make_w_in_segments(f"{tag}_w_in_segments")
        w_segs, vjp_segs = jax.vjp(lambda gw: tuple(s[0] for s in seg_op(gw[:, None])), g["w_in"])
        h, vjp_norm = jax.vjp(lambda x_, ng: _layer_norm(tag, x_, ng), x_l, small_l["norm_g"])
        segs = tuple(_mm(f"{tag}_inproj_fwd{i}", h, w_, "nn") for i, w_ in enumerate(w_segs))
        rest_g = {n: a for n, a in g.items() if n != "w_in"}
        rest_s = {n: a for n, a in small_l.items() if n != "norm_g"}
        y, vjp_tail = jax.vjp(lambda sg, x_, gg, ss: _layer_tail(tag, x_, sg, tables, _whole_weights(gg, ss)), segs, x_l, rest_g, rest_s)

        def backward(dy):
            dsegs, dx_skip, d_rest_g, d_rest_s = vjp_tail(dy)
            dws = tuple(_mm(f"{tag}_inproj_dw{i}", h, d, "tn", out_dtype=w_.dtype) for i, (w_, d) in enumerate(zip(w_segs, dsegs)))
            dg = dict(d_rest_g, w_in=vjp_segs(dws)[0])
            exchanging, first = _exchange_start(f"exchange_{tag}_start", [dg[n] for n in names], False, dsegs[0])
            dh = None
            for i, (w_, d) in enumerate(zip(w_segs, (first,) + tuple(dsegs[1:]))):
                dh = _mm(f"{tag}_inproj_dh{i}", d, w_, "nt", acc=dh, out_dtype=h.dtype if i == len(w_segs) - 1 else F32)
            dx_norm, d_norm_g = vjp_norm(dh)
            return exchanging, dx_skip + dx_norm, dict(d_rest_s, norm_g=d_norm_g)
        return y, backward

    y0, backward0 = layer(0, gathered0, small[0], x2)
    gathered1 = dict(zip(names, _exchange_wait("gather_layer1_wait", gathering1, y0)))
    y1, backward1 = layer(1, gathered1, small[1], y0)
    loss_splat, dy = _loss_call(y1, target)
    loss = lax.psum(loss_splat[0, 0], ("x", "y", "c"))
    exchanging1, d_y0, g_small1 = backward1(dy)
    exchanging0, g_x, g_small0 = backward0(d_y0)

    g_small = (g_small0, g_small1)
    sizes = [k for _ in range(DEPTH) for _, k in SMALL]
    g_vec = _pack_vectors([g_small[l][n] for l in range(DEPTH) for n, _ in SMALL])
    (parts_vec,) = _all_to_all([jnp.broadcast_to(g_vec[None], (N_DEV,) + g_vec.shape)])
    parts1 = _exchange_wait("exchange_l1_wait", exchanging1, parts_vec)
    parts0 = _exchange_wait("exchange_l0_wait", exchanging0, parts_vec)

    vec = lambda t: _pack_vectors([t[n][l] for l in range(DEPTH) for n, _ in SMALL])[None]
    outs = {}
    for i, (n, s) in enumerate(SHARDED):
        res = _adamw_call(f"adamw_{n}", [parts0[i], parts1[i]], view(w, n, s), view(m, n, s), view(v, n, s))
        outs[n] = [o.reshape(w[n].shape) for o in res]
    vec_outs = [_unpack_vectors(o, sizes) for o in _adamw_call("adamw_vectors", [parts_vec], vec(w), vec(m), vec(v))]
    for i, (n, _) in enumerate(SMALL):
        outs[n] = [jnp.stack([o[l * len(SMALL) + i] for l in range(DEPTH)]) for o in vec_outs]
    return (loss, g_x[None], *[outs[n][k] for k in range(4) for n in WEIGHTS])
```

```python
import functools
import math

import jax
import jax.numpy as jnp
from jax import lax
from jax.experimental import pallas as pl
from jax.experimental.pallas import tpu as pltpu

F32 = jnp.float32
BF16 = jnp.bfloat16
HI = lax.Precision.HIGHEST
MESH = pl.DeviceIdType.MESH

N_DEV = 8
D_MODEL = 1024
DEPTH = 2
RMS_EPS = 1e-6
ROPE_THETA = 10000.0
LANES = 128
MLA_HEADS = 4
MLA_ROPE = 64
MLA_QK = 192
MLA_Q_RANK = 384
MLA_KV_RANK = 256
DN_HEADS = 4
DN_CHUNK = 64
DN_CONV = 4
DIL_HEADS = 12
DIL_GROUP_HEADS = 4
DIL_DILATIONS = (1, 4, 16)
DIL_BLOCK = 128
BRANCH_W = 512
IN_WIDTH = 11464
NEG = -1e30
VMEM_LIMIT = 56 * 1024 * 1024

ADAM_LR, ADAM_B1, ADAM_B2, ADAM_EPS, ADAM_WD, ADAM_STEP = 0.001, 0.9, 0.999, 1e-08, 0.01, 10

_SEG = {}
_off = 0
for _n, _w in (("q_lat", 384), ("c_kv", 256), ("k_pe", 64), ("z_a", 512), ("dn_qkv", 1536), ("dn_a", 4), ("dn_b", 4),
               ("z_b", 512), ("dil_qkv", 4608), ("z_c", 512), ("gate", 3072)):
    _SEG[_n] = (_off, _w)
    _off += _w
assert _off == IN_WIDTH


def _pcall(body, **kw):
    return pl.pallas_call(body, **kw)


def _params(sem=None):
    return pltpu.CompilerParams(dimension_semantics=sem, vmem_limit_bytes=VMEM_LIMIT)


def _tile(n, target, mult):
    t = (min(n, target) // mult) * mult
    while t >= mult:
        if n % t == 0:
            return t
        t -= mult
    return n


def _mm(name, a, b, mode, out_dtype=F32, acc=None, tm=1024, tn=512, tk=1024):
    if mode == "nn":
        (M, K), (_, N) = a.shape, b.shape
    elif mode == "nt":
        (M, K), (N, _) = a.shape, b.shape
    else:
        (K, M), (_, N) = a.shape, b.shape
    tm, tn, tk = _tile(M, tm, LANES), _tile(N, tn, LANES), _tile(K, tk, LANES)
    nk = K // tk
    dims = {"nn": (((1,), (0,)), ((), ())), "nt": (((1,), (1,)), ((), ())), "tn": (((0,), (0,)), ((), ()))}[mode]
    a_spec = pl.BlockSpec((tk, tm), lambda i, j, k: (k, i)) if mode == "tn" else pl.BlockSpec((tm, tk), lambda i, j, k: (i, k))
    b_spec = pl.BlockSpec((tn, tk), lambda i, j, k: (j, k)) if mode == "nt" else pl.BlockSpec((tk, tn), lambda i, j, k: (k, j))
    o_spec = pl.BlockSpec((tm, tn), lambda i, j, k: (i, j))
    has_acc = acc is not None

    def body(*refs):
        a_ref, b_ref = refs[:2]
        c_ref = refs[2] if has_acc else None
        o_ref = refs[3] if has_acc else refs[2]
        prod = lax.dot_general(a_ref[...].astype(BF16), b_ref[...].astype(BF16), dims, preferred_element_type=F32)
        if nk == 1:
            o_ref[...] = (prod + c_ref[...].astype(F32) if has_acc else prod).astype(out_dtype)
            return
        acc_ref = refs[-1]
        k = pl.program_id(2)

        @pl.when(k == 0)
        def _():
            acc_ref[...] = prod + c_ref[...].astype(F32) if has_acc else prod

        @pl.when(k > 0)
        def _():
            acc_ref[...] += prod

        @pl.when(k == nk - 1)
        def _():
            o_ref[...] = acc_ref[...].astype(out_dtype)

    ins = [a, b] + ([acc] if has_acc else [])
    in_specs = [a_spec, b_spec] + ([o_spec] if has_acc else [])
    return _pcall(body, name=name, grid=(M // tm, N // tn, nk), in_specs=in_specs, out_specs=o_spec,
                  out_shape=jax.ShapeDtypeStruct((M, N), out_dtype), scratch_shapes=[pltpu.VMEM((tm, tn), F32)] if nk > 1 else [],
                  compiler_params=_params(("parallel", "parallel", "arbitrary")))(*ins)


def _make_multi_linear(name, n):
    @jax.custom_vjp
    def op(h, ws):
        return tuple(_mm(f"{name}_fwd{i}", h, w, "nn") for i, w in enumerate(ws))

    def fwd(h, ws):
        return op(h, ws), (h, ws)

    def bwd(res, douts):
        h, ws = res
        dh = None
        for i, (w, d) in enumerate(zip(ws, douts)):
            dh = _mm(f"{name}_dh{i}", d, w, "nt", acc=dh, out_dtype=h.dtype if i == len(ws) - 1 else F32)
        dws = tuple(_mm(f"{name}_dw{i}", h, d, "tn", out_dtype=w.dtype) for i, (w, d) in enumerate(zip(ws, douts)))
        return dh, dws

    op.defvjp(fwd, bwd)
    return op


def _make_resid_linear(name):
    @jax.custom_vjp
    def op(x, a, w):
        return _mm(f"{name}_fwd", a, w, "nn", acc=x)

    def fwd(x, a, w):
        return op(x, a, w), (a, w)

    def bwd(res, dy):
        a, w = res
        return dy, _mm(f"{name}_da", dy, w, "nt", out_dtype=a.dtype), _mm(f"{name}_dw", a, dy, "tn", out_dtype=w.dtype)

    op.defvjp(fwd, bwd)
    return op


def _make_rowwise(name, f, tile):
    def specs(rows, aux, params, consts, t):
        row = [pl.BlockSpec((t, a.shape[1]), lambda i: (i, 0)) for a in (*rows, *aux)]
        full = [pl.BlockSpec(p.shape, lambda i: (0, 0)) for p in (*params, *consts)]
        return row, full

    def fwd_call(rows, aux, params, consts):
        S = rows[0].shape[0]
        t = min(tile, S)
        n_in = len(rows) + len(aux) + len(params) + len(consts)
        shp = lambda a: jax.ShapeDtypeStruct((t, a.shape[1]), a.dtype)
        outs = jax.eval_shape(f, *[shp(a) for a in (*rows, *aux)], *params, *consts)
        row_specs, full_specs = specs(rows, aux, params, consts, t)

        def body(*refs):
            res = f(*[r[...] for r in refs[:n_in]])
            for o_ref, o in zip(refs[n_in:], res):
                o_ref[...] = o

        return _pcall(body, name=f"{name}_fwd", grid=(S // t,), in_specs=row_specs + full_specs,
                      out_specs=[pl.BlockSpec((t, o.shape[1]), lambda i: (i, 0)) for o in outs],
                      out_shape=[jax.ShapeDtypeStruct((S, o.shape[1]), o.dtype) for o in outs],
                      compiler_params=_params(("parallel",)))(*rows, *aux, *params, *consts)

    def bwd_call(rows, aux, params, consts, douts):
        S = rows[0].shape[0]
        t = min(tile, S)
        nr, na, npar, nc, nd = len(rows), len(aux), len(params), len(consts), len(douts)
        row_specs, full_specs = specs(rows, aux, params, consts, t)

        def body(*refs):
            vals = [r[...] for r in refs[:nr + na + npar + nc]]
            rv, av = vals[:nr], vals[nr:nr + na]
            pv, cv = vals[nr + na:nr + na + npar], vals[nr + na + npar:]
            dv = tuple(r[...] for r in refs[nr + na + npar + nc:nr + na + npar + nc + nd])
            out_refs = refs[nr + na + npar + nc + nd:]
            _, vjp = jax.vjp(lambda *rp: f(*rp[:nr], *av, *rp[nr:], *cv), *rv, *pv)
            grads = vjp(dv)
            for o_ref, g in zip(out_refs[:nr], grads[:nr]):
                o_ref[...] = g
            first = pl.program_id(0) == 0
            for o_ref, g in zip(out_refs[nr:], grads[nr:]):
                @pl.when(first)
                def _(o_ref=o_ref):
                    o_ref[...] = jnp.zeros_like(o_ref)
                o_ref[...] += g

        res = _pcall(body, name=f"{name}_bwd", grid=(S // t,),
                     in_specs=row_specs + full_specs + [pl.BlockSpec((t, d.shape[1]), lambda i: (i, 0)) for d in douts],
                     out_specs=[pl.BlockSpec((t, a.shape[1]), lambda i: (i, 0)) for a in rows]
                     + [pl.BlockSpec(p.shape, lambda i: (0, 0)) for p in params],
                     out_shape=[jax.ShapeDtypeStruct(a.shape, a.dtype) for a in (*rows, *params)],
                     compiler_params=_params(("arbitrary",)))(*rows, *aux, *params, *consts, *douts)
        return tuple(res[:nr]), tuple(res[nr:])

    @jax.custom_vjp
    def op(rows, aux, params, consts):
        return tuple(fwd_call(rows, aux, params, consts))

    def fwd(rows, aux, params, consts):
        return op(rows, aux, params, consts), (rows, aux, params, consts)

    def bwd(res, douts):
        rows, aux, params, consts = res
        drows, dparams = bwd_call(rows, aux, params, consts, tuple(douts))
        zeros = lambda xs: tuple(jnp.zeros_like(a) for a in xs)
        return drows, zeros(aux), dparams, zeros(consts)

    op.defvjp(fwd, bwd)
    return op


@jax.custom_vjp
def _swap_halves(x):
    return pltpu.roll(x, LANES // 2, 1)


_swap_halves.defvjp(lambda x: (_swap_halves(x), None), lambda _, g: (_swap_halves(g),))


def _rope(x, cos_t, sin_t):
    return x * cos_t + _swap_halves(x) * sin_t


def _rms(x, g, n=None):
    n = x.shape[-1] if n is None else n
    return x * lax.rsqrt(jnp.sum(x * x, axis=-1, keepdims=True) * (1.0 / n) + RMS_EPS) * g


def _heads(x):
    return [x[:, i * LANES:(i + 1) * LANES] for i in range(x.shape[1] // LANES)]


def _cat(xs):
    return jnp.concatenate(xs, axis=1)


def _silu(x):
    return x * jax.nn.sigmoid(x)


def _f_norm(x, g):
    return (_rms(x, g).astype(BF16),)


def _f_mla_a(q_lat, c_kv, kpe, cos_p, sin_p, qa_g, kva_g, kpe_g):
    kp = _rope(_rms(kpe, kpe_g, MLA_ROPE), cos_p, sin_p)
    return _rms(q_lat, qa_g).astype(BF16), _rms(c_kv, kva_g).astype(BF16), _cat([kp] * MLA_HEADS)


def _f_mla_b(q8, kn_raw, cos_p, sin_p, qn_g, qp_g, kn_g):
    hs = _heads(q8)
    qn = _cat([_rms(h, qn_g) for h in hs[:MLA_HEADS]])
    qp = _cat([_rope(_rms(h, qp_g, MLA_ROPE), cos_p, sin_p) for h in hs[MLA_HEADS:]])
    kn = _cat([_rms(h, kn_g) for h in _heads(kn_raw)])
    return qn, qp, kn


def _softplus(x):
    return jnp.maximum(x, 0.0) + jnp.log(1.0 + jnp.exp(-jnp.abs(x)))


def _l2n(x):
    return x * lax.rsqrt(jnp.sum(x * x, axis=-1, keepdims=True) + 1e-6)


def _f_dn_pre(mixed, ab, alog_f, dtb_f, e_a, e_b):
    hs = _heads(mixed)
    q = _cat([_l2n(h) * (LANES ** -0.5) for h in hs[:DN_HEADS]])
    k = _cat([_l2n(h) for h in hs[DN_HEADS:2 * DN_HEADS]])
    v = _cat(hs[2 * DN_HEADS:])
    a_f = jnp.dot(ab, e_a, precision=HI, preferred_element_type=F32)
    b_f = jnp.dot(ab, e_b, precision=HI, preferred_element_type=F32)
    g = -jnp.exp(alog_f) * _softplus(a_f + dtb_f)
    return q, k, v, g, jax.nn.sigmoid(b_f)


def _f_dil_pre(qkv, cos_h, sin_h, q_g, k_g):
    hs = _heads(qkv)
    q = [_rope(_rms(h, q_g), cos_h, sin_h) for h in hs[:DIL_HEADS]]
    k = [_rope(_rms(h, k_g), cos_h, sin_h) for h in hs[DIL_HEADS:2 * DIL_HEADS]]
    v = hs[2 * DIL_HEADS:]
    group = lambda xs, g: _cat(xs[g * DIL_GROUP_HEADS:(g + 1) * DIL_GROUP_HEADS])
    return tuple(group(xs, g) for xs in (q, k, v) for g in range(len(DIL_DILATIONS)))


def _f_merge_a(y_a, z_a, o_dn, z_b, o0, o1, o2, l0, l1, l2, z_c, out_g):
    y_b = _cat([_rms(h, out_g) for h in _heads(o_dn)])
    os_, ls = [_heads(o) for o in (o0, o1, o2)], [_heads(l) for l in (l0, l1, l2)]
    y_c = []
    for j in range(DIL_GROUP_HEADS):
        l3 = [ls[g][j] for g in range(3)]
        m = jnp.maximum(jnp.maximum(l3[0], l3[1]), l3[2])
        e3 = [jnp.exp(l - m) for l in l3]
        den = e3[0] + e3[1] + e3[2]
        y_c.append(sum(e3[g] * os_[g][j] for g in range(3)) / den)
    return tuple(t.astype(BF16) for t in (y_a * _silu(z_a), y_b * _silu(z_b), _cat(y_c) * _silu(z_c)))


def _f_merge_b(b0, b1, b2, gl):
    gs = [jax.nn.sigmoid(gl[:, i * D_MODEL:(i + 1) * D_MODEL]) for i in range(3)]
    return ((gs[0] * b0 + gs[1] * b1 + gs[2] * b2).astype(BF16),)


def _rope_tables(pos, inv_sign):
    S = pos.shape[0]
    t = min(S, 1024)

    def body(p_ref, c_ref, cp, sp, ch, sh):
        p = p_ref[...].astype(F32)
        c = c_ref[...]
        ang_p, ang_h = p * c[0:1], p * c[2:3]
        cp[...] = jnp.cos(ang_p) * jnp.abs(c[1:2])
        sp[...] = jnp.sin(ang_p) * c[1:2]
        ch[...] = jnp.cos(ang_h)
        sh[...] = jnp.sin(ang_h) * c[3:4]

    row = pl.BlockSpec((t, LANES), lambda i: (i, 0))
    return _pcall(body, name="rope_tables", grid=(S // t,),
                  in_specs=[pl.BlockSpec((t, 1), lambda i: (i, 0)), pl.BlockSpec((4, LANES), lambda i: (0, 0))],
                  out_specs=[row] * 4, out_shape=[jax.ShapeDtypeStruct((S, LANES), F32)] * 4,
                  compiler_params=_params(("parallel",)))(pos, inv_sign)


def _rope_consts():
    half_p, half_h = MLA_ROPE // 2, LANES // 2
    inv_p = 1.0 / (ROPE_THETA ** (jnp.arange(0, MLA_ROPE, 2, dtype=F32) / MLA_ROPE))
    inv_h = 1.0 / (ROPE_THETA ** (jnp.arange(0, LANES, 2, dtype=F32) / LANES))
    z = jnp.zeros((half_p,), F32)
    o = jnp.ones((half_p,), F32)
    return jnp.stack([jnp.concatenate([inv_p, z, inv_p, z]), jnp.concatenate([-o, z, o, z]),
                      jnp.concatenate([inv_h, inv_h]), jnp.concatenate([-jnp.ones((half_h,), F32), jnp.ones((half_h,), F32)])])


def _shift_rows(x, s, up):
    n = x.shape[0]
    r = lax.broadcasted_iota(jnp.int32, x.shape, 0)
    if up:
        return jnp.where(r < n - s, pltpu.roll(x, n - s, 0), 0.0)
    return jnp.where(r >= s, pltpu.roll(x, s, 0), 0.0)


def _make_shift(s):
    @jax.custom_vjp
    def sh(x):
        return _shift_rows(x, s, False)

    sh.defvjp(lambda x: (sh(x), None), lambda _, g: (_shift_rows(g, s, True),))
    return sh


def _f_conv(x, w):
    y = x * w[DN_CONV - 1:DN_CONV]
    for j in range(DN_CONV - 1):
        y = y + _make_shift(DN_CONV - 1 - j)(x) * w[j:j + 1]
    return _silu(y)


def _make_conv(name):
    def call(x, w, dy=None):
        S, C = x.shape
        col = pl.BlockSpec((S, LANES), lambda i: (0, i))
        wsp = pl.BlockSpec((DN_CONV, LANES), lambda i: (0, i))
        if dy is None:
            def body(x_ref, w_ref, o_ref):
                o_ref[...] = _f_conv(x_ref[...], w_ref[...])
            return _pcall(body, name=f"{name}_fwd", grid=(C // LANES,), in_specs=[col, wsp], out_specs=col,
                          out_shape=jax.ShapeDtypeStruct(x.shape, F32), compiler_params=_params(("parallel",)))(x, w)

        def body(x_ref, w_ref, dy_ref, dx_ref, dw_ref):
            _, vjp = jax.vjp(_f_conv, x_ref[...], w_ref[...])
            dx_ref[...], dw_ref[...] = vjp(dy_ref[...])
        return _pcall(body, name=f"{name}_bwd", grid=(C // LANES,), in_specs=[col, wsp, col], out_specs=[col, wsp],
                      out_shape=[jax.ShapeDtypeStruct(x.shape, F32), jax.ShapeDtypeStruct(w.shape, F32)],
                      compiler_params=_params(("parallel",)))(x, w, dy)

    @jax.custom_vjp
    def op(x, w):
        return call(x, w)

    op.defvjp(lambda x, w: (op(x, w), (x, w)), lambda res, dy: tuple(call(*res, dy)))
    return op


def _dot_nt(a, b):
    return lax.dot_general(a.astype(BF16), b.astype(BF16), (((1,), (1,)), ((), ())), preferred_element_type=F32)


def _dot_nn(a, b):
    return jnp.dot(a.astype(BF16), b.astype(BF16), preferred_element_type=F32)


def _dot_tn(a, b):
    return lax.dot_general(a.astype(BF16), b.astype(BF16), (((0,), (0,)), ((), ())), preferred_element_type=F32)


def _mla_scores(qn_r, qp_r, kn_r, kp_r, diagonal):
    scale = MLA_QK ** -0.5
    s = _dot_nt(qn_r[...] * scale, kn_r[...]) + _dot_nt(qp_r[...] * scale, kp_r[...])
    if diagonal:
        r = lax.broadcasted_iota(jnp.int32, s.shape, 0)
        c = lax.broadcasted_iota(jnp.int32, s.shape, 1)
        s = jnp.where(c <= r, s, NEG)
    return s


def _on_causal_pairs(qi, ki, step):
    @pl.when(ki < qi)
    def _():
        step(False)

    @pl.when(ki == qi)
    def _():
        step(True)


def _causal_pairs(n, t, by_key):
    pairs = [(q, k) for k in range(n) for q in range(k, n)] if by_key else [(q, k) for q in range(n) for k in range(q + 1)]
    qt, kt = (jnp.array([p[i] for p in pairs], jnp.int32) for i in (0, 1))
    return (qt, kt, pl.BlockSpec((t, LANES), lambda h, p, qt_r, kt_r: (qt_r[p], h)),
            pl.BlockSpec((t, LANES), lambda h, p, qt_r, kt_r: (kt_r[p], h)))


def _make_mla_attn(name):
    scale = MLA_QK ** -0.5

    def fwd_call(qn, qp, kn, kp, v):
        S = qn.shape[0]
        t = min(S, 512)
        n = S // t
        qt, kt, qs, ks = _causal_pairs(n, t, by_key=False)

        def body(qt_r, kt_r, qn_r, qp_r, kn_r, kp_r, v_r, o_r, lse_r, m_s, l_s, acc_s):
            qi, ki = qt_r[pl.program_id(1)], kt_r[pl.program_id(1)]

            @pl.when(ki == 0)
            def _():
                m_s[...] = jnp.full_like(m_s, NEG)
                l_s[...] = jnp.zeros_like(l_s)
                acc_s[...] = jnp.zeros_like(acc_s)

            def step(diagonal):
                s = _mla_scores(qn_r, qp_r, kn_r, kp_r, diagonal)
                m_old = m_s[...]
                m_new = jnp.maximum(m_old, jnp.max(s, axis=-1, keepdims=True))
                p = jnp.exp(s - m_new[:, :1])
                alpha = jnp.exp(m_old - m_new)
                l_s[...] = alpha * l_s[...] + jnp.sum(p, axis=-1, keepdims=True)
                acc_s[...] = alpha * acc_s[...] + _dot_nn(p, v_r[...])
                m_s[...] = m_new
            _on_causal_pairs(qi, ki, step)

            @pl.when(ki == qi)
            def _():
                o_r[...] = acc_s[...] / l_s[...]
                lse_r[...] = m_s[...] + jnp.log(l_s[...])

        spec = pltpu.PrefetchScalarGridSpec(num_scalar_prefetch=2, grid=(MLA_HEADS, qt.shape[0]), in_specs=[qs, qs, ks, ks, ks],
                                            out_specs=[qs, qs], scratch_shapes=[pltpu.VMEM((t, LANES), F32)] * 3)
        return _pcall(body, name=f"{name}_fwd", grid_spec=spec, out_shape=[jax.ShapeDtypeStruct((S, MLA_HEADS * LANES), F32)] * 2,
                      compiler_params=_params(("parallel", "arbitrary")))(qt, kt, qn, qp, kn, kp, v)

    def dq_call(qn, qp, kn, kp, v, o, lse, do):
        S = qn.shape[0]
        t = min(S, 512)
        n = S // t
        qt, kt, qs, ks = _causal_pairs(n, t, by_key=False)

        def body(qt_r, kt_r, qn_r, qp_r, kn_r, kp_r, v_r, o_r, lse_r, do_r, dqn_r, dqp_r, dl_r, dqn_s, dqp_s):
            qi, ki = qt_r[pl.program_id(1)], kt_r[pl.program_id(1)]

            @pl.when(ki == 0)
            def _():
                dqn_s[...] = jnp.zeros_like(dqn_s)
                dqp_s[...] = jnp.zeros_like(dqp_s)
                dl_r[...] = jnp.broadcast_to(jnp.sum(do_r[...] * o_r[...], axis=-1, keepdims=True), dl_r.shape)

            def step(diagonal):
                p = jnp.exp(_mla_scores(qn_r, qp_r, kn_r, kp_r, diagonal) - lse_r[...][:, :1])
                ds = p * (_dot_nt(do_r[...], v_r[...]) - dl_r[...][:, :1])
                dqn_s[...] += _dot_nn(ds, kn_r[...])
                dqp_s[...] += _dot_nn(ds, kp_r[...])
            _on_causal_pairs(qi, ki, step)

            @pl.when(ki == qi)
            def _():
                dqn_r[...] = dqn_s[...] * scale
                dqp_r[...] = dqp_s[...] * scale

        spec = pltpu.PrefetchScalarGridSpec(num_scalar_prefetch=2, grid=(MLA_HEADS, qt.shape[0]), in_specs=[qs, qs, ks, ks, ks, qs, qs, qs],
                                            out_specs=[qs, qs, qs], scratch_shapes=[pltpu.VMEM((t, LANES), F32)] * 2)
        return _pcall(body, name=f"{name}_dq", grid_spec=spec, out_shape=[jax.ShapeDtypeStruct((S, MLA_HEADS * LANES), F32)] * 3,
                      compiler_params=_params(("parallel", "arbitrary")))(qt, kt, qn, qp, kn, kp, v, o, lse, do)

    def dkv_call(qn, qp, kn, kp, v, lse, do, dl):
        S = qn.shape[0]
        t = min(S, 512)
        n = S // t
        qt, kt, qs, ks = _causal_pairs(n, t, by_key=True)

        def body(qt_r, kt_r, qn_r, qp_r, kn_r, kp_r, v_r, lse_r, do_r, dl_r, dkn_r, dkp_r, dv_r, dkn_s, dkp_s, dv_s):
            qi, ki = qt_r[pl.program_id(1)], kt_r[pl.program_id(1)]

            @pl.when(qi == ki)
            def _():
                dkn_s[...] = jnp.zeros_like(dkn_s)
                dkp_s[...] = jnp.zeros_like(dkp_s)
                dv_s[...] = jnp.zeros_like(dv_s)

            def step(diagonal):
                p = jnp.exp(_mla_scores(qn_r, qp_r, kn_r, kp_r, diagonal) - lse_r[...][:, :1])
                ds = p * (_dot_nt(do_r[...], v_r[...]) - dl_r[...][:, :1])
                dv_s[...] += _dot_tn(p, do_r[...])
                dkn_s[...] += _dot_tn(ds, qn_r[...] * scale)
                dkp_s[...] += _dot_tn(ds, qp_r[...] * scale)
            _on_causal_pairs(qi, ki, step)

            @pl.when(qi == n - 1)
            def _():
                dkn_r[...] = dkn_s[...]
                dkp_r[...] = dkp_s[...]
                dv_r[...] = dv_s[...]

        spec = pltpu.PrefetchScalarGridSpec(num_scalar_prefetch=2, grid=(MLA_HEADS, qt.shape[0]), in_specs=[qs, qs, ks, ks, ks, qs, qs, qs],
                                            out_specs=[ks, ks, ks], scratch_shapes=[pltpu.VMEM((t, LANES), F32)] * 3)
        return _pcall(body, name=f"{name}_dkv", grid_spec=spec, out_shape=[jax.ShapeDtypeStruct((S, MLA_HEADS * LANES), F32)] * 3,
                      compiler_params=_params(("parallel", "arbitrary")))(qt, kt, qn, qp, kn, kp, v, lse, do, dl)

    @jax.custom_vjp
    def op(qn, qp, kn, kp, v):
        return fwd_call(qn, qp, kn, kp, v)[0]

    def fwd(qn, qp, kn, kp, v):
        o, lse = fwd_call(qn, qp, kn, kp, v)
        return o, (qn, qp, kn, kp, v, o, lse)

    def bwd(res, do):
        qn, qp, kn, kp, v, o, lse = res
        dqn, dqp, dl = dq_call(qn, qp, kn, kp, v, o, lse, do)
        dkn, dkp, dv = dkv_call(qn, qp, kn, kp, v, lse, do, dl)
        return dqn, dqp, dkn, dkp, dv

    op.defvjp(fwd, bwd)
    return op


def _dil_block(q, kp, kc, vp, vc, has_prev):
    scale = LANES ** -0.5
    r = lax.broadcasted_iota(jnp.int32, (DIL_BLOCK, 2 * DIL_BLOCK), 0)
    c = lax.broadcasted_iota(jnp.int32, (DIL_BLOCK, 2 * DIL_BLOCK), 1)
    valid = ((c < DIL_BLOCK) & (c >= r) & has_prev) | ((c >= DIL_BLOCK) & (c - DIL_BLOCK <= r))
    s = jnp.where(valid, _dot_nt(q * scale, jnp.concatenate([kp, kc], axis=0)), NEG)
    m = jnp.max(s, axis=-1, keepdims=True)
    e = jnp.exp(s - m)
    den = jnp.sum(e, axis=-1, keepdims=True)
    o = _dot_nn(e, jnp.concatenate([vp, vc], axis=0)) / den
    return o, jnp.broadcast_to(m + jnp.log(den), o.shape)


DIL_TILE_ROWS = (1024, 1024, 2048)


def _make_dil_attn(name, d, tile_rows):
    def call(q, k, v, cts=None):
        S = q.shape[0]
        span = DIL_BLOCK * d
        G = max(1, min(tile_rows, S) // span)
        n = S // (G * span)
        at = (lambda i: i) if cts is None else (lambda i: n - 1 - i)
        tile = pl.BlockSpec((G * span, LANES), lambda h, i: (at(i), h))
        before = pl.BlockSpec((span, LANES), lambda h, i: (jnp.maximum(at(i) * G - 1, 0), h))

        def rows(r, j):
            return pl.ds(j * DIL_BLOCK, DIL_BLOCK) if d == 1 else pl.ds(r + j * span, DIL_BLOCK, stride=d)

        def over_residues(fn):
            if d == 1:
                fn(0)
            else:
                lax.fori_loop(0, d, lambda r, c: (fn(r), c)[1], 0)

        def block_inputs(r, j, q_r, kb_r, k_r, vb_r, v_r):
            kp = kb_r[rows(r, 0), :] if j == 0 else k_r[rows(r, j - 1), :]
            vp = vb_r[rows(r, 0), :] if j == 0 else v_r[rows(r, j - 1), :]
            return q_r[rows(r, j), :], kp, k_r[rows(r, j), :], vp, v_r[rows(r, j), :]

        if cts is None:
            def body(q_r, kb_r, k_r, vb_r, v_r, o_r, lse_r):
                first = at(pl.program_id(1)) * G

                def residue(r):
                    for j in range(G):
                        o_r[rows(r, j), :], lse_r[rows(r, j), :] = _dil_block(
                            *block_inputs(r, j, q_r, kb_r, k_r, vb_r, v_r), first + j > 0)
                over_residues(residue)
            return _pcall(body, name=f"{name}_fwd", grid=(DIL_GROUP_HEADS, n), in_specs=[tile, before, tile, before, tile],
                          out_specs=[tile, tile], out_shape=[jax.ShapeDtypeStruct(q.shape, F32)] * 2,
                          compiler_params=_params(("parallel", "parallel")))(q, k, k, v, v)

        def body(q_r, kb_r, k_r, vb_r, v_r, do_r, dl_r, dq_r, dk_r, dv_r, ck_s, cv_s):
            first = at(pl.program_id(1)) * G

            @pl.when(pl.program_id(1) == 0)
            def _():
                ck_s[...] = jnp.zeros_like(ck_s)
                cv_s[...] = jnp.zeros_like(cv_s)

            def residue(r):
                owed = None
                for j in range(G):
                    hp = first + j > 0
                    _, vjp = jax.vjp(lambda *a: _dil_block(*a, hp), *block_inputs(r, j, q_r, kb_r, k_r, vb_r, v_r))
                    dq, dkp, dkc, dvp, dvc = vjp((do_r[rows(r, j), :], dl_r[rows(r, j), :]))
                    dq_r[rows(r, j), :] = dq
                    if j == G - 1:
                        dkc, dvc = dkc + ck_s[rows(r, 0), :], dvc + cv_s[rows(r, 0), :]
                    dk_r[rows(r, j), :], dv_r[rows(r, j), :] = dkc, dvc
                    if j == 0:
                        owed = (dkp, dvp)
                    else:
                        dk_r[rows(r, j - 1), :] += dkp
                        dv_r[rows(r, j - 1), :] += dvp
                ck_s[rows(r, 0), :], cv_s[rows(r, 0), :] = owed
            over_residues(residue)
        return _pcall(body, name=f"{name}_bwd", grid=(DIL_GROUP_HEADS, n), in_specs=[tile, before, tile, before, tile, tile, tile],
                      out_specs=[tile] * 3, out_shape=[jax.ShapeDtypeStruct(q.shape, F32)] * 3,
                      scratch_shapes=[pltpu.VMEM((span, LANES), F32)] * 2,
                      compiler_params=_params(("parallel", "arbitrary")))(q, k, k, v, v, *cts)

    @jax.custom_vjp
    def op(q, k, v):
        return tuple(call(q, k, v))

    op.defvjp(lambda q, k, v: (op(q, k, v), (q, k, v)), lambda res, cts: tuple(call(*res, cts=cts)))
    return op


def _pdot(a, b, dims):
    return lax.dot_general(a, b, (dims, ((), ())), precision=lax.Precision.HIGH, preferred_element_type=F32)


DN_LOCAL_CHUNKS = 4


DN_BLOCK_HEADS = 4
DN_BLOCK = DN_BLOCK_HEADS * DN_CHUNK


def _dn_local(q, k, v, g, b):
    C, R = DN_CHUNK, DN_BLOCK
    r = lax.broadcasted_iota(jnp.int32, (R, R), 0)
    c = lax.broadcasted_iota(jnp.int32, (R, R), 1)
    same_head = (r // C) == (c // C)
    incl, strict = same_head & (r >= c), same_head & (r > c)
    eye = (r == c).astype(F32)
    avg = jnp.full((R, LANES), 1.0 / LANES, F32)
    rc = lax.broadcasted_iota(jnp.int32, (C, C), 0) >= lax.broadcasted_iota(jnp.int32, (C, C), 1)
    gc_lanes = _pdot(rc.astype(F32), g, ((1,), (0,)))
    us, ws, qes, kds, qks = [], [], [], [], []
    for first in range(0, DN_HEADS, DN_BLOCK_HEADS):
        stack = lambda x: jnp.concatenate(_heads(x)[first:first + DN_BLOCK_HEADS], axis=0)
        unstack = lambda x: [x[p * C:(p + 1) * C] for p in range(DN_BLOCK_HEADS)]
        gc, q_s, k_s, v_s, b_s = (stack(x) for x in (gc_lanes, q, k, v, b))
        gc_j = _pdot(avg, gc, ((1,), (1,)))
        decay = jnp.exp(jnp.where(incl, _cat([gc] * (R // LANES)) - gc_j, NEG))
        kb = k_s * b_s
        kk = _pdot(jnp.concatenate([kb, q_s], axis=0), k_s, ((1,), (1,)))
        a = jnp.where(strict, kk[:R] * decay, 0.0)
        inv, pw = eye - a, a
        for _ in range(5):
            pw = _pdot(pw, pw, ((1,), (0,)))
            inv = inv + _pdot(inv, pw, ((1,), (0,)))
        eg = jnp.exp(gc)
        uw = _pdot(inv, _cat([v_s * b_s, kb * eg]), ((1,), (0,)))
        g_last = jnp.concatenate([jnp.broadcast_to(x[C - 1:C], (C, LANES)) for x in unstack(gc)], axis=0)
        us += unstack(uw[:, :LANES])
        ws += unstack(uw[:, LANES:])
        qes += unstack(q_s * eg)
        kds += unstack(k_s * jnp.exp(g_last - gc))
        qks.append(kk[R:] * decay)
    egl = jnp.broadcast_to(jnp.exp(gc_lanes[C - 1:C]), (8, DN_HEADS * LANES))
    return _cat(us), _cat(ws), _cat(qes), _cat(kds), jnp.concatenate(qks, axis=0), egl


def _dn_scan(u, w, qe, kd, qk, egl, state):
    C = DN_CHUNK
    heads = [slice(h * LANES, (h + 1) * LANES) for h in range(DN_HEADS)]
    ws = [_pdot(jnp.concatenate([w[:, sl], qe[:, sl]], axis=0), state[sl, :], ((1,), (0,))) for sl in heads]
    v_new = [u[:, sl] - x[:C] for sl, x in zip(heads, ws)]
    local = []
    for i, first in enumerate(range(0, DN_HEADS, DN_BLOCK_HEADS)):
        y = _pdot(qk[i * DN_BLOCK:(i + 1) * DN_BLOCK], jnp.concatenate(v_new[first:first + DN_BLOCK_HEADS], axis=0), ((1,), (0,)))
        local += [y[p * C:(p + 1) * C] for p in range(DN_BLOCK_HEADS)]
    o = _cat([x[C:] + y for x, y in zip(ws, local)])
    states = [state[sl, :] * egl[0:1, sl] + _pdot(kd[:, sl], vn, ((0,), (0,))) for sl, vn in zip(heads, v_new)]
    return o, jnp.concatenate(states, axis=0)


def _make_delta_rule(name):
    W = DN_HEADS * LANES
    QK = DN_HEADS * DN_CHUNK

    def local_call(ins, cts=None):
        S = ins[0].shape[0]
        n = S // DN_CHUNK
        per = math.gcd(DN_LOCAL_CHUNKS, n)
        row = pl.BlockSpec((per * DN_CHUNK, W), lambda i: (i, 0))
        qkb = pl.BlockSpec((per * QK, DN_BLOCK), lambda i: (i, 0))
        eg = pl.BlockSpec((per, 8, W), lambda i: (i, 0, 0))
        rows = lambda j: slice(j * DN_CHUNK, (j + 1) * DN_CHUNK)
        qk_rows = lambda j: slice(j * QK, (j + 1) * QK)
        out_rows = [rows, rows, rows, rows, qk_rows]

        if cts is None:
            def body(*refs):
                for j in range(per):
                    res = _dn_local(*[r[rows(j), :] for r in refs[:5]])
                    for o_r, o, at_ in zip(refs[5:10], res[:5], out_rows):
                        o_r[at_(j), :] = o
                    refs[10][j] = res[5]
            return _pcall(body, name=f"{name}_local_fwd", grid=(n // per,), in_specs=[row] * 5, out_specs=[row] * 4 + [qkb, eg],
                          out_shape=[jax.ShapeDtypeStruct((S, W), F32)] * 4
                          + [jax.ShapeDtypeStruct((n * QK, DN_BLOCK), F32), jax.ShapeDtypeStruct((n, 8, W), F32)],
                          compiler_params=_params(("parallel",)))(*ins)

        def body(*refs):
            for j in range(per):
                _, vjp = jax.vjp(_dn_local, *[r[rows(j), :] for r in refs[:5]])
                grads = vjp(tuple(r[at_(j), :] for r, at_ in zip(refs[5:10], out_rows)) + (refs[10][j],))
                for o_r, o in zip(refs[11:], grads):
                    o_r[rows(j), :] = o
        return _pcall(body, name=f"{name}_local_bwd", grid=(n // per,), in_specs=[row] * 9 + [qkb, eg], out_specs=[row] * 5,
                      out_shape=[jax.ShapeDtypeStruct((S, W), F32)] * 5, compiler_params=_params(("parallel",)))(*ins, *cts)

    def scan_call(ins, saved=None, do=None):
        S = ins[0].shape[0]
        n = S // DN_CHUNK
        at = (lambda i: i) if do is None else (lambda i: n - 1 - i)
        row = pl.BlockSpec((DN_CHUNK, W), lambda i: (at(i), 0))
        qkb = pl.BlockSpec((QK, DN_BLOCK), lambda i: (at(i), 0))
        eg = pl.BlockSpec((None, 8, W), lambda i: (at(i), 0, 0))
        st = pl.BlockSpec((None, W, LANES), lambda i: (at(i), 0, 0))

        if do is None:
            def body(*refs):
                o_r, st_r, s_s = refs[6:]

                @pl.when(pl.program_id(0) == 0)
                def _():
                    s_s[...] = jnp.zeros_like(s_s)
                st_r[...] = s_s[...]
                o_r[...], s_s[...] = _dn_scan(*[r[...] for r in refs[:6]], s_s[...])
            return _pcall(body, name=f"{name}_scan_fwd", grid=(n,), in_specs=[row] * 4 + [qkb, eg], out_specs=[row, st],
                          out_shape=[jax.ShapeDtypeStruct((S, W), F32), jax.ShapeDtypeStruct((n, W, LANES), F32)],
                          scratch_shapes=[pltpu.VMEM((W, LANES), F32)], compiler_params=_params(("arbitrary",)))(*ins)

        def body(*refs):
            st_r, do_r = refs[6:8]
            outs, ds_s = refs[8:14], refs[14]

            @pl.when(pl.program_id(0) == 0)
            def _():
                ds_s[...] = jnp.zeros_like(ds_s)
            _, vjp = jax.vjp(_dn_scan, *[r[...] for r in refs[:6]], st_r[...])
            *grads, ds = vjp((do_r[...], ds_s[...]))
            for o_r, gval in zip(outs, grads):
                o_r[...] = gval
            ds_s[...] = ds
        return _pcall(body, name=f"{name}_scan_bwd", grid=(n,), in_specs=[row] * 4 + [qkb, eg, st, row], out_specs=[row] * 4 + [qkb, eg],
                      out_shape=[jax.ShapeDtypeStruct((S, W), F32)] * 4
                      + [jax.ShapeDtypeStruct((n * QK, DN_BLOCK), F32), jax.ShapeDtypeStruct((n, 8, W), F32)],
                      scratch_shapes=[pltpu.VMEM((W, LANES), F32)], compiler_params=_params(("arbitrary",)))(*ins, saved, do)

    @jax.custom_vjp
    def local(q, k, v, g, b):
        return tuple(local_call((q, k, v, g, b)))

    local.defvjp(lambda *a: (local(*a), a), lambda res, cts: tuple(local_call(res, tuple(cts))))

    @jax.custom_vjp
    def scan(u, w, qe, kd, qk, egl):
        return scan_call((u, w, qe, kd, qk, egl))[0]

    def scan_fwd(*a):
        o, states = scan_call(a)
        return o, (a, states)

    scan.defvjp(scan_fwd, lambda res, do: tuple(scan_call(res[0], res[1], do)))
    return lambda q, k, v, g, b: scan(*local(q, k, v, g, b))


def _loss_call(y, target):
    S, D = y.shape
    t = min(S, 512)
    n = S // t
    row = pl.BlockSpec((t, D), lambda i: (i, 0))

    def body(y_r, t_r, loss_r, dy_r, acc_s):
        i = pl.program_id(0)

        @pl.when(i == 0)
        def _():
            acc_s[...] = jnp.zeros_like(acc_s)
        err = y_r[...] - t_r[...]
        dy_r[...] = err * (1.0 / D)
        acc_s[...] += jnp.sum(err * err, axis=0, keepdims=True)

        @pl.when(i == n - 1)
        def _():
            loss_r[...] = jnp.broadcast_to(jnp.sum(acc_s[...], axis=1, keepdims=True) * (0.5 / D), loss_r.shape)

    return _pcall(body, name="loss_head", grid=(n,), in_specs=[row, row],
                  out_specs=[pl.BlockSpec((8, LANES), lambda i: (0, 0)), row],
                  out_shape=[jax.ShapeDtypeStruct((8, LANES), F32), jax.ShapeDtypeStruct((S, D), F32)],
                  scratch_shapes=[pltpu.VMEM((1, D), F32)], compiler_params=_params(("arbitrary",)))(y, target)


def _adamw_call(name, parts, w, m, v, rows=128):
    L, R, C = w.shape
    assert len(parts) == L
    t = _tile(R, rows, 8)
    row = pl.BlockSpec((None, t, C), lambda l, i: (l, i, 0))
    part = lambda k: pl.BlockSpec((N_DEV, t, C), lambda l, i: (0, jnp.where(l == k, i, 0), 0))

    def body(*refs):
        p_refs, (w_r, m_r, v_r, g_r, d_r, nm_r, nv_r) = refs[:L], refs[L:]

        def update(p_r):
            g = p_r[0].astype(F32)
            for s in range(1, N_DEV):
                g = g + p_r[s].astype(F32)
            m_new = ADAM_B1 * m_r[...] + (1.0 - ADAM_B1) * g
            v_new = ADAM_B2 * v_r[...] + (1.0 - ADAM_B2) * (g * g)
            m_hat = m_new / (1.0 - ADAM_B1 ** ADAM_STEP)
            v_hat = v_new / (1.0 - ADAM_B2 ** ADAM_STEP)
            g_r[...] = g
            d_r[...] = -ADAM_LR * (m_hat / (jnp.sqrt(v_hat) + ADAM_EPS) + ADAM_WD * w_r[...])
            nm_r[...] = m_new
            nv_r[...] = v_new

        for k in range(L):
            pl.when(pl.program_id(0) == k)(functools.partial(update, p_refs[k]))

    return _pcall(body, name=name, grid=(L, R // t), in_specs=[part(k) for k in range(L)] + [row, row, row],
                  out_specs=[row] * 4, out_shape=[jax.ShapeDtypeStruct((L, R, C), F32)] * 4,
                  compiler_params=_params(("arbitrary", "arbitrary")))(*parts, w, m, v)


def _my_place():
    x, y, c = lax.axis_index("x"), lax.axis_index("y"), lax.axis_index("c")
    return x, y, c


def _index(x, y, c):
    return 4 * x + 2 * y + c


def _all_gather(vs):
    n = len(vs)

    def body(*refs):
        v_refs, out_refs = refs[:n], refs[n:2 * n]
        send_sems, recv_sems, local_sems = refs[2 * n:]
        x, y, c = _my_place()
        me, sibling = (x, y, c), (x, y, 1 - c)
        chips = [(1 - x, y), (x, 1 - y), (1 - x, 1 - y)]

        def copy(a, k, block, to, src=None):
            rows = out_refs[a].at[_index(*block)]
            return pltpu.make_async_remote_copy(src_ref=rows if src is None else src, dst_ref=rows, send_sem=send_sems.at[a, k],
                                                recv_sem=recv_sems.at[a, k], device_id=to, device_id_type=MESH)

        mine = [pltpu.make_async_copy(v_refs[a], out_refs[a].at[_index(*me)], local_sems.at[a]) for a in range(n)]
        first, passed = [], []
        for a in range(n):
            mine[a].start()
            first += [copy(a, 0, me, sibling, src=v_refs[a])]
            first += [copy(a, 1 + j, me, (*chip, c), src=v_refs[a]) for j, chip in enumerate(chips)]
        for cp in first:
            cp.start()
        for j, chip in enumerate(chips):
            for a in range(n):
                copy(a, 1 + j, (*chip, c), me).wait_recv()
                passed.append(copy(a, 4 + j, (*chip, c), sibling))
                passed[-1].start()
        for a in range(n):
            copy(a, 0, sibling, me).wait_recv()
            for j, chip in enumerate(chips):
                copy(a, 4 + j, (*chip, 1 - c), me).wait_recv()
        for cp in first + passed:
            cp.wait_send()
        for a in range(n):
            mine[a].wait()

    any_ = pl.BlockSpec(memory_space=pl.ANY)
    return _pcall(body, name="gather_weights", in_specs=[any_] * n, out_specs=[any_] * n,
                  out_shape=[jax.ShapeDtypeStruct((N_DEV,) + v.shape, v.dtype) for v in vs],
                  scratch_shapes=[pltpu.SemaphoreType.DMA((n, 7)), pltpu.SemaphoreType.DMA((n, 7)), pltpu.SemaphoreType.DMA((n,))])(*vs)


def _all_to_all(vs):
    n = len(vs)

    def body(*refs):
        v_refs, out_refs = refs[:n], refs[n:2 * n]
        send_sems, recv_sems, local_sems = refs[2 * n:]
        x, y, c = _my_place()
        me = _index(x, y, c)
        mine = [pltpu.make_async_copy(v_refs[a].at[me], out_refs[a].at[me], local_sems.at[a]) for a in range(n)]
        copies = []
        for a in range(n):
            mine[a].start()
        for k in range(1, N_DEV):
            px = 1 - x if k & 4 else x
            py = 1 - y if k & 2 else y
            pc = 1 - c if k & 1 else c
            for a in range(n):
                cp = pltpu.make_async_remote_copy(src_ref=v_refs[a].at[_index(px, py, pc)], dst_ref=out_refs[a].at[me],
                                                  send_sem=send_sems.at[a, k - 1], recv_sem=recv_sems.at[a, k - 1],
                                                  device_id=(px, py, pc), device_id_type=MESH)
                cp.start()
                copies.append(cp)
        for cp in copies:
            cp.wait()
        for a in range(n):
            mine[a].wait()

    any_ = pl.BlockSpec(memory_space=pl.ANY)
    return _pcall(body, name="exchange_grads", in_specs=[any_] * n, out_specs=[any_] * n,
                  out_shape=[jax.ShapeDtypeStruct(v.shape, v.dtype) for v in vs],
                  scratch_shapes=[pltpu.SemaphoreType.DMA((n, 7)), pltpu.SemaphoreType.DMA((n, 7)), pltpu.SemaphoreType.DMA((n,))])(*vs)


_HBM = pl.BlockSpec(memory_space=pltpu.HBM)
_SEM = pl.BlockSpec(memory_space=pltpu.SEMAPHORE)
_EFFECT = pltpu.SideEffectType.DATAFLOW_SIDE_EFFECTING


def _direct_copies(gather, v_refs, land_refs, send_sems, recv_sems, local_sems):
    x, y, c = _my_place()
    me = _index(x, y, c)
    local, remote = [], []
    for a, (v_ref, land_ref) in enumerate(zip(v_refs, land_refs)):
        local.append(pltpu.make_async_copy(v_ref if gather else v_ref.at[me], land_ref.at[me], local_sems.at[a]))
    for k in range(1, N_DEV):
        px = 1 - x if k & 4 else x
        py = 1 - y if k & 2 else y
        pc = 1 - c if k & 1 else c
        for a, (v_ref, land_ref) in enumerate(zip(v_refs, land_refs)):
            sem = a * (N_DEV - 1) + k - 1
            remote.append(pltpu.make_async_remote_copy(
                src_ref=v_ref if gather else v_ref.at[_index(px, py, pc)], dst_ref=land_ref.at[me], send_sem=send_sems.at[sem],
                recv_sem=recv_sems.at[sem], device_id=(px, py, pc), device_id_type=MESH))
    return local, remote


def _exchange_start(name, vs, gather, thru):
    n = len(vs)
    lands = [lax.empty((N_DEV,) + v.shape if gather else v.shape, v.dtype) for v in vs]

    def body(*refs):
        v_refs, land_refs = refs[:n], refs[n:2 * n]
        send_sems, recv_sems, local_sems = refs[2 * n + 1:2 * n + 4]
        local, remote = _direct_copies(gather, v_refs, land_refs, send_sems, recv_sems, local_sems)
        for cp in local + remote:
            cp.start()

    hbm = lambda a: pltpu.HBM(a.shape, a.dtype)
    res = _pcall(body, name=name,
                 out_shape=(pltpu.SemaphoreType.DMA((n * (N_DEV - 1),)), pltpu.SemaphoreType.DMA((n * (N_DEV - 1),)), pltpu.SemaphoreType.DMA((n,)),
                            *[hbm(a) for a in (*vs, *lands, thru)]),
                 in_specs=[_HBM] * (2 * n + 1), out_specs=(_SEM, _SEM, _SEM, *[_HBM] * (2 * n + 1)),
                 input_output_aliases={i: 3 + i for i in range(2 * n + 1)},
                 compiler_params=pltpu.CompilerParams(has_side_effects=_EFFECT))(
        *[pltpu.with_memory_space_constraint(a, pltpu.HBM) for a in (*vs, *lands, thru)])
    return (gather, res[:3], res[3:3 + n], res[3 + n:3 + 2 * n]), res[3 + 2 * n]


def _exchange_wait(name, started, after):
    gather, sems, vs, lands = started
    n = len(vs)

    def body(*refs):
        v_refs, land_refs = refs[:n], refs[n:2 * n]
        send_sems, recv_sems, local_sems = refs[2 * n:2 * n + 3]
        local, remote = _direct_copies(gather, v_refs, land_refs, send_sems, recv_sems, local_sems)
        for cp in local:
            cp.wait()
        for cp in remote:
            cp.wait_send()
            cp.wait_recv()

    hbm = lambda a: pltpu.HBM(a.shape, a.dtype)
    res = _pcall(body, name=name, out_shape=tuple(hbm(a) for a in (*vs, *lands)),
                 in_specs=[_HBM] * (2 * n) + [_SEM] * 3 + [pl.BlockSpec(memory_space=pl.ANY)], out_specs=tuple([_HBM] * (2 * n)),
                 input_output_aliases={i: i for i in range(2 * n)},
                 compiler_params=pltpu.CompilerParams(has_side_effects=_EFFECT))(*vs, *lands, *sems, after)
    return list(res[n:])


W_IN_SHARD = IN_WIDTH // N_DEV
SEG_ORDER = ("q_lat", "c_kv", "k_pe", "z_a", "dn_qkv", "dn_ab", "z_b", "dil_qkv", "z_c", "gate")
SEG_WIDTH = (384, 256, LANES, 512, 1536, LANES, 512, 4608, 512, 3072)


def _w_in_plan():
    plan = []

    def add(seg, c0, c1, dst):
        while c0 < c1:
            d = c0 // W_IN_SHARD
            e = min(c1, (d + 1) * W_IN_SHARD)
            plan.append((seg, dst, d, c0 - d * W_IN_SHARD, e - c0))
            dst += e - c0
            c0 = e

    half = MLA_ROPE // 2
    for i, name in enumerate(SEG_ORDER):
        if name == "k_pe":
            o = _SEG["k_pe"][0]
            add(i, o, o + half, 0)
            add(i, o + half, o + 2 * half, LANES // 2)
        elif name == "dn_ab":
            o = _SEG["dn_a"][0]
            add(i, o, o + 2 * DN_HEADS, 0)
        else:
            o, w = _SEG[name]
            add(i, o, o + w, 0)
    return plan


def _make_w_in_segments(name):
    plan = _w_in_plan()
    nseg = len(SEG_ORDER)
    t = 256

    def fwd_call(g):
        L = g.shape[1]

        def body(g_ref, *o_refs):
            for i in (SEG_ORDER.index("k_pe"), SEG_ORDER.index("dn_ab")):
                o_refs[i][...] = jnp.zeros_like(o_refs[i])
            for seg, dst, d, src, n in plan:
                o_refs[seg][:, dst:dst + n] = g_ref[d, :, src:src + n]

        return _pcall(body, name=f"{name}_fwd", grid=(L, D_MODEL // t),
                      in_specs=[pl.BlockSpec((N_DEV, None, t, W_IN_SHARD), lambda l, i: (0, l, i, 0))],
                      out_specs=[pl.BlockSpec((None, t, w), lambda l, i: (l, i, 0)) for w in SEG_WIDTH],
                      out_shape=[jax.ShapeDtypeStruct((L, D_MODEL, w), g.dtype) for w in SEG_WIDTH],
                      compiler_params=_params(("parallel", "parallel")))(g)

    def bwd_call(ds):
        L = ds[0].shape[0]

        def body(*refs):
            d_refs, g_ref = refs[:nseg], refs[nseg]
            for seg, dst, d, src, n in plan:
                g_ref[d, :, src:src + n] = d_refs[seg][:, dst:dst + n]

        return _pcall(body, name=f"{name}_bwd", grid=(L, D_MODEL // t),
                      in_specs=[pl.BlockSpec((None, t, w), lambda l, i: (l, i, 0)) for w in SEG_WIDTH],
                      out_specs=pl.BlockSpec((N_DEV, None, t, W_IN_SHARD), lambda l, i: (0, l, i, 0)),
                      out_shape=jax.ShapeDtypeStruct((N_DEV, L, D_MODEL, W_IN_SHARD), ds[0].dtype),
                      compiler_params=_params(("parallel", "parallel")))(*ds)

    @jax.custom_vjp
    def op(g):
        return tuple(fwd_call(g))

    op.defvjp(lambda g: (op(g), None), lambda _, ds: (bwd_call(tuple(ds)),))
    return op


def _pe_pad(a):
    h = MLA_ROPE // 2
    z = jnp.zeros(a.shape[:-1] + (h,), a.dtype)
    return jnp.concatenate([a[..., :h], z, a[..., h:], z], axis=-1)


def _layer_norm(tag, x, norm_g):
    return _make_rowwise(f"{tag}_norm", _f_norm, 512)((x,), (), (norm_g[None, :],), ())[0]


def _layer(tag, x, tables, W):
    h = _layer_norm(tag, x, W["norm_g"])
    return _layer_tail(tag, x, _make_multi_linear(f"{tag}_inproj", 10)(h, W["w_in_segments"]), tables, W)


def _layer_tail(tag, x, segments, tables, W):
    cos_p, sin_p, cos_h, sin_h = tables
    row = lambda a: a[None, :]
    q_lat, c_kv, kpe, z_a, dn_qkv, ab, z_b, dil_qkv, z_c, gl = segments

    qn_lat, ckvn, kp = _make_rowwise(f"{tag}_mla_a", _f_mla_a, 512)(
        (q_lat, c_kv, kpe), (cos_p, sin_p),
        (row(W["mla_q_a_norm_g"]), row(W["mla_kv_a_norm_g"]), row(_pe_pad(W["mla_k_norm_g"][LANES:]))), ())
    wq = W["mla_w_q_b"].reshape(MLA_Q_RANK, MLA_HEADS, MLA_QK)
    wq = jnp.concatenate([wq[:, :, :LANES].reshape(MLA_Q_RANK, -1), _pe_pad(wq[:, :, LANES:]).reshape(MLA_Q_RANK, -1)], axis=1)
    wkv = W["mla_w_kv_b"].reshape(MLA_KV_RANK, MLA_HEADS, 2 * LANES)
    (q8,) = _make_multi_linear(f"{tag}_qb", 1)(qn_lat, (wq,))
    kn_raw, v_mla = _make_multi_linear(f"{tag}_kvb", 2)(
        ckvn, (wkv[:, :, :LANES].reshape(MLA_KV_RANK, -1), wkv[:, :, LANES:].reshape(MLA_KV_RANK, -1)))
    qn, qp, kn = _make_rowwise(f"{tag}_mla_b", _f_mla_b, 512)(
        (q8, kn_raw), (cos_p, sin_p),
        (row(W["mla_q_norm_g"][:LANES]), row(_pe_pad(W["mla_q_norm_g"][LANES:])), row(W["mla_k_norm_g"][:LANES])), ())
    y_a = _make_mla_attn(f"{tag}_mla")(qn, qp, kn, kp, v_mla)

    mixed = _make_conv(f"{tag}_conv")(dn_qkv, W["dn_conv_w"])
    lane_head = jnp.arange(DN_HEADS * LANES) // LANES
    e_a = (jnp.arange(LANES)[:, None] == lane_head[None, :]).astype(F32)
    e_b = (jnp.arange(LANES)[:, None] == lane_head[None, :] + DN_HEADS).astype(F32)
    q_dn, k_dn, v_dn, g_dn, b_dn = _make_rowwise(f"{tag}_dn_pre", _f_dn_pre, 512)(
        (mixed, ab), (), (row(jnp.repeat(W["dn_a_log"], LANES)), row(jnp.repeat(W["dn_dt_bias"], LANES))), (e_a, e_b))
    o_dn = _make_delta_rule(f"{tag}_dn")(q_dn, k_dn, v_dn, g_dn, b_dn)

    qkv_dil = _make_rowwise(f"{tag}_dil_pre", _f_dil_pre, 256)(
        (dil_qkv,), (cos_h, sin_h), (row(W["dil_q_norm_g"]), row(W["dil_k_norm_g"])), ())
    n_groups = len(DIL_DILATIONS)
    o_lse = [_make_dil_attn(f"{tag}_dil{g}", d, DIL_TILE_ROWS[g])(qkv_dil[g], qkv_dil[n_groups + g], qkv_dil[2 * n_groups + g])
             for g, d in enumerate(DIL_DILATIONS)]

    ya, yb, yc = _make_rowwise(f"{tag}_merge_a", _f_merge_a, 256)(
        (y_a, z_a, o_dn, z_b, *[o for o, _ in o_lse], *[l for _, l in o_lse], z_c), (), (row(W["dn_out_norm_g"]),), ())
    (b0,) = _make_multi_linear(f"{tag}_br0", 1)(ya, (W["w_branch"][0],))
    (b1,) = _make_multi_linear(f"{tag}_br1", 1)(yb, (W["w_branch"][1],))
    (b2,) = _make_multi_linear(f"{tag}_br2", 1)(yc, (W["w_branch"][2],))
    (mix,) = _make_rowwise(f"{tag}_merge_b", _f_merge_b, 256)((b0, b1, b2, gl), (), (), ())
    return _make_resid_linear(f"{tag}_out")(x, mix, W["w_out"])


SHARDED = (("w_in", (D_MODEL, W_IN_SHARD)), ("mla_w_q_b", (MLA_Q_RANK, MLA_HEADS * MLA_QK // N_DEV)),
           ("mla_w_kv_b", (MLA_KV_RANK, MLA_HEADS * 2 * LANES // N_DEV)), ("w_branch", (3 * BRANCH_W, D_MODEL // N_DEV)),
           ("w_out", (D_MODEL // N_DEV, D_MODEL)), ("dn_conv_w", (DN_CONV, 3 * DN_HEADS * LANES // N_DEV)))
SMALL = (("norm_g", D_MODEL), ("mla_q_a_norm_g", MLA_Q_RANK), ("mla_kv_a_norm_g", MLA_KV_RANK), ("mla_q_norm_g", MLA_QK),
         ("mla_k_norm_g", MLA_QK), ("dn_a_log", DN_HEADS), ("dn_dt_bias", DN_HEADS), ("dn_out_norm_g", LANES),
         ("dil_q_norm_g", LANES), ("dil_k_norm_g", LANES))
WEIGHTS = ("norm_g", "w_in", "mla_q_a_norm_g", "mla_w_q_b", "mla_kv_a_norm_g", "mla_w_kv_b", "mla_q_norm_g", "mla_k_norm_g",
           "dn_conv_w", "dn_a_log", "dn_dt_bias", "dn_out_norm_g", "dil_q_norm_g", "dil_k_norm_g", "w_branch", "w_out")


def _round_up(n, m):
    return -(-n // m) * m


def _pack_vectors(pieces):
    return jnp.concatenate([jnp.pad(p, (0, _round_up(p.shape[0], LANES) - p.shape[0])) for p in pieces]).reshape(-1, LANES)


def _unpack_vectors(flat, sizes):
    out, off = [], 0
    flat = flat.reshape(-1)
    for n in sizes:
        out.append(flat[off:off + n])
        off += _round_up(n, LANES)
    return out


def _whole_weights(g, small):
    W = dict(small)
    W["mla_w_q_b"] = g["mla_w_q_b"].transpose(1, 0, 2).reshape(MLA_Q_RANK, -1)
    W["mla_w_kv_b"] = g["mla_w_kv_b"].transpose(1, 0, 2).reshape(MLA_KV_RANK, -1)
    W["w_branch"] = g["w_branch"].reshape(N_DEV, 3, BRANCH_W, -1).transpose(1, 2, 0, 3).reshape(3, BRANCH_W, D_MODEL)
    W["w_out"] = g["w_out"].reshape(D_MODEL, D_MODEL)
    W["dn_conv_w"] = g["dn_conv_w"].transpose(1, 0, 2).reshape(DN_CONV, -1)
    return W


def kernel(x, positions, norm_g, w_in, mla_q_a_norm_g, mla_w_q_b, mla_kv_a_norm_g, mla_w_kv_b, mla_q_norm_g, mla_k_norm_g, dn_conv_w, dn_a_log, dn_dt_bias, dn_out_norm_g, dil_q_norm_g, dil_k_norm_g, w_branch, w_out, loss_target, m_norm_g, m_w_in, m_mla_q_a_norm_g, m_mla_w_q_b, m_mla_kv_a_norm_g, m_mla_w_kv_b, m_mla_q_norm_g, m_mla_k_norm_g, m_dn_conv_w, m_dn_a_log, m_dn_dt_bias, m_dn_out_norm_g, m_dil_q_norm_g, m_dil_k_norm_g, m_w_branch, m_w_out, v_norm_g, v_w_in, v_mla_q_a_norm_g, v_mla_w_q_b, v_mla_kv_a_norm_g, v_mla_w_kv_b, v_mla_q_norm_g, v_mla_k_norm_g, v_dn_conv_w, v_dn_a_log, v_dn_dt_bias, v_dn_out_norm_g, v_dil_q_norm_g, v_dil_k_norm_g, v_w_branch, v_w_out):
    w = dict(norm_g=norm_g, w_in=w_in, mla_q_a_norm_g=mla_q_a_norm_g, mla_w_q_b=mla_w_q_b, mla_kv_a_norm_g=mla_kv_a_norm_g,
             mla_w_kv_b=mla_w_kv_b, mla_q_norm_g=mla_q_norm_g, mla_k_norm_g=mla_k_norm_g, dn_conv_w=dn_conv_w, dn_a_log=dn_a_log,
             dn_dt_bias=dn_dt_bias, dn_out_norm_g=dn_out_norm_g, dil_q_norm_g=dil_q_norm_g, dil_k_norm_g=dil_k_norm_g,
             w_branch=w_branch, w_out=w_out)
    m = dict(norm_g=m_norm_g, w_in=m_w_in, mla_q_a_norm_g=m_mla_q_a_norm_g, mla_w_q_b=m_mla_w_q_b, mla_kv_a_norm_g=m_mla_kv_a_norm_g,
             mla_w_kv_b=m_mla_w_kv_b, mla_q_norm_g=m_mla_q_norm_g, mla_k_norm_g=m_mla_k_norm_g, dn_conv_w=m_dn_conv_w,
             dn_a_log=m_dn_a_log, dn_dt_bias=m_dn_dt_bias, dn_out_norm_g=m_dn_out_norm_g, dil_q_norm_g=m_dil_q_norm_g,
             dil_k_norm_g=m_dil_k_norm_g, w_branch=m_w_branch, w_out=m_w_out)
    v = dict(norm_g=v_norm_g, w_in=v_w_in, mla_q_a_norm_g=v_mla_q_a_norm_g, mla_w_q_b=v_mla_w_q_b, mla_kv_a_norm_g=v_mla_kv_a_norm_g,
             mla_w_kv_b=v_mla_w_kv_b, mla_q_norm_g=v_mla_q_norm_g, mla_k_norm_g=v_mla_k_norm_g, dn_conv_w=v_dn_conv_w,
             dn_a_log=v_dn_a_log, dn_dt_bias=v_dn_dt_bias, dn_out_norm_g=v_dn_out_norm_g, dil_q_norm_g=v_dil_q_norm_g,
             dil_k_norm_g=v_dil_k_norm_g, w_branch=v_w_branch, w_out=v_w_out)
    x2, target = x[0], loss_target[0]
    pos = positions[0][:, None]

    names = [n for n, _ in SHARDED]
    view = lambda t, n, s: t[n].reshape((DEPTH,) + s)
    shards = [[(view(w, n, s) if n == "dn_conv_w" else view(w, n, s).astype(BF16))[l] for n, s in SHARDED] for l in range(DEPTH)]
    small = [{n: w[n][l] for n, _ in SMALL} for l in range(DEPTH)]
    gathered0 = dict(zip(names, _all_gather(shards[0])))
    gathering1, pos = _exchange_start("gather_layer1_start", shards[1], True, pos)
    tables = _rope_tables(pos, _rope_consts())

    def layer(l, g, small_l, x_l):
        tag = f"l{l}"
        seg_op = _make_w_in_segments(f"{tag}_w_in_segments")
        w_segs, vjp_segs = jax.vjp(lambda gw: tuple(s[0] for s in seg_op(gw[:, None])), g["w_in"])
        h, vjp_norm = jax.vjp(lambda x_, ng: _layer_norm(tag, x_, ng), x_l, small_l["norm_g"])
        segs = tuple(_mm(f"{tag}_inproj_fwd{i}", h, w_, "nn") for i, w_ in enumerate(w_segs))
        rest_g = {n: a for n, a in g.items() if n != "w_in"}
        rest_s = {n: a for n, a in small_l.items() if n != "norm_g"}
        y, vjp_tail = jax.vjp(lambda sg, x_, gg, ss: _layer_tail(tag, x_, sg, tables, _whole_weights(gg, ss)), segs, x_l, rest_g, rest_s)

        def backward(dy):
            dsegs, dx_skip, d_rest_g, d_rest_s = vjp_tail(dy)
            dws = tuple(_mm(f"{tag}_inproj_dw{i}", h, d, "tn", out_dtype=w_.dtype) for i, (w_, d) in enumerate(zip(w_segs, dsegs)))
            dg = dict(d_rest_g, w_in=vjp_segs(dws)[0])
            exchanging, first = _exchange_start(f"exchange_{tag}_start", [dg[n] for n in names], False, dsegs[0])
            dh = None
            for i, (w_, d) in enumerate(zip(w_segs, (first,) + tuple(dsegs[1:]))):
                dh = _mm(f"{tag}_inproj_dh{i}", d, w_, "nt", acc=dh, out_dtype=h.dtype if i == len(w_segs) - 1 else F32)
            dx_norm, d_norm_g = vjp_norm(dh)
            return exchanging, dx_skip + dx_norm, dict(d_rest_s, norm_g=d_norm_g)
        return y, backward

    y0, backward0 = layer(0, gathered0, small[0], x2)
    gathered1 = dict(zip(names, _exchange_wait("gather_layer1_wait", gathering1, y0)))
    y1, backward1 = layer(1, gathered1, small[1], y0)
    loss_splat, dy = _loss_call(y1, target)
    loss = lax.psum(loss_splat[0, 0], ("x", "y", "c"))
    exchanging1, d_y0, g_small1 = backward1(dy)
    exchanging0, g_x, g_small0 = backward0(d_y0)

    g_small = (g_small0, g_small1)
    sizes = [k for _ in range(DEPTH) for _, k in SMALL]
    g_vec = _pack_vectors([g_small[l][n] for l in range(DEPTH) for n, _ in SMALL])
    (parts_vec,) = _all_to_all([jnp.broadcast_to(g_vec[None], (N_DEV,) + g_vec.shape)])
    parts1 = _exchange_wait("exchange_l1_wait", exchanging1, parts_vec)
    parts0 = _exchange_wait("exchange_l0_wait", exchanging0, parts_vec)

    vec = lambda t: _pack_vectors([t[n][l] for l in range(DEPTH) for n, _ in SMALL])[None]
    outs = {}
    for i, (n, s) in enumerate(SHARDED):
        res = _adamw_call(f"adamw_{n}", [parts0[i], parts1[i]], view(w, n, s), view(m, n, s), view(v, n, s))
        outs[n] = [o.reshape(w[n].shape) for o in res]
    vec_outs = [_unpack_vectors(o, sizes) for o in _adamw_call("adamw_vectors", [parts_vec], vec(w), vec(m), vec(v))]
    for i, (n, _) in enumerate(SMALL):
        outs[n] = [jnp.stack([o[l * len(SMALL) + i] for l in range(DEPTH)]) for o in vec_outs]
    return (loss, g_x[None], *[outs[n][k] for k in range(4) for n in WEIGHTS])
```

```python
import functools
import math

import jax
import jax.numpy as jnp
from jax import lax
from jax.experimental import pallas as pl
from jax.experimental.pallas import tpu as pltpu

F32 = jnp.float32
BF16 = jnp.bfloat16
HI = lax.Precision.HIGHEST
MESH = pl.DeviceIdType.MESH

N_DEV = 8
D_MODEL = 1024
DEPTH = 2
RMS_EPS = 1e-6
ROPE_THETA = 10000.0
LANES = 128
MLA_HEADS = 4
MLA_ROPE = 64
MLA_QK = 192
MLA_Q_RANK = 384
MLA_KV_RANK = 256
DN_HEADS = 4
DN_CHUNK = 64
DN_CONV = 4
DIL_HEADS = 12
DIL_GROUP_HEADS = 4
DIL_DILATIONS = (1, 4, 16)
DIL_BLOCK = 128
BRANCH_W = 512
IN_WIDTH = 11464
NEG = -1e30
VMEM_LIMIT = 56 * 1024 * 1024

ADAM_LR, ADAM_B1, ADAM_B2, ADAM_EPS, ADAM_WD, ADAM_STEP = 0.001, 0.9, 0.999, 1e-08, 0.01, 10

_SEG = {}
_off = 0
for _n, _w in (("q_lat", 384), ("c_kv", 256), ("k_pe", 64), ("z_a", 512), ("dn_qkv", 1536), ("dn_a", 4), ("dn_b", 4),
               ("z_b", 512), ("dil_qkv", 4608), ("z_c", 512), ("gate", 3072)):
    _SEG[_n] = (_off, _w)
    _off += _w
assert _off == IN_WIDTH


def _pcall(body, **kw):
    return pl.pallas_call(body, **kw)


def _params(sem=None):
    return pltpu.CompilerParams(dimension_semantics=sem, vmem_limit_bytes=VMEM_LIMIT)


def _tile(n, target, mult):
    t = (min(n, target) // mult) * mult
    while t >= mult:
        if n % t == 0:
            return t
        t -= mult
    return n


def _mm(name, a, b, mode, out_dtype=F32, acc=None, tm=1024, tn=512, tk=1024):
    if mode == "nn":
        (M, K), (_, N) = a.shape, b.shape
    elif mode == "nt":
        (M, K), (N, _) = a.shape, b.shape
    else:
        (K, M), (_, N) = a.shape, b.shape
    tm, tn, tk = _tile(M, tm, LANES), _tile(N, tn, LANES), _tile(K, tk, LANES)
    nk = K // tk
    dims = {"nn": (((1,), (0,)), ((), ())), "nt": (((1,), (1,)), ((), ())), "tn": (((0,), (0,)), ((), ()))}[mode]
    a_spec = pl.BlockSpec((tk, tm), lambda i, j, k: (k, i)) if mode == "tn" else pl.BlockSpec((tm, tk), lambda i, j, k: (i, k))
    b_spec = pl.BlockSpec((tn, tk), lambda i, j, k: (j, k)) if mode == "nt" else pl.BlockSpec((tk, tn), lambda i, j, k: (k, j))
    o_spec = pl.BlockSpec((tm, tn), lambda i, j, k: (i, j))
    has_acc = acc is not None

    def body(*refs):
        a_ref, b_ref = refs[:2]
        c_ref = refs[2] if has_acc else None
        o_ref = refs[3] if has_acc else refs[2]
        prod = lax.dot_general(a_ref[...].astype(BF16), b_ref[...].astype(BF16), dims, preferred_element_type=F32)
        if nk == 1:
            o_ref[...] = (prod + c_ref[...].astype(F32) if has_acc else prod).astype(out_dtype)
            return
        acc_ref = refs[-1]
        k = pl.program_id(2)

        @pl.when(k == 0)
        def _():
            acc_ref[...] = prod + c_ref[...].astype(F32) if has_acc else prod

        @pl.when(k > 0)
        def _():
            acc_ref[...] += prod

        @pl.when(k == nk - 1)
        def _():
            o_ref[...] = acc_ref[...].astype(out_dtype)

    ins = [a, b] + ([acc] if has_acc else [])
    in_specs = [a_spec, b_spec] + ([o_spec] if has_acc else [])
    return _pcall(body, name=name, grid=(M // tm, N // tn, nk), in_specs=in_specs, out_specs=o_spec,
                  out_shape=jax.ShapeDtypeStruct((M, N), out_dtype), scratch_shapes=[pltpu.VMEM((tm, tn), F32)] if nk > 1 else [],
                  compiler_params=_params(("parallel", "parallel", "arbitrary")))(*ins)


INPROJ_TILES = {"nn": dict(tm=1024, tn=1536, tk=1024), "tn": dict(tm=1024, tn=768, tk=2048), "nt": dict(tm=512, tn=1024, tk=8192)}


def _make_multi_linear(name, n):
    @jax.custom_vjp
    def op(h, ws):
        return tuple(_mm(f"{name}_fwd{i}", h, w, "nn") for i, w in enumerate(ws))

    def fwd(h, ws):
        return op(h, ws), (h, ws)

    def bwd(res, douts):
        h, ws = res
        dh = None
        for i, (w, d) in enumerate(zip(ws, douts)):
            dh = _mm(f"{name}_dh{i}", d, w, "nt", acc=dh, out_dtype=h.dtype if i == len(ws) - 1 else F32)
        dws = tuple(_mm(f"{name}_dw{i}", h, d, "tn", out_dtype=w.dtype) for i, (w, d) in enumerate(zip(ws, douts)))
        return dh, dws

    op.defvjp(fwd, bwd)
    return op


def _make_resid_linear(name):
    @jax.custom_vjp
    def op(x, a, w):
        return _mm(f"{name}_fwd", a, w, "nn", acc=x)

    def fwd(x, a, w):
        return op(x, a, w), (a, w)

    def bwd(res, dy):
        a, w = res
        return dy, _mm(f"{name}_da", dy, w, "nt", out_dtype=a.dtype), _mm(f"{name}_dw", a, dy, "tn", out_dtype=w.dtype)

    op.defvjp(fwd, bwd)
    return op


def _make_rowwise(name, f, tile):
    def specs(rows, aux, params, consts, t):
        row = [pl.BlockSpec((t, a.shape[1]), lambda i: (i, 0)) for a in (*rows, *aux)]
        full = [pl.BlockSpec(p.shape, lambda i: (0, 0)) for p in (*params, *consts)]
        return row, full

    def fwd_call(rows, aux, params, consts):
        S = rows[0].shape[0]
        t = min(tile, S)
        n_in = len(rows) + len(aux) + len(params) + len(consts)
        shp = lambda a: jax.ShapeDtypeStruct((t, a.shape[1]), a.dtype)
        outs = jax.eval_shape(f, *[shp(a) for a in (*rows, *aux)], *params, *consts)
        row_specs, full_specs = specs(rows, aux, params, consts, t)

        def body(*refs):
            res = f(*[r[...] for r in refs[:n_in]])
            for o_ref, o in zip(refs[n_in:], res):
                o_ref[...] = o

        return _pcall(body, name=f"{name}_fwd", grid=(S // t,), in_specs=row_specs + full_specs,
                      out_specs=[pl.BlockSpec((t, o.shape[1]), lambda i: (i, 0)) for o in outs],
                      out_shape=[jax.ShapeDtypeStruct((S, o.shape[1]), o.dtype) for o in outs],
                      compiler_params=_params(("parallel",)))(*rows, *aux, *params, *consts)

    def bwd_call(rows, aux, params, consts, douts):
        S = rows[0].shape[0]
        t = min(tile, S)
        nr, na, npar, nc, nd = len(rows), len(aux), len(params), len(consts), len(douts)
        row_specs, full_specs = specs(rows, aux, params, consts, t)

        def body(*refs):
            vals = [r[...] for r in refs[:nr + na + npar + nc]]
            rv, av = vals[:nr], vals[nr:nr + na]
            pv, cv = vals[nr + na:nr + na + npar], vals[nr + na + npar:]
            dv = tuple(r[...] for r in refs[nr + na + npar + nc:nr + na + npar + nc + nd])
            out_refs = refs[nr + na + npar + nc + nd:]
            _, vjp = jax.vjp(lambda *rp: f(*rp[:nr], *av, *rp[nr:], *cv), *rv, *pv)
            grads = vjp(dv)
            for o_ref, g in zip(out_refs[:nr], grads[:nr]):
                o_ref[...] = g
            first = pl.program_id(0) == 0
            for o_ref, g in zip(out_refs[nr:], grads[nr:]):
                @pl.when(first)
                def _(o_ref=o_ref):
                    o_ref[...] = jnp.zeros_like(o_ref)
                o_ref[...] += g

        res = _pcall(body, name=f"{name}_bwd", grid=(S // t,),
                     in_specs=row_specs + full_specs + [pl.BlockSpec((t, d.shape[1]), lambda i: (i, 0)) for d in douts],
                     out_specs=[pl.BlockSpec((t, a.shape[1]), lambda i: (i, 0)) for a in rows]
                     + [pl.BlockSpec(p.shape, lambda i: (0, 0)) for p in params],
                     out_shape=[jax.ShapeDtypeStruct(a.shape, a.dtype) for a in (*rows, *params)],
                     compiler_params=_params(("arbitrary",)))(*rows, *aux, *params, *consts, *douts)
        return tuple(res[:nr]), tuple(res[nr:])

    @jax.custom_vjp
    def op(rows, aux, params, consts):
        return tuple(fwd_call(rows, aux, params, consts))

    def fwd(rows, aux, params, consts):
        return op(rows, aux, params, consts), (rows, aux, params, consts)

    def bwd(res, douts):
        rows, aux, params, consts = res
        drows, dparams = bwd_call(rows, aux, params, consts, tuple(douts))
        zeros = lambda xs: tuple(jnp.zeros_like(a) for a in xs)
        return drows, zeros(aux), dparams, zeros(consts)

    op.defvjp(fwd, bwd)
    return op


@jax.custom_vjp
def _swap_halves(x):
    return pltpu.roll(x, LANES // 2, 1)


_swap_halves.defvjp(lambda x: (_swap_halves(x), None), lambda _, g: (_swap_halves(g),))


def _rope(x, cos_t, sin_t):
    return x * cos_t + _swap_halves(x) * sin_t


def _rms(x, g, n=None):
    n = x.shape[-1] if n is None else n
    return x * lax.rsqrt(jnp.sum(x * x, axis=-1, keepdims=True) * (1.0 / n) + RMS_EPS) * g


def _heads(x):
    return [x[:, i * LANES:(i + 1) * LANES] for i in range(x.shape[1] // LANES)]


def _cat(xs):
    return jnp.concatenate(xs, axis=1)


def _silu(x):
    return x * jax.nn.sigmoid(x)


def _f_norm(x, g):
    return (_rms(x, g).astype(BF16),)


def _f_mla_a(q_lat, c_kv, kpe, cos_p, sin_p, qa_g, kva_g, kpe_g):
    kp = _rope(_rms(kpe, kpe_g, MLA_ROPE), cos_p, sin_p)
    return _rms(q_lat, qa_g).astype(BF16), _rms(c_kv, kva_g).astype(BF16), _cat([kp] * MLA_HEADS)


def _f_mla_b(q8, kn_raw, cos_p, sin_p, qn_g, qp_g, kn_g):
    hs = _heads(q8)
    qn = _cat([_rms(h, qn_g) for h in hs[:MLA_HEADS]])
    qp = _cat([_rope(_rms(h, qp_g, MLA_ROPE), cos_p, sin_p) for h in hs[MLA_HEADS:]])
    kn = _cat([_rms(h, kn_g) for h in _heads(kn_raw)])
    return qn, qp, kn


def _softplus(x):
    return jnp.maximum(x, 0.0) + jnp.log(1.0 + jnp.exp(-jnp.abs(x)))


def _l2n(x):
    return x * lax.rsqrt(jnp.sum(x * x, axis=-1, keepdims=True) + 1e-6)


def _f_dn_pre(mixed, ab, alog_f, dtb_f, e_a, e_b):
    hs = _heads(mixed)
    q = _cat([_l2n(h) * (LANES ** -0.5) for h in hs[:DN_HEADS]])
    k = _cat([_l2n(h) for h in hs[DN_HEADS:2 * DN_HEADS]])
    v = _cat(hs[2 * DN_HEADS:])
    a_f = jnp.dot(ab, e_a, precision=HI, preferred_element_type=F32)
    b_f = jnp.dot(ab, e_b, precision=HI, preferred_element_type=F32)
    g = -jnp.exp(alog_f) * _softplus(a_f + dtb_f)
    return q, k, v, g, jax.nn.sigmoid(b_f)


def _f_dil_pre(qkv, cos_h, sin_h, q_g, k_g):
    hs = _heads(qkv)
    q = [_rope(_rms(h, q_g), cos_h, sin_h) for h in hs[:DIL_HEADS]]
    k = [_rope(_rms(h, k_g), cos_h, sin_h) for h in hs[DIL_HEADS:2 * DIL_HEADS]]
    v = hs[2 * DIL_HEADS:]
    group = lambda xs, g: _cat(xs[g * DIL_GROUP_HEADS:(g + 1) * DIL_GROUP_HEADS])
    return tuple(group(xs, g) for xs in (q, k, v) for g in range(len(DIL_DILATIONS)))


def _f_merge_a(y_a, z_a, o_dn, z_b, o0, o1, o2, l0, l1, l2, z_c, out_g):
    y_b = _cat([_rms(h, out_g) for h in _heads(o_dn)])
    os_, ls = [_heads(o) for o in (o0, o1, o2)], [_heads(l) for l in (l0, l1, l2)]
    y_c = []
    for j in range(DIL_GROUP_HEADS):
        l3 = [ls[g][j] for g in range(3)]
        m = jnp.maximum(jnp.maximum(l3[0], l3[1]), l3[2])
        e3 = [jnp.exp(l - m) for l in l3]
        den = e3[0] + e3[1] + e3[2]
        y_c.append(sum(e3[g] * os_[g][j] for g in range(3)) / den)
    return tuple(t.astype(BF16) for t in (y_a * _silu(z_a), y_b * _silu(z_b), _cat(y_c) * _silu(z_c)))


def _f_merge_b(b0, b1, b2, gl):
    gs = [jax.nn.sigmoid(gl[:, i * D_MODEL:(i + 1) * D_MODEL]) for i in range(3)]
    return ((gs[0] * b0 + gs[1] * b1 + gs[2] * b2).astype(BF16),)


def _rope_tables(pos, inv_sign):
    S = pos.shape[0]
    t = min(S, 1024)

    def body(p_ref, c_ref, cp, sp, ch, sh):
        p = p_ref[...].astype(F32)
        c = c_ref[...]
        ang_p, ang_h = p * c[0:1], p * c[2:3]
        cp[...] = jnp.cos(ang_p) * jnp.abs(c[1:2])
        sp[...] = jnp.sin(ang_p) * c[1:2]
        ch[...] = jnp.cos(ang_h)
        sh[...] = jnp.sin(ang_h) * c[3:4]

    row = pl.BlockSpec((t, LANES), lambda i: (i, 0))
    return _pcall(body, name="rope_tables", grid=(S // t,),
                  in_specs=[pl.BlockSpec((t, 1), lambda i: (i, 0)), pl.BlockSpec((4, LANES), lambda i: (0, 0))],
                  out_specs=[row] * 4, out_shape=[jax.ShapeDtypeStruct((S, LANES), F32)] * 4,
                  compiler_params=_params(("parallel",)))(pos, inv_sign)


def _rope_consts():
    half_p, half_h = MLA_ROPE // 2, LANES // 2
    inv_p = 1.0 / (ROPE_THETA ** (jnp.arange(0, MLA_ROPE, 2, dtype=F32) / MLA_ROPE))
    inv_h = 1.0 / (ROPE_THETA ** (jnp.arange(0, LANES, 2, dtype=F32) / LANES))
    z = jnp.zeros((half_p,), F32)
    o = jnp.ones((half_p,), F32)
    return jnp.stack([jnp.concatenate([inv_p, z, inv_p, z]), jnp.concatenate([-o, z, o, z]),
                      jnp.concatenate([inv_h, inv_h]), jnp.concatenate([-jnp.ones((half_h,), F32), jnp.ones((half_h,), F32)])])


def _shift_rows(x, s, up):
    n = x.shape[0]
    r = lax.broadcasted_iota(jnp.int32, x.shape, 0)
    if up:
        return jnp.where(r < n - s, pltpu.roll(x, n - s, 0), 0.0)
    return jnp.where(r >= s, pltpu.roll(x, s, 0), 0.0)


def _make_shift(s):
    @jax.custom_vjp
    def sh(x):
        return _shift_rows(x, s, False)

    sh.defvjp(lambda x: (sh(x), None), lambda _, g: (_shift_rows(g, s, True),))
    return sh


def _f_conv(x, w):
    y = x * w[DN_CONV - 1:DN_CONV]
    for j in range(DN_CONV - 1):
        y = y + _make_shift(DN_CONV - 1 - j)(x) * w[j:j + 1]
    return _silu(y)


def _make_conv(name):
    def call(x, w, dy=None):
        S, C = x.shape
        col = pl.BlockSpec((S, LANES), lambda i: (0, i))
        wsp = pl.BlockSpec((DN_CONV, LANES), lambda i: (0, i))
        if dy is None:
            def body(x_ref, w_ref, o_ref):
                o_ref[...] = _f_conv(x_ref[...], w_ref[...])
            return _pcall(body, name=f"{name}_fwd", grid=(C // LANES,), in_specs=[col, wsp], out_specs=col,
                          out_shape=jax.ShapeDtypeStruct(x.shape, F32), compiler_params=_params(("parallel",)))(x, w)

        def body(x_ref, w_ref, dy_ref, dx_ref, dw_ref):
            _, vjp = jax.vjp(_f_conv, x_ref[...], w_ref[...])
            dx_ref[...], dw_ref[...] = vjp(dy_ref[...])
        return _pcall(body, name=f"{name}_bwd", grid=(C // LANES,), in_specs=[col, wsp, col], out_specs=[col, wsp],
                      out_shape=[jax.ShapeDtypeStruct(x.shape, F32), jax.ShapeDtypeStruct(w.shape, F32)],
                      compiler_params=_params(("parallel",)))(x, w, dy)

    @jax.custom_vjp
    def op(x, w):
        return call(x, w)

    op.defvjp(lambda x, w: (op(x, w), (x, w)), lambda res, dy: tuple(call(*res, dy)))
    return op


def _dot_nt(a, b):
    return lax.dot_general(a.astype(BF16), b.astype(BF16), (((1,), (1,)), ((), ())), preferred_element_type=F32)


def _dot_nn(a, b):
    return jnp.dot(a.astype(BF16), b.astype(BF16), preferred_element_type=F32)


def _dot_tn(a, b):
    return lax.dot_general(a.astype(BF16), b.astype(BF16), (((0,), (0,)), ((), ())), preferred_element_type=F32)


def _mla_scores(qn_r, qp_r, kn_r, kp_r, diagonal):
    scale = MLA_QK ** -0.5
    s = _dot_nt(qn_r[...] * scale, kn_r[...]) + _dot_nt(qp_r[...] * scale, kp_r[...])
    if diagonal:
        r = lax.broadcasted_iota(jnp.int32, s.shape, 0)
        c = lax.broadcasted_iota(jnp.int32, s.shape, 1)
        s = jnp.where(c <= r, s, NEG)
    return s


def _on_causal_pairs(qi, ki, step):
    @pl.when(ki < qi)
    def _():
        step(False)

    @pl.when(ki == qi)
    def _():
        step(True)


def _causal_pairs(n, t, by_key):
    pairs = [(q, k) for k in range(n) for q in range(k, n)] if by_key else [(q, k) for q in range(n) for k in range(q + 1)]
    qt, kt = (jnp.array([p[i] for p in pairs], jnp.int32) for i in (0, 1))
    return (qt, kt, pl.BlockSpec((t, LANES), lambda h, p, qt_r, kt_r: (qt_r[p], h)),
            pl.BlockSpec((t, LANES), lambda h, p, qt_r, kt_r: (kt_r[p], h)))


def _make_mla_attn(name):
    scale = MLA_QK ** -0.5

    def fwd_call(qn, qp, kn, kp, v):
        S = qn.shape[0]
        t = min(S, 512)
        n = S // t
        qt, kt, qs, ks = _causal_pairs(n, t, by_key=False)

        def body(qt_r, kt_r, qn_r, qp_r, kn_r, kp_r, v_r, o_r, lse_r, m_s, l_s, acc_s):
            qi, ki = qt_r[pl.program_id(1)], kt_r[pl.program_id(1)]

            @pl.when(ki == 0)
            def _():
                m_s[...] = jnp.full_like(m_s, NEG)
                l_s[...] = jnp.zeros_like(l_s)
                acc_s[...] = jnp.zeros_like(acc_s)

            def step(diagonal):
                s = _mla_scores(qn_r, qp_r, kn_r, kp_r, diagonal)
                m_old = m_s[...]
                m_new = jnp.maximum(m_old, jnp.max(s, axis=-1, keepdims=True))
                p = jnp.exp(s - m_new[:, :1])
                alpha = jnp.exp(m_old - m_new)
                l_s[...] = alpha * l_s[...] + jnp.sum(p, axis=-1, keepdims=True)
                acc_s[...] = alpha * acc_s[...] + _dot_nn(p, v_r[...])
                m_s[...] = m_new
            _on_causal_pairs(qi, ki, step)

            @pl.when(ki == qi)
            def _():
                o_r[...] = acc_s[...] / l_s[...]
                lse_r[...] = m_s[...] + jnp.log(l_s[...])

        spec = pltpu.PrefetchScalarGridSpec(num_scalar_prefetch=2, grid=(MLA_HEADS, qt.shape[0]), in_specs=[qs, qs, ks, ks, ks],
                                            out_specs=[qs, qs], scratch_shapes=[pltpu.VMEM((t, LANES), F32)] * 3)
        return _pcall(body, name=f"{name}_fwd", grid_spec=spec, out_shape=[jax.ShapeDtypeStruct((S, MLA_HEADS * LANES), F32)] * 2,
                      compiler_params=_params(("parallel", "arbitrary")))(qt, kt, qn, qp, kn, kp, v)

    def dq_call(qn, qp, kn, kp, v, o, lse, do):
        S = qn.shape[0]
        t = min(S, 512)
        n = S // t
        qt, kt, qs, ks = _causal_pairs(n, t, by_key=False)

        def body(qt_r, kt_r, qn_r, qp_r, kn_r, kp_r, v_r, o_r, lse_r, do_r, dqn_r, dqp_r, dl_r, dqn_s, dqp_s):
            qi, ki = qt_r[pl.program_id(1)], kt_r[pl.program_id(1)]

            @pl.when(ki == 0)
            def _():
                dqn_s[...] = jnp.zeros_like(dqn_s)
                dqp_s[...] = jnp.zeros_like(dqp_s)
                dl_r[...] = jnp.broadcast_to(jnp.sum(do_r[...] * o_r[...], axis=-1, keepdims=True), dl_r.shape)

            def step(diagonal):
                p = jnp.exp(_mla_scores(qn_r, qp_r, kn_r, kp_r, diagonal) - lse_r[...][:, :1])
                ds = p * (_dot_nt(do_r[...], v_r[...]) - dl_r[...][:, :1])
                dqn_s[...] += _dot_nn(ds, kn_r[...])
                dqp_s[...] += _dot_nn(ds, kp_r[...])
            _on_causal_pairs(qi, ki, step)

            @pl.when(ki == qi)
            def _():
                dqn_r[...] = dqn_s[...] * scale
                dqp_r[...] = dqp_s[...] * scale

        spec = pltpu.PrefetchScalarGridSpec(num_scalar_prefetch=2, grid=(MLA_HEADS, qt.shape[0]), in_specs=[qs, qs, ks, ks, ks, qs, qs, qs],
                                            out_specs=[qs, qs, qs], scratch_shapes=[pltpu.VMEM((t, LANES), F32)] * 2)
        return _pcall(body, name=f"{name}_dq", grid_spec=spec, out_shape=[jax.ShapeDtypeStruct((S, MLA_HEADS * LANES), F32)] * 3,
                      compiler_params=_params(("parallel", "arbitrary")))(qt, kt, qn, qp, kn, kp, v, o, lse, do)

    def dkv_call(qn, qp, kn, kp, v, lse, do, dl):
        S = qn.shape[0]
        t = min(S, 512)
        n = S // t
        qt, kt, qs, ks = _causal_pairs(n, t, by_key=True)

        def body(qt_r, kt_r, qn_r, qp_r, kn_r, kp_r, v_r, lse_r, do_r, dl_r, dkn_r, dkp_r, dv_r, dkn_s, dkp_s, dv_s):
            qi, ki = qt_r[pl.program_id(1)], kt_r[pl.program_id(1)]

            @pl.when(qi == ki)
            def _():
                dkn_s[...] = jnp.zeros_like(dkn_s)
                dkp_s[...] = jnp.zeros_like(dkp_s)
                dv_s[...] = jnp.zeros_like(dv_s)

            def step(diagonal):
                p = jnp.exp(_mla_scores(qn_r, qp_r, kn_r, kp_r, diagonal) - lse_r[...][:, :1])
                ds = p * (_dot_nt(do_r[...], v_r[...]) - dl_r[...][:, :1])
                dv_s[...] += _dot_tn(p, do_r[...])
                dkn_s[...] += _dot_tn(ds, qn_r[...] * scale)
                dkp_s[...] += _dot_tn(ds, qp_r[...] * scale)
            _on_causal_pairs(qi, ki, step)

            @pl.when(qi == n - 1)
            def _():
                dkn_r[...] = dkn_s[...]
                dkp_r[...] = dkp_s[...]
                dv_r[...] = dv_s[...]

        spec = pltpu.PrefetchScalarGridSpec(num_scalar_prefetch=2, grid=(MLA_HEADS, qt.shape[0]), in_specs=[qs, qs, ks, ks, ks, qs, qs, qs],
                                            out_specs=[ks, ks, ks], scratch_shapes=[pltpu.VMEM((t, LANES), F32)] * 3)
        return _pcall(body, name=f"{name}_dkv", grid_spec=spec, out_shape=[jax.ShapeDtypeStruct((S, MLA_HEADS * LANES), F32)] * 3,
                      compiler_params=_params(("parallel", "arbitrary")))(qt, kt, qn, qp, kn, kp, v, lse, do, dl)

    @jax.custom_vjp
    def op(qn, qp, kn, kp, v):
        return fwd_call(qn, qp, kn, kp, v)[0]

    def fwd(qn, qp, kn, kp, v):
        o, lse = fwd_call(qn, qp, kn, kp, v)
        return o, (qn, qp, kn, kp, v, o, lse)

    def bwd(res, do):
        qn, qp, kn, kp, v, o, lse = res
        dqn, dqp, dl = dq_call(qn, qp, kn, kp, v, o, lse, do)
        dkn, dkp, dv = dkv_call(qn, qp, kn, kp, v, lse, do, dl)
        return dqn, dqp, dkn, dkp, dv

    op.defvjp(fwd, bwd)
    return op


def _dil_block(q, kp, kc, vp, vc, has_prev):
    scale = LANES ** -0.5
    r = lax.broadcasted_iota(jnp.int32, (DIL_BLOCK, 2 * DIL_BLOCK), 0)
    c = lax.broadcasted_iota(jnp.int32, (DIL_BLOCK, 2 * DIL_BLOCK), 1)
    valid = ((c < DIL_BLOCK) & (c >= r) & has_prev) | ((c >= DIL_BLOCK) & (c - DIL_BLOCK <= r))
    s = jnp.where(valid, _dot_nt(q * scale, jnp.concatenate([kp, kc], axis=0)), NEG)
    m = jnp.max(s, axis=-1, keepdims=True)
    e = jnp.exp(s - m)
    den = jnp.sum(e, axis=-1, keepdims=True)
    o = _dot_nn(e, jnp.concatenate([vp, vc], axis=0)) / den
    return o, jnp.broadcast_to(m + jnp.log(den), o.shape)


DIL_TILE_ROWS = (1024, 1024, 2048)


def _make_dil_attn(name, d, tile_rows):
    def call(q, k, v, cts=None):
        S = q.shape[0]
        span = DIL_BLOCK * d
        G = max(1, min(tile_rows, S) // span)
        n = S // (G * span)
        at = (lambda i: i) if cts is None else (lambda i: n - 1 - i)
        tile = pl.BlockSpec((G * span, LANES), lambda h, i: (at(i), h))
        before = pl.BlockSpec((span, LANES), lambda h, i: (jnp.maximum(at(i) * G - 1, 0), h))

        def rows(r, j):
            return pl.ds(j * DIL_BLOCK, DIL_BLOCK) if d == 1 else pl.ds(r + j * span, DIL_BLOCK, stride=d)

        def over_residues(fn):
            if d == 1:
                fn(0)
            else:
                lax.fori_loop(0, d, lambda r, c: (fn(r), c)[1], 0)

        def block_inputs(r, j, q_r, kb_r, k_r, vb_r, v_r):
            kp = kb_r[rows(r, 0), :] if j == 0 else k_r[rows(r, j - 1), :]
            vp = vb_r[rows(r, 0), :] if j == 0 else v_r[rows(r, j - 1), :]
            return q_r[rows(r, j), :], kp, k_r[rows(r, j), :], vp, v_r[rows(r, j), :]

        if cts is None:
            def body(q_r, kb_r, k_r, vb_r, v_r, o_r, lse_r):
                first = at(pl.program_id(1)) * G

                def residue(r):
                    for j in range(G):
                        o_r[rows(r, j), :], lse_r[rows(r, j), :] = _dil_block(
                            *block_inputs(r, j, q_r, kb_r, k_r, vb_r, v_r), first + j > 0)
                over_residues(residue)
            return _pcall(body, name=f"{name}_fwd", grid=(DIL_GROUP_HEADS, n), in_specs=[tile, before, tile, before, tile],
                          out_specs=[tile, tile], out_shape=[jax.ShapeDtypeStruct(q.shape, F32)] * 2,
                          compiler_params=_params(("parallel", "parallel")))(q, k, k, v, v)

        def body(q_r, kb_r, k_r, vb_r, v_r, do_r, dl_r, dq_r, dk_r, dv_r, ck_s, cv_s):
            first = at(pl.program_id(1)) * G

            @pl.when(pl.program_id(1) == 0)
            def _():
                ck_s[...] = jnp.zeros_like(ck_s)
                cv_s[...] = jnp.zeros_like(cv_s)

            def residue(r):
                owed = None
                for j in range(G):
                    hp = first + j > 0
                    _, vjp = jax.vjp(lambda *a: _dil_block(*a, hp), *block_inputs(r, j, q_r, kb_r, k_r, vb_r, v_r))
                    dq, dkp, dkc, dvp, dvc = vjp((do_r[rows(r, j), :], dl_r[rows(r, j), :]))
                    dq_r[rows(r, j), :] = dq
                    if j == G - 1:
                        dkc, dvc = dkc + ck_s[rows(r, 0), :], dvc + cv_s[rows(r, 0), :]
                    dk_r[rows(r, j), :], dv_r[rows(r, j), :] = dkc, dvc
                    if j == 0:
                        owed = (dkp, dvp)
                    else:
                        dk_r[rows(r, j - 1), :] += dkp
                        dv_r[rows(r, j - 1), :] += dvp
                ck_s[rows(r, 0), :], cv_s[rows(r, 0), :] = owed
            over_residues(residue)
        return _pcall(body, name=f"{name}_bwd", grid=(DIL_GROUP_HEADS, n), in_specs=[tile, before, tile, before, tile, tile, tile],
                      out_specs=[tile] * 3, out_shape=[jax.ShapeDtypeStruct(q.shape, F32)] * 3,
                      scratch_shapes=[pltpu.VMEM((span, LANES), F32)] * 2,
                      compiler_params=_params(("parallel", "arbitrary")))(q, k, k, v, v, *cts)

    @jax.custom_vjp
    def op(q, k, v):
        return tuple(call(q, k, v))

    op.defvjp(lambda q, k, v: (op(q, k, v), (q, k, v)), lambda res, cts: tuple(call(*res, cts=cts)))
    return op


def _pdot(a, b, dims):
    return lax.dot_general(a, b, (dims, ((), ())), precision=lax.Precision.HIGH, preferred_element_type=F32)


DN_LOCAL_CHUNKS = 4


DN_BLOCK_HEADS = 4
DN_BLOCK = DN_BLOCK_HEADS * DN_CHUNK


def _dn_local(q, k, v, g, b):
    C, R = DN_CHUNK, DN_BLOCK
    r = lax.broadcasted_iota(jnp.int32, (R, R), 0)
    c = lax.broadcasted_iota(jnp.int32, (R, R), 1)
    same_head = (r // C) == (c // C)
    incl, strict = same_head & (r >= c), same_head & (r > c)
    eye = (r == c).astype(F32)
    avg = jnp.full((R, LANES), 1.0 / LANES, F32)
    rc = lax.broadcasted_iota(jnp.int32, (C, C), 0) >= lax.broadcasted_iota(jnp.int32, (C, C), 1)
    gc_lanes = _pdot(rc.astype(F32), g, ((1,), (0,)))
    us, ws, qes, kds, qks = [], [], [], [], []
    for first in range(0, DN_HEADS, DN_BLOCK_HEADS):
        stack = lambda x: jnp.concatenate(_heads(x)[first:first + DN_BLOCK_HEADS], axis=0)
        unstack = lambda x: [x[p * C:(p + 1) * C] for p in range(DN_BLOCK_HEADS)]
        gc, q_s, k_s, v_s, b_s = (stack(x) for x in (gc_lanes, q, k, v, b))
        gc_j = _pdot(avg, gc, ((1,), (1,)))
        decay = jnp.exp(jnp.where(incl, _cat([gc] * (R // LANES)) - gc_j, NEG))
        kb = k_s * b_s
        kk = _pdot(jnp.concatenate([kb, q_s], axis=0), k_s, ((1,), (1,)))
        a = jnp.where(strict, kk[:R] * decay, 0.0)
        inv, pw = eye - a, a
        for _ in range(5):
            pw = _pdot(pw, pw, ((1,), (0,)))
            inv = inv + _pdot(inv, pw, ((1,), (0,)))
        eg = jnp.exp(gc)
        uw = _pdot(inv, _cat([v_s * b_s, kb * eg]), ((1,), (0,)))
        g_last = jnp.concatenate([jnp.broadcast_to(x[C - 1:C], (C, LANES)) for x in unstack(gc)], axis=0)
        us += unstack(uw[:, :LANES])
        ws += unstack(uw[:, LANES:])
        qes += unstack(q_s * eg)
        kds += unstack(k_s * jnp.exp(g_last - gc))
        qks.append(kk[R:] * decay)
    egl = jnp.broadcast_to(jnp.exp(gc_lanes[C - 1:C]), (8, DN_HEADS * LANES))
    return _cat(us), _cat(ws), _cat(qes), _cat(kds), jnp.concatenate(qks, axis=0), egl


def _dn_scan(u, w, qe, kd, qk, egl, state):
    C = DN_CHUNK
    heads = [slice(h * LANES, (h + 1) * LANES) for h in range(DN_HEADS)]
    ws = [_pdot(jnp.concatenate([w[:, sl], qe[:, sl]], axis=0), state[sl, :], ((1,), (0,))) for sl in heads]
    v_new = [u[:, sl] - x[:C] for sl, x in zip(heads, ws)]
    local = []
    for i, first in enumerate(range(0, DN_HEADS, DN_BLOCK_HEADS)):
        y = _pdot(qk[i * DN_BLOCK:(i + 1) * DN_BLOCK], jnp.concatenate(v_new[first:first + DN_BLOCK_HEADS], axis=0), ((1,), (0,)))
        local += [y[p * C:(p + 1) * C] for p in range(DN_BLOCK_HEADS)]
    o = _cat([x[C:] + y for x, y in zip(ws, local)])
    states = [state[sl, :] * egl[0:1, sl] + _pdot(kd[:, sl], vn, ((0,), (0,))) for sl, vn in zip(heads, v_new)]
    return o, jnp.concatenate(states, axis=0)


def _make_delta_rule(name):
    W = DN_HEADS * LANES
    QK = DN_HEADS * DN_CHUNK

    def local_call(ins, cts=None):
        S = ins[0].shape[0]
        n = S // DN_CHUNK
        per = math.gcd(DN_LOCAL_CHUNKS, n)
        row = pl.BlockSpec((per * DN_CHUNK, W), lambda i: (i, 0))
        qkb = pl.BlockSpec((per * QK, DN_BLOCK), lambda i: (i, 0))
        eg = pl.BlockSpec((per, 8, W), lambda i: (i, 0, 0))
        rows = lambda j: slice(j * DN_CHUNK, (j + 1) * DN_CHUNK)
        qk_rows = lambda j: slice(j * QK, (j + 1) * QK)
        out_rows = [rows, rows, rows, rows, qk_rows]

        if cts is None:
            def body(*refs):
                for j in range(per):
                    res = _dn_local(*[r[rows(j), :] for r in refs[:5]])
                    for o_r, o, at_ in zip(refs[5:10], res[:5], out_rows):
                        o_r[at_(j), :] = o
                    refs[10][j] = res[5]
            return _pcall(body, name=f"{name}_local_fwd", grid=(n // per,), in_specs=[row] * 5, out_specs=[row] * 4 + [qkb, eg],
                          out_shape=[jax.ShapeDtypeStruct((S, W), F32)] * 4
                          + [jax.ShapeDtypeStruct((n * QK, DN_BLOCK), F32), jax.ShapeDtypeStruct((n, 8, W), F32)],
                          compiler_params=_params(("parallel",)))(*ins)

        def body(*refs):
            for j in range(per):
                _, vjp = jax.vjp(_dn_local, *[r[rows(j), :] for r in refs[:5]])
                grads = vjp(tuple(r[at_(j), :] for r, at_ in zip(refs[5:10], out_rows)) + (refs[10][j],))
                for o_r, o in zip(refs[11:], grads):
                    o_r[rows(j), :] = o
        return _pcall(body, name=f"{name}_local_bwd", grid=(n // per,), in_specs=[row] * 9 + [qkb, eg], out_specs=[row] * 5,
                      out_shape=[jax.ShapeDtypeStruct((S, W), F32)] * 5, compiler_params=_params(("parallel",)))(*ins, *cts)

    def scan_call(ins, saved=None, do=None):
        S = ins[0].shape[0]
        n = S // DN_CHUNK
        at = (lambda i: i) if do is None else (lambda i: n - 1 - i)
        row = pl.BlockSpec((DN_CHUNK, W), lambda i: (at(i), 0))
        qkb = pl.BlockSpec((QK, DN_BLOCK), lambda i: (at(i), 0))
        eg = pl.BlockSpec((None, 8, W), lambda i: (at(i), 0, 0))
        st = pl.BlockSpec((None, W, LANES), lambda i: (at(i), 0, 0))

        if do is None:
            def body(*refs):
                o_r, st_r, s_s = refs[6:]

                @pl.when(pl.program_id(0) == 0)
                def _():
                    s_s[...] = jnp.zeros_like(s_s)
                st_r[...] = s_s[...]
                o_r[...], s_s[...] = _dn_scan(*[r[...] for r in refs[:6]], s_s[...])
            return _pcall(body, name=f"{name}_scan_fwd", grid=(n,), in_specs=[row] * 4 + [qkb, eg], out_specs=[row, st],
                          out_shape=[jax.ShapeDtypeStruct((S, W), F32), jax.ShapeDtypeStruct((n, W, LANES), F32)],
                          scratch_shapes=[pltpu.VMEM((W, LANES), F32)], compiler_params=_params(("arbitrary",)))(*ins)

        def body(*refs):
            st_r, do_r = refs[6:8]
            outs, ds_s = refs[8:14], refs[14]

            @pl.when(pl.program_id(0) == 0)
            def _():
                ds_s[...] = jnp.zeros_like(ds_s)
            _, vjp = jax.vjp(_dn_scan, *[r[...] for r in refs[:6]], st_r[...])
            *grads, ds = vjp((do_r[...], ds_s[...]))
            for o_r, gval in zip(outs, grads):
                o_r[...] = gval
            ds_s[...] = ds
        return _pcall(body, name=f"{name}_scan_bwd", grid=(n,), in_specs=[row] * 4 + [qkb, eg, st, row], out_specs=[row] * 4 + [qkb, eg],
                      out_shape=[jax.ShapeDtypeStruct((S, W), F32)] * 4
                      + [jax.ShapeDtypeStruct((n * QK, DN_BLOCK), F32), jax.ShapeDtypeStruct((n, 8, W), F32)],
                      scratch_shapes=[pltpu.VMEM((W, LANES), F32)], compiler_params=_params(("arbitrary",)))(*ins, saved, do)

    @jax.custom_vjp
    def local(q, k, v, g, b):
        return tuple(local_call((q, k, v, g, b)))

    local.defvjp(lambda *a: (local(*a), a), lambda res, cts: tuple(local_call(res, tuple(cts))))

    @jax.custom_vjp
    def scan(u, w, qe, kd, qk, egl):
        return scan_call((u, w, qe, kd, qk, egl))[0]

    def scan_fwd(*a):
        o, states = scan_call(a)
        return o, (a, states)

    scan.defvjp(scan_fwd, lambda res, do: tuple(scan_call(res[0], res[1], do)))
    return lambda q, k, v, g, b: scan(*local(q, k, v, g, b))


def _loss_call(y, target):
    S, D = y.shape
    t = min(S, 512)
    n = S // t
    row = pl.BlockSpec((t, D), lambda i: (i, 0))

    def body(y_r, t_r, loss_r, dy_r, acc_s):
        i = pl.program_id(0)

        @pl.when(i == 0)
        def _():
            acc_s[...] = jnp.zeros_like(acc_s)
        err = y_r[...] - t_r[...]
        dy_r[...] = err * (1.0 / D)
        acc_s[...] += jnp.sum(err * err, axis=0, keepdims=True)

        @pl.when(i == n - 1)
        def _():
            loss_r[...] = jnp.broadcast_to(jnp.sum(acc_s[...], axis=1, keepdims=True) * (0.5 / D), loss_r.shape)

    return _pcall(body, name="loss_head", grid=(n,), in_specs=[row, row],
                  out_specs=[pl.BlockSpec((8, LANES), lambda i: (0, 0)), row],
                  out_shape=[jax.ShapeDtypeStruct((8, LANES), F32), jax.ShapeDtypeStruct((S, D), F32)],
                  scratch_shapes=[pltpu.VMEM((1, D), F32)], compiler_params=_params(("arbitrary",)))(y, target)


def _adamw_call(name, parts, w, m, v, rows=128):
    L, R, C = w.shape
    assert len(parts) == L
    t = _tile(R, rows, 8)
    row = pl.BlockSpec((None, t, C), lambda l, i: (l, i, 0))
    part = lambda k: pl.BlockSpec((N_DEV, t, C), lambda l, i: (0, jnp.where(l == k, i, 0), 0))

    def body(*refs):
        p_refs, (w_r, m_r, v_r, g_r, d_r, nm_r, nv_r) = refs[:L], refs[L:]

        def update(p_r):
            g = p_r[0].astype(F32)
            for s in range(1, N_DEV):
                g = g + p_r[s].astype(F32)
            m_new = ADAM_B1 * m_r[...] + (1.0 - ADAM_B1) * g
            v_new = ADAM_B2 * v_r[...] + (1.0 - ADAM_B2) * (g * g)
            m_hat = m_new / (1.0 - ADAM_B1 ** ADAM_STEP)
            v_hat = v_new / (1.0 - ADAM_B2 ** ADAM_STEP)
            g_r[...] = g
            d_r[...] = -ADAM_LR * (m_hat / (jnp.sqrt(v_hat) + ADAM_EPS) + ADAM_WD * w_r[...])
            nm_r[...] = m_new
            nv_r[...] = v_new

        for k in range(L):
            pl.when(pl.program_id(0) == k)(functools.partial(update, p_refs[k]))

    return _pcall(body, name=name, grid=(L, R // t), in_specs=[part(k) for k in range(L)] + [row, row, row],
                  out_specs=[row] * 4, out_shape=[jax.ShapeDtypeStruct((L, R, C), F32)] * 4,
                  compiler_params=_params(("arbitrary", "arbitrary")))(*parts, w, m, v)


def _my_place():
    x, y, c = lax.axis_index("x"), lax.axis_index("y"), lax.axis_index("c")
    return x, y, c


def _index(x, y, c):
    return 4 * x + 2 * y + c


def _all_gather(vs):
    n = len(vs)

    def body(*refs):
        v_refs, out_refs = refs[:n], refs[n:2 * n]
        send_sems, recv_sems, local_sems = refs[2 * n:]
        x, y, c = _my_place()
        me, sibling = (x, y, c), (x, y, 1 - c)
        chips = [(1 - x, y), (x, 1 - y), (1 - x, 1 - y)]

        def copy(a, k, block, to, src=None):
            rows = out_refs[a].at[_index(*block)]
            return pltpu.make_async_remote_copy(src_ref=rows if src is None else src, dst_ref=rows, send_sem=send_sems.at[a, k],
                                                recv_sem=recv_sems.at[a, k], device_id=to, device_id_type=MESH)

        mine = [pltpu.make_async_copy(v_refs[a], out_refs[a].at[_index(*me)], local_sems.at[a]) for a in range(n)]
        first, passed = [], []
        for a in range(n):
            mine[a].start()
            first += [copy(a, 0, me, sibling, src=v_refs[a])]
            first += [copy(a, 1 + j, me, (*chip, c), src=v_refs[a]) for j, chip in enumerate(chips)]
        for cp in first:
            cp.start()
        for j, chip in enumerate(chips):
            for a in range(n):
                copy(a, 1 + j, (*chip, c), me).wait_recv()
                passed.append(copy(a, 4 + j, (*chip, c), sibling))
                passed[-1].start()
        for a in range(n):
            copy(a, 0, sibling, me).wait_recv()
            for j, chip in enumerate(chips):
                copy(a, 4 + j, (*chip, 1 - c), me).wait_recv()
        for cp in first + passed:
            cp.wait_send()
        for a in range(n):
            mine[a].wait()

    any_ = pl.BlockSpec(memory_space=pl.ANY)
    return _pcall(body, name="gather_weights", in_specs=[any_] * n, out_specs=[any_] * n,
                  out_shape=[jax.ShapeDtypeStruct((N_DEV,) + v.shape, v.dtype) for v in vs],
                  scratch_shapes=[pltpu.SemaphoreType.DMA((n, 7)), pltpu.SemaphoreType.DMA((n, 7)), pltpu.SemaphoreType.DMA((n,))])(*vs)


def _all_to_all(vs):
    n = len(vs)

    def body(*refs):
        v_refs, out_refs = refs[:n], refs[n:2 * n]
        send_sems, recv_sems, local_sems = refs[2 * n:]
        x, y, c = _my_place()
        me = _index(x, y, c)
        mine = [pltpu.make_async_copy(v_refs[a].at[me], out_refs[a].at[me], local_sems.at[a]) for a in range(n)]
        copies = []
        for a in range(n):
            mine[a].start()
        for k in range(1, N_DEV):
            px = 1 - x if k & 4 else x
            py = 1 - y if k & 2 else y
            pc = 1 - c if k & 1 else c
            for a in range(n):
                cp = pltpu.make_async_remote_copy(src_ref=v_refs[a].at[_index(px, py, pc)], dst_ref=out_refs[a].at[me],
                                                  send_sem=send_sems.at[a, k - 1], recv_sem=recv_sems.at[a, k - 1],
                                                  device_id=(px, py, pc), device_id_type=MESH)
                cp.start()
                copies.append(cp)
        for cp in copies:
            cp.wait()
        for a in range(n):
            mine[a].wait()

    any_ = pl.BlockSpec(memory_space=pl.ANY)
    return _pcall(body, name="exchange_grads", in_specs=[any_] * n, out_specs=[any_] * n,
                  out_shape=[jax.ShapeDtypeStruct(v.shape, v.dtype) for v in vs],
                  scratch_shapes=[pltpu.SemaphoreType.DMA((n, 7)), pltpu.SemaphoreType.DMA((n, 7)), pltpu.SemaphoreType.DMA((n,))])(*vs)


_HBM = pl.BlockSpec(memory_space=pltpu.HBM)
_SEM = pl.BlockSpec(memory_space=pltpu.SEMAPHORE)
_EFFECT = pltpu.SideEffectType.DATAFLOW_SIDE_EFFECTING


def _direct_copies(gather, v_refs, land_refs, send_sems, recv_sems, local_sems):
    x, y, c = _my_place()
    me = _index(x, y, c)
    local, remote = [], []
    for a, (v_ref, land_ref) in enumerate(zip(v_refs, land_refs)):
        local.append(pltpu.make_async_copy(v_ref if gather else v_ref.at[me], land_ref.at[me], local_sems.at[a]))
    for k in range(1, N_DEV):
        px = 1 - x if k & 4 else x
        py = 1 - y if k & 2 else y
        pc = 1 - c if k & 1 else c
        for a, (v_ref, land_ref) in enumerate(zip(v_refs, land_refs)):
            sem = a * (N_DEV - 1) + k - 1
            remote.append(pltpu.make_async_remote_copy(
                src_ref=v_ref if gather else v_ref.at[_index(px, py, pc)], dst_ref=land_ref.at[me], send_sem=send_sems.at[sem],
                recv_sem=recv_sems.at[sem], device_id=(px, py, pc), device_id_type=MESH))
    return local, remote


def _exchange_start(name, vs, gather, thru):
    n = len(vs)
    lands = [lax.empty((N_DEV,) + v.shape if gather else v.shape, v.dtype) for v in vs]

    def body(*refs):
        v_refs, land_refs = refs[:n], refs[n:2 * n]
        send_sems, recv_sems, local_sems = refs[2 * n + 1:2 * n + 4]
        local, remote = _direct_copies(gather, v_refs, land_refs, send_sems, recv_sems, local_sems)
        for cp in local + remote:
            cp.start()

    hbm = lambda a: pltpu.HBM(a.shape, a.dtype)
    res = _pcall(body, name=name,
                 out_shape=(pltpu.SemaphoreType.DMA((n * (N_DEV - 1),)), pltpu.SemaphoreType.DMA((n * (N_DEV - 1),)), pltpu.SemaphoreType.DMA((n,)),
                            *[hbm(a) for a in (*vs, *lands, thru)]),
                 in_specs=[_HBM] * (2 * n + 1), out_specs=(_SEM, _SEM, _SEM, *[_HBM] * (2 * n + 1)),
                 input_output_aliases={i: 3 + i for i in range(2 * n + 1)},
                 compiler_params=pltpu.CompilerParams(has_side_effects=_EFFECT))(
        *[pltpu.with_memory_space_constraint(a, pltpu.HBM) for a in (*vs, *lands, thru)])
    return (gather, res[:3], res[3:3 + n], res[3 + n:3 + 2 * n]), res[3 + 2 * n]


def _exchange_wait(name, started, after):
    gather, sems, vs, lands = started
    n = len(vs)

    def body(*refs):
        v_refs, land_refs = refs[:n], refs[n:2 * n]
        send_sems, recv_sems, local_sems = refs[2 * n:2 * n + 3]
        local, remote = _direct_copies(gather, v_refs, land_refs, send_sems, recv_sems, local_sems)
        for cp in local:
            cp.wait()
        for cp in remote:
            cp.wait_send()
            cp.wait_recv()

    hbm = lambda a: pltpu.HBM(a.shape, a.dtype)
    res = _pcall(body, name=name, out_shape=tuple(hbm(a) for a in (*vs, *lands)),
                 in_specs=[_HBM] * (2 * n) + [_SEM] * 3 + [pl.BlockSpec(memory_space=pl.ANY)], out_specs=tuple([_HBM] * (2 * n)),
                 input_output_aliases={i: i for i in range(2 * n)},
                 compiler_params=pltpu.CompilerParams(has_side_effects=_EFFECT))(*vs, *lands, *sems, after)
    return list(res[n:])


W_IN_SHARD = IN_WIDTH // N_DEV
SEG_ORDER = ("q_lat", "c_kv", "k_pe", "z_a", "dn_qkv", "dn_ab", "z_b", "dil_qkv", "z_c", "gate")
SEG_WIDTH = (384, 256, LANES, 512, 1536, LANES, 512, 4608, 512, 3072)


def _w_in_plan():
    plan = []

    def add(seg, c0, c1, dst):
        while c0 < c1:
            d = c0 // W_IN_SHARD
            e = min(c1, (d + 1) * W_IN_SHARD)
            plan.append((seg, dst, d, c0 - d * W_IN_SHARD, e - c0))
            dst += e - c0
            c0 = e

    half = MLA_ROPE // 2
    for i, name in enumerate(SEG_ORDER):
        if name == "k_pe":
            o = _SEG["k_pe"][0]
            add(i, o, o + half, 0)
            add(i, o + half, o + 2 * half, LANES // 2)
        elif name == "dn_ab":
            o = _SEG["dn_a"][0]
            add(i, o, o + 2 * DN_HEADS, 0)
        else:
            o, w = _SEG[name]
            add(i, o, o + w, 0)
    return plan


def _make_w_in_segments(name):
    plan = _w_in_plan()
    nseg = len(SEG_ORDER)
    t = 256

    def fwd_call(g):
        L = g.shape[1]

        def body(g_ref, *o_refs):
            for i in (SEG_ORDER.index("k_pe"), SEG_ORDER.index("dn_ab")):
                o_refs[i][...] = jnp.zeros_like(o_refs[i])
            for seg, dst, d, src, n in plan:
                o_refs[seg][:, dst:dst + n] = g_ref[d, :, src:src + n]

        return _pcall(body, name=f"{name}_fwd", grid=(L, D_MODEL // t),
                      in_specs=[pl.BlockSpec((N_DEV, None, t, W_IN_SHARD), lambda l, i: (0, l, i, 0))],
                      out_specs=[pl.BlockSpec((None, t, w), lambda l, i: (l, i, 0)) for w in SEG_WIDTH],
                      out_shape=[jax.ShapeDtypeStruct((L, D_MODEL, w), g.dtype) for w in SEG_WIDTH],
                      compiler_params=_params(("parallel", "parallel")))(g)

    def bwd_call(ds):
        L = ds[0].shape[0]

        def body(*refs):
            d_refs, g_ref = refs[:nseg], refs[nseg]
            for seg, dst, d, src, n in plan:
                g_ref[d, :, src:src + n] = d_refs[seg][:, dst:dst + n]

        return _pcall(body, name=f"{name}_bwd", grid=(L, D_MODEL // t),
                      in_specs=[pl.BlockSpec((None, t, w), lambda l, i: (l, i, 0)) for w in SEG_WIDTH],
                      out_specs=pl.BlockSpec((N_DEV, None, t, W_IN_SHARD), lambda l, i: (0, l, i, 0)),
                      out_shape=jax.ShapeDtypeStruct((N_DEV, L, D_MODEL, W_IN_SHARD), ds[0].dtype),
                      compiler_params=_params(("parallel", "parallel")))(*ds)

    @jax.custom_vjp
    def op(g):
        return tuple(fwd_call(g))

    op.defvjp(lambda g: (op(g), None), lambda _, ds: (bwd_call(tuple(ds)),))
    return op


def _pe_pad(a):
    h = MLA_ROPE // 2
    z = jnp.zeros(a.shape[:-1] + (h,), a.dtype)
    return jnp.concatenate([a[..., :h], z, a[..., h:], z], axis=-1)


def _layer_norm(tag, x, norm_g):
    return _make_rowwise(f"{tag}_norm", _f_norm, 512)((x,), (), (norm_g[None, :],), ())[0]


def _layer(tag, x, tables, W):
    h = _layer_norm(tag, x, W["norm_g"])
    return _layer_tail(tag, x, _make_multi_linear(f"{tag}_inproj", 10)(h, W["w_in_segments"]), tables, W)


def _layer_tail(tag, x, segments, tables, W):
    cos_p, sin_p, cos_h, sin_h = tables
    row = lambda a: a[None, :]
    q_lat, c_kv, kpe, z_a, dn_qkv, ab, z_b, dil_qkv, z_c, gl = segments

    qn_lat, ckvn, kp = _make_rowwise(f"{tag}_mla_a", _f_mla_a, 512)(
        (q_lat, c_kv, kpe), (cos_p, sin_p),
        (row(W["mla_q_a_norm_g"]), row(W["mla_kv_a_norm_g"]), row(_pe_pad(W["mla_k_norm_g"][LANES:]))), ())
    wq = W["mla_w_q_b"].reshape(MLA_Q_RANK, MLA_HEADS, MLA_QK)
    wq = jnp.concatenate([wq[:, :, :LANES].reshape(MLA_Q_RANK, -1), _pe_pad(wq[:, :, LANES:]).reshape(MLA_Q_RANK, -1)], axis=1)
    wkv = W["mla_w_kv_b"].reshape(MLA_KV_RANK, MLA_HEADS, 2 * LANES)
    (q8,) = _make_multi_linear(f"{tag}_qb", 1)(qn_lat, (wq,))
    kn_raw, v_mla = _make_multi_linear(f"{tag}_kvb", 2)(
        ckvn, (wkv[:, :, :LANES].reshape(MLA_KV_RANK, -1), wkv[:, :, LANES:].reshape(MLA_KV_RANK, -1)))
    qn, qp, kn = _make_rowwise(f"{tag}_mla_b", _f_mla_b, 512)(
        (q8, kn_raw), (cos_p, sin_p),
        (row(W["mla_q_norm_g"][:LANES]), row(_pe_pad(W["mla_q_norm_g"][LANES:])), row(W["mla_k_norm_g"][:LANES])), ())
    y_a = _make_mla_attn(f"{tag}_mla")(qn, qp, kn, kp, v_mla)

    mixed = _make_conv(f"{tag}_conv")(dn_qkv, W["dn_conv_w"])
    lane_head = jnp.arange(DN_HEADS * LANES) // LANES
    e_a = (jnp.arange(LANES)[:, None] == lane_head[None, :]).astype(F32)
    e_b = (jnp.arange(LANES)[:, None] == lane_head[None, :] + DN_HEADS).astype(F32)
    q_dn, k_dn, v_dn, g_dn, b_dn = _make_rowwise(f"{tag}_dn_pre", _f_dn_pre, 512)(
        (mixed, ab), (), (row(jnp.repeat(W["dn_a_log"], LANES)), row(jnp.repeat(W["dn_dt_bias"], LANES))), (e_a, e_b))
    o_dn = _make_delta_rule(f"{tag}_dn")(q_dn, k_dn, v_dn, g_dn, b_dn)

    qkv_dil = _make_rowwise(f"{tag}_dil_pre", _f_dil_pre, 256)(
        (dil_qkv,), (cos_h, sin_h), (row(W["dil_q_norm_g"]), row(W["dil_k_norm_g"])), ())
    n_groups = len(DIL_DILATIONS)
    o_lse = [_make_dil_attn(f"{tag}_dil{g}", d, DIL_TILE_ROWS[g])(qkv_dil[g], qkv_dil[n_groups + g], qkv_dil[2 * n_groups + g])
             for g, d in enumerate(DIL_DILATIONS)]

    ya, yb, yc = _make_rowwise(f"{tag}_merge_a", _f_merge_a, 256)(
        (y_a, z_a, o_dn, z_b, *[o for o, _ in o_lse], *[l for _, l in o_lse], z_c), (), (row(W["dn_out_norm_g"]),), ())
    (b0,) = _make_multi_linear(f"{tag}_br0", 1)(ya, (W["w_branch"][0],))
    (b1,) = _make_multi_linear(f"{tag}_br1", 1)(yb, (W["w_branch"][1],))
    (b2,) = _make_multi_linear(f"{tag}_br2", 1)(yc, (W["w_branch"][2],))
    (mix,) = _make_rowwise(f"{tag}_merge_b", _f_merge_b, 256)((b0, b1, b2, gl), (), (), ())
    return _make_resid_linear(f"{tag}_out")(x, mix, W["w_out"])


SHARDED = (("w_in", (D_MODEL, W_IN_SHARD)), ("mla_w_q_b", (MLA_Q_RANK, MLA_HEADS * MLA_QK // N_DEV)),
           ("mla_w_kv_b", (MLA_KV_RANK, MLA_HEADS * 2 * LANES // N_DEV)), ("w_branch", (3 * BRANCH_W, D_MODEL // N_DEV)),
           ("w_out", (D_MODEL // N_DEV, D_MODEL)), ("dn_conv_w", (DN_CONV, 3 * DN_HEADS * LANES // N_DEV)))
SMALL = (("norm_g", D_MODEL), ("mla_q_a_norm_g", MLA_Q_RANK), ("mla_kv_a_norm_g", MLA_KV_RANK), ("mla_q_norm_g", MLA_QK),
         ("mla_k_norm_g", MLA_QK), ("dn_a_log", DN_HEADS), ("dn_dt_bias", DN_HEADS), ("dn_out_norm_g", LANES),
         ("dil_q_norm_g", LANES), ("dil_k_norm_g", LANES))
WEIGHTS = ("norm_g", "w_in", "mla_q_a_norm_g", "mla_w_q_b", "mla_kv_a_norm_g", "mla_w_kv_b", "mla_q_norm_g", "mla_k_norm_g",
           "dn_conv_w", "dn_a_log", "dn_dt_bias", "dn_out_norm_g", "dil_q_norm_g", "dil_k_norm_g", "w_branch", "w_out")


def _round_up(n, m):
    return -(-n // m) * m


def _pack_vectors(pieces):
    return jnp.concatenate([jnp.pad(p, (0, _round_up(p.shape[0], LANES) - p.shape[0])) for p in pieces]).reshape(-1, LANES)


def _unpack_vectors(flat, sizes):
    out, off = [], 0
    flat = flat.reshape(-1)
    for n in sizes:
        out.append(flat[off:off + n])
        off += _round_up(n, LANES)
    return out


def _whole_weights(g, small):
    W = dict(small)
    W["mla_w_q_b"] = g["mla_w_q_b"].transpose(1, 0, 2).reshape(MLA_Q_RANK, -1)
    W["mla_w_kv_b"] = g["mla_w_kv_b"].transpose(1, 0, 2).reshape(MLA_KV_RANK, -1)
    W["w_branch"] = g["w_branch"].reshape(N_DEV, 3, BRANCH_W, -1).transpose(1, 2, 0, 3).reshape(3, BRANCH_W, D_MODEL)
    W["w_out"] = g["w_out"].reshape(D_MODEL, D_MODEL)
    W["dn_conv_w"] = g["dn_conv_w"].transpose(1, 0, 2).reshape(DN_CONV, -1)
    return W


def kernel(x, positions, norm_g, w_in, mla_q_a_norm_g, mla_w_q_b, mla_kv_a_norm_g, mla_w_kv_b, mla_q_norm_g, mla_k_norm_g, dn_conv_w, dn_a_log, dn_dt_bias, dn_out_norm_g, dil_q_norm_g, dil_k_norm_g, w_branch, w_out, loss_target, m_norm_g, m_w_in, m_mla_q_a_norm_g, m_mla_w_q_b, m_mla_kv_a_norm_g, m_mla_w_kv_b, m_mla_q_norm_g, m_mla_k_norm_g, m_dn_conv_w, m_dn_a_log, m_dn_dt_bias, m_dn_out_norm_g, m_dil_q_norm_g, m_dil_k_norm_g, m_w_branch, m_w_out, v_norm_g, v_w_in, v_mla_q_a_norm_g, v_mla_w_q_b, v_mla_kv_a_norm_g, v_mla_w_kv_b, v_mla_q_norm_g, v_mla_k_norm_g, v_dn_conv_w, v_dn_a_log, v_dn_dt_bias, v_dn_out_norm_g, v_dil_q_norm_g, v_dil_k_norm_g, v_w_branch, v_w_out):
    w = dict(norm_g=norm_g, w_in=w_in, mla_q_a_norm_g=mla_q_a_norm_g, mla_w_q_b=mla_w_q_b, mla_kv_a_norm_g=mla_kv_a_norm_g,
             mla_w_kv_b=mla_w_kv_b, mla_q_norm_g=mla_q_norm_g, mla_k_norm_g=mla_k_norm_g, dn_conv_w=dn_conv_w, dn_a_log=dn_a_log,
             dn_dt_bias=dn_dt_bias, dn_out_norm_g=dn_out_norm_g, dil_q_norm_g=dil_q_norm_g, dil_k_norm_g=dil_k_norm_g,
             w_branch=w_branch, w_out=w_out)
    m = dict(norm_g=m_norm_g, w_in=m_w_in, mla_q_a_norm_g=m_mla_q_a_norm_g, mla_w_q_b=m_mla_w_q_b, mla_kv_a_norm_g=m_mla_kv_a_norm_g,
             mla_w_kv_b=m_mla_w_kv_b, mla_q_norm_g=m_mla_q_norm_g, mla_k_norm_g=m_mla_k_norm_g, dn_conv_w=m_dn_conv_w,
             dn_a_log=m_dn_a_log, dn_dt_bias=m_dn_dt_bias, dn_out_norm_g=m_dn_out_norm_g, dil_q_norm_g=m_dil_q_norm_g,
             dil_k_norm_g=m_dil_k_norm_g, w_branch=m_w_branch, w_out=m_w_out)
    v = dict(norm_g=v_norm_g, w_in=v_w_in, mla_q_a_norm_g=v_mla_q_a_norm_g, mla_w_q_b=v_mla_w_q_b, mla_kv_a_norm_g=v_mla_kv_a_norm_g,
             mla_w_kv_b=v_mla_w_kv_b, mla_q_norm_g=v_mla_q_norm_g, mla_k_norm_g=v_mla_k_norm_g, dn_conv_w=v_dn_conv_w,
             dn_a_log=v_dn_a_log, dn_dt_bias=v_dn_dt_bias, dn_out_norm_g=v_dn_out_norm_g, dil_q_norm_g=v_dil_q_norm_g,
             dil_k_norm_g=v_dil_k_norm_g, w_branch=v_w_branch, w_out=v_w_out)
    x2, target = x[0], loss_target[0]
    pos = positions[0][:, None]

    names = [n for n, _ in SHARDED]
    view = lambda t, n, s: t[n].reshape((DEPTH,) + s)
    shards = [[(view(w, n, s) if n == "dn_conv_w" else view(w, n, s).astype(BF16))[l] for n, s in SHARDED] for l in range(DEPTH)]
    small = [{n: w[n][l] for n, _ in SMALL} for l in range(DEPTH)]
    gathered0 = dict(zip(names, _all_gather(shards[0])))
    gathering1, pos = _exchange_start("gather_layer1_start", shards[1], True, pos)
    tables = _rope_tables(pos, _rope_consts())

    def layer(l, g, small_l, x_l):
        tag = f"l{l}"
        seg_op = _make_w_in_segments(f"{tag}_w_in_segments")
        w_segs, vjp_segs = jax.vjp(lambda gw: tuple(s[0] for s in seg_op(gw[:, None])), g["w_in"])
        h, vjp_norm = jax.vjp(lambda x_, ng: _layer_norm(tag, x_, ng), x_l, small_l["norm_g"])
        segs = tuple(_mm(f"{tag}_inproj_fwd{i}", h, w_, "nn", **INPROJ_TILES["nn"]) for i, w_ in enumerate(w_segs))
        rest_g = {n: a for n, a in g.items() if n != "w_in"}
        rest_s = {n: a for n, a in small_l.items() if n != "norm_g"}
        y, vjp_tail = jax.vjp(lambda sg, x_, gg, ss: _layer_tail(tag, x_, sg, tables, _whole_weights(gg, ss)), segs, x_l, rest_g, rest_s)

        def backward(dy):
            dsegs, dx_skip, d_rest_g, d_rest_s = vjp_tail(dy)
            dws = tuple(_mm(f"{tag}_inproj_dw{i}", h, d, "tn", out_dtype=w_.dtype, **INPROJ_TILES["tn"])
                        for i, (w_, d) in enumerate(zip(w_segs, dsegs)))
            dg = dict(d_rest_g, w_in=vjp_segs(dws)[0])
            exchanging, first = _exchange_start(f"exchange_{tag}_start", [dg[n] for n in names], False, dsegs[0])
            dh = None
            for i, (w_, d) in enumerate(zip(w_segs, (first,) + tuple(dsegs[1:]))):
                dh = _mm(f"{tag}_inproj_dh{i}", d, w_, "nt", acc=dh, out_dtype=h.dtype if i == len(w_segs) - 1 else F32,
                         **INPROJ_TILES["nt"])
            dx_norm, d_norm_g = vjp_norm(dh)
            return exchanging, dx_skip + dx_norm, dict(d_rest_s, norm_g=d_norm_g)
        return y, backward

    y0, backward0 = layer(0, gathered0, small[0], x2)
    gathered1 = dict(zip(names, _exchange_wait("gather_layer1_wait", gathering1, y0)))
    y1, backward1 = layer(1, gathered1, small[1], y0)
    loss_splat, dy = _loss_call(y1, target)
    loss = lax.psum(loss_splat[0, 0], ("x", "y", "c"))
    exchanging1, d_y0, g_small1 = backward1(dy)
    exchanging0, g_x, g_small0 = backward0(d_y0)

    g_small = (g_small0, g_small1)
    sizes = [k for _ in range(DEPTH) for _, k in SMALL]
    g_vec = _pack_vectors([g_small[l][n] for l in range(DEPTH) for n, _ in SMALL])
    (parts_vec,) = _all_to_all([jnp.broadcast_to(g_vec[None], (N_DEV,) + g_vec.shape)])
    parts1 = _exchange_wait("exchange_l1_wait", exchanging1, parts_vec)
    parts0 = _exchange_wait("exchange_l0_wait", exchanging0, parts_vec)

    vec = lambda t: _pack_vectors([t[n][l] for l in range(DEPTH) for n, _ in SMALL])[None]
    outs = {}
    for i, (n, s) in enumerate(SHARDED):
        res = _adamw_call(f"adamw_{n}", [parts0[i], parts1[i]], view(w, n, s), view(m, n, s), view(v, n, s))
        outs[n] = [o.reshape(w[n].shape) for o in res]
    vec_outs = [_unpack_vectors(o, sizes) for o in _adamw_call("adamw_vectors", [parts_vec], vec(w), vec(m), vec(v))]
    for i, (n, _) in enumerate(SMALL):
        outs[n] = [jnp.stack([o[l * len(SMALL) + i] for l in range(DEPTH)]) for o in vec_outs]
    return (loss, g_x[None], *[outs[n][k] for k in range(4) for n in WEIGHTS])
```

```python
import functools
import math

import jax
import jax.numpy as jnp
from jax import lax
from jax.experimental import pallas as pl
from jax.experimental.pallas import tpu as pltpu

F32 = jnp.float32
BF16 = jnp.bfloat16
HI = lax.Precision.HIGHEST
MESH = pl.DeviceIdType.MESH

N_DEV = 8
D_MODEL = 1024
DEPTH = 2
RMS_EPS = 1e-6
ROPE_THETA = 10000.0
LANES = 128
MLA_HEADS = 4
MLA_ROPE = 64
MLA_QK = 192
MLA_Q_RANK = 384
MLA_KV_RANK = 256
DN_HEADS = 4
DN_CHUNK = 64
DN_CONV = 4
DIL_HEADS = 12
DIL_GROUP_HEADS = 4
DIL_DILATIONS = (1, 4, 16)
DIL_BLOCK = 128
BRANCH_W = 512
IN_WIDTH = 11464
NEG = -1e30
VMEM_LIMIT = 56 * 1024 * 1024

ADAM_LR, ADAM_B1, ADAM_B2, ADAM_EPS, ADAM_WD, ADAM_STEP = 0.001, 0.9, 0.999, 1e-08, 0.01, 10

_SEG = {}
_off = 0
for _n, _w in (("q_lat", 384), ("c_kv", 256), ("k_pe", 64), ("z_a", 512), ("dn_qkv", 1536), ("dn_a", 4), ("dn_b", 4),
               ("z_b", 512), ("dil_qkv", 4608), ("z_c", 512), ("gate", 3072)):
    _SEG[_n] = (_off, _w)
    _off += _w
assert _off == IN_WIDTH


def _pcall(body, **kw):
    return pl.pallas_call(body, **kw)


def _params(sem=None):
    return pltpu.CompilerParams(dimension_semantics=sem, vmem_limit_bytes=VMEM_LIMIT)


def _tile(n, target, mult):
    t = (min(n, target) // mult) * mult
    while t >= mult:
        if n % t == 0:
            return t
        t -= mult
    return n


def _mm(name, a, b, mode, out_dtype=F32, acc=None, tm=1024, tn=512, tk=1024):
    if mode == "nn":
        (M, K), (_, N) = a.shape, b.shape
    elif mode == "nt":
        (M, K), (N, _) = a.shape, b.shape
    else:
        (K, M), (_, N) = a.shape, b.shape
    tm, tn, tk = _tile(M, tm, LANES), _tile(N, tn, LANES), _tile(K, tk, LANES)
    nk = K // tk
    dims = {"nn": (((1,), (0,)), ((), ())), "nt": (((1,), (1,)), ((), ())), "tn": (((0,), (0,)), ((), ()))}[mode]
    a_spec = pl.BlockSpec((tk, tm), lambda i, j, k: (k, i)) if mode == "tn" else pl.BlockSpec((tm, tk), lambda i, j, k: (i, k))
    b_spec = pl.BlockSpec((tn, tk), lambda i, j, k: (j, k)) if mode == "nt" else pl.BlockSpec((tk, tn), lambda i, j, k: (k, j))
    o_spec = pl.BlockSpec((tm, tn), lambda i, j, k: (i, j))
    has_acc = acc is not None

    def body(*refs):
        a_ref, b_ref = refs[:2]
        c_ref = refs[2] if has_acc else None
        o_ref = refs[3] if has_acc else refs[2]
        prod = lax.dot_general(a_ref[...].astype(BF16), b_ref[...].astype(BF16), dims, preferred_element_type=F32)
        if nk == 1:
            o_ref[...] = (prod + c_ref[...].astype(F32) if has_acc else prod).astype(out_dtype)
            return
        acc_ref = refs[-1]
        k = pl.program_id(2)

        @pl.when(k == 0)
        def _():
            acc_ref[...] = prod + c_ref[...].astype(F32) if has_acc else prod

        @pl.when(k > 0)
        def _():
            acc_ref[...] += prod

        @pl.when(k == nk - 1)
        def _():
            o_ref[...] = acc_ref[...].astype(out_dtype)

    ins = [a, b] + ([acc] if has_acc else [])
    in_specs = [a_spec, b_spec] + ([o_spec] if has_acc else [])
    return _pcall(body, name=name, grid=(M // tm, N // tn, nk), in_specs=in_specs, out_specs=o_spec,
                  out_shape=jax.ShapeDtypeStruct((M, N), out_dtype), scratch_shapes=[pltpu.VMEM((tm, tn), F32)] if nk > 1 else [],
                  compiler_params=_params(("parallel", "parallel", "arbitrary")))(*ins)


INPROJ_TILES = {"nn": dict(tm=1024, tn=1536, tk=1024), "tn": dict(tm=1024, tn=768, tk=2048), "nt": dict(tm=512, tn=1024, tk=8192)}


def _make_multi_linear(name, n):
    @jax.custom_vjp
    def op(h, ws):
        return tuple(_mm(f"{name}_fwd{i}", h, w, "nn") for i, w in enumerate(ws))

    def fwd(h, ws):
        return op(h, ws), (h, ws)

    def bwd(res, douts):
        h, ws = res
        dh = None
        for i, (w, d) in enumerate(zip(ws, douts)):
            dh = _mm(f"{name}_dh{i}", d, w, "nt", acc=dh, out_dtype=h.dtype if i == len(ws) - 1 else F32)
        dws = tuple(_mm(f"{name}_dw{i}", h, d, "tn", out_dtype=w.dtype) for i, (w, d) in enumerate(zip(ws, douts)))
        return dh, dws

    op.defvjp(fwd, bwd)
    return op


def _make_resid_linear(name):
    @jax.custom_vjp
    def op(x, a, w):
        return _mm(f"{name}_fwd", a, w, "nn", acc=x)

    def fwd(x, a, w):
        return op(x, a, w), (a, w)

    def bwd(res, dy):
        a, w = res
        return dy, _mm(f"{name}_da", dy, w, "nt", out_dtype=a.dtype), _mm(f"{name}_dw", a, dy, "tn", out_dtype=w.dtype)

    op.defvjp(fwd, bwd)
    return op


def _make_rowwise(name, f, tile):
    def specs(rows, aux, params, consts, t):
        row = [pl.BlockSpec((t, a.shape[1]), lambda i: (i, 0)) for a in (*rows, *aux)]
        full = [pl.BlockSpec(p.shape, lambda i: (0, 0)) for p in (*params, *consts)]
        return row, full

    def fwd_call(rows, aux, params, consts):
        S = rows[0].shape[0]
        t = min(tile, S)
        n_in = len(rows) + len(aux) + len(params) + len(consts)
        shp = lambda a: jax.ShapeDtypeStruct((t, a.shape[1]), a.dtype)
        outs = jax.eval_shape(f, *[shp(a) for a in (*rows, *aux)], *params, *consts)
        row_specs, full_specs = specs(rows, aux, params, consts, t)

        def body(*refs):
            res = f(*[r[...] for r in refs[:n_in]])
            for o_ref, o in zip(refs[n_in:], res):
                o_ref[...] = o

        return _pcall(body, name=f"{name}_fwd", grid=(S // t,), in_specs=row_specs + full_specs,
                      out_specs=[pl.BlockSpec((t, o.shape[1]), lambda i: (i, 0)) for o in outs],
                      out_shape=[jax.ShapeDtypeStruct((S, o.shape[1]), o.dtype) for o in outs],
                      compiler_params=_params(("parallel",)))(*rows, *aux, *params, *consts)

    def bwd_call(rows, aux, params, consts, douts):
        S = rows[0].shape[0]
        t = min(tile, S)
        nr, na, npar, nc, nd = len(rows), len(aux), len(params), len(consts), len(douts)
        row_specs, full_specs = specs(rows, aux, params, consts, t)

        def body(*refs):
            vals = [r[...] for r in refs[:nr + na + npar + nc]]
            rv, av = vals[:nr], vals[nr:nr + na]
            pv, cv = vals[nr + na:nr + na + npar], vals[nr + na + npar:]
            dv = tuple(r[...] for r in refs[nr + na + npar + nc:nr + na + npar + nc + nd])
            out_refs = refs[nr + na + npar + nc + nd:]
            _, vjp = jax.vjp(lambda *rp: f(*rp[:nr], *av, *rp[nr:], *cv), *rv, *pv)
            grads = vjp(dv)
            for o_ref, g in zip(out_refs[:nr], grads[:nr]):
                o_ref[...] = g
            first = pl.program_id(0) == 0
            for o_ref, g in zip(out_refs[nr:], grads[nr:]):
                @pl.when(first)
                def _(o_ref=o_ref):
                    o_ref[...] = jnp.zeros_like(o_ref)
                o_ref[...] += g

        res = _pcall(body, name=f"{name}_bwd", grid=(S // t,),
                     in_specs=row_specs + full_specs + [pl.BlockSpec((t, d.shape[1]), lambda i: (i, 0)) for d in douts],
                     out_specs=[pl.BlockSpec((t, a.shape[1]), lambda i: (i, 0)) for a in rows]
                     + [pl.BlockSpec(p.shape, lambda i: (0, 0)) for p in params],
                     out_shape=[jax.ShapeDtypeStruct(a.shape, a.dtype) for a in (*rows, *params)],
                     compiler_params=_params(("arbitrary",)))(*rows, *aux, *params, *consts, *douts)
        return tuple(res[:nr]), tuple(res[nr:])

    @jax.custom_vjp
    def op(rows, aux, params, consts):
        return tuple(fwd_call(rows, aux, params, consts))

    def fwd(rows, aux, params, consts):
        return op(rows, aux, params, consts), (rows, aux, params, consts)

    def bwd(res, douts):
        rows, aux, params, consts = res
        drows, dparams = bwd_call(rows, aux, params, consts, tuple(douts))
        zeros = lambda xs: tuple(jnp.zeros_like(a) for a in xs)
        return drows, zeros(aux), dparams, zeros(consts)

    op.defvjp(fwd, bwd)
    return op


@jax.custom_vjp
def _swap_halves(x):
    return pltpu.roll(x, LANES // 2, 1)


_swap_halves.defvjp(lambda x: (_swap_halves(x), None), lambda _, g: (_swap_halves(g),))


def _rope(x, cos_t, sin_t):
    return x * cos_t + _swap_halves(x) * sin_t


def _rms(x, g, n=None):
    n = x.shape[-1] if n is None else n
    return x * lax.rsqrt(jnp.sum(x * x, axis=-1, keepdims=True) * (1.0 / n) + RMS_EPS) * g


def _heads(x):
    return [x[:, i * LANES:(i + 1) * LANES] for i in range(x.shape[1] // LANES)]


def _cat(xs):
    return jnp.concatenate(xs, axis=1)


def _silu(x):
    return x * jax.nn.sigmoid(x)


def _f_norm(x, g):
    return (_rms(x, g).astype(BF16),)


def _f_mla_a(q_lat, c_kv, kpe, cos_p, sin_p, qa_g, kva_g, kpe_g):
    kp = _rope(_rms(kpe, kpe_g, MLA_ROPE), cos_p, sin_p)
    return _rms(q_lat, qa_g).astype(BF16), _rms(c_kv, kva_g).astype(BF16), _cat([kp] * MLA_HEADS)


def _f_mla_b(q8, kn_raw, cos_p, sin_p, qn_g, qp_g, kn_g):
    hs = _heads(q8)
    qn = _cat([_rms(h, qn_g) for h in hs[:MLA_HEADS]])
    qp = _cat([_rope(_rms(h, qp_g, MLA_ROPE), cos_p, sin_p) for h in hs[MLA_HEADS:]])
    kn = _cat([_rms(h, kn_g) for h in _heads(kn_raw)])
    return qn, qp, kn


def _softplus(x):
    return jnp.maximum(x, 0.0) + jnp.log(1.0 + jnp.exp(-jnp.abs(x)))


def _l2n(x):
    return x * lax.rsqrt(jnp.sum(x * x, axis=-1, keepdims=True) + 1e-6)


def _f_dn_pre(mixed, ab, alog_f, dtb_f, e_a, e_b):
    hs = _heads(mixed)
    q = _cat([_l2n(h) * (LANES ** -0.5) for h in hs[:DN_HEADS]])
    k = _cat([_l2n(h) for h in hs[DN_HEADS:2 * DN_HEADS]])
    v = _cat(hs[2 * DN_HEADS:])
    a_f = jnp.dot(ab, e_a, precision=HI, preferred_element_type=F32)
    b_f = jnp.dot(ab, e_b, precision=HI, preferred_element_type=F32)
    g = -jnp.exp(alog_f) * _softplus(a_f + dtb_f)
    return q, k, v, g, jax.nn.sigmoid(b_f)


def _f_dil_pre(qkv, cos_h, sin_h, q_g, k_g):
    hs = _heads(qkv)
    q = [_rope(_rms(h, q_g), cos_h, sin_h) for h in hs[:DIL_HEADS]]
    k = [_rope(_rms(h, k_g), cos_h, sin_h) for h in hs[DIL_HEADS:2 * DIL_HEADS]]
    v = hs[2 * DIL_HEADS:]
    group = lambda xs, g: _cat(xs[g * DIL_GROUP_HEADS:(g + 1) * DIL_GROUP_HEADS])
    return tuple(group(xs, g) for xs in (q, k, v) for g in range(len(DIL_DILATIONS)))


def _f_merge_a(y_a, z_a, o_dn, z_b, o0, o1, o2, l0, l1, l2, z_c, out_g):
    y_b = _cat([_rms(h, out_g) for h in _heads(o_dn)])
    os_, ls = [_heads(o) for o in (o0, o1, o2)], [_heads(l) for l in (l0, l1, l2)]
    y_c = []
    for j in range(DIL_GROUP_HEADS):
        l3 = [ls[g][j] for g in range(3)]
        m = jnp.maximum(jnp.maximum(l3[0], l3[1]), l3[2])
        e3 = [jnp.exp(l - m) for l in l3]
        den = e3[0] + e3[1] + e3[2]
        y_c.append(sum(e3[g] * os_[g][j] for g in range(3)) / den)
    return tuple(t.astype(BF16) for t in (y_a * _silu(z_a), y_b * _silu(z_b), _cat(y_c) * _silu(z_c)))


def _f_merge_b(b0, b1, b2, gl):
    gs = [jax.nn.sigmoid(gl[:, i * D_MODEL:(i + 1) * D_MODEL]) for i in range(3)]
    return ((gs[0] * b0 + gs[1] * b1 + gs[2] * b2).astype(BF16),)


def _rope_tables(pos, inv_sign):
    S = pos.shape[0]
    t = min(S, 1024)

    def body(p_ref, c_ref, cp, sp, ch, sh):
        p = p_ref[...].astype(F32)
        c = c_ref[...]
        ang_p, ang_h = p * c[0:1], p * c[2:3]
        cp[...] = jnp.cos(ang_p) * jnp.abs(c[1:2])
        sp[...] = jnp.sin(ang_p) * c[1:2]
        ch[...] = jnp.cos(ang_h)
        sh[...] = jnp.sin(ang_h) * c[3:4]

    row = pl.BlockSpec((t, LANES), lambda i: (i, 0))
    return _pcall(body, name="rope_tables", grid=(S // t,),
                  in_specs=[pl.BlockSpec((t, 1), lambda i: (i, 0)), pl.BlockSpec((4, LANES), lambda i: (0, 0))],
                  out_specs=[row] * 4, out_shape=[jax.ShapeDtypeStruct((S, LANES), F32)] * 4,
                  compiler_params=_params(("parallel",)))(pos, inv_sign)


def _rope_consts():
    half_p, half_h = MLA_ROPE // 2, LANES // 2
    inv_p = 1.0 / (ROPE_THETA ** (jnp.arange(0, MLA_ROPE, 2, dtype=F32) / MLA_ROPE))
    inv_h = 1.0 / (ROPE_THETA ** (jnp.arange(0, LANES, 2, dtype=F32) / LANES))
    z = jnp.zeros((half_p,), F32)
    o = jnp.ones((half_p,), F32)
    return jnp.stack([jnp.concatenate([inv_p, z, inv_p, z]), jnp.concatenate([-o, z, o, z]),
                      jnp.concatenate([inv_h, inv_h]), jnp.concatenate([-jnp.ones((half_h,), F32), jnp.ones((half_h,), F32)])])


def _shift_rows(x, s, up):
    n = x.shape[0]
    r = lax.broadcasted_iota(jnp.int32, x.shape, 0)
    if up:
        return jnp.where(r < n - s, pltpu.roll(x, n - s, 0), 0.0)
    return jnp.where(r >= s, pltpu.roll(x, s, 0), 0.0)


def _make_shift(s):
    @jax.custom_vjp
    def sh(x):
        return _shift_rows(x, s, False)

    sh.defvjp(lambda x: (sh(x), None), lambda _, g: (_shift_rows(g, s, True),))
    return sh


def _f_conv(x, w):
    y = x * w[DN_CONV - 1:DN_CONV]
    for j in range(DN_CONV - 1):
        y = y + _make_shift(DN_CONV - 1 - j)(x) * w[j:j + 1]
    return _silu(y)


def _make_conv(name):
    def call(x, w, dy=None):
        S, C = x.shape
        col = pl.BlockSpec((S, LANES), lambda i: (0, i))
        wsp = pl.BlockSpec((DN_CONV, LANES), lambda i: (0, i))
        if dy is None:
            def body(x_ref, w_ref, o_ref):
                o_ref[...] = _f_conv(x_ref[...], w_ref[...])
            return _pcall(body, name=f"{name}_fwd", grid=(C // LANES,), in_specs=[col, wsp], out_specs=col,
                          out_shape=jax.ShapeDtypeStruct(x.shape, F32), compiler_params=_params(("parallel",)))(x, w)

        def body(x_ref, w_ref, dy_ref, dx_ref, dw_ref):
            _, vjp = jax.vjp(_f_conv, x_ref[...], w_ref[...])
            dx_ref[...], dw_ref[...] = vjp(dy_ref[...])
        return _pcall(body, name=f"{name}_bwd", grid=(C // LANES,), in_specs=[col, wsp, col], out_specs=[col, wsp],
                      out_shape=[jax.ShapeDtypeStruct(x.shape, F32), jax.ShapeDtypeStruct(w.shape, F32)],
                      compiler_params=_params(("parallel",)))(x, w, dy)

    @jax.custom_vjp
    def op(x, w):
        return call(x, w)

    op.defvjp(lambda x, w: (op(x, w), (x, w)), lambda res, dy: tuple(call(*res, dy)))
    return op


def _dot_nt(a, b):
    return lax.dot_general(a.astype(BF16), b.astype(BF16), (((1,), (1,)), ((), ())), preferred_element_type=F32)


def _dot_nn(a, b):
    return jnp.dot(a.astype(BF16), b.astype(BF16), preferred_element_type=F32)


def _dot_tn(a, b):
    return lax.dot_general(a.astype(BF16), b.astype(BF16), (((0,), (0,)), ((), ())), preferred_element_type=F32)


def _mla_scores(qn_r, qp_r, kn_r, kp_r, diagonal):
    scale = MLA_QK ** -0.5
    s = _dot_nt(qn_r[...] * scale, kn_r[...]) + _dot_nt(qp_r[...] * scale, kp_r[...])
    if diagonal:
        r = lax.broadcasted_iota(jnp.int32, s.shape, 0)
        c = lax.broadcasted_iota(jnp.int32, s.shape, 1)
        s = jnp.where(c <= r, s, NEG)
    return s


def _on_causal_pairs(qi, ki, step):
    @pl.when(ki < qi)
    def _():
        step(False)

    @pl.when(ki == qi)
    def _():
        step(True)


def _causal_pairs(n, t, by_key):
    pairs = [(q, k) for k in range(n) for q in range(k, n)] if by_key else [(q, k) for q in range(n) for k in range(q + 1)]
    qt, kt = (jnp.array([p[i] for p in pairs], jnp.int32) for i in (0, 1))
    return (qt, kt, pl.BlockSpec((t, LANES), lambda h, p, qt_r, kt_r: (qt_r[p], h)),
            pl.BlockSpec((t, LANES), lambda h, p, qt_r, kt_r: (kt_r[p], h)))


def _make_mla_attn(name):
    scale = MLA_QK ** -0.5

    def fwd_call(qn, qp, kn, kp, v):
        S = qn.shape[0]
        t = min(S, 512)
        n = S // t
        qt, kt, qs, ks = _causal_pairs(n, t, by_key=False)

        def body(qt_r, kt_r, qn_r, qp_r, kn_r, kp_r, v_r, o_r, lse_r, m_s, l_s, acc_s):
            qi, ki = qt_r[pl.program_id(1)], kt_r[pl.program_id(1)]

            @pl.when(ki == 0)
            def _():
                m_s[...] = jnp.full_like(m_s, NEG)
                l_s[...] = jnp.zeros_like(l_s)
                acc_s[...] = jnp.zeros_like(acc_s)

            def step(diagonal):
                s = _mla_scores(qn_r, qp_r, kn_r, kp_r, diagonal)
                m_old = m_s[...]
                m_new = jnp.maximum(m_old, jnp.max(s, axis=-1, keepdims=True))
                p = jnp.exp(s - m_new[:, :1])
                alpha = jnp.exp(m_old - m_new)
                l_s[...] = alpha * l_s[...] + jnp.sum(p, axis=-1, keepdims=True)
                acc_s[...] = alpha * acc_s[...] + _dot_nn(p, v_r[...])
                m_s[...] = m_new
            _on_causal_pairs(qi, ki, step)

            @pl.when(ki == qi)
            def _():
                o_r[...] = acc_s[...] / l_s[...]
                lse_r[...] = m_s[...] + jnp.log(l_s[...])

        spec = pltpu.PrefetchScalarGridSpec(num_scalar_prefetch=2, grid=(MLA_HEADS, qt.shape[0]), in_specs=[qs, qs, ks, ks, ks],
                                            out_specs=[qs, qs], scratch_shapes=[pltpu.VMEM((t, LANES), F32)] * 3)
        return _pcall(body, name=f"{name}_fwd", grid_spec=spec, out_shape=[jax.ShapeDtypeStruct((S, MLA_HEADS * LANES), F32)] * 2,
                      compiler_params=_params(("parallel", "arbitrary")))(qt, kt, qn, qp, kn, kp, v)

    def dq_call(qn, qp, kn, kp, v, o, lse, do):
        S = qn.shape[0]
        t = min(S, 512)
        n = S // t
        qt, kt, qs, ks = _causal_pairs(n, t, by_key=False)

        def body(qt_r, kt_r, qn_r, qp_r, kn_r, kp_r, v_r, o_r, lse_r, do_r, dqn_r, dqp_r, dl_r, dqn_s, dqp_s):
            qi, ki = qt_r[pl.program_id(1)], kt_r[pl.program_id(1)]

            @pl.when(ki == 0)
            def _():
                dqn_s[...] = jnp.zeros_like(dqn_s)
                dqp_s[...] = jnp.zeros_like(dqp_s)
                dl_r[...] = jnp.broadcast_to(jnp.sum(do_r[...] * o_r[...], axis=-1, keepdims=True), dl_r.shape)

            def step(diagonal):
                p = jnp.exp(_mla_scores(qn_r, qp_r, kn_r, kp_r, diagonal) - lse_r[...][:, :1])
                ds = p * (_dot_nt(do_r[...], v_r[...]) - dl_r[...][:, :1])
                dqn_s[...] += _dot_nn(ds, kn_r[...])
                dqp_s[...] += _dot_nn(ds, kp_r[...])
            _on_causal_pairs(qi, ki, step)

            @pl.when(ki == qi)
            def _():
                dqn_r[...] = dqn_s[...] * scale
                dqp_r[...] = dqp_s[...] * scale

        spec = pltpu.PrefetchScalarGridSpec(num_scalar_prefetch=2, grid=(MLA_HEADS, qt.shape[0]), in_specs=[qs, qs, ks, ks, ks, qs, qs, qs],
                                            out_specs=[qs, qs, qs], scratch_shapes=[pltpu.VMEM((t, LANES), F32)] * 2)
        return _pcall(body, name=f"{name}_dq", grid_spec=spec, out_shape=[jax.ShapeDtypeStruct((S, MLA_HEADS * LANES), F32)] * 3,
                      compiler_params=_params(("parallel", "arbitrary")))(qt, kt, qn, qp, kn, kp, v, o, lse, do)

    def dkv_call(qn, qp, kn, kp, v, lse, do, dl):
        S = qn.shape[0]
        t = min(S, 512)
        n = S // t
        qt, kt, qs, ks = _causal_pairs(n, t, by_key=True)

        def body(qt_r, kt_r, qn_r, qp_r, kn_r, kp_r, v_r, lse_r, do_r, dl_r, dkn_r, dkp_r, dv_r, dkn_s, dkp_s, dv_s):
            qi, ki = qt_r[pl.program_id(1)], kt_r[pl.program_id(1)]

            @pl.when(qi == ki)
            def _():
                dkn_s[...] = jnp.zeros_like(dkn_s)
                dkp_s[...] = jnp.zeros_like(dkp_s)
                dv_s[...] = jnp.zeros_like(dv_s)

            def step(diagonal):
                p = jnp.exp(_mla_scores(qn_r, qp_r, kn_r, kp_r, diagonal) - lse_r[...][:, :1])
                ds = p * (_dot_nt(do_r[...], v_r[...]) - dl_r[...][:, :1])
                dv_s[...] += _dot_tn(p, do_r[...])
                dkn_s[...] += _dot_tn(ds, qn_r[...] * scale)
                dkp_s[...] += _dot_tn(ds, qp_r[...] * scale)
            _on_causal_pairs(qi, ki, step)

            @pl.when(qi == n - 1)
            def _():
                dkn_r[...] = dkn_s[...]
                dkp_r[...] = dkp_s[...]
                dv_r[...] = dv_s[...]

        spec = pltpu.PrefetchScalarGridSpec(num_scalar_prefetch=2, grid=(MLA_HEADS, qt.shape[0]), in_specs=[qs, qs, ks, ks, ks, qs, qs, qs],
                                            out_specs=[ks, ks, ks], scratch_shapes=[pltpu.VMEM((t, LANES), F32)] * 3)
        return _pcall(body, name=f"{name}_dkv", grid_spec=spec, out_shape=[jax.ShapeDtypeStruct((S, MLA_HEADS * LANES), F32)] * 3,
                      compiler_params=_params(("parallel", "arbitrary")))(qt, kt, qn, qp, kn, kp, v, lse, do, dl)

    @jax.custom_vjp
    def op(qn, qp, kn, kp, v):
        return fwd_call(qn, qp, kn, kp, v)[0]

    def fwd(qn, qp, kn, kp, v):
        o, lse = fwd_call(qn, qp, kn, kp, v)
        return o, (qn, qp, kn, kp, v, o, lse)

    def bwd(res, do):
        qn, qp, kn, kp, v, o, lse = res
        dqn, dqp, dl = dq_call(qn, qp, kn, kp, v, o, lse, do)
        dkn, dkp, dv = dkv_call(qn, qp, kn, kp, v, lse, do, dl)
        return dqn, dqp, dkn, dkp, dv

    op.defvjp(fwd, bwd)
    return op


def _dil_block(q, kp, kc, vp, vc, has_prev):
    scale = LANES ** -0.5
    r = lax.broadcasted_iota(jnp.int32, (DIL_BLOCK, 2 * DIL_BLOCK), 0)
    c = lax.broadcasted_iota(jnp.int32, (DIL_BLOCK, 2 * DIL_BLOCK), 1)
    valid = ((c < DIL_BLOCK) & (c >= r) & has_prev) | ((c >= DIL_BLOCK) & (c - DIL_BLOCK <= r))
    s = jnp.where(valid, _dot_nt(q * scale, jnp.concatenate([kp, kc], axis=0)), NEG)
    m = jnp.max(s, axis=-1, keepdims=True)
    e = jnp.exp(s - m)
    den = jnp.sum(e, axis=-1, keepdims=True)
    o = _dot_nn(e, jnp.concatenate([vp, vc], axis=0)) / den
    return o, jnp.broadcast_to(m + jnp.log(den), o.shape)


DIL_TILE_ROWS = (1024, 1024, 2048)


def _make_dil_attn(name, d, tile_rows):
    def call(q, k, v, cts=None):
        S = q.shape[0]
        span = DIL_BLOCK * d
        G = max(1, min(tile_rows, S) // span)
        n = S // (G * span)
        at = (lambda i: i) if cts is None else (lambda i: n - 1 - i)
        tile = pl.BlockSpec((G * span, LANES), lambda h, i: (at(i), h))
        before = pl.BlockSpec((span, LANES), lambda h, i: (jnp.maximum(at(i) * G - 1, 0), h))

        def rows(r, j):
            return pl.ds(j * DIL_BLOCK, DIL_BLOCK) if d == 1 else pl.ds(r + j * span, DIL_BLOCK, stride=d)

        def over_residues(fn):
            if d == 1:
                fn(0)
            else:
                lax.fori_loop(0, d, lambda r, c: (fn(r), c)[1], 0)

        def block_inputs(r, j, q_r, kb_r, k_r, vb_r, v_r):
            kp = kb_r[rows(r, 0), :] if j == 0 else k_r[rows(r, j - 1), :]
            vp = vb_r[rows(r, 0), :] if j == 0 else v_r[rows(r, j - 1), :]
            return q_r[rows(r, j), :], kp, k_r[rows(r, j), :], vp, v_r[rows(r, j), :]

        if cts is None:
            def body(q_r, kb_r, k_r, vb_r, v_r, o_r, lse_r):
                first = at(pl.program_id(1)) * G

                def residue(r):
                    for j in range(G):
                        o_r[rows(r, j), :], lse_r[rows(r, j), :] = _dil_block(
                            *block_inputs(r, j, q_r, kb_r, k_r, vb_r, v_r), first + j > 0)
                over_residues(residue)
            return _pcall(body, name=f"{name}_fwd", grid=(DIL_GROUP_HEADS, n), in_specs=[tile, before, tile, before, tile],
                          out_specs=[tile, tile], out_shape=[jax.ShapeDtypeStruct(q.shape, F32)] * 2,
                          compiler_params=_params(("parallel", "parallel")))(q, k, k, v, v)

        def body(q_r, kb_r, k_r, vb_r, v_r, do_r, dl_r, dq_r, dk_r, dv_r, ck_s, cv_s):
            first = at(pl.program_id(1)) * G

            @pl.when(pl.program_id(1) == 0)
            def _():
                ck_s[...] = jnp.zeros_like(ck_s)
                cv_s[...] = jnp.zeros_like(cv_s)

            def residue(r):
                owed = None
                for j in range(G):
                    hp = first + j > 0
                    _, vjp = jax.vjp(lambda *a: _dil_block(*a, hp), *block_inputs(r, j, q_r, kb_r, k_r, vb_r, v_r))
                    dq, dkp, dkc, dvp, dvc = vjp((do_r[rows(r, j), :], dl_r[rows(r, j), :]))
                    dq_r[rows(r, j), :] = dq
                    if j == G - 1:
                        dkc, dvc = dkc + ck_s[rows(r, 0), :], dvc + cv_s[rows(r, 0), :]
                    dk_r[rows(r, j), :], dv_r[rows(r, j), :] = dkc, dvc
                    if j == 0:
                        owed = (dkp, dvp)
                    else:
                        dk_r[rows(r, j - 1), :] += dkp
                        dv_r[rows(r, j - 1), :] += dvp
                ck_s[rows(r, 0), :], cv_s[rows(r, 0), :] = owed
            over_residues(residue)
        return _pcall(body, name=f"{name}_bwd", grid=(DIL_GROUP_HEADS, n), in_specs=[tile, before, tile, before, tile, tile, tile],
                      out_specs=[tile] * 3, out_shape=[jax.ShapeDtypeStruct(q.shape, F32)] * 3,
                      scratch_shapes=[pltpu.VMEM((span, LANES), F32)] * 2,
                      compiler_params=_params(("parallel", "arbitrary")))(q, k, k, v, v, *cts)

    @jax.custom_vjp
    def op(q, k, v):
        return tuple(call(q, k, v))

    op.defvjp(lambda q, k, v: (op(q, k, v), (q, k, v)), lambda res, cts: tuple(call(*res, cts=cts)))
    return op


def _pdot(a, b, dims):
    return lax.dot_general(a, b, (dims, ((), ())), precision=lax.Precision.HIGH, preferred_element_type=F32)


DN_LOCAL_CHUNKS = 4


DN_BLOCK_HEADS = 4
DN_BLOCK = DN_BLOCK_HEADS * DN_CHUNK


def _inverse_cotangent(inv, d):
    return -_pdot(inv, _pdot(d, inv, ((1,), (1,))), ((0,), (0,)))


@jax.custom_vjp
def _unit_lower_inverse(a):
    n = a.shape[0]
    eye = (lax.broadcasted_iota(jnp.int32, (n, n), 0) == lax.broadcasted_iota(jnp.int32, (n, n), 1)).astype(F32)
    inv, pw = eye - a, a
    for _ in range(5):
        pw = _pdot(pw, pw, ((1,), (0,)))
        inv = inv + _pdot(inv, pw, ((1,), (0,)))
    return inv


def _unit_lower_inverse_fwd(a):
    inv = _unit_lower_inverse(a)
    return inv, inv


_unit_lower_inverse.defvjp(_unit_lower_inverse_fwd, lambda inv, d: (_inverse_cotangent(inv, d),))


@jax.custom_vjp
def _known_inverse(a, inv):
    return inv


_known_inverse.defvjp(lambda a, inv: (inv, inv), lambda inv, d: (_inverse_cotangent(inv, d), jnp.zeros_like(inv)))


def _dn_local(q, k, v, g, b, known=None):
    C, R = DN_CHUNK, DN_BLOCK
    r = lax.broadcasted_iota(jnp.int32, (R, R), 0)
    c = lax.broadcasted_iota(jnp.int32, (R, R), 1)
    same_head = (r // C) == (c // C)
    incl, strict = same_head & (r >= c), same_head & (r > c)
    avg = jnp.full((R, LANES), 1.0 / LANES, F32)
    rc = lax.broadcasted_iota(jnp.int32, (C, C), 0) >= lax.broadcasted_iota(jnp.int32, (C, C), 1)
    gc_lanes = _pdot(rc.astype(F32), g, ((1,), (0,)))
    us, ws, qes, kds, qks, invs = [], [], [], [], [], []
    for first in range(0, DN_HEADS, DN_BLOCK_HEADS):
        stack = lambda x: jnp.concatenate(_heads(x)[first:first + DN_BLOCK_HEADS], axis=0)
        unstack = lambda x: [x[p * C:(p + 1) * C] for p in range(DN_BLOCK_HEADS)]
        gc, q_s, k_s, v_s, b_s = (stack(x) for x in (gc_lanes, q, k, v, b))
        gc_j = _pdot(avg, gc, ((1,), (1,)))
        decay = jnp.exp(jnp.where(incl, _cat([gc] * (R // LANES)) - gc_j, NEG))
        kb = k_s * b_s
        kk = _pdot(jnp.concatenate([kb, q_s], axis=0), k_s, ((1,), (1,)))
        a = jnp.where(strict, kk[:R] * decay, 0.0)
        inv = _unit_lower_inverse(a) if known is None else _known_inverse(a, known[len(invs) * R:(len(invs) + 1) * R])
        invs.append(inv)
        eg = jnp.exp(gc)
        uw = _pdot(inv, _cat([v_s * b_s, kb * eg]), ((1,), (0,)))
        g_last = jnp.concatenate([jnp.broadcast_to(x[C - 1:C], (C, LANES)) for x in unstack(gc)], axis=0)
        us += unstack(uw[:, :LANES])
        ws += unstack(uw[:, LANES:])
        qes += unstack(q_s * eg)
        kds += unstack(k_s * jnp.exp(g_last - gc))
        qks.append(kk[R:] * decay)
    egl = jnp.broadcast_to(jnp.exp(gc_lanes[C - 1:C]), (8, DN_HEADS * LANES))
    return _cat(us), _cat(ws), _cat(qes), _cat(kds), jnp.concatenate(qks, axis=0), egl, jnp.concatenate(invs, axis=0)


def _dn_scan(u, w, qe, kd, qk, egl, state):
    C = DN_CHUNK
    heads = [slice(h * LANES, (h + 1) * LANES) for h in range(DN_HEADS)]
    ws = [_pdot(jnp.concatenate([w[:, sl], qe[:, sl]], axis=0), state[sl, :], ((1,), (0,))) for sl in heads]
    v_new = [u[:, sl] - x[:C] for sl, x in zip(heads, ws)]
    local = []
    for i, first in enumerate(range(0, DN_HEADS, DN_BLOCK_HEADS)):
        y = _pdot(qk[i * DN_BLOCK:(i + 1) * DN_BLOCK], jnp.concatenate(v_new[first:first + DN_BLOCK_HEADS], axis=0), ((1,), (0,)))
        local += [y[p * C:(p + 1) * C] for p in range(DN_BLOCK_HEADS)]
    o = _cat([x[C:] + y for x, y in zip(ws, local)])
    states = [state[sl, :] * egl[0:1, sl] + _pdot(kd[:, sl], vn, ((0,), (0,))) for sl, vn in zip(heads, v_new)]
    return o, jnp.concatenate(states, axis=0)


def _make_delta_rule(name):
    W = DN_HEADS * LANES
    QK = DN_HEADS * DN_CHUNK

    def local_call(ins, cts=None):
        S = ins[0].shape[0]
        n = S // DN_CHUNK
        per = math.gcd(DN_LOCAL_CHUNKS, n)
        row = pl.BlockSpec((per * DN_CHUNK, W), lambda i: (i, 0))
        qkb = pl.BlockSpec((per * QK, DN_BLOCK), lambda i: (i, 0))
        eg = pl.BlockSpec((per, 8, W), lambda i: (i, 0, 0))
        rows = lambda j: slice(j * DN_CHUNK, (j + 1) * DN_CHUNK)
        qk_rows = lambda j: slice(j * QK, (j + 1) * QK)
        out_rows = [rows, rows, rows, rows, qk_rows]

        if cts is None:
            def body(*refs):
                for j in range(per):
                    res = _dn_local(*[r[rows(j), :] for r in refs[:5]])
                    for o_r, o, at_ in zip(refs[5:10], res[:5], out_rows):
                        o_r[at_(j), :] = o
                    refs[10][j] = res[5]
                    refs[11][qk_rows(j), :] = res[6]
            blockdiag = jax.ShapeDtypeStruct((n * QK, DN_BLOCK), F32)
            return _pcall(body, name=f"{name}_local_fwd", grid=(n // per,), in_specs=[row] * 5, out_specs=[row] * 4 + [qkb, eg, qkb],
                          out_shape=[jax.ShapeDtypeStruct((S, W), F32)] * 4 + [blockdiag, jax.ShapeDtypeStruct((n, 8, W), F32), blockdiag],
                          compiler_params=_params(("parallel",)))(*ins)

        def body(*refs):
            for j in range(per):
                known = refs[5][qk_rows(j), :]
                _, vjp = jax.vjp(lambda *a: _dn_local(*a, known=known)[:6], *[r[rows(j), :] for r in refs[:5]])
                grads = vjp(tuple(r[at_(j), :] for r, at_ in zip(refs[6:11], out_rows)) + (refs[11][j],))
                for o_r, o in zip(refs[12:], grads):
                    o_r[rows(j), :] = o
        return _pcall(body, name=f"{name}_local_bwd", grid=(n // per,), in_specs=[row] * 5 + [qkb] + [row] * 4 + [qkb, eg],
                      out_specs=[row] * 5, out_shape=[jax.ShapeDtypeStruct((S, W), F32)] * 5,
                      compiler_params=_params(("parallel",)))(*ins, *cts)

    def scan_call(ins, saved=None, do=None):
        S = ins[0].shape[0]
        n = S // DN_CHUNK
        at = (lambda i: i) if do is None else (lambda i: n - 1 - i)
        row = pl.BlockSpec((DN_CHUNK, W), lambda i: (at(i), 0))
        qkb = pl.BlockSpec((QK, DN_BLOCK), lambda i: (at(i), 0))
        eg = pl.BlockSpec((None, 8, W), lambda i: (at(i), 0, 0))
        st = pl.BlockSpec((None, W, LANES), lambda i: (at(i), 0, 0))

        if do is None:
            def body(*refs):
                o_r, st_r, s_s = refs[6:]

                @pl.when(pl.program_id(0) == 0)
                def _():
                    s_s[...] = jnp.zeros_like(s_s)
                st_r[...] = s_s[...]
                o_r[...], s_s[...] = _dn_scan(*[r[...] for r in refs[:6]], s_s[...])
            return _pcall(body, name=f"{name}_scan_fwd", grid=(n,), in_specs=[row] * 4 + [qkb, eg], out_specs=[row, st],
                          out_shape=[jax.ShapeDtypeStruct((S, W), F32), jax.ShapeDtypeStruct((n, W, LANES), F32)],
                          scratch_shapes=[pltpu.VMEM((W, LANES), F32)], compiler_params=_params(("arbitrary",)))(*ins)

        def body(*refs):
            st_r, do_r = refs[6:8]
            outs, ds_s = refs[8:14], refs[14]

            @pl.when(pl.program_id(0) == 0)
            def _():
                ds_s[...] = jnp.zeros_like(ds_s)
            _, vjp = jax.vjp(_dn_scan, *[r[...] for r in refs[:6]], st_r[...])
            *grads, ds = vjp((do_r[...], ds_s[...]))
            for o_r, gval in zip(outs, grads):
                o_r[...] = gval
            ds_s[...] = ds
        return _pcall(body, name=f"{name}_scan_bwd", grid=(n,), in_specs=[row] * 4 + [qkb, eg, st, row], out_specs=[row] * 4 + [qkb, eg],
                      out_shape=[jax.ShapeDtypeStruct((S, W), F32)] * 4
                      + [jax.ShapeDtypeStruct((n * QK, DN_BLOCK), F32), jax.ShapeDtypeStruct((n, 8, W), F32)],
                      scratch_shapes=[pltpu.VMEM((W, LANES), F32)], compiler_params=_params(("arbitrary",)))(*ins, saved, do)

    @jax.custom_vjp
    def local(q, k, v, g, b):
        return tuple(local_call((q, k, v, g, b))[:6])

    def local_fwd(*a):
        *outs, inverses = local_call(a)
        return tuple(outs), (*a, inverses)

    local.defvjp(local_fwd, lambda res, cts: tuple(local_call(res, tuple(cts))))

    @jax.custom_vjp
    def scan(u, w, qe, kd, qk, egl):
        return scan_call((u, w, qe, kd, qk, egl))[0]

    def scan_fwd(*a):
        o, states = scan_call(a)
        return o, (a, states)

    scan.defvjp(scan_fwd, lambda res, do: tuple(scan_call(res[0], res[1], do)))
    return lambda q, k, v, g, b: scan(*local(q, k, v, g, b))


def _loss_call(y, target):
    S, D = y.shape
    t = min(S, 512)
    n = S // t
    row = pl.BlockSpec((t, D), lambda i: (i, 0))

    def body(y_r, t_r, loss_r, dy_r, acc_s):
        i = pl.program_id(0)

        @pl.when(i == 0)
        def _():
            acc_s[...] = jnp.zeros_like(acc_s)
        err = y_r[...] - t_r[...]
        dy_r[...] = err * (1.0 / D)
        acc_s[...] += jnp.sum(err * err, axis=0, keepdims=True)

        @pl.when(i == n - 1)
        def _():
            loss_r[...] = jnp.broadcast_to(jnp.sum(acc_s[...], axis=1, keepdims=True) * (0.5 / D), loss_r.shape)

    return _pcall(body, name="loss_head", grid=(n,), in_specs=[row, row],
                  out_specs=[pl.BlockSpec((8, LANES), lambda i: (0, 0)), row],
                  out_shape=[jax.ShapeDtypeStruct((8, LANES), F32), jax.ShapeDtypeStruct((S, D), F32)],
                  scratch_shapes=[pltpu.VMEM((1, D), F32)], compiler_params=_params(("arbitrary",)))(y, target)


def _adamw_call(name, parts, w, m, v, rows=128):
    L, R, C = w.shape
    assert len(parts) == L
    t = _tile(R, rows, 8)
    row = pl.BlockSpec((None, t, C), lambda l, i: (l, i, 0))
    part = lambda k: pl.BlockSpec((N_DEV, t, C), lambda l, i: (0, jnp.where(l == k, i, 0), 0))

    def body(*refs):
        p_refs, (w_r, m_r, v_r, g_r, d_r, nm_r, nv_r) = refs[:L], refs[L:]

        def update(p_r):
            g = p_r[0].astype(F32)
            for s in range(1, N_DEV):
                g = g + p_r[s].astype(F32)
            m_new = ADAM_B1 * m_r[...] + (1.0 - ADAM_B1) * g
            v_new = ADAM_B2 * v_r[...] + (1.0 - ADAM_B2) * (g * g)
            m_hat = m_new / (1.0 - ADAM_B1 ** ADAM_STEP)
            v_hat = v_new / (1.0 - ADAM_B2 ** ADAM_STEP)
            g_r[...] = g
            d_r[...] = -ADAM_LR * (m_hat / (jnp.sqrt(v_hat) + ADAM_EPS) + ADAM_WD * w_r[...])
            nm_r[...] = m_new
            nv_r[...] = v_new

        for k in range(L):
            pl.when(pl.program_id(0) == k)(functools.partial(update, p_refs[k]))

    return _pcall(body, name=name, grid=(L, R // t), in_specs=[part(k) for k in range(L)] + [row, row, row],
                  out_specs=[row] * 4, out_shape=[jax.ShapeDtypeStruct((L, R, C), F32)] * 4,
                  compiler_params=_params(("arbitrary", "arbitrary")))(*parts, w, m, v)


def _my_place():
    x, y, c = lax.axis_index("x"), lax.axis_index("y"), lax.axis_index("c")
    return x, y, c


def _index(x, y, c):
    return 4 * x + 2 * y + c


def _all_gather(vs):
    n = len(vs)

    def body(*refs):
        v_refs, out_refs = refs[:n], refs[n:2 * n]
        send_sems, recv_sems, local_sems = refs[2 * n:]
        x, y, c = _my_place()
        me, sibling = (x, y, c), (x, y, 1 - c)
        chips = [(1 - x, y), (x, 1 - y), (1 - x, 1 - y)]

        def copy(a, k, block, to, src=None):
            rows = out_refs[a].at[_index(*block)]
            return pltpu.make_async_remote_copy(src_ref=rows if src is None else src, dst_ref=rows, send_sem=send_sems.at[a, k],
                                                recv_sem=recv_sems.at[a, k], device_id=to, device_id_type=MESH)

        mine = [pltpu.make_async_copy(v_refs[a], out_refs[a].at[_index(*me)], local_sems.at[a]) for a in range(n)]
        first, passed = [], []
        for a in range(n):
            mine[a].start()
            first += [copy(a, 0, me, sibling, src=v_refs[a])]
            first += [copy(a, 1 + j, me, (*chip, c), src=v_refs[a]) for j, chip in enumerate(chips)]
        for cp in first:
            cp.start()
        for j, chip in enumerate(chips):
            for a in range(n):
                copy(a, 1 + j, (*chip, c), me).wait_recv()
                passed.append(copy(a, 4 + j, (*chip, c), sibling))
                passed[-1].start()
        for a in range(n):
            copy(a, 0, sibling, me).wait_recv()
            for j, chip in enumerate(chips):
                copy(a, 4 + j, (*chip, 1 - c), me).wait_recv()
        for cp in first + passed:
            cp.wait_send()
        for a in range(n):
            mine[a].wait()

    any_ = pl.BlockSpec(memory_space=pl.ANY)
    return _pcall(body, name="gather_weights", in_specs=[any_] * n, out_specs=[any_] * n,
                  out_shape=[jax.ShapeDtypeStruct((N_DEV,) + v.shape, v.dtype) for v in vs],
                  scratch_shapes=[pltpu.SemaphoreType.DMA((n, 7)), pltpu.SemaphoreType.DMA((n, 7)), pltpu.SemaphoreType.DMA((n,))])(*vs)


def _all_to_all(vs):
    n = len(vs)

    def body(*refs):
        v_refs, out_refs = refs[:n], refs[n:2 * n]
        send_sems, recv_sems, local_sems = refs[2 * n:]
        x, y, c = _my_place()
        me = _index(x, y, c)
        mine = [pltpu.make_async_copy(v_refs[a].at[me], out_refs[a].at[me], local_sems.at[a]) for a in range(n)]
        copies = []
        for a in range(n):
            mine[a].start()
        for k in range(1, N_DEV):
            px = 1 - x if k & 4 else x
            py = 1 - y if k & 2 else y
            pc = 1 - c if k & 1 else c
            for a in range(n):
                cp = pltpu.make_async_remote_copy(src_ref=v_refs[a].at[_index(px, py, pc)], dst_ref=out_refs[a].at[me],
                                                  send_sem=send_sems.at[a, k - 1], recv_sem=recv_sems.at[a, k - 1],
                                                  device_id=(px, py, pc), device_id_type=MESH)
                cp.start()
                copies.append(cp)
        for cp in copies:
            cp.wait()
        for a in range(n):
            mine[a].wait()

    any_ = pl.BlockSpec(memory_space=pl.ANY)
    return _pcall(body, name="exchange_grads", in_specs=[any_] * n, out_specs=[any_] * n,
                  out_shape=[jax.ShapeDtypeStruct(v.shape, v.dtype) for v in vs],
                  scratch_shapes=[pltpu.SemaphoreType.DMA((n, 7)), pltpu.SemaphoreType.DMA((n, 7)), pltpu.SemaphoreType.DMA((n,))])(*vs)


_HBM = pl.BlockSpec(memory_space=pltpu.HBM)
_SEM = pl.BlockSpec(memory_space=pltpu.SEMAPHORE)
_EFFECT = pltpu.SideEffectType.DATAFLOW_SIDE_EFFECTING


def _direct_copies(gather, v_refs, land_refs, send_sems, recv_sems, local_sems):
    x, y, c = _my_place()
    me = _index(x, y, c)
    local, remote = [], []
    for a, (v_ref, land_ref) in enumerate(zip(v_refs, land_refs)):
        local.append(pltpu.make_async_copy(v_ref if gather else v_ref.at[me], land_ref.at[me], local_sems.at[a]))
    for k in range(1, N_DEV):
        px = 1 - x if k & 4 else x
        py = 1 - y if k & 2 else y
        pc = 1 - c if k & 1 else c
        for a, (v_ref, land_ref) in enumerate(zip(v_refs, land_refs)):
            sem = a * (N_DEV - 1) + k - 1
            remote.append(pltpu.make_async_remote_copy(
                src_ref=v_ref if gather else v_ref.at[_index(px, py, pc)], dst_ref=land_ref.at[me], send_sem=send_sems.at[sem],
                recv_sem=recv_sems.at[sem], device_id=(px, py, pc), device_id_type=MESH))
    return local, remote


def _exchange_start(name, vs, gather, thru):
    n = len(vs)
    lands = [lax.empty((N_DEV,) + v.shape if gather else v.shape, v.dtype) for v in vs]

    def body(*refs):
        v_refs, land_refs = refs[:n], refs[n:2 * n]
        send_sems, recv_sems, local_sems = refs[2 * n + 1:2 * n + 4]
        local, remote = _direct_copies(gather, v_refs, land_refs, send_sems, recv_sems, local_sems)
        for cp in local + remote:
            cp.start()

    hbm = lambda a: pltpu.HBM(a.shape, a.dtype)
    res = _pcall(body, name=name,
                 out_shape=(pltpu.SemaphoreType.DMA((n * (N_DEV - 1),)), pltpu.SemaphoreType.DMA((n * (N_DEV - 1),)), pltpu.SemaphoreType.DMA((n,)),
                            *[hbm(a) for a in (*vs, *lands, thru)]),
                 in_specs=[_HBM] * (2 * n + 1), out_specs=(_SEM, _SEM, _SEM, *[_HBM] * (2 * n + 1)),
                 input_output_aliases={i: 3 + i for i in range(2 * n + 1)},
                 compiler_params=pltpu.CompilerParams(has_side_effects=_EFFECT))(
        *[pltpu.with_memory_space_constraint(a, pltpu.HBM) for a in (*vs, *lands, thru)])
    return (gather, res[:3], res[3:3 + n], res[3 + n:3 + 2 * n]), res[3 + 2 * n]


def _exchange_wait(name, started, after):
    gather, sems, vs, lands = started
    n = len(vs)

    def body(*refs):
        v_refs, land_refs = refs[:n], refs[n:2 * n]
        send_sems, recv_sems, local_sems = refs[2 * n:2 * n + 3]
        local, remote = _direct_copies(gather, v_refs, land_refs, send_sems, recv_sems, local_sems)
        for cp in local:
            cp.wait()
        for cp in remote:
            cp.wait_send()
            cp.wait_recv()

    hbm = lambda a: pltpu.HBM(a.shape, a.dtype)
    res = _pcall(body, name=name, out_shape=tuple(hbm(a) for a in (*vs, *lands)),
                 in_specs=[_HBM] * (2 * n) + [_SEM] * 3 + [pl.BlockSpec(memory_space=pl.ANY)], out_specs=tuple([_HBM] * (2 * n)),
                 input_output_aliases={i: i for i in range(2 * n)},
                 compiler_params=pltpu.CompilerParams(has_side_effects=_EFFECT))(*vs, *lands, *sems, after)
    return list(res[n:])


W_IN_SHARD = IN_WIDTH // N_DEV
SEG_ORDER = ("q_lat", "c_kv", "k_pe", "z_a", "dn_qkv", "dn_ab", "z_b", "dil_qkv", "z_c", "gate")
SEG_WIDTH = (384, 256, LANES, 512, 1536, LANES, 512, 4608, 512, 3072)


def _w_in_plan():
    plan = []

    def add(seg, c0, c1, dst):
        while c0 < c1:
            d = c0 // W_IN_SHARD
            e = min(c1, (d + 1) * W_IN_SHARD)
            plan.append((seg, dst, d, c0 - d * W_IN_SHARD, e - c0))
            dst += e - c0
            c0 = e

    half = MLA_ROPE // 2
    for i, name in enumerate(SEG_ORDER):
        if name == "k_pe":
            o = _SEG["k_pe"][0]
            add(i, o, o + half, 0)
            add(i, o + half, o + 2 * half, LANES // 2)
        elif name == "dn_ab":
            o = _SEG["dn_a"][0]
            add(i, o, o + 2 * DN_HEADS, 0)
        else:
            o, w = _SEG[name]
            add(i, o, o + w, 0)
    return plan


def _make_w_in_segments(name):
    plan = _w_in_plan()
    nseg = len(SEG_ORDER)
    t = 256

    def fwd_call(g):
        L = g.shape[1]

        def body(g_ref, *o_refs):
            for i in (SEG_ORDER.index("k_pe"), SEG_ORDER.index("dn_ab")):
                o_refs[i][...] = jnp.zeros_like(o_refs[i])
            for seg, dst, d, src, n in plan:
                o_refs[seg][:, dst:dst + n] = g_ref[d, :, src:src + n]

        return _pcall(body, name=f"{name}_fwd", grid=(L, D_MODEL // t),
                      in_specs=[pl.BlockSpec((N_DEV, None, t, W_IN_SHARD), lambda l, i: (0, l, i, 0))],
                      out_specs=[pl.BlockSpec((None, t, w), lambda l, i: (l, i, 0)) for w in SEG_WIDTH],
                      out_shape=[jax.ShapeDtypeStruct((L, D_MODEL, w), g.dtype) for w in SEG_WIDTH],
                      compiler_params=_params(("parallel", "parallel")))(g)

    def bwd_call(ds):
        L = ds[0].shape[0]

        def body(*refs):
            d_refs, g_ref = refs[:nseg], refs[nseg]
            for seg, dst, d, src, n in plan:
                g_ref[d, :, src:src + n] = d_refs[seg][:, dst:dst + n]

        return _pcall(body, name=f"{name}_bwd", grid=(L, D_MODEL // t),
                      in_specs=[pl.BlockSpec((None, t, w), lambda l, i: (l, i, 0)) for w in SEG_WIDTH],
                      out_specs=pl.BlockSpec((N_DEV, None, t, W_IN_SHARD), lambda l, i: (0, l, i, 0)),
                      out_shape=jax.ShapeDtypeStruct((N_DEV, L, D_MODEL, W_IN_SHARD), ds[0].dtype),
                      compiler_params=_params(("parallel", "parallel")))(*ds)

    @jax.custom_vjp
    def op(g):
        return tuple(fwd_call(g))

    op.defvjp(lambda g: (op(g), None), lambda _, ds: (bwd_call(tuple(ds)),))
    return op


def _pe_pad(a):
    h = MLA_ROPE // 2
    z = jnp.zeros(a.shape[:-1] + (h,), a.dtype)
    return jnp.concatenate([a[..., :h], z, a[..., h:], z], axis=-1)


def _layer_norm(tag, x, norm_g):
    return _make_rowwise(f"{tag}_norm", _f_norm, 512)((x,), (), (norm_g[None, :],), ())[0]


def _layer(tag, x, tables, W):
    h = _layer_norm(tag, x, W["norm_g"])
    return _layer_tail(tag, x, _make_multi_linear(f"{tag}_inproj", 10)(h, W["w_in_segments"]), tables, W)


def _layer_tail(tag, x, segments, tables, W):
    cos_p, sin_p, cos_h, sin_h = tables
    row = lambda a: a[None, :]
    q_lat, c_kv, kpe, z_a, dn_qkv, ab, z_b, dil_qkv, z_c, gl = segments

    qn_lat, ckvn, kp = _make_rowwise(f"{tag}_mla_a", _f_mla_a, 512)(
        (q_lat, c_kv, kpe), (cos_p, sin_p),
        (row(W["mla_q_a_norm_g"]), row(W["mla_kv_a_norm_g"]), row(_pe_pad(W["mla_k_norm_g"][LANES:]))), ())
    wq = W["mla_w_q_b"].reshape(MLA_Q_RANK, MLA_HEADS, MLA_QK)
    wq = jnp.concatenate([wq[:, :, :LANES].reshape(MLA_Q_RANK, -1), _pe_pad(wq[:, :, LANES:]).reshape(MLA_Q_RANK, -1)], axis=1)
    wkv = W["mla_w_kv_b"].reshape(MLA_KV_RANK, MLA_HEADS, 2 * LANES)
    (q8,) = _make_multi_linear(f"{tag}_qb", 1)(qn_lat, (wq,))
    kn_raw, v_mla = _make_multi_linear(f"{tag}_kvb", 2)(
        ckvn, (wkv[:, :, :LANES].reshape(MLA_KV_RANK, -1), wkv[:, :, LANES:].reshape(MLA_KV_RANK, -1)))
    qn, qp, kn = _make_rowwise(f"{tag}_mla_b", _f_mla_b, 512)(
        (q8, kn_raw), (cos_p, sin_p),
        (row(W["mla_q_norm_g"][:LANES]), row(_pe_pad(W["mla_q_norm_g"][LANES:])), row(W["mla_k_norm_g"][:LANES])), ())
    y_a = _make_mla_attn(f"{tag}_mla")(qn, qp, kn, kp, v_mla)

    mixed = _make_conv(f"{tag}_conv")(dn_qkv, W["dn_conv_w"])
    lane_head = jnp.arange(DN_HEADS * LANES) // LANES
    e_a = (jnp.arange(LANES)[:, None] == lane_head[None, :]).astype(F32)
    e_b = (jnp.arange(LANES)[:, None] == lane_head[None, :] + DN_HEADS).astype(F32)
    q_dn, k_dn, v_dn, g_dn, b_dn = _make_rowwise(f"{tag}_dn_pre", _f_dn_pre, 512)(
        (mixed, ab), (), (row(jnp.repeat(W["dn_a_log"], LANES)), row(jnp.repeat(W["dn_dt_bias"], LANES))), (e_a, e_b))
    o_dn = _make_delta_rule(f"{tag}_dn")(q_dn, k_dn, v_dn, g_dn, b_dn)

    qkv_dil = _make_rowwise(f"{tag}_dil_pre", _f_dil_pre, 256)(
        (dil_qkv,), (cos_h, sin_h), (row(W["dil_q_norm_g"]), row(W["dil_k_norm_g"])), ())
    n_groups = len(DIL_DILATIONS)
    o_lse = [_make_dil_attn(f"{tag}_dil{g}", d, DIL_TILE_ROWS[g])(qkv_dil[g], qkv_dil[n_groups + g], qkv_dil[2 * n_groups + g])
             for g, d in enumerate(DIL_DILATIONS)]

    ya, yb, yc = _make_rowwise(f"{tag}_merge_a", _f_merge_a, 256)(
        (y_a, z_a, o_dn, z_b, *[o for o, _ in o_lse], *[l for _, l in o_lse], z_c), (), (row(W["dn_out_norm_g"]),), ())
    (b0,) = _make_multi_linear(f"{tag}_br0", 1)(ya, (W["w_branch"][0],))
    (b1,) = _make_multi_linear(f"{tag}_br1", 1)(yb, (W["w_branch"][1],))
    (b2,) = _make_multi_linear(f"{tag}_br2", 1)(yc, (W["w_branch"][2],))
    (mix,) = _make_rowwise(f"{tag}_merge_b", _f_merge_b, 256)((b0, b1, b2, gl), (), (), ())
    return _make_resid_linear(f"{tag}_out")(x, mix, W["w_out"])


SHARDED = (("w_in", (D_MODEL, W_IN_SHARD)), ("mla_w_q_b", (MLA_Q_RANK, MLA_HEADS * MLA_QK // N_DEV)),
           ("mla_w_kv_b", (MLA_KV_RANK, MLA_HEADS * 2 * LANES // N_DEV)), ("w_branch", (3 * BRANCH_W, D_MODEL // N_DEV)),
           ("w_out", (D_MODEL // N_DEV, D_MODEL)), ("dn_conv_w", (DN_CONV, 3 * DN_HEADS * LANES // N_DEV)))
SMALL = (("norm_g", D_MODEL), ("mla_q_a_norm_g", MLA_Q_RANK), ("mla_kv_a_norm_g", MLA_KV_RANK), ("mla_q_norm_g", MLA_QK),
         ("mla_k_norm_g", MLA_QK), ("dn_a_log", DN_HEADS), ("dn_dt_bias", DN_HEADS), ("dn_out_norm_g", LANES),
         ("dil_q_norm_g", LANES), ("dil_k_norm_g", LANES))
WEIGHTS = ("norm_g", "w_in", "mla_q_a_norm_g", "mla_w_q_b", "mla_kv_a_norm_g", "mla_w_kv_b", "mla_q_norm_g", "mla_k_norm_g",
           "dn_conv_w", "dn_a_log", "dn_dt_bias", "dn_out_norm_g", "dil_q_norm_g", "dil_k_norm_g", "w_branch", "w_out")


def _round_up(n, m):
    return -(-n // m) * m


def _pack_vectors(pieces):
    return jnp.concatenate([jnp.pad(p, (0, _round_up(p.shape[0], LANES) - p.shape[0])) for p in pieces]).reshape(-1, LANES)


def _unpack_vectors(flat, sizes):
    out, off = [], 0
    flat = flat.reshape(-1)
    for n in sizes:
        out.append(flat[off:off + n])
        off += _round_up(n, LANES)
    return out


def _whole_weights(g, small):
    W = dict(small)
    W["mla_w_q_b"] = g["mla_w_q_b"].transpose(1, 0, 2).reshape(MLA_Q_RANK, -1)
    W["mla_w_kv_b"] = g["mla_w_kv_b"].transpose(1, 0, 2).reshape(MLA_KV_RANK, -1)
    W["w_branch"] = g["w_branch"].reshape(N_DEV, 3, BRANCH_W, -1).transpose(1, 2, 0, 3).reshape(3, BRANCH_W, D_MODEL)
    W["w_out"] = g["w_out"].reshape(D_MODEL, D_MODEL)
    W["dn_conv_w"] = g["dn_conv_w"].transpose(1, 0, 2).reshape(DN_CONV, -1)
    return W


def kernel(x, positions, norm_g, w_in, mla_q_a_norm_g, mla_w_q_b, mla_kv_a_norm_g, mla_w_kv_b, mla_q_norm_g, mla_k_norm_g, dn_conv_w, dn_a_log, dn_dt_bias, dn_out_norm_g, dil_q_norm_g, dil_k_norm_g, w_branch, w_out, loss_target, m_norm_g, m_w_in, m_mla_q_a_norm_g, m_mla_w_q_b, m_mla_kv_a_norm_g, m_mla_w_kv_b, m_mla_q_norm_g, m_mla_k_norm_g, m_dn_conv_w, m_dn_a_log, m_dn_dt_bias, m_dn_out_norm_g, m_dil_q_norm_g, m_dil_k_norm_g, m_w_branch, m_w_out, v_norm_g, v_w_in, v_mla_q_a_norm_g, v_mla_w_q_b, v_mla_kv_a_norm_g, v_mla_w_kv_b, v_mla_q_norm_g, v_mla_k_norm_g, v_dn_conv_w, v_dn_a_log, v_dn_dt_bias, v_dn_out_norm_g, v_dil_q_norm_g, v_dil_k_norm_g, v_w_branch, v_w_out):
    w = dict(norm_g=norm_g, w_in=w_in, mla_q_a_norm_g=mla_q_a_norm_g, mla_w_q_b=mla_w_q_b, mla_kv_a_norm_g=mla_kv_a_norm_g,
             mla_w_kv_b=mla_w_kv_b, mla_q_norm_g=mla_q_norm_g, mla_k_norm_g=mla_k_norm_g, dn_conv_w=dn_conv_w, dn_a_log=dn_a_log,
             dn_dt_bias=dn_dt_bias, dn_out_norm_g=dn_out_norm_g, dil_q_norm_g=dil_q_norm_g, dil_k_norm_g=dil_k_norm_g,
             w_branch=w_branch, w_out=w_out)
    m = dict(norm_g=m_norm_g, w_in=m_w_in, mla_q_a_norm_g=m_mla_q_a_norm_g, mla_w_q_b=m_mla_w_q_b, mla_kv_a_norm_g=m_mla_kv_a_norm_g,
             mla_w_kv_b=m_mla_w_kv_b, mla_q_norm_g=m_mla_q_norm_g, mla_k_norm_g=m_mla_k_norm_g, dn_conv_w=m_dn_conv_w,
             dn_a_log=m_dn_a_log, dn_dt_bias=m_dn_dt_bias, dn_out_norm_g=m_dn_out_norm_g, dil_q_norm_g=m_dil_q_norm_g,
             dil_k_norm_g=m_dil_k_norm_g, w_branch=m_w_branch, w_out=m_w_out)
    v = dict(norm_g=v_norm_g, w_in=v_w_in, mla_q_a_norm_g=v_mla_q_a_norm_g, mla_w_q_b=v_mla_w_q_b, mla_kv_a_norm_g=v_mla_kv_a_norm_g,
             mla_w_kv_b=v_mla_w_kv_b, mla_q_norm_g=v_mla_q_norm_g, mla_k_norm_g=v_mla_k_norm_g, dn_conv_w=v_dn_conv_w,
             dn_a_log=v_dn_a_log, dn_dt_bias=v_dn_dt_bias, dn_out_norm_g=v_dn_out_norm_g, dil_q_norm_g=v_dil_q_norm_g,
             dil_k_norm_g=v_dil_k_norm_g, w_branch=v_w_branch, w_out=v_w_out)
    x2, target = x[0], loss_target[0]
    pos = positions[0][:, None]

    names = [n for n, _ in SHARDED]
    view = lambda t, n, s: t[n].reshape((DEPTH,) + s)
    shards = [[(view(w, n, s) if n == "dn_conv_w" else view(w, n, s).astype(BF16))[l] for n, s in SHARDED] for l in range(DEPTH)]
    small = [{n: w[n][l] for n, _ in SMALL} for l in range(DEPTH)]
    gathered0 = dict(zip(names, _all_gather(shards[0])))
    gathering1, pos = _exchange_start("gather_layer1_start", shards[1], True, pos)
    tables = _rope_tables(pos, _rope_consts())

    def layer(l, g, small_l, x_l):
        tag = f"l{l}"
        seg_op = _make_w_in_segments(f"{tag}_w_in_segments")
        w_segs, vjp_segs = jax.vjp(lambda gw: tuple(s[0] for s in seg_op(gw[:, None])), g["w_in"])
        h, vjp_norm = jax.vjp(lambda x_, ng: _layer_norm(tag, x_, ng), x_l, small_l["norm_g"])
        segs = tuple(_mm(f"{tag}_inproj_fwd{i}", h, w_, "nn", **INPROJ_TILES["nn"]) for i, w_ in enumerate(w_segs))
        rest_g = {n: a for n, a in g.items() if n != "w_in"}
        rest_s = {n: a for n, a in small_l.items() if n != "norm_g"}
        y, vjp_tail = jax.vjp(lambda sg, x_, gg, ss: _layer_tail(tag, x_, sg, tables, _whole_weights(gg, ss)), segs, x_l, rest_g, rest_s)

        def backward(dy):
            dsegs, dx_skip, d_rest_g, d_rest_s = vjp_tail(dy)
            dws = tuple(_mm(f"{tag}_inproj_dw{i}", h, d, "tn", out_dtype=w_.dtype, **INPROJ_TILES["tn"])
                        for i, (w_, d) in enumerate(zip(w_segs, dsegs)))
            dg = dict(d_rest_g, w_in=vjp_segs(dws)[0])
            exchanging, first = _exchange_start(f"exchange_{tag}_start", [dg[n] for n in names], False, dsegs[0])
            dh = None
            for i, (w_, d) in enumerate(zip(w_segs, (first,) + tuple(dsegs[1:]))):
                dh = _mm(f"{tag}_inproj_dh{i}", d, w_, "nt", acc=dh, out_dtype=h.dtype if i == len(w_segs) - 1 else F32,
                         **INPROJ_TILES["nt"])
            dx_norm, d_norm_g = vjp_norm(dh)
            return exchanging, dx_skip + dx_norm, dict(d_rest_s, norm_g=d_norm_g)
        return y, backward

    y0, backward0 = layer(0, gathered0, small[0], x2)
    gathered1 = dict(zip(names, _exchange_wait("gather_layer1_wait", gathering1, y0)))
    y1, backward1 = layer(1, gathered1, small[1], y0)
    loss_splat, dy = _loss_call(y1, target)
    loss = lax.psum(loss_splat[0, 0], ("x", "y", "c"))
    exchanging1, d_y0, g_small1 = backward1(dy)
    exchanging0, g_x, g_small0 = backward0(d_y0)

    g_small = (g_small0, g_small1)
    sizes = [k for _ in range(DEPTH) for _, k in SMALL]
    g_vec = _pack_vectors([g_small[l][n] for l in range(DEPTH) for n, _ in SMALL])
    (parts_vec,) = _all_to_all([jnp.broadcast_to(g_vec[None], (N_DEV,) + g_vec.shape)])
    parts1 = _exchange_wait("exchange_l1_wait", exchanging1, parts_vec)
    parts0 = _exchange_wait("exchange_l0_wait", exchanging0, parts_vec)

    vec = lambda t: _pack_vectors([t[n][l] for l in range(DEPTH) for n, _ in SMALL])[None]
    outs = {}
    for i, (n, s) in enumerate(SHARDED):
        res = _adamw_call(f"adamw_{n}", [parts0[i], parts1[i]], view(w, n, s), view(m, n, s), view(v, n, s))
        outs[n] = [o.reshape(w[n].shape) for o in res]
    vec_outs = [_unpack_vectors(o, sizes) for o in _adamw_call("adamw_vectors", [parts_vec], vec(w), vec(m), vec(v))]
    for i, (n, _) in enumerate(SMALL):
        outs[n] = [jnp.stack([o[l * len(SMALL) + i] for l in range(DEPTH)]) for o in vec_outs]
    return (loss, g_x[None], *[outs[n][k] for k in range(4) for n in WEIGHTS])
```

```python
import functools
import math

import jax
import jax.numpy as jnp
from jax import lax
from jax.experimental import pallas as pl
from jax.experimental.pallas import tpu as pltpu

F32 = jnp.float32
BF16 = jnp.bfloat16
HI = lax.Precision.HIGHEST
MESH = pl.DeviceIdType.MESH

N_DEV = 8
D_MODEL = 1024
DEPTH = 2
RMS_EPS = 1e-6
ROPE_THETA = 10000.0
LANES = 128
MLA_HEADS = 4
MLA_ROPE = 64
MLA_QK = 192
MLA_Q_RANK = 384
MLA_KV_RANK = 256
DN_HEADS = 4
DN_CHUNK = 64
DN_CONV = 4
DIL_HEADS = 12
DIL_GROUP_HEADS = 4
DIL_DILATIONS = (1, 4, 16)
DIL_BLOCK = 128
BRANCH_W = 512
IN_WIDTH = 11464
NEG = -1e30
VMEM_LIMIT = 56 * 1024 * 1024

ADAM_LR, ADAM_B1, ADAM_B2, ADAM_EPS, ADAM_WD, ADAM_STEP = 0.001, 0.9, 0.999, 1e-08, 0.01, 10

_SEG = {}
_off = 0
for _n, _w in (("q_lat", 384), ("c_kv", 256), ("k_pe", 64), ("z_a", 512), ("dn_qkv", 1536), ("dn_a", 4), ("dn_b", 4),
               ("z_b", 512), ("dil_qkv", 4608), ("z_c", 512), ("gate", 3072)):
    _SEG[_n] = (_off, _w)
    _off += _w
assert _off == IN_WIDTH


def _pcall(body, **kw):
    return pl.pallas_call(body, **kw)


def _params(sem=None):
    return pltpu.CompilerParams(dimension_semantics=sem, vmem_limit_bytes=VMEM_LIMIT)


def _tile(n, target, mult):
    t = (min(n, target) // mult) * mult
    while t >= mult:
        if n % t == 0:
            return t
        t -= mult
    return n


def _mm(name, a, b, mode, out_dtype=F32, acc=None, tm=1024, tn=512, tk=1024):
    if mode == "nn":
        (M, K), (_, N) = a.shape, b.shape
    elif mode == "nt":
        (M, K), (N, _) = a.shape, b.shape
    else:
        (K, M), (_, N) = a.shape, b.shape
    tm, tn, tk = _tile(M, tm, LANES), _tile(N, tn, LANES), _tile(K, tk, LANES)
    nk = K // tk
    dims = {"nn": (((1,), (0,)), ((), ())), "nt": (((1,), (1,)), ((), ())), "tn": (((0,), (0,)), ((), ()))}[mode]
    a_spec = pl.BlockSpec((tk, tm), lambda i, j, k: (k, i)) if mode == "tn" else pl.BlockSpec((tm, tk), lambda i, j, k: (i, k))
    b_spec = pl.BlockSpec((tn, tk), lambda i, j, k: (j, k)) if mode == "nt" else pl.BlockSpec((tk, tn), lambda i, j, k: (k, j))
    o_spec = pl.BlockSpec((tm, tn), lambda i, j, k: (i, j))
    has_acc = acc is not None

    def body(*refs):
        a_ref, b_ref = refs[:2]
        c_ref = refs[2] if has_acc else None
        o_ref = refs[3] if has_acc else refs[2]
        prod = lax.dot_general(a_ref[...].astype(BF16), b_ref[...].astype(BF16), dims, preferred_element_type=F32)
        if nk == 1:
            o_ref[...] = (prod + c_ref[...].astype(F32) if has_acc else prod).astype(out_dtype)
            return
        acc_ref = refs[-1]
        k = pl.program_id(2)

        @pl.when(k == 0)
        def _():
            acc_ref[...] = prod + c_ref[...].astype(F32) if has_acc else prod

        @pl.when(k > 0)
        def _():
            acc_ref[...] += prod

        @pl.when(k == nk - 1)
        def _():
            o_ref[...] = acc_ref[...].astype(out_dtype)

    ins = [a, b] + ([acc] if has_acc else [])
    in_specs = [a_spec, b_spec] + ([o_spec] if has_acc else [])
    return _pcall(body, name=name, grid=(M // tm, N // tn, nk), in_specs=in_specs, out_specs=o_spec,
                  out_shape=jax.ShapeDtypeStruct((M, N), out_dtype), scratch_shapes=[pltpu.VMEM((tm, tn), F32)] if nk > 1 else [],
                  compiler_params=_params(("parallel", "parallel", "arbitrary")))(*ins)


def _mm_nt_sum(name, pairs, acc, out_dtype, tm=512, tn=1024):
    M, N = pairs[0][0].shape[0], pairs[0][1].shape[0]
    tm, tn = _tile(M, tm, LANES), _tile(N, tn, LANES)
    n = len(pairs)
    has_acc = acc is not None
    o_spec = pl.BlockSpec((tm, tn), lambda i, j: (i, j))

    def body(*refs):
        total = refs[2 * n][...].astype(F32) if has_acc else None
        for a_ref, b_ref in zip(refs[:n], refs[n:2 * n]):
            prod = lax.dot_general(a_ref[...].astype(BF16), b_ref[...].astype(BF16), (((1,), (1,)), ((), ())), preferred_element_type=F32)
            total = prod if total is None else total + prod
        refs[-1][...] = total.astype(out_dtype)

    in_specs = ([pl.BlockSpec((tm, a.shape[1]), lambda i, j: (i, 0)) for a, _ in pairs]
                + [pl.BlockSpec((tn, b.shape[1]), lambda i, j: (j, 0)) for _, b in pairs] + ([o_spec] if has_acc else []))
    return _pcall(body, name=name, grid=(M // tm, N // tn), in_specs=in_specs, out_specs=o_spec,
                  out_shape=jax.ShapeDtypeStruct((M, N), out_dtype), compiler_params=_params(("parallel", "parallel")))(
        *[a for a, _ in pairs], *[b for _, b in pairs], *([acc] if has_acc else []))


INPROJ_DH_GROUPS = ((0, 1, 2, 3, 5, 6, 8), (4,), (9,), (7,))

INPROJ_TILES = {"nn": dict(tm=1024, tn=1536, tk=1024), "tn": dict(tm=1024, tn=768, tk=2048)}


def _make_multi_linear(name, n):
    @jax.custom_vjp
    def op(h, ws):
        return tuple(_mm(f"{name}_fwd{i}", h, w, "nn") for i, w in enumerate(ws))

    def fwd(h, ws):
        return op(h, ws), (h, ws)

    def bwd(res, douts):
        h, ws = res
        dh = None
        for i, (w, d) in enumerate(zip(ws, douts)):
            dh = _mm(f"{name}_dh{i}", d, w, "nt", acc=dh, out_dtype=h.dtype if i == len(ws) - 1 else F32)
        dws = tuple(_mm(f"{name}_dw{i}", h, d, "tn", out_dtype=w.dtype) for i, (w, d) in enumerate(zip(ws, douts)))
        return dh, dws

    op.defvjp(fwd, bwd)
    return op


def _make_resid_linear(name):
    @jax.custom_vjp
    def op(x, a, w):
        return _mm(f"{name}_fwd", a, w, "nn", acc=x)

    def fwd(x, a, w):
        return op(x, a, w), (a, w)

    def bwd(res, dy):
        a, w = res
        return dy, _mm(f"{name}_da", dy, w, "nt", out_dtype=a.dtype), _mm(f"{name}_dw", a, dy, "tn", out_dtype=w.dtype)

    op.defvjp(fwd, bwd)
    return op


def _make_rowwise(name, f, tile):
    def specs(rows, aux, params, consts, t):
        row = [pl.BlockSpec((t, a.shape[1]), lambda i: (i, 0)) for a in (*rows, *aux)]
        full = [pl.BlockSpec(p.shape, lambda i: (0, 0)) for p in (*params, *consts)]
        return row, full

    def fwd_call(rows, aux, params, consts):
        S = rows[0].shape[0]
        t = min(tile, S)
        n_in = len(rows) + len(aux) + len(params) + len(consts)
        shp = lambda a: jax.ShapeDtypeStruct((t, a.shape[1]), a.dtype)
        outs = jax.eval_shape(f, *[shp(a) for a in (*rows, *aux)], *params, *consts)
        row_specs, full_specs = specs(rows, aux, params, consts, t)

        def body(*refs):
            res = f(*[r[...] for r in refs[:n_in]])
            for o_ref, o in zip(refs[n_in:], res):
                o_ref[...] = o

        return _pcall(body, name=f"{name}_fwd", grid=(S // t,), in_specs=row_specs + full_specs,
                      out_specs=[pl.BlockSpec((t, o.shape[1]), lambda i: (i, 0)) for o in outs],
                      out_shape=[jax.ShapeDtypeStruct((S, o.shape[1]), o.dtype) for o in outs],
                      compiler_params=_params(("parallel",)))(*rows, *aux, *params, *consts)

    def bwd_call(rows, aux, params, consts, douts):
        S = rows[0].shape[0]
        t = min(tile, S)
        nr, na, npar, nc, nd = len(rows), len(aux), len(params), len(consts), len(douts)
        row_specs, full_specs = specs(rows, aux, params, consts, t)

        def body(*refs):
            vals = [r[...] for r in refs[:nr + na + npar + nc]]
            rv, av = vals[:nr], vals[nr:nr + na]
            pv, cv = vals[nr + na:nr + na + npar], vals[nr + na + npar:]
            dv = tuple(r[...] for r in refs[nr + na + npar + nc:nr + na + npar + nc + nd])
            out_refs = refs[nr + na + npar + nc + nd:]
            _, vjp = jax.vjp(lambda *rp: f(*rp[:nr], *av, *rp[nr:], *cv), *rv, *pv)
            grads = vjp(dv)
            for o_ref, g in zip(out_refs[:nr], grads[:nr]):
                o_ref[...] = g
            first = pl.program_id(0) == 0
            for o_ref, g in zip(out_refs[nr:], grads[nr:]):
                @pl.when(first)
                def _(o_ref=o_ref):
                    o_ref[...] = jnp.zeros_like(o_ref)
                o_ref[...] += g

        res = _pcall(body, name=f"{name}_bwd", grid=(S // t,),
                     in_specs=row_specs + full_specs + [pl.BlockSpec((t, d.shape[1]), lambda i: (i, 0)) for d in douts],
                     out_specs=[pl.BlockSpec((t, a.shape[1]), lambda i: (i, 0)) for a in rows]
                     + [pl.BlockSpec(p.shape, lambda i: (0, 0)) for p in params],
                     out_shape=[jax.ShapeDtypeStruct(a.shape, a.dtype) for a in (*rows, *params)],
                     compiler_params=_params(("arbitrary",)))(*rows, *aux, *params, *consts, *douts)
        return tuple(res[:nr]), tuple(res[nr:])

    @jax.custom_vjp
    def op(rows, aux, params, consts):
        return tuple(fwd_call(rows, aux, params, consts))

    def fwd(rows, aux, params, consts):
        return op(rows, aux, params, consts), (rows, aux, params, consts)

    def bwd(res, douts):
        rows, aux, params, consts = res
        drows, dparams = bwd_call(rows, aux, params, consts, tuple(douts))
        zeros = lambda xs: tuple(jnp.zeros_like(a) for a in xs)
        return drows, zeros(aux), dparams, zeros(consts)

    op.defvjp(fwd, bwd)
    return op


@jax.custom_vjp
def _swap_halves(x):
    return pltpu.roll(x, LANES // 2, 1)


_swap_halves.defvjp(lambda x: (_swap_halves(x), None), lambda _, g: (_swap_halves(g),))


def _rope(x, cos_t, sin_t):
    return x * cos_t + _swap_halves(x) * sin_t


def _rms(x, g, n=None):
    n = x.shape[-1] if n is None else n
    return x * lax.rsqrt(jnp.sum(x * x, axis=-1, keepdims=True) * (1.0 / n) + RMS_EPS) * g


def _heads(x):
    return [x[:, i * LANES:(i + 1) * LANES] for i in range(x.shape[1] // LANES)]


def _cat(xs):
    return jnp.concatenate(xs, axis=1)


def _silu(x):
    return x * jax.nn.sigmoid(x)


def _f_norm(x, g):
    return (_rms(x, g).astype(BF16),)


def _f_mla_a(q_lat, c_kv, kpe, cos_p, sin_p, qa_g, kva_g, kpe_g):
    kp = _rope(_rms(kpe, kpe_g, MLA_ROPE), cos_p, sin_p)
    return _rms(q_lat, qa_g).astype(BF16), _rms(c_kv, kva_g).astype(BF16), _cat([kp] * MLA_HEADS)


def _f_mla_b(q8, kn_raw, cos_p, sin_p, qn_g, qp_g, kn_g):
    hs = _heads(q8)
    qn = _cat([_rms(h, qn_g) for h in hs[:MLA_HEADS]])
    qp = _cat([_rope(_rms(h, qp_g, MLA_ROPE), cos_p, sin_p) for h in hs[MLA_HEADS:]])
    kn = _cat([_rms(h, kn_g) for h in _heads(kn_raw)])
    return qn, qp, kn


def _softplus(x):
    return jnp.maximum(x, 0.0) + jnp.log(1.0 + jnp.exp(-jnp.abs(x)))


def _l2n(x):
    return x * lax.rsqrt(jnp.sum(x * x, axis=-1, keepdims=True) + 1e-6)


def _f_dn_pre(mixed, ab, alog_f, dtb_f, e_a, e_b):
    hs = _heads(mixed)
    q = _cat([_l2n(h) * (LANES ** -0.5) for h in hs[:DN_HEADS]])
    k = _cat([_l2n(h) for h in hs[DN_HEADS:2 * DN_HEADS]])
    v = _cat(hs[2 * DN_HEADS:])
    a_f = jnp.dot(ab, e_a, precision=HI, preferred_element_type=F32)
    b_f = jnp.dot(ab, e_b, precision=HI, preferred_element_type=F32)
    g = -jnp.exp(alog_f) * _softplus(a_f + dtb_f)
    return q, k, v, g, jax.nn.sigmoid(b_f)


def _f_dil_pre(qkv, cos_h, sin_h, q_g, k_g):
    hs = _heads(qkv)
    q = [_rope(_rms(h, q_g), cos_h, sin_h) for h in hs[:DIL_HEADS]]
    k = [_rope(_rms(h, k_g), cos_h, sin_h) for h in hs[DIL_HEADS:2 * DIL_HEADS]]
    v = hs[2 * DIL_HEADS:]
    group = lambda xs, g: _cat(xs[g * DIL_GROUP_HEADS:(g + 1) * DIL_GROUP_HEADS])
    return tuple(group(xs, g) for xs in (q, k, v) for g in range(len(DIL_DILATIONS)))


def _f_merge_a(y_a, z_a, o_dn, z_b, o0, o1, o2, l0, l1, l2, z_c, out_g):
    y_b = _cat([_rms(h, out_g) for h in _heads(o_dn)])
    os_, ls = [_heads(o) for o in (o0, o1, o2)], [_heads(l) for l in (l0, l1, l2)]
    y_c = []
    for j in range(DIL_GROUP_HEADS):
        l3 = [ls[g][j] for g in range(3)]
        m = jnp.maximum(jnp.maximum(l3[0], l3[1]), l3[2])
        e3 = [jnp.exp(l - m) for l in l3]
        den = e3[0] + e3[1] + e3[2]
        y_c.append(sum(e3[g] * os_[g][j] for g in range(3)) / den)
    return tuple(t.astype(BF16) for t in (y_a * _silu(z_a), y_b * _silu(z_b), _cat(y_c) * _silu(z_c)))


def _f_merge_b(b0, b1, b2, gl):
    gs = [jax.nn.sigmoid(gl[:, i * D_MODEL:(i + 1) * D_MODEL]) for i in range(3)]
    return ((gs[0] * b0 + gs[1] * b1 + gs[2] * b2).astype(BF16),)


def _rope_tables(pos, inv_sign):
    S = pos.shape[0]
    t = min(S, 1024)

    def body(p_ref, c_ref, cp, sp, ch, sh):
        p = p_ref[...].astype(F32)
        c = c_ref[...]
        ang_p, ang_h = p * c[0:1], p * c[2:3]
        cp[...] = jnp.cos(ang_p) * jnp.abs(c[1:2])
        sp[...] = jnp.sin(ang_p) * c[1:2]
        ch[...] = jnp.cos(ang_h)
        sh[...] = jnp.sin(ang_h) * c[3:4]

    row = pl.BlockSpec((t, LANES), lambda i: (i, 0))
    return _pcall(body, name="rope_tables", grid=(S // t,),
                  in_specs=[pl.BlockSpec((t, 1), lambda i: (i, 0)), pl.BlockSpec((4, LANES), lambda i: (0, 0))],
                  out_specs=[row] * 4, out_shape=[jax.ShapeDtypeStruct((S, LANES), F32)] * 4,
                  compiler_params=_params(("parallel",)))(pos, inv_sign)


def _rope_consts():
    half_p, half_h = MLA_ROPE // 2, LANES // 2
    inv_p = 1.0 / (ROPE_THETA ** (jnp.arange(0, MLA_ROPE, 2, dtype=F32) / MLA_ROPE))
    inv_h = 1.0 / (ROPE_THETA ** (jnp.arange(0, LANES, 2, dtype=F32) / LANES))
    z = jnp.zeros((half_p,), F32)
    o = jnp.ones((half_p,), F32)
    return jnp.stack([jnp.concatenate([inv_p, z, inv_p, z]), jnp.concatenate([-o, z, o, z]),
                      jnp.concatenate([inv_h, inv_h]), jnp.concatenate([-jnp.ones((half_h,), F32), jnp.ones((half_h,), F32)])])


def _shift_rows(x, s, up):
    n = x.shape[0]
    r = lax.broadcasted_iota(jnp.int32, x.shape, 0)
    if up:
        return jnp.where(r < n - s, pltpu.roll(x, n - s, 0), 0.0)
    return jnp.where(r >= s, pltpu.roll(x, s, 0), 0.0)


def _make_shift(s):
    @jax.custom_vjp
    def sh(x):
        return _shift_rows(x, s, False)

    sh.defvjp(lambda x: (sh(x), None), lambda _, g: (_shift_rows(g, s, True),))
    return sh


def _f_conv(x, w):
    y = x * w[DN_CONV - 1:DN_CONV]
    for j in range(DN_CONV - 1):
        y = y + _make_shift(DN_CONV - 1 - j)(x) * w[j:j + 1]
    return _silu(y)


def _make_conv(name):
    def call(x, w, dy=None):
        S, C = x.shape
        col = pl.BlockSpec((S, LANES), lambda i: (0, i))
        wsp = pl.BlockSpec((DN_CONV, LANES), lambda i: (0, i))
        if dy is None:
            def body(x_ref, w_ref, o_ref):
                o_ref[...] = _f_conv(x_ref[...], w_ref[...])
            return _pcall(body, name=f"{name}_fwd", grid=(C // LANES,), in_specs=[col, wsp], out_specs=col,
                          out_shape=jax.ShapeDtypeStruct(x.shape, F32), compiler_params=_params(("parallel",)))(x, w)

        def body(x_ref, w_ref, dy_ref, dx_ref, dw_ref):
            _, vjp = jax.vjp(_f_conv, x_ref[...], w_ref[...])
            dx_ref[...], dw_ref[...] = vjp(dy_ref[...])
        return _pcall(body, name=f"{name}_bwd", grid=(C // LANES,), in_specs=[col, wsp, col], out_specs=[col, wsp],
                      out_shape=[jax.ShapeDtypeStruct(x.shape, F32), jax.ShapeDtypeStruct(w.shape, F32)],
                      compiler_params=_params(("parallel",)))(x, w, dy)

    @jax.custom_vjp
    def op(x, w):
        return call(x, w)

    op.defvjp(lambda x, w: (op(x, w), (x, w)), lambda res, dy: tuple(call(*res, dy)))
    return op


def _dot_nt(a, b):
    return lax.dot_general(a.astype(BF16), b.astype(BF16), (((1,), (1,)), ((), ())), preferred_element_type=F32)


def _dot_nn(a, b):
    return jnp.dot(a.astype(BF16), b.astype(BF16), preferred_element_type=F32)


def _dot_tn(a, b):
    return lax.dot_general(a.astype(BF16), b.astype(BF16), (((0,), (0,)), ((), ())), preferred_element_type=F32)


def _mla_scores(qn_r, qp_r, kn_r, kp_r, diagonal):
    scale = MLA_QK ** -0.5
    s = _dot_nt(qn_r[...] * scale, kn_r[...]) + _dot_nt(qp_r[...] * scale, kp_r[...])
    if diagonal:
        r = lax.broadcasted_iota(jnp.int32, s.shape, 0)
        c = lax.broadcasted_iota(jnp.int32, s.shape, 1)
        s = jnp.where(c <= r, s, NEG)
    return s


def _on_causal_pairs(qi, ki, step):
    @pl.when(ki < qi)
    def _():
        step(False)

    @pl.when(ki == qi)
    def _():
        step(True)


def _causal_pairs(n, t, by_key):
    pairs = [(q, k) for k in range(n) for q in range(k, n)] if by_key else [(q, k) for q in range(n) for k in range(q + 1)]
    qt, kt = (jnp.array([p[i] for p in pairs], jnp.int32) for i in (0, 1))
    return (qt, kt, pl.BlockSpec((t, LANES), lambda h, p, qt_r, kt_r: (qt_r[p], h)),
            pl.BlockSpec((t, LANES), lambda h, p, qt_r, kt_r: (kt_r[p], h)))


def _make_mla_attn(name):
    scale = MLA_QK ** -0.5

    def fwd_call(qn, qp, kn, kp, v):
        S = qn.shape[0]
        t = min(S, 512)
        n = S // t
        qt, kt, qs, ks = _causal_pairs(n, t, by_key=False)

        def body(qt_r, kt_r, qn_r, qp_r, kn_r, kp_r, v_r, o_r, lse_r, m_s, l_s, acc_s):
            qi, ki = qt_r[pl.program_id(1)], kt_r[pl.program_id(1)]

            @pl.when(ki == 0)
            def _():
                m_s[...] = jnp.full_like(m_s, NEG)
                l_s[...] = jnp.zeros_like(l_s)
                acc_s[...] = jnp.zeros_like(acc_s)

            def step(diagonal):
                s = _mla_scores(qn_r, qp_r, kn_r, kp_r, diagonal)
                m_old = m_s[...]
                m_new = jnp.maximum(m_old, jnp.max(s, axis=-1, keepdims=True))
                p = jnp.exp(s - m_new[:, :1])
                alpha = jnp.exp(m_old - m_new)
                l_s[...] = alpha * l_s[...] + jnp.sum(p, axis=-1, keepdims=True)
                acc_s[...] = alpha * acc_s[...] + _dot_nn(p, v_r[...])
                m_s[...] = m_new
            _on_causal_pairs(qi, ki, step)

            @pl.when(ki == qi)
            def _():
                o_r[...] = acc_s[...] / l_s[...]
                lse_r[...] = m_s[...] + jnp.log(l_s[...])

        spec = pltpu.PrefetchScalarGridSpec(num_scalar_prefetch=2, grid=(MLA_HEADS, qt.shape[0]), in_specs=[qs, qs, ks, ks, ks],
                                            out_specs=[qs, qs], scratch_shapes=[pltpu.VMEM((t, LANES), F32)] * 3)
        return _pcall(body, name=f"{name}_fwd", grid_spec=spec, out_shape=[jax.ShapeDtypeStruct((S, MLA_HEADS * LANES), F32)] * 2,
                      compiler_params=_params(("parallel", "arbitrary")))(qt, kt, qn, qp, kn, kp, v)

    def bwd_call(qn, qp, kn, kp, v, o, lse, do):
        S = qn.shape[0]
        t = min(S, 512)
        n = S // t
        qt, kt, qs, ks = _causal_pairs(n, t, by_key=True)
        head = pl.BlockSpec((S, LANES), lambda h, p, qt_r, kt_r: (0, h))
        n_pairs = qt.shape[0]

        def body(qt_r, kt_r, qn_r, qp_r, kn_r, kp_r, v_r, o_r, lse_r, do_r, dqn_r, dqp_r, dkn_r, dkp_r, dv_r, dkn_s, dkp_s, dv_s, dl_s):
            pair = pl.program_id(1)
            qi, ki = qt_r[pair], kt_r[pair]
            rows = pl.ds(pl.multiple_of(qi * t, t), t)

            @pl.when(pair == 0)
            def _():
                dqn_r[...] = jnp.zeros_like(dqn_r)
                dqp_r[...] = jnp.zeros_like(dqp_r)

            @pl.when(ki == 0)
            def _():
                dl_s[rows, :] = jnp.broadcast_to(jnp.sum(do_r[...] * o_r[...], axis=-1, keepdims=True), (t, LANES))

            @pl.when(qi == ki)
            def _():
                dkn_s[...] = jnp.zeros_like(dkn_s)
                dkp_s[...] = jnp.zeros_like(dkp_s)
                dv_s[...] = jnp.zeros_like(dv_s)

            def step(diagonal):
                p = jnp.exp(_mla_scores(qn_r, qp_r, kn_r, kp_r, diagonal) - lse_r[...][:, :1])
                ds = p * (_dot_nt(do_r[...], v_r[...]) - dl_s[rows, :][:, :1])
                dv_s[...] += _dot_tn(p, do_r[...])
                dkn_s[...] += _dot_tn(ds, qn_r[...] * scale)
                dkp_s[...] += _dot_tn(ds, qp_r[...] * scale)
                dqn_r[rows, :] += _dot_nn(ds, kn_r[...])
                dqp_r[rows, :] += _dot_nn(ds, kp_r[...])
            _on_causal_pairs(qi, ki, step)

            @pl.when(qi == n - 1)
            def _():
                dkn_r[...] = dkn_s[...]
                dkp_r[...] = dkp_s[...]
                dv_r[...] = dv_s[...]

            @pl.when(pair == n_pairs - 1)
            def _():
                dqn_r[...] = dqn_r[...] * scale
                dqp_r[...] = dqp_r[...] * scale

        spec = pltpu.PrefetchScalarGridSpec(num_scalar_prefetch=2, grid=(MLA_HEADS, n_pairs), in_specs=[qs, qs, ks, ks, ks, qs, qs, qs],
                                            out_specs=[head, head, ks, ks, ks],
                                            scratch_shapes=[pltpu.VMEM((t, LANES), F32)] * 3 + [pltpu.VMEM((S, LANES), F32)])
        return _pcall(body, name=f"{name}_bwd", grid_spec=spec, out_shape=[jax.ShapeDtypeStruct((S, MLA_HEADS * LANES), F32)] * 5,
                      compiler_params=_params(("parallel", "arbitrary")))(qt, kt, qn, qp, kn, kp, v, o, lse, do)

    @jax.custom_vjp
    def op(qn, qp, kn, kp, v):
        return fwd_call(qn, qp, kn, kp, v)[0]

    def fwd(qn, qp, kn, kp, v):
        o, lse = fwd_call(qn, qp, kn, kp, v)
        return o, (qn, qp, kn, kp, v, o, lse)

    def bwd(res, do):
        return tuple(bwd_call(*res, do))

    op.defvjp(fwd, bwd)
    return op


def _dil_block(q, kp, kc, vp, vc, has_prev):
    scale = LANES ** -0.5
    r = lax.broadcasted_iota(jnp.int32, (DIL_BLOCK, 2 * DIL_BLOCK), 0)
    c = lax.broadcasted_iota(jnp.int32, (DIL_BLOCK, 2 * DIL_BLOCK), 1)
    valid = ((c < DIL_BLOCK) & (c >= r) & has_prev) | ((c >= DIL_BLOCK) & (c - DIL_BLOCK <= r))
    s = jnp.where(valid, _dot_nt(q * scale, jnp.concatenate([kp, kc], axis=0)), NEG)
    m = jnp.max(s, axis=-1, keepdims=True)
    e = jnp.exp(s - m)
    den = jnp.sum(e, axis=-1, keepdims=True)
    o = _dot_nn(e, jnp.concatenate([vp, vc], axis=0)) / den
    return o, jnp.broadcast_to(m + jnp.log(den), o.shape)


DIL_TILE_ROWS = (1024, 1024, 2048)


def _make_dil_attn(name, d, tile_rows):
    def call(q, k, v, cts=None):
        S = q.shape[0]
        span = DIL_BLOCK * d
        G = max(1, min(tile_rows, S) // span)
        n = S // (G * span)
        at = (lambda i: i) if cts is None else (lambda i: n - 1 - i)
        tile = pl.BlockSpec((G * span, LANES), lambda h, i: (at(i), h))
        before = pl.BlockSpec((span, LANES), lambda h, i: (jnp.maximum(at(i) * G - 1, 0), h))

        def rows(r, j):
            return pl.ds(j * DIL_BLOCK, DIL_BLOCK) if d == 1 else pl.ds(r + j * span, DIL_BLOCK, stride=d)

        def over_residues(fn):
            if d == 1:
                fn(0)
            else:
                lax.fori_loop(0, d, lambda r, c: (fn(r), c)[1], 0)

        def block_inputs(r, j, q_r, kb_r, k_r, vb_r, v_r):
            kp = kb_r[rows(r, 0), :] if j == 0 else k_r[rows(r, j - 1), :]
            vp = vb_r[rows(r, 0), :] if j == 0 else v_r[rows(r, j - 1), :]
            return q_r[rows(r, j), :], kp, k_r[rows(r, j), :], vp, v_r[rows(r, j), :]

        if cts is None:
            def body(q_r, kb_r, k_r, vb_r, v_r, o_r, lse_r):
                first = at(pl.program_id(1)) * G

                def residue(r):
                    for j in range(G):
                        o_r[rows(r, j), :], lse_r[rows(r, j), :] = _dil_block(
                            *block_inputs(r, j, q_r, kb_r, k_r, vb_r, v_r), first + j > 0)
                over_residues(residue)
            return _pcall(body, name=f"{name}_fwd", grid=(DIL_GROUP_HEADS, n), in_specs=[tile, before, tile, before, tile],
                          out_specs=[tile, tile], out_shape=[jax.ShapeDtypeStruct(q.shape, F32)] * 2,
                          compiler_params=_params(("parallel", "parallel")))(q, k, k, v, v)

        def body(q_r, kb_r, k_r, vb_r, v_r, do_r, dl_r, dq_r, dk_r, dv_r, ck_s, cv_s):
            first = at(pl.program_id(1)) * G

            @pl.when(pl.program_id(1) == 0)
            def _():
                ck_s[...] = jnp.zeros_like(ck_s)
                cv_s[...] = jnp.zeros_like(cv_s)

            def residue(r):
                owed = None
                for j in range(G):
                    hp = first + j > 0
                    _, vjp = jax.vjp(lambda *a: _dil_block(*a, hp), *block_inputs(r, j, q_r, kb_r, k_r, vb_r, v_r))
                    dq, dkp, dkc, dvp, dvc = vjp((do_r[rows(r, j), :], dl_r[rows(r, j), :]))
                    dq_r[rows(r, j), :] = dq
                    if j == G - 1:
                        dkc, dvc = dkc + ck_s[rows(r, 0), :], dvc + cv_s[rows(r, 0), :]
                    dk_r[rows(r, j), :], dv_r[rows(r, j), :] = dkc, dvc
                    if j == 0:
                        owed = (dkp, dvp)
                    else:
                        dk_r[rows(r, j - 1), :] += dkp
                        dv_r[rows(r, j - 1), :] += dvp
                ck_s[rows(r, 0), :], cv_s[rows(r, 0), :] = owed
            over_residues(residue)
        return _pcall(body, name=f"{name}_bwd", grid=(DIL_GROUP_HEADS, n), in_specs=[tile, before, tile, before, tile, tile, tile],
                      out_specs=[tile] * 3, out_shape=[jax.ShapeDtypeStruct(q.shape, F32)] * 3,
                      scratch_shapes=[pltpu.VMEM((span, LANES), F32)] * 2,
                      compiler_params=_params(("parallel", "arbitrary")))(q, k, k, v, v, *cts)

    @jax.custom_vjp
    def op(q, k, v):
        return tuple(call(q, k, v))

    op.defvjp(lambda q, k, v: (op(q, k, v), (q, k, v)), lambda res, cts: tuple(call(*res, cts=cts)))
    return op


def _pdot(a, b, dims):
    return lax.dot_general(a, b, (dims, ((), ())), precision=lax.Precision.HIGH, preferred_element_type=F32)


DN_LOCAL_CHUNKS = 4


DN_BLOCK_HEADS = 4
DN_BLOCK = DN_BLOCK_HEADS * DN_CHUNK


def _inverse_cotangent(inv, d):
    return -_pdot(inv, _pdot(d, inv, ((1,), (1,))), ((0,), (0,)))


@jax.custom_vjp
def _unit_lower_inverse(a):
    n = a.shape[0]
    eye = (lax.broadcasted_iota(jnp.int32, (n, n), 0) == lax.broadcasted_iota(jnp.int32, (n, n), 1)).astype(F32)
    inv, pw = eye - a, a
    for _ in range(5):
        pw = _pdot(pw, pw, ((1,), (0,)))
        inv = inv + _pdot(inv, pw, ((1,), (0,)))
    return inv


def _unit_lower_inverse_fwd(a):
    inv = _unit_lower_inverse(a)
    return inv, inv


_unit_lower_inverse.defvjp(_unit_lower_inverse_fwd, lambda inv, d: (_inverse_cotangent(inv, d),))


@jax.custom_vjp
def _known_inverse(a, inv):
    return inv


_known_inverse.defvjp(lambda a, inv: (inv, inv), lambda inv, d: (_inverse_cotangent(inv, d), jnp.zeros_like(inv)))


def _dn_local(q, k, v, g, b, known=None):
    C, R = DN_CHUNK, DN_BLOCK
    r = lax.broadcasted_iota(jnp.int32, (R, R), 0)
    c = lax.broadcasted_iota(jnp.int32, (R, R), 1)
    same_head = (r // C) == (c // C)
    incl, strict = same_head & (r >= c), same_head & (r > c)
    avg = jnp.full((R, LANES), 1.0 / LANES, F32)
    rc = lax.broadcasted_iota(jnp.int32, (C, C), 0) >= lax.broadcasted_iota(jnp.int32, (C, C), 1)
    gc_lanes = _pdot(rc.astype(F32), g, ((1,), (0,)))
    us, ws, qes, kds, qks, invs = [], [], [], [], [], []
    for first in range(0, DN_HEADS, DN_BLOCK_HEADS):
        stack = lambda x: jnp.concatenate(_heads(x)[first:first + DN_BLOCK_HEADS], axis=0)
        unstack = lambda x: [x[p * C:(p + 1) * C] for p in range(DN_BLOCK_HEADS)]
        gc, q_s, k_s, v_s, b_s = (stack(x) for x in (gc_lanes, q, k, v, b))
        gc_j = _pdot(avg, gc, ((1,), (1,)))
        decay = jnp.exp(jnp.where(incl, _cat([gc] * (R // LANES)) - gc_j, NEG))
        kb = k_s * b_s
        kk = _pdot(jnp.concatenate([kb, q_s], axis=0), k_s, ((1,), (1,)))
        a = jnp.where(strict, kk[:R] * decay, 0.0)
        inv = _unit_lower_inverse(a) if known is None else _known_inverse(a, known[len(invs) * R:(len(invs) + 1) * R])
        invs.append(inv)
        eg = jnp.exp(gc)
        uw = _pdot(inv, _cat([v_s * b_s, kb * eg]), ((1,), (0,)))
        g_last = jnp.concatenate([jnp.broadcast_to(x[C - 1:C], (C, LANES)) for x in unstack(gc)], axis=0)
        us += unstack(uw[:, :LANES])
        ws += unstack(uw[:, LANES:])
        qes += unstack(q_s * eg)
        kds += unstack(k_s * jnp.exp(g_last - gc))
        qks.append(kk[R:] * decay)
    egl = jnp.broadcast_to(jnp.exp(gc_lanes[C - 1:C]), (8, DN_HEADS * LANES))
    return _cat(us), _cat(ws), _cat(qes), _cat(kds), jnp.concatenate(qks, axis=0), egl, jnp.concatenate(invs, axis=0)


def _dn_scan(u, w, qe, kd, qk, egl, state):
    C = DN_CHUNK
    heads = [slice(h * LANES, (h + 1) * LANES) for h in range(DN_HEADS)]
    ws = [_pdot(jnp.concatenate([w[:, sl], qe[:, sl]], axis=0), state[sl, :], ((1,), (0,))) for sl in heads]
    v_new = [u[:, sl] - x[:C] for sl, x in zip(heads, ws)]
    local = []
    for i, first in enumerate(range(0, DN_HEADS, DN_BLOCK_HEADS)):
        y = _pdot(qk[i * DN_BLOCK:(i + 1) * DN_BLOCK], jnp.concatenate(v_new[first:first + DN_BLOCK_HEADS], axis=0), ((1,), (0,)))
        local += [y[p * C:(p + 1) * C] for p in range(DN_BLOCK_HEADS)]
    o = _cat([x[C:] + y for x, y in zip(ws, local)])
    states = [state[sl, :] * egl[0:1, sl] + _pdot(kd[:, sl], vn, ((0,), (0,))) for sl, vn in zip(heads, v_new)]
    return o, jnp.concatenate(states, axis=0)


def _make_delta_rule(name):
    W = DN_HEADS * LANES
    QK = DN_HEADS * DN_CHUNK

    def local_call(ins, cts=None):
        S = ins[0].shape[0]
        n = S // DN_CHUNK
        per = math.gcd(DN_LOCAL_CHUNKS, n)
        row = pl.BlockSpec((per * DN_CHUNK, W), lambda i: (i, 0))
        qkb = pl.BlockSpec((per * QK, DN_BLOCK), lambda i: (i, 0))
        eg = pl.BlockSpec((per, 8, W), lambda i: (i, 0, 0))
        rows = lambda j: slice(j * DN_CHUNK, (j + 1) * DN_CHUNK)
        qk_rows = lambda j: slice(j * QK, (j + 1) * QK)
        out_rows = [rows, rows, rows, rows, qk_rows]

        if cts is None:
            def body(*refs):
                for j in range(per):
                    res = _dn_local(*[r[rows(j), :] for r in refs[:5]])
                    for o_r, o, at_ in zip(refs[5:10], res[:5], out_rows):
                        o_r[at_(j), :] = o
                    refs[10][j] = res[5]
                    refs[11][qk_rows(j), :] = res[6]
            blockdiag = jax.ShapeDtypeStruct((n * QK, DN_BLOCK), F32)
            return _pcall(body, name=f"{name}_local_fwd", grid=(n // per,), in_specs=[row] * 5, out_specs=[row] * 4 + [qkb, eg, qkb],
                          out_shape=[jax.ShapeDtypeStruct((S, W), F32)] * 4 + [blockdiag, jax.ShapeDtypeStruct((n, 8, W), F32), blockdiag],
                          compiler_params=_params(("parallel",)))(*ins)

        def body(*refs):
            for j in range(per):
                known = refs[5][qk_rows(j), :]
                _, vjp = jax.vjp(lambda *a: _dn_local(*a, known=known)[:6], *[r[rows(j), :] for r in refs[:5]])
                grads = vjp(tuple(r[at_(j), :] for r, at_ in zip(refs[6:11], out_rows)) + (refs[11][j],))
                for o_r, o in zip(refs[12:], grads):
                    o_r[rows(j), :] = o
        return _pcall(body, name=f"{name}_local_bwd", grid=(n // per,), in_specs=[row] * 5 + [qkb] + [row] * 4 + [qkb, eg],
                      out_specs=[row] * 5, out_shape=[jax.ShapeDtypeStruct((S, W), F32)] * 5,
                      compiler_params=_params(("parallel",)))(*ins, *cts)

    def scan_call(ins, saved=None, do=None):
        S = ins[0].shape[0]
        n = S // DN_CHUNK
        at = (lambda i: i) if do is None else (lambda i: n - 1 - i)
        row = pl.BlockSpec((DN_CHUNK, W), lambda i: (at(i), 0))
        qkb = pl.BlockSpec((QK, DN_BLOCK), lambda i: (at(i), 0))
        eg = pl.BlockSpec((None, 8, W), lambda i: (at(i), 0, 0))
        st = pl.BlockSpec((None, W, LANES), lambda i: (at(i), 0, 0))

        if do is None:
            def body(*refs):
                o_r, st_r, s_s = refs[6:]

                @pl.when(pl.program_id(0) == 0)
                def _():
                    s_s[...] = jnp.zeros_like(s_s)
                st_r[...] = s_s[...]
                o_r[...], s_s[...] = _dn_scan(*[r[...] for r in refs[:6]], s_s[...])
            return _pcall(body, name=f"{name}_scan_fwd", grid=(n,), in_specs=[row] * 4 + [qkb, eg], out_specs=[row, st],
                          out_shape=[jax.ShapeDtypeStruct((S, W), F32), jax.ShapeDtypeStruct((n, W, LANES), F32)],
                          scratch_shapes=[pltpu.VMEM((W, LANES), F32)], compiler_params=_params(("arbitrary",)))(*ins)

        def body(*refs):
            st_r, do_r = refs[6:8]
            outs, ds_s = refs[8:14], refs[14]

            @pl.when(pl.program_id(0) == 0)
            def _():
                ds_s[...] = jnp.zeros_like(ds_s)
            _, vjp = jax.vjp(_dn_scan, *[r[...] for r in refs[:6]], st_r[...])
            *grads, ds = vjp((do_r[...], ds_s[...]))
            for o_r, gval in zip(outs, grads):
                o_r[...] = gval
            ds_s[...] = ds
        return _pcall(body, name=f"{name}_scan_bwd", grid=(n,), in_specs=[row] * 4 + [qkb, eg, st, row], out_specs=[row] * 4 + [qkb, eg],
                      out_shape=[jax.ShapeDtypeStruct((S, W), F32)] * 4
                      + [jax.ShapeDtypeStruct((n * QK, DN_BLOCK), F32), jax.ShapeDtypeStruct((n, 8, W), F32)],
                      scratch_shapes=[pltpu.VMEM((W, LANES), F32)], compiler_params=_params(("arbitrary",)))(*ins, saved, do)

    @jax.custom_vjp
    def local(q, k, v, g, b):
        return tuple(local_call((q, k, v, g, b))[:6])

    def local_fwd(*a):
        *outs, inverses = local_call(a)
        return tuple(outs), (*a, inverses)

    local.defvjp(local_fwd, lambda res, cts: tuple(local_call(res, tuple(cts))))

    @jax.custom_vjp
    def scan(u, w, qe, kd, qk, egl):
        return scan_call((u, w, qe, kd, qk, egl))[0]

    def scan_fwd(*a):
        o, states = scan_call(a)
        return o, (a, states)

    scan.defvjp(scan_fwd, lambda res, do: tuple(scan_call(res[0], res[1], do)))
    return lambda q, k, v, g, b: scan(*local(q, k, v, g, b))


def _loss_call(y, target):
    S, D = y.shape
    t = min(S, 512)
    n = S // t
    row = pl.BlockSpec((t, D), lambda i: (i, 0))

    def body(y_r, t_r, loss_r, dy_r, acc_s):
        i = pl.program_id(0)

        @pl.when(i == 0)
        def _():
            acc_s[...] = jnp.zeros_like(acc_s)
        err = y_r[...] - t_r[...]
        dy_r[...] = err * (1.0 / D)
        acc_s[...] += jnp.sum(err * err, axis=0, keepdims=True)

        @pl.when(i == n - 1)
        def _():
            loss_r[...] = jnp.broadcast_to(jnp.sum(acc_s[...], axis=1, keepdims=True) * (0.5 / D), loss_r.shape)

    return _pcall(body, name="loss_head", grid=(n,), in_specs=[row, row],
                  out_specs=[pl.BlockSpec((8, LANES), lambda i: (0, 0)), row],
                  out_shape=[jax.ShapeDtypeStruct((8, LANES), F32), jax.ShapeDtypeStruct((S, D), F32)],
                  scratch_shapes=[pltpu.VMEM((1, D), F32)], compiler_params=_params(("arbitrary",)))(y, target)


def _adamw_call(name, parts, w, m, v, rows=128):
    L, R, C = w.shape
    assert len(parts) == L
    t = _tile(R, rows, 8)
    row = pl.BlockSpec((None, t, C), lambda l, i: (l, i, 0))
    part = lambda k: pl.BlockSpec((N_DEV, t, C), lambda l, i: (0, jnp.where(l == k, i, 0), 0))

    def body(*refs):
        p_refs, (w_r, m_r, v_r, g_r, d_r, nm_r, nv_r) = refs[:L], refs[L:]

        def update(p_r):
            g = p_r[0].astype(F32)
            for s in range(1, N_DEV):
                g = g + p_r[s].astype(F32)
            m_new = ADAM_B1 * m_r[...] + (1.0 - ADAM_B1) * g
            v_new = ADAM_B2 * v_r[...] + (1.0 - ADAM_B2) * (g * g)
            m_hat = m_new / (1.0 - ADAM_B1 ** ADAM_STEP)
            v_hat = v_new / (1.0 - ADAM_B2 ** ADAM_STEP)
            g_r[...] = g
            d_r[...] = -ADAM_LR * (m_hat / (jnp.sqrt(v_hat) + ADAM_EPS) + ADAM_WD * w_r[...])
            nm_r[...] = m_new
            nv_r[...] = v_new

        for k in range(L):
            pl.when(pl.program_id(0) == k)(functools.partial(update, p_refs[k]))

    return _pcall(body, name=name, grid=(L, R // t), in_specs=[part(k) for k in range(L)] + [row, row, row],
                  out_specs=[row] * 4, out_shape=[jax.ShapeDtypeStruct((L, R, C), F32)] * 4,
                  compiler_params=_params(("arbitrary", "arbitrary")))(*parts, w, m, v)


def _my_place():
    x, y, c = lax.axis_index("x"), lax.axis_index("y"), lax.axis_index("c")
    return x, y, c


def _index(x, y, c):
    return 4 * x + 2 * y + c


def _all_gather(vs):
    n = len(vs)

    def body(*refs):
        v_refs, out_refs = refs[:n], refs[n:2 * n]
        send_sems, recv_sems, local_sems = refs[2 * n:]
        x, y, c = _my_place()
        me, sibling = (x, y, c), (x, y, 1 - c)
        chips = [(1 - x, y), (x, 1 - y), (1 - x, 1 - y)]

        def copy(a, k, block, to, src=None):
            rows = out_refs[a].at[_index(*block)]
            return pltpu.make_async_remote_copy(src_ref=rows if src is None else src, dst_ref=rows, send_sem=send_sems.at[a, k],
                                                recv_sem=recv_sems.at[a, k], device_id=to, device_id_type=MESH)

        mine = [pltpu.make_async_copy(v_refs[a], out_refs[a].at[_index(*me)], local_sems.at[a]) for a in range(n)]
        first, passed = [], []
        for a in range(n):
            mine[a].start()
            first += [copy(a, 0, me, sibling, src=v_refs[a])]
            first += [copy(a, 1 + j, me, (*chip, c), src=v_refs[a]) for j, chip in enumerate(chips)]
        for cp in first:
            cp.start()
        for j, chip in enumerate(chips):
            for a in range(n):
                copy(a, 1 + j, (*chip, c), me).wait_recv()
                passed.append(copy(a, 4 + j, (*chip, c), sibling))
                passed[-1].start()
        for a in range(n):
            copy(a, 0, sibling, me).wait_recv()
            for j, chip in enumerate(chips):
                copy(a, 4 + j, (*chip, 1 - c), me).wait_recv()
        for cp in first + passed:
            cp.wait_send()
        for a in range(n):
            mine[a].wait()

    any_ = pl.BlockSpec(memory_space=pl.ANY)
    return _pcall(body, name="gather_weights", in_specs=[any_] * n, out_specs=[any_] * n,
                  out_shape=[jax.ShapeDtypeStruct((N_DEV,) + v.shape, v.dtype) for v in vs],
                  scratch_shapes=[pltpu.SemaphoreType.DMA((n, 7)), pltpu.SemaphoreType.DMA((n, 7)), pltpu.SemaphoreType.DMA((n,))])(*vs)


def _all_to_all(vs):
    n = len(vs)

    def body(*refs):
        v_refs, out_refs = refs[:n], refs[n:2 * n]
        send_sems, recv_sems, local_sems = refs[2 * n:]
        x, y, c = _my_place()
        me = _index(x, y, c)
        mine = [pltpu.make_async_copy(v_refs[a].at[me], out_refs[a].at[me], local_sems.at[a]) for a in range(n)]
        copies = []
        for a in range(n):
            mine[a].start()
        for k in range(1, N_DEV):
            px = 1 - x if k & 4 else x
            py = 1 - y if k & 2 else y
            pc = 1 - c if k & 1 else c
            for a in range(n):
                cp = pltpu.make_async_remote_copy(src_ref=v_refs[a].at[_index(px, py, pc)], dst_ref=out_refs[a].at[me],
                                                  send_sem=send_sems.at[a, k - 1], recv_sem=recv_sems.at[a, k - 1],
                                                  device_id=(px, py, pc), device_id_type=MESH)
                cp.start()
                copies.append(cp)
        for cp in copies:
            cp.wait()
        for a in range(n):
            mine[a].wait()

    any_ = pl.BlockSpec(memory_space=pl.ANY)
    return _pcall(body, name="exchange_grads", in_specs=[any_] * n, out_specs=[any_] * n,
                  out_shape=[jax.ShapeDtypeStruct(v.shape, v.dtype) for v in vs],
                  scratch_shapes=[pltpu.SemaphoreType.DMA((n, 7)), pltpu.SemaphoreType.DMA((n, 7)), pltpu.SemaphoreType.DMA((n,))])(*vs)


_HBM = pl.BlockSpec(memory_space=pltpu.HBM)
_SEM = pl.BlockSpec(memory_space=pltpu.SEMAPHORE)
_EFFECT = pltpu.SideEffectType.DATAFLOW_SIDE_EFFECTING


def _direct_copies(gather, v_refs, land_refs, send_sems, recv_sems, local_sems):
    x, y, c = _my_place()
    me = _index(x, y, c)
    local, remote = [], []
    for a, (v_ref, land_ref) in enumerate(zip(v_refs, land_refs)):
        local.append(pltpu.make_async_copy(v_ref if gather else v_ref.at[me], land_ref.at[me], local_sems.at[a]))
    for k in range(1, N_DEV):
        px = 1 - x if k & 4 else x
        py = 1 - y if k & 2 else y
        pc = 1 - c if k & 1 else c
        for a, (v_ref, land_ref) in enumerate(zip(v_refs, land_refs)):
            sem = a * (N_DEV - 1) + k - 1
            remote.append(pltpu.make_async_remote_copy(
                src_ref=v_ref if gather else v_ref.at[_index(px, py, pc)], dst_ref=land_ref.at[me], send_sem=send_sems.at[sem],
                recv_sem=recv_sems.at[sem], device_id=(px, py, pc), device_id_type=MESH))
    return local, remote


def _exchange_start(name, vs, gather, thru):
    n = len(vs)
    lands = [lax.empty((N_DEV,) + v.shape if gather else v.shape, v.dtype) for v in vs]

    def body(*refs):
        v_refs, land_refs = refs[:n], refs[n:2 * n]
        send_sems, recv_sems, local_sems = refs[2 * n + 1:2 * n + 4]
        local, remote = _direct_copies(gather, v_refs, land_refs, send_sems, recv_sems, local_sems)
        for cp in local + remote:
            cp.start()

    hbm = lambda a: pltpu.HBM(a.shape, a.dtype)
    res = _pcall(body, name=name,
                 out_shape=(pltpu.SemaphoreType.DMA((n * (N_DEV - 1),)), pltpu.SemaphoreType.DMA((n * (N_DEV - 1),)), pltpu.SemaphoreType.DMA((n,)),
                            *[hbm(a) for a in (*vs, *lands, thru)]),
                 in_specs=[_HBM] * (2 * n + 1), out_specs=(_SEM, _SEM, _SEM, *[_HBM] * (2 * n + 1)),
                 input_output_aliases={i: 3 + i for i in range(2 * n + 1)},
                 compiler_params=pltpu.CompilerParams(has_side_effects=_EFFECT))(
        *[pltpu.with_memory_space_constraint(a, pltpu.HBM) for a in (*vs, *lands, thru)])
    return (gather, res[:3], res[3:3 + n], res[3 + n:3 + 2 * n]), res[3 + 2 * n]


def _exchange_wait(name, started, after):
    gather, sems, vs, lands = started
    n = len(vs)

    def body(*refs):
        v_refs, land_refs = refs[:n], refs[n:2 * n]
        send_sems, recv_sems, local_sems = refs[2 * n:2 * n + 3]
        local, remote = _direct_copies(gather, v_refs, land_refs, send_sems, recv_sems, local_sems)
        for cp in local:
            cp.wait()
        for cp in remote:
            cp.wait_send()
            cp.wait_recv()

    hbm = lambda a: pltpu.HBM(a.shape, a.dtype)
    res = _pcall(body, name=name, out_shape=tuple(hbm(a) for a in (*vs, *lands)),
                 in_specs=[_HBM] * (2 * n) + [_SEM] * 3 + [pl.BlockSpec(memory_space=pl.ANY)], out_specs=tuple([_HBM] * (2 * n)),
                 input_output_aliases={i: i for i in range(2 * n)},
                 compiler_params=pltpu.CompilerParams(has_side_effects=_EFFECT))(*vs, *lands, *sems, after)
    return list(res[n:])


W_IN_SHARD = IN_WIDTH // N_DEV
SEG_ORDER = ("q_lat", "c_kv", "k_pe", "z_a", "dn_qkv", "dn_ab", "z_b", "dil_qkv", "z_c", "gate")
SEG_WIDTH = (384, 256, LANES, 512, 1536, LANES, 512, 4608, 512, 3072)


def _w_in_plan():
    plan = []

    def add(seg, c0, c1, dst):
        while c0 < c1:
            d = c0 // W_IN_SHARD
            e = min(c1, (d + 1) * W_IN_SHARD)
            plan.append((seg, dst, d, c0 - d * W_IN_SHARD, e - c0))
            dst += e - c0
            c0 = e

    half = MLA_ROPE // 2
    for i, name in enumerate(SEG_ORDER):
        if name == "k_pe":
            o = _SEG["k_pe"][0]
            add(i, o, o + half, 0)
            add(i, o + half, o + 2 * half, LANES // 2)
        elif name == "dn_ab":
            o = _SEG["dn_a"][0]
            add(i, o, o + 2 * DN_HEADS, 0)
        else:
            o, w = _SEG[name]
            add(i, o, o + w, 0)
    return plan


def _make_w_in_segments(name):
    plan = _w_in_plan()
    nseg = len(SEG_ORDER)
    t = 256

    def fwd_call(g):
        L = g.shape[1]

        def body(g_ref, *o_refs):
            for i in (SEG_ORDER.index("k_pe"), SEG_ORDER.index("dn_ab")):
                o_refs[i][...] = jnp.zeros_like(o_refs[i])
            for seg, dst, d, src, n in plan:
                o_refs[seg][:, dst:dst + n] = g_ref[d, :, src:src + n]

        return _pcall(body, name=f"{name}_fwd", grid=(L, D_MODEL // t),
                      in_specs=[pl.BlockSpec((N_DEV, None, t, W_IN_SHARD), lambda l, i: (0, l, i, 0))],
                      out_specs=[pl.BlockSpec((None, t, w), lambda l, i: (l, i, 0)) for w in SEG_WIDTH],
                      out_shape=[jax.ShapeDtypeStruct((L, D_MODEL, w), g.dtype) for w in SEG_WIDTH],
                      compiler_params=_params(("parallel", "parallel")))(g)

    def bwd_call(ds):
        L = ds[0].shape[0]

        def body(*refs):
            d_refs, g_ref = refs[:nseg], refs[nseg]
            for seg, dst, d, src, n in plan:
                g_ref[d, :, src:src + n] = d_refs[seg][:, dst:dst + n]

        return _pcall(body, name=f"{name}_bwd", grid=(L, D_MODEL // t),
                      in_specs=[pl.BlockSpec((None, t, w), lambda l, i: (l, i, 0)) for w in SEG_WIDTH],
                      out_specs=pl.BlockSpec((N_DEV, None, t, W_IN_SHARD), lambda l, i: (0, l, i, 0)),
                      out_shape=jax.ShapeDtypeStruct((N_DEV, L, D_MODEL, W_IN_SHARD), ds[0].dtype),
                      compiler_params=_params(("parallel", "parallel")))(*ds)

    @jax.custom_vjp
    def op(g):
        return tuple(fwd_call(g))

    op.defvjp(lambda g: (op(g), None), lambda _, ds: (bwd_call(tuple(ds)),))
    return op


def _pe_pad(a):
    h = MLA_ROPE // 2
    z = jnp.zeros(a.shape[:-1] + (h,), a.dtype)
    return jnp.concatenate([a[..., :h], z, a[..., h:], z], axis=-1)


def _layer_norm(tag, x, norm_g):
    return _make_rowwise(f"{tag}_norm", _f_norm, 512)((x,), (), (norm_g[None, :],), ())[0]


def _layer(tag, x, tables, W):
    h = _layer_norm(tag, x, W["norm_g"])
    return _layer_tail(tag, x, _make_multi_linear(f"{tag}_inproj", 10)(h, W["w_in_segments"]), tables, W)


def _layer_tail(tag, x, segments, tables, W):
    cos_p, sin_p, cos_h, sin_h = tables
    row = lambda a: a[None, :]
    q_lat, c_kv, kpe, z_a, dn_qkv, ab, z_b, dil_qkv, z_c, gl = segments

    qn_lat, ckvn, kp = _make_rowwise(f"{tag}_mla_a", _f_mla_a, 512)(
        (q_lat, c_kv, kpe), (cos_p, sin_p),
        (row(W["mla_q_a_norm_g"]), row(W["mla_kv_a_norm_g"]), row(_pe_pad(W["mla_k_norm_g"][LANES:]))), ())
    wq = W["mla_w_q_b"].reshape(MLA_Q_RANK, MLA_HEADS, MLA_QK)
    wq = jnp.concatenate([wq[:, :, :LANES].reshape(MLA_Q_RANK, -1), _pe_pad(wq[:, :, LANES:]).reshape(MLA_Q_RANK, -1)], axis=1)
    wkv = W["mla_w_kv_b"].reshape(MLA_KV_RANK, MLA_HEADS, 2 * LANES)
    (q8,) = _make_multi_linear(f"{tag}_qb", 1)(qn_lat, (wq,))
    kn_raw, v_mla = _make_multi_linear(f"{tag}_kvb", 2)(
        ckvn, (wkv[:, :, :LANES].reshape(MLA_KV_RANK, -1), wkv[:, :, LANES:].reshape(MLA_KV_RANK, -1)))
    qn, qp, kn = _make_rowwise(f"{tag}_mla_b", _f_mla_b, 512)(
        (q8, kn_raw), (cos_p, sin_p),
        (row(W["mla_q_norm_g"][:LANES]), row(_pe_pad(W["mla_q_norm_g"][LANES:])), row(W["mla_k_norm_g"][:LANES])), ())
    y_a = _make_mla_attn(f"{tag}_mla")(qn, qp, kn, kp, v_mla)

    mixed = _make_conv(f"{tag}_conv")(dn_qkv, W["dn_conv_w"])
    lane_head = jnp.arange(DN_HEADS * LANES) // LANES
    e_a = (jnp.arange(LANES)[:, None] == lane_head[None, :]).astype(F32)
    e_b = (jnp.arange(LANES)[:, None] == lane_head[None, :] + DN_HEADS).astype(F32)
    q_dn, k_dn, v_dn, g_dn, b_dn = _make_rowwise(f"{tag}_dn_pre", _f_dn_pre, 512)(
        (mixed, ab), (), (row(jnp.repeat(W["dn_a_log"], LANES)), row(jnp.repeat(W["dn_dt_bias"], LANES))), (e_a, e_b))
    o_dn = _make_delta_rule(f"{tag}_dn")(q_dn, k_dn, v_dn, g_dn, b_dn)

    qkv_dil = _make_rowwise(f"{tag}_dil_pre", _f_dil_pre, 256)(
        (dil_qkv,), (cos_h, sin_h), (row(W["dil_q_norm_g"]), row(W["dil_k_norm_g"])), ())
    n_groups = len(DIL_DILATIONS)
    o_lse = [_make_dil_attn(f"{tag}_dil{g}", d, DIL_TILE_ROWS[g])(qkv_dil[g], qkv_dil[n_groups + g], qkv_dil[2 * n_groups + g])
             for g, d in enumerate(DIL_DILATIONS)]

    ya, yb, yc = _make_rowwise(f"{tag}_merge_a", _f_merge_a, 256)(
        (y_a, z_a, o_dn, z_b, *[o for o, _ in o_lse], *[l for _, l in o_lse], z_c), (), (row(W["dn_out_norm_g"]),), ())
    (b0,) = _make_multi_linear(f"{tag}_br0", 1)(ya, (W["w_branch"][0],))
    (b1,) = _make_multi_linear(f"{tag}_br1", 1)(yb, (W["w_branch"][1],))
    (b2,) = _make_multi_linear(f"{tag}_br2", 1)(yc, (W["w_branch"][2],))
    (mix,) = _make_rowwise(f"{tag}_merge_b", _f_merge_b, 256)((b0, b1, b2, gl), (), (), ())
    return _make_resid_linear(f"{tag}_out")(x, mix, W["w_out"])


SHARDED = (("w_in", (D_MODEL, W_IN_SHARD)), ("mla_w_q_b", (MLA_Q_RANK, MLA_HEADS * MLA_QK // N_DEV)),
           ("mla_w_kv_b", (MLA_KV_RANK, MLA_HEADS * 2 * LANES // N_DEV)), ("w_branch", (3 * BRANCH_W, D_MODEL // N_DEV)),
           ("w_out", (D_MODEL // N_DEV, D_MODEL)), ("dn_conv_w", (DN_CONV, 3 * DN_HEADS * LANES // N_DEV)))
SMALL = (("norm_g", D_MODEL), ("mla_q_a_norm_g", MLA_Q_RANK), ("mla_kv_a_norm_g", MLA_KV_RANK), ("mla_q_norm_g", MLA_QK),
         ("mla_k_norm_g", MLA_QK), ("dn_a_log", DN_HEADS), ("dn_dt_bias", DN_HEADS), ("dn_out_norm_g", LANES),
         ("dil_q_norm_g", LANES), ("dil_k_norm_g", LANES))
WEIGHTS = ("norm_g", "w_in", "mla_q_a_norm_g", "mla_w_q_b", "mla_kv_a_norm_g", "mla_w_kv_b", "mla_q_norm_g", "mla_k_norm_g",
           "dn_conv_w", "dn_a_log", "dn_dt_bias", "dn_out_norm_g", "dil_q_norm_g", "dil_k_norm_g", "w_branch", "w_out")


def _round_up(n, m):
    return -(-n // m) * m


def _pack_vectors(pieces):
    return jnp.concatenate([jnp.pad(p, (0, _round_up(p.shape[0], LANES) - p.shape[0])) for p in pieces]).reshape(-1, LANES)


def _unpack_vectors(flat, sizes):
    out, off = [], 0
    flat = flat.reshape(-1)
    for n in sizes:
        out.append(flat[off:off + n])
        off += _round_up(n, LANES)
    return out


def _whole_weights(g, small):
    W = dict(small)
    W["mla_w_q_b"] = g["mla_w_q_b"].transpose(1, 0, 2).reshape(MLA_Q_RANK, -1)
    W["mla_w_kv_b"] = g["mla_w_kv_b"].transpose(1, 0, 2).reshape(MLA_KV_RANK, -1)
    W["w_branch"] = g["w_branch"].reshape(N_DEV, 3, BRANCH_W, -1).transpose(1, 2, 0, 3).reshape(3, BRANCH_W, D_MODEL)
    W["w_out"] = g["w_out"].reshape(D_MODEL, D_MODEL)
    W["dn_conv_w"] = g["dn_conv_w"].transpose(1, 0, 2).reshape(DN_CONV, -1)
    return W


def kernel(x, positions, norm_g, w_in, mla_q_a_norm_g, mla_w_q_b, mla_kv_a_norm_g, mla_w_kv_b, mla_q_norm_g, mla_k_norm_g, dn_conv_w, dn_a_log, dn_dt_bias, dn_out_norm_g, dil_q_norm_g, dil_k_norm_g, w_branch, w_out, loss_target, m_norm_g, m_w_in, m_mla_q_a_norm_g, m_mla_w_q_b, m_mla_kv_a_norm_g, m_mla_w_kv_b, m_mla_q_norm_g, m_mla_k_norm_g, m_dn_conv_w, m_dn_a_log, m_dn_dt_bias, m_dn_out_norm_g, m_dil_q_norm_g, m_dil_k_norm_g, m_w_branch, m_w_out, v_norm_g, v_w_in, v_mla_q_a_norm_g, v_mla_w_q_b, v_mla_kv_a_norm_g, v_mla_w_kv_b, v_mla_q_norm_g, v_mla_k_norm_g, v_dn_conv_w, v_dn_a_log, v_dn_dt_bias, v_dn_out_norm_g, v_dil_q_norm_g, v_dil_k_norm_g, v_w_branch, v_w_out):
    w = dict(norm_g=norm_g, w_in=w_in, mla_q_a_norm_g=mla_q_a_norm_g, mla_w_q_b=mla_w_q_b, mla_kv_a_norm_g=mla_kv_a_norm_g,
             mla_w_kv_b=mla_w_kv_b, mla_q_norm_g=mla_q_norm_g, mla_k_norm_g=mla_k_norm_g, dn_conv_w=dn_conv_w, dn_a_log=dn_a_log,
             dn_dt_bias=dn_dt_bias, dn_out_norm_g=dn_out_norm_g, dil_q_norm_g=dil_q_norm_g, dil_k_norm_g=dil_k_norm_g,
             w_branch=w_branch, w_out=w_out)
    m = dict(norm_g=m_norm_g, w_in=m_w_in, mla_q_a_norm_g=m_mla_q_a_norm_g, mla_w_q_b=m_mla_w_q_b, mla_kv_a_norm_g=m_mla_kv_a_norm_g,
             mla_w_kv_b=m_mla_w_kv_b, mla_q_norm_g=m_mla_q_norm_g, mla_k_norm_g=m_mla_k_norm_g, dn_conv_w=m_dn_conv_w,
             dn_a_log=m_dn_a_log, dn_dt_bias=m_dn_dt_bias, dn_out_norm_g=m_dn_out_norm_g, dil_q_norm_g=m_dil_q_norm_g,
             dil_k_norm_g=m_dil_k_norm_g, w_branch=m_w_branch, w_out=m_w_out)
    v = dict(norm_g=v_norm_g, w_in=v_w_in, mla_q_a_norm_g=v_mla_q_a_norm_g, mla_w_q_b=v_mla_w_q_b, mla_kv_a_norm_g=v_mla_kv_a_norm_g,
             mla_w_kv_b=v_mla_w_kv_b, mla_q_norm_g=v_mla_q_norm_g, mla_k_norm_g=v_mla_k_norm_g, dn_conv_w=v_dn_conv_w,
             dn_a_log=v_dn_a_log, dn_dt_bias=v_dn_dt_bias, dn_out_norm_g=v_dn_out_norm_g, dil_q_norm_g=v_dil_q_norm_g,
             dil_k_norm_g=v_dil_k_norm_g, w_branch=v_w_branch, w_out=v_w_out)
    x2, target = x[0], loss_target[0]
    pos = positions[0][:, None]

    names = [n for n, _ in SHARDED]
    view = lambda t, n, s: t[n].reshape((DEPTH,) + s)
    shards = [[(view(w, n, s) if n == "dn_conv_w" else view(w, n, s).astype(BF16))[l] for n, s in SHARDED] for l in range(DEPTH)]
    small = [{n: w[n][l] for n, _ in SMALL} for l in range(DEPTH)]
    gathered0 = dict(zip(names, _all_gather(shards[0])))
    gathering1, pos = _exchange_start("gather_layer1_start", shards[1], True, pos)
    tables = _rope_tables(pos, _rope_consts())

    def layer(l, g, small_l, x_l):
        tag = f"l{l}"
        seg_op = _make_w_in_segments(f"{tag}_w_in_segments")
        w_segs, vjp_segs = jax.vjp(lambda gw: tuple(s[0] for s in seg_op(gw[:, None])), g["w_in"])
        h, vjp_norm = jax.vjp(lambda x_, ng: _layer_norm(tag, x_, ng), x_l, small_l["norm_g"])
        segs = tuple(_mm(f"{tag}_inproj_fwd{i}", h, w_, "nn", **INPROJ_TILES["nn"]) for i, w_ in enumerate(w_segs))
        rest_g = {n: a for n, a in g.items() if n != "w_in"}
        rest_s = {n: a for n, a in small_l.items() if n != "norm_g"}
        y, vjp_tail = jax.vjp(lambda sg, x_, gg, ss: _layer_tail(tag, x_, sg, tables, _whole_weights(gg, ss)), segs, x_l, rest_g, rest_s)

        def backward(dy):
            dsegs, dx_skip, d_rest_g, d_rest_s = vjp_tail(dy)
            dws = tuple(_mm(f"{tag}_inproj_dw{i}", h, d, "tn", out_dtype=w_.dtype, **INPROJ_TILES["tn"])
                        for i, (w_, d) in enumerate(zip(w_segs, dsegs)))
            dg = dict(d_rest_g, w_in=vjp_segs(dws)[0])
            exchanging, first = _exchange_start(f"exchange_{tag}_start", [dg[n] for n in names], False, dsegs[0])
            dsegs = (first,) + tuple(dsegs[1:])
            dh = None
            for i, group in enumerate(INPROJ_DH_GROUPS):
                dh = _mm_nt_sum(f"{tag}_inproj_dh{i}", [(dsegs[s], w_segs[s]) for s in group], dh,
                                h.dtype if i == len(INPROJ_DH_GROUPS) - 1 else F32)
            dx_norm, d_norm_g = vjp_norm(dh)
            return exchanging, dx_skip + dx_norm, dict(d_rest_s, norm_g=d_norm_g)
        return y, backward

    y0, backward0 = layer(0, gathered0, small[0], x2)
    gathered1 = dict(zip(names, _exchange_wait("gather_layer1_wait", gathering1, y0)))
    y1, backward1 = layer(1, gathered1, small[1], y0)
    loss_splat, dy = _loss_call(y1, target)
    loss = lax.psum(loss_splat[0, 0], ("x", "y", "c"))
    exchanging1, d_y0, g_small1 = backward1(dy)
    exchanging0, g_x, g_small0 = backward0(d_y0)

    g_small = (g_small0, g_small1)
    sizes = [k for _ in range(DEPTH) for _, k in SMALL]
    g_vec = _pack_vectors([g_small[l][n] for l in range(DEPTH) for n, _ in SMALL])
    (parts_vec,) = _all_to_all([jnp.broadcast_to(g_vec[None], (N_DEV,) + g_vec.shape)])
    parts1 = _exchange_wait("exchange_l1_wait", exchanging1, parts_vec)
    parts0 = _exchange_wait("exchange_l0_wait", exchanging0, parts_vec)

    vec = lambda t: _pack_vectors([t[n][l] for l in range(DEPTH) for n, _ in SMALL])[None]
    outs = {}
    for i, (n, s) in enumerate(SHARDED):
        res = _adamw_call(f"adamw_{n}", [parts0[i], parts1[i]], view(w, n, s), view(m, n, s), view(v, n, s))
        outs[n] = [o.reshape(w[n].shape) for o in res]
    vec_outs = [_unpack_vectors(o, sizes) for o in _adamw_call("adamw_vectors", [parts_vec], vec(w), vec(m), vec(v))]
    for i, (n, _) in enumerate(SMALL):
        outs[n] = [jnp.stack([o[l * len(SMALL) + i] for l in range(DEPTH)]) for o in vec_outs]
    return (loss, g_x[None], *[outs[n][k] for k in range(4) for n in WEIGHTS])
```

```python
import functools
import math

import jax
import jax.numpy as jnp
from jax import lax
from jax.experimental import pallas as pl
from jax.experimental.pallas import tpu as pltpu

F32 = jnp.float32
BF16 = jnp.bfloat16
HI = lax.Precision.HIGHEST
MESH = pl.DeviceIdType.MESH

N_DEV = 8
D_MODEL = 1024
DEPTH = 2
RMS_EPS = 1e-6
ROPE_THETA = 10000.0
LANES = 128
MLA_HEADS = 4
MLA_ROPE = 64
MLA_QK = 192
MLA_Q_RANK = 384
MLA_KV_RANK = 256
DN_HEADS = 4
DN_CHUNK = 64
DN_CONV = 4
DIL_HEADS = 12
DIL_GROUP_HEADS = 4
DIL_DILATIONS = (1, 4, 16)
DIL_BLOCK = 128
BRANCH_W = 512
IN_WIDTH = 11464
NEG = -1e30
VMEM_LIMIT = 56 * 1024 * 1024

ADAM_LR, ADAM_B1, ADAM_B2, ADAM_EPS, ADAM_WD, ADAM_STEP = 0.001, 0.9, 0.999, 1e-08, 0.01, 10

_SEG = {}
_off = 0
for _n, _w in (("q_lat", 384), ("c_kv", 256), ("k_pe", 64), ("z_a", 512), ("dn_qkv", 1536), ("dn_a", 4), ("dn_b", 4),
               ("z_b", 512), ("dil_qkv", 4608), ("z_c", 512), ("gate", 3072)):
    _SEG[_n] = (_off, _w)
    _off += _w
assert _off == IN_WIDTH


def _pcall(body, **kw):
    return pl.pallas_call(body, **kw)


def _params(sem=None):
    return pltpu.CompilerParams(dimension_semantics=sem, vmem_limit_bytes=VMEM_LIMIT)


def _tile(n, target, mult):
    t = (min(n, target) // mult) * mult
    while t >= mult:
        if n % t == 0:
            return t
        t -= mult
    return n


def _mm(name, a, b, mode, out_dtype=F32, acc=None, tm=1024, tn=512, tk=1024):
    if mode == "nn":
        (M, K), (_, N) = a.shape, b.shape
    elif mode == "nt":
        (M, K), (N, _) = a.shape, b.shape
    else:
        (K, M), (_, N) = a.shape, b.shape
    tm, tn, tk = _tile(M, tm, LANES), _tile(N, tn, LANES), _tile(K, tk, LANES)
    nk = K // tk
    dims = {"nn": (((1,), (0,)), ((), ())), "nt": (((1,), (1,)), ((), ())), "tn": (((0,), (0,)), ((), ()))}[mode]
    a_spec = pl.BlockSpec((tk, tm), lambda i, j, k: (k, i)) if mode == "tn" else pl.BlockSpec((tm, tk), lambda i, j, k: (i, k))
    b_spec = pl.BlockSpec((tn, tk), lambda i, j, k: (j, k)) if mode == "nt" else pl.BlockSpec((tk, tn), lambda i, j, k: (k, j))
    o_spec = pl.BlockSpec((tm, tn), lambda i, j, k: (i, j))
    has_acc = acc is not None

    def body(*refs):
        a_ref, b_ref = refs[:2]
        c_ref = refs[2] if has_acc else None
        o_ref = refs[3] if has_acc else refs[2]
        prod = lax.dot_general(a_ref[...].astype(BF16), b_ref[...].astype(BF16), dims, preferred_element_type=F32)
        if nk == 1:
            o_ref[...] = (prod + c_ref[...].astype(F32) if has_acc else prod).astype(out_dtype)
            return
        acc_ref = refs[-1]
        k = pl.program_id(2)

        @pl.when(k == 0)
        def _():
            acc_ref[...] = prod + c_ref[...].astype(F32) if has_acc else prod

        @pl.when(k > 0)
        def _():
            acc_ref[...] += prod

        @pl.when(k == nk - 1)
        def _():
            o_ref[...] = acc_ref[...].astype(out_dtype)

    ins = [a, b] + ([acc] if has_acc else [])
    in_specs = [a_spec, b_spec] + ([o_spec] if has_acc else [])
    return _pcall(body, name=name, grid=(M // tm, N // tn, nk), in_specs=in_specs, out_specs=o_spec,
                  out_shape=jax.ShapeDtypeStruct((M, N), out_dtype), scratch_shapes=[pltpu.VMEM((tm, tn), F32)] if nk > 1 else [],
                  compiler_params=_params(("parallel", "parallel", "arbitrary")))(*ins)


def _mm_nt_sum(name, pairs, acc, out_dtype, tm=512, tn=1024):
    M, N = pairs[0][0].shape[0], pairs[0][1].shape[0]
    tm, tn = _tile(M, tm, LANES), _tile(N, tn, LANES)
    n = len(pairs)
    has_acc = acc is not None
    o_spec = pl.BlockSpec((tm, tn), lambda i, j: (i, j))

    def body(*refs):
        total = refs[2 * n][...].astype(F32) if has_acc else None
        for a_ref, b_ref in zip(refs[:n], refs[n:2 * n]):
            prod = lax.dot_general(a_ref[...].astype(BF16), b_ref[...].astype(BF16), (((1,), (1,)), ((), ())), preferred_element_type=F32)
            total = prod if total is None else total + prod
        refs[-1][...] = total.astype(out_dtype)

    in_specs = ([pl.BlockSpec((tm, a.shape[1]), lambda i, j: (i, 0)) for a, _ in pairs]
                + [pl.BlockSpec((tn, b.shape[1]), lambda i, j: (j, 0)) for _, b in pairs] + ([o_spec] if has_acc else []))
    return _pcall(body, name=name, grid=(M // tm, N // tn), in_specs=in_specs, out_specs=o_spec,
                  out_shape=jax.ShapeDtypeStruct((M, N), out_dtype), compiler_params=_params(("parallel", "parallel")))(
        *[a for a, _ in pairs], *[b for _, b in pairs], *([acc] if has_acc else []))


INPROJ_DH_GROUPS = ((0, 1, 2, 3, 5, 6, 8), (4,), (9,), (7,))

INPROJ_DTYPE = BF16

INPROJ_TILES = {"nn": dict(tm=1024, tn=1536, tk=1024), "tn": dict(tm=1024, tn=768, tk=2048)}


def _make_multi_linear(name, n, out_dtype=F32):
    @jax.custom_vjp
    def op(h, ws):
        return tuple(_mm(f"{name}_fwd{i}", h, w, "nn", out_dtype=out_dtype) for i, w in enumerate(ws))

    def fwd(h, ws):
        return op(h, ws), (h, ws)

    def bwd(res, douts):
        h, ws = res
        dh = None
        for i, (w, d) in enumerate(zip(ws, douts)):
            dh = _mm(f"{name}_dh{i}", d, w, "nt", acc=dh, out_dtype=h.dtype if i == len(ws) - 1 else F32)
        dws = tuple(_mm(f"{name}_dw{i}", h, d, "tn", out_dtype=w.dtype) for i, (w, d) in enumerate(zip(ws, douts)))
        return dh, dws

    op.defvjp(fwd, bwd)
    return op


def _make_resid_linear(name):
    @jax.custom_vjp
    def op(x, a, w):
        return _mm(f"{name}_fwd", a, w, "nn", acc=x)

    def fwd(x, a, w):
        return op(x, a, w), (a, w)

    def bwd(res, dy):
        a, w = res
        return dy, _mm(f"{name}_da", dy, w, "nt", out_dtype=a.dtype), _mm(f"{name}_dw", a, dy, "tn", out_dtype=w.dtype)

    op.defvjp(fwd, bwd)
    return op


def _make_rowwise(name, f, tile):
    def specs(rows, aux, params, consts, t):
        row = [pl.BlockSpec((t, a.shape[1]), lambda i: (i, 0)) for a in (*rows, *aux)]
        full = [pl.BlockSpec(p.shape, lambda i: (0, 0)) for p in (*params, *consts)]
        return row, full

    def fwd_call(rows, aux, params, consts):
        S = rows[0].shape[0]
        t = min(tile, S)
        n_in = len(rows) + len(aux) + len(params) + len(consts)
        shp = lambda a: jax.ShapeDtypeStruct((t, a.shape[1]), a.dtype)
        outs = jax.eval_shape(f, *[shp(a) for a in (*rows, *aux)], *params, *consts)
        row_specs, full_specs = specs(rows, aux, params, consts, t)

        def body(*refs):
            res = f(*[r[...] for r in refs[:n_in]])
            for o_ref, o in zip(refs[n_in:], res):
                o_ref[...] = o

        return _pcall(body, name=f"{name}_fwd", grid=(S // t,), in_specs=row_specs + full_specs,
                      out_specs=[pl.BlockSpec((t, o.shape[1]), lambda i: (i, 0)) for o in outs],
                      out_shape=[jax.ShapeDtypeStruct((S, o.shape[1]), o.dtype) for o in outs],
                      compiler_params=_params(("parallel",)))(*rows, *aux, *params, *consts)

    def bwd_call(rows, aux, params, consts, douts):
        S = rows[0].shape[0]
        t = min(tile, S)
        nr, na, npar, nc, nd = len(rows), len(aux), len(params), len(consts), len(douts)
        row_specs, full_specs = specs(rows, aux, params, consts, t)

        def body(*refs):
            vals = [r[...] for r in refs[:nr + na + npar + nc]]
            rv, av = vals[:nr], vals[nr:nr + na]
            pv, cv = vals[nr + na:nr + na + npar], vals[nr + na + npar:]
            dv = tuple(r[...] for r in refs[nr + na + npar + nc:nr + na + npar + nc + nd])
            out_refs = refs[nr + na + npar + nc + nd:]
            _, vjp = jax.vjp(lambda *rp: f(*rp[:nr], *av, *rp[nr:], *cv), *rv, *pv)
            grads = vjp(dv)
            for o_ref, g in zip(out_refs[:nr], grads[:nr]):
                o_ref[...] = g
            first = pl.program_id(0) == 0
            for o_ref, g in zip(out_refs[nr:], grads[nr:]):
                @pl.when(first)
                def _(o_ref=o_ref):
                    o_ref[...] = jnp.zeros_like(o_ref)
                o_ref[...] += g

        res = _pcall(body, name=f"{name}_bwd", grid=(S // t,),
                     in_specs=row_specs + full_specs + [pl.BlockSpec((t, d.shape[1]), lambda i: (i, 0)) for d in douts],
                     out_specs=[pl.BlockSpec((t, a.shape[1]), lambda i: (i, 0)) for a in rows]
                     + [pl.BlockSpec(p.shape, lambda i: (0, 0)) for p in params],
                     out_shape=[jax.ShapeDtypeStruct(a.shape, a.dtype) for a in (*rows, *params)],
                     compiler_params=_params(("arbitrary",)))(*rows, *aux, *params, *consts, *douts)
        return tuple(res[:nr]), tuple(res[nr:])

    @jax.custom_vjp
    def op(rows, aux, params, consts):
        return tuple(fwd_call(rows, aux, params, consts))

    def fwd(rows, aux, params, consts):
        return op(rows, aux, params, consts), (rows, aux, params, consts)

    def bwd(res, douts):
        rows, aux, params, consts = res
        drows, dparams = bwd_call(rows, aux, params, consts, tuple(douts))
        zeros = lambda xs: tuple(jnp.zeros_like(a) for a in xs)
        return drows, zeros(aux), dparams, zeros(consts)

    op.defvjp(fwd, bwd)
    return op


@jax.custom_vjp
def _swap_halves(x):
    return pltpu.roll(x, LANES // 2, 1)


_swap_halves.defvjp(lambda x: (_swap_halves(x), None), lambda _, g: (_swap_halves(g),))


def _rope(x, cos_t, sin_t):
    return x * cos_t + _swap_halves(x) * sin_t


def _rms(x, g, n=None):
    n = x.shape[-1] if n is None else n
    return x * lax.rsqrt(jnp.sum(x * x, axis=-1, keepdims=True) * (1.0 / n) + RMS_EPS) * g


def _heads(x):
    return [x[:, i * LANES:(i + 1) * LANES] for i in range(x.shape[1] // LANES)]


def _cat(xs):
    return jnp.concatenate(xs, axis=1)


def _silu(x):
    return x * jax.nn.sigmoid(x)


def _f_norm(x, g):
    return (_rms(x, g).astype(BF16),)


def _f_mla_a(q_lat, c_kv, kpe, cos_p, sin_p, qa_g, kva_g, kpe_g):
    q_lat, c_kv, kpe = (t.astype(F32) for t in (q_lat, c_kv, kpe))
    kp =_rope(_rms(kpe, kpe_g, MLA_ROPE), cos_p, sin_p)
    return _rms(q_lat, qa_g).astype(BF16), _rms(c_kv, kva_g).astype(BF16), _cat([kp] * MLA_HEADS)


def _f_mla_b(q8, kn_raw, cos_p, sin_p, qn_g, qp_g, kn_g):
    hs = _heads(q8)
    qn = _cat([_rms(h, qn_g) for h in hs[:MLA_HEADS]])
    qp = _cat([_rope(_rms(h, qp_g, MLA_ROPE), cos_p, sin_p) for h in hs[MLA_HEADS:]])
    kn = _cat([_rms(h, kn_g) for h in _heads(kn_raw)])
    return qn, qp, kn


def _softplus(x):
    return jnp.maximum(x, 0.0) + jnp.log(1.0 + jnp.exp(-jnp.abs(x)))


def _l2n(x):
    return x * lax.rsqrt(jnp.sum(x * x, axis=-1, keepdims=True) + 1e-6)


def _f_dn_pre(mixed, ab, alog_f, dtb_f, e_a, e_b):
    hs = _heads(mixed)
    q = _cat([_l2n(h) * (LANES ** -0.5) for h in hs[:DN_HEADS]])
    k = _cat([_l2n(h) for h in hs[DN_HEADS:2 * DN_HEADS]])
    v = _cat(hs[2 * DN_HEADS:])
    ab = ab.astype(F32)
    a_f = jnp.dot(ab, e_a, precision=HI, preferred_element_type=F32)
    b_f = jnp.dot(ab, e_b, precision=HI, preferred_element_type=F32)
    g = -jnp.exp(alog_f) * _softplus(a_f + dtb_f)
    return q, k, v, g, jax.nn.sigmoid(b_f)


def _f_dil_pre(qkv, cos_h, sin_h, q_g, k_g):
    hs = [h.astype(F32) for h in _heads(qkv)]
    q = [_rope(_rms(h, q_g), cos_h, sin_h) for h in hs[:DIL_HEADS]]
    k = [_rope(_rms(h, k_g), cos_h, sin_h) for h in hs[DIL_HEADS:2 * DIL_HEADS]]
    v = hs[2 * DIL_HEADS:]
    group = lambda xs, g: _cat(xs[g * DIL_GROUP_HEADS:(g + 1) * DIL_GROUP_HEADS])
    return tuple(group(xs, g) for xs in (q, k, v) for g in range(len(DIL_DILATIONS)))


def _f_merge_a(y_a, z_a, o_dn, z_b, o0, o1, o2, l0, l1, l2, z_c, out_g):
    z_a, z_b, z_c = (z.astype(F32) for z in (z_a, z_b, z_c))
    y_b = _cat([_rms(h, out_g) for h in _heads(o_dn)])
    os_, ls = [_heads(o) for o in (o0, o1, o2)], [_heads(l) for l in (l0, l1, l2)]
    y_c = []
    for j in range(DIL_GROUP_HEADS):
        l3 = [ls[g][j] for g in range(3)]
        m = jnp.maximum(jnp.maximum(l3[0], l3[1]), l3[2])
        e3 = [jnp.exp(l - m) for l in l3]
        den = e3[0] + e3[1] + e3[2]
        y_c.append(sum(e3[g] * os_[g][j] for g in range(3)) / den)
    return tuple(t.astype(BF16) for t in (y_a * _silu(z_a), y_b * _silu(z_b), _cat(y_c) * _silu(z_c)))


def _f_merge_b(b0, b1, b2, gl):
    gs = [jax.nn.sigmoid(gl[:, i * D_MODEL:(i + 1) * D_MODEL].astype(F32)) for i in range(3)]
    return ((gs[0] * b0 + gs[1] * b1 + gs[2] * b2).astype(BF16),)


def _rope_tables(pos, inv_sign):
    S = pos.shape[0]
    t = min(S, 1024)

    def body(p_ref, c_ref, cp, sp, ch, sh):
        p = p_ref[...].astype(F32)
        c = c_ref[...]
        ang_p, ang_h = p * c[0:1], p * c[2:3]
        cp[...] = jnp.cos(ang_p) * jnp.abs(c[1:2])
        sp[...] = jnp.sin(ang_p) * c[1:2]
        ch[...] = jnp.cos(ang_h)
        sh[...] = jnp.sin(ang_h) * c[3:4]

    row = pl.BlockSpec((t, LANES), lambda i: (i, 0))
    return _pcall(body, name="rope_tables", grid=(S // t,),
                  in_specs=[pl.BlockSpec((t, 1), lambda i: (i, 0)), pl.BlockSpec((4, LANES), lambda i: (0, 0))],
                  out_specs=[row] * 4, out_shape=[jax.ShapeDtypeStruct((S, LANES), F32)] * 4,
                  compiler_params=_params(("parallel",)))(pos, inv_sign)


def _rope_consts():
    half_p, half_h = MLA_ROPE // 2, LANES // 2
    inv_p = 1.0 / (ROPE_THETA ** (jnp.arange(0, MLA_ROPE, 2, dtype=F32) / MLA_ROPE))
    inv_h = 1.0 / (ROPE_THETA ** (jnp.arange(0, LANES, 2, dtype=F32) / LANES))
    z = jnp.zeros((half_p,), F32)
    o = jnp.ones((half_p,), F32)
    return jnp.stack([jnp.concatenate([inv_p, z, inv_p, z]), jnp.concatenate([-o, z, o, z]),
                      jnp.concatenate([inv_h, inv_h]), jnp.concatenate([-jnp.ones((half_h,), F32), jnp.ones((half_h,), F32)])])


def _shift_rows(x, s, up):
    n = x.shape[0]
    r = lax.broadcasted_iota(jnp.int32, x.shape, 0)
    if up:
        return jnp.where(r < n - s, pltpu.roll(x, n - s, 0), 0.0)
    return jnp.where(r >= s, pltpu.roll(x, s, 0), 0.0)


def _make_shift(s):
    @jax.custom_vjp
    def sh(x):
        return _shift_rows(x, s, False)

    sh.defvjp(lambda x: (sh(x), None), lambda _, g: (_shift_rows(g, s, True),))
    return sh


def _f_conv(x, w):
    x = x.astype(F32)
    y = x * w[DN_CONV - 1:DN_CONV]
    for j in range(DN_CONV - 1):
        y = y + _make_shift(DN_CONV - 1 - j)(x) * w[j:j + 1]
    return _silu(y)


def _make_conv(name):
    def call(x, w, dy=None):
        S, C = x.shape
        col = pl.BlockSpec((S, LANES), lambda i: (0, i))
        wsp = pl.BlockSpec((DN_CONV, LANES), lambda i: (0, i))
        if dy is None:
            def body(x_ref, w_ref, o_ref):
                o_ref[...] = _f_conv(x_ref[...], w_ref[...])
            return _pcall(body, name=f"{name}_fwd", grid=(C // LANES,), in_specs=[col, wsp], out_specs=col,
                          out_shape=jax.ShapeDtypeStruct(x.shape, F32), compiler_params=_params(("parallel",)))(x, w)

        def body(x_ref, w_ref, dy_ref, dx_ref, dw_ref):
            _, vjp = jax.vjp(_f_conv, x_ref[...], w_ref[...])
            dx_ref[...], dw_ref[...] = vjp(dy_ref[...])
        return _pcall(body, name=f"{name}_bwd", grid=(C // LANES,), in_specs=[col, wsp, col], out_specs=[col, wsp],
                      out_shape=[jax.ShapeDtypeStruct(x.shape, x.dtype), jax.ShapeDtypeStruct(w.shape, F32)],
                      compiler_params=_params(("parallel",)))(x, w, dy)

    @jax.custom_vjp
    def op(x, w):
        return call(x, w)

    op.defvjp(lambda x, w: (op(x, w), (x, w)), lambda res, dy: tuple(call(*res, dy)))
    return op


def _dot_nt(a, b):
    return lax.dot_general(a.astype(BF16), b.astype(BF16), (((1,), (1,)), ((), ())), preferred_element_type=F32)


def _dot_nn(a, b):
    return jnp.dot(a.astype(BF16), b.astype(BF16), preferred_element_type=F32)


def _dot_tn(a, b):
    return lax.dot_general(a.astype(BF16), b.astype(BF16), (((0,), (0,)), ((), ())), preferred_element_type=F32)


def _mla_scores(qn_r, qp_r, kn_r, kp_r, diagonal):
    scale = MLA_QK ** -0.5
    s = _dot_nt(qn_r[...] * scale, kn_r[...]) + _dot_nt(qp_r[...] * scale, kp_r[...])
    if diagonal:
        r = lax.broadcasted_iota(jnp.int32, s.shape, 0)
        c = lax.broadcasted_iota(jnp.int32, s.shape, 1)
        s = jnp.where(c <= r, s, NEG)
    return s


def _on_causal_pairs(qi, ki, step):
    @pl.when(ki < qi)
    def _():
        step(False)

    @pl.when(ki == qi)
    def _():
        step(True)


def _causal_pairs(n, t, by_key):
    pairs = [(q, k) for k in range(n) for q in range(k, n)] if by_key else [(q, k) for q in range(n) for k in range(q + 1)]
    qt, kt = (jnp.array([p[i] for p in pairs], jnp.int32) for i in (0, 1))
    return (qt, kt, pl.BlockSpec((t, LANES), lambda h, p, qt_r, kt_r: (qt_r[p], h)),
            pl.BlockSpec((t, LANES), lambda h, p, qt_r, kt_r: (kt_r[p], h)))


def _make_mla_attn(name):
    scale = MLA_QK ** -0.5

    def fwd_call(qn, qp, kn, kp, v):
        S = qn.shape[0]
        t = min(S, 512)
        n = S // t
        qt, kt, qs, ks = _causal_pairs(n, t, by_key=False)

        def body(qt_r, kt_r, qn_r, qp_r, kn_r, kp_r, v_r, o_r, lse_r, m_s, l_s, acc_s):
            qi, ki = qt_r[pl.program_id(1)], kt_r[pl.program_id(1)]

            @pl.when(ki == 0)
            def _():
                m_s[...] = jnp.full_like(m_s, NEG)
                l_s[...] = jnp.zeros_like(l_s)
                acc_s[...] = jnp.zeros_like(acc_s)

            def step(diagonal):
                s = _mla_scores(qn_r, qp_r, kn_r, kp_r, diagonal)
                m_old = m_s[...]
                m_new = jnp.maximum(m_old, jnp.max(s, axis=-1, keepdims=True))
                p = jnp.exp(s - m_new[:, :1])
                alpha = jnp.exp(m_old - m_new)
                l_s[...] = alpha * l_s[...] + jnp.sum(p, axis=-1, keepdims=True)
                acc_s[...] = alpha * acc_s[...] + _dot_nn(p, v_r[...])
                m_s[...] = m_new
            _on_causal_pairs(qi, ki, step)

            @pl.when(ki == qi)
            def _():
                o_r[...] = acc_s[...] / l_s[...]
                lse_r[...] = m_s[...] + jnp.log(l_s[...])

        spec = pltpu.PrefetchScalarGridSpec(num_scalar_prefetch=2, grid=(MLA_HEADS, qt.shape[0]), in_specs=[qs, qs, ks, ks, ks],
                                            out_specs=[qs, qs], scratch_shapes=[pltpu.VMEM((t, LANES), F32)] * 3)
        return _pcall(body, name=f"{name}_fwd", grid_spec=spec, out_shape=[jax.ShapeDtypeStruct((S, MLA_HEADS * LANES), F32)] * 2,
                      compiler_params=_params(("parallel", "arbitrary")))(qt, kt, qn, qp, kn, kp, v)

    def bwd_call(qn, qp, kn, kp, v, o, lse, do):
        S = qn.shape[0]
        t = min(S, 512)
        n = S // t
        qt, kt, qs, ks = _causal_pairs(n, t, by_key=True)
        head = pl.BlockSpec((S, LANES), lambda h, p, qt_r, kt_r: (0, h))
        n_pairs = qt.shape[0]

        def body(qt_r, kt_r, qn_r, qp_r, kn_r, kp_r, v_r, o_r, lse_r, do_r, dqn_r, dqp_r, dkn_r, dkp_r, dv_r, dkn_s, dkp_s, dv_s, dl_s):
            pair = pl.program_id(1)
            qi, ki = qt_r[pair], kt_r[pair]
            rows = pl.ds(pl.multiple_of(qi * t, t), t)

            @pl.when(pair == 0)
            def _():
                dqn_r[...] = jnp.zeros_like(dqn_r)
                dqp_r[...] = jnp.zeros_like(dqp_r)

            @pl.when(ki == 0)
            def _():
                dl_s[rows, :] = jnp.broadcast_to(jnp.sum(do_r[...] * o_r[...], axis=-1, keepdims=True), (t, LANES))

            @pl.when(qi == ki)
            def _():
                dkn_s[...] = jnp.zeros_like(dkn_s)
                dkp_s[...] = jnp.zeros_like(dkp_s)
                dv_s[...] = jnp.zeros_like(dv_s)

            def step(diagonal):
                p = jnp.exp(_mla_scores(qn_r, qp_r, kn_r, kp_r, diagonal) - lse_r[...][:, :1])
                ds = p * (_dot_nt(do_r[...], v_r[...]) - dl_s[rows, :][:, :1])
                dv_s[...] += _dot_tn(p, do_r[...])
                dkn_s[...] += _dot_tn(ds, qn_r[...] * scale)
                dkp_s[...] += _dot_tn(ds, qp_r[...] * scale)
                dqn_r[rows, :] += _dot_nn(ds, kn_r[...])
                dqp_r[rows, :] += _dot_nn(ds, kp_r[...])
            _on_causal_pairs(qi, ki, step)

            @pl.when(qi == n - 1)
            def _():
                dkn_r[...] = dkn_s[...]
                dkp_r[...] = dkp_s[...]
                dv_r[...] = dv_s[...]

            @pl.when(pair == n_pairs - 1)
            def _():
                dqn_r[...] = dqn_r[...] * scale
                dqp_r[...] = dqp_r[...] * scale

        spec = pltpu.PrefetchScalarGridSpec(num_scalar_prefetch=2, grid=(MLA_HEADS, n_pairs), in_specs=[qs, qs, ks, ks, ks, qs, qs, qs],
                                            out_specs=[head, head, ks, ks, ks],
                                            scratch_shapes=[pltpu.VMEM((t, LANES), F32)] * 3 + [pltpu.VMEM((S, LANES), F32)])
        return _pcall(body, name=f"{name}_bwd", grid_spec=spec, out_shape=[jax.ShapeDtypeStruct((S, MLA_HEADS * LANES), F32)] * 5,
                      compiler_params=_params(("parallel", "arbitrary")))(qt, kt, qn, qp, kn, kp, v, o, lse, do)

    @jax.custom_vjp
    def op(qn, qp, kn, kp, v):
        return fwd_call(qn, qp, kn, kp, v)[0]

    def fwd(qn, qp, kn, kp, v):
        o, lse = fwd_call(qn, qp, kn, kp, v)
        return o, (qn, qp, kn, kp, v, o, lse)

    def bwd(res, do):
        return tuple(bwd_call(*res, do))

    op.defvjp(fwd, bwd)
    return op


def _dil_block(q, kp, kc, vp, vc, has_prev):
    scale = LANES ** -0.5
    r = lax.broadcasted_iota(jnp.int32, (DIL_BLOCK, 2 * DIL_BLOCK), 0)
    c = lax.broadcasted_iota(jnp.int32, (DIL_BLOCK, 2 * DIL_BLOCK), 1)
    valid = ((c < DIL_BLOCK) & (c >= r) & has_prev) | ((c >= DIL_BLOCK) & (c - DIL_BLOCK <= r))
    s = jnp.where(valid, _dot_nt(q * scale, jnp.concatenate([kp, kc], axis=0)), NEG)
    m = jnp.max(s, axis=-1, keepdims=True)
    e = jnp.exp(s - m)
    den = jnp.sum(e, axis=-1, keepdims=True)
    o = _dot_nn(e, jnp.concatenate([vp, vc], axis=0)) / den
    return o, jnp.broadcast_to(m + jnp.log(den), o.shape)


DIL_TILE_ROWS = (1024, 1024, 2048)


def _make_dil_attn(name, d, tile_rows):
    def call(q, k, v, cts=None):
        S = q.shape[0]
        span = DIL_BLOCK * d
        G = max(1, min(tile_rows, S) // span)
        n = S // (G * span)
        at = (lambda i: i) if cts is None else (lambda i: n - 1 - i)
        tile = pl.BlockSpec((G * span, LANES), lambda h, i: (at(i), h))
        before = pl.BlockSpec((span, LANES), lambda h, i: (jnp.maximum(at(i) * G - 1, 0), h))

        def rows(r, j):
            return pl.ds(j * DIL_BLOCK, DIL_BLOCK) if d == 1 else pl.ds(r + j * span, DIL_BLOCK, stride=d)

        def over_residues(fn):
            if d == 1:
                fn(0)
            else:
                lax.fori_loop(0, d, lambda r, c: (fn(r), c)[1], 0)

        def block_inputs(r, j, q_r, kb_r, k_r, vb_r, v_r):
            kp = kb_r[rows(r, 0), :] if j == 0 else k_r[rows(r, j - 1), :]
            vp = vb_r[rows(r, 0), :] if j == 0 else v_r[rows(r, j - 1), :]
            return q_r[rows(r, j), :], kp, k_r[rows(r, j), :], vp, v_r[rows(r, j), :]

        if cts is None:
            def body(q_r, kb_r, k_r, vb_r, v_r, o_r, lse_r):
                first = at(pl.program_id(1)) * G

                def residue(r):
                    for j in range(G):
                        o_r[rows(r, j), :], lse_r[rows(r, j), :] = _dil_block(
                            *block_inputs(r, j, q_r, kb_r, k_r, vb_r, v_r), first + j > 0)
                over_residues(residue)
            return _pcall(body, name=f"{name}_fwd", grid=(DIL_GROUP_HEADS, n), in_specs=[tile, before, tile, before, tile],
                          out_specs=[tile, tile], out_shape=[jax.ShapeDtypeStruct(q.shape, F32)] * 2,
                          compiler_params=_params(("parallel", "parallel")))(q, k, k, v, v)

        def body(q_r, kb_r, k_r, vb_r, v_r, do_r, dl_r, dq_r, dk_r, dv_r, ck_s, cv_s):
            first = at(pl.program_id(1)) * G

            @pl.when(pl.program_id(1) == 0)
            def _():
                ck_s[...] = jnp.zeros_like(ck_s)
                cv_s[...] = jnp.zeros_like(cv_s)

            def residue(r):
                owed = None
                for j in range(G):
                    hp = first + j > 0
                    _, vjp = jax.vjp(lambda *a: _dil_block(*a, hp), *block_inputs(r, j, q_r, kb_r, k_r, vb_r, v_r))
                    dq, dkp, dkc, dvp, dvc = vjp((do_r[rows(r, j), :], dl_r[rows(r, j), :]))
                    dq_r[rows(r, j), :] = dq
                    if j == G - 1:
                        dkc, dvc = dkc + ck_s[rows(r, 0), :], dvc + cv_s[rows(r, 0), :]
                    dk_r[rows(r, j), :], dv_r[rows(r, j), :] = dkc, dvc
                    if j == 0:
                        owed = (dkp, dvp)
                    else:
                        dk_r[rows(r, j - 1), :] += dkp
                        dv_r[rows(r, j - 1), :] += dvp
                ck_s[rows(r, 0), :], cv_s[rows(r, 0), :] = owed
            over_residues(residue)
        return _pcall(body, name=f"{name}_bwd", grid=(DIL_GROUP_HEADS, n), in_specs=[tile, before, tile, before, tile, tile, tile],
                      out_specs=[tile] * 3, out_shape=[jax.ShapeDtypeStruct(q.shape, F32)] * 3,
                      scratch_shapes=[pltpu.VMEM((span, LANES), F32)] * 2,
                      compiler_params=_params(("parallel", "arbitrary")))(q, k, k, v, v, *cts)

    @jax.custom_vjp
    def op(q, k, v):
        return tuple(call(q, k, v))

    op.defvjp(lambda q, k, v: (op(q, k, v), (q, k, v)), lambda res, cts: tuple(call(*res, cts=cts)))
    return op


def _pdot(a, b, dims):
    return lax.dot_general(a, b, (dims, ((), ())), precision=lax.Precision.HIGH, preferred_element_type=F32)


DN_LOCAL_CHUNKS = 4


DN_BLOCK_HEADS = 4
DN_BLOCK = DN_BLOCK_HEADS * DN_CHUNK


def _inverse_cotangent(inv, d):
    return -_pdot(inv, _pdot(d, inv, ((1,), (1,))), ((0,), (0,)))


@jax.custom_vjp
def _unit_lower_inverse(a):
    n = a.shape[0]
    eye = (lax.broadcasted_iota(jnp.int32, (n, n), 0) == lax.broadcasted_iota(jnp.int32, (n, n), 1)).astype(F32)
    inv, pw = eye - a, a
    for _ in range(5):
        pw = _pdot(pw, pw, ((1,), (0,)))
        inv = inv + _pdot(inv, pw, ((1,), (0,)))
    return inv


def _unit_lower_inverse_fwd(a):
    inv = _unit_lower_inverse(a)
    return inv, inv


_unit_lower_inverse.defvjp(_unit_lower_inverse_fwd, lambda inv, d: (_inverse_cotangent(inv, d),))


@jax.custom_vjp
def _known_inverse(a, inv):
    return inv


_known_inverse.defvjp(lambda a, inv: (inv, inv), lambda inv, d: (_inverse_cotangent(inv, d), jnp.zeros_like(inv)))


def _dn_local(q, k, v, g, b, known=None):
    C, R = DN_CHUNK, DN_BLOCK
    r = lax.broadcasted_iota(jnp.int32, (R, R), 0)
    c = lax.broadcasted_iota(jnp.int32, (R, R), 1)
    same_head = (r // C) == (c // C)
    incl, strict = same_head & (r >= c), same_head & (r > c)
    avg = jnp.full((R, LANES), 1.0 / LANES, F32)
    rc = lax.broadcasted_iota(jnp.int32, (C, C), 0) >= lax.broadcasted_iota(jnp.int32, (C, C), 1)
    gc_lanes = _pdot(rc.astype(F32), g, ((1,), (0,)))
    us, ws, qes, kds, qks, invs = [], [], [], [], [], []
    for first in range(0, DN_HEADS, DN_BLOCK_HEADS):
        stack = lambda x: jnp.concatenate(_heads(x)[first:first + DN_BLOCK_HEADS], axis=0)
        unstack = lambda x: [x[p * C:(p + 1) * C] for p in range(DN_BLOCK_HEADS)]
        gc, q_s, k_s, v_s, b_s = (stack(x) for x in (gc_lanes, q, k, v, b))
        gc_j = _pdot(avg, gc, ((1,), (1,)))
        decay = jnp.exp(jnp.where(incl, _cat([gc] * (R // LANES)) - gc_j, NEG))
        kb = k_s * b_s
        kk = _pdot(jnp.concatenate([kb, q_s], axis=0), k_s, ((1,), (1,)))
        a = jnp.where(strict, kk[:R] * decay, 0.0)
        inv = _unit_lower_inverse(a) if known is None else _known_inverse(a, known[len(invs) * R:(len(invs) + 1) * R])
        invs.append(inv)
        eg = jnp.exp(gc)
        uw = _pdot(inv, _cat([v_s * b_s, kb * eg]), ((1,), (0,)))
        g_last = jnp.concatenate([jnp.broadcast_to(x[C - 1:C], (C, LANES)) for x in unstack(gc)], axis=0)
        us += unstack(uw[:, :LANES])
        ws += unstack(uw[:, LANES:])
        qes += unstack(q_s * eg)
        kds += unstack(k_s * jnp.exp(g_last - gc))
        qks.append(kk[R:] * decay)
    egl = jnp.broadcast_to(jnp.exp(gc_lanes[C - 1:C]), (8, DN_HEADS * LANES))
    return _cat(us), _cat(ws), _cat(qes), _cat(kds), jnp.concatenate(qks, axis=0), egl, jnp.concatenate(invs, axis=0)


def _dn_scan(u, w, qe, kd, qk, egl, state):
    C = DN_CHUNK
    heads = [slice(h * LANES, (h + 1) * LANES) for h in range(DN_HEADS)]
    ws = [_pdot(jnp.concatenate([w[:, sl], qe[:, sl]], axis=0), state[sl, :], ((1,), (0,))) for sl in heads]
    v_new = [u[:, sl] - x[:C] for sl, x in zip(heads, ws)]
    local = []
    for i, first in enumerate(range(0, DN_HEADS, DN_BLOCK_HEADS)):
        y = _pdot(qk[i * DN_BLOCK:(i + 1) * DN_BLOCK], jnp.concatenate(v_new[first:first + DN_BLOCK_HEADS], axis=0), ((1,), (0,)))
        local += [y[p * C:(p + 1) * C] for p in range(DN_BLOCK_HEADS)]
    o = _cat([x[C:] + y for x, y in zip(ws, local)])
    states = [state[sl, :] * egl[0:1, sl] + _pdot(kd[:, sl], vn, ((0,), (0,))) for sl, vn in zip(heads, v_new)]
    return o, jnp.concatenate(states, axis=0)


def _make_delta_rule(name):
    W = DN_HEADS * LANES
    QK = DN_HEADS * DN_CHUNK

    def local_call(ins, cts=None):
        S = ins[0].shape[0]
        n = S // DN_CHUNK
        per = math.gcd(DN_LOCAL_CHUNKS, n)
        row = pl.BlockSpec((per * DN_CHUNK, W), lambda i: (i, 0))
        qkb = pl.BlockSpec((per * QK, DN_BLOCK), lambda i: (i, 0))
        eg = pl.BlockSpec((per, 8, W), lambda i: (i, 0, 0))
        rows = lambda j: slice(j * DN_CHUNK, (j + 1) * DN_CHUNK)
        qk_rows = lambda j: slice(j * QK, (j + 1) * QK)
        out_rows = [rows, rows, rows, rows, qk_rows]

        if cts is None:
            def body(*refs):
                for j in range(per):
                    res = _dn_local(*[r[rows(j), :] for r in refs[:5]])
                    for o_r, o, at_ in zip(refs[5:10], res[:5], out_rows):
                        o_r[at_(j), :] = o
                    refs[10][j] = res[5]
                    refs[11][qk_rows(j), :] = res[6]
            blockdiag = jax.ShapeDtypeStruct((n * QK, DN_BLOCK), F32)
            return _pcall(body, name=f"{name}_local_fwd", grid=(n // per,), in_specs=[row] * 5, out_specs=[row] * 4 + [qkb, eg, qkb],
                          out_shape=[jax.ShapeDtypeStruct((S, W), F32)] * 4 + [blockdiag, jax.ShapeDtypeStruct((n, 8, W), F32), blockdiag],
                          compiler_params=_params(("parallel",)))(*ins)

        def body(*refs):
            for j in range(per):
                known = refs[5][qk_rows(j), :]
                _, vjp = jax.vjp(lambda *a: _dn_local(*a, known=known)[:6], *[r[rows(j), :] for r in refs[:5]])
                grads = vjp(tuple(r[at_(j), :] for r, at_ in zip(refs[6:11], out_rows)) + (refs[11][j],))
                for o_r, o in zip(refs[12:], grads):
                    o_r[rows(j), :] = o
        return _pcall(body, name=f"{name}_local_bwd", grid=(n // per,), in_specs=[row] * 5 + [qkb] + [row] * 4 + [qkb, eg],
                      out_specs=[row] * 5, out_shape=[jax.ShapeDtypeStruct((S, W), F32)] * 5,
                      compiler_params=_params(("parallel",)))(*ins, *cts)

    def scan_call(ins, saved=None, do=None):
        S = ins[0].shape[0]
        n = S // DN_CHUNK
        at = (lambda i: i) if do is None else (lambda i: n - 1 - i)
        row = pl.BlockSpec((DN_CHUNK, W), lambda i: (at(i), 0))
        qkb = pl.BlockSpec((QK, DN_BLOCK), lambda i: (at(i), 0))
        eg = pl.BlockSpec((None, 8, W), lambda i: (at(i), 0, 0))
        st = pl.BlockSpec((None, W, LANES), lambda i: (at(i), 0, 0))

        if do is None:
            def body(*refs):
                o_r, st_r, s_s = refs[6:]

                @pl.when(pl.program_id(0) == 0)
                def _():
                    s_s[...] = jnp.zeros_like(s_s)
                st_r[...] = s_s[...]
                o_r[...], s_s[...] = _dn_scan(*[r[...] for r in refs[:6]], s_s[...])
            return _pcall(body, name=f"{name}_scan_fwd", grid=(n,), in_specs=[row] * 4 + [qkb, eg], out_specs=[row, st],
                          out_shape=[jax.ShapeDtypeStruct((S, W), F32), jax.ShapeDtypeStruct((n, W, LANES), F32)],
                          scratch_shapes=[pltpu.VMEM((W, LANES), F32)], compiler_params=_params(("arbitrary",)))(*ins)

        def body(*refs):
            st_r, do_r = refs[6:8]
            outs, ds_s = refs[8:14], refs[14]

            @pl.when(pl.program_id(0) == 0)
            def _():
                ds_s[...] = jnp.zeros_like(ds_s)
            _, vjp = jax.vjp(_dn_scan, *[r[...] for r in refs[:6]], st_r[...])
            *grads, ds = vjp((do_r[...], ds_s[...]))
            for o_r, gval in zip(outs, grads):
                o_r[...] = gval
            ds_s[...] = ds
        return _pcall(body, name=f"{name}_scan_bwd", grid=(n,), in_specs=[row] * 4 + [qkb, eg, st, row], out_specs=[row] * 4 + [qkb, eg],
                      out_shape=[jax.ShapeDtypeStruct((S, W), F32)] * 4
                      + [jax.ShapeDtypeStruct((n * QK, DN_BLOCK), F32), jax.ShapeDtypeStruct((n, 8, W), F32)],
                      scratch_shapes=[pltpu.VMEM((W, LANES), F32)], compiler_params=_params(("arbitrary",)))(*ins, saved, do)

    @jax.custom_vjp
    def local(q, k, v, g, b):
        return tuple(local_call((q, k, v, g, b))[:6])

    def local_fwd(*a):
        *outs, inverses = local_call(a)
        return tuple(outs), (*a, inverses)

    local.defvjp(local_fwd, lambda res, cts: tuple(local_call(res, tuple(cts))))

    @jax.custom_vjp
    def scan(u, w, qe, kd, qk, egl):
        return scan_call((u, w, qe, kd, qk, egl))[0]

    def scan_fwd(*a):
        o, states = scan_call(a)
        return o, (a, states)

    scan.defvjp(scan_fwd, lambda res, do: tuple(scan_call(res[0], res[1], do)))
    return lambda q, k, v, g, b: scan(*local(q, k, v, g, b))


def _loss_call(y, target):
    S, D = y.shape
    t = min(S, 512)
    n = S // t
    row = pl.BlockSpec((t, D), lambda i: (i, 0))

    def body(y_r, t_r, loss_r, dy_r, acc_s):
        i = pl.program_id(0)

        @pl.when(i == 0)
        def _():
            acc_s[...] = jnp.zeros_like(acc_s)
        err = y_r[...] - t_r[...]
        dy_r[...] = err * (1.0 / D)
        acc_s[...] += jnp.sum(err * err, axis=0, keepdims=True)

        @pl.when(i == n - 1)
        def _():
            loss_r[...] = jnp.broadcast_to(jnp.sum(acc_s[...], axis=1, keepdims=True) * (0.5 / D), loss_r.shape)

    return _pcall(body, name="loss_head", grid=(n,), in_specs=[row, row],
                  out_specs=[pl.BlockSpec((8, LANES), lambda i: (0, 0)), row],
                  out_shape=[jax.ShapeDtypeStruct((8, LANES), F32), jax.ShapeDtypeStruct((S, D), F32)],
                  scratch_shapes=[pltpu.VMEM((1, D), F32)], compiler_params=_params(("arbitrary",)))(y, target)


def _adamw_call(name, parts, w, m, v, rows=128):
    L, R, C = w.shape
    assert len(parts) == L
    t = _tile(R, rows, 8)
    row = pl.BlockSpec((None, t, C), lambda l, i: (l, i, 0))
    part = lambda k: pl.BlockSpec((N_DEV, t, C), lambda l, i: (0, jnp.where(l == k, i, 0), 0))

    def body(*refs):
        p_refs, (w_r, m_r, v_r, g_r, d_r, nm_r, nv_r) = refs[:L], refs[L:]

        def update(p_r):
            g = p_r[0].astype(F32)
            for s in range(1, N_DEV):
                g = g + p_r[s].astype(F32)
            m_new = ADAM_B1 * m_r[...] + (1.0 - ADAM_B1) * g
            v_new = ADAM_B2 * v_r[...] + (1.0 - ADAM_B2) * (g * g)
            m_hat = m_new / (1.0 - ADAM_B1 ** ADAM_STEP)
            v_hat = v_new / (1.0 - ADAM_B2 ** ADAM_STEP)
            g_r[...] = g
            d_r[...] = -ADAM_LR * (m_hat / (jnp.sqrt(v_hat) + ADAM_EPS) + ADAM_WD * w_r[...])
            nm_r[...] = m_new
            nv_r[...] = v_new

        for k in range(L):
            pl.when(pl.program_id(0) == k)(functools.partial(update, p_refs[k]))

    return _pcall(body, name=name, grid=(L, R // t), in_specs=[part(k) for k in range(L)] + [row, row, row],
                  out_specs=[row] * 4, out_shape=[jax.ShapeDtypeStruct((L, R, C), F32)] * 4,
                  compiler_params=_params(("arbitrary", "arbitrary")))(*parts, w, m, v)


def _my_place():
    x, y, c = lax.axis_index("x"), lax.axis_index("y"), lax.axis_index("c")
    return x, y, c


def _index(x, y, c):
    return 4 * x + 2 * y + c


def _all_gather(vs):
    n = len(vs)

    def body(*refs):
        v_refs, out_refs = refs[:n], refs[n:2 * n]
        send_sems, recv_sems, local_sems = refs[2 * n:]
        x, y, c = _my_place()
        me, sibling = (x, y, c), (x, y, 1 - c)
        chips = [(1 - x, y), (x, 1 - y), (1 - x, 1 - y)]

        def copy(a, k, block, to, src=None):
            rows = out_refs[a].at[_index(*block)]
            return pltpu.make_async_remote_copy(src_ref=rows if src is None else src, dst_ref=rows, send_sem=send_sems.at[a, k],
                                                recv_sem=recv_sems.at[a, k], device_id=to, device_id_type=MESH)

        mine = [pltpu.make_async_copy(v_refs[a], out_refs[a].at[_index(*me)], local_sems.at[a]) for a in range(n)]
        first, passed = [], []
        for a in range(n):
            mine[a].start()
            first += [copy(a, 0, me, sibling, src=v_refs[a])]
            first += [copy(a, 1 + j, me, (*chip, c), src=v_refs[a]) for j, chip in enumerate(chips)]
        for cp in first:
            cp.start()
        for j, chip in enumerate(chips):
            for a in range(n):
                copy(a, 1 + j, (*chip, c), me).wait_recv()
                passed.append(copy(a, 4 + j, (*chip, c), sibling))
                passed[-1].start()
        for a in range(n):
            copy(a, 0, sibling, me).wait_recv()
            for j, chip in enumerate(chips):
                copy(a, 4 + j, (*chip, 1 - c), me).wait_recv()
        for cp in first + passed:
            cp.wait_send()
        for a in range(n):
            mine[a].wait()

    any_ = pl.BlockSpec(memory_space=pl.ANY)
    return _pcall(body, name="gather_weights", in_specs=[any_] * n, out_specs=[any_] * n,
                  out_shape=[jax.ShapeDtypeStruct((N_DEV,) + v.shape, v.dtype) for v in vs],
                  scratch_shapes=[pltpu.SemaphoreType.DMA((n, 7)), pltpu.SemaphoreType.DMA((n, 7)), pltpu.SemaphoreType.DMA((n,))])(*vs)


def _all_to_all(vs):
    n = len(vs)

    def body(*refs):
        v_refs, out_refs = refs[:n], refs[n:2 * n]
        send_sems, recv_sems, local_sems = refs[2 * n:]
        x, y, c = _my_place()
        me = _index(x, y, c)
        mine = [pltpu.make_async_copy(v_refs[a].at[me], out_refs[a].at[me], local_sems.at[a]) for a in range(n)]
        copies = []
        for a in range(n):
            mine[a].start()
        for k in range(1, N_DEV):
            px = 1 - x if k & 4 else x
            py = 1 - y if k & 2 else y
            pc = 1 - c if k & 1 else c
            for a in range(n):
                cp = pltpu.make_async_remote_copy(src_ref=v_refs[a].at[_index(px, py, pc)], dst_ref=out_refs[a].at[me],
                                                  send_sem=send_sems.at[a, k - 1], recv_sem=recv_sems.at[a, k - 1],
                                                  device_id=(px, py, pc), device_id_type=MESH)
                cp.start()
                copies.append(cp)
        for cp in copies:
            cp.wait()
        for a in range(n):
            mine[a].wait()

    any_ = pl.BlockSpec(memory_space=pl.ANY)
    return _pcall(body, name="exchange_grads", in_specs=[any_] * n, out_specs=[any_] * n,
                  out_shape=[jax.ShapeDtypeStruct(v.shape, v.dtype) for v in vs],
                  scratch_shapes=[pltpu.SemaphoreType.DMA((n, 7)), pltpu.SemaphoreType.DMA((n, 7)), pltpu.SemaphoreType.DMA((n,))])(*vs)


_HBM = pl.BlockSpec(memory_space=pltpu.HBM)
_SEM = pl.BlockSpec(memory_space=pltpu.SEMAPHORE)
_EFFECT = pltpu.SideEffectType.DATAFLOW_SIDE_EFFECTING


def _direct_copies(gather, v_refs, land_refs, send_sems, recv_sems, local_sems):
    x, y, c = _my_place()
    me = _index(x, y, c)
    local, remote = [], []
    for a, (v_ref, land_ref) in enumerate(zip(v_refs, land_refs)):
        local.append(pltpu.make_async_copy(v_ref if gather else v_ref.at[me], land_ref.at[me], local_sems.at[a]))
    for k in range(1, N_DEV):
        px = 1 - x if k & 4 else x
        py = 1 - y if k & 2 else y
        pc = 1 - c if k & 1 else c
        for a, (v_ref, land_ref) in enumerate(zip(v_refs, land_refs)):
            sem = a * (N_DEV - 1) + k - 1
            remote.append(pltpu.make_async_remote_copy(
                src_ref=v_ref if gather else v_ref.at[_index(px, py, pc)], dst_ref=land_ref.at[me], send_sem=send_sems.at[sem],
                recv_sem=recv_sems.at[sem], device_id=(px, py, pc), device_id_type=MESH))
    return local, remote


def _exchange_start(name, vs, gather, thru):
    n = len(vs)
    lands = [lax.empty((N_DEV,) + v.shape if gather else v.shape, v.dtype) for v in vs]

    def body(*refs):
        v_refs, land_refs = refs[:n], refs[n:2 * n]
        send_sems, recv_sems, local_sems = refs[2 * n + 1:2 * n + 4]
        local, remote = _direct_copies(gather, v_refs, land_refs, send_sems, recv_sems, local_sems)
        for cp in local + remote:
            cp.start()

    hbm = lambda a: pltpu.HBM(a.shape, a.dtype)
    res = _pcall(body, name=name,
                 out_shape=(pltpu.SemaphoreType.DMA((n * (N_DEV - 1),)), pltpu.SemaphoreType.DMA((n * (N_DEV - 1),)), pltpu.SemaphoreType.DMA((n,)),
                            *[hbm(a) for a in (*vs, *lands, thru)]),
                 in_specs=[_HBM] * (2 * n + 1), out_specs=(_SEM, _SEM, _SEM, *[_HBM] * (2 * n + 1)),
                 input_output_aliases={i: 3 + i for i in range(2 * n + 1)},
                 compiler_params=pltpu.CompilerParams(has_side_effects=_EFFECT))(
        *[pltpu.with_memory_space_constraint(a, pltpu.HBM) for a in (*vs, *lands, thru)])
    return (gather, res[:3], res[3:3 + n], res[3 + n:3 + 2 * n]), res[3 + 2 * n]


def _exchange_wait(name, started, after):
    gather, sems, vs, lands = started
    n = len(vs)

    def body(*refs):
        v_refs, land_refs = refs[:n], refs[n:2 * n]
        send_sems, recv_sems, local_sems = refs[2 * n:2 * n + 3]
        local, remote = _direct_copies(gather, v_refs, land_refs, send_sems, recv_sems, local_sems)
        for cp in local:
            cp.wait()
        for cp in remote:
            cp.wait_send()
            cp.wait_recv()

    hbm = lambda a: pltpu.HBM(a.shape, a.dtype)
    res = _pcall(body, name=name, out_shape=tuple(hbm(a) for a in (*vs, *lands)),
                 in_specs=[_HBM] * (2 * n) + [_SEM] * 3 + [pl.BlockSpec(memory_space=pl.ANY)], out_specs=tuple([_HBM] * (2 * n)),
                 input_output_aliases={i: i for i in range(2 * n)},
                 compiler_params=pltpu.CompilerParams(has_side_effects=_EFFECT))(*vs, *lands, *sems, after)
    return list(res[n:])


W_IN_SHARD = IN_WIDTH // N_DEV
SEG_ORDER = ("q_lat", "c_kv", "k_pe", "z_a", "dn_qkv", "dn_ab", "z_b", "dil_qkv", "z_c", "gate")
SEG_WIDTH = (384, 256, LANES, 512, 1536, LANES, 512, 4608, 512, 3072)


def _w_in_plan():
    plan = []

    def add(seg, c0, c1, dst):
        while c0 < c1:
            d = c0 // W_IN_SHARD
            e = min(c1, (d + 1) * W_IN_SHARD)
            plan.append((seg, dst, d, c0 - d * W_IN_SHARD, e - c0))
            dst += e - c0
            c0 = e

    half = MLA_ROPE // 2
    for i, name in enumerate(SEG_ORDER):
        if name == "k_pe":
            o = _SEG["k_pe"][0]
            add(i, o, o + half, 0)
            add(i, o + half, o + 2 * half, LANES // 2)
        elif name == "dn_ab":
            o = _SEG["dn_a"][0]
            add(i, o, o + 2 * DN_HEADS, 0)
        else:
            o, w = _SEG[name]
            add(i, o, o + w, 0)
    return plan


def _make_w_in_segments(name):
    plan = _w_in_plan()
    nseg = len(SEG_ORDER)
    t = 256

    def fwd_call(g):
        L = g.shape[1]

        def body(g_ref, *o_refs):
            for i in (SEG_ORDER.index("k_pe"), SEG_ORDER.index("dn_ab")):
                o_refs[i][...] = jnp.zeros_like(o_refs[i])
            for seg, dst, d, src, n in plan:
                o_refs[seg][:, dst:dst + n] = g_ref[d, :, src:src + n]

        return _pcall(body, name=f"{name}_fwd", grid=(L, D_MODEL // t),
                      in_specs=[pl.BlockSpec((N_DEV, None, t, W_IN_SHARD), lambda l, i: (0, l, i, 0))],
                      out_specs=[pl.BlockSpec((None, t, w), lambda l, i: (l, i, 0)) for w in SEG_WIDTH],
                      out_shape=[jax.ShapeDtypeStruct((L, D_MODEL, w), g.dtype) for w in SEG_WIDTH],
                      compiler_params=_params(("parallel", "parallel")))(g)

    def bwd_call(ds):
        L = ds[0].shape[0]

        def body(*refs):
            d_refs, g_ref = refs[:nseg], refs[nseg]
            for seg, dst, d, src, n in plan:
                g_ref[d, :, src:src + n] = d_refs[seg][:, dst:dst + n]

        return _pcall(body, name=f"{name}_bwd", grid=(L, D_MODEL // t),
                      in_specs=[pl.BlockSpec((None, t, w), lambda l, i: (l, i, 0)) for w in SEG_WIDTH],
                      out_specs=pl.BlockSpec((N_DEV, None, t, W_IN_SHARD), lambda l, i: (0, l, i, 0)),
                      out_shape=jax.ShapeDtypeStruct((N_DEV, L, D_MODEL, W_IN_SHARD), ds[0].dtype),
                      compiler_params=_params(("parallel", "parallel")))(*ds)

    @jax.custom_vjp
    def op(g):
        return tuple(fwd_call(g))

    op.defvjp(lambda g: (op(g), None), lambda _, ds: (bwd_call(tuple(ds)),))
    return op


def _pe_pad(a):
    h = MLA_ROPE // 2
    z = jnp.zeros(a.shape[:-1] + (h,), a.dtype)
    return jnp.concatenate([a[..., :h], z, a[..., h:], z], axis=-1)


def _layer_norm(tag, x, norm_g):
    return _make_rowwise(f"{tag}_norm", _f_norm, 512)((x,), (), (norm_g[None, :],), ())[0]


def _layer(tag, x, tables, W):
    h = _layer_norm(tag, x, W["norm_g"])
    return _layer_tail(tag, x, _make_multi_linear(f"{tag}_inproj", 10, INPROJ_DTYPE)(h, W["w_in_segments"]), tables, W)


def _layer_tail(tag, x, segments, tables, W):
    cos_p, sin_p, cos_h, sin_h = tables
    row = lambda a: a[None, :]
    q_lat, c_kv, kpe, z_a, dn_qkv, ab, z_b, dil_qkv, z_c, gl = segments

    qn_lat, ckvn, kp = _make_rowwise(f"{tag}_mla_a", _f_mla_a, 512)(
        (q_lat, c_kv, kpe), (cos_p, sin_p),
        (row(W["mla_q_a_norm_g"]), row(W["mla_kv_a_norm_g"]), row(_pe_pad(W["mla_k_norm_g"][LANES:]))), ())
    wq = W["mla_w_q_b"].reshape(MLA_Q_RANK, MLA_HEADS, MLA_QK)
    wq = jnp.concatenate([wq[:, :, :LANES].reshape(MLA_Q_RANK, -1), _pe_pad(wq[:, :, LANES:]).reshape(MLA_Q_RANK, -1)], axis=1)
    wkv = W["mla_w_kv_b"].reshape(MLA_KV_RANK, MLA_HEADS, 2 * LANES)
    (q8,) = _make_multi_linear(f"{tag}_qb", 1)(qn_lat, (wq,))
    kn_raw, v_mla = _make_multi_linear(f"{tag}_kvb", 2)(
        ckvn, (wkv[:, :, :LANES].reshape(MLA_KV_RANK, -1), wkv[:, :, LANES:].reshape(MLA_KV_RANK, -1)))
    qn, qp, kn = _make_rowwise(f"{tag}_mla_b", _f_mla_b, 512)(
        (q8, kn_raw), (cos_p, sin_p),
        (row(W["mla_q_norm_g"][:LANES]), row(_pe_pad(W["mla_q_norm_g"][LANES:])), row(W["mla_k_norm_g"][:LANES])), ())
    y_a = _make_mla_attn(f"{tag}_mla")(qn, qp, kn, kp, v_mla)

    mixed = _make_conv(f"{tag}_conv")(dn_qkv, W["dn_conv_w"])
    lane_head = jnp.arange(DN_HEADS * LANES) // LANES
    e_a = (jnp.arange(LANES)[:, None] == lane_head[None, :]).astype(F32)
    e_b = (jnp.arange(LANES)[:, None] == lane_head[None, :] + DN_HEADS).astype(F32)
    q_dn, k_dn, v_dn, g_dn, b_dn = _make_rowwise(f"{tag}_dn_pre", _f_dn_pre, 512)(
        (mixed, ab), (), (row(jnp.repeat(W["dn_a_log"], LANES)), row(jnp.repeat(W["dn_dt_bias"], LANES))), (e_a, e_b))
    o_dn = _make_delta_rule(f"{tag}_dn")(q_dn, k_dn, v_dn, g_dn, b_dn)

    qkv_dil = _make_rowwise(f"{tag}_dil_pre", _f_dil_pre, 256)(
        (dil_qkv,), (cos_h, sin_h), (row(W["dil_q_norm_g"]), row(W["dil_k_norm_g"])), ())
    n_groups = len(DIL_DILATIONS)
    o_lse = [_make_dil_attn(f"{tag}_dil{g}", d, DIL_TILE_ROWS[g])(qkv_dil[g], qkv_dil[n_groups + g], qkv_dil[2 * n_groups + g])
             for g, d in enumerate(DIL_DILATIONS)]

    ya, yb, yc = _make_rowwise(f"{tag}_merge_a", _f_merge_a, 256)(
        (y_a, z_a, o_dn, z_b, *[o for o, _ in o_lse], *[l for _, l in o_lse], z_c), (), (row(W["dn_out_norm_g"]),), ())
    (b0,) = _make_multi_linear(f"{tag}_br0", 1)(ya, (W["w_branch"][0],))
    (b1,) = _make_multi_linear(f"{tag}_br1", 1)(yb, (W["w_branch"][1],))
    (b2,) = _make_multi_linear(f"{tag}_br2", 1)(yc, (W["w_branch"][2],))
    (mix,) = _make_rowwise(f"{tag}_merge_b", _f_merge_b, 256)((b0, b1, b2, gl), (), (), ())
    return _make_resid_linear(f"{tag}_out")(x, mix, W["w_out"])


SHARDED = (("w_in", (D_MODEL, W_IN_SHARD)), ("mla_w_q_b", (MLA_Q_RANK, MLA_HEADS * MLA_QK // N_DEV)),
           ("mla_w_kv_b", (MLA_KV_RANK, MLA_HEADS * 2 * LANES // N_DEV)), ("w_branch", (3 * BRANCH_W, D_MODEL // N_DEV)),
           ("w_out", (D_MODEL // N_DEV, D_MODEL)), ("dn_conv_w", (DN_CONV, 3 * DN_HEADS * LANES // N_DEV)))
SMALL = (("norm_g", D_MODEL), ("mla_q_a_norm_g", MLA_Q_RANK), ("mla_kv_a_norm_g", MLA_KV_RANK), ("mla_q_norm_g", MLA_QK),
         ("mla_k_norm_g", MLA_QK), ("dn_a_log", DN_HEADS), ("dn_dt_bias", DN_HEADS), ("dn_out_norm_g", LANES),
         ("dil_q_norm_g", LANES), ("dil_k_norm_g", LANES))
WEIGHTS = ("norm_g", "w_in", "mla_q_a_norm_g", "mla_w_q_b", "mla_kv_a_norm_g", "mla_w_kv_b", "mla_q_norm_g", "mla_k_norm_g",
           "dn_conv_w", "dn_a_log", "dn_dt_bias", "dn_out_norm_g", "dil_q_norm_g", "dil_k_norm_g", "w_branch", "w_out")


def _round_up(n, m):
    return -(-n // m) * m


def _pack_vectors(pieces):
    return jnp.concatenate([jnp.pad(p, (0, _round_up(p.shape[0], LANES) - p.shape[0])) for p in pieces]).reshape(-1, LANES)


def _unpack_vectors(flat, sizes):
    out, off = [], 0
    flat = flat.reshape(-1)
    for n in sizes:
        out.append(flat[off:off + n])
        off += _round_up(n, LANES)
    return out


def _whole_weights(g, small):
    W = dict(small)
    W["mla_w_q_b"] = g["mla_w_q_b"].transpose(1, 0, 2).reshape(MLA_Q_RANK, -1)
    W["mla_w_kv_b"] = g["mla_w_kv_b"].transpose(1, 0, 2).reshape(MLA_KV_RANK, -1)
    W["w_branch"] = g["w_branch"].reshape(N_DEV, 3, BRANCH_W, -1).transpose(1, 2, 0, 3).reshape(3, BRANCH_W, D_MODEL)
    W["w_out"] = g["w_out"].reshape(D_MODEL, D_MODEL)
    W["dn_conv_w"] = g["dn_conv_w"].transpose(1, 0, 2).reshape(DN_CONV, -1)
    return W


def kernel(x, positions, norm_g, w_in, mla_q_a_norm_g, mla_w_q_b, mla_kv_a_norm_g, mla_w_kv_b, mla_q_norm_g, mla_k_norm_g, dn_conv_w, dn_a_log, dn_dt_bias, dn_out_norm_g, dil_q_norm_g, dil_k_norm_g, w_branch, w_out, loss_target, m_norm_g, m_w_in, m_mla_q_a_norm_g, m_mla_w_q_b, m_mla_kv_a_norm_g, m_mla_w_kv_b, m_mla_q_norm_g, m_mla_k_norm_g, m_dn_conv_w, m_dn_a_log, m_dn_dt_bias, m_dn_out_norm_g, m_dil_q_norm_g, m_dil_k_norm_g, m_w_branch, m_w_out, v_norm_g, v_w_in, v_mla_q_a_norm_g, v_mla_w_q_b, v_mla_kv_a_norm_g, v_mla_w_kv_b, v_mla_q_norm_g, v_mla_k_norm_g, v_dn_conv_w, v_dn_a_log, v_dn_dt_bias, v_dn_out_norm_g, v_dil_q_norm_g, v_dil_k_norm_g, v_w_branch, v_w_out):
    w = dict(norm_g=norm_g, w_in=w_in, mla_q_a_norm_g=mla_q_a_norm_g, mla_w_q_b=mla_w_q_b, mla_kv_a_norm_g=mla_kv_a_norm_g,
             mla_w_kv_b=mla_w_kv_b, mla_q_norm_g=mla_q_norm_g, mla_k_norm_g=mla_k_norm_g, dn_conv_w=dn_conv_w, dn_a_log=dn_a_log,
             dn_dt_bias=dn_dt_bias, dn_out_norm_g=dn_out_norm_g, dil_q_norm_g=dil_q_norm_g, dil_k_norm_g=dil_k_norm_g,
             w_branch=w_branch, w_out=w_out)
    m = dict(norm_g=m_norm_g, w_in=m_w_in, mla_q_a_norm_g=m_mla_q_a_norm_g, mla_w_q_b=m_mla_w_q_b, mla_kv_a_norm_g=m_mla_kv_a_norm_g,
             mla_w_kv_b=m_mla_w_kv_b, mla_q_norm_g=m_mla_q_norm_g, mla_k_norm_g=m_mla_k_norm_g, dn_conv_w=m_dn_conv_w,
             dn_a_log=m_dn_a_log, dn_dt_bias=m_dn_dt_bias, dn_out_norm_g=m_dn_out_norm_g, dil_q_norm_g=m_dil_q_norm_g,
             dil_k_norm_g=m_dil_k_norm_g, w_branch=m_w_branch, w_out=m_w_out)
    v = dict(norm_g=v_norm_g, w_in=v_w_in, mla_q_a_norm_g=v_mla_q_a_norm_g, mla_w_q_b=v_mla_w_q_b, mla_kv_a_norm_g=v_mla_kv_a_norm_g,
             mla_w_kv_b=v_mla_w_kv_b, mla_q_norm_g=v_mla_q_norm_g, mla_k_norm_g=v_mla_k_norm_g, dn_conv_w=v_dn_conv_w,
             dn_a_log=v_dn_a_log, dn_dt_bias=v_dn_dt_bias, dn_out_norm_g=v_dn_out_norm_g, dil_q_norm_g=v_dil_q_norm_g,
             dil_k_norm_g=v_dil_k_norm_g, w_branch=v_w_branch, w_out=v_w_out)
    x2, target = x[0], loss_target[0]
    pos = positions[0][:, None]

    names = [n for n, _ in SHARDED]
    view = lambda t, n, s: t[n].reshape((DEPTH,) + s)
    shards = [[(view(w, n, s) if n == "dn_conv_w" else view(w, n, s).astype(BF16))[l] for n, s in SHARDED] for l in range(DEPTH)]
    small = [{n: w[n][l] for n, _ in SMALL} for l in range(DEPTH)]
    gathered0 = dict(zip(names, _all_gather(shards[0])))
    gathering1, pos = _exchange_start("gather_layer1_start", shards[1], True, pos)
    tables = _rope_tables(pos, _rope_consts())

    def layer(l, g, small_l, x_l):
        tag = f"l{l}"
        seg_op = _make_w_in_segments(f"{tag}_w_in_segments")
        w_segs, vjp_segs = jax.vjp(lambda gw: tuple(s[0] for s in seg_op(gw[:, None])), g["w_in"])
        h, vjp_norm = jax.vjp(lambda x_, ng: _layer_norm(tag, x_, ng), x_l, small_l["norm_g"])
        segs = tuple(_mm(f"{tag}_inproj_fwd{i}", h, w_, "nn", out_dtype=INPROJ_DTYPE, **INPROJ_TILES["nn"]) for i, w_ in enumerate(w_segs))
        rest_g = {n: a for n, a in g.items() if n != "w_in"}
        rest_s = {n: a for n, a in small_l.items() if n != "norm_g"}
        y, vjp_tail = jax.vjp(lambda sg, x_, gg, ss: _layer_tail(tag, x_, sg, tables, _whole_weights(gg, ss)), segs, x_l, rest_g, rest_s)

        def backward(dy):
            dsegs, dx_skip, d_rest_g, d_rest_s = vjp_tail(dy)
            dws = tuple(_mm(f"{tag}_inproj_dw{i}", h, d, "tn", out_dtype=w_.dtype, **INPROJ_TILES["tn"])
                        for i, (w_, d) in enumerate(zip(w_segs, dsegs)))
            dg = dict(d_rest_g, w_in=vjp_segs(dws)[0])
            exchanging, first = _exchange_start(f"exchange_{tag}_start", [dg[n] for n in names], False, dsegs[0])
            dsegs = (first,) + tuple(dsegs[1:])
            dh = None
            for i, group in enumerate(INPROJ_DH_GROUPS):
                dh = _mm_nt_sum(f"{tag}_inproj_dh{i}", [(dsegs[s], w_segs[s]) for s in group], dh,
                                h.dtype if i == len(INPROJ_DH_GROUPS) - 1 else F32)
            dx_norm, d_norm_g = vjp_norm(dh)
            return exchanging, dx_skip + dx_norm, dict(d_rest_s, norm_g=d_norm_g)
        return y, backward

    y0, backward0 = layer(0, gathered0, small[0], x2)
    gathered1 = dict(zip(names, _exchange_wait("gather_layer1_wait", gathering1, y0)))
    y1, backward1 = layer(1, gathered1, small[1], y0)
    loss_splat, dy = _loss_call(y1, target)
    loss = lax.psum(loss_splat[0, 0], ("x", "y", "c"))
    exchanging1, d_y0, g_small1 = backward1(dy)
    exchanging0, g_x, g_small0 = backward0(d_y0)

    g_small = (g_small0, g_small1)
    sizes = [k for _ in range(DEPTH) for _, k in SMALL]
    g_vec = _pack_vectors([g_small[l][n] for l in range(DEPTH) for n, _ in SMALL])
    (parts_vec,) = _all_to_all([jnp.broadcast_to(g_vec[None], (N_DEV,) + g_vec.shape)])
    parts1 = _exchange_wait("exchange_l1_wait", exchanging1, parts_vec)
    parts0 = _exchange_wait("exchange_l0_wait", exchanging0, parts_vec)

    vec = lambda t: _pack_vectors([t[n][l] for l in range(DEPTH) for n, _ in SMALL])[None]
    outs = {}
    for i, (n, s) in enumerate(SHARDED):
        res = _adamw_call(f"adamw_{n}", [parts0[i], parts1[i]], view(w, n, s), view(m, n, s), view(v, n, s))
        outs[n] = [o.reshape(w[n].shape) for o in res]
    vec_outs = [_unpack_vectors(o, sizes) for o in _adamw_call("adamw_vectors", [parts_vec], vec(w), vec(m), vec(v))]
    for i, (n, _) in enumerate(SMALL):
        outs[n] = [jnp.stack([o[l * len(SMALL) + i] for l in range(DEPTH)]) for o in vec_outs]
    return (loss, g_x[None], *[outs[n][k] for k in range(4) for n in WEIGHTS])
```

```python
import functools
import math

import jax
import jax.numpy as jnp
from jax import lax
from jax.experimental import pallas as pl
from jax.experimental.pallas import tpu as pltpu

F32 = jnp.float32
BF16 = jnp.bfloat16
HI = lax.Precision.HIGHEST
MESH = pl.DeviceIdType.MESH

N_DEV = 8
D_MODEL = 1024
DEPTH = 2
RMS_EPS = 1e-6
ROPE_THETA = 10000.0
LANES = 128
MLA_HEADS = 4
MLA_ROPE = 64
MLA_QK = 192
MLA_Q_RANK = 384
MLA_KV_RANK = 256
DN_HEADS = 4
DN_CHUNK = 64
DN_CONV = 4
DIL_HEADS = 12
DIL_GROUP_HEADS = 4
DIL_DILATIONS = (1, 4, 16)
DIL_BLOCK = 128
BRANCH_W = 512
IN_WIDTH = 11464
NEG = -1e30
VMEM_LIMIT = 56 * 1024 * 1024

ADAM_LR, ADAM_B1, ADAM_B2, ADAM_EPS, ADAM_WD, ADAM_STEP = 0.001, 0.9, 0.999, 1e-08, 0.01, 10

_SEG = {}
_off = 0
for _n, _w in (("q_lat", 384), ("c_kv", 256), ("k_pe", 64), ("z_a", 512), ("dn_qkv", 1536), ("dn_a", 4), ("dn_b", 4),
               ("z_b", 512), ("dil_qkv", 4608), ("z_c", 512), ("gate", 3072)):
    _SEG[_n] = (_off, _w)
    _off += _w
assert _off == IN_WIDTH


def _pcall(body, **kw):
    return pl.pallas_call(body, **kw)


def _params(sem=None):
    return pltpu.CompilerParams(dimension_semantics=sem, vmem_limit_bytes=VMEM_LIMIT)


def _tile(n, target, mult):
    t = (min(n, target) // mult) * mult
    while t >= mult:
        if n % t == 0:
            return t
        t -= mult
    return n


def _mm(name, a, b, mode, out_dtype=F32, acc=None, tm=1024, tn=512, tk=1024):
    if mode == "nn":
        (M, K), (_, N) = a.shape, b.shape
    elif mode == "nt":
        (M, K), (N, _) = a.shape, b.shape
    else:
        (K, M), (_, N) = a.shape, b.shape
    tm, tn, tk = _tile(M, tm, LANES), _tile(N, tn, LANES), _tile(K, tk, LANES)
    nk = K // tk
    dims = {"nn": (((1,), (0,)), ((), ())), "nt": (((1,), (1,)), ((), ())), "tn": (((0,), (0,)), ((), ()))}[mode]
    a_spec = pl.BlockSpec((tk, tm), lambda i, j, k: (k, i)) if mode == "tn" else pl.BlockSpec((tm, tk), lambda i, j, k: (i, k))
    b_spec = pl.BlockSpec((tn, tk), lambda i, j, k: (j, k)) if mode == "nt" else pl.BlockSpec((tk, tn), lambda i, j, k: (k, j))
    o_spec = pl.BlockSpec((tm, tn), lambda i, j, k: (i, j))
    has_acc = acc is not None

    def body(*refs):
        a_ref, b_ref = refs[:2]
        c_ref = refs[2] if has_acc else None
        o_ref = refs[3] if has_acc else refs[2]
        prod = lax.dot_general(a_ref[...].astype(BF16), b_ref[...].astype(BF16), dims, preferred_element_type=F32)
        if nk == 1:
            o_ref[...] = (prod + c_ref[...].astype(F32) if has_acc else prod).astype(out_dtype)
            return
        acc_ref = refs[-1]
        k = pl.program_id(2)

        @pl.when(k == 0)
        def _():
            acc_ref[...] = prod + c_ref[...].astype(F32) if has_acc else prod

        @pl.when(k > 0)
        def _():
            acc_ref[...] += prod

        @pl.when(k == nk - 1)
        def _():
            o_ref[...] = acc_ref[...].astype(out_dtype)

    ins = [a, b] + ([acc] if has_acc else [])
    in_specs = [a_spec, b_spec] + ([o_spec] if has_acc else [])
    return _pcall(body, name=name, grid=(M // tm, N // tn, nk), in_specs=in_specs, out_specs=o_spec,
                  out_shape=jax.ShapeDtypeStruct((M, N), out_dtype), scratch_shapes=[pltpu.VMEM((tm, tn), F32)] if nk > 1 else [],
                  compiler_params=_params(("parallel", "parallel", "arbitrary")))(*ins)


def _mm_nt_sum(name, pairs, acc, out_dtype, tm=512, tn=1024):
    M, N = pairs[0][0].shape[0], pairs[0][1].shape[0]
    tm, tn = _tile(M, tm, LANES), _tile(N, tn, LANES)
    n = len(pairs)
    has_acc = acc is not None
    o_spec = pl.BlockSpec((tm, tn), lambda i, j: (i, j))

    def body(*refs):
        total = refs[2 * n][...].astype(F32) if has_acc else None
        for a_ref, b_ref in zip(refs[:n], refs[n:2 * n]):
            prod = lax.dot_general(a_ref[...].astype(BF16), b_ref[...].astype(BF16), (((1,), (1,)), ((), ())), preferred_element_type=F32)
            total = prod if total is None else total + prod
        refs[-1][...] = total.astype(out_dtype)

    in_specs = ([pl.BlockSpec((tm, a.shape[1]), lambda i, j: (i, 0)) for a, _ in pairs]
                + [pl.BlockSpec((tn, b.shape[1]), lambda i, j: (j, 0)) for _, b in pairs] + ([o_spec] if has_acc else []))
    return _pcall(body, name=name, grid=(M // tm, N // tn), in_specs=in_specs, out_specs=o_spec,
                  out_shape=jax.ShapeDtypeStruct((M, N), out_dtype), compiler_params=_params(("parallel", "parallel")))(
        *[a for a, _ in pairs], *[b for _, b in pairs], *([acc] if has_acc else []))


INPROJ_DH_GROUPS = ((0, 1, 2, 3, 5, 6, 8), (4,), (9,), (7,))

INPROJ_DTYPE = BF16

INPROJ_TILES = {"nn": dict(tm=1024, tn=1536, tk=1024), "tn": dict(tm=1024, tn=768, tk=2048)}


def _make_multi_linear(name, n, out_dtype=F32):
    @jax.custom_vjp
    def op(h, ws):
        return tuple(_mm(f"{name}_fwd{i}", h, w, "nn", out_dtype=out_dtype) for i, w in enumerate(ws))

    def fwd(h, ws):
        return op(h, ws), (h, ws)

    def bwd(res, douts):
        h, ws = res
        dh = None
        for i, (w, d) in enumerate(zip(ws, douts)):
            dh = _mm(f"{name}_dh{i}", d, w, "nt", acc=dh, out_dtype=h.dtype if i == len(ws) - 1 else F32)
        dws = tuple(_mm(f"{name}_dw{i}", h, d, "tn", out_dtype=w.dtype) for i, (w, d) in enumerate(zip(ws, douts)))
        return dh, dws

    op.defvjp(fwd, bwd)
    return op


def _make_resid_linear(name):
    @jax.custom_vjp
    def op(x, a, w):
        return _mm(f"{name}_fwd", a, w, "nn", acc=x)

    def fwd(x, a, w):
        return op(x, a, w), (a, w)

    def bwd(res, dy):
        a, w = res
        return dy, _mm(f"{name}_da", dy, w, "nt", out_dtype=a.dtype), _mm(f"{name}_dw", a, dy, "tn", out_dtype=w.dtype)

    op.defvjp(fwd, bwd)
    return op


def _make_rowwise(name, f, tile):
    def specs(rows, aux, params, consts, t):
        row = [pl.BlockSpec((t, a.shape[1]), lambda i: (i, 0)) for a in (*rows, *aux)]
        full = [pl.BlockSpec(p.shape, lambda i: (0, 0)) for p in (*params, *consts)]
        return row, full

    def fwd_call(rows, aux, params, consts):
        S = rows[0].shape[0]
        t = min(tile, S)
        n_in = len(rows) + len(aux) + len(params) + len(consts)
        shp = lambda a: jax.ShapeDtypeStruct((t, a.shape[1]), a.dtype)
        outs = jax.eval_shape(f, *[shp(a) for a in (*rows, *aux)], *params, *consts)
        row_specs, full_specs = specs(rows, aux, params, consts, t)

        def body(*refs):
            res = f(*[r[...] for r in refs[:n_in]])
            for o_ref, o in zip(refs[n_in:], res):
                o_ref[...] = o

        return _pcall(body, name=f"{name}_fwd", grid=(S // t,), in_specs=row_specs + full_specs,
                      out_specs=[pl.BlockSpec((t, o.shape[1]), lambda i: (i, 0)) for o in outs],
                      out_shape=[jax.ShapeDtypeStruct((S, o.shape[1]), o.dtype) for o in outs],
                      compiler_params=_params(("parallel",)))(*rows, *aux, *params, *consts)

    def bwd_call(rows, aux, params, consts, douts):
        S = rows[0].shape[0]
        t = min(tile, S)
        nr, na, npar, nc, nd = len(rows), len(aux), len(params), len(consts), len(douts)
        row_specs, full_specs = specs(rows, aux, params, consts, t)

        def body(*refs):
            vals = [r[...] for r in refs[:nr + na + npar + nc]]
            rv, av = vals[:nr], vals[nr:nr + na]
            pv, cv = vals[nr + na:nr + na + npar], vals[nr + na + npar:]
            dv = tuple(r[...] for r in refs[nr + na + npar + nc:nr + na + npar + nc + nd])
            out_refs = refs[nr + na + npar + nc + nd:]
            _, vjp = jax.vjp(lambda *rp: f(*rp[:nr], *av, *rp[nr:], *cv), *rv, *pv)
            grads = vjp(dv)
            for o_ref, g in zip(out_refs[:nr], grads[:nr]):
                o_ref[...] = g
            first = pl.program_id(0) == 0
            for o_ref, g in zip(out_refs[nr:], grads[nr:]):
                @pl.when(first)
                def _(o_ref=o_ref):
                    o_ref[...] = jnp.zeros_like(o_ref)
                o_ref[...] += g

        res = _pcall(body, name=f"{name}_bwd", grid=(S // t,),
                     in_specs=row_specs + full_specs + [pl.BlockSpec((t, d.shape[1]), lambda i: (i, 0)) for d in douts],
                     out_specs=[pl.BlockSpec((t, a.shape[1]), lambda i: (i, 0)) for a in rows]
                     + [pl.BlockSpec(p.shape, lambda i: (0, 0)) for p in params],
                     out_shape=[jax.ShapeDtypeStruct(a.shape, a.dtype) for a in (*rows, *params)],
                     compiler_params=_params(("arbitrary",)))(*rows, *aux, *params, *consts, *douts)
        return tuple(res[:nr]), tuple(res[nr:])

    @jax.custom_vjp
    def op(rows, aux, params, consts):
        return tuple(fwd_call(rows, aux, params, consts))

    def fwd(rows, aux, params, consts):
        return op(rows, aux, params, consts), (rows, aux, params, consts)

    def bwd(res, douts):
        rows, aux, params, consts = res
        drows, dparams = bwd_call(rows, aux, params, consts, tuple(douts))
        zeros = lambda xs: tuple(jnp.zeros_like(a) for a in xs)
        return drows, zeros(aux), dparams, zeros(consts)

    op.defvjp(fwd, bwd)
    return op


@jax.custom_vjp
def _swap_halves(x):
    return pltpu.roll(x, LANES // 2, 1)


_swap_halves.defvjp(lambda x: (_swap_halves(x), None), lambda _, g: (_swap_halves(g),))


def _rope(x, cos_t, sin_t):
    return x * cos_t + _swap_halves(x) * sin_t


def _rms(x, g, n=None):
    n = x.shape[-1] if n is None else n
    return x * lax.rsqrt(jnp.sum(x * x, axis=-1, keepdims=True) * (1.0 / n) + RMS_EPS) * g


def _heads(x):
    return [x[:, i * LANES:(i + 1) * LANES] for i in range(x.shape[1] // LANES)]


def _cat(xs):
    return jnp.concatenate(xs, axis=1)


def _silu(x):
    return x * jax.nn.sigmoid(x)


def _f_norm(x, g):
    return (_rms(x, g).astype(BF16),)


def _f_mla_a(q_lat, c_kv, kpe, cos_p, sin_p, qa_g, kva_g, kpe_g):
    q_lat, c_kv, kpe = (t.astype(F32) for t in (q_lat, c_kv, kpe))
    kp =_rope(_rms(kpe, kpe_g, MLA_ROPE), cos_p, sin_p)
    return _rms(q_lat, qa_g).astype(BF16), _rms(c_kv, kva_g).astype(BF16), _cat([kp] * MLA_HEADS)


def _f_mla_b(q8, kn_raw, cos_p, sin_p, qn_g, qp_g, kn_g):
    hs = _heads(q8)
    qn = _cat([_rms(h, qn_g) for h in hs[:MLA_HEADS]])
    qp = _cat([_rope(_rms(h, qp_g, MLA_ROPE), cos_p, sin_p) for h in hs[MLA_HEADS:]])
    kn = _cat([_rms(h, kn_g) for h in _heads(kn_raw)])
    return qn, qp, kn


def _softplus(x):
    return jnp.maximum(x, 0.0) + jnp.log(1.0 + jnp.exp(-jnp.abs(x)))


def _l2n(x):
    return x * lax.rsqrt(jnp.sum(x * x, axis=-1, keepdims=True) + 1e-6)


def _f_dn_pre(mixed, ab, alog_f, dtb_f, e_a, e_b):
    hs = _heads(mixed)
    q = _cat([_l2n(h) * (LANES ** -0.5) for h in hs[:DN_HEADS]])
    k = _cat([_l2n(h) for h in hs[DN_HEADS:2 * DN_HEADS]])
    v = _cat(hs[2 * DN_HEADS:])
    ab = ab.astype(F32)
    a_f = jnp.dot(ab, e_a, precision=HI, preferred_element_type=F32)
    b_f = jnp.dot(ab, e_b, precision=HI, preferred_element_type=F32)
    g = -jnp.exp(alog_f) * _softplus(a_f + dtb_f)
    return q, k, v, g, jax.nn.sigmoid(b_f)


def _f_dil_pre(qkv, cos_h, sin_h, q_g, k_g):
    hs = [h.astype(F32) for h in _heads(qkv)]
    q = [_rope(_rms(h, q_g), cos_h, sin_h) for h in hs[:DIL_HEADS]]
    k = [_rope(_rms(h, k_g), cos_h, sin_h) for h in hs[DIL_HEADS:2 * DIL_HEADS]]
    v = hs[2 * DIL_HEADS:]
    group = lambda xs, g: _cat(xs[g * DIL_GROUP_HEADS:(g + 1) * DIL_GROUP_HEADS])
    return tuple(group(xs, g) for xs in (q, k, v) for g in range(len(DIL_DILATIONS)))


def _f_merge_a(y_a, z_a, o_dn, z_b, o0, o1, o2, l0, l1, l2, z_c, out_g):
    z_a, z_b, z_c = (z.astype(F32) for z in (z_a, z_b, z_c))
    y_b = _cat([_rms(h, out_g) for h in _heads(o_dn)])
    os_, ls = [_heads(o) for o in (o0, o1, o2)], [_heads(l) for l in (l0, l1, l2)]
    y_c = []
    for j in range(DIL_GROUP_HEADS):
        l3 = [ls[g][j] for g in range(3)]
        m = jnp.maximum(jnp.maximum(l3[0], l3[1]), l3[2])
        e3 = [jnp.exp(l - m) for l in l3]
        den = e3[0] + e3[1] + e3[2]
        y_c.append(sum(e3[g] * os_[g][j] for g in range(3)) / den)
    return tuple(t.astype(BF16) for t in (y_a * _silu(z_a), y_b * _silu(z_b), _cat(y_c) * _silu(z_c)))


def _f_merge_b(b0, b1, b2, gl):
    gs = [jax.nn.sigmoid(gl[:, i * D_MODEL:(i + 1) * D_MODEL].astype(F32)) for i in range(3)]
    return ((gs[0] * b0 + gs[1] * b1 + gs[2] * b2).astype(BF16),)


def _rope_tables(pos, inv_sign):
    S = pos.shape[0]
    t = min(S, 1024)

    def body(p_ref, c_ref, cp, sp, ch, sh):
        p = p_ref[...].astype(F32)
        c = c_ref[...]
        ang_p, ang_h = p * c[0:1], p * c[2:3]
        cp[...] = jnp.cos(ang_p) * jnp.abs(c[1:2])
        sp[...] = jnp.sin(ang_p) * c[1:2]
        ch[...] = jnp.cos(ang_h)
        sh[...] = jnp.sin(ang_h) * c[3:4]

    row = pl.BlockSpec((t, LANES), lambda i: (i, 0))
    return _pcall(body, name="rope_tables", grid=(S // t,),
                  in_specs=[pl.BlockSpec((t, 1), lambda i: (i, 0)), pl.BlockSpec((4, LANES), lambda i: (0, 0))],
                  out_specs=[row] * 4, out_shape=[jax.ShapeDtypeStruct((S, LANES), F32)] * 4,
                  compiler_params=_params(("parallel",)))(pos, inv_sign)


def _rope_consts():
    half_p, half_h = MLA_ROPE // 2, LANES // 2
    inv_p = 1.0 / (ROPE_THETA ** (jnp.arange(0, MLA_ROPE, 2, dtype=F32) / MLA_ROPE))
    inv_h = 1.0 / (ROPE_THETA ** (jnp.arange(0, LANES, 2, dtype=F32) / LANES))
    z = jnp.zeros((half_p,), F32)
    o = jnp.ones((half_p,), F32)
    return jnp.stack([jnp.concatenate([inv_p, z, inv_p, z]), jnp.concatenate([-o, z, o, z]),
                      jnp.concatenate([inv_h, inv_h]), jnp.concatenate([-jnp.ones((half_h,), F32), jnp.ones((half_h,), F32)])])


def _shift_rows(x, s, up):
    n = x.shape[0]
    r = lax.broadcasted_iota(jnp.int32, x.shape, 0)
    if up:
        return jnp.where(r < n - s, pltpu.roll(x, n - s, 0), 0.0)
    return jnp.where(r >= s, pltpu.roll(x, s, 0), 0.0)


def _make_shift(s):
    @jax.custom_vjp
    def sh(x):
        return _shift_rows(x, s, False)

    sh.defvjp(lambda x: (sh(x), None), lambda _, g: (_shift_rows(g, s, True),))
    return sh


def _f_conv(x, w):
    x = x.astype(F32)
    y = x * w[DN_CONV - 1:DN_CONV]
    for j in range(DN_CONV - 1):
        y = y + _make_shift(DN_CONV - 1 - j)(x) * w[j:j + 1]
    return _silu(y)


def _make_conv(name):
    def call(x, w, dy=None):
        S, C = x.shape
        col = pl.BlockSpec((S, LANES), lambda i: (0, i))
        wsp = pl.BlockSpec((DN_CONV, LANES), lambda i: (0, i))
        if dy is None:
            def body(x_ref, w_ref, o_ref):
                o_ref[...] = _f_conv(x_ref[...], w_ref[...])
            return _pcall(body, name=f"{name}_fwd", grid=(C // LANES,), in_specs=[col, wsp], out_specs=col,
                          out_shape=jax.ShapeDtypeStruct(x.shape, F32), compiler_params=_params(("parallel",)))(x, w)

        def body(x_ref, w_ref, dy_ref, dx_ref, dw_ref):
            _, vjp = jax.vjp(_f_conv, x_ref[...], w_ref[...])
            dx_ref[...], dw_ref[...] = vjp(dy_ref[...])
        return _pcall(body, name=f"{name}_bwd", grid=(C // LANES,), in_specs=[col, wsp, col], out_specs=[col, wsp],
                      out_shape=[jax.ShapeDtypeStruct(x.shape, x.dtype), jax.ShapeDtypeStruct(w.shape, F32)],
                      compiler_params=_params(("parallel",)))(x, w, dy)

    @jax.custom_vjp
    def op(x, w):
        return call(x, w)

    op.defvjp(lambda x, w: (op(x, w), (x, w)), lambda res, dy: tuple(call(*res, dy)))
    return op


def _dot_nt(a, b):
    return lax.dot_general(a.astype(BF16), b.astype(BF16), (((1,), (1,)), ((), ())), preferred_element_type=F32)


def _dot_nn(a, b):
    return jnp.dot(a.astype(BF16), b.astype(BF16), preferred_element_type=F32)


def _dot_tn(a, b):
    return lax.dot_general(a.astype(BF16), b.astype(BF16), (((0,), (0,)), ((), ())), preferred_element_type=F32)


def _mla_scores(qn_r, qp_r, kn_r, kp_r, diagonal):
    scale = MLA_QK ** -0.5
    s = _dot_nt(qn_r[...] * scale, kn_r[...]) + _dot_nt(qp_r[...] * scale, kp_r[...])
    if diagonal:
        r = lax.broadcasted_iota(jnp.int32, s.shape, 0)
        c = lax.broadcasted_iota(jnp.int32, s.shape, 1)
        s = jnp.where(c <= r, s, NEG)
    return s


def _on_causal_pairs(qi, ki, step):
    @pl.when(ki < qi)
    def _():
        step(False)

    @pl.when(ki == qi)
    def _():
        step(True)


def _causal_pairs(n, t, by_key):
    pairs = [(q, k) for k in range(n) for q in range(k, n)] if by_key else [(q, k) for q in range(n) for k in range(q + 1)]
    qt, kt = (jnp.array([p[i] for p in pairs], jnp.int32) for i in (0, 1))
    return (qt, kt, pl.BlockSpec((t, LANES), lambda h, p, qt_r, kt_r: (qt_r[p], h)),
            pl.BlockSpec((t, LANES), lambda h, p, qt_r, kt_r: (kt_r[p], h)))


def _make_mla_attn(name):
    scale = MLA_QK ** -0.5

    def fwd_call(qn, qp, kn, kp, v):
        S = qn.shape[0]
        t = min(S, 512)
        n = S // t
        qt, kt, qs, ks = _causal_pairs(n, t, by_key=False)

        def body(qt_r, kt_r, qn_r, qp_r, kn_r, kp_r, v_r, o_r, lse_r, m_s, l_s, acc_s):
            qi, ki = qt_r[pl.program_id(1)], kt_r[pl.program_id(1)]

            @pl.when(ki == 0)
            def _():
                m_s[...] = jnp.full_like(m_s, NEG)
                l_s[...] = jnp.zeros_like(l_s)
                acc_s[...] = jnp.zeros_like(acc_s)

            def step(diagonal):
                s = _mla_scores(qn_r, qp_r, kn_r, kp_r, diagonal)
                m_old = m_s[...]
                m_new = jnp.maximum(m_old, jnp.max(s, axis=-1, keepdims=True))
                p = jnp.exp(s - m_new[:, :1])
                alpha = jnp.exp(m_old - m_new)
                l_s[...] = alpha * l_s[...] + jnp.sum(p, axis=-1, keepdims=True)
                acc_s[...] = alpha * acc_s[...] + _dot_nn(p, v_r[...])
                m_s[...] = m_new
            _on_causal_pairs(qi, ki, step)

            @pl.when(ki == qi)
            def _():
                o_r[...] = acc_s[...] / l_s[...]
                lse_r[...] = m_s[...] + jnp.log(l_s[...])

        spec = pltpu.PrefetchScalarGridSpec(num_scalar_prefetch=2, grid=(MLA_HEADS, qt.shape[0]), in_specs=[qs, qs, ks, ks, ks],
                                            out_specs=[qs, qs], scratch_shapes=[pltpu.VMEM((t, LANES), F32)] * 3)
        return _pcall(body, name=f"{name}_fwd", grid_spec=spec, out_shape=[jax.ShapeDtypeStruct((S, MLA_HEADS * LANES), F32)] * 2,
                      compiler_params=_params(("parallel", "arbitrary")))(qt, kt, qn, qp, kn, kp, v)

    def bwd_call(qn, qp, kn, kp, v, o, lse, do):
        S = qn.shape[0]
        t = min(S, 512)
        n = S // t
        qt, kt, qs, ks = _causal_pairs(n, t, by_key=True)
        head = pl.BlockSpec((S, LANES), lambda h, p, qt_r, kt_r: (0, h))
        n_pairs = qt.shape[0]

        def body(qt_r, kt_r, qn_r, qp_r, kn_r, kp_r, v_r, o_r, lse_r, do_r, dqn_r, dqp_r, dkn_r, dkp_r, dv_r, dkn_s, dkp_s, dv_s, dl_s):
            pair = pl.program_id(1)
            qi, ki = qt_r[pair], kt_r[pair]
            rows = pl.ds(pl.multiple_of(qi * t, t), t)

            @pl.when(pair == 0)
            def _():
                dqn_r[...] = jnp.zeros_like(dqn_r)
                dqp_r[...] = jnp.zeros_like(dqp_r)

            @pl.when(ki == 0)
            def _():
                dl_s[rows, :] = jnp.broadcast_to(jnp.sum(do_r[...] * o_r[...], axis=-1, keepdims=True), (t, LANES))

            @pl.when(qi == ki)
            def _():
                dkn_s[...] = jnp.zeros_like(dkn_s)
                dkp_s[...] = jnp.zeros_like(dkp_s)
                dv_s[...] = jnp.zeros_like(dv_s)

            def step(diagonal):
                p = jnp.exp(_mla_scores(qn_r, qp_r, kn_r, kp_r, diagonal) - lse_r[...][:, :1])
                ds = p * (_dot_nt(do_r[...], v_r[...]) - dl_s[rows, :][:, :1])
                dv_s[...] += _dot_tn(p, do_r[...])
                dkn_s[...] += _dot_tn(ds, qn_r[...] * scale)
                dkp_s[...] += _dot_tn(ds, qp_r[...] * scale)
                dqn_r[rows, :] += _dot_nn(ds, kn_r[...])
                dqp_r[rows, :] += _dot_nn(ds, kp_r[...])
            _on_causal_pairs(qi, ki, step)

            @pl.when(qi == n - 1)
            def _():
                dkn_r[...] = dkn_s[...]
                dkp_r[...] = dkp_s[...]
                dv_r[...] = dv_s[...]

            @pl.when(pair == n_pairs - 1)
            def _():
                dqn_r[...] = dqn_r[...] * scale
                dqp_r[...] = dqp_r[...] * scale

        spec = pltpu.PrefetchScalarGridSpec(num_scalar_prefetch=2, grid=(MLA_HEADS, n_pairs), in_specs=[qs, qs, ks, ks, ks, qs, qs, qs],
                                            out_specs=[head, head, ks, ks, ks],
                                            scratch_shapes=[pltpu.VMEM((t, LANES), F32)] * 3 + [pltpu.VMEM((S, LANES), F32)])
        return _pcall(body, name=f"{name}_bwd", grid_spec=spec, out_shape=[jax.ShapeDtypeStruct((S, MLA_HEADS * LANES), F32)] * 5,
                      compiler_params=_params(("parallel", "arbitrary")))(qt, kt, qn, qp, kn, kp, v, o, lse, do)

    @jax.custom_vjp
    def op(qn, qp, kn, kp, v):
        return fwd_call(qn, qp, kn, kp, v)[0]

    def fwd(qn, qp, kn, kp, v):
        o, lse = fwd_call(qn, qp, kn, kp, v)
        return o, (qn, qp, kn, kp, v, o, lse)

    def bwd(res, do):
        return tuple(bwd_call(*res, do))

    op.defvjp(fwd, bwd)
    return op


def _dil_block(q, kp, kc, vp, vc, has_prev):
    scale = LANES ** -0.5
    r = lax.broadcasted_iota(jnp.int32, (DIL_BLOCK, 2 * DIL_BLOCK), 0)
    c = lax.broadcasted_iota(jnp.int32, (DIL_BLOCK, 2 * DIL_BLOCK), 1)
    valid = ((c < DIL_BLOCK) & (c >= r) & has_prev) | ((c >= DIL_BLOCK) & (c - DIL_BLOCK <= r))
    s = jnp.where(valid, _dot_nt(q * scale, jnp.concatenate([kp, kc], axis=0)), NEG)
    m = jnp.max(s, axis=-1, keepdims=True)
    e = jnp.exp(s - m)
    den = jnp.sum(e, axis=-1, keepdims=True)
    o = _dot_nn(e, jnp.concatenate([vp, vc], axis=0)) / den
    return o, jnp.broadcast_to(m + jnp.log(den), o.shape)


DIL_TILE_ROWS = (1024, 1024, 2048)


def _make_dil_attn(name, d, tile_rows):
    def call(q, k, v, cts=None):
        S = q.shape[0]
        span = DIL_BLOCK * d
        G = max(1, min(tile_rows, S) // span)
        n = S // (G * span)
        at = (lambda i: i) if cts is None else (lambda i: n - 1 - i)
        tile = pl.BlockSpec((G * span, LANES), lambda h, i: (at(i), h))
        before = pl.BlockSpec((span, LANES), lambda h, i: (jnp.maximum(at(i) * G - 1, 0), h))

        def rows(r, j):
            return pl.ds(j * DIL_BLOCK, DIL_BLOCK) if d == 1 else pl.ds(r + j * span, DIL_BLOCK, stride=d)

        def over_residues(fn):
            if d == 1:
                fn(0)
            else:
                lax.fori_loop(0, d, lambda r, c: (fn(r), c)[1], 0)

        def block_inputs(r, j, q_r, kb_r, k_r, vb_r, v_r):
            kp = kb_r[rows(r, 0), :] if j == 0 else k_r[rows(r, j - 1), :]
            vp = vb_r[rows(r, 0), :] if j == 0 else v_r[rows(r, j - 1), :]
            return q_r[rows(r, j), :], kp, k_r[rows(r, j), :], vp, v_r[rows(r, j), :]

        if cts is None:
            def body(q_r, kb_r, k_r, vb_r, v_r, o_r, lse_r):
                first = at(pl.program_id(1)) * G

                def residue(r):
                    for j in range(G):
                        o_r[rows(r, j), :], lse_r[rows(r, j), :] = _dil_block(
                            *block_inputs(r, j, q_r, kb_r, k_r, vb_r, v_r), first + j > 0)
                over_residues(residue)
            return _pcall(body, name=f"{name}_fwd", grid=(DIL_GROUP_HEADS, n), in_specs=[tile, before, tile, before, tile],
                          out_specs=[tile, tile], out_shape=[jax.ShapeDtypeStruct(q.shape, F32)] * 2,
                          compiler_params=_params(("parallel", "parallel")))(q, k, k, v, v)

        def body(q_r, kb_r, k_r, vb_r, v_r, do_r, dl_r, dq_r, dk_r, dv_r, ck_s, cv_s):
            first = at(pl.program_id(1)) * G

            @pl.when(pl.program_id(1) == 0)
            def _():
                ck_s[...] = jnp.zeros_like(ck_s)
                cv_s[...] = jnp.zeros_like(cv_s)

            def residue(r):
                owed = None
                for j in range(G):
                    hp = first + j > 0
                    _, vjp = jax.vjp(lambda *a: _dil_block(*a, hp), *block_inputs(r, j, q_r, kb_r, k_r, vb_r, v_r))
                    dq, dkp, dkc, dvp, dvc = vjp((do_r[rows(r, j), :], dl_r[rows(r, j), :]))
                    dq_r[rows(r, j), :] = dq
                    if j == G - 1:
                        dkc, dvc = dkc + ck_s[rows(r, 0), :], dvc + cv_s[rows(r, 0), :]
                    dk_r[rows(r, j), :], dv_r[rows(r, j), :] = dkc, dvc
                    if j == 0:
                        owed = (dkp, dvp)
                    else:
                        dk_r[rows(r, j - 1), :] += dkp
                        dv_r[rows(r, j - 1), :] += dvp
                ck_s[rows(r, 0), :], cv_s[rows(r, 0), :] = owed
            over_residues(residue)
        return _pcall(body, name=f"{name}_bwd", grid=(DIL_GROUP_HEADS, n), in_specs=[tile, before, tile, before, tile, tile, tile],
                      out_specs=[tile] * 3, out_shape=[jax.ShapeDtypeStruct(q.shape, F32)] * 3,
                      scratch_shapes=[pltpu.VMEM((span, LANES), F32)] * 2,
                      compiler_params=_params(("parallel", "arbitrary")))(q, k, k, v, v, *cts)

    @jax.custom_vjp
    def op(q, k, v):
        return tuple(call(q, k, v))

    op.defvjp(lambda q, k, v: (op(q, k, v), (q, k, v)), lambda res, cts: tuple(call(*res, cts=cts)))
    return op


def _pdot(a, b, dims):
    return lax.dot_general(a, b, (dims, ((), ())), precision=lax.Precision.HIGH, preferred_element_type=F32)


DN_LOCAL_CHUNKS = 4


DN_BLOCK_HEADS = 4
DN_BLOCK = DN_BLOCK_HEADS * DN_CHUNK


def _inverse_cotangent(inv, d):
    return -_pdot(inv, _pdot(d, inv, ((1,), (1,))), ((0,), (0,)))


@jax.custom_vjp
def _unit_lower_inverse(a):
    n = a.shape[0]
    eye = (lax.broadcasted_iota(jnp.int32, (n, n), 0) == lax.broadcasted_iota(jnp.int32, (n, n), 1)).astype(F32)
    inv, pw = eye - a, a
    for _ in range(5):
        pw = _pdot(pw, pw, ((1,), (0,)))
        inv = inv + _pdot(inv, pw, ((1,), (0,)))
    return inv


def _unit_lower_inverse_fwd(a):
    inv = _unit_lower_inverse(a)
    return inv, inv


_unit_lower_inverse.defvjp(_unit_lower_inverse_fwd, lambda inv, d: (_inverse_cotangent(inv, d),))


@jax.custom_vjp
def _known_inverse(a, inv):
    return inv


_known_inverse.defvjp(lambda a, inv: (inv, inv), lambda inv, d: (_inverse_cotangent(inv, d), jnp.zeros_like(inv)))


def _dn_local(q, k, v, g, b, known=None):
    C, R = DN_CHUNK, DN_BLOCK
    r = lax.broadcasted_iota(jnp.int32, (R, R), 0)
    c = lax.broadcasted_iota(jnp.int32, (R, R), 1)
    same_head = (r // C) == (c // C)
    incl, strict = same_head & (r >= c), same_head & (r > c)
    avg = jnp.full((R, LANES), 1.0 / LANES, F32)
    rc = lax.broadcasted_iota(jnp.int32, (C, C), 0) >= lax.broadcasted_iota(jnp.int32, (C, C), 1)
    gc_lanes = _pdot(rc.astype(F32), g, ((1,), (0,)))
    us, ws, qes, kds, qks, invs = [], [], [], [], [], []
    for first in range(0, DN_HEADS, DN_BLOCK_HEADS):
        stack = lambda x: jnp.concatenate(_heads(x)[first:first + DN_BLOCK_HEADS], axis=0)
        unstack = lambda x: [x[p * C:(p + 1) * C] for p in range(DN_BLOCK_HEADS)]
        gc, q_s, k_s, v_s, b_s = (stack(x) for x in (gc_lanes, q, k, v, b))
        gc_j = _pdot(avg, gc, ((1,), (1,)))
        decay = jnp.exp(jnp.where(incl, _cat([gc] * (R // LANES)) - gc_j, NEG))
        kb = k_s * b_s
        kk = _pdot(jnp.concatenate([kb, q_s], axis=0), k_s, ((1,), (1,)))
        a = jnp.where(strict, kk[:R] * decay, 0.0)
        inv = _unit_lower_inverse(a) if known is None else _known_inverse(a, known[len(invs) * R:(len(invs) + 1) * R])
        invs.append(inv)
        eg = jnp.exp(gc)
        uw = _pdot(inv, _cat([v_s * b_s, kb * eg]), ((1,), (0,)))
        g_last = jnp.concatenate([jnp.broadcast_to(x[C - 1:C], (C, LANES)) for x in unstack(gc)], axis=0)
        us += unstack(uw[:, :LANES])
        ws += unstack(uw[:, LANES:])
        qes += unstack(q_s * eg)
        kds += unstack(k_s * jnp.exp(g_last - gc))
        qks.append(kk[R:] * decay)
    egl = jnp.broadcast_to(jnp.exp(gc_lanes[C - 1:C]), (8, DN_HEADS * LANES))
    return _cat(us), _cat(ws), _cat(qes), _cat(kds), jnp.concatenate(qks, axis=0), egl, jnp.concatenate(invs, axis=0)


def _dn_scan(u, w, qe, kd, qk, egl, state):
    C = DN_CHUNK
    heads = [slice(h * LANES, (h + 1) * LANES) for h in range(DN_HEADS)]
    ws = [_pdot(jnp.concatenate([w[:, sl], qe[:, sl]], axis=0), state[sl, :], ((1,), (0,))) for sl in heads]
    v_new = [u[:, sl] - x[:C] for sl, x in zip(heads, ws)]
    local = []
    for i, first in enumerate(range(0, DN_HEADS, DN_BLOCK_HEADS)):
        y = _pdot(qk[i * DN_BLOCK:(i + 1) * DN_BLOCK], jnp.concatenate(v_new[first:first + DN_BLOCK_HEADS], axis=0), ((1,), (0,)))
        local += [y[p * C:(p + 1) * C] for p in range(DN_BLOCK_HEADS)]
    o = _cat([x[C:] + y for x, y in zip(ws, local)])
    states = [state[sl, :] * egl[0:1, sl] + _pdot(kd[:, sl], vn, ((0,), (0,))) for sl, vn in zip(heads, v_new)]
    return o, jnp.concatenate(states, axis=0)


def _make_delta_rule(name):
    W = DN_HEADS * LANES
    QK = DN_HEADS * DN_CHUNK

    def local_call(ins, cts=None):
        S = ins[0].shape[0]
        n = S // DN_CHUNK
        per = math.gcd(DN_LOCAL_CHUNKS, n)
        row = pl.BlockSpec((per * DN_CHUNK, W), lambda i: (i, 0))
        qkb = pl.BlockSpec((per * QK, DN_BLOCK), lambda i: (i, 0))
        eg = pl.BlockSpec((per, 8, W), lambda i: (i, 0, 0))
        rows = lambda j: slice(j * DN_CHUNK, (j + 1) * DN_CHUNK)
        qk_rows = lambda j: slice(j * QK, (j + 1) * QK)
        out_rows = [rows, rows, rows, rows, qk_rows]

        if cts is None:
            def body(*refs):
                for j in range(per):
                    res = _dn_local(*[r[rows(j), :] for r in refs[:5]])
                    for o_r, o, at_ in zip(refs[5:10], res[:5], out_rows):
                        o_r[at_(j), :] = o
                    refs[10][j] = res[5]
                    refs[11][qk_rows(j), :] = res[6]
            blockdiag = jax.ShapeDtypeStruct((n * QK, DN_BLOCK), F32)
            return _pcall(body, name=f"{name}_local_fwd", grid=(n // per,), in_specs=[row] * 5, out_specs=[row] * 4 + [qkb, eg, qkb],
                          out_shape=[jax.ShapeDtypeStruct((S, W), F32)] * 4 + [blockdiag, jax.ShapeDtypeStruct((n, 8, W), F32), blockdiag],
                          compiler_params=_params(("parallel",)))(*ins)

        def body(*refs):
            for j in range(per):
                known = refs[5][qk_rows(j), :]
                _, vjp = jax.vjp(lambda *a: _dn_local(*a, known=known)[:6], *[r[rows(j), :] for r in refs[:5]])
                grads = vjp(tuple(r[at_(j), :] for r, at_ in zip(refs[6:11], out_rows)) + (refs[11][j],))
                for o_r, o in zip(refs[12:], grads):
                    o_r[rows(j), :] = o
        return _pcall(body, name=f"{name}_local_bwd", grid=(n // per,), in_specs=[row] * 5 + [qkb] + [row] * 4 + [qkb, eg],
                      out_specs=[row] * 5, out_shape=[jax.ShapeDtypeStruct((S, W), F32)] * 5,
                      compiler_params=_params(("parallel",)))(*ins, *cts)

    def scan_call(ins, saved=None, do=None):
        S = ins[0].shape[0]
        n = S // DN_CHUNK
        at = (lambda i: i) if do is None else (lambda i: n - 1 - i)
        row = pl.BlockSpec((DN_CHUNK, W), lambda i: (at(i), 0))
        qkb = pl.BlockSpec((QK, DN_BLOCK), lambda i: (at(i), 0))
        eg = pl.BlockSpec((None, 8, W), lambda i: (at(i), 0, 0))
        st = pl.BlockSpec((None, W, LANES), lambda i: (at(i), 0, 0))

        if do is None:
            def body(*refs):
                o_r, st_r, s_s = refs[6:]

                @pl.when(pl.program_id(0) == 0)
                def _():
                    s_s[...] = jnp.zeros_like(s_s)
                st_r[...] = s_s[...]
                o_r[...], s_s[...] = _dn_scan(*[r[...] for r in refs[:6]], s_s[...])
            return _pcall(body, name=f"{name}_scan_fwd", grid=(n,), in_specs=[row] * 4 + [qkb, eg], out_specs=[row, st],
                          out_shape=[jax.ShapeDtypeStruct((S, W), F32), jax.ShapeDtypeStruct((n, W, LANES), F32)],
                          scratch_shapes=[pltpu.VMEM((W, LANES), F32)], compiler_params=_params(("arbitrary",)))(*ins)

        def body(*refs):
            st_r, do_r = refs[6:8]
            outs, ds_s = refs[8:14], refs[14]

            @pl.when(pl.program_id(0) == 0)
            def _():
                ds_s[...] = jnp.zeros_like(ds_s)
            _, vjp = jax.vjp(_dn_scan, *[r[...] for r in refs[:6]], st_r[...])
            *grads, ds = vjp((do_r[...], ds_s[...]))
            for o_r, gval in zip(outs, grads):
                o_r[...] = gval
            ds_s[...] = ds
        return _pcall(body, name=f"{name}_scan_bwd", grid=(n,), in_specs=[row] * 4 + [qkb, eg, st, row], out_specs=[row] * 4 + [qkb, eg],
                      out_shape=[jax.ShapeDtypeStruct((S, W), F32)] * 4
                      + [jax.ShapeDtypeStruct((n * QK, DN_BLOCK), F32), jax.ShapeDtypeStruct((n, 8, W), F32)],
                      scratch_shapes=[pltpu.VMEM((W, LANES), F32)], compiler_params=_params(("arbitrary",)))(*ins, saved, do)

    @jax.custom_vjp
    def local(q, k, v, g, b):
        return tuple(local_call((q, k, v, g, b))[:6])

    def local_fwd(*a):
        *outs, inverses = local_call(a)
        return tuple(outs), (*a, inverses)

    local.defvjp(local_fwd, lambda res, cts: tuple(local_call(res, tuple(cts))))

    @jax.custom_vjp
    def scan(u, w, qe, kd, qk, egl):
        return scan_call((u, w, qe, kd, qk, egl))[0]

    def scan_fwd(*a):
        o, states = scan_call(a)
        return o, (a, states)

    scan.defvjp(scan_fwd, lambda res, do: tuple(scan_call(res[0], res[1], do)))
    return lambda q, k, v, g, b: scan(*local(q, k, v, g, b))


def _loss_call(y, target):
    S, D = y.shape
    t = min(S, 512)
    n = S // t
    row = pl.BlockSpec((t, D), lambda i: (i, 0))

    def body(y_r, t_r, loss_r, dy_r, acc_s):
        i = pl.program_id(0)

        @pl.when(i == 0)
        def _():
            acc_s[...] = jnp.zeros_like(acc_s)
        err = y_r[...] - t_r[...]
        dy_r[...] = err * (1.0 / D)
        acc_s[...] += jnp.sum(err * err, axis=0, keepdims=True)

        @pl.when(i == n - 1)
        def _():
            loss_r[...] = jnp.broadcast_to(jnp.sum(acc_s[...], axis=1, keepdims=True) * (0.5 / D), loss_r.shape)

    return _pcall(body, name="loss_head", grid=(n,), in_specs=[row, row],
                  out_specs=[pl.BlockSpec((8, LANES), lambda i: (0, 0)), row],
                  out_shape=[jax.ShapeDtypeStruct((8, LANES), F32), jax.ShapeDtypeStruct((S, D), F32)],
                  scratch_shapes=[pltpu.VMEM((1, D), F32)], compiler_params=_params(("arbitrary",)))(y, target)


def _adamw_layer(name, part, w, m, v, layer, earlier=None, rows=128):
    L, R, C = w.shape
    t = _tile(R, rows, 8)
    row = pl.BlockSpec((None, t, C), lambda i: (layer, i, 0))

    def body(p_r, w_r, m_r, v_r, *rest):
        g_r, d_r, nm_r, nv_r = rest[-4:]
        g = p_r[0].astype(F32)
        for s in range(1, N_DEV):
            g = g + p_r[s].astype(F32)
        m_new = ADAM_B1 * m_r[...] + (1.0 - ADAM_B1) * g
        v_new = ADAM_B2 * v_r[...] + (1.0 - ADAM_B2) * (g * g)
        m_hat = m_new / (1.0 - ADAM_B1 ** ADAM_STEP)
        v_hat = v_new / (1.0 - ADAM_B2 ** ADAM_STEP)
        g_r[...] = g
        d_r[...] = -ADAM_LR * (m_hat / (jnp.sqrt(v_hat) + ADAM_EPS) + ADAM_WD * w_r[...])
        nm_r[...] = m_new
        nv_r[...] = v_new

    extra = [] if earlier is None else list(earlier)
    return _pcall(body, name=name, grid=(R // t,),
                  in_specs=[pl.BlockSpec((N_DEV, t, C), lambda i: (0, i, 0)), row, row, row] + [pl.BlockSpec(memory_space=pl.ANY)] * len(extra),
                  out_specs=[row] * 4, out_shape=[jax.ShapeDtypeStruct((L, R, C), F32)] * 4,
                  input_output_aliases={4 + k: k for k in range(len(extra))},
                  compiler_params=_params(("parallel",)))(part, w, m, v, *extra)


def _my_place():
    x, y, c = lax.axis_index("x"), lax.axis_index("y"), lax.axis_index("c")
    return x, y, c


def _index(x, y, c):
    return 4 * x + 2 * y + c


def _all_gather(vs):
    n = len(vs)

    def body(*refs):
        v_refs, out_refs = refs[:n], refs[n:2 * n]
        send_sems, recv_sems, local_sems = refs[2 * n:]
        x, y, c = _my_place()
        me, sibling = (x, y, c), (x, y, 1 - c)
        chips = [(1 - x, y), (x, 1 - y), (1 - x, 1 - y)]

        def copy(a, k, block, to, src=None):
            rows = out_refs[a].at[_index(*block)]
            return pltpu.make_async_remote_copy(src_ref=rows if src is None else src, dst_ref=rows, send_sem=send_sems.at[a, k],
                                                recv_sem=recv_sems.at[a, k], device_id=to, device_id_type=MESH)

        mine = [pltpu.make_async_copy(v_refs[a], out_refs[a].at[_index(*me)], local_sems.at[a]) for a in range(n)]
        first, passed = [], []
        for a in range(n):
            mine[a].start()
            first += [copy(a, 0, me, sibling, src=v_refs[a])]
            first += [copy(a, 1 + j, me, (*chip, c), src=v_refs[a]) for j, chip in enumerate(chips)]
        for cp in first:
            cp.start()
        for j, chip in enumerate(chips):
            for a in range(n):
                copy(a, 1 + j, (*chip, c), me).wait_recv()
                passed.append(copy(a, 4 + j, (*chip, c), sibling))
                passed[-1].start()
        for a in range(n):
            copy(a, 0, sibling, me).wait_recv()
            for j, chip in enumerate(chips):
                copy(a, 4 + j, (*chip, 1 - c), me).wait_recv()
        for cp in first + passed:
            cp.wait_send()
        for a in range(n):
            mine[a].wait()

    any_ = pl.BlockSpec(memory_space=pl.ANY)
    return _pcall(body, name="gather_weights", in_specs=[any_] * n, out_specs=[any_] * n,
                  out_shape=[jax.ShapeDtypeStruct((N_DEV,) + v.shape, v.dtype) for v in vs],
                  scratch_shapes=[pltpu.SemaphoreType.DMA((n, 7)), pltpu.SemaphoreType.DMA((n, 7)), pltpu.SemaphoreType.DMA((n,))])(*vs)


def _all_to_all(vs, after):
    n = len(vs)

    def body(*refs):
        v_refs, out_refs = refs[:n], refs[n + 1:2 * n + 1]
        send_sems, recv_sems, local_sems = refs[2 * n + 1:]
        x, y, c = _my_place()
        me = _index(x, y, c)
        mine = [pltpu.make_async_copy(v_refs[a].at[me], out_refs[a].at[me], local_sems.at[a]) for a in range(n)]
        copies = []
        for a in range(n):
            mine[a].start()
        for k in range(1, N_DEV):
            px = 1 - x if k & 4 else x
            py = 1 - y if k & 2 else y
            pc = 1 - c if k & 1 else c
            for a in range(n):
                cp = pltpu.make_async_remote_copy(src_ref=v_refs[a].at[_index(px, py, pc)], dst_ref=out_refs[a].at[me],
                                                  send_sem=send_sems.at[a, k - 1], recv_sem=recv_sems.at[a, k - 1],
                                                  device_id=(px, py, pc), device_id_type=MESH)
                cp.start()
                copies.append(cp)
        for cp in copies:
            cp.wait()
        for a in range(n):
            mine[a].wait()

    any_ = pl.BlockSpec(memory_space=pl.ANY)
    return _pcall(body, name="exchange_vectors", in_specs=[any_] * (n + 1), out_specs=[any_] * n,
                  out_shape=[jax.ShapeDtypeStruct(v.shape, v.dtype) for v in vs],
                  scratch_shapes=[pltpu.SemaphoreType.DMA((n, 7)), pltpu.SemaphoreType.DMA((n, 7)), pltpu.SemaphoreType.DMA((n,))])(*vs, after)


_HBM = pl.BlockSpec(memory_space=pltpu.HBM)
_SEM = pl.BlockSpec(memory_space=pltpu.SEMAPHORE)
_EFFECT = pltpu.SideEffectType.DATAFLOW_SIDE_EFFECTING


def _direct_copies(gather, v_refs, land_refs, send_sems, recv_sems, local_sems):
    x, y, c = _my_place()
    me = _index(x, y, c)
    local, remote = [], []
    for a, (v_ref, land_ref) in enumerate(zip(v_refs, land_refs)):
        local.append(pltpu.make_async_copy(v_ref if gather else v_ref.at[me], land_ref.at[me], local_sems.at[a]))
    for k in range(1, N_DEV):
        px = 1 - x if k & 4 else x
        py = 1 - y if k & 2 else y
        pc = 1 - c if k & 1 else c
        for a, (v_ref, land_ref) in enumerate(zip(v_refs, land_refs)):
            sem = a * (N_DEV - 1) + k - 1
            remote.append(pltpu.make_async_remote_copy(
                src_ref=v_ref if gather else v_ref.at[_index(px, py, pc)], dst_ref=land_ref.at[me], send_sem=send_sems.at[sem],
                recv_sem=recv_sems.at[sem], device_id=(px, py, pc), device_id_type=MESH))
    return local, remote


def _exchange_start(name, vs, gather, thru):
    n = len(vs)
    lands = [lax.empty((N_DEV,) + v.shape if gather else v.shape, v.dtype) for v in vs]

    def body(*refs):
        v_refs, land_refs = refs[:n], refs[n:2 * n]
        send_sems, recv_sems, local_sems = refs[2 * n + 1:2 * n + 4]
        local, remote = _direct_copies(gather, v_refs, land_refs, send_sems, recv_sems, local_sems)
        for cp in local + remote:
            cp.start()

    hbm = lambda a: pltpu.HBM(a.shape, a.dtype)
    res = _pcall(body, name=name,
                 out_shape=(pltpu.SemaphoreType.DMA((n * (N_DEV - 1),)), pltpu.SemaphoreType.DMA((n * (N_DEV - 1),)), pltpu.SemaphoreType.DMA((n,)),
                            *[hbm(a) for a in (*vs, *lands, thru)]),
                 in_specs=[_HBM] * (2 * n + 1), out_specs=(_SEM, _SEM, _SEM, *[_HBM] * (2 * n + 1)),
                 input_output_aliases={i: 3 + i for i in range(2 * n + 1)},
                 compiler_params=pltpu.CompilerParams(has_side_effects=_EFFECT))(
        *[pltpu.with_memory_space_constraint(a, pltpu.HBM) for a in (*vs, *lands, thru)])
    return (gather, res[:3], res[3:3 + n], res[3 + n:3 + 2 * n]), res[3 + 2 * n]


def _exchange_wait(name, started, after):
    gather, sems, vs, lands = started
    n = len(vs)

    def body(*refs):
        v_refs, land_refs = refs[:n], refs[n:2 * n]
        send_sems, recv_sems, local_sems = refs[2 * n:2 * n + 3]
        local, remote = _direct_copies(gather, v_refs, land_refs, send_sems, recv_sems, local_sems)
        for cp in local:
            cp.wait()
        for cp in remote:
            cp.wait_send()
            cp.wait_recv()

    hbm = lambda a: pltpu.HBM(a.shape, a.dtype)
    res = _pcall(body, name=name, out_shape=tuple(hbm(a) for a in (*vs, *lands)),
                 in_specs=[_HBM] * (2 * n) + [_SEM] * 3 + [pl.BlockSpec(memory_space=pl.ANY)], out_specs=tuple([_HBM] * (2 * n)),
                 input_output_aliases={i: i for i in range(2 * n)},
                 compiler_params=pltpu.CompilerParams(has_side_effects=_EFFECT))(*vs, *lands, *sems, after)
    return list(res[n:])


W_IN_SHARD = IN_WIDTH // N_DEV
SEG_ORDER = ("q_lat", "c_kv", "k_pe", "z_a", "dn_qkv", "dn_ab", "z_b", "dil_qkv", "z_c", "gate")
SEG_WIDTH = (384, 256, LANES, 512, 1536, LANES, 512, 4608, 512, 3072)


def _w_in_plan():
    plan = []

    def add(seg, c0, c1, dst):
        while c0 < c1:
            d = c0 // W_IN_SHARD
            e = min(c1, (d + 1) * W_IN_SHARD)
            plan.append((seg, dst, d, c0 - d * W_IN_SHARD, e - c0))
            dst += e - c0
            c0 = e

    half = MLA_ROPE // 2
    for i, name in enumerate(SEG_ORDER):
        if name == "k_pe":
            o = _SEG["k_pe"][0]
            add(i, o, o + half, 0)
            add(i, o + half, o + 2 * half, LANES // 2)
        elif name == "dn_ab":
            o = _SEG["dn_a"][0]
            add(i, o, o + 2 * DN_HEADS, 0)
        else:
            o, w = _SEG[name]
            add(i, o, o + w, 0)
    return plan


def _make_w_in_segments(name):
    plan = _w_in_plan()
    nseg = len(SEG_ORDER)
    t = 256

    def fwd_call(g):
        L = g.shape[1]

        def body(g_ref, *o_refs):
            for i in (SEG_ORDER.index("k_pe"), SEG_ORDER.index("dn_ab")):
                o_refs[i][...] = jnp.zeros_like(o_refs[i])
            for seg, dst, d, src, n in plan:
                o_refs[seg][:, dst:dst + n] = g_ref[d, :, src:src + n]

        return _pcall(body, name=f"{name}_fwd", grid=(L, D_MODEL // t),
                      in_specs=[pl.BlockSpec((N_DEV, None, t, W_IN_SHARD), lambda l, i: (0, l, i, 0))],
                      out_specs=[pl.BlockSpec((None, t, w), lambda l, i: (l, i, 0)) for w in SEG_WIDTH],
                      out_shape=[jax.ShapeDtypeStruct((L, D_MODEL, w), g.dtype) for w in SEG_WIDTH],
                      compiler_params=_params(("parallel", "parallel")))(g)

    def bwd_call(ds):
        L = ds[0].shape[0]

        def body(*refs):
            d_refs, g_ref = refs[:nseg], refs[nseg]
            for seg, dst, d, src, n in plan:
                g_ref[d, :, src:src + n] = d_refs[seg][:, dst:dst + n]

        return _pcall(body, name=f"{name}_bwd", grid=(L, D_MODEL // t),
                      in_specs=[pl.BlockSpec((None, t, w), lambda l, i: (l, i, 0)) for w in SEG_WIDTH],
                      out_specs=pl.BlockSpec((N_DEV, None, t, W_IN_SHARD), lambda l, i: (0, l, i, 0)),
                      out_shape=jax.ShapeDtypeStruct((N_DEV, L, D_MODEL, W_IN_SHARD), ds[0].dtype),
                      compiler_params=_params(("parallel", "parallel")))(*ds)

    @jax.custom_vjp
    def op(g):
        return tuple(fwd_call(g))

    op.defvjp(lambda g: (op(g), None), lambda _, ds: (bwd_call(tuple(ds)),))
    return op


def _pe_pad(a):
    h = MLA_ROPE // 2
    z = jnp.zeros(a.shape[:-1] + (h,), a.dtype)
    return jnp.concatenate([a[..., :h], z, a[..., h:], z], axis=-1)


def _layer_norm(tag, x, norm_g):
    return _make_rowwise(f"{tag}_norm", _f_norm, 512)((x,), (), (norm_g[None, :],), ())[0]


def _layer(tag, x, tables, W):
    h = _layer_norm(tag, x, W["norm_g"])
    return _layer_tail(tag, x, _make_multi_linear(f"{tag}_inproj", 10, INPROJ_DTYPE)(h, W["w_in_segments"]), tables, W)


def _layer_tail(tag, x, segments, tables, W):
    cos_p, sin_p, cos_h, sin_h = tables
    row = lambda a: a[None, :]
    q_lat, c_kv, kpe, z_a, dn_qkv, ab, z_b, dil_qkv, z_c, gl = segments

    qn_lat, ckvn, kp = _make_rowwise(f"{tag}_mla_a", _f_mla_a, 512)(
        (q_lat, c_kv, kpe), (cos_p, sin_p),
        (row(W["mla_q_a_norm_g"]), row(W["mla_kv_a_norm_g"]), row(_pe_pad(W["mla_k_norm_g"][LANES:]))), ())
    wq = W["mla_w_q_b"].reshape(MLA_Q_RANK, MLA_HEADS, MLA_QK)
    wq = jnp.concatenate([wq[:, :, :LANES].reshape(MLA_Q_RANK, -1), _pe_pad(wq[:, :, LANES:]).reshape(MLA_Q_RANK, -1)], axis=1)
    wkv = W["mla_w_kv_b"].reshape(MLA_KV_RANK, MLA_HEADS, 2 * LANES)
    (q8,) = _make_multi_linear(f"{tag}_qb", 1)(qn_lat, (wq,))
    kn_raw, v_mla = _make_multi_linear(f"{tag}_kvb", 2)(
        ckvn, (wkv[:, :, :LANES].reshape(MLA_KV_RANK, -1), wkv[:, :, LANES:].reshape(MLA_KV_RANK, -1)))
    qn, qp, kn = _make_rowwise(f"{tag}_mla_b", _f_mla_b, 512)(
        (q8, kn_raw), (cos_p, sin_p),
        (row(W["mla_q_norm_g"][:LANES]), row(_pe_pad(W["mla_q_norm_g"][LANES:])), row(W["mla_k_norm_g"][:LANES])), ())
    y_a = _make_mla_attn(f"{tag}_mla")(qn, qp, kn, kp, v_mla)

    mixed = _make_conv(f"{tag}_conv")(dn_qkv, W["dn_conv_w"])
    lane_head = jnp.arange(DN_HEADS * LANES) // LANES
    e_a = (jnp.arange(LANES)[:, None] == lane_head[None, :]).astype(F32)
    e_b = (jnp.arange(LANES)[:, None] == lane_head[None, :] + DN_HEADS).astype(F32)
    q_dn, k_dn, v_dn, g_dn, b_dn = _make_rowwise(f"{tag}_dn_pre", _f_dn_pre, 512)(
        (mixed, ab), (), (row(jnp.repeat(W["dn_a_log"], LANES)), row(jnp.repeat(W["dn_dt_bias"], LANES))), (e_a, e_b))
    o_dn = _make_delta_rule(f"{tag}_dn")(q_dn, k_dn, v_dn, g_dn, b_dn)

    qkv_dil = _make_rowwise(f"{tag}_dil_pre", _f_dil_pre, 256)(
        (dil_qkv,), (cos_h, sin_h), (row(W["dil_q_norm_g"]), row(W["dil_k_norm_g"])), ())
    n_groups = len(DIL_DILATIONS)
    o_lse = [_make_dil_attn(f"{tag}_dil{g}", d, DIL_TILE_ROWS[g])(qkv_dil[g], qkv_dil[n_groups + g], qkv_dil[2 * n_groups + g])
             for g, d in enumerate(DIL_DILATIONS)]

    ya, yb, yc = _make_rowwise(f"{tag}_merge_a", _f_merge_a, 256)(
        (y_a, z_a, o_dn, z_b, *[o for o, _ in o_lse], *[l for _, l in o_lse], z_c), (), (row(W["dn_out_norm_g"]),), ())
    (b0,) = _make_multi_linear(f"{tag}_br0", 1)(ya, (W["w_branch"][0],))
    (b1,) = _make_multi_linear(f"{tag}_br1", 1)(yb, (W["w_branch"][1],))
    (b2,) = _make_multi_linear(f"{tag}_br2", 1)(yc, (W["w_branch"][2],))
    (mix,) = _make_rowwise(f"{tag}_merge_b", _f_merge_b, 256)((b0, b1, b2, gl), (), (), ())
    return _make_resid_linear(f"{tag}_out")(x, mix, W["w_out"])


SHARDED = (("w_in", (D_MODEL, W_IN_SHARD)), ("mla_w_q_b", (MLA_Q_RANK, MLA_HEADS * MLA_QK // N_DEV)),
           ("mla_w_kv_b", (MLA_KV_RANK, MLA_HEADS * 2 * LANES // N_DEV)), ("w_branch", (3 * BRANCH_W, D_MODEL // N_DEV)),
           ("w_out", (D_MODEL // N_DEV, D_MODEL)), ("dn_conv_w", (DN_CONV, 3 * DN_HEADS * LANES // N_DEV)))
SMALL = (("norm_g", D_MODEL), ("mla_q_a_norm_g", MLA_Q_RANK), ("mla_kv_a_norm_g", MLA_KV_RANK), ("mla_q_norm_g", MLA_QK),
         ("mla_k_norm_g", MLA_QK), ("dn_a_log", DN_HEADS), ("dn_dt_bias", DN_HEADS), ("dn_out_norm_g", LANES),
         ("dil_q_norm_g", LANES), ("dil_k_norm_g", LANES))
WEIGHTS = ("norm_g", "w_in", "mla_q_a_norm_g", "mla_w_q_b", "mla_kv_a_norm_g", "mla_w_kv_b", "mla_q_norm_g", "mla_k_norm_g",
           "dn_conv_w", "dn_a_log", "dn_dt_bias", "dn_out_norm_g", "dil_q_norm_g", "dil_k_norm_g", "w_branch", "w_out")


def _round_up(n, m):
    return -(-n // m) * m


def _pack_vectors(pieces):
    return jnp.concatenate([jnp.pad(p, (0, _round_up(p.shape[0], LANES) - p.shape[0])) for p in pieces]).reshape(-1, LANES)


def _unpack_vectors(flat, sizes):
    out, off = [], 0
    flat = flat.reshape(-1)
    for n in sizes:
        out.append(flat[off:off + n])
        off += _round_up(n, LANES)
    return out


def _whole_weights(g, small):
    W = dict(small)
    W["mla_w_q_b"] = g["mla_w_q_b"].transpose(1, 0, 2).reshape(MLA_Q_RANK, -1)
    W["mla_w_kv_b"] = g["mla_w_kv_b"].transpose(1, 0, 2).reshape(MLA_KV_RANK, -1)
    W["w_branch"] = g["w_branch"].reshape(N_DEV, 3, BRANCH_W, -1).transpose(1, 2, 0, 3).reshape(3, BRANCH_W, D_MODEL)
    W["w_out"] = g["w_out"].reshape(D_MODEL, D_MODEL)
    W["dn_conv_w"] = g["dn_conv_w"].transpose(1, 0, 2).reshape(DN_CONV, -1)
    return W


def kernel(x, positions, norm_g, w_in, mla_q_a_norm_g, mla_w_q_b, mla_kv_a_norm_g, mla_w_kv_b, mla_q_norm_g, mla_k_norm_g, dn_conv_w, dn_a_log, dn_dt_bias, dn_out_norm_g, dil_q_norm_g, dil_k_norm_g, w_branch, w_out, loss_target, m_norm_g, m_w_in, m_mla_q_a_norm_g, m_mla_w_q_b, m_mla_kv_a_norm_g, m_mla_w_kv_b, m_mla_q_norm_g, m_mla_k_norm_g, m_dn_conv_w, m_dn_a_log, m_dn_dt_bias, m_dn_out_norm_g, m_dil_q_norm_g, m_dil_k_norm_g, m_w_branch, m_w_out, v_norm_g, v_w_in, v_mla_q_a_norm_g, v_mla_w_q_b, v_mla_kv_a_norm_g, v_mla_w_kv_b, v_mla_q_norm_g, v_mla_k_norm_g, v_dn_conv_w, v_dn_a_log, v_dn_dt_bias, v_dn_out_norm_g, v_dil_q_norm_g, v_dil_k_norm_g, v_w_branch, v_w_out):
    w = dict(norm_g=norm_g, w_in=w_in, mla_q_a_norm_g=mla_q_a_norm_g, mla_w_q_b=mla_w_q_b, mla_kv_a_norm_g=mla_kv_a_norm_g,
             mla_w_kv_b=mla_w_kv_b, mla_q_norm_g=mla_q_norm_g, mla_k_norm_g=mla_k_norm_g, dn_conv_w=dn_conv_w, dn_a_log=dn_a_log,
             dn_dt_bias=dn_dt_bias, dn_out_norm_g=dn_out_norm_g, dil_q_norm_g=dil_q_norm_g, dil_k_norm_g=dil_k_norm_g,
             w_branch=w_branch, w_out=w_out)
    m = dict(norm_g=m_norm_g, w_in=m_w_in, mla_q_a_norm_g=m_mla_q_a_norm_g, mla_w_q_b=m_mla_w_q_b, mla_kv_a_norm_g=m_mla_kv_a_norm_g,
             mla_w_kv_b=m_mla_w_kv_b, mla_q_norm_g=m_mla_q_norm_g, mla_k_norm_g=m_mla_k_norm_g, dn_conv_w=m_dn_conv_w,
             dn_a_log=m_dn_a_log, dn_dt_bias=m_dn_dt_bias, dn_out_norm_g=m_dn_out_norm_g, dil_q_norm_g=m_dil_q_norm_g,
             dil_k_norm_g=m_dil_k_norm_g, w_branch=m_w_branch, w_out=m_w_out)
    v = dict(norm_g=v_norm_g, w_in=v_w_in, mla_q_a_norm_g=v_mla_q_a_norm_g, mla_w_q_b=v_mla_w_q_b, mla_kv_a_norm_g=v_mla_kv_a_norm_g,
             mla_w_kv_b=v_mla_w_kv_b, mla_q_norm_g=v_mla_q_norm_g, mla_k_norm_g=v_mla_k_norm_g, dn_conv_w=v_dn_conv_w,
             dn_a_log=v_dn_a_log, dn_dt_bias=v_dn_dt_bias, dn_out_norm_g=v_dn_out_norm_g, dil_q_norm_g=v_dil_q_norm_g,
             dil_k_norm_g=v_dil_k_norm_g, w_branch=v_w_branch, w_out=v_w_out)
    x2, target = x[0], loss_target[0]
    pos = positions[0][:, None]

    names = [n for n, _ in SHARDED]
    view = lambda t, n, s: t[n].reshape((DEPTH,) + s)
    shards = [[(view(w, n, s) if n == "dn_conv_w" else view(w, n, s).astype(BF16))[l] for n, s in SHARDED] for l in range(DEPTH)]
    small = [{n: w[n][l] for n, _ in SMALL} for l in range(DEPTH)]
    gathered0 = dict(zip(names, _all_gather(shards[0])))
    gathering1, pos = _exchange_start("gather_layer1_start", shards[1], True, pos)
    tables = _rope_tables(pos, _rope_consts())

    def layer(l, g, small_l, x_l):
        tag = f"l{l}"
        seg_op = _make_w_in_segments(f"{tag}_w_in_segments")
        w_segs, vjp_segs = jax.vjp(lambda gw: tuple(s[0] for s in seg_op(gw[:, None])), g["w_in"])
        h, vjp_norm = jax.vjp(lambda x_, ng: _layer_norm(tag, x_, ng), x_l, small_l["norm_g"])
        segs = tuple(_mm(f"{tag}_inproj_fwd{i}", h, w_, "nn", out_dtype=INPROJ_DTYPE, **INPROJ_TILES["nn"]) for i, w_ in enumerate(w_segs))
        rest_g = {n: a for n, a in g.items() if n != "w_in"}
        rest_s = {n: a for n, a in small_l.items() if n != "norm_g"}
        y, vjp_tail = jax.vjp(lambda sg, x_, gg, ss: _layer_tail(tag, x_, sg, tables, _whole_weights(gg, ss)), segs, x_l, rest_g, rest_s)

        def backward(dy):
            dsegs, dx_skip, d_rest_g, d_rest_s = vjp_tail(dy)
            others, big = _exchange_start(f"exchange_{tag}_others_start", [d_rest_g[n] for n in names if n != "w_in"], False, dsegs[7])
            dsegs = tuple(dsegs[:7]) + (big,) + tuple(dsegs[8:])
            dws = tuple(_mm(f"{tag}_inproj_dw{i}", h, d, "tn", out_dtype=w_.dtype, **INPROJ_TILES["tn"])
                        for i, (w_, d) in enumerate(zip(w_segs, dsegs)))
            projection, first = _exchange_start(f"exchange_{tag}_w_in_start", [vjp_segs(dws)[0]], False, dsegs[0])
            dsegs = (first,) + tuple(dsegs[1:])
            dh = None
            for i, group in enumerate(INPROJ_DH_GROUPS):
                dh = _mm_nt_sum(f"{tag}_inproj_dh{i}", [(dsegs[s], w_segs[s]) for s in group], dh,
                                h.dtype if i == len(INPROJ_DH_GROUPS) - 1 else F32)
            dx_norm, d_norm_g = vjp_norm(dh)
            return (projection, others), dx_skip + dx_norm, dict(d_rest_s, norm_g=d_norm_g)

        def landed(exchanging, after):
            projection, others = (_exchange_wait(f"exchange_{tag}_{k}_wait", e, after) for k, e in zip(("w_in", "others"), exchanging))
            return projection + others
        return y, backward, landed

    y0, backward0, landed0 = layer(0, gathered0, small[0], x2)
    gathered1 = dict(zip(names, _exchange_wait("gather_layer1_wait", gathering1, y0)))
    y1, backward1, landed1 = layer(1, gathered1, small[1], y0)
    loss_splat, dy = _loss_call(y1, target)
    loss = lax.psum(loss_splat[0, 0], ("x", "y", "c"))
    exchanging1, d_y0, g_small1 = backward1(dy)
    exchanging0, g_x, g_small0 = backward0(d_y0)

    state = lambda n, s: (view(w, n, s), view(m, n, s), view(v, n, s))
    parts1 = landed1(exchanging1, g_x)
    updated = {n: _adamw_layer(f"adamw_l1_{n}", parts1[i], *state(n, s), 1) for i, (n, s) in enumerate(SHARDED)}
    g_small = (g_small0, g_small1)
    sizes = [k for _ in range(DEPTH) for _, k in SMALL]
    g_vec = _pack_vectors([g_small[l][n] for l in range(DEPTH) for n, _ in SMALL])
    (parts_vec,) = _all_to_all([jnp.broadcast_to(g_vec[None], (N_DEV,) + g_vec.shape)], updated["w_in"][0])
    parts0 = landed0(exchanging0, parts_vec)

    vec = lambda t: _pack_vectors([t[n][l] for l in range(DEPTH) for n, _ in SMALL])[None]
    outs = {}
    for i, (n, s) in enumerate(SHARDED):
        res = _adamw_layer(f"adamw_l0_{n}", parts0[i], *state(n, s), 0, earlier=updated[n])
        outs[n] = [o.reshape(w[n].shape) for o in res]
    vec_outs = [_unpack_vectors(o, sizes) for o in _adamw_layer("adamw_vectors", parts_vec, vec(w), vec(m), vec(v), 0)]
    for i, (n, _) in enumerate(SMALL):
        outs[n] = [jnp.stack([o[l * len(SMALL) + i] for l in range(DEPTH)]) for o in vec_outs]
    return (loss, g_x[None], *[outs[n][k] for k in range(4) for n in WEIGHTS])
```

```python
import functools
import math

import jax
import jax.numpy as jnp
from jax import lax
from jax.experimental import pallas as pl
from jax.experimental.pallas import tpu as pltpu

F32 = jnp.float32
BF16 = jnp.bfloat16
HI = lax.Precision.HIGHEST
MESH = pl.DeviceIdType.MESH

N_DEV = 8
D_MODEL = 1024
DEPTH = 2
RMS_EPS = 1e-6
ROPE_THETA = 10000.0
LANES = 128
MLA_HEADS = 4
MLA_ROPE = 64
MLA_QK = 192
MLA_Q_RANK = 384
MLA_KV_RANK = 256
DN_HEADS = 4
DN_CHUNK = 64
DN_CONV = 4
DIL_HEADS = 12
DIL_GROUP_HEADS = 4
DIL_DILATIONS = (1, 4, 16)
DIL_BLOCK = 128
BRANCH_W = 512
IN_WIDTH = 11464
NEG = -1e30
VMEM_LIMIT = 56 * 1024 * 1024

ADAM_LR, ADAM_B1, ADAM_B2, ADAM_EPS, ADAM_WD, ADAM_STEP = 0.001, 0.9, 0.999, 1e-08, 0.01, 10

_SEG = {}
_off = 0
for _n, _w in (("q_lat", 384), ("c_kv", 256), ("k_pe", 64), ("z_a", 512), ("dn_qkv", 1536), ("dn_a", 4), ("dn_b", 4),
               ("z_b", 512), ("dil_qkv", 4608), ("z_c", 512), ("gate", 3072)):
    _SEG[_n] = (_off, _w)
    _off += _w
assert _off == IN_WIDTH


def _pcall(body, **kw):
    return pl.pallas_call(body, **kw)


def _params(sem=None):
    return pltpu.CompilerParams(dimension_semantics=sem, vmem_limit_bytes=VMEM_LIMIT)


def _tile(n, target, mult):
    t = (min(n, target) // mult) * mult
    while t >= mult:
        if n % t == 0:
            return t
        t -= mult
    return n


def _mm(name, a, b, mode, out_dtype=F32, acc=None, tm=1024, tn=512, tk=1024):
    if mode == "nn":
        (M, K), (_, N) = a.shape, b.shape
    elif mode == "nt":
        (M, K), (N, _) = a.shape, b.shape
    else:
        (K, M), (_, N) = a.shape, b.shape
    tm, tn, tk = _tile(M, tm, LANES), _tile(N, tn, LANES), _tile(K, tk, LANES)
    nk = K // tk
    dims = {"nn": (((1,), (0,)), ((), ())), "nt": (((1,), (1,)), ((), ())), "tn": (((0,), (0,)), ((), ()))}[mode]
    a_spec = pl.BlockSpec((tk, tm), lambda i, j, k: (k, i)) if mode == "tn" else pl.BlockSpec((tm, tk), lambda i, j, k: (i, k))
    b_spec = pl.BlockSpec((tn, tk), lambda i, j, k: (j, k)) if mode == "nt" else pl.BlockSpec((tk, tn), lambda i, j, k: (k, j))
    o_spec = pl.BlockSpec((tm, tn), lambda i, j, k: (i, j))
    has_acc = acc is not None

    def body(*refs):
        a_ref, b_ref = refs[:2]
        c_ref = refs[2] if has_acc else None
        o_ref = refs[3] if has_acc else refs[2]
        prod = lax.dot_general(a_ref[...].astype(BF16), b_ref[...].astype(BF16), dims, preferred_element_type=F32)
        if nk == 1:
            o_ref[...] = (prod + c_ref[...].astype(F32) if has_acc else prod).astype(out_dtype)
            return
        acc_ref = refs[-1]
        k = pl.program_id(2)

        @pl.when(k == 0)
        def _():
            acc_ref[...] = prod + c_ref[...].astype(F32) if has_acc else prod

        @pl.when(k > 0)
        def _():
            acc_ref[...] += prod

        @pl.when(k == nk - 1)
        def _():
            o_ref[...] = acc_ref[...].astype(out_dtype)

    ins = [a, b] + ([acc] if has_acc else [])
    in_specs = [a_spec, b_spec] + ([o_spec] if has_acc else [])
    return _pcall(body, name=name, grid=(M // tm, N // tn, nk), in_specs=in_specs, out_specs=o_spec,
                  out_shape=jax.ShapeDtypeStruct((M, N), out_dtype), scratch_shapes=[pltpu.VMEM((tm, tn), F32)] if nk > 1 else [],
                  compiler_params=_params(("parallel", "parallel", "arbitrary")))(*ins)


def _mm_nt_sum(name, pairs, acc, out_dtype, tm=512, tn=1024):
    M, N = pairs[0][0].shape[0], pairs[0][1].shape[0]
    tm, tn = _tile(M, tm, LANES), _tile(N, tn, LANES)
    n = len(pairs)
    has_acc = acc is not None
    o_spec = pl.BlockSpec((tm, tn), lambda i, j: (i, j))

    def body(*refs):
        total = refs[2 * n][...].astype(F32) if has_acc else None
        for a_ref, b_ref in zip(refs[:n], refs[n:2 * n]):
            prod = lax.dot_general(a_ref[...].astype(BF16), b_ref[...].astype(BF16), (((1,), (1,)), ((), ())), preferred_element_type=F32)
            total = prod if total is None else total + prod
        refs[-1][...] = total.astype(out_dtype)

    in_specs = ([pl.BlockSpec((tm, a.shape[1]), lambda i, j: (i, 0)) for a, _ in pairs]
                + [pl.BlockSpec((tn, b.shape[1]), lambda i, j: (j, 0)) for _, b in pairs] + ([o_spec] if has_acc else []))
    return _pcall(body, name=name, grid=(M // tm, N // tn), in_specs=in_specs, out_specs=o_spec,
                  out_shape=jax.ShapeDtypeStruct((M, N), out_dtype), compiler_params=_params(("parallel", "parallel")))(
        *[a for a, _ in pairs], *[b for _, b in pairs], *([acc] if has_acc else []))


def _mm_nn_many(name, a, bs, out_dtype, tm=1024):
    M, K = a.shape
    tm = _tile(M, tm, LANES)
    n = len(bs)

    def body(*refs):
        a_t = refs[0][...].astype(BF16)
        for b_ref, o_ref in zip(refs[1:1 + n], refs[1 + n:]):
            o_ref[...] = jnp.dot(a_t, b_ref[...].astype(BF16), preferred_element_type=F32).astype(out_dtype)

    return _pcall(body, name=name, grid=(M // tm,),
                  in_specs=[pl.BlockSpec((tm, K), lambda i: (i, 0))] + [pl.BlockSpec(b.shape, lambda i: (0, 0)) for b in bs],
                  out_specs=[pl.BlockSpec((tm, b.shape[1]), lambda i: (i, 0)) for b in bs],
                  out_shape=[jax.ShapeDtypeStruct((M, b.shape[1]), out_dtype) for b in bs],
                  compiler_params=_params(("parallel",)))(a, *bs)


def _mm_tn_many(name, a, bs, out_dtype, tk=1024):
    K, M = a.shape
    tk = _tile(K, tk, LANES)
    nk, n = K // tk, len(bs)

    def body(*refs):
        a_t = refs[0][...].astype(BF16)
        k = pl.program_id(0)
        for b_ref, o_ref, acc in zip(refs[1:1 + n], refs[1 + n:1 + 2 * n], refs[1 + 2 * n:]):
            prod = lax.dot_general(a_t, b_ref[...].astype(BF16), (((0,), (0,)), ((), ())), preferred_element_type=F32)

            @pl.when(k == 0)
            def _(acc=acc, prod=prod):
                acc[...] = prod

            @pl.when(k > 0)
            def _(acc=acc, prod=prod):
                acc[...] += prod

            @pl.when(k == nk - 1)
            def _(acc=acc, o_ref=o_ref):
                o_ref[...] = acc[...].astype(out_dtype)

    return _pcall(body, name=name, grid=(nk,),
                  in_specs=[pl.BlockSpec((tk, M), lambda k: (k, 0))] + [pl.BlockSpec((tk, b.shape[1]), lambda k: (k, 0)) for b in bs],
                  out_specs=[pl.BlockSpec((M, b.shape[1]), lambda k: (0, 0)) for b in bs],
                  out_shape=[jax.ShapeDtypeStruct((M, b.shape[1]), out_dtype) for b in bs],
                  scratch_shapes=[pltpu.VMEM((M, b.shape[1]), F32) for b in bs],
                  compiler_params=_params(("arbitrary",)))(a, *bs)


INPROJ_DH_GROUPS = ((0, 1, 2, 3, 5, 6, 8), (4,), (9,), (7,))
INPROJ_NARROW = INPROJ_DH_GROUPS[0]

INPROJ_DTYPE = BF16

INPROJ_TILES = {"nn": dict(tm=1024, tn=1536, tk=1024), "tn": dict(tm=1024, tn=768, tk=2048)}


def _make_multi_linear(name, n, out_dtype=F32):
    @jax.custom_vjp
    def op(h, ws):
        return tuple(_mm(f"{name}_fwd{i}", h, w, "nn", out_dtype=out_dtype) for i, w in enumerate(ws))

    def fwd(h, ws):
        return op(h, ws), (h, ws)

    def bwd(res, douts):
        h, ws = res
        dh = None
        for i, (w, d) in enumerate(zip(ws, douts)):
            dh = _mm(f"{name}_dh{i}", d, w, "nt", acc=dh, out_dtype=h.dtype if i == len(ws) - 1 else F32)
        dws = tuple(_mm(f"{name}_dw{i}", h, d, "tn", out_dtype=w.dtype) for i, (w, d) in enumerate(zip(ws, douts)))
        return dh, dws

    op.defvjp(fwd, bwd)
    return op


def _make_resid_linear(name):
    @jax.custom_vjp
    def op(x, a, w):
        return _mm(f"{name}_fwd", a, w, "nn", acc=x)

    def fwd(x, a, w):
        return op(x, a, w), (a, w)

    def bwd(res, dy):
        a, w = res
        return dy, _mm(f"{name}_da", dy, w, "nt", out_dtype=a.dtype), _mm(f"{name}_dw", a, dy, "tn", out_dtype=w.dtype)

    op.defvjp(fwd, bwd)
    return op


def _make_rowwise(name, f, tile):
    def specs(rows, aux, params, consts, t):
        row = [pl.BlockSpec((t, a.shape[1]), lambda i: (i, 0)) for a in (*rows, *aux)]
        full = [pl.BlockSpec(p.shape, lambda i: (0, 0)) for p in (*params, *consts)]
        return row, full

    def fwd_call(rows, aux, params, consts):
        S = rows[0].shape[0]
        t = min(tile, S)
        n_in = len(rows) + len(aux) + len(params) + len(consts)
        shp = lambda a: jax.ShapeDtypeStruct((t, a.shape[1]), a.dtype)
        outs = jax.eval_shape(f, *[shp(a) for a in (*rows, *aux)], *params, *consts)
        row_specs, full_specs = specs(rows, aux, params, consts, t)

        def body(*refs):
            res = f(*[r[...] for r in refs[:n_in]])
            for o_ref, o in zip(refs[n_in:], res):
                o_ref[...] = o

        return _pcall(body, name=f"{name}_fwd", grid=(S // t,), in_specs=row_specs + full_specs,
                      out_specs=[pl.BlockSpec((t, o.shape[1]), lambda i: (i, 0)) for o in outs],
                      out_shape=[jax.ShapeDtypeStruct((S, o.shape[1]), o.dtype) for o in outs],
                      compiler_params=_params(("parallel",)))(*rows, *aux, *params, *consts)

    def bwd_call(rows, aux, params, consts, douts):
        S = rows[0].shape[0]
        t = min(tile, S)
        nr, na, npar, nc, nd = len(rows), len(aux), len(params), len(consts), len(douts)
        row_specs, full_specs = specs(rows, aux, params, consts, t)

        def body(*refs):
            vals = [r[...] for r in refs[:nr + na + npar + nc]]
            rv, av = vals[:nr], vals[nr:nr + na]
            pv, cv = vals[nr + na:nr + na + npar], vals[nr + na + npar:]
            dv = tuple(r[...] for r in refs[nr + na + npar + nc:nr + na + npar + nc + nd])
            out_refs = refs[nr + na + npar + nc + nd:]
            _, vjp = jax.vjp(lambda *rp: f(*rp[:nr], *av, *rp[nr:], *cv), *rv, *pv)
            grads = vjp(dv)
            for o_ref, g in zip(out_refs[:nr], grads[:nr]):
                o_ref[...] = g
            first = pl.program_id(0) == 0
            for o_ref, g in zip(out_refs[nr:], grads[nr:]):
                @pl.when(first)
                def _(o_ref=o_ref):
                    o_ref[...] = jnp.zeros_like(o_ref)
                o_ref[...] += g

        res = _pcall(body, name=f"{name}_bwd", grid=(S // t,),
                     in_specs=row_specs + full_specs + [pl.BlockSpec((t, d.shape[1]), lambda i: (i, 0)) for d in douts],
                     out_specs=[pl.BlockSpec((t, a.shape[1]), lambda i: (i, 0)) for a in rows]
                     + [pl.BlockSpec(p.shape, lambda i: (0, 0)) for p in params],
                     out_shape=[jax.ShapeDtypeStruct(a.shape, a.dtype) for a in (*rows, *params)],
                     compiler_params=_params(("arbitrary",)))(*rows, *aux, *params, *consts, *douts)
        return tuple(res[:nr]), tuple(res[nr:])

    @jax.custom_vjp
    def op(rows, aux, params, consts):
        return tuple(fwd_call(rows, aux, params, consts))

    def fwd(rows, aux, params, consts):
        return op(rows, aux, params, consts), (rows, aux, params, consts)

    def bwd(res, douts):
        rows, aux, params, consts = res
        drows, dparams = bwd_call(rows, aux, params, consts, tuple(douts))
        zeros = lambda xs: tuple(jnp.zeros_like(a) for a in xs)
        return drows, zeros(aux), dparams, zeros(consts)

    op.defvjp(fwd, bwd)
    return op


@jax.custom_vjp
def _swap_halves(x):
    return pltpu.roll(x, LANES // 2, 1)


_swap_halves.defvjp(lambda x: (_swap_halves(x), None), lambda _, g: (_swap_halves(g),))


def _rope(x, cos_t, sin_t):
    return x * cos_t + _swap_halves(x) * sin_t


def _rms(x, g, n=None):
    n = x.shape[-1] if n is None else n
    return x * lax.rsqrt(jnp.sum(x * x, axis=-1, keepdims=True) * (1.0 / n) + RMS_EPS) * g


def _heads(x):
    return [x[:, i * LANES:(i + 1) * LANES] for i in range(x.shape[1] // LANES)]


def _cat(xs):
    return jnp.concatenate(xs, axis=1)


def _silu(x):
    return x * jax.nn.sigmoid(x)


def _f_norm(x, g):
    return (_rms(x, g).astype(BF16),)


def _f_mla_a(q_lat, c_kv, kpe, cos_p, sin_p, qa_g, kva_g, kpe_g):
    q_lat, c_kv, kpe = (t.astype(F32) for t in (q_lat, c_kv, kpe))
    kp =_rope(_rms(kpe, kpe_g, MLA_ROPE), cos_p, sin_p)
    return _rms(q_lat, qa_g).astype(BF16), _rms(c_kv, kva_g).astype(BF16), _cat([kp] * MLA_HEADS)


def _f_mla_b(q8, kn_raw, cos_p, sin_p, qn_g, qp_g, kn_g):
    hs = _heads(q8)
    qn = _cat([_rms(h, qn_g) for h in hs[:MLA_HEADS]])
    qp = _cat([_rope(_rms(h, qp_g, MLA_ROPE), cos_p, sin_p) for h in hs[MLA_HEADS:]])
    kn = _cat([_rms(h, kn_g) for h in _heads(kn_raw)])
    return qn, qp, kn


def _softplus(x):
    return jnp.maximum(x, 0.0) + jnp.log(1.0 + jnp.exp(-jnp.abs(x)))


def _l2n(x):
    return x * lax.rsqrt(jnp.sum(x * x, axis=-1, keepdims=True) + 1e-6)


def _f_dn_pre(mixed, ab, alog_f, dtb_f, e_a, e_b):
    hs = _heads(mixed)
    q = _cat([_l2n(h) * (LANES ** -0.5) for h in hs[:DN_HEADS]])
    k = _cat([_l2n(h) for h in hs[DN_HEADS:2 * DN_HEADS]])
    v = _cat(hs[2 * DN_HEADS:])
    ab = ab.astype(F32)
    a_f = jnp.dot(ab, e_a, precision=HI, preferred_element_type=F32)
    b_f = jnp.dot(ab, e_b, precision=HI, preferred_element_type=F32)
    g = -jnp.exp(alog_f) * _softplus(a_f + dtb_f)
    return q, k, v, g, jax.nn.sigmoid(b_f)


def _f_dil_pre(qkv, cos_h, sin_h, q_g, k_g):
    hs = [h.astype(F32) for h in _heads(qkv)]
    q = [_rope(_rms(h, q_g), cos_h, sin_h) for h in hs[:DIL_HEADS]]
    k = [_rope(_rms(h, k_g), cos_h, sin_h) for h in hs[DIL_HEADS:2 * DIL_HEADS]]
    v = hs[2 * DIL_HEADS:]
    group = lambda xs, g: _cat(xs[g * DIL_GROUP_HEADS:(g + 1) * DIL_GROUP_HEADS])
    return tuple(group(xs, g) for xs in (q, k, v) for g in range(len(DIL_DILATIONS)))


def _f_merge_a(y_a, z_a, o_dn, z_b, o0, o1, o2, l0, l1, l2, z_c, out_g):
    z_a, z_b, z_c = (z.astype(F32) for z in (z_a, z_b, z_c))
    y_b = _cat([_rms(h, out_g) for h in _heads(o_dn)])
    os_, ls = [_heads(o) for o in (o0, o1, o2)], [_heads(l) for l in (l0, l1, l2)]
    y_c = []
    for j in range(DIL_GROUP_HEADS):
        l3 = [ls[g][j] for g in range(3)]
        m = jnp.maximum(jnp.maximum(l3[0], l3[1]), l3[2])
        e3 = [jnp.exp(l - m) for l in l3]
        den = e3[0] + e3[1] + e3[2]
        y_c.append(sum(e3[g] * os_[g][j] for g in range(3)) / den)
    return tuple(t.astype(BF16) for t in (y_a * _silu(z_a), y_b * _silu(z_b), _cat(y_c) * _silu(z_c)))


def _f_merge_b(b0, b1, b2, gl):
    gs = [jax.nn.sigmoid(gl[:, i * D_MODEL:(i + 1) * D_MODEL].astype(F32)) for i in range(3)]
    return ((gs[0] * b0 + gs[1] * b1 + gs[2] * b2).astype(BF16),)


def _rope_tables(pos, inv_sign):
    S = pos.shape[0]
    t = min(S, 1024)

    def body(p_ref, c_ref, cp, sp, ch, sh):
        p = p_ref[...].astype(F32)
        c = c_ref[...]
        ang_p, ang_h = p * c[0:1], p * c[2:3]
        cp[...] = jnp.cos(ang_p) * jnp.abs(c[1:2])
        sp[...] = jnp.sin(ang_p) * c[1:2]
        ch[...] = jnp.cos(ang_h)
        sh[...] = jnp.sin(ang_h) * c[3:4]

    row = pl.BlockSpec((t, LANES), lambda i: (i, 0))
    return _pcall(body, name="rope_tables", grid=(S // t,),
                  in_specs=[pl.BlockSpec((t, 1), lambda i: (i, 0)), pl.BlockSpec((4, LANES), lambda i: (0, 0))],
                  out_specs=[row] * 4, out_shape=[jax.ShapeDtypeStruct((S, LANES), F32)] * 4,
                  compiler_params=_params(("parallel",)))(pos, inv_sign)


def _rope_consts():
    half_p, half_h = MLA_ROPE // 2, LANES // 2
    inv_p = 1.0 / (ROPE_THETA ** (jnp.arange(0, MLA_ROPE, 2, dtype=F32) / MLA_ROPE))
    inv_h = 1.0 / (ROPE_THETA ** (jnp.arange(0, LANES, 2, dtype=F32) / LANES))
    z = jnp.zeros((half_p,), F32)
    o = jnp.ones((half_p,), F32)
    return jnp.stack([jnp.concatenate([inv_p, z, inv_p, z]), jnp.concatenate([-o, z, o, z]),
                      jnp.concatenate([inv_h, inv_h]), jnp.concatenate([-jnp.ones((half_h,), F32), jnp.ones((half_h,), F32)])])


def _shift_rows(x, s, up):
    n = x.shape[0]
    r = lax.broadcasted_iota(jnp.int32, x.shape, 0)
    if up:
        return jnp.where(r < n - s, pltpu.roll(x, n - s, 0), 0.0)
    return jnp.where(r >= s, pltpu.roll(x, s, 0), 0.0)


def _make_shift(s):
    @jax.custom_vjp
    def sh(x):
        return _shift_rows(x, s, False)

    sh.defvjp(lambda x: (sh(x), None), lambda _, g: (_shift_rows(g, s, True),))
    return sh


def _f_conv(x, w):
    x = x.astype(F32)
    y = x * w[DN_CONV - 1:DN_CONV]
    for j in range(DN_CONV - 1):
        y = y + _make_shift(DN_CONV - 1 - j)(x) * w[j:j + 1]
    return _silu(y)


def _make_conv(name):
    def call(x, w, dy=None):
        S, C = x.shape
        col = pl.BlockSpec((S, LANES), lambda i: (0, i))
        wsp = pl.BlockSpec((DN_CONV, LANES), lambda i: (0, i))
        if dy is None:
            def body(x_ref, w_ref, o_ref):
                o_ref[...] = _f_conv(x_ref[...], w_ref[...])
            return _pcall(body, name=f"{name}_fwd", grid=(C // LANES,), in_specs=[col, wsp], out_specs=col,
                          out_shape=jax.ShapeDtypeStruct(x.shape, F32), compiler_params=_params(("parallel",)))(x, w)

        def body(x_ref, w_ref, dy_ref, dx_ref, dw_ref):
            _, vjp = jax.vjp(_f_conv, x_ref[...], w_ref[...])
            dx_ref[...], dw_ref[...] = vjp(dy_ref[...])
        return _pcall(body, name=f"{name}_bwd", grid=(C // LANES,), in_specs=[col, wsp, col], out_specs=[col, wsp],
                      out_shape=[jax.ShapeDtypeStruct(x.shape, x.dtype), jax.ShapeDtypeStruct(w.shape, F32)],
                      compiler_params=_params(("parallel",)))(x, w, dy)

    @jax.custom_vjp
    def op(x, w):
        return call(x, w)

    op.defvjp(lambda x, w: (op(x, w), (x, w)), lambda res, dy: tuple(call(*res, dy)))
    return op


def _dot_nt(a, b):
    return lax.dot_general(a.astype(BF16), b.astype(BF16), (((1,), (1,)), ((), ())), preferred_element_type=F32)


def _dot_nn(a, b):
    return jnp.dot(a.astype(BF16), b.astype(BF16), preferred_element_type=F32)


def _dot_tn(a, b):
    return lax.dot_general(a.astype(BF16), b.astype(BF16), (((0,), (0,)), ((), ())), preferred_element_type=F32)


def _mla_scores(qn_r, qp_r, kn_r, kp_r, diagonal):
    scale = MLA_QK ** -0.5
    s = _dot_nt(qn_r[...] * scale, kn_r[...]) + _dot_nt(qp_r[...] * scale, kp_r[...])
    if diagonal:
        r = lax.broadcasted_iota(jnp.int32, s.shape, 0)
        c = lax.broadcasted_iota(jnp.int32, s.shape, 1)
        s = jnp.where(c <= r, s, NEG)
    return s


def _on_causal_pairs(qi, ki, step):
    @pl.when(ki < qi)
    def _():
        step(False)

    @pl.when(ki == qi)
    def _():
        step(True)


def _causal_pairs(n, t, by_key):
    pairs = [(q, k) for k in range(n) for q in range(k, n)] if by_key else [(q, k) for q in range(n) for k in range(q + 1)]
    qt, kt = (jnp.array([p[i] for p in pairs], jnp.int32) for i in (0, 1))
    return (qt, kt, pl.BlockSpec((t, LANES), lambda h, p, qt_r, kt_r: (qt_r[p], h)),
            pl.BlockSpec((t, LANES), lambda h, p, qt_r, kt_r: (kt_r[p], h)))


def _make_mla_attn(name):
    scale = MLA_QK ** -0.5

    def fwd_call(qn, qp, kn, kp, v):
        S = qn.shape[0]
        t = min(S, 512)
        n = S // t
        qt, kt, qs, ks = _causal_pairs(n, t, by_key=False)

        def body(qt_r, kt_r, qn_r, qp_r, kn_r, kp_r, v_r, o_r, lse_r, m_s, l_s, acc_s):
            qi, ki = qt_r[pl.program_id(1)], kt_r[pl.program_id(1)]

            @pl.when(ki == 0)
            def _():
                m_s[...] = jnp.full_like(m_s, NEG)
                l_s[...] = jnp.zeros_like(l_s)
                acc_s[...] = jnp.zeros_like(acc_s)

            def step(diagonal):
                s = _mla_scores(qn_r, qp_r, kn_r, kp_r, diagonal)
                m_old = m_s[...]
                m_new = jnp.maximum(m_old, jnp.max(s, axis=-1, keepdims=True))
                p = jnp.exp(s - m_new[:, :1])
                alpha = jnp.exp(m_old - m_new)
                l_s[...] = alpha * l_s[...] + jnp.sum(p, axis=-1, keepdims=True)
                acc_s[...] = alpha * acc_s[...] + _dot_nn(p, v_r[...])
                m_s[...] = m_new
            _on_causal_pairs(qi, ki, step)

            @pl.when(ki == qi)
            def _():
                o_r[...] = acc_s[...] / l_s[...]
                lse_r[...] = m_s[...] + jnp.log(l_s[...])

        spec = pltpu.PrefetchScalarGridSpec(num_scalar_prefetch=2, grid=(MLA_HEADS, qt.shape[0]), in_specs=[qs, qs, ks, ks, ks],
                                            out_specs=[qs, qs], scratch_shapes=[pltpu.VMEM((t, LANES), F32)] * 3)
        return _pcall(body, name=f"{name}_fwd", grid_spec=spec, out_shape=[jax.ShapeDtypeStruct((S, MLA_HEADS * LANES), F32)] * 2,
                      compiler_params=_params(("parallel", "arbitrary")))(qt, kt, qn, qp, kn, kp, v)

    def bwd_call(qn, qp, kn, kp, v, o, lse, do):
        S = qn.shape[0]
        t = min(S, 512)
        n = S // t
        qt, kt, qs, ks = _causal_pairs(n, t, by_key=True)
        head = pl.BlockSpec((S, LANES), lambda h, p, qt_r, kt_r: (0, h))
        n_pairs = qt.shape[0]

        def body(qt_r, kt_r, qn_r, qp_r, kn_r, kp_r, v_r, o_r, lse_r, do_r, dqn_r, dqp_r, dkn_r, dkp_r, dv_r, dkn_s, dkp_s, dv_s, dl_s):
            pair = pl.program_id(1)
            qi, ki = qt_r[pair], kt_r[pair]
            rows = pl.ds(pl.multiple_of(qi * t, t), t)

            @pl.when(pair == 0)
            def _():
                dqn_r[...] = jnp.zeros_like(dqn_r)
                dqp_r[...] = jnp.zeros_like(dqp_r)

            @pl.when(ki == 0)
            def _():
                dl_s[rows, :] = jnp.broadcast_to(jnp.sum(do_r[...] * o_r[...], axis=-1, keepdims=True), (t, LANES))

            @pl.when(qi == ki)
            def _():
                dkn_s[...] = jnp.zeros_like(dkn_s)
                dkp_s[...] = jnp.zeros_like(dkp_s)
                dv_s[...] = jnp.zeros_like(dv_s)

            def step(diagonal):
                p = jnp.exp(_mla_scores(qn_r, qp_r, kn_r, kp_r, diagonal) - lse_r[...][:, :1])
                ds = p * (_dot_nt(do_r[...], v_r[...]) - dl_s[rows, :][:, :1])
                dv_s[...] += _dot_tn(p, do_r[...])
                dkn_s[...] += _dot_tn(ds, qn_r[...] * scale)
                dkp_s[...] += _dot_tn(ds, qp_r[...] * scale)
                dqn_r[rows, :] += _dot_nn(ds, kn_r[...])
                dqp_r[rows, :] += _dot_nn(ds, kp_r[...])
            _on_causal_pairs(qi, ki, step)

            @pl.when(qi == n - 1)
            def _():
                dkn_r[...] = dkn_s[...]
                dkp_r[...] = dkp_s[...]
                dv_r[...] = dv_s[...]

            @pl.when(pair == n_pairs - 1)
            def _():
                dqn_r[...] = dqn_r[...] * scale
                dqp_r[...] = dqp_r[...] * scale

        spec = pltpu.PrefetchScalarGridSpec(num_scalar_prefetch=2, grid=(MLA_HEADS, n_pairs), in_specs=[qs, qs, ks, ks, ks, qs, qs, qs],
                                            out_specs=[head, head, ks, ks, ks],
                                            scratch_shapes=[pltpu.VMEM((t, LANES), F32)] * 3 + [pltpu.VMEM((S, LANES), F32)])
        return _pcall(body, name=f"{name}_bwd", grid_spec=spec, out_shape=[jax.ShapeDtypeStruct((S, MLA_HEADS * LANES), F32)] * 5,
                      compiler_params=_params(("parallel", "arbitrary")))(qt, kt, qn, qp, kn, kp, v, o, lse, do)

    @jax.custom_vjp
    def op(qn, qp, kn, kp, v):
        return fwd_call(qn, qp, kn, kp, v)[0]

    def fwd(qn, qp, kn, kp, v):
        o, lse = fwd_call(qn, qp, kn, kp, v)
        return o, (qn, qp, kn, kp, v, o, lse)

    def bwd(res, do):
        return tuple(bwd_call(*res, do))

    op.defvjp(fwd, bwd)
    return op


def _dil_block(q, kp, kc, vp, vc, has_prev):
    scale = LANES ** -0.5
    r = lax.broadcasted_iota(jnp.int32, (DIL_BLOCK, 2 * DIL_BLOCK), 0)
    c = lax.broadcasted_iota(jnp.int32, (DIL_BLOCK, 2 * DIL_BLOCK), 1)
    valid = ((c < DIL_BLOCK) & (c >= r) & has_prev) | ((c >= DIL_BLOCK) & (c - DIL_BLOCK <= r))
    s = jnp.where(valid, _dot_nt(q * scale, jnp.concatenate([kp, kc], axis=0)), NEG)
    m = jnp.max(s, axis=-1, keepdims=True)
    e = jnp.exp(s - m)
    den = jnp.sum(e, axis=-1, keepdims=True)
    o = _dot_nn(e, jnp.concatenate([vp, vc], axis=0)) / den
    return o, jnp.broadcast_to(m + jnp.log(den), o.shape)


DIL_TILE_ROWS = (1024, 1024, 2048)


def _make_dil_attn(name, d, tile_rows):
    def call(q, k, v, cts=None):
        S = q.shape[0]
        span = DIL_BLOCK * d
        G = max(1, min(tile_rows, S) // span)
        n = S // (G * span)
        at = (lambda i: i) if cts is None else (lambda i: n - 1 - i)
        tile = pl.BlockSpec((G * span, LANES), lambda h, i: (at(i), h))
        before = pl.BlockSpec((span, LANES), lambda h, i: (jnp.maximum(at(i) * G - 1, 0), h))

        def rows(r, j):
            return pl.ds(j * DIL_BLOCK, DIL_BLOCK) if d == 1 else pl.ds(r + j * span, DIL_BLOCK, stride=d)

        def over_residues(fn):
            if d == 1:
                fn(0)
            else:
                lax.fori_loop(0, d, lambda r, c: (fn(r), c)[1], 0)

        def block_inputs(r, j, q_r, kb_r, k_r, vb_r, v_r):
            kp = kb_r[rows(r, 0), :] if j == 0 else k_r[rows(r, j - 1), :]
            vp = vb_r[rows(r, 0), :] if j == 0 else v_r[rows(r, j - 1), :]
            return q_r[rows(r, j), :], kp, k_r[rows(r, j), :], vp, v_r[rows(r, j), :]

        if cts is None:
            def body(q_r, kb_r, k_r, vb_r, v_r, o_r, lse_r):
                first = at(pl.program_id(1)) * G

                def residue(r):
                    for j in range(G):
                        o_r[rows(r, j), :], lse_r[rows(r, j), :] = _dil_block(
                            *block_inputs(r, j, q_r, kb_r, k_r, vb_r, v_r), first + j > 0)
                over_residues(residue)
            return _pcall(body, name=f"{name}_fwd", grid=(DIL_GROUP_HEADS, n), in_specs=[tile, before, tile, before, tile],
                          out_specs=[tile, tile], out_shape=[jax.ShapeDtypeStruct(q.shape, F32)] * 2,
                          compiler_params=_params(("parallel", "parallel")))(q, k, k, v, v)

        def body(q_r, kb_r, k_r, vb_r, v_r, do_r, dl_r, dq_r, dk_r, dv_r, ck_s, cv_s):
            first = at(pl.program_id(1)) * G

            @pl.when(pl.program_id(1) == 0)
            def _():
                ck_s[...] = jnp.zeros_like(ck_s)
                cv_s[...] = jnp.zeros_like(cv_s)

            def residue(r):
                owed = None
                for j in range(G):
                    hp = first + j > 0
                    _, vjp = jax.vjp(lambda *a: _dil_block(*a, hp), *block_inputs(r, j, q_r, kb_r, k_r, vb_r, v_r))
                    dq, dkp, dkc, dvp, dvc = vjp((do_r[rows(r, j), :], dl_r[rows(r, j), :]))
                    dq_r[rows(r, j), :] = dq
                    if j == G - 1:
                        dkc, dvc = dkc + ck_s[rows(r, 0), :], dvc + cv_s[rows(r, 0), :]
                    dk_r[rows(r, j), :], dv_r[rows(r, j), :] = dkc, dvc
                    if j == 0:
                        owed = (dkp, dvp)
                    else:
                        dk_r[rows(r, j - 1), :] += dkp
                        dv_r[rows(r, j - 1), :] += dvp
                ck_s[rows(r, 0), :], cv_s[rows(r, 0), :] = owed
            over_residues(residue)
        return _pcall(body, name=f"{name}_bwd", grid=(DIL_GROUP_HEADS, n), in_specs=[tile, before, tile, before, tile, tile, tile],
                      out_specs=[tile] * 3, out_shape=[jax.ShapeDtypeStruct(q.shape, F32)] * 3,
                      scratch_shapes=[pltpu.VMEM((span, LANES), F32)] * 2,
                      compiler_params=_params(("parallel", "arbitrary")))(q, k, k, v, v, *cts)

    @jax.custom_vjp
    def op(q, k, v):
        return tuple(call(q, k, v))

    op.defvjp(lambda q, k, v: (op(q, k, v), (q, k, v)), lambda res, cts: tuple(call(*res, cts=cts)))
    return op


def _pdot(a, b, dims):
    return lax.dot_general(a, b, (dims, ((), ())), precision=lax.Precision.HIGH, preferred_element_type=F32)


DN_LOCAL_CHUNKS = 4


DN_BLOCK_HEADS = 4
DN_BLOCK = DN_BLOCK_HEADS * DN_CHUNK


def _inverse_cotangent(inv, d):
    return -_pdot(inv, _pdot(d, inv, ((1,), (1,))), ((0,), (0,)))


@jax.custom_vjp
def _unit_lower_inverse(a):
    n = a.shape[0]
    eye = (lax.broadcasted_iota(jnp.int32, (n, n), 0) == lax.broadcasted_iota(jnp.int32, (n, n), 1)).astype(F32)
    inv, pw = eye - a, a
    for _ in range(5):
        pw = _pdot(pw, pw, ((1,), (0,)))
        inv = inv + _pdot(inv, pw, ((1,), (0,)))
    return inv


def _unit_lower_inverse_fwd(a):
    inv = _unit_lower_inverse(a)
    return inv, inv


_unit_lower_inverse.defvjp(_unit_lower_inverse_fwd, lambda inv, d: (_inverse_cotangent(inv, d),))


@jax.custom_vjp
def _known_inverse(a, inv):
    return inv


_known_inverse.defvjp(lambda a, inv: (inv, inv), lambda inv, d: (_inverse_cotangent(inv, d), jnp.zeros_like(inv)))


def _dn_local(q, k, v, g, b, known=None):
    C, R = DN_CHUNK, DN_BLOCK
    r = lax.broadcasted_iota(jnp.int32, (R, R), 0)
    c = lax.broadcasted_iota(jnp.int32, (R, R), 1)
    same_head = (r // C) == (c // C)
    incl, strict = same_head & (r >= c), same_head & (r > c)
    avg = jnp.full((R, LANES), 1.0 / LANES, F32)
    rc = lax.broadcasted_iota(jnp.int32, (C, C), 0) >= lax.broadcasted_iota(jnp.int32, (C, C), 1)
    gc_lanes = _pdot(rc.astype(F32), g, ((1,), (0,)))
    us, ws, qes, kds, qks, invs = [], [], [], [], [], []
    for first in range(0, DN_HEADS, DN_BLOCK_HEADS):
        stack = lambda x: jnp.concatenate(_heads(x)[first:first + DN_BLOCK_HEADS], axis=0)
        unstack = lambda x: [x[p * C:(p + 1) * C] for p in range(DN_BLOCK_HEADS)]
        gc, q_s, k_s, v_s, b_s = (stack(x) for x in (gc_lanes, q, k, v, b))
        gc_j = _pdot(avg, gc, ((1,), (1,)))
        decay = jnp.exp(jnp.where(incl, _cat([gc] * (R // LANES)) - gc_j, NEG))
        kb = k_s * b_s
        kk = _pdot(jnp.concatenate([kb, q_s], axis=0), k_s, ((1,), (1,)))
        a = jnp.where(strict, kk[:R] * decay, 0.0)
        inv = _unit_lower_inverse(a) if known is None else _known_inverse(a, known[len(invs) * R:(len(invs) + 1) * R])
        invs.append(inv)
        eg = jnp.exp(gc)
        uw = _pdot(inv, _cat([v_s * b_s, kb * eg]), ((1,), (0,)))
        g_last = jnp.concatenate([jnp.broadcast_to(x[C - 1:C], (C, LANES)) for x in unstack(gc)], axis=0)
        us += unstack(uw[:, :LANES])
        ws += unstack(uw[:, LANES:])
        qes += unstack(q_s * eg)
        kds += unstack(k_s * jnp.exp(g_last - gc))
        qks.append(kk[R:] * decay)
    egl = jnp.broadcast_to(jnp.exp(gc_lanes[C - 1:C]), (8, DN_HEADS * LANES))
    return _cat(us), _cat(ws), _cat(qes), _cat(kds), jnp.concatenate(qks, axis=0), egl, jnp.concatenate(invs, axis=0)


def _dn_scan(u, w, qe, kd, qk, egl, state):
    C = DN_CHUNK
    heads = [slice(h * LANES, (h + 1) * LANES) for h in range(DN_HEADS)]
    ws = [_pdot(jnp.concatenate([w[:, sl], qe[:, sl]], axis=0), state[sl, :], ((1,), (0,))) for sl in heads]
    v_new = [u[:, sl] - x[:C] for sl, x in zip(heads, ws)]
    local = []
    for i, first in enumerate(range(0, DN_HEADS, DN_BLOCK_HEADS)):
        y = _pdot(qk[i * DN_BLOCK:(i + 1) * DN_BLOCK], jnp.concatenate(v_new[first:first + DN_BLOCK_HEADS], axis=0), ((1,), (0,)))
        local += [y[p * C:(p + 1) * C] for p in range(DN_BLOCK_HEADS)]
    o = _cat([x[C:] + y for x, y in zip(ws, local)])
    states = [state[sl, :] * egl[0:1, sl] + _pdot(kd[:, sl], vn, ((0,), (0,))) for sl, vn in zip(heads, v_new)]
    return o, jnp.concatenate(states, axis=0)


def _make_delta_rule(name):
    W = DN_HEADS * LANES
    QK = DN_HEADS * DN_CHUNK

    def local_call(ins, cts=None):
        S = ins[0].shape[0]
        n = S // DN_CHUNK
        per = math.gcd(DN_LOCAL_CHUNKS, n)
        row = pl.BlockSpec((per * DN_CHUNK, W), lambda i: (i, 0))
        qkb = pl.BlockSpec((per * QK, DN_BLOCK), lambda i: (i, 0))
        eg = pl.BlockSpec((per, 8, W), lambda i: (i, 0, 0))
        rows = lambda j: slice(j * DN_CHUNK, (j + 1) * DN_CHUNK)
        qk_rows = lambda j: slice(j * QK, (j + 1) * QK)
        out_rows = [rows, rows, rows, rows, qk_rows]

        if cts is None:
            def body(*refs):
                for j in range(per):
                    res = _dn_local(*[r[rows(j), :] for r in refs[:5]])
                    for o_r, o, at_ in zip(refs[5:10], res[:5], out_rows):
                        o_r[at_(j), :] = o
                    refs[10][j] = res[5]
                    refs[11][qk_rows(j), :] = res[6]
            blockdiag = jax.ShapeDtypeStruct((n * QK, DN_BLOCK), F32)
            return _pcall(body, name=f"{name}_local_fwd", grid=(n // per,), in_specs=[row] * 5, out_specs=[row] * 4 + [qkb, eg, qkb],
                          out_shape=[jax.ShapeDtypeStruct((S, W), F32)] * 4 + [blockdiag, jax.ShapeDtypeStruct((n, 8, W), F32), blockdiag],
                          compiler_params=_params(("parallel",)))(*ins)

        def body(*refs):
            for j in range(per):
                known = refs[5][qk_rows(j), :]
                _, vjp = jax.vjp(lambda *a: _dn_local(*a, known=known)[:6], *[r[rows(j), :] for r in refs[:5]])
                grads = vjp(tuple(r[at_(j), :] for r, at_ in zip(refs[6:11], out_rows)) + (refs[11][j],))
                for o_r, o in zip(refs[12:], grads):
                    o_r[rows(j), :] = o
        return _pcall(body, name=f"{name}_local_bwd", grid=(n // per,), in_specs=[row] * 5 + [qkb] + [row] * 4 + [qkb, eg],
                      out_specs=[row] * 5, out_shape=[jax.ShapeDtypeStruct((S, W), F32)] * 5,
                      compiler_params=_params(("parallel",)))(*ins, *cts)

    def scan_call(ins, saved=None, do=None):
        S = ins[0].shape[0]
        n = S // DN_CHUNK
        at = (lambda i: i) if do is None else (lambda i: n - 1 - i)
        row = pl.BlockSpec((DN_CHUNK, W), lambda i: (at(i), 0))
        qkb = pl.BlockSpec((QK, DN_BLOCK), lambda i: (at(i), 0))
        eg = pl.BlockSpec((None, 8, W), lambda i: (at(i), 0, 0))
        st = pl.BlockSpec((None, W, LANES), lambda i: (at(i), 0, 0))

        if do is None:
            def body(*refs):
                o_r, st_r, s_s = refs[6:]

                @pl.when(pl.program_id(0) == 0)
                def _():
                    s_s[...] = jnp.zeros_like(s_s)
                st_r[...] = s_s[...]
                o_r[...], s_s[...] = _dn_scan(*[r[...] for r in refs[:6]], s_s[...])
            return _pcall(body, name=f"{name}_scan_fwd", grid=(n,), in_specs=[row] * 4 + [qkb, eg], out_specs=[row, st],
                          out_shape=[jax.ShapeDtypeStruct((S, W), F32), jax.ShapeDtypeStruct((n, W, LANES), F32)],
                          scratch_shapes=[pltpu.VMEM((W, LANES), F32)], compiler_params=_params(("arbitrary",)))(*ins)

        def body(*refs):
            st_r, do_r = refs[6:8]
            outs, ds_s = refs[8:14], refs[14]

            @pl.when(pl.program_id(0) == 0)
            def _():
                ds_s[...] = jnp.zeros_like(ds_s)
            _, vjp = jax.vjp(_dn_scan, *[r[...] for r in refs[:6]], st_r[...])
            *grads, ds = vjp((do_r[...], ds_s[...]))
            for o_r, gval in zip(outs, grads):
                o_r[...] = gval
            ds_s[...] = ds
        return _pcall(body, name=f"{name}_scan_bwd", grid=(n,), in_specs=[row] * 4 + [qkb, eg, st, row], out_specs=[row] * 4 + [qkb, eg],
                      out_shape=[jax.ShapeDtypeStruct((S, W), F32)] * 4
                      + [jax.ShapeDtypeStruct((n * QK, DN_BLOCK), F32), jax.ShapeDtypeStruct((n, 8, W), F32)],
                      scratch_shapes=[pltpu.VMEM((W, LANES), F32)], compiler_params=_params(("arbitrary",)))(*ins, saved, do)

    @jax.custom_vjp
    def local(q, k, v, g, b):
        return tuple(local_call((q, k, v, g, b))[:6])

    def local_fwd(*a):
        *outs, inverses = local_call(a)
        return tuple(outs), (*a, inverses)

    local.defvjp(local_fwd, lambda res, cts: tuple(local_call(res, tuple(cts))))

    @jax.custom_vjp
    def scan(u, w, qe, kd, qk, egl):
        return scan_call((u, w, qe, kd, qk, egl))[0]

    def scan_fwd(*a):
        o, states = scan_call(a)
        return o, (a, states)

    scan.defvjp(scan_fwd, lambda res, do: tuple(scan_call(res[0], res[1], do)))
    return lambda q, k, v, g, b: scan(*local(q, k, v, g, b))


def _loss_call(y, target):
    S, D = y.shape
    t = min(S, 512)
    n = S // t
    row = pl.BlockSpec((t, D), lambda i: (i, 0))

    def body(y_r, t_r, loss_r, dy_r, acc_s):
        i = pl.program_id(0)

        @pl.when(i == 0)
        def _():
            acc_s[...] = jnp.zeros_like(acc_s)
        err = y_r[...] - t_r[...]
        dy_r[...] = err * (1.0 / D)
        acc_s[...] += jnp.sum(err * err, axis=0, keepdims=True)

        @pl.when(i == n - 1)
        def _():
            loss_r[...] = jnp.broadcast_to(jnp.sum(acc_s[...], axis=1, keepdims=True) * (0.5 / D), loss_r.shape)

    return _pcall(body, name="loss_head", grid=(n,), in_specs=[row, row],
                  out_specs=[pl.BlockSpec((8, LANES), lambda i: (0, 0)), row],
                  out_shape=[jax.ShapeDtypeStruct((8, LANES), F32), jax.ShapeDtypeStruct((S, D), F32)],
                  scratch_shapes=[pltpu.VMEM((1, D), F32)], compiler_params=_params(("arbitrary",)))(y, target)


def _adamw_layer(name, part, w, m, v, layer, earlier=None, rows=128):
    L, R, C = w.shape
    t = _tile(R, rows, 8)
    row = pl.BlockSpec((None, t, C), lambda i: (layer, i, 0))

    def body(p_r, w_r, m_r, v_r, *rest):
        g_r, d_r, nm_r, nv_r = rest[-4:]
        g = p_r[0].astype(F32)
        for s in range(1, N_DEV):
            g = g + p_r[s].astype(F32)
        m_new = ADAM_B1 * m_r[...] + (1.0 - ADAM_B1) * g
        v_new = ADAM_B2 * v_r[...] + (1.0 - ADAM_B2) * (g * g)
        m_hat = m_new / (1.0 - ADAM_B1 ** ADAM_STEP)
        v_hat = v_new / (1.0 - ADAM_B2 ** ADAM_STEP)
        g_r[...] = g
        d_r[...] = -ADAM_LR * (m_hat / (jnp.sqrt(v_hat) + ADAM_EPS) + ADAM_WD * w_r[...])
        nm_r[...] = m_new
        nv_r[...] = v_new

    extra = [] if earlier is None else list(earlier)
    return _pcall(body, name=name, grid=(R // t,),
                  in_specs=[pl.BlockSpec((N_DEV, t, C), lambda i: (0, i, 0)), row, row, row] + [pl.BlockSpec(memory_space=pl.ANY)] * len(extra),
                  out_specs=[row] * 4, out_shape=[jax.ShapeDtypeStruct((L, R, C), F32)] * 4,
                  input_output_aliases={4 + k: k for k in range(len(extra))},
                  compiler_params=_params(("parallel",)))(part, w, m, v, *extra)


def _my_place():
    x, y, c = lax.axis_index("x"), lax.axis_index("y"), lax.axis_index("c")
    return x, y, c


def _index(x, y, c):
    return 4 * x + 2 * y + c


def _all_gather(vs):
    n = len(vs)

    def body(*refs):
        v_refs, out_refs = refs[:n], refs[n:2 * n]
        send_sems, recv_sems, local_sems = refs[2 * n:]
        x, y, c = _my_place()
        me, sibling = (x, y, c), (x, y, 1 - c)
        chips = [(1 - x, y), (x, 1 - y), (1 - x, 1 - y)]

        def copy(a, k, block, to, src=None):
            rows = out_refs[a].at[_index(*block)]
            return pltpu.make_async_remote_copy(src_ref=rows if src is None else src, dst_ref=rows, send_sem=send_sems.at[a, k],
                                                recv_sem=recv_sems.at[a, k], device_id=to, device_id_type=MESH)

        mine = [pltpu.make_async_copy(v_refs[a], out_refs[a].at[_index(*me)], local_sems.at[a]) for a in range(n)]
        first, passed = [], []
        for a in range(n):
            mine[a].start()
            first += [copy(a, 0, me, sibling, src=v_refs[a])]
            first += [copy(a, 1 + j, me, (*chip, c), src=v_refs[a]) for j, chip in enumerate(chips)]
        for cp in first:
            cp.start()
        for j, chip in enumerate(chips):
            for a in range(n):
                copy(a, 1 + j, (*chip, c), me).wait_recv()
                passed.append(copy(a, 4 + j, (*chip, c), sibling))
                passed[-1].start()
        for a in range(n):
            copy(a, 0, sibling, me).wait_recv()
            for j, chip in enumerate(chips):
                copy(a, 4 + j, (*chip, 1 - c), me).wait_recv()
        for cp in first + passed:
            cp.wait_send()
        for a in range(n):
            mine[a].wait()

    any_ = pl.BlockSpec(memory_space=pl.ANY)
    return _pcall(body, name="gather_weights", in_specs=[any_] * n, out_specs=[any_] * n,
                  out_shape=[jax.ShapeDtypeStruct((N_DEV,) + v.shape, v.dtype) for v in vs],
                  scratch_shapes=[pltpu.SemaphoreType.DMA((n, 7)), pltpu.SemaphoreType.DMA((n, 7)), pltpu.SemaphoreType.DMA((n,))])(*vs)


def _all_to_all(vs, after):
    n = len(vs)

    def body(*refs):
        v_refs, out_refs = refs[:n], refs[n + 1:2 * n + 1]
        send_sems, recv_sems, local_sems = refs[2 * n + 1:]
        x, y, c = _my_place()
        me = _index(x, y, c)
        mine = [pltpu.make_async_copy(v_refs[a].at[me], out_refs[a].at[me], local_sems.at[a]) for a in range(n)]
        copies = []
        for a in range(n):
            mine[a].start()
        for k in range(1, N_DEV):
            px = 1 - x if k & 4 else x
            py = 1 - y if k & 2 else y
            pc = 1 - c if k & 1 else c
            for a in range(n):
                cp = pltpu.make_async_remote_copy(src_ref=v_refs[a].at[_index(px, py, pc)], dst_ref=out_refs[a].at[me],
                                                  send_sem=send_sems.at[a, k - 1], recv_sem=recv_sems.at[a, k - 1],
                                                  device_id=(px, py, pc), device_id_type=MESH)
                cp.start()
                copies.append(cp)
        for cp in copies:
            cp.wait()
        for a in range(n):
            mine[a].wait()

    any_ = pl.BlockSpec(memory_space=pl.ANY)
    return _pcall(body, name="exchange_vectors", in_specs=[any_] * (n + 1), out_specs=[any_] * n,
                  out_shape=[jax.ShapeDtypeStruct(v.shape, v.dtype) for v in vs],
                  scratch_shapes=[pltpu.SemaphoreType.DMA((n, 7)), pltpu.SemaphoreType.DMA((n, 7)), pltpu.SemaphoreType.DMA((n,))])(*vs, after)


_HBM = pl.BlockSpec(memory_space=pltpu.HBM)
_SEM = pl.BlockSpec(memory_space=pltpu.SEMAPHORE)
_EFFECT = pltpu.SideEffectType.DATAFLOW_SIDE_EFFECTING


def _direct_copies(gather, v_refs, land_refs, send_sems, recv_sems, local_sems):
    x, y, c = _my_place()
    me = _index(x, y, c)
    local, remote = [], []
    for a, (v_ref, land_ref) in enumerate(zip(v_refs, land_refs)):
        local.append(pltpu.make_async_copy(v_ref if gather else v_ref.at[me], land_ref.at[me], local_sems.at[a]))
    for k in range(1, N_DEV):
        px = 1 - x if k & 4 else x
        py = 1 - y if k & 2 else y
        pc = 1 - c if k & 1 else c
        for a, (v_ref, land_ref) in enumerate(zip(v_refs, land_refs)):
            sem = a * (N_DEV - 1) + k - 1
            remote.append(pltpu.make_async_remote_copy(
                src_ref=v_ref if gather else v_ref.at[_index(px, py, pc)], dst_ref=land_ref.at[me], send_sem=send_sems.at[sem],
                recv_sem=recv_sems.at[sem], device_id=(px, py, pc), device_id_type=MESH))
    return local, remote


def _exchange_start(name, vs, gather, thru):
    n = len(vs)
    lands = [lax.empty((N_DEV,) + v.shape if gather else v.shape, v.dtype) for v in vs]

    def body(*refs):
        v_refs, land_refs = refs[:n], refs[n:2 * n]
        send_sems, recv_sems, local_sems = refs[2 * n + 1:2 * n + 4]
        local, remote = _direct_copies(gather, v_refs, land_refs, send_sems, recv_sems, local_sems)
        for cp in local + remote:
            cp.start()

    hbm = lambda a: pltpu.HBM(a.shape, a.dtype)
    res = _pcall(body, name=name,
                 out_shape=(pltpu.SemaphoreType.DMA((n * (N_DEV - 1),)), pltpu.SemaphoreType.DMA((n * (N_DEV - 1),)), pltpu.SemaphoreType.DMA((n,)),
                            *[hbm(a) for a in (*vs, *lands, thru)]),
                 in_specs=[_HBM] * (2 * n + 1), out_specs=(_SEM, _SEM, _SEM, *[_HBM] * (2 * n + 1)),
                 input_output_aliases={i: 3 + i for i in range(2 * n + 1)},
                 compiler_params=pltpu.CompilerParams(has_side_effects=_EFFECT))(
        *[pltpu.with_memory_space_constraint(a, pltpu.HBM) for a in (*vs, *lands, thru)])
    return (gather, res[:3], res[3:3 + n], res[3 + n:3 + 2 * n]), res[3 + 2 * n]


def _exchange_wait(name, started, after):
    gather, sems, vs, lands = started
    n = len(vs)

    def body(*refs):
        v_refs, land_refs = refs[:n], refs[n:2 * n]
        send_sems, recv_sems, local_sems = refs[2 * n:2 * n + 3]
        local, remote = _direct_copies(gather, v_refs, land_refs, send_sems, recv_sems, local_sems)
        for cp in local:
            cp.wait()
        for cp in remote:
            cp.wait_send()
            cp.wait_recv()

    hbm = lambda a: pltpu.HBM(a.shape, a.dtype)
    res = _pcall(body, name=name, out_shape=tuple(hbm(a) for a in (*vs, *lands)),
                 in_specs=[_HBM] * (2 * n) + [_SEM] * 3 + [pl.BlockSpec(memory_space=pl.ANY)], out_specs=tuple([_HBM] * (2 * n)),
                 input_output_aliases={i: i for i in range(2 * n)},
                 compiler_params=pltpu.CompilerParams(has_side_effects=_EFFECT))(*vs, *lands, *sems, after)
    return list(res[n:])


W_IN_SHARD = IN_WIDTH // N_DEV
SEG_ORDER = ("q_lat", "c_kv", "k_pe", "z_a", "dn_qkv", "dn_ab", "z_b", "dil_qkv", "z_c", "gate")
SEG_WIDTH = (384, 256, LANES, 512, 1536, LANES, 512, 4608, 512, 3072)


def _w_in_plan():
    plan = []

    def add(seg, c0, c1, dst):
        while c0 < c1:
            d = c0 // W_IN_SHARD
            e = min(c1, (d + 1) * W_IN_SHARD)
            plan.append((seg, dst, d, c0 - d * W_IN_SHARD, e - c0))
            dst += e - c0
            c0 = e

    half = MLA_ROPE // 2
    for i, name in enumerate(SEG_ORDER):
        if name == "k_pe":
            o = _SEG["k_pe"][0]
            add(i, o, o + half, 0)
            add(i, o + half, o + 2 * half, LANES // 2)
        elif name == "dn_ab":
            o = _SEG["dn_a"][0]
            add(i, o, o + 2 * DN_HEADS, 0)
        else:
            o, w = _SEG[name]
            add(i, o, o + w, 0)
    return plan


def _make_w_in_segments(name):
    plan = _w_in_plan()
    nseg = len(SEG_ORDER)
    t = 256

    def fwd_call(g):
        L = g.shape[1]

        def body(g_ref, *o_refs):
            for i in (SEG_ORDER.index("k_pe"), SEG_ORDER.index("dn_ab")):
                o_refs[i][...] = jnp.zeros_like(o_refs[i])
            for seg, dst, d, src, n in plan:
                o_refs[seg][:, dst:dst + n] = g_ref[d, :, src:src + n]

        return _pcall(body, name=f"{name}_fwd", grid=(L, D_MODEL // t),
                      in_specs=[pl.BlockSpec((N_DEV, None, t, W_IN_SHARD), lambda l, i: (0, l, i, 0))],
                      out_specs=[pl.BlockSpec((None, t, w), lambda l, i: (l, i, 0)) for w in SEG_WIDTH],
                      out_shape=[jax.ShapeDtypeStruct((L, D_MODEL, w), g.dtype) for w in SEG_WIDTH],
                      compiler_params=_params(("parallel", "parallel")))(g)

    def bwd_call(ds):
        L = ds[0].shape[0]

        def body(*refs):
            d_refs, g_ref = refs[:nseg], refs[nseg]
            for seg, dst, d, src, n in plan:
                g_ref[d, :, src:src + n] = d_refs[seg][:, dst:dst + n]

        return _pcall(body, name=f"{name}_bwd", grid=(L, D_MODEL // t),
                      in_specs=[pl.BlockSpec((None, t, w), lambda l, i: (l, i, 0)) for w in SEG_WIDTH],
                      out_specs=pl.BlockSpec((N_DEV, None, t, W_IN_SHARD), lambda l, i: (0, l, i, 0)),
                      out_shape=jax.ShapeDtypeStruct((N_DEV, L, D_MODEL, W_IN_SHARD), ds[0].dtype),
                      compiler_params=_params(("parallel", "parallel")))(*ds)

    @jax.custom_vjp
    def op(g):
        return tuple(fwd_call(g))

    op.defvjp(lambda g: (op(g), None), lambda _, ds: (bwd_call(tuple(ds)),))
    return op


def _pe_pad(a):
    h = MLA_ROPE // 2
    z = jnp.zeros(a.shape[:-1] + (h,), a.dtype)
    return jnp.concatenate([a[..., :h], z, a[..., h:], z], axis=-1)


def _layer_norm(tag, x, norm_g):
    return _make_rowwise(f"{tag}_norm", _f_norm, 512)((x,), (), (norm_g[None, :],), ())[0]


def _layer(tag, x, tables, W):
    h = _layer_norm(tag, x, W["norm_g"])
    return _layer_tail(tag, x, _make_multi_linear(f"{tag}_inproj", 10, INPROJ_DTYPE)(h, W["w_in_segments"]), tables, W)


def _layer_tail(tag, x, segments, tables, W):
    cos_p, sin_p, cos_h, sin_h = tables
    row = lambda a: a[None, :]
    q_lat, c_kv, kpe, z_a, dn_qkv, ab, z_b, dil_qkv, z_c, gl = segments

    qn_lat, ckvn, kp = _make_rowwise(f"{tag}_mla_a", _f_mla_a, 512)(
        (q_lat, c_kv, kpe), (cos_p, sin_p),
        (row(W["mla_q_a_norm_g"]), row(W["mla_kv_a_norm_g"]), row(_pe_pad(W["mla_k_norm_g"][LANES:]))), ())
    wq = W["mla_w_q_b"].reshape(MLA_Q_RANK, MLA_HEADS, MLA_QK)
    wq = jnp.concatenate([wq[:, :, :LANES].reshape(MLA_Q_RANK, -1), _pe_pad(wq[:, :, LANES:]).reshape(MLA_Q_RANK, -1)], axis=1)
    wkv = W["mla_w_kv_b"].reshape(MLA_KV_RANK, MLA_HEADS, 2 * LANES)
    (q8,) = _make_multi_linear(f"{tag}_qb", 1)(qn_lat, (wq,))
    kn_raw, v_mla = _make_multi_linear(f"{tag}_kvb", 2)(
        ckvn, (wkv[:, :, :LANES].reshape(MLA_KV_RANK, -1), wkv[:, :, LANES:].reshape(MLA_KV_RANK, -1)))
    qn, qp, kn = _make_rowwise(f"{tag}_mla_b", _f_mla_b, 512)(
        (q8, kn_raw), (cos_p, sin_p),
        (row(W["mla_q_norm_g"][:LANES]), row(_pe_pad(W["mla_q_norm_g"][LANES:])), row(W["mla_k_norm_g"][:LANES])), ())
    y_a = _make_mla_attn(f"{tag}_mla")(qn, qp, kn, kp, v_mla)

    mixed = _make_conv(f"{tag}_conv")(dn_qkv, W["dn_conv_w"])
    lane_head = jnp.arange(DN_HEADS * LANES) // LANES
    e_a = (jnp.arange(LANES)[:, None] == lane_head[None, :]).astype(F32)
    e_b = (jnp.arange(LANES)[:, None] == lane_head[None, :] + DN_HEADS).astype(F32)
    q_dn, k_dn, v_dn, g_dn, b_dn = _make_rowwise(f"{tag}_dn_pre", _f_dn_pre, 512)(
        (mixed, ab), (), (row(jnp.repeat(W["dn_a_log"], LANES)), row(jnp.repeat(W["dn_dt_bias"], LANES))), (e_a, e_b))
    o_dn = _make_delta_rule(f"{tag}_dn")(q_dn, k_dn, v_dn, g_dn, b_dn)

    qkv_dil = _make_rowwise(f"{tag}_dil_pre", _f_dil_pre, 256)(
        (dil_qkv,), (cos_h, sin_h), (row(W["dil_q_norm_g"]), row(W["dil_k_norm_g"])), ())
    n_groups = len(DIL_DILATIONS)
    o_lse = [_make_dil_attn(f"{tag}_dil{g}", d, DIL_TILE_ROWS[g])(qkv_dil[g], qkv_dil[n_groups + g], qkv_dil[2 * n_groups + g])
             for g, d in enumerate(DIL_DILATIONS)]

    ya, yb, yc = _make_rowwise(f"{tag}_merge_a", _f_merge_a, 256)(
        (y_a, z_a, o_dn, z_b, *[o for o, _ in o_lse], *[l for _, l in o_lse], z_c), (), (row(W["dn_out_norm_g"]),), ())
    (b0,) = _make_multi_linear(f"{tag}_br0", 1)(ya, (W["w_branch"][0],))
    (b1,) = _make_multi_linear(f"{tag}_br1", 1)(yb, (W["w_branch"][1],))
    (b2,) = _make_multi_linear(f"{tag}_br2", 1)(yc, (W["w_branch"][2],))
    (mix,) = _make_rowwise(f"{tag}_merge_b", _f_merge_b, 256)((b0, b1, b2, gl), (), (), ())
    return _make_resid_linear(f"{tag}_out")(x, mix, W["w_out"])


SHARDED = (("w_in", (D_MODEL, W_IN_SHARD)), ("mla_w_q_b", (MLA_Q_RANK, MLA_HEADS * MLA_QK // N_DEV)),
           ("mla_w_kv_b", (MLA_KV_RANK, MLA_HEADS * 2 * LANES // N_DEV)), ("w_branch", (3 * BRANCH_W, D_MODEL // N_DEV)),
           ("w_out", (D_MODEL // N_DEV, D_MODEL)), ("dn_conv_w", (DN_CONV, 3 * DN_HEADS * LANES // N_DEV)))
SMALL = (("norm_g", D_MODEL), ("mla_q_a_norm_g", MLA_Q_RANK), ("mla_kv_a_norm_g", MLA_KV_RANK), ("mla_q_norm_g", MLA_QK),
         ("mla_k_norm_g", MLA_QK), ("dn_a_log", DN_HEADS), ("dn_dt_bias", DN_HEADS), ("dn_out_norm_g", LANES),
         ("dil_q_norm_g", LANES), ("dil_k_norm_g", LANES))
WEIGHTS = ("norm_g", "w_in", "mla_q_a_norm_g", "mla_w_q_b", "mla_kv_a_norm_g", "mla_w_kv_b", "mla_q_norm_g", "mla_k_norm_g",
           "dn_conv_w", "dn_a_log", "dn_dt_bias", "dn_out_norm_g", "dil_q_norm_g", "dil_k_norm_g", "w_branch", "w_out")


def _round_up(n, m):
    return -(-n // m) * m


def _pack_vectors(pieces):
    return jnp.concatenate([jnp.pad(p, (0, _round_up(p.shape[0], LANES) - p.shape[0])) for p in pieces]).reshape(-1, LANES)


def _unpack_vectors(flat, sizes):
    out, off = [], 0
    flat = flat.reshape(-1)
    for n in sizes:
        out.append(flat[off:off + n])
        off += _round_up(n, LANES)
    return out


def _whole_weights(g, small):
    W = dict(small)
    W["mla_w_q_b"] = g["mla_w_q_b"].transpose(1, 0, 2).reshape(MLA_Q_RANK, -1)
    W["mla_w_kv_b"] = g["mla_w_kv_b"].transpose(1, 0, 2).reshape(MLA_KV_RANK, -1)
    W["w_branch"] = g["w_branch"].reshape(N_DEV, 3, BRANCH_W, -1).transpose(1, 2, 0, 3).reshape(3, BRANCH_W, D_MODEL)
    W["w_out"] = g["w_out"].reshape(D_MODEL, D_MODEL)
    W["dn_conv_w"] = g["dn_conv_w"].transpose(1, 0, 2).reshape(DN_CONV, -1)
    return W


def kernel(x, positions, norm_g, w_in, mla_q_a_norm_g, mla_w_q_b, mla_kv_a_norm_g, mla_w_kv_b, mla_q_norm_g, mla_k_norm_g, dn_conv_w, dn_a_log, dn_dt_bias, dn_out_norm_g, dil_q_norm_g, dil_k_norm_g, w_branch, w_out, loss_target, m_norm_g, m_w_in, m_mla_q_a_norm_g, m_mla_w_q_b, m_mla_kv_a_norm_g, m_mla_w_kv_b, m_mla_q_norm_g, m_mla_k_norm_g, m_dn_conv_w, m_dn_a_log, m_dn_dt_bias, m_dn_out_norm_g, m_dil_q_norm_g, m_dil_k_norm_g, m_w_branch, m_w_out, v_norm_g, v_w_in, v_mla_q_a_norm_g, v_mla_w_q_b, v_mla_kv_a_norm_g, v_mla_w_kv_b, v_mla_q_norm_g, v_mla_k_norm_g, v_dn_conv_w, v_dn_a_log, v_dn_dt_bias, v_dn_out_norm_g, v_dil_q_norm_g, v_dil_k_norm_g, v_w_branch, v_w_out):
    w = dict(norm_g=norm_g, w_in=w_in, mla_q_a_norm_g=mla_q_a_norm_g, mla_w_q_b=mla_w_q_b, mla_kv_a_norm_g=mla_kv_a_norm_g,
             mla_w_kv_b=mla_w_kv_b, mla_q_norm_g=mla_q_norm_g, mla_k_norm_g=mla_k_norm_g, dn_conv_w=dn_conv_w, dn_a_log=dn_a_log,
             dn_dt_bias=dn_dt_bias, dn_out_norm_g=dn_out_norm_g, dil_q_norm_g=dil_q_norm_g, dil_k_norm_g=dil_k_norm_g,
             w_branch=w_branch, w_out=w_out)
    m = dict(norm_g=m_norm_g, w_in=m_w_in, mla_q_a_norm_g=m_mla_q_a_norm_g, mla_w_q_b=m_mla_w_q_b, mla_kv_a_norm_g=m_mla_kv_a_norm_g,
             mla_w_kv_b=m_mla_w_kv_b, mla_q_norm_g=m_mla_q_norm_g, mla_k_norm_g=m_mla_k_norm_g, dn_conv_w=m_dn_conv_w,
             dn_a_log=m_dn_a_log, dn_dt_bias=m_dn_dt_bias, dn_out_norm_g=m_dn_out_norm_g, dil_q_norm_g=m_dil_q_norm_g,
             dil_k_norm_g=m_dil_k_norm_g, w_branch=m_w_branch, w_out=m_w_out)
    v = dict(norm_g=v_norm_g, w_in=v_w_in, mla_q_a_norm_g=v_mla_q_a_norm_g, mla_w_q_b=v_mla_w_q_b, mla_kv_a_norm_g=v_mla_kv_a_norm_g,
             mla_w_kv_b=v_mla_w_kv_b, mla_q_norm_g=v_mla_q_norm_g, mla_k_norm_g=v_mla_k_norm_g, dn_conv_w=v_dn_conv_w,
             dn_a_log=v_dn_a_log, dn_dt_bias=v_dn_dt_bias, dn_out_norm_g=v_dn_out_norm_g, dil_q_norm_g=v_dil_q_norm_g,
             dil_k_norm_g=v_dil_k_norm_g, w_branch=v_w_branch, w_out=v_w_out)
    x2, target = x[0], loss_target[0]
    pos = positions[0][:, None]

    names = [n for n, _ in SHARDED]
    view = lambda t, n, s: t[n].reshape((DEPTH,) + s)
    shards = [[(view(w, n, s) if n == "dn_conv_w" else view(w, n, s).astype(BF16))[l] for n, s in SHARDED] for l in range(DEPTH)]
    small = [{n: w[n][l] for n, _ in SMALL} for l in range(DEPTH)]
    (w_in0,) = _all_gather(shards[0][:1])
    gathering0, w_in0 = _exchange_start("gather_layer0_others_start", shards[0][1:], True, w_in0)
    gathering1, w_in0 = _exchange_start("gather_layer1_start", shards[1], True, w_in0)
    tables = _rope_tables(pos, _rope_consts())

    def layer(l, g, small_l, x_l, pending=None):
        tag = f"l{l}"
        seg_op = _make_w_in_segments(f"{tag}_w_in_segments")
        w_segs, vjp_segs = jax.vjp(lambda gw: tuple(s[0] for s in seg_op(gw[:, None])), g["w_in"])
        h, vjp_norm = jax.vjp(lambda x_, ng: _layer_norm(tag, x_, ng), x_l, small_l["norm_g"])
        wide = [i for i in range(len(w_segs)) if i not in INPROJ_NARROW]
        segs = dict(zip(INPROJ_NARROW, _mm_nn_many(f"{tag}_inproj_fwd_narrow", h, [w_segs[i] for i in INPROJ_NARROW], INPROJ_DTYPE)))
        segs.update({i: _mm(f"{tag}_inproj_fwd{i}", h, w_segs[i], "nn", out_dtype=INPROJ_DTYPE, **INPROJ_TILES["nn"]) for i in wide})
        segs = tuple(segs[i] for i in range(len(w_segs)))
        if pending is not None:
            g = dict(g, **dict(zip(names[1:], _exchange_wait(f"gather_{tag}_others_wait", pending, segs[7]))))
        rest_g = {n: a for n, a in g.items() if n != "w_in"}
        rest_s = {n: a for n, a in small_l.items() if n != "norm_g"}
        y, vjp_tail = jax.vjp(lambda sg, x_, gg, ss: _layer_tail(tag, x_, sg, tables, _whole_weights(gg, ss)), segs, x_l, rest_g, rest_s)

        def backward(dy):
            dsegs, dx_skip, d_rest_g, d_rest_s = vjp_tail(dy)
            others, big = _exchange_start(f"exchange_{tag}_others_start", [d_rest_g[n] for n in names if n != "w_in"], False, dsegs[7])
            dsegs = tuple(dsegs[:7]) + (big,) + tuple(dsegs[8:])
            dws = dict(zip(INPROJ_NARROW, _mm_tn_many(f"{tag}_inproj_dw_narrow", h, [dsegs[i] for i in INPROJ_NARROW], w_segs[0].dtype)))
            dws.update({i: _mm(f"{tag}_inproj_dw{i}", h, dsegs[i], "tn", out_dtype=w_segs[i].dtype, **INPROJ_TILES["tn"]) for i in wide})
            dws = tuple(dws[i] for i in range(len(w_segs)))
            projection, first =_exchange_start(f"exchange_{tag}_w_in_start", [vjp_segs(dws)[0]], False, dsegs[0])
            dsegs = (first,) + tuple(dsegs[1:])
            dh = None
            for i, group in enumerate(INPROJ_DH_GROUPS):
                dh = _mm_nt_sum(f"{tag}_inproj_dh{i}", [(dsegs[s], w_segs[s]) for s in group], dh,
                                h.dtype if i == len(INPROJ_DH_GROUPS) - 1 else F32)
            dx_norm, d_norm_g = vjp_norm(dh)
            return (projection, others), dx_skip + dx_norm, dict(d_rest_s, norm_g=d_norm_g)

        def landed(exchanging, after):
            projection, others = (_exchange_wait(f"exchange_{tag}_{k}_wait", e, after) for k, e in zip(("w_in", "others"), exchanging))
            return projection + others
        return y, backward, landed

    y0, backward0, landed0 = layer(0, {"w_in": w_in0}, small[0], x2, gathering0)
    gathered1 = dict(zip(names, _exchange_wait("gather_layer1_wait", gathering1, y0)))
    y1, backward1, landed1 = layer(1, gathered1, small[1], y0)
    loss_splat, dy = _loss_call(y1, target)
    loss = lax.psum(loss_splat[0, 0], ("x", "y", "c"))
    exchanging1, d_y0, g_small1 = backward1(dy)
    exchanging0, g_x, g_small0 = backward0(d_y0)

    state = lambda n, s: (view(w, n, s), view(m, n, s), view(v, n, s))
    parts1 = landed1(exchanging1, g_x)
    updated = {n: _adamw_layer(f"adamw_l1_{n}", parts1[i], *state(n, s), 1) for i, (n, s) in enumerate(SHARDED)}
    g_small = (g_small0, g_small1)
    sizes = [k for _ in range(DEPTH) for _, k in SMALL]
    g_vec = _pack_vectors([g_small[l][n] for l in range(DEPTH) for n, _ in SMALL])
    (parts_vec,) = _all_to_all([jnp.broadcast_to(g_vec[None], (N_DEV,) + g_vec.shape)], updated["w_in"][0])
    parts0 = landed0(exchanging0, parts_vec)

    vec = lambda t: _pack_vectors([t[n][l] for l in range(DEPTH) for n, _ in SMALL])[None]
    outs = {}
    for i, (n, s) in enumerate(SHARDED):
        res = _adamw_layer(f"adamw_l0_{n}", parts0[i], *state(n, s), 0, earlier=updated[n])
        outs[n] = [o.reshape(w[n].shape) for o in res]
    vec_outs = [_unpack_vectors(o, sizes) for o in _adamw_layer("adamw_vectors", parts_vec, vec(w), vec(m), vec(v), 0)]
    for i, (n, _) in enumerate(SMALL):
        outs[n] = [jnp.stack([o[l * len(SMALL) + i] for l in range(DEPTH)]) for o in vec_outs]
    return (loss, g_x[None], *[outs[n][k] for k in range(4) for n in WEIGHTS])
```

```python
import functools
import math

import jax
import jax.numpy as jnp
from jax import lax
from jax.experimental import pallas as pl
from jax.experimental.pallas import tpu as pltpu

F32 = jnp.float32
BF16 = jnp.bfloat16
HI = lax.Precision.HIGHEST
MESH = pl.DeviceIdType.MESH

N_DEV = 8
D_MODEL = 1024
DEPTH = 2
RMS_EPS = 1e-6
ROPE_THETA = 10000.0
LANES = 128
MLA_HEADS = 4
MLA_ROPE = 64
MLA_QK = 192
MLA_Q_RANK = 384
MLA_KV_RANK = 256
DN_HEADS = 4
DN_CHUNK = 64
DN_CONV = 4
DIL_HEADS = 12
DIL_GROUP_HEADS = 4
DIL_DILATIONS = (1, 4, 16)
DIL_BLOCK = 128
BRANCH_W = 512
IN_WIDTH = 11464
NEG = -1e30
VMEM_LIMIT = 56 * 1024 * 1024

ADAM_LR, ADAM_B1, ADAM_B2, ADAM_EPS, ADAM_WD, ADAM_STEP = 0.001, 0.9, 0.999, 1e-08, 0.01, 10

_SEG = {}
_off = 0
for _n, _w in (("q_lat", 384), ("c_kv", 256), ("k_pe", 64), ("z_a", 512), ("dn_qkv", 1536), ("dn_a", 4), ("dn_b", 4),
               ("z_b", 512), ("dil_qkv", 4608), ("z_c", 512), ("gate", 3072)):
    _SEG[_n] = (_off, _w)
    _off += _w
assert _off == IN_WIDTH


def _pcall(body, **kw):
    return pl.pallas_call(body, **kw)


def _params(sem=None):
    return pltpu.CompilerParams(dimension_semantics=sem, vmem_limit_bytes=VMEM_LIMIT)


def _tile(n, target, mult):
    t = (min(n, target) // mult) * mult
    while t >= mult:
        if n % t == 0:
            return t
        t -= mult
    return n


def _mm(name, a, b, mode, out_dtype=F32, acc=None, tm=1024, tn=512, tk=1024):
    if mode == "nn":
        (M, K), (_, N) = a.shape, b.shape
    elif mode == "nt":
        (M, K), (N, _) = a.shape, b.shape
    else:
        (K, M), (_, N) = a.shape, b.shape
    tm, tn, tk = _tile(M, tm, LANES), _tile(N, tn, LANES), _tile(K, tk, LANES)
    nk = K // tk
    dims = {"nn": (((1,), (0,)), ((), ())), "nt": (((1,), (1,)), ((), ())), "tn": (((0,), (0,)), ((), ()))}[mode]
    a_spec = pl.BlockSpec((tk, tm), lambda i, j, k: (k, i)) if mode == "tn" else pl.BlockSpec((tm, tk), lambda i, j, k: (i, k))
    b_spec = pl.BlockSpec((tn, tk), lambda i, j, k: (j, k)) if mode == "nt" else pl.BlockSpec((tk, tn), lambda i, j, k: (k, j))
    o_spec = pl.BlockSpec((tm, tn), lambda i, j, k: (i, j))
    has_acc = acc is not None

    def body(*refs):
        a_ref, b_ref = refs[:2]
        c_ref = refs[2] if has_acc else None
        o_ref = refs[3] if has_acc else refs[2]
        prod = lax.dot_general(a_ref[...].astype(BF16), b_ref[...].astype(BF16), dims, preferred_element_type=F32)
        if nk == 1:
            o_ref[...] = (prod + c_ref[...].astype(F32) if has_acc else prod).astype(out_dtype)
            return
        acc_ref = refs[-1]
        k = pl.program_id(2)

        @pl.when(k == 0)
        def _():
            acc_ref[...] = prod + c_ref[...].astype(F32) if has_acc else prod

        @pl.when(k > 0)
        def _():
            acc_ref[...] += prod

        @pl.when(k == nk - 1)
        def _():
            o_ref[...] = acc_ref[...].astype(out_dtype)

    ins = [a, b] + ([acc] if has_acc else [])
    in_specs = [a_spec, b_spec] + ([o_spec] if has_acc else [])
    return _pcall(body, name=name, grid=(M // tm, N // tn, nk), in_specs=in_specs, out_specs=o_spec,
                  out_shape=jax.ShapeDtypeStruct((M, N), out_dtype), scratch_shapes=[pltpu.VMEM((tm, tn), F32)] if nk > 1 else [],
                  compiler_params=_params(("parallel", "parallel", "arbitrary")))(*ins)


def _mm_nt_sum(name, pairs, acc, out_dtype, tm=512, tn=1024):
    M, N = pairs[0][0].shape[0], pairs[0][1].shape[0]
    tm, tn = _tile(M, tm, LANES), _tile(N, tn, LANES)
    n = len(pairs)
    has_acc = acc is not None
    o_spec = pl.BlockSpec((tm, tn), lambda i, j: (i, j))

    def body(*refs):
        total = refs[2 * n][...].astype(F32) if has_acc else None
        for a_ref, b_ref in zip(refs[:n], refs[n:2 * n]):
            prod = lax.dot_general(a_ref[...].astype(BF16), b_ref[...].astype(BF16), (((1,), (1,)), ((), ())), preferred_element_type=F32)
            total = prod if total is None else total + prod
        refs[-1][...] = total.astype(out_dtype)

    in_specs = ([pl.BlockSpec((tm, a.shape[1]), lambda i, j: (i, 0)) for a, _ in pairs]
                + [pl.BlockSpec((tn, b.shape[1]), lambda i, j: (j, 0)) for _, b in pairs] + ([o_spec] if has_acc else []))
    return _pcall(body, name=name, grid=(M // tm, N // tn), in_specs=in_specs, out_specs=o_spec,
                  out_shape=jax.ShapeDtypeStruct((M, N), out_dtype), compiler_params=_params(("parallel", "parallel")))(
        *[a for a, _ in pairs], *[b for _, b in pairs], *([acc] if has_acc else []))


def _mm_nn_many(name, a, bs, out_dtype, tm=1024):
    M, K = a.shape
    tm = _tile(M, tm, LANES)
    n = len(bs)

    def body(*refs):
        a_t = refs[0][...].astype(BF16)
        for b_ref, o_ref in zip(refs[1:1 + n], refs[1 + n:]):
            o_ref[...] = jnp.dot(a_t, b_ref[...].astype(BF16), preferred_element_type=F32).astype(out_dtype)

    return _pcall(body, name=name, grid=(M // tm,),
                  in_specs=[pl.BlockSpec((tm, K), lambda i: (i, 0))] + [pl.BlockSpec(b.shape, lambda i: (0, 0)) for b in bs],
                  out_specs=[pl.BlockSpec((tm, b.shape[1]), lambda i: (i, 0)) for b in bs],
                  out_shape=[jax.ShapeDtypeStruct((M, b.shape[1]), out_dtype) for b in bs],
                  compiler_params=_params(("parallel",)))(a, *bs)


def _mm_tn_many(name, a, bs, out_dtype, tk=1024):
    K, M = a.shape
    tk = _tile(K, tk, LANES)
    nk, n = K // tk, len(bs)

    def body(*refs):
        a_t = refs[0][...].astype(BF16)
        k = pl.program_id(0)
        for b_ref, o_ref, acc in zip(refs[1:1 + n], refs[1 + n:1 + 2 * n], refs[1 + 2 * n:]):
            prod = lax.dot_general(a_t, b_ref[...].astype(BF16), (((0,), (0,)), ((), ())), preferred_element_type=F32)

            @pl.when(k == 0)
            def _(acc=acc, prod=prod):
                acc[...] = prod

            @pl.when(k > 0)
            def _(acc=acc, prod=prod):
                acc[...] += prod

            @pl.when(k == nk - 1)
            def _(acc=acc, o_ref=o_ref):
                o_ref[...] = acc[...].astype(out_dtype)

    return _pcall(body, name=name, grid=(nk,),
                  in_specs=[pl.BlockSpec((tk, M), lambda k: (k, 0))] + [pl.BlockSpec((tk, b.shape[1]), lambda k: (k, 0)) for b in bs],
                  out_specs=[pl.BlockSpec((M, b.shape[1]), lambda k: (0, 0)) for b in bs],
                  out_shape=[jax.ShapeDtypeStruct((M, b.shape[1]), out_dtype) for b in bs],
                  scratch_shapes=[pltpu.VMEM((M, b.shape[1]), F32) for b in bs],
                  compiler_params=_params(("arbitrary",)))(a, *bs)


INPROJ_DH_GROUPS = ((0, 1, 2, 3, 5, 6, 8), (4,), (9,), (7,))
INPROJ_NARROW = INPROJ_DH_GROUPS[0]

INPROJ_DTYPE = BF16

INPROJ_TILES = {"nn": dict(tm=1024, tn=1536, tk=1024), "tn": dict(tm=1024, tn=768, tk=2048)}


def _make_multi_linear(name, n, out_dtype=F32):
    @jax.custom_vjp
    def op(h, ws):
        return tuple(_mm(f"{name}_fwd{i}", h, w, "nn", out_dtype=out_dtype) for i, w in enumerate(ws))

    def fwd(h, ws):
        return op(h, ws), (h, ws)

    def bwd(res, douts):
        h, ws = res
        dh = None
        for i, (w, d) in enumerate(zip(ws, douts)):
            dh = _mm(f"{name}_dh{i}", d, w, "nt", acc=dh, out_dtype=h.dtype if i == len(ws) - 1 else F32)
        dws = tuple(_mm(f"{name}_dw{i}", h, d, "tn", out_dtype=w.dtype) for i, (w, d) in enumerate(zip(ws, douts)))
        return dh, dws

    op.defvjp(fwd, bwd)
    return op


def _make_resid_linear(name):
    @jax.custom_vjp
    def op(x, a, w):
        return _mm(f"{name}_fwd", a, w, "nn", acc=x)

    def fwd(x, a, w):
        return op(x, a, w), (a, w)

    def bwd(res, dy):
        a, w = res
        return dy, _mm(f"{name}_da", dy, w, "nt", out_dtype=a.dtype), _mm(f"{name}_dw", a, dy, "tn", out_dtype=w.dtype)

    op.defvjp(fwd, bwd)
    return op


def _make_rowwise(name, f, tile):
    def specs(rows, aux, params, consts, t):
        row = [pl.BlockSpec((t, a.shape[1]), lambda i: (i, 0)) for a in (*rows, *aux)]
        full = [pl.BlockSpec(p.shape, lambda i: (0, 0)) for p in (*params, *consts)]
        return row, full

    def fwd_call(rows, aux, params, consts):
        S = rows[0].shape[0]
        t = min(tile, S)
        n_in = len(rows) + len(aux) + len(params) + len(consts)
        shp = lambda a: jax.ShapeDtypeStruct((t, a.shape[1]), a.dtype)
        outs = jax.eval_shape(f, *[shp(a) for a in (*rows, *aux)], *params, *consts)
        row_specs, full_specs = specs(rows, aux, params, consts, t)

        def body(*refs):
            res = f(*[r[...] for r in refs[:n_in]])
            for o_ref, o in zip(refs[n_in:], res):
                o_ref[...] = o

        return _pcall(body, name=f"{name}_fwd", grid=(S // t,), in_specs=row_specs + full_specs,
                      out_specs=[pl.BlockSpec((t, o.shape[1]), lambda i: (i, 0)) for o in outs],
                      out_shape=[jax.ShapeDtypeStruct((S, o.shape[1]), o.dtype) for o in outs],
                      compiler_params=_params(("parallel",)))(*rows, *aux, *params, *consts)

    def bwd_call(rows, aux, params, consts, douts):
        S = rows[0].shape[0]
        t = min(tile, S)
        nr, na, npar, nc, nd = len(rows), len(aux), len(params), len(consts), len(douts)
        row_specs, full_specs = specs(rows, aux, params, consts, t)

        def body(*refs):
            vals = [r[...] for r in refs[:nr + na + npar + nc]]
            rv, av = vals[:nr], vals[nr:nr + na]
            pv, cv = vals[nr + na:nr + na + npar], vals[nr + na + npar:]
            dv = tuple(r[...] for r in refs[nr + na + npar + nc:nr + na + npar + nc + nd])
            out_refs = refs[nr + na + npar + nc + nd:]
            _, vjp = jax.vjp(lambda *rp: f(*rp[:nr], *av, *rp[nr:], *cv), *rv, *pv)
            grads = vjp(dv)
            for o_ref, g in zip(out_refs[:nr], grads[:nr]):
                o_ref[...] = g
            first = pl.program_id(0) == 0
            for o_ref, g in zip(out_refs[nr:], grads[nr:]):
                @pl.when(first)
                def _(o_ref=o_ref):
                    o_ref[...] = jnp.zeros_like(o_ref)
                o_ref[...] += g

        res = _pcall(body, name=f"{name}_bwd", grid=(S // t,),
                     in_specs=row_specs + full_specs + [pl.BlockSpec((t, d.shape[1]), lambda i: (i, 0)) for d in douts],
                     out_specs=[pl.BlockSpec((t, a.shape[1]), lambda i: (i, 0)) for a in rows]
                     + [pl.BlockSpec(p.shape, lambda i: (0, 0)) for p in params],
                     out_shape=[jax.ShapeDtypeStruct(a.shape, a.dtype) for a in (*rows, *params)],
                     compiler_params=_params(("arbitrary",)))(*rows, *aux, *params, *consts, *douts)
        return tuple(res[:nr]), tuple(res[nr:])

    @jax.custom_vjp
    def op(rows, aux, params, consts):
        return tuple(fwd_call(rows, aux, params, consts))

    def fwd(rows, aux, params, consts):
        return op(rows, aux, params, consts), (rows, aux, params, consts)

    def bwd(res, douts):
        rows, aux, params, consts = res
        drows, dparams = bwd_call(rows, aux, params, consts, tuple(douts))
        zeros = lambda xs: tuple(jnp.zeros_like(a) for a in xs)
        return drows, zeros(aux), dparams, zeros(consts)

    op.defvjp(fwd, bwd)
    return op


@jax.custom_vjp
def _swap_halves(x):
    return pltpu.roll(x, LANES // 2, 1)


_swap_halves.defvjp(lambda x: (_swap_halves(x), None), lambda _, g: (_swap_halves(g),))


def _rope(x, cos_t, sin_t):
    return x * cos_t + _swap_halves(x) * sin_t


def _rms(x, g, n=None):
    n = x.shape[-1] if n is None else n
    return x * lax.rsqrt(jnp.sum(x * x, axis=-1, keepdims=True) * (1.0 / n) + RMS_EPS) * g


def _heads(x):
    return [x[:, i * LANES:(i + 1) * LANES] for i in range(x.shape[1] // LANES)]


def _cat(xs):
    return jnp.concatenate(xs, axis=1)


def _silu(x):
    return x * jax.nn.sigmoid(x)


def _f_norm(x, g):
    return (_rms(x, g).astype(BF16),)


def _f_mla_a(q_lat, c_kv, kpe, cos_p, sin_p, qa_g, kva_g, kpe_g):
    q_lat, c_kv, kpe = (t.astype(F32) for t in (q_lat, c_kv, kpe))
    kp =_rope(_rms(kpe, kpe_g, MLA_ROPE), cos_p, sin_p)
    return _rms(q_lat, qa_g).astype(BF16), _rms(c_kv, kva_g).astype(BF16), _cat([kp] * MLA_HEADS)


def _f_mla_b(q8, kn_raw, cos_p, sin_p, qn_g, qp_g, kn_g):
    hs = _heads(q8)
    qn = _cat([_rms(h, qn_g) for h in hs[:MLA_HEADS]])
    qp = _cat([_rope(_rms(h, qp_g, MLA_ROPE), cos_p, sin_p) for h in hs[MLA_HEADS:]])
    kn = _cat([_rms(h, kn_g) for h in _heads(kn_raw)])
    return qn, qp, kn


def _softplus(x):
    return jnp.maximum(x, 0.0) + jnp.log(1.0 + jnp.exp(-jnp.abs(x)))


def _l2n(x):
    return x * lax.rsqrt(jnp.sum(x * x, axis=-1, keepdims=True) + 1e-6)


def _f_dn_pre(mixed, ab, alog_f, dtb_f, e_a, e_b):
    hs = _heads(mixed)
    q = _cat([_l2n(h) * (LANES ** -0.5) for h in hs[:DN_HEADS]])
    k = _cat([_l2n(h) for h in hs[DN_HEADS:2 * DN_HEADS]])
    v = _cat(hs[2 * DN_HEADS:])
    ab = ab.astype(F32)
    a_f = jnp.dot(ab, e_a, precision=HI, preferred_element_type=F32)
    b_f = jnp.dot(ab, e_b, precision=HI, preferred_element_type=F32)
    g = -jnp.exp(alog_f) * _softplus(a_f + dtb_f)
    return q, k, v, g, jax.nn.sigmoid(b_f)


def _f_dil_pre(qkv, cos_h, sin_h, q_g, k_g):
    hs = [h.astype(F32) for h in _heads(qkv)]
    q = [_rope(_rms(h, q_g), cos_h, sin_h) for h in hs[:DIL_HEADS]]
    k = [_rope(_rms(h, k_g), cos_h, sin_h) for h in hs[DIL_HEADS:2 * DIL_HEADS]]
    v = hs[2 * DIL_HEADS:]
    group = lambda xs, g: _cat(xs[g * DIL_GROUP_HEADS:(g + 1) * DIL_GROUP_HEADS])
    return tuple(group(xs, g) for xs in (q, k, v) for g in range(len(DIL_DILATIONS)))


def _f_merge_a(y_a, z_a, o_dn, z_b, o0, o1, o2, l0, l1, l2, z_c, out_g):
    z_a, z_b, z_c = (z.astype(F32) for z in (z_a, z_b, z_c))
    y_b = _cat([_rms(h, out_g) for h in _heads(o_dn)])
    os_, ls = [_heads(o) for o in (o0, o1, o2)], [_heads(l) for l in (l0, l1, l2)]
    y_c = []
    for j in range(DIL_GROUP_HEADS):
        l3 = [ls[g][j] for g in range(3)]
        m = jnp.maximum(jnp.maximum(l3[0], l3[1]), l3[2])
        e3 = [jnp.exp(l - m) for l in l3]
        den = e3[0] + e3[1] + e3[2]
        y_c.append(sum(e3[g] * os_[g][j] for g in range(3)) / den)
    return tuple(t.astype(BF16) for t in (y_a * _silu(z_a), y_b * _silu(z_b), _cat(y_c) * _silu(z_c)))


def _f_merge_b(b0, b1, b2, gl):
    gs = [jax.nn.sigmoid(gl[:, i * D_MODEL:(i + 1) * D_MODEL].astype(F32)) for i in range(3)]
    return ((gs[0] * b0 + gs[1] * b1 + gs[2] * b2).astype(BF16),)


def _rope_tables(pos, inv_sign):
    S = pos.shape[0]
    t = min(S, 1024)

    def body(p_ref, c_ref, cp, sp, ch, sh):
        p = p_ref[...].astype(F32)
        c = c_ref[...]
        ang_p, ang_h = p * c[0:1], p * c[2:3]
        cp[...] = jnp.cos(ang_p) * jnp.abs(c[1:2])
        sp[...] = jnp.sin(ang_p) * c[1:2]
        ch[...] = jnp.cos(ang_h)
        sh[...] = jnp.sin(ang_h) * c[3:4]

    row = pl.BlockSpec((t, LANES), lambda i: (i, 0))
    return _pcall(body, name="rope_tables", grid=(S // t,),
                  in_specs=[pl.BlockSpec((t, 1), lambda i: (i, 0)), pl.BlockSpec((4, LANES), lambda i: (0, 0))],
                  out_specs=[row] * 4, out_shape=[jax.ShapeDtypeStruct((S, LANES), F32)] * 4,
                  compiler_params=_params(("parallel",)))(pos, inv_sign)


def _rope_consts():
    half_p, half_h = MLA_ROPE // 2, LANES // 2
    inv_p = 1.0 / (ROPE_THETA ** (jnp.arange(0, MLA_ROPE, 2, dtype=F32) / MLA_ROPE))
    inv_h = 1.0 / (ROPE_THETA ** (jnp.arange(0, LANES, 2, dtype=F32) / LANES))
    z = jnp.zeros((half_p,), F32)
    o = jnp.ones((half_p,), F32)
    return jnp.stack([jnp.concatenate([inv_p, z, inv_p, z]), jnp.concatenate([-o, z, o, z]),
                      jnp.concatenate([inv_h, inv_h]), jnp.concatenate([-jnp.ones((half_h,), F32), jnp.ones((half_h,), F32)])])


def _shift_rows(x, s, up):
    n = x.shape[0]
    r = lax.broadcasted_iota(jnp.int32, x.shape, 0)
    if up:
        return jnp.where(r < n - s, pltpu.roll(x, n - s, 0), 0.0)
    return jnp.where(r >= s, pltpu.roll(x, s, 0), 0.0)


def _make_shift(s):
    @jax.custom_vjp
    def sh(x):
        return _shift_rows(x, s, False)

    sh.defvjp(lambda x: (sh(x), None), lambda _, g: (_shift_rows(g, s, True),))
    return sh


def _f_conv(x, w):
    x = x.astype(F32)
    y = x * w[DN_CONV - 1:DN_CONV]
    for j in range(DN_CONV - 1):
        y = y + _make_shift(DN_CONV - 1 - j)(x) * w[j:j + 1]
    return _silu(y)


def _make_conv(name):
    def call(x, w, dy=None):
        S, C = x.shape
        col = pl.BlockSpec((S, LANES), lambda i: (0, i))
        wsp = pl.BlockSpec((DN_CONV, LANES), lambda i: (0, i))
        if dy is None:
            def body(x_ref, w_ref, o_ref):
                o_ref[...] = _f_conv(x_ref[...], w_ref[...])
            return _pcall(body, name=f"{name}_fwd", grid=(C // LANES,), in_specs=[col, wsp], out_specs=col,
                          out_shape=jax.ShapeDtypeStruct(x.shape, F32), compiler_params=_params(("parallel",)))(x, w)

        def body(x_ref, w_ref, dy_ref, dx_ref, dw_ref):
            _, vjp = jax.vjp(_f_conv, x_ref[...], w_ref[...])
            dx_ref[...], dw_ref[...] = vjp(dy_ref[...])
        return _pcall(body, name=f"{name}_bwd", grid=(C // LANES,), in_specs=[col, wsp, col], out_specs=[col, wsp],
                      out_shape=[jax.ShapeDtypeStruct(x.shape, x.dtype), jax.ShapeDtypeStruct(w.shape, F32)],
                      compiler_params=_params(("parallel",)))(x, w, dy)

    @jax.custom_vjp
    def op(x, w):
        return call(x, w)

    op.defvjp(lambda x, w: (op(x, w), (x, w)), lambda res, dy: tuple(call(*res, dy)))
    return op


def _dot_nt(a, b):
    return lax.dot_general(a.astype(BF16), b.astype(BF16), (((1,), (1,)), ((), ())), preferred_element_type=F32)


def _dot_nn(a, b):
    return jnp.dot(a.astype(BF16), b.astype(BF16), preferred_element_type=F32)


def _dot_tn(a, b):
    return lax.dot_general(a.astype(BF16), b.astype(BF16), (((0,), (0,)), ((), ())), preferred_element_type=F32)


def _mla_scores(qn_r, qp_r, kn_r, kp_r, diagonal):
    scale = MLA_QK ** -0.5
    s = _dot_nt(qn_r[...] * scale, kn_r[...]) + _dot_nt(qp_r[...] * scale, kp_r[...])
    if diagonal:
        r = lax.broadcasted_iota(jnp.int32, s.shape, 0)
        c = lax.broadcasted_iota(jnp.int32, s.shape, 1)
        s = jnp.where(c <= r, s, NEG)
    return s


def _on_causal_pairs(qi, ki, step):
    @pl.when(ki < qi)
    def _():
        step(False)

    @pl.when(ki == qi)
    def _():
        step(True)


def _causal_pairs(n, t, by_key):
    pairs = [(q, k) for k in range(n) for q in range(k, n)] if by_key else [(q, k) for q in range(n) for k in range(q + 1)]
    qt, kt = (jnp.array([p[i] for p in pairs], jnp.int32) for i in (0, 1))
    return (qt, kt, pl.BlockSpec((t, LANES), lambda h, p, qt_r, kt_r: (qt_r[p], h)),
            pl.BlockSpec((t, LANES), lambda h, p, qt_r, kt_r: (kt_r[p], h)))


def _make_mla_attn(name):
    scale = MLA_QK ** -0.5

    def fwd_call(qn, qp, kn, kp, v):
        S = qn.shape[0]
        t = min(S, 512)
        n = S // t
        qt, kt, qs, ks = _causal_pairs(n, t, by_key=False)

        def body(qt_r, kt_r, qn_r, qp_r, kn_r, kp_r, v_r, o_r, lse_r, m_s, l_s, acc_s):
            qi, ki = qt_r[pl.program_id(1)], kt_r[pl.program_id(1)]

            @pl.when(ki == 0)
            def _():
                m_s[...] = jnp.full_like(m_s, NEG)
                l_s[...] = jnp.zeros_like(l_s)
                acc_s[...] = jnp.zeros_like(acc_s)

            def step(diagonal):
                s = _mla_scores(qn_r, qp_r, kn_r, kp_r, diagonal)
                m_old = m_s[...]
                m_new = jnp.maximum(m_old, jnp.max(s, axis=-1, keepdims=True))
                p = jnp.exp(s - m_new[:, :1])
                alpha = jnp.exp(m_old - m_new)
                l_s[...] = alpha * l_s[...] + jnp.sum(p, axis=-1, keepdims=True)
                acc_s[...] = alpha * acc_s[...] + _dot_nn(p, v_r[...])
                m_s[...] = m_new
            _on_causal_pairs(qi, ki, step)

            @pl.when(ki == qi)
            def _():
                o_r[...] = acc_s[...] / l_s[...]
                lse_r[...] = m_s[...] + jnp.log(l_s[...])

        spec = pltpu.PrefetchScalarGridSpec(num_scalar_prefetch=2, grid=(MLA_HEADS, qt.shape[0]), in_specs=[qs, qs, ks, ks, ks],
                                            out_specs=[qs, qs], scratch_shapes=[pltpu.VMEM((t, LANES), F32)] * 3)
        return _pcall(body, name=f"{name}_fwd", grid_spec=spec, out_shape=[jax.ShapeDtypeStruct((S, MLA_HEADS * LANES), F32)] * 2,
                      compiler_params=_params(("parallel", "arbitrary")))(qt, kt, qn, qp, kn, kp, v)

    def bwd_call(qn, qp, kn, kp, v, o, lse, do):
        S = qn.shape[0]
        t = min(S, 512)
        n = S // t
        qt, kt, qs, ks = _causal_pairs(n, t, by_key=True)
        head = pl.BlockSpec((S, LANES), lambda h, p, qt_r, kt_r: (0, h))
        n_pairs = qt.shape[0]

        def body(qt_r, kt_r, qn_r, qp_r, kn_r, kp_r, v_r, o_r, lse_r, do_r, dqn_r, dqp_r, dkn_r, dkp_r, dv_r, dkn_s, dkp_s, dv_s, dl_s):
            pair = pl.program_id(1)
            qi, ki = qt_r[pair], kt_r[pair]
            rows = pl.ds(pl.multiple_of(qi * t, t), t)

            @pl.when(pair == 0)
            def _():
                dqn_r[...] = jnp.zeros_like(dqn_r)
                dqp_r[...] = jnp.zeros_like(dqp_r)

            @pl.when(ki == 0)
            def _():
                dl_s[rows, :] = jnp.broadcast_to(jnp.sum(do_r[...] * o_r[...], axis=-1, keepdims=True), (t, LANES))

            @pl.when(qi == ki)
            def _():
                dkn_s[...] = jnp.zeros_like(dkn_s)
                dkp_s[...] = jnp.zeros_like(dkp_s)
                dv_s[...] = jnp.zeros_like(dv_s)

            def step(diagonal):
                p = jnp.exp(_mla_scores(qn_r, qp_r, kn_r, kp_r, diagonal) - lse_r[...][:, :1])
                ds = p * (_dot_nt(do_r[...], v_r[...]) - dl_s[rows, :][:, :1])
                dv_s[...] += _dot_tn(p, do_r[...])
                dkn_s[...] += _dot_tn(ds, qn_r[...] * scale)
                dkp_s[...] += _dot_tn(ds, qp_r[...] * scale)
                dqn_r[rows, :] += _dot_nn(ds, kn_r[...])
                dqp_r[rows, :] += _dot_nn(ds, kp_r[...])
            _on_causal_pairs(qi, ki, step)

            @pl.when(qi == n - 1)
            def _():
                dkn_r[...] = dkn_s[...]
                dkp_r[...] = dkp_s[...]
                dv_r[...] = dv_s[...]

            @pl.when(pair == n_pairs - 1)
            def _():
                dqn_r[...] = dqn_r[...] * scale
                dqp_r[...] = dqp_r[...] * scale

        spec = pltpu.PrefetchScalarGridSpec(num_scalar_prefetch=2, grid=(MLA_HEADS, n_pairs), in_specs=[qs, qs, ks, ks, ks, qs, qs, qs],
                                            out_specs=[head, head, ks, ks, ks],
                                            scratch_shapes=[pltpu.VMEM((t, LANES), F32)] * 3 + [pltpu.VMEM((S, LANES), F32)])
        return _pcall(body, name=f"{name}_bwd", grid_spec=spec, out_shape=[jax.ShapeDtypeStruct((S, MLA_HEADS * LANES), F32)] * 5,
                      compiler_params=_params(("parallel", "arbitrary")))(qt, kt, qn, qp, kn, kp, v, o, lse, do)

    @jax.custom_vjp
    def op(qn, qp, kn, kp, v):
        return fwd_call(qn, qp, kn, kp, v)[0]

    def fwd(qn, qp, kn, kp, v):
        o, lse = fwd_call(qn, qp, kn, kp, v)
        return o, (qn, qp, kn, kp, v, o, lse)

    def bwd(res, do):
        return tuple(bwd_call(*res, do))

    op.defvjp(fwd, bwd)
    return op


def _dil_block(q, kp, kc, vp, vc, has_prev):
    scale = LANES ** -0.5
    r = lax.broadcasted_iota(jnp.int32, (DIL_BLOCK, 2 * DIL_BLOCK), 0)
    c = lax.broadcasted_iota(jnp.int32, (DIL_BLOCK, 2 * DIL_BLOCK), 1)
    valid = ((c < DIL_BLOCK) & (c >= r) & has_prev) | ((c >= DIL_BLOCK) & (c - DIL_BLOCK <= r))
    s = jnp.where(valid, _dot_nt(q * scale, jnp.concatenate([kp, kc], axis=0)), NEG)
    m = jnp.max(s, axis=-1, keepdims=True)
    e = jnp.exp(s - m)
    den = jnp.sum(e, axis=-1, keepdims=True)
    o = _dot_nn(e, jnp.concatenate([vp, vc], axis=0)) / den
    return o, jnp.broadcast_to(m + jnp.log(den), o.shape)


DIL_TILE_ROWS = (1024, 1024, 2048)


def _make_dil_attn(name, d, tile_rows):
    def call(q, k, v, cts=None):
        S = q.shape[0]
        span = DIL_BLOCK * d
        G = max(1, min(tile_rows, S) // span)
        n = S // (G * span)
        at = (lambda i: i) if cts is None else (lambda i: n - 1 - i)
        tile = pl.BlockSpec((G * span, LANES), lambda h, i: (at(i), h))
        before = pl.BlockSpec((span, LANES), lambda h, i: (jnp.maximum(at(i) * G - 1, 0), h))

        def rows(r, j):
            return pl.ds(j * DIL_BLOCK, DIL_BLOCK) if d == 1 else pl.ds(r + j * span, DIL_BLOCK, stride=d)

        def over_residues(fn):
            if d == 1:
                fn(0)
            else:
                lax.fori_loop(0, d, lambda r, c: (fn(r), c)[1], 0)

        def block_inputs(r, j, q_r, kb_r, k_r, vb_r, v_r):
            kp = kb_r[rows(r, 0), :] if j == 0 else k_r[rows(r, j - 1), :]
            vp = vb_r[rows(r, 0), :] if j == 0 else v_r[rows(r, j - 1), :]
            return q_r[rows(r, j), :], kp, k_r[rows(r, j), :], vp, v_r[rows(r, j), :]

        if cts is None:
            def body(q_r, kb_r, k_r, vb_r, v_r, o_r, lse_r):
                first = at(pl.program_id(1)) * G

                def residue(r):
                    for j in range(G):
                        o_r[rows(r, j), :], lse_r[rows(r, j), :] = _dil_block(
                            *block_inputs(r, j, q_r, kb_r, k_r, vb_r, v_r), first + j > 0)
                over_residues(residue)
            return _pcall(body, name=f"{name}_fwd", grid=(DIL_GROUP_HEADS, n), in_specs=[tile, before, tile, before, tile],
                          out_specs=[tile, tile], out_shape=[jax.ShapeDtypeStruct(q.shape, F32)] * 2,
                          compiler_params=_params(("parallel", "parallel")))(q, k, k, v, v)

        def body(q_r, kb_r, k_r, vb_r, v_r, do_r, dl_r, dq_r, dk_r, dv_r, ck_s, cv_s):
            first = at(pl.program_id(1)) * G

            @pl.when(pl.program_id(1) == 0)
            def _():
                ck_s[...] = jnp.zeros_like(ck_s)
                cv_s[...] = jnp.zeros_like(cv_s)

            def residue(r):
                owed = None
                for j in range(G):
                    hp = first + j > 0
                    _, vjp = jax.vjp(lambda *a: _dil_block(*a, hp), *block_inputs(r, j, q_r, kb_r, k_r, vb_r, v_r))
                    dq, dkp, dkc, dvp, dvc = vjp((do_r[rows(r, j), :], dl_r[rows(r, j), :]))
                    dq_r[rows(r, j), :] = dq
                    if j == G - 1:
                        dkc, dvc = dkc + ck_s[rows(r, 0), :], dvc + cv_s[rows(r, 0), :]
                    dk_r[rows(r, j), :], dv_r[rows(r, j), :] = dkc, dvc
                    if j == 0:
                        owed = (dkp, dvp)
                    else:
                        dk_r[rows(r, j - 1), :] += dkp
                        dv_r[rows(r, j - 1), :] += dvp
                ck_s[rows(r, 0), :], cv_s[rows(r, 0), :] = owed
            over_residues(residue)
        return _pcall(body, name=f"{name}_bwd", grid=(DIL_GROUP_HEADS, n), in_specs=[tile, before, tile, before, tile, tile, tile],
                      out_specs=[tile] * 3, out_shape=[jax.ShapeDtypeStruct(q.shape, F32)] * 3,
                      scratch_shapes=[pltpu.VMEM((span, LANES), F32)] * 2,
                      compiler_params=_params(("parallel", "arbitrary")))(q, k, k, v, v, *cts)

    @jax.custom_vjp
    def op(q, k, v):
        return tuple(call(q, k, v))

    op.defvjp(lambda q, k, v: (op(q, k, v), (q, k, v)), lambda res, cts: tuple(call(*res, cts=cts)))
    return op


def _pdot(a, b, dims):
    return lax.dot_general(a, b, (dims, ((), ())), precision=lax.Precision.HIGH, preferred_element_type=F32)


DN_LOCAL_CHUNKS = 4


DN_BLOCK_HEADS = 4
DN_BLOCK = DN_BLOCK_HEADS * DN_CHUNK


def _inverse_cotangent(inv, d):
    return -_pdot(inv, _pdot(d, inv, ((1,), (1,))), ((0,), (0,)))


@jax.custom_vjp
def _unit_lower_inverse(a):
    n = a.shape[0]
    eye = (lax.broadcasted_iota(jnp.int32, (n, n), 0) == lax.broadcasted_iota(jnp.int32, (n, n), 1)).astype(F32)
    inv, pw = eye - a, a
    for level in range(5):
        dot = _pdot if level < 2 else (lambda x, y, dims: lax.dot_general(x, y, (dims, ((), ())), preferred_element_type=F32))
        pw = dot(pw, pw, ((1,), (0,)))
        inv = inv + dot(inv, pw, ((1,), (0,)))
    return inv


def _unit_lower_inverse_fwd(a):
    inv = _unit_lower_inverse(a)
    return inv, inv


_unit_lower_inverse.defvjp(_unit_lower_inverse_fwd, lambda inv, d: (_inverse_cotangent(inv, d),))


@jax.custom_vjp
def _known_inverse(a, inv):
    return inv


_known_inverse.defvjp(lambda a, inv: (inv, inv), lambda inv, d: (_inverse_cotangent(inv, d), jnp.zeros_like(inv)))


def _dn_local(q, k, v, g, b, known=None):
    C, R = DN_CHUNK, DN_BLOCK
    r = lax.broadcasted_iota(jnp.int32, (R, R), 0)
    c = lax.broadcasted_iota(jnp.int32, (R, R), 1)
    same_head = (r // C) == (c // C)
    incl, strict = same_head & (r >= c), same_head & (r > c)
    avg = jnp.full((R, LANES), 1.0 / LANES, F32)
    rc = lax.broadcasted_iota(jnp.int32, (C, C), 0) >= lax.broadcasted_iota(jnp.int32, (C, C), 1)
    gc_lanes = _pdot(rc.astype(F32), g, ((1,), (0,)))
    us, ws, qes, kds, qks, invs = [], [], [], [], [], []
    for first in range(0, DN_HEADS, DN_BLOCK_HEADS):
        stack = lambda x: jnp.concatenate(_heads(x)[first:first + DN_BLOCK_HEADS], axis=0)
        unstack = lambda x: [x[p * C:(p + 1) * C] for p in range(DN_BLOCK_HEADS)]
        gc, q_s, k_s, v_s, b_s = (stack(x) for x in (gc_lanes, q, k, v, b))
        gc_j = _pdot(avg, gc, ((1,), (1,)))
        decay = jnp.exp(jnp.where(incl, _cat([gc] * (R // LANES)) - gc_j, NEG))
        kb = k_s * b_s
        kk = _pdot(jnp.concatenate([kb, q_s], axis=0), k_s, ((1,), (1,)))
        a = jnp.where(strict, kk[:R] * decay, 0.0)
        inv = _unit_lower_inverse(a) if known is None else _known_inverse(a, known[len(invs) * R:(len(invs) + 1) * R])
        invs.append(inv)
        eg = jnp.exp(gc)
        uw = _pdot(inv, _cat([v_s * b_s, kb * eg]), ((1,), (0,)))
        g_last = jnp.concatenate([jnp.broadcast_to(x[C - 1:C], (C, LANES)) for x in unstack(gc)], axis=0)
        us += unstack(uw[:, :LANES])
        ws += unstack(uw[:, LANES:])
        qes += unstack(q_s * eg)
        kds += unstack(k_s * jnp.exp(g_last - gc))
        qks.append(kk[R:] * decay)
    egl = jnp.broadcast_to(jnp.exp(gc_lanes[C - 1:C]), (8, DN_HEADS * LANES))
    return _cat(us), _cat(ws), _cat(qes), _cat(kds), jnp.concatenate(qks, axis=0), egl, jnp.concatenate(invs, axis=0)


def _dn_scan(u, w, qe, kd, qk, egl, state):
    C = DN_CHUNK
    heads = [slice(h * LANES, (h + 1) * LANES) for h in range(DN_HEADS)]
    ws = [_pdot(jnp.concatenate([w[:, sl], qe[:, sl]], axis=0), state[sl, :], ((1,), (0,))) for sl in heads]
    v_new = [u[:, sl] - x[:C] for sl, x in zip(heads, ws)]
    local = []
    for i, first in enumerate(range(0, DN_HEADS, DN_BLOCK_HEADS)):
        y = _pdot(qk[i * DN_BLOCK:(i + 1) * DN_BLOCK], jnp.concatenate(v_new[first:first + DN_BLOCK_HEADS], axis=0), ((1,), (0,)))
        local += [y[p * C:(p + 1) * C] for p in range(DN_BLOCK_HEADS)]
    o = _cat([x[C:] + y for x, y in zip(ws, local)])
    states = [state[sl, :] * egl[0:1, sl] + _pdot(kd[:, sl], vn, ((0,), (0,))) for sl, vn in zip(heads, v_new)]
    return o, jnp.concatenate(states, axis=0)


def _make_delta_rule(name):
    W = DN_HEADS * LANES
    QK = DN_HEADS * DN_CHUNK

    def local_call(ins, cts=None):
        S = ins[0].shape[0]
        n = S // DN_CHUNK
        per = math.gcd(DN_LOCAL_CHUNKS, n)
        row = pl.BlockSpec((per * DN_CHUNK, W), lambda i: (i, 0))
        qkb = pl.BlockSpec((per * QK, DN_BLOCK), lambda i: (i, 0))
        eg = pl.BlockSpec((per, 8, W), lambda i: (i, 0, 0))
        rows = lambda j: slice(j * DN_CHUNK, (j + 1) * DN_CHUNK)
        qk_rows = lambda j: slice(j * QK, (j + 1) * QK)
        out_rows = [rows, rows, rows, rows, qk_rows]

        if cts is None:
            def body(*refs):
                for j in range(per):
                    res = _dn_local(*[r[rows(j), :] for r in refs[:5]])
                    for o_r, o, at_ in zip(refs[5:10], res[:5], out_rows):
                        o_r[at_(j), :] = o
                    refs[10][j] = res[5]
                    refs[11][qk_rows(j), :] = res[6]
            blockdiag = jax.ShapeDtypeStruct((n * QK, DN_BLOCK), F32)
            return _pcall(body, name=f"{name}_local_fwd", grid=(n // per,), in_specs=[row] * 5, out_specs=[row] * 4 + [qkb, eg, qkb],
                          out_shape=[jax.ShapeDtypeStruct((S, W), F32)] * 4 + [blockdiag, jax.ShapeDtypeStruct((n, 8, W), F32), blockdiag],
                          compiler_params=_params(("parallel",)))(*ins)

        def body(*refs):
            for j in range(per):
                known = refs[5][qk_rows(j), :]
                _, vjp = jax.vjp(lambda *a: _dn_local(*a, known=known)[:6], *[r[rows(j), :] for r in refs[:5]])
                grads = vjp(tuple(r[at_(j), :] for r, at_ in zip(refs[6:11], out_rows)) + (refs[11][j],))
                for o_r, o in zip(refs[12:], grads):
                    o_r[rows(j), :] = o
        return _pcall(body, name=f"{name}_local_bwd", grid=(n // per,), in_specs=[row] * 5 + [qkb] + [row] * 4 + [qkb, eg],
                      out_specs=[row] * 5, out_shape=[jax.ShapeDtypeStruct((S, W), F32)] * 5,
                      compiler_params=_params(("parallel",)))(*ins, *cts)

    def scan_call(ins, saved=None, do=None):
        S = ins[0].shape[0]
        n = S // DN_CHUNK
        at = (lambda i: i) if do is None else (lambda i: n - 1 - i)
        row = pl.BlockSpec((DN_CHUNK, W), lambda i: (at(i), 0))
        qkb = pl.BlockSpec((QK, DN_BLOCK), lambda i: (at(i), 0))
        eg = pl.BlockSpec((None, 8, W), lambda i: (at(i), 0, 0))
        st = pl.BlockSpec((None, W, LANES), lambda i: (at(i), 0, 0))

        if do is None:
            def body(*refs):
                o_r, st_r, s_s = refs[6:]

                @pl.when(pl.program_id(0) == 0)
                def _():
                    s_s[...] = jnp.zeros_like(s_s)
                st_r[...] = s_s[...]
                o_r[...], s_s[...] = _dn_scan(*[r[...] for r in refs[:6]], s_s[...])
            return _pcall(body, name=f"{name}_scan_fwd", grid=(n,), in_specs=[row] * 4 + [qkb, eg], out_specs=[row, st],
                          out_shape=[jax.ShapeDtypeStruct((S, W), F32), jax.ShapeDtypeStruct((n, W, LANES), F32)],
                          scratch_shapes=[pltpu.VMEM((W, LANES), F32)], compiler_params=_params(("arbitrary",)))(*ins)

        def body(*refs):
            st_r, do_r = refs[6:8]
            outs, ds_s = refs[8:14], refs[14]

            @pl.when(pl.program_id(0) == 0)
            def _():
                ds_s[...] = jnp.zeros_like(ds_s)
            _, vjp = jax.vjp(_dn_scan, *[r[...] for r in refs[:6]], st_r[...])
            *grads, ds = vjp((do_r[...], ds_s[...]))
            for o_r, gval in zip(outs, grads):
                o_r[...] = gval
            ds_s[...] = ds
        return _pcall(body, name=f"{name}_scan_bwd", grid=(n,), in_specs=[row] * 4 + [qkb, eg, st, row], out_specs=[row] * 4 + [qkb, eg],
                      out_shape=[jax.ShapeDtypeStruct((S, W), F32)] * 4
                      + [jax.ShapeDtypeStruct((n * QK, DN_BLOCK), F32), jax.ShapeDtypeStruct((n, 8, W), F32)],
                      scratch_shapes=[pltpu.VMEM((W, LANES), F32)], compiler_params=_params(("arbitrary",)))(*ins, saved, do)

    @jax.custom_vjp
    def local(q, k, v, g, b):
        return tuple(local_call((q, k, v, g, b))[:6])

    def local_fwd(*a):
        *outs, inverses = local_call(a)
        return tuple(outs), (*a, inverses)

    local.defvjp(local_fwd, lambda res, cts: tuple(local_call(res, tuple(cts))))

    @jax.custom_vjp
    def scan(u, w, qe, kd, qk, egl):
        return scan_call((u, w, qe, kd, qk, egl))[0]

    def scan_fwd(*a):
        o, states = scan_call(a)
        return o, (a, states)

    scan.defvjp(scan_fwd, lambda res, do: tuple(scan_call(res[0], res[1], do)))
    return lambda q, k, v, g, b: scan(*local(q, k, v, g, b))


def _loss_call(y, target):
    S, D = y.shape
    t = min(S, 512)
    n = S // t
    row = pl.BlockSpec((t, D), lambda i: (i, 0))

    def body(y_r, t_r, loss_r, dy_r, acc_s):
        i = pl.program_id(0)

        @pl.when(i == 0)
        def _():
            acc_s[...] = jnp.zeros_like(acc_s)
        err = y_r[...] - t_r[...]
        dy_r[...] = err * (1.0 / D)
        acc_s[...] += jnp.sum(err * err, axis=0, keepdims=True)

        @pl.when(i == n - 1)
        def _():
            loss_r[...] = jnp.broadcast_to(jnp.sum(acc_s[...], axis=1, keepdims=True) * (0.5 / D), loss_r.shape)

    return _pcall(body, name="loss_head", grid=(n,), in_specs=[row, row],
                  out_specs=[pl.BlockSpec((8, LANES), lambda i: (0, 0)), row],
                  out_shape=[jax.ShapeDtypeStruct((8, LANES), F32), jax.ShapeDtypeStruct((S, D), F32)],
                  scratch_shapes=[pltpu.VMEM((1, D), F32)], compiler_params=_params(("arbitrary",)))(y, target)


def _adamw_layer(name, part, w, m, v, layer, earlier=None, rows=128):
    L, R, C = w.shape
    t = _tile(R, rows, 8)
    row = pl.BlockSpec((None, t, C), lambda i: (layer, i, 0))

    def body(p_r, w_r, m_r, v_r, *rest):
        g_r, d_r, nm_r, nv_r = rest[-4:]
        g = p_r[0].astype(F32)
        for s in range(1, N_DEV):
            g = g + p_r[s].astype(F32)
        m_new = ADAM_B1 * m_r[...] + (1.0 - ADAM_B1) * g
        v_new = ADAM_B2 * v_r[...] + (1.0 - ADAM_B2) * (g * g)
        m_hat = m_new / (1.0 - ADAM_B1 ** ADAM_STEP)
        v_hat = v_new / (1.0 - ADAM_B2 ** ADAM_STEP)
        g_r[...] = g
        d_r[...] = -ADAM_LR * (m_hat / (jnp.sqrt(v_hat) + ADAM_EPS) + ADAM_WD * w_r[...])
        nm_r[...] = m_new
        nv_r[...] = v_new

    extra = [] if earlier is None else list(earlier)
    return _pcall(body, name=name, grid=(R // t,),
                  in_specs=[pl.BlockSpec((N_DEV, t, C), lambda i: (0, i, 0)), row, row, row] + [pl.BlockSpec(memory_space=pl.ANY)] * len(extra),
                  out_specs=[row] * 4, out_shape=[jax.ShapeDtypeStruct((L, R, C), F32)] * 4,
                  input_output_aliases={4 + k: k for k in range(len(extra))},
                  compiler_params=_params(("parallel",)))(part, w, m, v, *extra)


def _my_place():
    x, y, c = lax.axis_index("x"), lax.axis_index("y"), lax.axis_index("c")
    return x, y, c


def _index(x, y, c):
    return 4 * x + 2 * y + c


def _all_gather(vs):
    n = len(vs)

    def body(*refs):
        v_refs, out_refs = refs[:n], refs[n:2 * n]
        send_sems, recv_sems, local_sems = refs[2 * n:]
        x, y, c = _my_place()
        me, sibling = (x, y, c), (x, y, 1 - c)
        chips = [(1 - x, y), (x, 1 - y), (1 - x, 1 - y)]

        def copy(a, k, block, to, src=None):
            rows = out_refs[a].at[_index(*block)]
            return pltpu.make_async_remote_copy(src_ref=rows if src is None else src, dst_ref=rows, send_sem=send_sems.at[a, k],
                                                recv_sem=recv_sems.at[a, k], device_id=to, device_id_type=MESH)

        mine = [pltpu.make_async_copy(v_refs[a], out_refs[a].at[_index(*me)], local_sems.at[a]) for a in range(n)]
        first, passed = [], []
        for a in range(n):
            mine[a].start()
            first += [copy(a, 0, me, sibling, src=v_refs[a])]
            first += [copy(a, 1 + j, me, (*chip, c), src=v_refs[a]) for j, chip in enumerate(chips)]
        for cp in first:
            cp.start()
        for j, chip in enumerate(chips):
            for a in range(n):
                copy(a, 1 + j, (*chip, c), me).wait_recv()
                passed.append(copy(a, 4 + j, (*chip, c), sibling))
                passed[-1].start()
        for a in range(n):
            copy(a, 0, sibling, me).wait_recv()
            for j, chip in enumerate(chips):
                copy(a, 4 + j, (*chip, 1 - c), me).wait_recv()
        for cp in first + passed:
            cp.wait_send()
        for a in range(n):
            mine[a].wait()

    any_ = pl.BlockSpec(memory_space=pl.ANY)
    return _pcall(body, name="gather_weights", in_specs=[any_] * n, out_specs=[any_] * n,
                  out_shape=[jax.ShapeDtypeStruct((N_DEV,) + v.shape, v.dtype) for v in vs],
                  scratch_shapes=[pltpu.SemaphoreType.DMA((n, 7)), pltpu.SemaphoreType.DMA((n, 7)), pltpu.SemaphoreType.DMA((n,))])(*vs)


def _all_to_all(vs, after):
    n = len(vs)

    def body(*refs):
        v_refs, out_refs = refs[:n], refs[n + 1:2 * n + 1]
        send_sems, recv_sems, local_sems = refs[2 * n + 1:]
        x, y, c = _my_place()
        me = _index(x, y, c)
        mine = [pltpu.make_async_copy(v_refs[a].at[me], out_refs[a].at[me], local_sems.at[a]) for a in range(n)]
        copies = []
        for a in range(n):
            mine[a].start()
        for k in range(1, N_DEV):
            px = 1 - x if k & 4 else x
            py = 1 - y if k & 2 else y
            pc = 1 - c if k & 1 else c
            for a in range(n):
                cp = pltpu.make_async_remote_copy(src_ref=v_refs[a].at[_index(px, py, pc)], dst_ref=out_refs[a].at[me],
                                                  send_sem=send_sems.at[a, k - 1], recv_sem=recv_sems.at[a, k - 1],
                                                  device_id=(px, py, pc), device_id_type=MESH)
                cp.start()
                copies.append(cp)
        for cp in copies:
            cp.wait()
        for a in range(n):
            mine[a].wait()

    any_ = pl.BlockSpec(memory_space=pl.ANY)
    return _pcall(body, name="exchange_vectors", in_specs=[any_] * (n + 1), out_specs=[any_] * n,
                  out_shape=[jax.ShapeDtypeStruct(v.shape, v.dtype) for v in vs],
                  scratch_shapes=[pltpu.SemaphoreType.DMA((n, 7)), pltpu.SemaphoreType.DMA((n, 7)), pltpu.SemaphoreType.DMA((n,))])(*vs, after)


_HBM = pl.BlockSpec(memory_space=pltpu.HBM)
_SEM = pl.BlockSpec(memory_space=pltpu.SEMAPHORE)
_EFFECT = pltpu.SideEffectType.DATAFLOW_SIDE_EFFECTING


def _direct_copies(gather, v_refs, land_refs, send_sems, recv_sems, local_sems):
    x, y, c = _my_place()
    me = _index(x, y, c)
    local, remote = [], []
    for a, (v_ref, land_ref) in enumerate(zip(v_refs, land_refs)):
        local.append(pltpu.make_async_copy(v_ref if gather else v_ref.at[me], land_ref.at[me], local_sems.at[a]))
    for k in range(1, N_DEV):
        px = 1 - x if k & 4 else x
        py = 1 - y if k & 2 else y
        pc = 1 - c if k & 1 else c
        for a, (v_ref, land_ref) in enumerate(zip(v_refs, land_refs)):
            sem = a * (N_DEV - 1) + k - 1
            remote.append(pltpu.make_async_remote_copy(
                src_ref=v_ref if gather else v_ref.at[_index(px, py, pc)], dst_ref=land_ref.at[me], send_sem=send_sems.at[sem],
                recv_sem=recv_sems.at[sem], device_id=(px, py, pc), device_id_type=MESH))
    return local, remote


def _exchange_start(name, vs, gather, thru):
    n = len(vs)
    lands = [lax.empty((N_DEV,) + v.shape if gather else v.shape, v.dtype) for v in vs]

    def body(*refs):
        v_refs, land_refs = refs[:n], refs[n:2 * n]
        send_sems, recv_sems, local_sems = refs[2 * n + 1:2 * n + 4]
        local, remote = _direct_copies(gather, v_refs, land_refs, send_sems, recv_sems, local_sems)
        for cp in local + remote:
            cp.start()

    hbm = lambda a: pltpu.HBM(a.shape, a.dtype)
    res = _pcall(body, name=name,
                 out_shape=(pltpu.SemaphoreType.DMA((n * (N_DEV - 1),)), pltpu.SemaphoreType.DMA((n * (N_DEV - 1),)), pltpu.SemaphoreType.DMA((n,)),
                            *[hbm(a) for a in (*vs, *lands, thru)]),
                 in_specs=[_HBM] * (2 * n + 1), out_specs=(_SEM, _SEM, _SEM, *[_HBM] * (2 * n + 1)),
                 input_output_aliases={i: 3 + i for i in range(2 * n + 1)},
                 compiler_params=pltpu.CompilerParams(has_side_effects=_EFFECT))(
        *[pltpu.with_memory_space_constraint(a, pltpu.HBM) for a in (*vs, *lands, thru)])
    return (gather, res[:3], res[3:3 + n], res[3 + n:3 + 2 * n]), res[3 + 2 * n]


def _exchange_wait(name, started, after):
    gather, sems, vs, lands = started
    n = len(vs)

    def body(*refs):
        v_refs, land_refs = refs[:n], refs[n:2 * n]
        send_sems, recv_sems, local_sems = refs[2 * n:2 * n + 3]
        local, remote = _direct_copies(gather, v_refs, land_refs, send_sems, recv_sems, local_sems)
        for cp in local:
            cp.wait()
        for cp in remote:
            cp.wait_send()
            cp.wait_recv()

    hbm = lambda a: pltpu.HBM(a.shape, a.dtype)
    res = _pcall(body, name=name, out_shape=tuple(hbm(a) for a in (*vs, *lands)),
                 in_specs=[_HBM] * (2 * n) + [_SEM] * 3 + [pl.BlockSpec(memory_space=pl.ANY)], out_specs=tuple([_HBM] * (2 * n)),
                 input_output_aliases={i: i for i in range(2 * n)},
                 compiler_params=pltpu.CompilerParams(has_side_effects=_EFFECT))(*vs, *lands, *sems, after)
    return list(res[n:])


W_IN_SHARD = IN_WIDTH // N_DEV
SEG_ORDER = ("q_lat", "c_kv", "k_pe", "z_a", "dn_qkv", "dn_ab", "z_b", "dil_qkv", "z_c", "gate")
SEG_WIDTH = (384, 256, LANES, 512, 1536, LANES, 512, 4608, 512, 3072)


def _w_in_plan():
    plan = []

    def add(seg, c0, c1, dst):
        while c0 < c1:
            d = c0 // W_IN_SHARD
            e = min(c1, (d + 1) * W_IN_SHARD)
            plan.append((seg, dst, d, c0 - d * W_IN_SHARD, e - c0))
            dst += e - c0
            c0 = e

    half = MLA_ROPE // 2
    for i, name in enumerate(SEG_ORDER):
        if name == "k_pe":
            o = _SEG["k_pe"][0]
            add(i, o, o + half, 0)
            add(i, o + half, o + 2 * half, LANES // 2)
        elif name == "dn_ab":
            o = _SEG["dn_a"][0]
            add(i, o, o + 2 * DN_HEADS, 0)
        else:
            o, w = _SEG[name]
            add(i, o, o + w, 0)
    return plan


def _make_w_in_segments(name):
    plan = _w_in_plan()
    nseg = len(SEG_ORDER)
    t = 256

    def fwd_call(g):
        L = g.shape[1]

        def body(g_ref, *o_refs):
            for i in (SEG_ORDER.index("k_pe"), SEG_ORDER.index("dn_ab")):
                o_refs[i][...] = jnp.zeros_like(o_refs[i])
            for seg, dst, d, src, n in plan:
                o_refs[seg][:, dst:dst + n] = g_ref[d, :, src:src + n]

        return _pcall(body, name=f"{name}_fwd", grid=(L, D_MODEL // t),
                      in_specs=[pl.BlockSpec((N_DEV, None, t, W_IN_SHARD), lambda l, i: (0, l, i, 0))],
                      out_specs=[pl.BlockSpec((None, t, w), lambda l, i: (l, i, 0)) for w in SEG_WIDTH],
                      out_shape=[jax.ShapeDtypeStruct((L, D_MODEL, w), g.dtype) for w in SEG_WIDTH],
                      compiler_params=_params(("parallel", "parallel")))(g)

    def bwd_call(ds):
        L = ds[0].shape[0]

        def body(*refs):
            d_refs, g_ref = refs[:nseg], refs[nseg]
            for seg, dst, d, src, n in plan:
                g_ref[d, :, src:src + n] = d_refs[seg][:, dst:dst + n]

        return _pcall(body, name=f"{name}_bwd", grid=(L, D_MODEL // t),
                      in_specs=[pl.BlockSpec((None, t, w), lambda l, i: (l, i, 0)) for w in SEG_WIDTH],
                      out_specs=pl.BlockSpec((N_DEV, None, t, W_IN_SHARD), lambda l, i: (0, l, i, 0)),
                      out_shape=jax.ShapeDtypeStruct((N_DEV, L, D_MODEL, W_IN_SHARD), ds[0].dtype),
                      compiler_params=_params(("parallel", "parallel")))(*ds)

    @jax.custom_vjp
    def op(g):
        return tuple(fwd_call(g))

    op.defvjp(lambda g: (op(g), None), lambda _, ds: (bwd_call(tuple(ds)),))
    return op


def _pe_pad(a):
    h = MLA_ROPE // 2
    z = jnp.zeros(a.shape[:-1] + (h,), a.dtype)
    return jnp.concatenate([a[..., :h], z, a[..., h:], z], axis=-1)


def _layer_norm(tag, x, norm_g):
    return _make_rowwise(f"{tag}_norm", _f_norm, 512)((x,), (), (norm_g[None, :],), ())[0]


def _layer(tag, x, tables, W):
    h = _layer_norm(tag, x, W["norm_g"])
    return _layer_tail(tag, x, _make_multi_linear(f"{tag}_inproj", 10, INPROJ_DTYPE)(h, W["w_in_segments"]), tables, W)


def _layer_tail(tag, x, segments, tables, W):
    cos_p, sin_p, cos_h, sin_h = tables
    row = lambda a: a[None, :]
    q_lat, c_kv, kpe, z_a, dn_qkv, ab, z_b, dil_qkv, z_c, gl = segments

    qn_lat, ckvn, kp = _make_rowwise(f"{tag}_mla_a", _f_mla_a, 512)(
        (q_lat, c_kv, kpe), (cos_p, sin_p),
        (row(W["mla_q_a_norm_g"]), row(W["mla_kv_a_norm_g"]), row(_pe_pad(W["mla_k_norm_g"][LANES:]))), ())
    wq = W["mla_w_q_b"].reshape(MLA_Q_RANK, MLA_HEADS, MLA_QK)
    wq = jnp.concatenate([wq[:, :, :LANES].reshape(MLA_Q_RANK, -1), _pe_pad(wq[:, :, LANES:]).reshape(MLA_Q_RANK, -1)], axis=1)
    wkv = W["mla_w_kv_b"].reshape(MLA_KV_RANK, MLA_HEADS, 2 * LANES)
    (q8,) = _make_multi_linear(f"{tag}_qb", 1)(qn_lat, (wq,))
    kn_raw, v_mla = _make_multi_linear(f"{tag}_kvb", 2)(
        ckvn, (wkv[:, :, :LANES].reshape(MLA_KV_RANK, -1), wkv[:, :, LANES:].reshape(MLA_KV_RANK, -1)))
    qn, qp, kn = _make_rowwise(f"{tag}_mla_b", _f_mla_b, 512)(
        (q8, kn_raw), (cos_p, sin_p),
        (row(W["mla_q_norm_g"][:LANES]), row(_pe_pad(W["mla_q_norm_g"][LANES:])), row(W["mla_k_norm_g"][:LANES])), ())
    y_a = _make_mla_attn(f"{tag}_mla")(qn, qp, kn, kp, v_mla)

    mixed = _make_conv(f"{tag}_conv")(dn_qkv, W["dn_conv_w"])
    lane_head = jnp.arange(DN_HEADS * LANES) // LANES
    e_a = (jnp.arange(LANES)[:, None] == lane_head[None, :]).astype(F32)
    e_b = (jnp.arange(LANES)[:, None] == lane_head[None, :] + DN_HEADS).astype(F32)
    q_dn, k_dn, v_dn, g_dn, b_dn = _make_rowwise(f"{tag}_dn_pre", _f_dn_pre, 512)(
        (mixed, ab), (), (row(jnp.repeat(W["dn_a_log"], LANES)), row(jnp.repeat(W["dn_dt_bias"], LANES))), (e_a, e_b))
    o_dn = _make_delta_rule(f"{tag}_dn")(q_dn, k_dn, v_dn, g_dn, b_dn)

    qkv_dil = _make_rowwise(f"{tag}_dil_pre", _f_dil_pre, 256)(
        (dil_qkv,), (cos_h, sin_h), (row(W["dil_q_norm_g"]), row(W["dil_k_norm_g"])), ())
    n_groups = len(DIL_DILATIONS)
    o_lse = [_make_dil_attn(f"{tag}_dil{g}", d, DIL_TILE_ROWS[g])(qkv_dil[g], qkv_dil[n_groups + g], qkv_dil[2 * n_groups + g])
             for g, d in enumerate(DIL_DILATIONS)]

    ya, yb, yc = _make_rowwise(f"{tag}_merge_a", _f_merge_a, 256)(
        (y_a, z_a, o_dn, z_b, *[o for o, _ in o_lse], *[l for _, l in o_lse], z_c), (), (row(W["dn_out_norm_g"]),), ())
    (b0,) = _make_multi_linear(f"{tag}_br0", 1)(ya, (W["w_branch"][0],))
    (b1,) = _make_multi_linear(f"{tag}_br1", 1)(yb, (W["w_branch"][1],))
    (b2,) = _make_multi_linear(f"{tag}_br2", 1)(yc, (W["w_branch"][2],))
    (mix,) = _make_rowwise(f"{tag}_merge_b", _f_merge_b, 256)((b0, b1, b2, gl), (), (), ())
    return _make_resid_linear(f"{tag}_out")(x, mix, W["w_out"])


SHARDED = (("w_in", (D_MODEL, W_IN_SHARD)), ("mla_w_q_b", (MLA_Q_RANK, MLA_HEADS * MLA_QK // N_DEV)),
           ("mla_w_kv_b", (MLA_KV_RANK, MLA_HEADS * 2 * LANES // N_DEV)), ("w_branch", (3 * BRANCH_W, D_MODEL // N_DEV)),
           ("w_out", (D_MODEL // N_DEV, D_MODEL)), ("dn_conv_w", (DN_CONV, 3 * DN_HEADS * LANES // N_DEV)))
SMALL = (("norm_g", D_MODEL), ("mla_q_a_norm_g", MLA_Q_RANK), ("mla_kv_a_norm_g", MLA_KV_RANK), ("mla_q_norm_g", MLA_QK),
         ("mla_k_norm_g", MLA_QK), ("dn_a_log", DN_HEADS), ("dn_dt_bias", DN_HEADS), ("dn_out_norm_g", LANES),
         ("dil_q_norm_g", LANES), ("dil_k_norm_g", LANES))
WEIGHTS = ("norm_g", "w_in", "mla_q_a_norm_g", "mla_w_q_b", "mla_kv_a_norm_g", "mla_w_kv_b", "mla_q_norm_g", "mla_k_norm_g",
           "dn_conv_w", "dn_a_log", "dn_dt_bias", "dn_out_norm_g", "dil_q_norm_g", "dil_k_norm_g", "w_branch", "w_out")


def _round_up(n, m):
    return -(-n // m) * m


def _pack_vectors(pieces):
    return jnp.concatenate([jnp.pad(p, (0, _round_up(p.shape[0], LANES) - p.shape[0])) for p in pieces]).reshape(-1, LANES)


def _unpack_vectors(flat, sizes):
    out, off = [], 0
    flat = flat.reshape(-1)
    for n in sizes:
        out.append(flat[off:off + n])
        off += _round_up(n, LANES)
    return out


def _whole_weights(g, small):
    W = dict(small)
    W["mla_w_q_b"] = g["mla_w_q_b"].transpose(1, 0, 2).reshape(MLA_Q_RANK, -1)
    W["mla_w_kv_b"] = g["mla_w_kv_b"].transpose(1, 0, 2).reshape(MLA_KV_RANK, -1)
    W["w_branch"] = g["w_branch"].reshape(N_DEV, 3, BRANCH_W, -1).transpose(1, 2, 0, 3).reshape(3, BRANCH_W, D_MODEL)
    W["w_out"] = g["w_out"].reshape(D_MODEL, D_MODEL)
    W["dn_conv_w"] = g["dn_conv_w"].transpose(1, 0, 2).reshape(DN_CONV, -1)
    return W


def kernel(x, positions, norm_g, w_in, mla_q_a_norm_g, mla_w_q_b, mla_kv_a_norm_g, mla_w_kv_b, mla_q_norm_g, mla_k_norm_g, dn_conv_w, dn_a_log, dn_dt_bias, dn_out_norm_g, dil_q_norm_g, dil_k_norm_g, w_branch, w_out, loss_target, m_norm_g, m_w_in, m_mla_q_a_norm_g, m_mla_w_q_b, m_mla_kv_a_norm_g, m_mla_w_kv_b, m_mla_q_norm_g, m_mla_k_norm_g, m_dn_conv_w, m_dn_a_log, m_dn_dt_bias, m_dn_out_norm_g, m_dil_q_norm_g, m_dil_k_norm_g, m_w_branch, m_w_out, v_norm_g, v_w_in, v_mla_q_a_norm_g, v_mla_w_q_b, v_mla_kv_a_norm_g, v_mla_w_kv_b, v_mla_q_norm_g, v_mla_k_norm_g, v_dn_conv_w, v_dn_a_log, v_dn_dt_bias, v_dn_out_norm_g, v_dil_q_norm_g, v_dil_k_norm_g, v_w_branch, v_w_out):
    w = dict(norm_g=norm_g, w_in=w_in, mla_q_a_norm_g=mla_q_a_norm_g, mla_w_q_b=mla_w_q_b, mla_kv_a_norm_g=mla_kv_a_norm_g,
             mla_w_kv_b=mla_w_kv_b, mla_q_norm_g=mla_q_norm_g, mla_k_norm_g=mla_k_norm_g, dn_conv_w=dn_conv_w, dn_a_log=dn_a_log,
             dn_dt_bias=dn_dt_bias, dn_out_norm_g=dn_out_norm_g, dil_q_norm_g=dil_q_norm_g, dil_k_norm_g=dil_k_norm_g,
             w_branch=w_branch, w_out=w_out)
    m = dict(norm_g=m_norm_g, w_in=m_w_in, mla_q_a_norm_g=m_mla_q_a_norm_g, mla_w_q_b=m_mla_w_q_b, mla_kv_a_norm_g=m_mla_kv_a_norm_g,
             mla_w_kv_b=m_mla_w_kv_b, mla_q_norm_g=m_mla_q_norm_g, mla_k_norm_g=m_mla_k_norm_g, dn_conv_w=m_dn_conv_w,
             dn_a_log=m_dn_a_log, dn_dt_bias=m_dn_dt_bias, dn_out_norm_g=m_dn_out_norm_g, dil_q_norm_g=m_dil_q_norm_g,
             dil_k_norm_g=m_dil_k_norm_g, w_branch=m_w_branch, w_out=m_w_out)
    v = dict(norm_g=v_norm_g, w_in=v_w_in, mla_q_a_norm_g=v_mla_q_a_norm_g, mla_w_q_b=v_mla_w_q_b, mla_kv_a_norm_g=v_mla_kv_a_norm_g,
             mla_w_kv_b=v_mla_w_kv_b, mla_q_norm_g=v_mla_q_norm_g, mla_k_norm_g=v_mla_k_norm_g, dn_conv_w=v_dn_conv_w,
             dn_a_log=v_dn_a_log, dn_dt_bias=v_dn_dt_bias, dn_out_norm_g=v_dn_out_norm_g, dil_q_norm_g=v_dil_q_norm_g,
             dil_k_norm_g=v_dil_k_norm_g, w_branch=v_w_branch, w_out=v_w_out)
    x2, target = x[0], loss_target[0]
    pos = positions[0][:, None]

    names = [n for n, _ in SHARDED]
    view = lambda t, n, s: t[n].reshape((DEPTH,) + s)
    shards = [[(view(w, n, s) if n == "dn_conv_w" else view(w, n, s).astype(BF16))[l] for n, s in SHARDED] for l in range(DEPTH)]
    small = [{n: w[n][l] for n, _ in SMALL} for l in range(DEPTH)]
    (w_in0,) = _all_gather(shards[0][:1])
    gathering0, w_in0 = _exchange_start("gather_layer0_others_start", shards[0][1:], True, w_in0)
    gathering1, w_in0 = _exchange_start("gather_layer1_start", shards[1], True, w_in0)
    tables = _rope_tables(pos, _rope_consts())

    def layer(l, g, small_l, x_l, pending=None):
        tag = f"l{l}"
        seg_op = _make_w_in_segments(f"{tag}_w_in_segments")
        w_segs, vjp_segs = jax.vjp(lambda gw: tuple(s[0] for s in seg_op(gw[:, None])), g["w_in"])
        h, vjp_norm = jax.vjp(lambda x_, ng: _layer_norm(tag, x_, ng), x_l, small_l["norm_g"])
        wide = [i for i in range(len(w_segs)) if i not in INPROJ_NARROW]
        segs = dict(zip(INPROJ_NARROW, _mm_nn_many(f"{tag}_inproj_fwd_narrow", h, [w_segs[i] for i in INPROJ_NARROW], INPROJ_DTYPE)))
        segs.update({i: _mm(f"{tag}_inproj_fwd{i}", h, w_segs[i], "nn", out_dtype=INPROJ_DTYPE, **INPROJ_TILES["nn"]) for i in wide})
        segs = tuple(segs[i] for i in range(len(w_segs)))
        if pending is not None:
            g = dict(g, **dict(zip(names[1:], _exchange_wait(f"gather_{tag}_others_wait", pending, segs[7]))))
        rest_g = {n: a for n, a in g.items() if n != "w_in"}
        rest_s = {n: a for n, a in small_l.items() if n != "norm_g"}
        y, vjp_tail = jax.vjp(lambda sg, x_, gg, ss: _layer_tail(tag, x_, sg, tables, _whole_weights(gg, ss)), segs, x_l, rest_g, rest_s)

        def backward(dy):
            dsegs, dx_skip, d_rest_g, d_rest_s = vjp_tail(dy)
            others, big = _exchange_start(f"exchange_{tag}_others_start", [d_rest_g[n] for n in names if n != "w_in"], False, dsegs[7])
            dsegs = tuple(dsegs[:7]) + (big,) + tuple(dsegs[8:])
            dws = dict(zip(INPROJ_NARROW, _mm_tn_many(f"{tag}_inproj_dw_narrow", h, [dsegs[i] for i in INPROJ_NARROW], w_segs[0].dtype)))
            dws.update({i: _mm(f"{tag}_inproj_dw{i}", h, dsegs[i], "tn", out_dtype=w_segs[i].dtype, **INPROJ_TILES["tn"]) for i in wide})
            dws = tuple(dws[i] for i in range(len(w_segs)))
            projection, first =_exchange_start(f"exchange_{tag}_w_in_start", [vjp_segs(dws)[0]], False, dsegs[0])
            dsegs = (first,) + tuple(dsegs[1:])
            dh = None
            for i, group in enumerate(INPROJ_DH_GROUPS):
                dh = _mm_nt_sum(f"{tag}_inproj_dh{i}", [(dsegs[s], w_segs[s]) for s in group], dh,
                                h.dtype if i == len(INPROJ_DH_GROUPS) - 1 else F32)
            dx_norm, d_norm_g = vjp_norm(dh)
            return (projection, others), dx_skip + dx_norm, dict(d_rest_s, norm_g=d_norm_g)

        def landed(exchanging, after):
            projection, others = (_exchange_wait(f"exchange_{tag}_{k}_wait", e, after) for k, e in zip(("w_in", "others"), exchanging))
            return projection + others
        return y, backward, landed

    y0, backward0, landed0 = layer(0, {"w_in": w_in0}, small[0], x2, gathering0)
    gathered1 = dict(zip(names, _exchange_wait("gather_layer1_wait", gathering1, y0)))
    y1, backward1, landed1 = layer(1, gathered1, small[1], y0)
    loss_splat, dy = _loss_call(y1, target)
    loss = lax.psum(loss_splat[0, 0], ("x", "y", "c"))
    exchanging1, d_y0, g_small1 = backward1(dy)
    exchanging0, g_x, g_small0 = backward0(d_y0)

    state = lambda n, s: (view(w, n, s), view(m, n, s), view(v, n, s))
    parts1 = landed1(exchanging1, g_x)
    updated = {n: _adamw_layer(f"adamw_l1_{n}", parts1[i], *state(n, s), 1) for i, (n, s) in enumerate(SHARDED)}
    g_small = (g_small0, g_small1)
    sizes = [k for _ in range(DEPTH) for _, k in SMALL]
    g_vec = _pack_vectors([g_small[l][n] for l in range(DEPTH) for n, _ in SMALL])
    (parts_vec,) = _all_to_all([jnp.broadcast_to(g_vec[None], (N_DEV,) + g_vec.shape)], updated["w_in"][0])
    parts0 = landed0(exchanging0, parts_vec)

    vec = lambda t: _pack_vectors([t[n][l] for l in range(DEPTH) for n, _ in SMALL])[None]
    outs = {}
    for i, (n, s) in enumerate(SHARDED):
        res = _adamw_layer(f"adamw_l0_{n}", parts0[i], *state(n, s), 0, earlier=updated[n])
        outs[n] = [o.reshape(w[n].shape) for o in res]
    vec_outs = [_unpack_vectors(o, sizes) for o in _adamw_layer("adamw_vectors", parts_vec, vec(w), vec(m), vec(v), 0)]
    for i, (n, _) in enumerate(SMALL):
        outs[n] = [jnp.stack([o[l * len(SMALL) + i] for l in range(DEPTH)]) for o in vec_outs]
    return (loss, g_x[None], *[outs[n][k] for k in range(4) for n in WEIGHTS])
```

```python
import functools
import math

import jax
import jax.numpy as jnp
from jax import lax
from jax.experimental import pallas as pl
from jax.experimental.pallas import tpu as pltpu

F32 = jnp.float32
BF16 = jnp.bfloat16
HI = lax.Precision.HIGHEST
MESH = pl.DeviceIdType.MESH

N_DEV = 8
D_MODEL = 1024
DEPTH = 2
RMS_EPS = 1e-6
ROPE_THETA = 10000.0
LANES = 128
MLA_HEADS = 4
MLA_ROPE = 64
MLA_QK = 192
MLA_Q_RANK = 384
MLA_KV_RANK = 256
DN_HEADS = 4
DN_CHUNK = 64
DN_CONV = 4
DIL_HEADS = 12
DIL_GROUP_HEADS = 4
DIL_DILATIONS = (1, 4, 16)
DIL_BLOCK = 128
BRANCH_W = 512
IN_WIDTH = 11464
NEG = -1e30
VMEM_LIMIT = 56 * 1024 * 1024

ADAM_LR, ADAM_B1, ADAM_B2, ADAM_EPS, ADAM_WD, ADAM_STEP = 0.001, 0.9, 0.999, 1e-08, 0.01, 10

_SEG = {}
_off = 0
for _n, _w in (("q_lat", 384), ("c_kv", 256), ("k_pe", 64), ("z_a", 512), ("dn_qkv", 1536), ("dn_a", 4), ("dn_b", 4),
               ("z_b", 512), ("dil_qkv", 4608), ("z_c", 512), ("gate", 3072)):
    _SEG[_n] = (_off, _w)
    _off += _w
assert _off == IN_WIDTH


def _pcall(body, **kw):
    return pl.pallas_call(body, **kw)


def _params(sem=None):
    return pltpu.CompilerParams(dimension_semantics=sem, vmem_limit_bytes=VMEM_LIMIT)


def _tile(n, target, mult):
    t = (min(n, target) // mult) * mult
    while t >= mult:
        if n % t == 0:
            return t
        t -= mult
    return n


def _mm(name, a, b, mode, out_dtype=F32, acc=None, tm=1024, tn=512, tk=1024):
    if mode == "nn":
        (M, K), (_, N) = a.shape, b.shape
    elif mode == "nt":
        (M, K), (N, _) = a.shape, b.shape
    else:
        (K, M), (_, N) = a.shape, b.shape
    tm, tn, tk = _tile(M, tm, LANES), _tile(N, tn, LANES), _tile(K, tk, LANES)
    nk = K // tk
    dims = {"nn": (((1,), (0,)), ((), ())), "nt": (((1,), (1,)), ((), ())), "tn": (((0,), (0,)), ((), ()))}[mode]
    a_spec = pl.BlockSpec((tk, tm), lambda i, j, k: (k, i)) if mode == "tn" else pl.BlockSpec((tm, tk), lambda i, j, k: (i, k))
    b_spec = pl.BlockSpec((tn, tk), lambda i, j, k: (j, k)) if mode == "nt" else pl.BlockSpec((tk, tn), lambda i, j, k: (k, j))
    o_spec = pl.BlockSpec((tm, tn), lambda i, j, k: (i, j))
    has_acc = acc is not None

    def body(*refs):
        a_ref, b_ref = refs[:2]
        c_ref = refs[2] if has_acc else None
        o_ref = refs[3] if has_acc else refs[2]
        prod = lax.dot_general(a_ref[...].astype(BF16), b_ref[...].astype(BF16), dims, preferred_element_type=F32)
        if nk == 1:
            o_ref[...] = (prod + c_ref[...].astype(F32) if has_acc else prod).astype(out_dtype)
            return
        acc_ref = refs[-1]
        k = pl.program_id(2)

        @pl.when(k == 0)
        def _():
            acc_ref[...] = prod + c_ref[...].astype(F32) if has_acc else prod

        @pl.when(k > 0)
        def _():
            acc_ref[...] += prod

        @pl.when(k == nk - 1)
        def _():
            o_ref[...] = acc_ref[...].astype(out_dtype)

    ins = [a, b] + ([acc] if has_acc else [])
    in_specs = [a_spec, b_spec] + ([o_spec] if has_acc else [])
    return _pcall(body, name=name, grid=(M // tm, N // tn, nk), in_specs=in_specs, out_specs=o_spec,
                  out_shape=jax.ShapeDtypeStruct((M, N), out_dtype), scratch_shapes=[pltpu.VMEM((tm, tn), F32)] if nk > 1 else [],
                  compiler_params=_params(("parallel", "parallel", "arbitrary")))(*ins)


def _mm_nt_sum(name, pairs, acc, out_dtype, tm=512, tn=1024):
    M, N = pairs[0][0].shape[0], pairs[0][1].shape[0]
    tm, tn = _tile(M, tm, LANES), _tile(N, tn, LANES)
    n = len(pairs)
    has_acc = acc is not None
    o_spec = pl.BlockSpec((tm, tn), lambda i, j: (i, j))

    def body(*refs):
        total = refs[2 * n][...].astype(F32) if has_acc else None
        for a_ref, b_ref in zip(refs[:n], refs[n:2 * n]):
            prod = lax.dot_general(a_ref[...].astype(BF16), b_ref[...].astype(BF16), (((1,), (1,)), ((), ())), preferred_element_type=F32)
            total = prod if total is None else total + prod
        refs[-1][...] = total.astype(out_dtype)

    in_specs = ([pl.BlockSpec((tm, a.shape[1]), lambda i, j: (i, 0)) for a, _ in pairs]
                + [pl.BlockSpec((tn, b.shape[1]), lambda i, j: (j, 0)) for _, b in pairs] + ([o_spec] if has_acc else []))
    return _pcall(body, name=name, grid=(M // tm, N // tn), in_specs=in_specs, out_specs=o_spec,
                  out_shape=jax.ShapeDtypeStruct((M, N), out_dtype), compiler_params=_params(("parallel", "parallel")))(
        *[a for a, _ in pairs], *[b for _, b in pairs], *([acc] if has_acc else []))


def _mm_nn_many(name, a, bs, out_dtype, tm=1024):
    M, K = a.shape
    tm = _tile(M, tm, LANES)
    n = len(bs)

    def body(*refs):
        a_t = refs[0][...].astype(BF16)
        for b_ref, o_ref in zip(refs[1:1 + n], refs[1 + n:]):
            o_ref[...] = jnp.dot(a_t, b_ref[...].astype(BF16), preferred_element_type=F32).astype(out_dtype)

    return _pcall(body, name=name, grid=(M // tm,),
                  in_specs=[pl.BlockSpec((tm, K), lambda i: (i, 0))] + [pl.BlockSpec(b.shape, lambda i: (0, 0)) for b in bs],
                  out_specs=[pl.BlockSpec((tm, b.shape[1]), lambda i: (i, 0)) for b in bs],
                  out_shape=[jax.ShapeDtypeStruct((M, b.shape[1]), out_dtype) for b in bs],
                  compiler_params=_params(("parallel",)))(a, *bs)


def _mm_tn_many(name, a, bs, out_dtype, tk=1024):
    K, M = a.shape
    tk = _tile(K, tk, LANES)
    nk, n = K // tk, len(bs)

    def body(*refs):
        a_t = refs[0][...].astype(BF16)
        k = pl.program_id(0)
        for b_ref, o_ref, acc in zip(refs[1:1 + n], refs[1 + n:1 + 2 * n], refs[1 + 2 * n:]):
            prod = lax.dot_general(a_t, b_ref[...].astype(BF16), (((0,), (0,)), ((), ())), preferred_element_type=F32)

            @pl.when(k == 0)
            def _(acc=acc, prod=prod):
                acc[...] = prod

            @pl.when(k > 0)
            def _(acc=acc, prod=prod):
                acc[...] += prod

            @pl.when(k == nk - 1)
            def _(acc=acc, o_ref=o_ref):
                o_ref[...] = acc[...].astype(out_dtype)

    return _pcall(body, name=name, grid=(nk,),
                  in_specs=[pl.BlockSpec((tk, M), lambda k: (k, 0))] + [pl.BlockSpec((tk, b.shape[1]), lambda k: (k, 0)) for b in bs],
                  out_specs=[pl.BlockSpec((M, b.shape[1]), lambda k: (0, 0)) for b in bs],
                  out_shape=[jax.ShapeDtypeStruct((M, b.shape[1]), out_dtype) for b in bs],
                  scratch_shapes=[pltpu.VMEM((M, b.shape[1]), F32) for b in bs],
                  compiler_params=_params(("arbitrary",)))(a, *bs)


INPROJ_DH_GROUPS = ((0, 1, 2, 3, 5, 6, 8), (4,), (9,), (7,))
INPROJ_NARROW = INPROJ_DH_GROUPS[0]

INPROJ_DTYPE = BF16

INPROJ_TILES = {"nn": dict(tm=1024, tn=1536, tk=1024), "tn": dict(tm=1024, tn=768, tk=2048)}


def _make_multi_linear(name, n, out_dtype=F32):
    @jax.custom_vjp
    def op(h, ws):
        return tuple(_mm(f"{name}_fwd{i}", h, w, "nn", out_dtype=out_dtype) for i, w in enumerate(ws))

    def fwd(h, ws):
        return op(h, ws), (h, ws)

    def bwd(res, douts):
        h, ws = res
        dh = None
        for i, (w, d) in enumerate(zip(ws, douts)):
            dh = _mm(f"{name}_dh{i}", d, w, "nt", acc=dh, out_dtype=h.dtype if i == len(ws) - 1 else F32)
        dws = tuple(_mm(f"{name}_dw{i}", h, d, "tn", out_dtype=w.dtype) for i, (w, d) in enumerate(zip(ws, douts)))
        return dh, dws

    op.defvjp(fwd, bwd)
    return op


def _make_resid_linear(name):
    @jax.custom_vjp
    def op(x, a, w):
        return _mm(f"{name}_fwd", a, w, "nn", acc=x)

    def fwd(x, a, w):
        return op(x, a, w), (a, w)

    def bwd(res, dy):
        a, w = res
        return dy, _mm(f"{name}_da", dy, w, "nt", out_dtype=a.dtype), _mm(f"{name}_dw", a, dy, "tn", out_dtype=w.dtype)

    op.defvjp(fwd, bwd)
    return op


def _make_rowwise(name, f, tile):
    def specs(rows, aux, params, consts, t):
        row = [pl.BlockSpec((t, a.shape[1]), lambda i: (i, 0)) for a in (*rows, *aux)]
        full = [pl.BlockSpec(p.shape, lambda i: (0, 0)) for p in (*params, *consts)]
        return row, full

    def fwd_call(rows, aux, params, consts):
        S = rows[0].shape[0]
        t = min(tile, S)
        n_in = len(rows) + len(aux) + len(params) + len(consts)
        shp = lambda a: jax.ShapeDtypeStruct((t, a.shape[1]), a.dtype)
        outs = jax.eval_shape(f, *[shp(a) for a in (*rows, *aux)], *params, *consts)
        row_specs, full_specs = specs(rows, aux, params, consts, t)

        def body(*refs):
            res = f(*[r[...] for r in refs[:n_in]])
            for o_ref, o in zip(refs[n_in:], res):
                o_ref[...] = o

        return _pcall(body, name=f"{name}_fwd", grid=(S // t,), in_specs=row_specs + full_specs,
                      out_specs=[pl.BlockSpec((t, o.shape[1]), lambda i: (i, 0)) for o in outs],
                      out_shape=[jax.ShapeDtypeStruct((S, o.shape[1]), o.dtype) for o in outs],
                      compiler_params=_params(("parallel",)))(*rows, *aux, *params, *consts)

    def bwd_call(rows, aux, params, consts, douts):
        S = rows[0].shape[0]
        t = min(tile, S)
        nr, na, npar, nc, nd = len(rows), len(aux), len(params), len(consts), len(douts)
        row_specs, full_specs = specs(rows, aux, params, consts, t)

        def body(*refs):
            vals = [r[...] for r in refs[:nr + na + npar + nc]]
            rv, av = vals[:nr], vals[nr:nr + na]
            pv, cv = vals[nr + na:nr + na + npar], vals[nr + na + npar:]
            dv = tuple(r[...] for r in refs[nr + na + npar + nc:nr + na + npar + nc + nd])
            out_refs = refs[nr + na + npar + nc + nd:]
            _, vjp = jax.vjp(lambda *rp: f(*rp[:nr], *av, *rp[nr:], *cv), *rv, *pv)
            grads = vjp(dv)
            for o_ref, g in zip(out_refs[:nr], grads[:nr]):
                o_ref[...] = g
            first = pl.program_id(0) == 0
            for o_ref, g in zip(out_refs[nr:], grads[nr:]):
                @pl.when(first)
                def _(o_ref=o_ref):
                    o_ref[...] = jnp.zeros_like(o_ref)
                o_ref[...] += g

        res = _pcall(body, name=f"{name}_bwd", grid=(S // t,),
                     in_specs=row_specs + full_specs + [pl.BlockSpec((t, d.shape[1]), lambda i: (i, 0)) for d in douts],
                     out_specs=[pl.BlockSpec((t, a.shape[1]), lambda i: (i, 0)) for a in rows]
                     + [pl.BlockSpec(p.shape, lambda i: (0, 0)) for p in params],
                     out_shape=[jax.ShapeDtypeStruct(a.shape, a.dtype) for a in (*rows, *params)],
                     compiler_params=_params(("arbitrary",)))(*rows, *aux, *params, *consts, *douts)
        return tuple(res[:nr]), tuple(res[nr:])

    @jax.custom_vjp
    def op(rows, aux, params, consts):
        return tuple(fwd_call(rows, aux, params, consts))

    def fwd(rows, aux, params, consts):
        return op(rows, aux, params, consts), (rows, aux, params, consts)

    def bwd(res, douts):
        rows, aux, params, consts = res
        drows, dparams = bwd_call(rows, aux, params, consts, tuple(douts))
        zeros = lambda xs: tuple(jnp.zeros_like(a) for a in xs)
        return drows, zeros(aux), dparams, zeros(consts)

    op.defvjp(fwd, bwd)
    return op


@jax.custom_vjp
def _swap_halves(x):
    return pltpu.roll(x, LANES // 2, 1)


_swap_halves.defvjp(lambda x: (_swap_halves(x), None), lambda _, g: (_swap_halves(g),))


def _rope(x, cos_t, sin_t):
    return x * cos_t + _swap_halves(x) * sin_t


def _rms(x, g, n=None):
    n = x.shape[-1] if n is None else n
    return x * lax.rsqrt(jnp.sum(x * x, axis=-1, keepdims=True) * (1.0 / n) + RMS_EPS) * g


def _heads(x):
    return [x[:, i * LANES:(i + 1) * LANES] for i in range(x.shape[1] // LANES)]


def _cat(xs):
    return jnp.concatenate(xs, axis=1)


def _silu(x):
    return x * jax.nn.sigmoid(x)


def _f_norm(x, g):
    return (_rms(x, g).astype(BF16),)


def _f_mla_a(q_lat, c_kv, kpe, cos_p, sin_p, qa_g, kva_g, kpe_g):
    q_lat, c_kv, kpe = (t.astype(F32) for t in (q_lat, c_kv, kpe))
    kp =_rope(_rms(kpe, kpe_g, MLA_ROPE), cos_p, sin_p)
    return _rms(q_lat, qa_g).astype(BF16), _rms(c_kv, kva_g).astype(BF16), _cat([kp] * MLA_HEADS)


def _f_mla_b(q8, kn_raw, cos_p, sin_p, qn_g, qp_g, kn_g):
    hs = _heads(q8)
    qn = _cat([_rms(h, qn_g) for h in hs[:MLA_HEADS]])
    qp = _cat([_rope(_rms(h, qp_g, MLA_ROPE), cos_p, sin_p) for h in hs[MLA_HEADS:]])
    kn = _cat([_rms(h, kn_g) for h in _heads(kn_raw)])
    return qn, qp, kn


def _softplus(x):
    return jnp.maximum(x, 0.0) + jnp.log(1.0 + jnp.exp(-jnp.abs(x)))


def _l2n(x):
    return x * lax.rsqrt(jnp.sum(x * x, axis=-1, keepdims=True) + 1e-6)


def _f_dn_pre(mixed, ab, alog_f, dtb_f, e_a, e_b):
    hs = _heads(mixed)
    q = _cat([_l2n(h) * (LANES ** -0.5) for h in hs[:DN_HEADS]])
    k = _cat([_l2n(h) for h in hs[DN_HEADS:2 * DN_HEADS]])
    v = _cat(hs[2 * DN_HEADS:])
    ab = ab.astype(F32)
    a_f = jnp.dot(ab, e_a, precision=HI, preferred_element_type=F32)
    b_f = jnp.dot(ab, e_b, precision=HI, preferred_element_type=F32)
    g = -jnp.exp(alog_f) * _softplus(a_f + dtb_f)
    return q, k, v, g, jax.nn.sigmoid(b_f)


def _f_dil_pre(qkv, cos_h, sin_h, q_g, k_g):
    hs = [h.astype(F32) for h in _heads(qkv)]
    q = [_rope(_rms(h, q_g), cos_h, sin_h) for h in hs[:DIL_HEADS]]
    k = [_rope(_rms(h, k_g), cos_h, sin_h) for h in hs[DIL_HEADS:2 * DIL_HEADS]]
    v = hs[2 * DIL_HEADS:]
    group = lambda xs, g: _cat(xs[g * DIL_GROUP_HEADS:(g + 1) * DIL_GROUP_HEADS])
    return tuple(group(xs, g) for xs in (q, k, v) for g in range(len(DIL_DILATIONS)))


def _f_merge_a(y_a, z_a, o_dn, z_b, o0, o1, o2, l0, l1, l2, z_c, out_g):
    z_a, z_b, z_c = (z.astype(F32) for z in (z_a, z_b, z_c))
    y_b = _cat([_rms(h, out_g) for h in _heads(o_dn)])
    os_, ls = [_heads(o) for o in (o0, o1, o2)], [_heads(l) for l in (l0, l1, l2)]
    y_c = []
    for j in range(DIL_GROUP_HEADS):
        l3 = [ls[g][j] for g in range(3)]
        m = jnp.maximum(jnp.maximum(l3[0], l3[1]), l3[2])
        e3 = [jnp.exp(l - m) for l in l3]
        den = e3[0] + e3[1] + e3[2]
        y_c.append(sum(e3[g] * os_[g][j] for g in range(3)) / den)
    return tuple(t.astype(BF16) for t in (y_a * _silu(z_a), y_b * _silu(z_b), _cat(y_c) * _silu(z_c)))


def _f_merge_b(b0, b1, b2, gl):
    gs = [jax.nn.sigmoid(gl[:, i * D_MODEL:(i + 1) * D_MODEL].astype(F32)) for i in range(3)]
    return ((gs[0] * b0 + gs[1] * b1 + gs[2] * b2).astype(BF16),)


def _rope_tables(pos, inv_sign):
    S = pos.shape[0]
    t = min(S, 1024)

    def body(p_ref, c_ref, cp, sp, ch, sh):
        p = p_ref[...].astype(F32)
        c = c_ref[...]
        ang_p, ang_h = p * c[0:1], p * c[2:3]
        cp[...] = jnp.cos(ang_p) * jnp.abs(c[1:2])
        sp[...] = jnp.sin(ang_p) * c[1:2]
        ch[...] = jnp.cos(ang_h)
        sh[...] = jnp.sin(ang_h) * c[3:4]

    row = pl.BlockSpec((t, LANES), lambda i: (i, 0))
    return _pcall(body, name="rope_tables", grid=(S // t,),
                  in_specs=[pl.BlockSpec((t, 1), lambda i: (i, 0)), pl.BlockSpec((4, LANES), lambda i: (0, 0))],
                  out_specs=[row] * 4, out_shape=[jax.ShapeDtypeStruct((S, LANES), F32)] * 4,
                  compiler_params=_params(("parallel",)))(pos, inv_sign)


def _rope_consts():
    half_p, half_h = MLA_ROPE // 2, LANES // 2
    inv_p = 1.0 / (ROPE_THETA ** (jnp.arange(0, MLA_ROPE, 2, dtype=F32) / MLA_ROPE))
    inv_h = 1.0 / (ROPE_THETA ** (jnp.arange(0, LANES, 2, dtype=F32) / LANES))
    z = jnp.zeros((half_p,), F32)
    o = jnp.ones((half_p,), F32)
    return jnp.stack([jnp.concatenate([inv_p, z, inv_p, z]), jnp.concatenate([-o, z, o, z]),
                      jnp.concatenate([inv_h, inv_h]), jnp.concatenate([-jnp.ones((half_h,), F32), jnp.ones((half_h,), F32)])])


def _shift_rows(x, s, up):
    n = x.shape[0]
    r = lax.broadcasted_iota(jnp.int32, x.shape, 0)
    if up:
        return jnp.where(r < n - s, pltpu.roll(x, n - s, 0), 0.0)
    return jnp.where(r >= s, pltpu.roll(x, s, 0), 0.0)


def _make_shift(s):
    @jax.custom_vjp
    def sh(x):
        return _shift_rows(x, s, False)

    sh.defvjp(lambda x: (sh(x), None), lambda _, g: (_shift_rows(g, s, True),))
    return sh


def _f_conv(x, w):
    x = x.astype(F32)
    y = x * w[DN_CONV - 1:DN_CONV]
    for j in range(DN_CONV - 1):
        y = y + _make_shift(DN_CONV - 1 - j)(x) * w[j:j + 1]
    return _silu(y)


def _make_conv(name):
    def call(x, w, dy=None):
        S, C = x.shape
        col = pl.BlockSpec((S, LANES), lambda i: (0, i))
        wsp = pl.BlockSpec((DN_CONV, LANES), lambda i: (0, i))
        if dy is None:
            def body(x_ref, w_ref, o_ref):
                o_ref[...] = _f_conv(x_ref[...], w_ref[...])
            return _pcall(body, name=f"{name}_fwd", grid=(C // LANES,), in_specs=[col, wsp], out_specs=col,
                          out_shape=jax.ShapeDtypeStruct(x.shape, F32), compiler_params=_params(("parallel",)))(x, w)

        def body(x_ref, w_ref, dy_ref, dx_ref, dw_ref):
            _, vjp = jax.vjp(_f_conv, x_ref[...], w_ref[...])
            dx_ref[...], dw_ref[...] = vjp(dy_ref[...])
        return _pcall(body, name=f"{name}_bwd", grid=(C // LANES,), in_specs=[col, wsp, col], out_specs=[col, wsp],
                      out_shape=[jax.ShapeDtypeStruct(x.shape, x.dtype), jax.ShapeDtypeStruct(w.shape, F32)],
                      compiler_params=_params(("parallel",)))(x, w, dy)

    @jax.custom_vjp
    def op(x, w):
        return call(x, w)

    op.defvjp(lambda x, w: (op(x, w), (x, w)), lambda res, dy: tuple(call(*res, dy)))
    return op


def _dot_nt(a, b):
    return lax.dot_general(a.astype(BF16), b.astype(BF16), (((1,), (1,)), ((), ())), preferred_element_type=F32)


def _dot_nn(a, b):
    return jnp.dot(a.astype(BF16), b.astype(BF16), preferred_element_type=F32)


def _dot_tn(a, b):
    return lax.dot_general(a.astype(BF16), b.astype(BF16), (((0,), (0,)), ((), ())), preferred_element_type=F32)


def _mla_scores(qn, qp, kn, kp, diagonal):
    scale = MLA_QK ** -0.5
    s = _dot_nt(qn * scale, kn) + _dot_nt(qp * scale, kp)
    if diagonal:
        r = lax.broadcasted_iota(jnp.int32, s.shape, 0)
        c = lax.broadcasted_iota(jnp.int32, s.shape, 1)
        s = jnp.where(c <= r, s, NEG)
    return s


def _on_causal_pairs(qi, ki, step):
    @pl.when(ki < qi)
    def _():
        step(False)

    @pl.when(ki == qi)
    def _():
        step(True)


def _causal_pairs(n, t, by_key, heads):
    pairs = [(q, k) for k in range(n) for q in range(k, n)] if by_key else [(q, k) for q in range(n) for k in range(q + 1)]
    qt, kt = (jnp.array([p[i] for p in pairs], jnp.int32) for i in (0, 1))
    return (qt, kt, pl.BlockSpec((t, heads * LANES), lambda h, p, qt_r, kt_r: (qt_r[p], h)),
            pl.BlockSpec((t, heads * LANES), lambda h, p, qt_r, kt_r: (kt_r[p], h)))


MLA_FWD_HEADS = 4
MLA_BWD_HEADS = 2


def _make_mla_attn(name):
    scale = MLA_QK ** -0.5

    def fwd_call(qn, qp, kn, kp, v):
        S = qn.shape[0]
        t = min(S, 512)
        n = S // t
        hp = MLA_FWD_HEADS
        qt, kt, qs, ks = _causal_pairs(n, t, False, hp)
        cols = [slice(j * LANES, (j + 1) * LANES) for j in range(hp)]

        def body(qt_r, kt_r, qn_r, qp_r, kn_r, kp_r, v_r, o_r, lse_r, m_s, l_s, acc_s):
            qi, ki = qt_r[pl.program_id(1)], kt_r[pl.program_id(1)]

            @pl.when(ki == 0)
            def _():
                m_s[...] = jnp.full_like(m_s, NEG)
                l_s[...] = jnp.zeros_like(l_s)
                acc_s[...] = jnp.zeros_like(acc_s)

            def step(diagonal):
                for c in cols:
                    s = _mla_scores(qn_r[:, c], qp_r[:, c], kn_r[:, c], kp_r[:, c], diagonal)
                    m_old = m_s[:, c]
                    m_new = jnp.maximum(m_old, jnp.max(s, axis=-1, keepdims=True))
                    p = jnp.exp(s - m_new[:, :1])
                    alpha = jnp.exp(m_old - m_new)
                    l_s[:, c] = alpha * l_s[:, c] + jnp.sum(p, axis=-1, keepdims=True)
                    acc_s[:, c] = alpha * acc_s[:, c] + _dot_nn(p, v_r[:, c])
                    m_s[:, c] = m_new
            _on_causal_pairs(qi, ki, step)

            @pl.when(ki == qi)
            def _():
                o_r[...] = acc_s[...] / l_s[...]
                lse_r[...] = m_s[...] + jnp.log(l_s[...])

        spec = pltpu.PrefetchScalarGridSpec(num_scalar_prefetch=2, grid=(MLA_HEADS // hp, qt.shape[0]), in_specs=[qs, qs, ks, ks, ks],
                                            out_specs=[qs, qs], scratch_shapes=[pltpu.VMEM((t, hp * LANES), F32)] * 3)
        return _pcall(body, name=f"{name}_fwd", grid_spec=spec, out_shape=[jax.ShapeDtypeStruct((S, MLA_HEADS * LANES), F32)] * 2,
                      compiler_params=_params(("parallel", "arbitrary")))(qt, kt, qn, qp, kn, kp, v)

    def bwd_call(qn, qp, kn, kp, v, o, lse, do):
        S = qn.shape[0]
        t = min(S, 512)
        n = S // t
        hp = MLA_BWD_HEADS
        qt, kt, qs, ks = _causal_pairs(n, t, True, hp)
        head = pl.BlockSpec((S, hp * LANES), lambda h, p, qt_r, kt_r: (0, h))
        n_pairs = qt.shape[0]
        cols = [slice(j * LANES, (j + 1) * LANES) for j in range(hp)]

        def body(qt_r, kt_r, qn_r, qp_r, kn_r, kp_r, v_r, o_r, lse_r, do_r, dqn_r, dqp_r, dkn_r, dkp_r, dv_r, dkn_s, dkp_s, dv_s, dl_s):
            pair = pl.program_id(1)
            qi, ki = qt_r[pair], kt_r[pair]
            rows = pl.ds(pl.multiple_of(qi * t, t), t)

            @pl.when(pair == 0)
            def _():
                dqn_r[...] = jnp.zeros_like(dqn_r)
                dqp_r[...] = jnp.zeros_like(dqp_r)

            @pl.when(ki == 0)
            def _():
                for c in cols:
                    dl_s[rows, c] = jnp.broadcast_to(jnp.sum(do_r[:, c] * o_r[:, c], axis=-1, keepdims=True), (t, LANES))

            @pl.when(qi == ki)
            def _():
                dkn_s[...] = jnp.zeros_like(dkn_s)
                dkp_s[...] = jnp.zeros_like(dkp_s)
                dv_s[...] = jnp.zeros_like(dv_s)

            def step(diagonal):
                for c in cols:
                    qn, qp, kn, kp, do = qn_r[:, c], qp_r[:, c], kn_r[:, c], kp_r[:, c], do_r[:, c]
                    p = jnp.exp(_mla_scores(qn, qp, kn, kp, diagonal) - lse_r[:, c][:, :1])
                    ds = p * (_dot_nt(do, v_r[:, c]) - dl_s[rows, c][:, :1])
                    dv_s[:, c] += _dot_tn(p, do)
                    dkn_s[:, c] += _dot_tn(ds, qn * scale)
                    dkp_s[:, c] += _dot_tn(ds, qp * scale)
                    dqn_r[rows, c] += _dot_nn(ds, kn)
                    dqp_r[rows, c] += _dot_nn(ds, kp)
            _on_causal_pairs(qi, ki, step)

            @pl.when(qi == n - 1)
            def _():
                dkn_r[...] = dkn_s[...]
                dkp_r[...] = dkp_s[...]
                dv_r[...] = dv_s[...]

            @pl.when(pair == n_pairs - 1)
            def _():
                dqn_r[...] = dqn_r[...] * scale
                dqp_r[...] = dqp_r[...] * scale

        spec = pltpu.PrefetchScalarGridSpec(num_scalar_prefetch=2, grid=(MLA_HEADS // hp, n_pairs), in_specs=[qs, qs, ks, ks, ks, qs, qs, qs],
                                            out_specs=[head, head, ks, ks, ks],
                                            scratch_shapes=[pltpu.VMEM((t, hp * LANES), F32)] * 3 + [pltpu.VMEM((S, hp * LANES), F32)])
        return _pcall(body, name=f"{name}_bwd", grid_spec=spec, out_shape=[jax.ShapeDtypeStruct((S, MLA_HEADS * LANES), F32)] * 5,
                      compiler_params=_params(("parallel", "arbitrary")))(qt, kt, qn, qp, kn, kp, v, o, lse, do)

    @jax.custom_vjp
    def op(qn, qp, kn, kp, v):
        return fwd_call(qn, qp, kn, kp, v)[0]

    def fwd(qn, qp, kn, kp, v):
        o, lse = fwd_call(qn, qp, kn, kp, v)
        return o, (qn, qp, kn, kp, v, o, lse)

    def bwd(res, do):
        return tuple(bwd_call(*res, do))

    op.defvjp(fwd, bwd)
    return op


def _dil_block(q, kp, kc, vp, vc, has_prev):
    scale = LANES ** -0.5
    r = lax.broadcasted_iota(jnp.int32, (DIL_BLOCK, 2 * DIL_BLOCK), 0)
    c = lax.broadcasted_iota(jnp.int32, (DIL_BLOCK, 2 * DIL_BLOCK), 1)
    valid = ((c < DIL_BLOCK) & (c >= r) & has_prev) | ((c >= DIL_BLOCK) & (c - DIL_BLOCK <= r))
    s = jnp.where(valid, _dot_nt(q * scale, jnp.concatenate([kp, kc], axis=0)), NEG)
    m = jnp.max(s, axis=-1, keepdims=True)
    e = jnp.exp(s - m)
    den = jnp.sum(e, axis=-1, keepdims=True)
    o = _dot_nn(e, jnp.concatenate([vp, vc], axis=0)) / den
    return o, jnp.broadcast_to(m + jnp.log(den), o.shape)


DIL_TILE_ROWS = (1024, 1024, 2048)


def _make_dil_attn(name, d, tile_rows):
    def call(q, k, v, cts=None):
        S = q.shape[0]
        span = DIL_BLOCK * d
        G = max(1, min(tile_rows, S) // span)
        n = S // (G * span)
        at = (lambda i: i) if cts is None else (lambda i: n - 1 - i)
        tile = pl.BlockSpec((G * span, LANES), lambda h, i: (at(i), h))
        before = pl.BlockSpec((span, LANES), lambda h, i: (jnp.maximum(at(i) * G - 1, 0), h))

        def rows(r, j):
            return pl.ds(j * DIL_BLOCK, DIL_BLOCK) if d == 1 else pl.ds(r + j * span, DIL_BLOCK, stride=d)

        def over_residues(fn):
            if d == 1:
                fn(0)
            else:
                lax.fori_loop(0, d, lambda r, c: (fn(r), c)[1], 0)

        def block_inputs(r, j, q_r, kb_r, k_r, vb_r, v_r):
            kp = kb_r[rows(r, 0), :] if j == 0 else k_r[rows(r, j - 1), :]
            vp = vb_r[rows(r, 0), :] if j == 0 else v_r[rows(r, j - 1), :]
            return q_r[rows(r, j), :], kp, k_r[rows(r, j), :], vp, v_r[rows(r, j), :]

        if cts is None:
            def body(q_r, kb_r, k_r, vb_r, v_r, o_r, lse_r):
                first = at(pl.program_id(1)) * G

                def residue(r):
                    for j in range(G):
                        o_r[rows(r, j), :], lse_r[rows(r, j), :] = _dil_block(
                            *block_inputs(r, j, q_r, kb_r, k_r, vb_r, v_r), first + j > 0)
                over_residues(residue)
            return _pcall(body, name=f"{name}_fwd", grid=(DIL_GROUP_HEADS, n), in_specs=[tile, before, tile, before, tile],
                          out_specs=[tile, tile], out_shape=[jax.ShapeDtypeStruct(q.shape, F32)] * 2,
                          compiler_params=_params(("parallel", "parallel")))(q, k, k, v, v)

        def body(q_r, kb_r, k_r, vb_r, v_r, do_r, dl_r, dq_r, dk_r, dv_r, ck_s, cv_s):
            first = at(pl.program_id(1)) * G

            @pl.when(pl.program_id(1) == 0)
            def _():
                ck_s[...] = jnp.zeros_like(ck_s)
                cv_s[...] = jnp.zeros_like(cv_s)

            def residue(r):
                owed = None
                for j in range(G):
                    hp = first + j > 0
                    _, vjp = jax.vjp(lambda *a: _dil_block(*a, hp), *block_inputs(r, j, q_r, kb_r, k_r, vb_r, v_r))
                    dq, dkp, dkc, dvp, dvc = vjp((do_r[rows(r, j), :], dl_r[rows(r, j), :]))
                    dq_r[rows(r, j), :] = dq
                    if j == G - 1:
                        dkc, dvc = dkc + ck_s[rows(r, 0), :], dvc + cv_s[rows(r, 0), :]
                    dk_r[rows(r, j), :], dv_r[rows(r, j), :] = dkc, dvc
                    if j == 0:
                        owed = (dkp, dvp)
                    else:
                        dk_r[rows(r, j - 1), :] += dkp
                        dv_r[rows(r, j - 1), :] += dvp
                ck_s[rows(r, 0), :], cv_s[rows(r, 0), :] = owed
            over_residues(residue)
        return _pcall(body, name=f"{name}_bwd", grid=(DIL_GROUP_HEADS, n), in_specs=[tile, before, tile, before, tile, tile, tile],
                      out_specs=[tile] * 3, out_shape=[jax.ShapeDtypeStruct(q.shape, F32)] * 3,
                      scratch_shapes=[pltpu.VMEM((span, LANES), F32)] * 2,
                      compiler_params=_params(("parallel", "arbitrary")))(q, k, k, v, v, *cts)

    @jax.custom_vjp
    def op(q, k, v):
        return tuple(call(q, k, v))

    op.defvjp(lambda q, k, v: (op(q, k, v), (q, k, v)), lambda res, cts: tuple(call(*res, cts=cts)))
    return op


def _pdot(a, b, dims):
    return lax.dot_general(a, b, (dims, ((), ())), precision=lax.Precision.HIGH, preferred_element_type=F32)


DN_LOCAL_CHUNKS = 4


DN_BLOCK_HEADS = 4
DN_BLOCK = DN_BLOCK_HEADS * DN_CHUNK


def _inverse_cotangent(inv, d):
    return -_pdot(inv, _pdot(d, inv, ((1,), (1,))), ((0,), (0,)))


@jax.custom_vjp
def _unit_lower_inverse(a):
    n = a.shape[0]
    eye = (lax.broadcasted_iota(jnp.int32, (n, n), 0) == lax.broadcasted_iota(jnp.int32, (n, n), 1)).astype(F32)
    inv, pw = eye - a, a
    for level in range(5):
        dot = _pdot if level < 2 else (lambda x, y, dims: lax.dot_general(x, y, (dims, ((), ())), preferred_element_type=F32))
        pw = dot(pw, pw, ((1,), (0,)))
        inv = inv + dot(inv, pw, ((1,), (0,)))
    return inv


def _unit_lower_inverse_fwd(a):
    inv = _unit_lower_inverse(a)
    return inv, inv


_unit_lower_inverse.defvjp(_unit_lower_inverse_fwd, lambda inv, d: (_inverse_cotangent(inv, d),))


@jax.custom_vjp
def _known_inverse(a, inv):
    return inv


_known_inverse.defvjp(lambda a, inv: (inv, inv), lambda inv, d: (_inverse_cotangent(inv, d), jnp.zeros_like(inv)))


def _dn_local(q, k, v, g, b, known=None):
    C, R = DN_CHUNK, DN_BLOCK
    r = lax.broadcasted_iota(jnp.int32, (R, R), 0)
    c = lax.broadcasted_iota(jnp.int32, (R, R), 1)
    same_head = (r // C) == (c // C)
    incl, strict = same_head & (r >= c), same_head & (r > c)
    avg = jnp.full((R, LANES), 1.0 / LANES, F32)
    rc = lax.broadcasted_iota(jnp.int32, (C, C), 0) >= lax.broadcasted_iota(jnp.int32, (C, C), 1)
    gc_lanes = _pdot(rc.astype(F32), g, ((1,), (0,)))
    us, ws, qes, kds, qks, invs = [], [], [], [], [], []
    for first in range(0, DN_HEADS, DN_BLOCK_HEADS):
        stack = lambda x: jnp.concatenate(_heads(x)[first:first + DN_BLOCK_HEADS], axis=0)
        unstack = lambda x: [x[p * C:(p + 1) * C] for p in range(DN_BLOCK_HEADS)]
        gc, q_s, k_s, v_s, b_s = (stack(x) for x in (gc_lanes, q, k, v, b))
        gc_j = _pdot(avg, gc, ((1,), (1,)))
        decay = jnp.exp(jnp.where(incl, _cat([gc] * (R // LANES)) - gc_j, NEG))
        kb = k_s * b_s
        kk = _pdot(jnp.concatenate([kb, q_s], axis=0), k_s, ((1,), (1,)))
        a = jnp.where(strict, kk[:R] * decay, 0.0)
        inv = _unit_lower_inverse(a) if known is None else _known_inverse(a, known[len(invs) * R:(len(invs) + 1) * R])
        invs.append(inv)
        eg = jnp.exp(gc)
        uw = _pdot(inv, _cat([v_s * b_s, kb * eg]), ((1,), (0,)))
        g_last = jnp.concatenate([jnp.broadcast_to(x[C - 1:C], (C, LANES)) for x in unstack(gc)], axis=0)
        us += unstack(uw[:, :LANES])
        ws += unstack(uw[:, LANES:])
        qes += unstack(q_s * eg)
        kds += unstack(k_s * jnp.exp(g_last - gc))
        qks.append(kk[R:] * decay)
    egl = jnp.broadcast_to(jnp.exp(gc_lanes[C - 1:C]), (8, DN_HEADS * LANES))
    return _cat(us), _cat(ws), _cat(qes), _cat(kds), jnp.concatenate(qks, axis=0), egl, jnp.concatenate(invs, axis=0)


def _dn_scan(u, w, qe, kd, qk, egl, state):
    C = DN_CHUNK
    heads = [slice(h * LANES, (h + 1) * LANES) for h in range(DN_HEADS)]
    ws = [_pdot(jnp.concatenate([w[:, sl], qe[:, sl]], axis=0), state[sl, :], ((1,), (0,))) for sl in heads]
    v_new = [u[:, sl] - x[:C] for sl, x in zip(heads, ws)]
    local = []
    for i, first in enumerate(range(0, DN_HEADS, DN_BLOCK_HEADS)):
        y = _pdot(qk[i * DN_BLOCK:(i + 1) * DN_BLOCK], jnp.concatenate(v_new[first:first + DN_BLOCK_HEADS], axis=0), ((1,), (0,)))
        local += [y[p * C:(p + 1) * C] for p in range(DN_BLOCK_HEADS)]
    o = _cat([x[C:] + y for x, y in zip(ws, local)])
    states = [state[sl, :] * egl[0:1, sl] + _pdot(kd[:, sl], vn, ((0,), (0,))) for sl, vn in zip(heads, v_new)]
    return o, jnp.concatenate(states, axis=0)


def _make_delta_rule(name):
    W = DN_HEADS * LANES
    QK = DN_HEADS * DN_CHUNK

    def local_call(ins, cts=None):
        S = ins[0].shape[0]
        n = S // DN_CHUNK
        per = math.gcd(DN_LOCAL_CHUNKS, n)
        row = pl.BlockSpec((per * DN_CHUNK, W), lambda i: (i, 0))
        qkb = pl.BlockSpec((per * QK, DN_BLOCK), lambda i: (i, 0))
        eg = pl.BlockSpec((per, 8, W), lambda i: (i, 0, 0))
        rows = lambda j: slice(j * DN_CHUNK, (j + 1) * DN_CHUNK)
        qk_rows = lambda j: slice(j * QK, (j + 1) * QK)
        out_rows = [rows, rows, rows, rows, qk_rows]

        if cts is None:
            def body(*refs):
                for j in range(per):
                    res = _dn_local(*[r[rows(j), :] for r in refs[:5]])
                    for o_r, o, at_ in zip(refs[5:10], res[:5], out_rows):
                        o_r[at_(j), :] = o
                    refs[10][j] = res[5]
                    refs[11][qk_rows(j), :] = res[6]
            blockdiag = jax.ShapeDtypeStruct((n * QK, DN_BLOCK), F32)
            return _pcall(body, name=f"{name}_local_fwd", grid=(n // per,), in_specs=[row] * 5, out_specs=[row] * 4 + [qkb, eg, qkb],
                          out_shape=[jax.ShapeDtypeStruct((S, W), F32)] * 4 + [blockdiag, jax.ShapeDtypeStruct((n, 8, W), F32), blockdiag],
                          compiler_params=_params(("parallel",)))(*ins)

        def body(*refs):
            for j in range(per):
                known = refs[5][qk_rows(j), :]
                _, vjp = jax.vjp(lambda *a: _dn_local(*a, known=known)[:6], *[r[rows(j), :] for r in refs[:5]])
                grads = vjp(tuple(r[at_(j), :] for r, at_ in zip(refs[6:11], out_rows)) + (refs[11][j],))
                for o_r, o in zip(refs[12:], grads):
                    o_r[rows(j), :] = o
        return _pcall(body, name=f"{name}_local_bwd", grid=(n // per,), in_specs=[row] * 5 + [qkb] + [row] * 4 + [qkb, eg],
                      out_specs=[row] * 5, out_shape=[jax.ShapeDtypeStruct((S, W), F32)] * 5,
                      compiler_params=_params(("parallel",)))(*ins, *cts)

    def scan_call(ins, saved=None, do=None):
        S = ins[0].shape[0]
        n = S // DN_CHUNK
        at = (lambda i: i) if do is None else (lambda i: n - 1 - i)
        row = pl.BlockSpec((DN_CHUNK, W), lambda i: (at(i), 0))
        qkb = pl.BlockSpec((QK, DN_BLOCK), lambda i: (at(i), 0))
        eg = pl.BlockSpec((None, 8, W), lambda i: (at(i), 0, 0))
        st = pl.BlockSpec((None, W, LANES), lambda i: (at(i), 0, 0))

        if do is None:
            def body(*refs):
                o_r, st_r, s_s = refs[6:]

                @pl.when(pl.program_id(0) == 0)
                def _():
                    s_s[...] = jnp.zeros_like(s_s)
                st_r[...] = s_s[...]
                o_r[...], s_s[...] = _dn_scan(*[r[...] for r in refs[:6]], s_s[...])
            return _pcall(body, name=f"{name}_scan_fwd", grid=(n,), in_specs=[row] * 4 + [qkb, eg], out_specs=[row, st],
                          out_shape=[jax.ShapeDtypeStruct((S, W), F32), jax.ShapeDtypeStruct((n, W, LANES), F32)],
                          scratch_shapes=[pltpu.VMEM((W, LANES), F32)], compiler_params=_params(("arbitrary",)))(*ins)

        def body(*refs):
            st_r, do_r = refs[6:8]
            outs, ds_s = refs[8:14], refs[14]

            @pl.when(pl.program_id(0) == 0)
            def _():
                ds_s[...] = jnp.zeros_like(ds_s)
            _, vjp = jax.vjp(_dn_scan, *[r[...] for r in refs[:6]], st_r[...])
            *grads, ds = vjp((do_r[...], ds_s[...]))
            for o_r, gval in zip(outs, grads):
                o_r[...] = gval
            ds_s[...] = ds
        return _pcall(body, name=f"{name}_scan_bwd", grid=(n,), in_specs=[row] * 4 + [qkb, eg, st, row], out_specs=[row] * 4 + [qkb, eg],
                      out_shape=[jax.ShapeDtypeStruct((S, W), F32)] * 4
                      + [jax.ShapeDtypeStruct((n * QK, DN_BLOCK), F32), jax.ShapeDtypeStruct((n, 8, W), F32)],
                      scratch_shapes=[pltpu.VMEM((W, LANES), F32)], compiler_params=_params(("arbitrary",)))(*ins, saved, do)

    @jax.custom_vjp
    def local(q, k, v, g, b):
        return tuple(local_call((q, k, v, g, b))[:6])

    def local_fwd(*a):
        *outs, inverses = local_call(a)
        return tuple(outs), (*a, inverses)

    local.defvjp(local_fwd, lambda res, cts: tuple(local_call(res, tuple(cts))))

    @jax.custom_vjp
    def scan(u, w, qe, kd, qk, egl):
        return scan_call((u, w, qe, kd, qk, egl))[0]

    def scan_fwd(*a):
        o, states = scan_call(a)
        return o, (a, states)

    scan.defvjp(scan_fwd, lambda res, do: tuple(scan_call(res[0], res[1], do)))
    return lambda q, k, v, g, b: scan(*local(q, k, v, g, b))


def _loss_call(y, target):
    S, D = y.shape
    t = min(S, 512)
    n = S // t
    row = pl.BlockSpec((t, D), lambda i: (i, 0))

    def body(y_r, t_r, loss_r, dy_r, acc_s):
        i = pl.program_id(0)

        @pl.when(i == 0)
        def _():
            acc_s[...] = jnp.zeros_like(acc_s)
        err = y_r[...] - t_r[...]
        dy_r[...] = err * (1.0 / D)
        acc_s[...] += jnp.sum(err * err, axis=0, keepdims=True)

        @pl.when(i == n - 1)
        def _():
            loss_r[...] = jnp.broadcast_to(jnp.sum(acc_s[...], axis=1, keepdims=True) * (0.5 / D), loss_r.shape)

    return _pcall(body, name="loss_head", grid=(n,), in_specs=[row, row],
                  out_specs=[pl.BlockSpec((8, LANES), lambda i: (0, 0)), row],
                  out_shape=[jax.ShapeDtypeStruct((8, LANES), F32), jax.ShapeDtypeStruct((S, D), F32)],
                  scratch_shapes=[pltpu.VMEM((1, D), F32)], compiler_params=_params(("arbitrary",)))(y, target)


def _adamw_layer(name, part, w, m, v, layer, earlier=None, rows=128):
    L, R, C = w.shape
    t = _tile(R, rows, 8)
    row = pl.BlockSpec((None, t, C), lambda i: (layer, i, 0))

    def body(p_r, w_r, m_r, v_r, *rest):
        g_r, d_r, nm_r, nv_r = rest[-4:]
        g = p_r[0].astype(F32)
        for s in range(1, N_DEV):
            g = g + p_r[s].astype(F32)
        m_new = ADAM_B1 * m_r[...] + (1.0 - ADAM_B1) * g
        v_new = ADAM_B2 * v_r[...] + (1.0 - ADAM_B2) * (g * g)
        m_hat = m_new / (1.0 - ADAM_B1 ** ADAM_STEP)
        v_hat = v_new / (1.0 - ADAM_B2 ** ADAM_STEP)
        g_r[...] = g
        d_r[...] = -ADAM_LR * (m_hat / (jnp.sqrt(v_hat) + ADAM_EPS) + ADAM_WD * w_r[...])
        nm_r[...] = m_new
        nv_r[...] = v_new

    extra = [] if earlier is None else list(earlier)
    return _pcall(body, name=name, grid=(R // t,),
                  in_specs=[pl.BlockSpec((N_DEV, t, C), lambda i: (0, i, 0)), row, row, row] + [pl.BlockSpec(memory_space=pl.ANY)] * len(extra),
                  out_specs=[row] * 4, out_shape=[jax.ShapeDtypeStruct((L, R, C), F32)] * 4,
                  input_output_aliases={4 + k: k for k in range(len(extra))},
                  compiler_params=_params(("parallel",)))(part, w, m, v, *extra)


def _my_place():
    x, y, c = lax.axis_index("x"), lax.axis_index("y"), lax.axis_index("c")
    return x, y, c


def _index(x, y, c):
    return 4 * x + 2 * y + c


def _all_gather(vs):
    n = len(vs)

    def body(*refs):
        v_refs, out_refs = refs[:n], refs[n:2 * n]
        send_sems, recv_sems, local_sems = refs[2 * n:]
        x, y, c = _my_place()
        me, sibling = (x, y, c), (x, y, 1 - c)
        chips = [(1 - x, y), (x, 1 - y), (1 - x, 1 - y)]

        def copy(a, k, block, to, src=None):
            rows = out_refs[a].at[_index(*block)]
            return pltpu.make_async_remote_copy(src_ref=rows if src is None else src, dst_ref=rows, send_sem=send_sems.at[a, k],
                                                recv_sem=recv_sems.at[a, k], device_id=to, device_id_type=MESH)

        mine = [pltpu.make_async_copy(v_refs[a], out_refs[a].at[_index(*me)], local_sems.at[a]) for a in range(n)]
        first, passed = [], []
        for a in range(n):
            mine[a].start()
            first += [copy(a, 0, me, sibling, src=v_refs[a])]
            first += [copy(a, 1 + j, me, (*chip, c), src=v_refs[a]) for j, chip in enumerate(chips)]
        for cp in first:
            cp.start()
        for j, chip in enumerate(chips):
            for a in range(n):
                copy(a, 1 + j, (*chip, c), me).wait_recv()
                passed.append(copy(a, 4 + j, (*chip, c), sibling))
                passed[-1].start()
        for a in range(n):
            copy(a, 0, sibling, me).wait_recv()
            for j, chip in enumerate(chips):
                copy(a, 4 + j, (*chip, 1 - c), me).wait_recv()
        for cp in first + passed:
            cp.wait_send()
        for a in range(n):
            mine[a].wait()

    any_ = pl.BlockSpec(memory_space=pl.ANY)
    return _pcall(body, name="gather_weights", in_specs=[any_] * n, out_specs=[any_] * n,
                  out_shape=[jax.ShapeDtypeStruct((N_DEV,) + v.shape, v.dtype) for v in vs],
                  scratch_shapes=[pltpu.SemaphoreType.DMA((n, 7)), pltpu.SemaphoreType.DMA((n, 7)), pltpu.SemaphoreType.DMA((n,))])(*vs)


def _all_to_all(vs, after):
    n = len(vs)

    def body(*refs):
        v_refs, out_refs = refs[:n], refs[n + 1:2 * n + 1]
        send_sems, recv_sems, local_sems = refs[2 * n + 1:]
        x, y, c = _my_place()
        me = _index(x, y, c)
        mine = [pltpu.make_async_copy(v_refs[a].at[me], out_refs[a].at[me], local_sems.at[a]) for a in range(n)]
        copies = []
        for a in range(n):
            mine[a].start()
        for k in range(1, N_DEV):
            px = 1 - x if k & 4 else x
            py = 1 - y if k & 2 else y
            pc = 1 - c if k & 1 else c
            for a in range(n):
                cp = pltpu.make_async_remote_copy(src_ref=v_refs[a].at[_index(px, py, pc)], dst_ref=out_refs[a].at[me],
                                                  send_sem=send_sems.at[a, k - 1], recv_sem=recv_sems.at[a, k - 1],
                                                  device_id=(px, py, pc), device_id_type=MESH)
                cp.start()
                copies.append(cp)
        for cp in copies:
            cp.wait()
        for a in range(n):
            mine[a].wait()

    any_ = pl.BlockSpec(memory_space=pl.ANY)
    return _pcall(body, name="exchange_vectors", in_specs=[any_] * (n + 1), out_specs=[any_] * n,
                  out_shape=[jax.ShapeDtypeStruct(v.shape, v.dtype) for v in vs],
                  scratch_shapes=[pltpu.SemaphoreType.DMA((n, 7)), pltpu.SemaphoreType.DMA((n, 7)), pltpu.SemaphoreType.DMA((n,))])(*vs, after)


_HBM = pl.BlockSpec(memory_space=pltpu.HBM)
_SEM = pl.BlockSpec(memory_space=pltpu.SEMAPHORE)
_EFFECT = pltpu.SideEffectType.DATAFLOW_SIDE_EFFECTING


def _direct_copies(gather, v_refs, land_refs, send_sems, recv_sems, local_sems):
    x, y, c = _my_place()
    me = _index(x, y, c)
    local, remote = [], []
    for a, (v_ref, land_ref) in enumerate(zip(v_refs, land_refs)):
        local.append(pltpu.make_async_copy(v_ref if gather else v_ref.at[me], land_ref.at[me], local_sems.at[a]))
    for k in range(1, N_DEV):
        px = 1 - x if k & 4 else x
        py = 1 - y if k & 2 else y
        pc = 1 - c if k & 1 else c
        for a, (v_ref, land_ref) in enumerate(zip(v_refs, land_refs)):
            sem = a * (N_DEV - 1) + k - 1
            remote.append(pltpu.make_async_remote_copy(
                src_ref=v_ref if gather else v_ref.at[_index(px, py, pc)], dst_ref=land_ref.at[me], send_sem=send_sems.at[sem],
                recv_sem=recv_sems.at[sem], device_id=(px, py, pc), device_id_type=MESH))
    return local, remote


def _exchange_start(name, vs, gather, thru):
    n = len(vs)
    lands = [lax.empty((N_DEV,) + v.shape if gather else v.shape, v.dtype) for v in vs]

    def body(*refs):
        v_refs, land_refs = refs[:n], refs[n:2 * n]
        send_sems, recv_sems, local_sems = refs[2 * n + 1:2 * n + 4]
        local, remote = _direct_copies(gather, v_refs, land_refs, send_sems, recv_sems, local_sems)
        for cp in local + remote:
            cp.start()

    hbm = lambda a: pltpu.HBM(a.shape, a.dtype)
    res = _pcall(body, name=name,
                 out_shape=(pltpu.SemaphoreType.DMA((n * (N_DEV - 1),)), pltpu.SemaphoreType.DMA((n * (N_DEV - 1),)), pltpu.SemaphoreType.DMA((n,)),
                            *[hbm(a) for a in (*vs, *lands, thru)]),
                 in_specs=[_HBM] * (2 * n + 1), out_specs=(_SEM, _SEM, _SEM, *[_HBM] * (2 * n + 1)),
                 input_output_aliases={i: 3 + i for i in range(2 * n + 1)},
                 compiler_params=pltpu.CompilerParams(has_side_effects=_EFFECT))(
        *[pltpu.with_memory_space_constraint(a, pltpu.HBM) for a in (*vs, *lands, thru)])
    return (gather, res[:3], res[3:3 + n], res[3 + n:3 + 2 * n]), res[3 + 2 * n]


def _exchange_wait(name, started, after):
    gather, sems, vs, lands = started
    n = len(vs)

    def body(*refs):
        v_refs, land_refs = refs[:n], refs[n:2 * n]
        send_sems, recv_sems, local_sems = refs[2 * n:2 * n + 3]
        local, remote = _direct_copies(gather, v_refs, land_refs, send_sems, recv_sems, local_sems)
        for cp in local:
            cp.wait()
        for cp in remote:
            cp.wait_send()
            cp.wait_recv()

    hbm = lambda a: pltpu.HBM(a.shape, a.dtype)
    res = _pcall(body, name=name, out_shape=tuple(hbm(a) for a in (*vs, *lands)),
                 in_specs=[_HBM] * (2 * n) + [_SEM] * 3 + [pl.BlockSpec(memory_space=pl.ANY)], out_specs=tuple([_HBM] * (2 * n)),
                 input_output_aliases={i: i for i in range(2 * n)},
                 compiler_params=pltpu.CompilerParams(has_side_effects=_EFFECT))(*vs, *lands, *sems, after)
    return list(res[n:])


W_IN_SHARD = IN_WIDTH // N_DEV
SEG_ORDER = ("q_lat", "c_kv", "k_pe", "z_a", "dn_qkv", "dn_ab", "z_b", "dil_qkv", "z_c", "gate")
SEG_WIDTH = (384, 256, LANES, 512, 1536, LANES, 512, 4608, 512, 3072)


def _w_in_plan():
    plan = []

    def add(seg, c0, c1, dst):
        while c0 < c1:
            d = c0 // W_IN_SHARD
            e = min(c1, (d + 1) * W_IN_SHARD)
            plan.append((seg, dst, d, c0 - d * W_IN_SHARD, e - c0))
            dst += e - c0
            c0 = e

    half = MLA_ROPE // 2
    for i, name in enumerate(SEG_ORDER):
        if name == "k_pe":
            o = _SEG["k_pe"][0]
            add(i, o, o + half, 0)
            add(i, o + half, o + 2 * half, LANES // 2)
        elif name == "dn_ab":
            o = _SEG["dn_a"][0]
            add(i, o, o + 2 * DN_HEADS, 0)
        else:
            o, w = _SEG[name]
            add(i, o, o + w, 0)
    return plan


def _make_w_in_segments(name):
    plan = _w_in_plan()
    nseg = len(SEG_ORDER)
    t = 256

    def fwd_call(g):
        L = g.shape[1]

        def body(g_ref, *o_refs):
            for i in (SEG_ORDER.index("k_pe"), SEG_ORDER.index("dn_ab")):
                o_refs[i][...] = jnp.zeros_like(o_refs[i])
            for seg, dst, d, src, n in plan:
                o_refs[seg][:, dst:dst + n] = g_ref[d, :, src:src + n]

        return _pcall(body, name=f"{name}_fwd", grid=(L, D_MODEL // t),
                      in_specs=[pl.BlockSpec((N_DEV, None, t, W_IN_SHARD), lambda l, i: (0, l, i, 0))],
                      out_specs=[pl.BlockSpec((None, t, w), lambda l, i: (l, i, 0)) for w in SEG_WIDTH],
                      out_shape=[jax.ShapeDtypeStruct((L, D_MODEL, w), g.dtype) for w in SEG_WIDTH],
                      compiler_params=_params(("parallel", "parallel")))(g)

    def bwd_call(ds):
        L = ds[0].shape[0]

        def body(*refs):
            d_refs, g_ref = refs[:nseg], refs[nseg]
            for seg, dst, d, src, n in plan:
                g_ref[d, :, src:src + n] = d_refs[seg][:, dst:dst + n]

        return _pcall(body, name=f"{name}_bwd", grid=(L, D_MODEL // t),
                      in_specs=[pl.BlockSpec((None, t, w), lambda l, i: (l, i, 0)) for w in SEG_WIDTH],
                      out_specs=pl.BlockSpec((N_DEV, None, t, W_IN_SHARD), lambda l, i: (0, l, i, 0)),
                      out_shape=jax.ShapeDtypeStruct((N_DEV, L, D_MODEL, W_IN_SHARD), ds[0].dtype),
                      compiler_params=_params(("parallel", "parallel")))(*ds)

    @jax.custom_vjp
    def op(g):
        return tuple(fwd_call(g))

    op.defvjp(lambda g: (op(g), None), lambda _, ds: (bwd_call(tuple(ds)),))
    return op


def _pe_pad(a):
    h = MLA_ROPE // 2
    z = jnp.zeros(a.shape[:-1] + (h,), a.dtype)
    return jnp.concatenate([a[..., :h], z, a[..., h:], z], axis=-1)


def _layer_norm(tag, x, norm_g):
    return _make_rowwise(f"{tag}_norm", _f_norm, 512)((x,), (), (norm_g[None, :],), ())[0]


def _layer(tag, x, tables, W):
    h = _layer_norm(tag, x, W["norm_g"])
    return _layer_tail(tag, x, _make_multi_linear(f"{tag}_inproj", 10, INPROJ_DTYPE)(h, W["w_in_segments"]), tables, W)


def _layer_tail(tag, x, segments, tables, W):
    cos_p, sin_p, cos_h, sin_h = tables
    row = lambda a: a[None, :]
    q_lat, c_kv, kpe, z_a, dn_qkv, ab, z_b, dil_qkv, z_c, gl = segments

    qn_lat, ckvn, kp = _make_rowwise(f"{tag}_mla_a", _f_mla_a, 512)(
        (q_lat, c_kv, kpe), (cos_p, sin_p),
        (row(W["mla_q_a_norm_g"]), row(W["mla_kv_a_norm_g"]), row(_pe_pad(W["mla_k_norm_g"][LANES:]))), ())
    wq = W["mla_w_q_b"].reshape(MLA_Q_RANK, MLA_HEADS, MLA_QK)
    wq = jnp.concatenate([wq[:, :, :LANES].reshape(MLA_Q_RANK, -1), _pe_pad(wq[:, :, LANES:]).reshape(MLA_Q_RANK, -1)], axis=1)
    wkv = W["mla_w_kv_b"].reshape(MLA_KV_RANK, MLA_HEADS, 2 * LANES)
    (q8,) = _make_multi_linear(f"{tag}_qb", 1)(qn_lat, (wq,))
    kn_raw, v_mla = _make_multi_linear(f"{tag}_kvb", 2)(
        ckvn, (wkv[:, :, :LANES].reshape(MLA_KV_RANK, -1), wkv[:, :, LANES:].reshape(MLA_KV_RANK, -1)))
    qn, qp, kn = _make_rowwise(f"{tag}_mla_b", _f_mla_b, 512)(
        (q8, kn_raw), (cos_p, sin_p),
        (row(W["mla_q_norm_g"][:LANES]), row(_pe_pad(W["mla_q_norm_g"][LANES:])), row(W["mla_k_norm_g"][:LANES])), ())
    y_a = _make_mla_attn(f"{tag}_mla")(qn, qp, kn, kp, v_mla)

    mixed = _make_conv(f"{tag}_conv")(dn_qkv, W["dn_conv_w"])
    lane_head = jnp.arange(DN_HEADS * LANES) // LANES
    e_a = (jnp.arange(LANES)[:, None] == lane_head[None, :]).astype(F32)
    e_b = (jnp.arange(LANES)[:, None] == lane_head[None, :] + DN_HEADS).astype(F32)
    q_dn, k_dn, v_dn, g_dn, b_dn = _make_rowwise(f"{tag}_dn_pre", _f_dn_pre, 512)(
        (mixed, ab), (), (row(jnp.repeat(W["dn_a_log"], LANES)), row(jnp.repeat(W["dn_dt_bias"], LANES))), (e_a, e_b))
    o_dn = _make_delta_rule(f"{tag}_dn")(q_dn, k_dn, v_dn, g_dn, b_dn)

    qkv_dil = _make_rowwise(f"{tag}_dil_pre", _f_dil_pre, 256)(
        (dil_qkv,), (cos_h, sin_h), (row(W["dil_q_norm_g"]), row(W["dil_k_norm_g"])), ())
    n_groups = len(DIL_DILATIONS)
    o_lse = [_make_dil_attn(f"{tag}_dil{g}", d, DIL_TILE_ROWS[g])(qkv_dil[g], qkv_dil[n_groups + g], qkv_dil[2 * n_groups + g])
             for g, d in enumerate(DIL_DILATIONS)]

    ya, yb, yc = _make_rowwise(f"{tag}_merge_a", _f_merge_a, 256)(
        (y_a, z_a, o_dn, z_b, *[o for o, _ in o_lse], *[l for _, l in o_lse], z_c), (), (row(W["dn_out_norm_g"]),), ())
    (b0,) = _make_multi_linear(f"{tag}_br0", 1)(ya, (W["w_branch"][0],))
    (b1,) = _make_multi_linear(f"{tag}_br1", 1)(yb, (W["w_branch"][1],))
    (b2,) = _make_multi_linear(f"{tag}_br2", 1)(yc, (W["w_branch"][2],))
    (mix,) = _make_rowwise(f"{tag}_merge_b", _f_merge_b, 256)((b0, b1, b2, gl), (), (), ())
    return _make_resid_linear(f"{tag}_out")(x, mix, W["w_out"])


SHARDED = (("w_in", (D_MODEL, W_IN_SHARD)), ("mla_w_q_b", (MLA_Q_RANK, MLA_HEADS * MLA_QK // N_DEV)),
           ("mla_w_kv_b", (MLA_KV_RANK, MLA_HEADS * 2 * LANES // N_DEV)), ("w_branch", (3 * BRANCH_W, D_MODEL // N_DEV)),
           ("w_out", (D_MODEL // N_DEV, D_MODEL)), ("dn_conv_w", (DN_CONV, 3 * DN_HEADS * LANES // N_DEV)))
SMALL = (("norm_g", D_MODEL), ("mla_q_a_norm_g", MLA_Q_RANK), ("mla_kv_a_norm_g", MLA_KV_RANK), ("mla_q_norm_g", MLA_QK),
         ("mla_k_norm_g", MLA_QK), ("dn_a_log", DN_HEADS), ("dn_dt_bias", DN_HEADS), ("dn_out_norm_g", LANES),
         ("dil_q_norm_g", LANES), ("dil_k_norm_g", LANES))
WEIGHTS = ("norm_g", "w_in", "mla_q_a_norm_g", "mla_w_q_b", "mla_kv_a_norm_g", "mla_w_kv_b", "mla_q_norm_g", "mla_k_norm_g",
           "dn_conv_w", "dn_a_log", "dn_dt_bias", "dn_out_norm_g", "dil_q_norm_g", "dil_k_norm_g", "w_branch", "w_out")


def _round_up(n, m):
    return -(-n // m) * m


def _pack_vectors(pieces):
    return jnp.concatenate([jnp.pad(p, (0, _round_up(p.shape[0], LANES) - p.shape[0])) for p in pieces]).reshape(-1, LANES)


def _unpack_vectors(flat, sizes):
    out, off = [], 0
    flat = flat.reshape(-1)
    for n in sizes:
        out.append(flat[off:off + n])
        off += _round_up(n, LANES)
    return out


def _whole_weights(g, small):
    W = dict(small)
    W["mla_w_q_b"] = g["mla_w_q_b"].transpose(1, 0, 2).reshape(MLA_Q_RANK, -1)
    W["mla_w_kv_b"] = g["mla_w_kv_b"].transpose(1, 0, 2).reshape(MLA_KV_RANK, -1)
    W["w_branch"] = g["w_branch"].reshape(N_DEV, 3, BRANCH_W, -1).transpose(1, 2, 0, 3).reshape(3, BRANCH_W, D_MODEL)
    W["w_out"] = g["w_out"].reshape(D_MODEL, D_MODEL)
    W["dn_conv_w"] = g["dn_conv_w"].transpose(1, 0, 2).reshape(DN_CONV, -1)
    return W


def kernel(x, positions, norm_g, w_in, mla_q_a_norm_g, mla_w_q_b, mla_kv_a_norm_g, mla_w_kv_b, mla_q_norm_g, mla_k_norm_g, dn_conv_w, dn_a_log, dn_dt_bias, dn_out_norm_g, dil_q_norm_g, dil_k_norm_g, w_branch, w_out, loss_target, m_norm_g, m_w_in, m_mla_q_a_norm_g, m_mla_w_q_b, m_mla_kv_a_norm_g, m_mla_w_kv_b, m_mla_q_norm_g, m_mla_k_norm_g, m_dn_conv_w, m_dn_a_log, m_dn_dt_bias, m_dn_out_norm_g, m_dil_q_norm_g, m_dil_k_norm_g, m_w_branch, m_w_out, v_norm_g, v_w_in, v_mla_q_a_norm_g, v_mla_w_q_b, v_mla_kv_a_norm_g, v_mla_w_kv_b, v_mla_q_norm_g, v_mla_k_norm_g, v_dn_conv_w, v_dn_a_log, v_dn_dt_bias, v_dn_out_norm_g, v_dil_q_norm_g, v_dil_k_norm_g, v_w_branch, v_w_out):
    w = dict(norm_g=norm_g, w_in=w_in, mla_q_a_norm_g=mla_q_a_norm_g, mla_w_q_b=mla_w_q_b, mla_kv_a_norm_g=mla_kv_a_norm_g,
             mla_w_kv_b=mla_w_kv_b, mla_q_norm_g=mla_q_norm_g, mla_k_norm_g=mla_k_norm_g, dn_conv_w=dn_conv_w, dn_a_log=dn_a_log,
             dn_dt_bias=dn_dt_bias, dn_out_norm_g=dn_out_norm_g, dil_q_norm_g=dil_q_norm_g, dil_k_norm_g=dil_k_norm_g,
             w_branch=w_branch, w_out=w_out)
    m = dict(norm_g=m_norm_g, w_in=m_w_in, mla_q_a_norm_g=m_mla_q_a_norm_g, mla_w_q_b=m_mla_w_q_b, mla_kv_a_norm_g=m_mla_kv_a_norm_g,
             mla_w_kv_b=m_mla_w_kv_b, mla_q_norm_g=m_mla_q_norm_g, mla_k_norm_g=m_mla_k_norm_g, dn_conv_w=m_dn_conv_w,
             dn_a_log=m_dn_a_log, dn_dt_bias=m_dn_dt_bias, dn_out_norm_g=m_dn_out_norm_g, dil_q_norm_g=m_dil_q_norm_g,
             dil_k_norm_g=m_dil_k_norm_g, w_branch=m_w_branch, w_out=m_w_out)
    v = dict(norm_g=v_norm_g, w_in=v_w_in, mla_q_a_norm_g=v_mla_q_a_norm_g, mla_w_q_b=v_mla_w_q_b, mla_kv_a_norm_g=v_mla_kv_a_norm_g,
             mla_w_kv_b=v_mla_w_kv_b, mla_q_norm_g=v_mla_q_norm_g, mla_k_norm_g=v_mla_k_norm_g, dn_conv_w=v_dn_conv_w,
             dn_a_log=v_dn_a_log, dn_dt_bias=v_dn_dt_bias, dn_out_norm_g=v_dn_out_norm_g, dil_q_norm_g=v_dil_q_norm_g,
             dil_k_norm_g=v_dil_k_norm_g, w_branch=v_w_branch, w_out=v_w_out)
    x2, target = x[0], loss_target[0]
    pos = positions[0][:, None]

    names = [n for n, _ in SHARDED]
    view = lambda t, n, s: t[n].reshape((DEPTH,) + s)
    shards = [[(view(w, n, s) if n == "dn_conv_w" else view(w, n, s).astype(BF16))[l] for n, s in SHARDED] for l in range(DEPTH)]
    small = [{n: w[n][l] for n, _ in SMALL} for l in range(DEPTH)]
    (w_in0,) = _all_gather(shards[0][:1])
    gathering0, w_in0 = _exchange_start("gather_layer0_others_start", shards[0][1:], True, w_in0)
    gathering1, w_in0 = _exchange_start("gather_layer1_start", shards[1], True, w_in0)
    tables = _rope_tables(pos, _rope_consts())

    def layer(l, g, small_l, x_l, pending=None):
        tag = f"l{l}"
        seg_op = _make_w_in_segments(f"{tag}_w_in_segments")
        w_segs, vjp_segs = jax.vjp(lambda gw: tuple(s[0] for s in seg_op(gw[:, None])), g["w_in"])
        h, vjp_norm = jax.vjp(lambda x_, ng: _layer_norm(tag, x_, ng), x_l, small_l["norm_g"])
        wide = [i for i in range(len(w_segs)) if i not in INPROJ_NARROW]
        segs = dict(zip(INPROJ_NARROW, _mm_nn_many(f"{tag}_inproj_fwd_narrow", h, [w_segs[i] for i in INPROJ_NARROW], INPROJ_DTYPE)))
        segs.update({i: _mm(f"{tag}_inproj_fwd{i}", h, w_segs[i], "nn", out_dtype=INPROJ_DTYPE, **INPROJ_TILES["nn"]) for i in wide})
        segs = tuple(segs[i] for i in range(len(w_segs)))
        if pending is not None:
            g = dict(g, **dict(zip(names[1:], _exchange_wait(f"gather_{tag}_others_wait", pending, segs[7]))))
        rest_g = {n: a for n, a in g.items() if n != "w_in"}
        rest_s = {n: a for n, a in small_l.items() if n != "norm_g"}
        y, vjp_tail = jax.vjp(lambda sg, x_, gg, ss: _layer_tail(tag, x_, sg, tables, _whole_weights(gg, ss)), segs, x_l, rest_g, rest_s)

        def backward(dy):
            dsegs, dx_skip, d_rest_g, d_rest_s = vjp_tail(dy)
            others, big = _exchange_start(f"exchange_{tag}_others_start", [d_rest_g[n] for n in names if n != "w_in"], False, dsegs[7])
            dsegs = tuple(dsegs[:7]) + (big,) + tuple(dsegs[8:])
            dws = dict(zip(INPROJ_NARROW, _mm_tn_many(f"{tag}_inproj_dw_narrow", h, [dsegs[i] for i in INPROJ_NARROW], w_segs[0].dtype)))
            dws.update({i: _mm(f"{tag}_inproj_dw{i}", h, dsegs[i], "tn", out_dtype=w_segs[i].dtype, **INPROJ_TILES["tn"]) for i in wide})
            dws = tuple(dws[i] for i in range(len(w_segs)))
            projection, first =_exchange_start(f"exchange_{tag}_w_in_start", [vjp_segs(dws)[0]], False, dsegs[0])
            dsegs = (first,) + tuple(dsegs[1:])
            dh = None
            for i, group in enumerate(INPROJ_DH_GROUPS):
                dh = _mm_nt_sum(f"{tag}_inproj_dh{i}", [(dsegs[s], w_segs[s]) for s in group], dh,
                                h.dtype if i == len(INPROJ_DH_GROUPS) - 1 else F32)
            dx_norm, d_norm_g = vjp_norm(dh)
            return (projection, others), dx_skip + dx_norm, dict(d_rest_s, norm_g=d_norm_g)

        def landed(exchanging, after):
            projection, others = (_exchange_wait(f"exchange_{tag}_{k}_wait", e, after) for k, e in zip(("w_in", "others"), exchanging))
            return projection + others
        return y, backward, landed

    y0, backward0, landed0 = layer(0, {"w_in": w_in0}, small[0], x2, gathering0)
    gathered1 = dict(zip(names, _exchange_wait("gather_layer1_wait", gathering1, y0)))
    y1, backward1, landed1 = layer(1, gathered1, small[1], y0)
    loss_splat, dy = _loss_call(y1, target)
    loss = lax.psum(loss_splat[0, 0], ("x", "y", "c"))
    exchanging1, d_y0, g_small1 = backward1(dy)
    exchanging0, g_x, g_small0 = backward0(d_y0)

    state = lambda n, s: (view(w, n, s), view(m, n, s), view(v, n, s))
    parts1 = landed1(exchanging1, g_x)
    updated = {n: _adamw_layer(f"adamw_l1_{n}", parts1[i], *state(n, s), 1) for i, (n, s) in enumerate(SHARDED)}
    g_small = (g_small0, g_small1)
    sizes = [k for _ in range(DEPTH) for _, k in SMALL]
    g_vec = _pack_vectors([g_small[l][n] for l in range(DEPTH) for n, _ in SMALL])
    (parts_vec,) = _all_to_all([jnp.broadcast_to(g_vec[None], (N_DEV,) + g_vec.shape)], updated["w_in"][0])
    parts0 = landed0(exchanging0, parts_vec)

    vec = lambda t: _pack_vectors([t[n][l] for l in range(DEPTH) for n, _ in SMALL])[None]
    outs = {}
    for i, (n, s) in enumerate(SHARDED):
        res = _adamw_layer(f"adamw_l0_{n}", parts0[i], *state(n, s), 0, earlier=updated[n])
        outs[n] = [o.reshape(w[n].shape) for o in res]
    vec_outs = [_unpack_vectors(o, sizes) for o in _adamw_layer("adamw_vectors", parts_vec, vec(w), vec(m), vec(v), 0)]
    for i, (n, _) in enumerate(SMALL):
        outs[n] = [jnp.stack([o[l * len(SMALL) + i] for l in range(DEPTH)]) for o in vec_outs]
    return (loss, g_x[None], *[outs[n][k] for k in range(4) for n in WEIGHTS])
```

```python
import functools
import math

import jax
import jax.numpy as jnp
from jax import lax
from jax.experimental import pallas as pl
from jax.experimental.pallas import tpu as pltpu

F32 = jnp.float32
BF16 = jnp.bfloat16
HI = lax.Precision.HIGHEST
MESH = pl.DeviceIdType.MESH

N_DEV = 8
D_MODEL = 1024
DEPTH = 2
RMS_EPS = 1e-6
ROPE_THETA = 10000.0
LANES = 128
MLA_HEADS = 4
MLA_ROPE = 64
MLA_QK = 192
MLA_Q_RANK = 384
MLA_KV_RANK = 256
DN_HEADS = 4
DN_CHUNK = 64
DN_CONV = 4
DIL_HEADS = 12
DIL_GROUP_HEADS = 4
DIL_DILATIONS = (1, 4, 16)
DIL_BLOCK = 128
BRANCH_W = 512
IN_WIDTH = 11464
NEG = -1e30
VMEM_LIMIT = 56 * 1024 * 1024

ADAM_LR, ADAM_B1, ADAM_B2, ADAM_EPS, ADAM_WD, ADAM_STEP = 0.001, 0.9, 0.999, 1e-08, 0.01, 10

_SEG = {}
_off = 0
for _n, _w in (("q_lat", 384), ("c_kv", 256), ("k_pe", 64), ("z_a", 512), ("dn_qkv", 1536), ("dn_a", 4), ("dn_b", 4),
               ("z_b", 512), ("dil_qkv", 4608), ("z_c", 512), ("gate", 3072)):
    _SEG[_n] = (_off, _w)
    _off += _w
assert _off == IN_WIDTH


def _pcall(body, **kw):
    return pl.pallas_call(body, **kw)


def _params(sem=None):
    return pltpu.CompilerParams(dimension_semantics=sem, vmem_limit_bytes=VMEM_LIMIT)


def _tile(n, target, mult):
    t = (min(n, target) // mult) * mult
    while t >= mult:
        if n % t == 0:
            return t
        t -= mult
    return n


def _mm(name, a, b, mode, out_dtype=F32, acc=None, tm=1024, tn=512, tk=1024):
    if mode == "nn":
        (M, K), (_, N) = a.shape, b.shape
    elif mode == "nt":
        (M, K), (N, _) = a.shape, b.shape
    else:
        (K, M), (_, N) = a.shape, b.shape
    tm, tn, tk = _tile(M, tm, LANES), _tile(N, tn, LANES), _tile(K, tk, LANES)
    nk = K // tk
    dims = {"nn": (((1,), (0,)), ((), ())), "nt": (((1,), (1,)), ((), ())), "tn": (((0,), (0,)), ((), ()))}[mode]
    a_spec = pl.BlockSpec((tk, tm), lambda i, j, k: (k, i)) if mode == "tn" else pl.BlockSpec((tm, tk), lambda i, j, k: (i, k))
    b_spec = pl.BlockSpec((tn, tk), lambda i, j, k: (j, k)) if mode == "nt" else pl.BlockSpec((tk, tn), lambda i, j, k: (k, j))
    o_spec = pl.BlockSpec((tm, tn), lambda i, j, k: (i, j))
    has_acc = acc is not None

    def body(*refs):
        a_ref, b_ref = refs[:2]
        c_ref = refs[2] if has_acc else None
        o_ref = refs[3] if has_acc else refs[2]
        prod = lax.dot_general(a_ref[...].astype(BF16), b_ref[...].astype(BF16), dims, preferred_element_type=F32)
        if nk == 1:
            o_ref[...] = (prod + c_ref[...].astype(F32) if has_acc else prod).astype(out_dtype)
            return
        acc_ref = refs[-1]
        k = pl.program_id(2)

        @pl.when(k == 0)
        def _():
            acc_ref[...] = prod + c_ref[...].astype(F32) if has_acc else prod

        @pl.when(k > 0)
        def _():
            acc_ref[...] += prod

        @pl.when(k == nk - 1)
        def _():
            o_ref[...] = acc_ref[...].astype(out_dtype)

    ins = [a, b] + ([acc] if has_acc else [])
    in_specs = [a_spec, b_spec] + ([o_spec] if has_acc else [])
    return _pcall(body, name=name, grid=(M // tm, N // tn, nk), in_specs=in_specs, out_specs=o_spec,
                  out_shape=jax.ShapeDtypeStruct((M, N), out_dtype), scratch_shapes=[pltpu.VMEM((tm, tn), F32)] if nk > 1 else [],
                  compiler_params=_params(("parallel", "parallel", "arbitrary")))(*ins)


def _mm_nt_sum(name, pairs, acc, out_dtype, tm=512, tn=1024):
    M, N = pairs[0][0].shape[0], pairs[0][1].shape[0]
    tm, tn = _tile(M, tm, LANES), _tile(N, tn, LANES)
    n = len(pairs)
    has_acc = acc is not None
    o_spec = pl.BlockSpec((tm, tn), lambda i, j: (i, j))

    def body(*refs):
        total = refs[2 * n][...].astype(F32) if has_acc else None
        for a_ref, b_ref in zip(refs[:n], refs[n:2 * n]):
            prod = lax.dot_general(a_ref[...].astype(BF16), b_ref[...].astype(BF16), (((1,), (1,)), ((), ())), preferred_element_type=F32)
            total = prod if total is None else total + prod
        refs[-1][...] = total.astype(out_dtype)

    in_specs = ([pl.BlockSpec((tm, a.shape[1]), lambda i, j: (i, 0)) for a, _ in pairs]
                + [pl.BlockSpec((tn, b.shape[1]), lambda i, j: (j, 0)) for _, b in pairs] + ([o_spec] if has_acc else []))
    return _pcall(body, name=name, grid=(M // tm, N // tn), in_specs=in_specs, out_specs=o_spec,
                  out_shape=jax.ShapeDtypeStruct((M, N), out_dtype), compiler_params=_params(("parallel", "parallel")))(
        *[a for a, _ in pairs], *[b for _, b in pairs], *([acc] if has_acc else []))


def _mm_nn_many(name, a, bs, out_dtype, tm=1024):
    M, K = a.shape
    tm = _tile(M, tm, LANES)
    n = len(bs)

    def body(*refs):
        a_t = refs[0][...].astype(BF16)
        for b_ref, o_ref in zip(refs[1:1 + n], refs[1 + n:]):
            o_ref[...] = jnp.dot(a_t, b_ref[...].astype(BF16), preferred_element_type=F32).astype(out_dtype)

    return _pcall(body, name=name, grid=(M // tm,),
                  in_specs=[pl.BlockSpec((tm, K), lambda i: (i, 0))] + [pl.BlockSpec(b.shape, lambda i: (0, 0)) for b in bs],
                  out_specs=[pl.BlockSpec((tm, b.shape[1]), lambda i: (i, 0)) for b in bs],
                  out_shape=[jax.ShapeDtypeStruct((M, b.shape[1]), out_dtype) for b in bs],
                  compiler_params=_params(("parallel",)))(a, *bs)


def _mm_tn_many(name, a, bs, out_dtype, tk=1024):
    K, M = a.shape
    tk = _tile(K, tk, LANES)
    nk, n = K // tk, len(bs)

    def body(*refs):
        a_t = refs[0][...].astype(BF16)
        k = pl.program_id(0)
        for b_ref, o_ref, acc in zip(refs[1:1 + n], refs[1 + n:1 + 2 * n], refs[1 + 2 * n:]):
            prod = lax.dot_general(a_t, b_ref[...].astype(BF16), (((0,), (0,)), ((), ())), preferred_element_type=F32)

            @pl.when(k == 0)
            def _(acc=acc, prod=prod):
                acc[...] = prod

            @pl.when(k > 0)
            def _(acc=acc, prod=prod):
                acc[...] += prod

            @pl.when(k == nk - 1)
            def _(acc=acc, o_ref=o_ref):
                o_ref[...] = acc[...].astype(out_dtype)

    return _pcall(body, name=name, grid=(nk,),
                  in_specs=[pl.BlockSpec((tk, M), lambda k: (k, 0))] + [pl.BlockSpec((tk, b.shape[1]), lambda k: (k, 0)) for b in bs],
                  out_specs=[pl.BlockSpec((M, b.shape[1]), lambda k: (0, 0)) for b in bs],
                  out_shape=[jax.ShapeDtypeStruct((M, b.shape[1]), out_dtype) for b in bs],
                  scratch_shapes=[pltpu.VMEM((M, b.shape[1]), F32) for b in bs],
                  compiler_params=_params(("arbitrary",)))(a, *bs)


INPROJ_DH_GROUPS = ((0, 1, 2, 3, 5, 6, 8), (4,), (9,), (7,))
INPROJ_NARROW = INPROJ_DH_GROUPS[0]

INPROJ_DTYPE = BF16

INPROJ_TILES = {"nn": dict(tm=1024, tn=1536, tk=1024), "tn": dict(tm=1024, tn=768, tk=2048)}


def _make_multi_linear(name, n, out_dtype=F32):
    @jax.custom_vjp
    def op(h, ws):
        return tuple(_mm(f"{name}_fwd{i}", h, w, "nn", out_dtype=out_dtype) for i, w in enumerate(ws))

    def fwd(h, ws):
        return op(h, ws), (h, ws)

    def bwd(res, douts):
        h, ws = res
        dh = None
        for i, (w, d) in enumerate(zip(ws, douts)):
            dh = _mm(f"{name}_dh{i}", d, w, "nt", acc=dh, out_dtype=h.dtype if i == len(ws) - 1 else F32)
        dws = tuple(_mm(f"{name}_dw{i}", h, d, "tn", out_dtype=w.dtype) for i, (w, d) in enumerate(zip(ws, douts)))
        return dh, dws

    op.defvjp(fwd, bwd)
    return op


def _make_resid_linear(name):
    @jax.custom_vjp
    def op(x, a, w):
        return _mm(f"{name}_fwd", a, w, "nn", acc=x)

    def fwd(x, a, w):
        return op(x, a, w), (a, w)

    def bwd(res, dy):
        a, w = res
        return dy, _mm(f"{name}_da", dy, w, "nt", out_dtype=a.dtype), _mm(f"{name}_dw", a, dy, "tn", out_dtype=w.dtype)

    op.defvjp(fwd, bwd)
    return op


def _make_rowwise(name, f, tile):
    def specs(rows, aux, params, consts, t):
        row = [pl.BlockSpec((t, a.shape[1]), lambda i: (i, 0)) for a in (*rows, *aux)]
        full = [pl.BlockSpec(p.shape, lambda i: (0, 0)) for p in (*params, *consts)]
        return row, full

    def fwd_call(rows, aux, params, consts):
        S = rows[0].shape[0]
        t = min(tile, S)
        n_in = len(rows) + len(aux) + len(params) + len(consts)
        shp = lambda a: jax.ShapeDtypeStruct((t, a.shape[1]), a.dtype)
        outs = jax.eval_shape(f, *[shp(a) for a in (*rows, *aux)], *params, *consts)
        row_specs, full_specs = specs(rows, aux, params, consts, t)

        def body(*refs):
            res = f(*[r[...] for r in refs[:n_in]])
            for o_ref, o in zip(refs[n_in:], res):
                o_ref[...] = o

        return _pcall(body, name=f"{name}_fwd", grid=(S // t,), in_specs=row_specs + full_specs,
                      out_specs=[pl.BlockSpec((t, o.shape[1]), lambda i: (i, 0)) for o in outs],
                      out_shape=[jax.ShapeDtypeStruct((S, o.shape[1]), o.dtype) for o in outs],
                      compiler_params=_params(("parallel",)))(*rows, *aux, *params, *consts)

    def bwd_call(rows, aux, params, consts, douts):
        S = rows[0].shape[0]
        t = min(tile, S)
        nr, na, npar, nc, nd = len(rows), len(aux), len(params), len(consts), len(douts)
        row_specs, full_specs = specs(rows, aux, params, consts, t)

        def body(*refs):
            vals = [r[...] for r in refs[:nr + na + npar + nc]]
            rv, av = vals[:nr], vals[nr:nr + na]
            pv, cv = vals[nr + na:nr + na + npar], vals[nr + na + npar:]
            dv = tuple(r[...] for r in refs[nr + na + npar + nc:nr + na + npar + nc + nd])
            out_refs = refs[nr + na + npar + nc + nd:]
            _, vjp = jax.vjp(lambda *rp: f(*rp[:nr], *av, *rp[nr:], *cv), *rv, *pv)
            grads = vjp(dv)
            for o_ref, g in zip(out_refs[:nr], grads[:nr]):
                o_ref[...] = g
            first = pl.program_id(0) == 0
            for o_ref, g in zip(out_refs[nr:], grads[nr:]):
                @pl.when(first)
                def _(o_ref=o_ref):
                    o_ref[...] = jnp.zeros_like(o_ref)
                o_ref[...] += g

        res = _pcall(body, name=f"{name}_bwd", grid=(S // t,),
                     in_specs=row_specs + full_specs + [pl.BlockSpec((t, d.shape[1]), lambda i: (i, 0)) for d in douts],
                     out_specs=[pl.BlockSpec((t, a.shape[1]), lambda i: (i, 0)) for a in rows]
                     + [pl.BlockSpec(p.shape, lambda i: (0, 0)) for p in params],
                     out_shape=[jax.ShapeDtypeStruct(a.shape, a.dtype) for a in (*rows, *params)],
                     compiler_params=_params(("arbitrary",)))(*rows, *aux, *params, *consts, *douts)
        return tuple(res[:nr]), tuple(res[nr:])

    @jax.custom_vjp
    def op(rows, aux, params, consts):
        return tuple(fwd_call(rows, aux, params, consts))

    def fwd(rows, aux, params, consts):
        return op(rows, aux, params, consts), (rows, aux, params, consts)

    def bwd(res, douts):
        rows, aux, params, consts = res
        drows, dparams = bwd_call(rows, aux, params, consts, tuple(douts))
        zeros = lambda xs: tuple(jnp.zeros_like(a) for a in xs)
        return drows, zeros(aux), dparams, zeros(consts)

    op.defvjp(fwd, bwd)
    return op


@jax.custom_vjp
def _swap_halves(x):
    return pltpu.roll(x, LANES // 2, 1)


_swap_halves.defvjp(lambda x: (_swap_halves(x), None), lambda _, g: (_swap_halves(g),))


def _rope(x, cos_t, sin_t):
    return x * cos_t + _swap_halves(x) * sin_t


def _rms(x, g, n=None):
    n = x.shape[-1] if n is None else n
    return x * lax.rsqrt(jnp.sum(x * x, axis=-1, keepdims=True) * (1.0 / n) + RMS_EPS) * g


def _heads(x):
    return [x[:, i * LANES:(i + 1) * LANES] for i in range(x.shape[1] // LANES)]


def _cat(xs):
    return jnp.concatenate(xs, axis=1)


def _silu(x):
    return x * jax.nn.sigmoid(x)


def _f_norm(x, g):
    return (_rms(x, g).astype(BF16),)


def _f_mla_a(q_lat, c_kv, kpe, cos_p, sin_p, qa_g, kva_g, kpe_g):
    q_lat, c_kv, kpe = (t.astype(F32) for t in (q_lat, c_kv, kpe))
    kp =_rope(_rms(kpe, kpe_g, MLA_ROPE), cos_p, sin_p)
    return _rms(q_lat, qa_g).astype(BF16), _rms(c_kv, kva_g).astype(BF16), _cat([kp] * MLA_HEADS)


def _f_mla_b(q8, kn_raw, cos_p, sin_p, qn_g, qp_g, kn_g):
    hs = _heads(q8)
    qn = _cat([_rms(h, qn_g) for h in hs[:MLA_HEADS]])
    qp = _cat([_rope(_rms(h, qp_g, MLA_ROPE), cos_p, sin_p) for h in hs[MLA_HEADS:]])
    kn = _cat([_rms(h, kn_g) for h in _heads(kn_raw)])
    return qn, qp, kn


def _softplus(x):
    return jnp.maximum(x, 0.0) + jnp.log(1.0 + jnp.exp(-jnp.abs(x)))


def _l2n(x):
    return x * lax.rsqrt(jnp.sum(x * x, axis=-1, keepdims=True) + 1e-6)


def _f_dn_pre(mixed, ab, alog_f, dtb_f, e_a, e_b):
    hs = _heads(mixed)
    q = _cat([_l2n(h) * (LANES ** -0.5) for h in hs[:DN_HEADS]])
    k = _cat([_l2n(h) for h in hs[DN_HEADS:2 * DN_HEADS]])
    v = _cat(hs[2 * DN_HEADS:])
    ab = ab.astype(F32)
    a_f = jnp.dot(ab, e_a, precision=HI, preferred_element_type=F32)
    b_f = jnp.dot(ab, e_b, precision=HI, preferred_element_type=F32)
    g = -jnp.exp(alog_f) * _softplus(a_f + dtb_f)
    return q, k, v, g, jax.nn.sigmoid(b_f)


def _f_dil_pre(qkv, cos_h, sin_h, q_g, k_g):
    hs = [h.astype(F32) for h in _heads(qkv)]
    q = [_rope(_rms(h, q_g), cos_h, sin_h) for h in hs[:DIL_HEADS]]
    k = [_rope(_rms(h, k_g), cos_h, sin_h) for h in hs[DIL_HEADS:2 * DIL_HEADS]]
    v = hs[2 * DIL_HEADS:]
    group = lambda xs, g: _cat(xs[g * DIL_GROUP_HEADS:(g + 1) * DIL_GROUP_HEADS])
    return tuple(group(xs, g) for xs in (q, k, v) for g in range(len(DIL_DILATIONS)))


def _f_merge_a(y_a, z_a, o_dn, z_b, o0, o1, o2, l0, l1, l2, z_c, out_g):
    z_a, z_b, z_c = (z.astype(F32) for z in (z_a, z_b, z_c))
    y_b = _cat([_rms(h, out_g) for h in _heads(o_dn)])
    os_, ls = [_heads(o) for o in (o0, o1, o2)], [_heads(l) for l in (l0, l1, l2)]
    y_c = []
    for j in range(DIL_GROUP_HEADS):
        l3 = [ls[g][j] for g in range(3)]
        m = jnp.maximum(jnp.maximum(l3[0], l3[1]), l3[2])
        e3 = [jnp.exp(l - m) for l in l3]
        den = e3[0] + e3[1] + e3[2]
        y_c.append(sum(e3[g] * os_[g][j] for g in range(3)) / den)
    return tuple(t.astype(BF16) for t in (y_a * _silu(z_a), y_b * _silu(z_b), _cat(y_c) * _silu(z_c)))


def _f_merge_b(b0, b1, b2, gl):
    gs = [jax.nn.sigmoid(gl[:, i * D_MODEL:(i + 1) * D_MODEL].astype(F32)) for i in range(3)]
    return ((gs[0] * b0 + gs[1] * b1 + gs[2] * b2).astype(BF16),)


def _rope_tables(pos, inv_sign):
    S = pos.shape[0]
    t = min(S, 1024)

    def body(p_ref, c_ref, cp, sp, ch, sh):
        p = p_ref[...].astype(F32)
        c = c_ref[...]
        ang_p, ang_h = p * c[0:1], p * c[2:3]
        cp[...] = jnp.cos(ang_p) * jnp.abs(c[1:2])
        sp[...] = jnp.sin(ang_p) * c[1:2]
        ch[...] = jnp.cos(ang_h)
        sh[...] = jnp.sin(ang_h) * c[3:4]

    row = pl.BlockSpec((t, LANES), lambda i: (i, 0))
    return _pcall(body, name="rope_tables", grid=(S // t,),
                  in_specs=[pl.BlockSpec((t, 1), lambda i: (i, 0)), pl.BlockSpec((4, LANES), lambda i: (0, 0))],
                  out_specs=[row] * 4, out_shape=[jax.ShapeDtypeStruct((S, LANES), F32)] * 4,
                  compiler_params=_params(("parallel",)))(pos, inv_sign)


def _rope_consts():
    half_p, half_h = MLA_ROPE // 2, LANES // 2
    inv_p = 1.0 / (ROPE_THETA ** (jnp.arange(0, MLA_ROPE, 2, dtype=F32) / MLA_ROPE))
    inv_h = 1.0 / (ROPE_THETA ** (jnp.arange(0, LANES, 2, dtype=F32) / LANES))
    z = jnp.zeros((half_p,), F32)
    o = jnp.ones((half_p,), F32)
    return jnp.stack([jnp.concatenate([inv_p, z, inv_p, z]), jnp.concatenate([-o, z, o, z]),
                      jnp.concatenate([inv_h, inv_h]), jnp.concatenate([-jnp.ones((half_h,), F32), jnp.ones((half_h,), F32)])])


def _shift_rows(x, s, up):
    n = x.shape[0]
    r = lax.broadcasted_iota(jnp.int32, x.shape, 0)
    if up:
        return jnp.where(r < n - s, pltpu.roll(x, n - s, 0), 0.0)
    return jnp.where(r >= s, pltpu.roll(x, s, 0), 0.0)


def _make_shift(s):
    @jax.custom_vjp
    def sh(x):
        return _shift_rows(x, s, False)

    sh.defvjp(lambda x: (sh(x), None), lambda _, g: (_shift_rows(g, s, True),))
    return sh


def _f_conv(x, w):
    x = x.astype(F32)
    y = x * w[DN_CONV - 1:DN_CONV]
    for j in range(DN_CONV - 1):
        y = y + _make_shift(DN_CONV - 1 - j)(x) * w[j:j + 1]
    return _silu(y)


def _make_conv(name):
    def call(x, w, dy=None):
        S, C = x.shape
        col = pl.BlockSpec((S, LANES), lambda i: (0, i))
        wsp = pl.BlockSpec((DN_CONV, LANES), lambda i: (0, i))
        if dy is None:
            def body(x_ref, w_ref, o_ref):
                o_ref[...] = _f_conv(x_ref[...], w_ref[...])
            return _pcall(body, name=f"{name}_fwd", grid=(C // LANES,), in_specs=[col, wsp], out_specs=col,
                          out_shape=jax.ShapeDtypeStruct(x.shape, F32), compiler_params=_params(("parallel",)))(x, w)

        def body(x_ref, w_ref, dy_ref, dx_ref, dw_ref):
            _, vjp = jax.vjp(_f_conv, x_ref[...], w_ref[...])
            dx_ref[...], dw_ref[...] = vjp(dy_ref[...])
        return _pcall(body, name=f"{name}_bwd", grid=(C // LANES,), in_specs=[col, wsp, col], out_specs=[col, wsp],
                      out_shape=[jax.ShapeDtypeStruct(x.shape, x.dtype), jax.ShapeDtypeStruct(w.shape, F32)],
                      compiler_params=_params(("parallel",)))(x, w, dy)

    @jax.custom_vjp
    def op(x, w):
        return call(x, w)

    op.defvjp(lambda x, w: (op(x, w), (x, w)), lambda res, dy: tuple(call(*res, dy)))
    return op


def _dot_nt(a, b):
    return lax.dot_general(a.astype(BF16), b.astype(BF16), (((1,), (1,)), ((), ())), preferred_element_type=F32)


def _dot_nn(a, b):
    return jnp.dot(a.astype(BF16), b.astype(BF16), preferred_element_type=F32)


def _dot_tn(a, b):
    return lax.dot_general(a.astype(BF16), b.astype(BF16), (((0,), (0,)), ((), ())), preferred_element_type=F32)


def _mla_scores(qn, qp, kn, kp, diagonal):
    scale = MLA_QK ** -0.5
    s = _dot_nt(qn * scale, kn) + _dot_nt(qp * scale, kp)
    if diagonal:
        r = lax.broadcasted_iota(jnp.int32, s.shape, 0)
        c = lax.broadcasted_iota(jnp.int32, s.shape, 1)
        s = jnp.where(c <= r, s, NEG)
    return s


def _on_causal_pairs(qi, ki, step):
    @pl.when(ki < qi)
    def _():
        step(False)

    @pl.when(ki == qi)
    def _():
        step(True)


def _causal_pairs(n, t, by_key, heads):
    pairs = [(q, k) for k in range(n) for q in range(k, n)] if by_key else [(q, k) for q in range(n) for k in range(q + 1)]
    qt, kt = (jnp.array([p[i] for p in pairs], jnp.int32) for i in (0, 1))
    return (qt, kt, pl.BlockSpec((t, heads * LANES), lambda h, p, qt_r, kt_r: (qt_r[p], h)),
            pl.BlockSpec((t, heads * LANES), lambda h, p, qt_r, kt_r: (kt_r[p], h)))


MLA_FWD_HEADS = 4
MLA_BWD_HEADS = 2


def _make_mla_attn(name):
    scale = MLA_QK ** -0.5

    def fwd_call(qn, qp, kn, kp, v):
        S = qn.shape[0]
        t = min(S, 512)
        n = S // t
        hp = MLA_FWD_HEADS
        qt, kt, qs, ks = _causal_pairs(n, t, False, hp)
        cols = [slice(j * LANES, (j + 1) * LANES) for j in range(hp)]

        def body(qt_r, kt_r, qn_r, qp_r, kn_r, kp_r, v_r, o_r, lse_r, m_s, l_s, acc_s):
            qi, ki = qt_r[pl.program_id(1)], kt_r[pl.program_id(1)]

            @pl.when(ki == 0)
            def _():
                m_s[...] = jnp.full_like(m_s, NEG)
                l_s[...] = jnp.zeros_like(l_s)
                acc_s[...] = jnp.zeros_like(acc_s)

            def step(diagonal):
                for c in cols:
                    s = _mla_scores(qn_r[:, c], qp_r[:, c], kn_r[:, c], kp_r[:, c], diagonal)
                    m_old = m_s[:, c]
                    m_new = jnp.maximum(m_old, jnp.max(s, axis=-1, keepdims=True))
                    p = jnp.exp(s - m_new[:, :1])
                    alpha = jnp.exp(m_old - m_new)
                    l_s[:, c] = alpha * l_s[:, c] + jnp.sum(p, axis=-1, keepdims=True)
                    acc_s[:, c] = alpha * acc_s[:, c] + _dot_nn(p, v_r[:, c])
                    m_s[:, c] = m_new
            _on_causal_pairs(qi, ki, step)

            @pl.when(ki == qi)
            def _():
                o_r[...] = acc_s[...] / l_s[...]
                lse_r[...] = m_s[...] + jnp.log(l_s[...])

        spec = pltpu.PrefetchScalarGridSpec(num_scalar_prefetch=2, grid=(MLA_HEADS // hp, qt.shape[0]), in_specs=[qs, qs, ks, ks, ks],
                                            out_specs=[qs, qs], scratch_shapes=[pltpu.VMEM((t, hp * LANES), F32)] * 3)
        return _pcall(body, name=f"{name}_fwd", grid_spec=spec, out_shape=[jax.ShapeDtypeStruct((S, MLA_HEADS * LANES), F32)] * 2,
                      compiler_params=_params(("parallel", "arbitrary")))(qt, kt, qn, qp, kn, kp, v)

    def bwd_call(qn, qp, kn, kp, v, o, lse, do):
        S = qn.shape[0]
        t = min(S, 512)
        n = S // t
        hp = MLA_BWD_HEADS
        qt, kt, qs, ks = _causal_pairs(n, t, True, hp)
        head = pl.BlockSpec((S, hp * LANES), lambda h, p, qt_r, kt_r: (0, h))
        n_pairs = qt.shape[0]
        cols = [slice(j * LANES, (j + 1) * LANES) for j in range(hp)]

        def body(qt_r, kt_r, qn_r, qp_r, kn_r, kp_r, v_r, o_r, lse_r, do_r, dqn_r, dqp_r, dkn_r, dkp_r, dv_r, dkn_s, dkp_s, dv_s, dl_s):
            pair = pl.program_id(1)
            qi, ki = qt_r[pair], kt_r[pair]
            rows = pl.ds(pl.multiple_of(qi * t, t), t)

            @pl.when(pair == 0)
            def _():
                dqn_r[...] = jnp.zeros_like(dqn_r)
                dqp_r[...] = jnp.zeros_like(dqp_r)

            @pl.when(ki == 0)
            def _():
                for c in cols:
                    dl_s[rows, c] = jnp.broadcast_to(jnp.sum(do_r[:, c] * o_r[:, c], axis=-1, keepdims=True), (t, LANES))

            @pl.when(qi == ki)
            def _():
                dkn_s[...] = jnp.zeros_like(dkn_s)
                dkp_s[...] = jnp.zeros_like(dkp_s)
                dv_s[...] = jnp.zeros_like(dv_s)

            def step(diagonal):
                for c in cols:
                    qn, qp, kn, kp, do = qn_r[:, c], qp_r[:, c], kn_r[:, c], kp_r[:, c], do_r[:, c]
                    p = jnp.exp(_mla_scores(qn, qp, kn, kp, diagonal) - lse_r[:, c][:, :1])
                    ds = p * (_dot_nt(do, v_r[:, c]) - dl_s[rows, c][:, :1])
                    dv_s[:, c] += _dot_tn(p, do)
                    dkn_s[:, c] += _dot_tn(ds, qn * scale)
                    dkp_s[:, c] += _dot_tn(ds, qp * scale)
                    dqn_r[rows, c] += _dot_nn(ds, kn)
                    dqp_r[rows, c] += _dot_nn(ds, kp)
            _on_causal_pairs(qi, ki, step)

            @pl.when(qi == n - 1)
            def _():
                dkn_r[...] = dkn_s[...]
                dkp_r[...] = dkp_s[...]
                dv_r[...] = dv_s[...]

            @pl.when(pair == n_pairs - 1)
            def _():
                dqn_r[...] = dqn_r[...] * scale
                dqp_r[...] = dqp_r[...] * scale

        spec = pltpu.PrefetchScalarGridSpec(num_scalar_prefetch=2, grid=(MLA_HEADS // hp, n_pairs), in_specs=[qs, qs, ks, ks, ks, qs, qs, qs],
                                            out_specs=[head, head, ks, ks, ks],
                                            scratch_shapes=[pltpu.VMEM((t, hp * LANES), F32)] * 3 + [pltpu.VMEM((S, hp * LANES), F32)])
        return _pcall(body, name=f"{name}_bwd", grid_spec=spec, out_shape=[jax.ShapeDtypeStruct((S, MLA_HEADS * LANES), F32)] * 5,
                      compiler_params=_params(("parallel", "arbitrary")))(qt, kt, qn, qp, kn, kp, v, o, lse, do)

    @jax.custom_vjp
    def op(qn, qp, kn, kp, v):
        return fwd_call(qn, qp, kn, kp, v)[0]

    def fwd(qn, qp, kn, kp, v):
        o, lse = fwd_call(qn, qp, kn, kp, v)
        return o, (qn, qp, kn, kp, v, o, lse)

    def bwd(res, do):
        return tuple(bwd_call(*res, do))

    op.defvjp(fwd, bwd)
    return op


def _dil_block(q, kp, kc, vp, vc, has_prev):
    scale = LANES ** -0.5
    r = lax.broadcasted_iota(jnp.int32, (DIL_BLOCK, 2 * DIL_BLOCK), 0)
    c = lax.broadcasted_iota(jnp.int32, (DIL_BLOCK, 2 * DIL_BLOCK), 1)
    valid = ((c < DIL_BLOCK) & (c >= r) & has_prev) | ((c >= DIL_BLOCK) & (c - DIL_BLOCK <= r))
    s = jnp.where(valid, _dot_nt(q * scale, jnp.concatenate([kp, kc], axis=0)), NEG)
    m = jnp.max(s, axis=-1, keepdims=True)
    e = jnp.exp(s - m)
    den = jnp.sum(e, axis=-1, keepdims=True)
    o = _dot_nn(e, jnp.concatenate([vp, vc], axis=0)) / den
    return o, jnp.broadcast_to(m + jnp.log(den), o.shape)


DIL_TILE_ROWS = (1024, 1024, 2048)
DIL_STEP_HEADS = 2


def _make_dil_attn(name, d, tile_rows):
    def call(q, k, v, cts=None):
        S = q.shape[0]
        span = DIL_BLOCK * d
        G = max(1, min(tile_rows, S) // span)
        n = S // (G * span)
        at = (lambda i: i) if cts is None else (lambda i: n - 1 - i)
        step_heads = DIL_STEP_HEADS if d == 1 else 1
        width = step_heads * LANES
        tile = pl.BlockSpec((G * span, width), lambda h, i: (at(i), h))
        before = pl.BlockSpec((span, width), lambda h, i: (jnp.maximum(at(i) * G - 1, 0), h))
        heads = [slice(h * LANES, (h + 1) * LANES) for h in range(step_heads)]

        def rows(r, j):
            return pl.ds(j * DIL_BLOCK, DIL_BLOCK) if d == 1 else pl.ds(r + j * span, DIL_BLOCK, stride=d)

        def over_residues(fn):
            if d == 1:
                fn(0)
            else:
                lax.fori_loop(0, d, lambda r, c: (fn(r), c)[1], 0)

        def block_inputs(r, j, c, q_r, kb_r, k_r, vb_r, v_r):
            kp = kb_r[rows(r, 0), c] if j == 0 else k_r[rows(r, j - 1), c]
            vp = vb_r[rows(r, 0), c] if j == 0 else v_r[rows(r, j - 1), c]
            return q_r[rows(r, j), c], kp, k_r[rows(r, j), c], vp, v_r[rows(r, j), c]

        if cts is None:
            def body(q_r, kb_r, k_r, vb_r, v_r, o_r, lse_r):
                first = at(pl.program_id(1)) * G

                def residue(r):
                    for j in range(G):
                        for c in heads:
                            o_r[rows(r, j), c], lse_r[rows(r, j), c] = _dil_block(
                                *block_inputs(r, j, c, q_r, kb_r, k_r, vb_r, v_r), first + j > 0)
                over_residues(residue)
            return _pcall(body, name=f"{name}_fwd", grid=(DIL_GROUP_HEADS // step_heads, n), in_specs=[tile, before, tile, before, tile],
                          out_specs=[tile, tile], out_shape=[jax.ShapeDtypeStruct(q.shape, F32)] * 2,
                          compiler_params=_params(("parallel", "parallel")))(q, k, k, v, v)

        def body(q_r, kb_r, k_r, vb_r, v_r, do_r, dl_r, dq_r, dk_r, dv_r, ck_s, cv_s):
            first = at(pl.program_id(1)) * G

            @pl.when(pl.program_id(1) == 0)
            def _():
                ck_s[...] = jnp.zeros_like(ck_s)
                cv_s[...] = jnp.zeros_like(cv_s)

            def residue(r):
                for c in heads:
                    owed = None
                    for j in range(G):
                        hp = first + j > 0
                        _, vjp = jax.vjp(lambda *a: _dil_block(*a, hp), *block_inputs(r, j, c, q_r, kb_r, k_r, vb_r, v_r))
                        dq, dkp, dkc, dvp, dvc = vjp((do_r[rows(r, j), c], dl_r[rows(r, j), c]))
                        dq_r[rows(r, j), c] = dq
                        if j == G - 1:
                            dkc, dvc = dkc + ck_s[rows(r, 0), c], dvc + cv_s[rows(r, 0), c]
                        dk_r[rows(r, j), c], dv_r[rows(r, j), c] = dkc, dvc
                        if j == 0:
                            owed = (dkp, dvp)
                        else:
                            dk_r[rows(r, j - 1), c] += dkp
                            dv_r[rows(r, j - 1), c] += dvp
                    ck_s[rows(r, 0), c], cv_s[rows(r, 0), c] = owed
            over_residues(residue)
        return _pcall(body, name=f"{name}_bwd", grid=(DIL_GROUP_HEADS // step_heads, n), in_specs=[tile, before, tile, before, tile, tile, tile],
                      out_specs=[tile] * 3, out_shape=[jax.ShapeDtypeStruct(q.shape, F32)] * 3,
                      scratch_shapes=[pltpu.VMEM((span, width), F32)] * 2,
                      compiler_params=_params(("parallel", "arbitrary")))(q, k, k, v, v, *cts)

    @jax.custom_vjp
    def op(q, k, v):
        return tuple(call(q, k, v))

    op.defvjp(lambda q, k, v: (op(q, k, v), (q, k, v)), lambda res, cts: tuple(call(*res, cts=cts)))
    return op


def _pdot(a, b, dims):
    return lax.dot_general(a, b, (dims, ((), ())), precision=lax.Precision.HIGH, preferred_element_type=F32)


DN_LOCAL_CHUNKS = 4


DN_BLOCK_HEADS = 4
DN_BLOCK = DN_BLOCK_HEADS * DN_CHUNK


def _inverse_cotangent(inv, d):
    return -_pdot(inv, _pdot(d, inv, ((1,), (1,))), ((0,), (0,)))


@jax.custom_vjp
def _unit_lower_inverse(a):
    n = a.shape[0]
    eye = (lax.broadcasted_iota(jnp.int32, (n, n), 0) == lax.broadcasted_iota(jnp.int32, (n, n), 1)).astype(F32)
    inv, pw = eye - a, a
    for level in range(5):
        dot = _pdot if level < 2 else (lambda x, y, dims: lax.dot_general(x, y, (dims, ((), ())), preferred_element_type=F32))
        pw = dot(pw, pw, ((1,), (0,)))
        inv = inv + dot(inv, pw, ((1,), (0,)))
    return inv


def _unit_lower_inverse_fwd(a):
    inv = _unit_lower_inverse(a)
    return inv, inv


_unit_lower_inverse.defvjp(_unit_lower_inverse_fwd, lambda inv, d: (_inverse_cotangent(inv, d),))


@jax.custom_vjp
def _known_inverse(a, inv):
    return inv


_known_inverse.defvjp(lambda a, inv: (inv, inv), lambda inv, d: (_inverse_cotangent(inv, d), jnp.zeros_like(inv)))


def _dn_local(q, k, v, g, b, known=None):
    C, R = DN_CHUNK, DN_BLOCK
    r = lax.broadcasted_iota(jnp.int32, (R, R), 0)
    c = lax.broadcasted_iota(jnp.int32, (R, R), 1)
    same_head = (r // C) == (c // C)
    incl, strict = same_head & (r >= c), same_head & (r > c)
    avg = jnp.full((R, LANES), 1.0 / LANES, F32)
    rc = lax.broadcasted_iota(jnp.int32, (C, C), 0) >= lax.broadcasted_iota(jnp.int32, (C, C), 1)
    gc_lanes = _pdot(rc.astype(F32), g, ((1,), (0,)))
    us, ws, qes, kds, qks, invs = [], [], [], [], [], []
    for first in range(0, DN_HEADS, DN_BLOCK_HEADS):
        stack = lambda x: jnp.concatenate(_heads(x)[first:first + DN_BLOCK_HEADS], axis=0)
        unstack = lambda x: [x[p * C:(p + 1) * C] for p in range(DN_BLOCK_HEADS)]
        gc, q_s, k_s, v_s, b_s = (stack(x) for x in (gc_lanes, q, k, v, b))
        gc_j = _pdot(avg, gc, ((1,), (1,)))
        decay = jnp.exp(jnp.where(incl, _cat([gc] * (R // LANES)) - gc_j, NEG))
        kb = k_s * b_s
        kk = _pdot(jnp.concatenate([kb, q_s], axis=0), k_s, ((1,), (1,)))
        a = jnp.where(strict, kk[:R] * decay, 0.0)
        inv = _unit_lower_inverse(a) if known is None else _known_inverse(a, known[len(invs) * R:(len(invs) + 1) * R])
        invs.append(inv)
        eg = jnp.exp(gc)
        uw = _pdot(inv, _cat([v_s * b_s, kb * eg]), ((1,), (0,)))
        g_last = jnp.concatenate([jnp.broadcast_to(x[C - 1:C], (C, LANES)) for x in unstack(gc)], axis=0)
        us += unstack(uw[:, :LANES])
        ws += unstack(uw[:, LANES:])
        qes += unstack(q_s * eg)
        kds += unstack(k_s * jnp.exp(g_last - gc))
        qks.append(kk[R:] * decay)
    egl = jnp.broadcast_to(jnp.exp(gc_lanes[C - 1:C]), (8, DN_HEADS * LANES))
    return _cat(us), _cat(ws), _cat(qes), _cat(kds), jnp.concatenate(qks, axis=0), egl, jnp.concatenate(invs, axis=0)


def _dn_scan(u, w, qe, kd, qk, egl, state):
    C = DN_CHUNK
    heads = [slice(h * LANES, (h + 1) * LANES) for h in range(DN_HEADS)]
    ws = [_pdot(jnp.concatenate([w[:, sl], qe[:, sl]], axis=0), state[sl, :], ((1,), (0,))) for sl in heads]
    v_new = [u[:, sl] - x[:C] for sl, x in zip(heads, ws)]
    local = []
    for i, first in enumerate(range(0, DN_HEADS, DN_BLOCK_HEADS)):
        y = _pdot(qk[i * DN_BLOCK:(i + 1) * DN_BLOCK], jnp.concatenate(v_new[first:first + DN_BLOCK_HEADS], axis=0), ((1,), (0,)))
        local += [y[p * C:(p + 1) * C] for p in range(DN_BLOCK_HEADS)]
    o = _cat([x[C:] + y for x, y in zip(ws, local)])
    states = [state[sl, :] * egl[0:1, sl] + _pdot(kd[:, sl], vn, ((0,), (0,))) for sl, vn in zip(heads, v_new)]
    return o, jnp.concatenate(states, axis=0)


def _make_delta_rule(name):
    W = DN_HEADS * LANES
    QK = DN_HEADS * DN_CHUNK

    def local_call(ins, cts=None):
        S = ins[0].shape[0]
        n = S // DN_CHUNK
        per = math.gcd(DN_LOCAL_CHUNKS, n)
        row = pl.BlockSpec((per * DN_CHUNK, W), lambda i: (i, 0))
        qkb = pl.BlockSpec((per * QK, DN_BLOCK), lambda i: (i, 0))
        eg = pl.BlockSpec((per, 8, W), lambda i: (i, 0, 0))
        rows = lambda j: slice(j * DN_CHUNK, (j + 1) * DN_CHUNK)
        qk_rows = lambda j: slice(j * QK, (j + 1) * QK)
        out_rows = [rows, rows, rows, rows, qk_rows]

        if cts is None:
            def body(*refs):
                for j in range(per):
                    res = _dn_local(*[r[rows(j), :] for r in refs[:5]])
                    for o_r, o, at_ in zip(refs[5:10], res[:5], out_rows):
                        o_r[at_(j), :] = o
                    refs[10][j] = res[5]
                    refs[11][qk_rows(j), :] = res[6]
            blockdiag = jax.ShapeDtypeStruct((n * QK, DN_BLOCK), F32)
            return _pcall(body, name=f"{name}_local_fwd", grid=(n // per,), in_specs=[row] * 5, out_specs=[row] * 4 + [qkb, eg, qkb],
                          out_shape=[jax.ShapeDtypeStruct((S, W), F32)] * 4 + [blockdiag, jax.ShapeDtypeStruct((n, 8, W), F32), blockdiag],
                          compiler_params=_params(("parallel",)))(*ins)

        def body(*refs):
            for j in range(per):
                known = refs[5][qk_rows(j), :]
                _, vjp = jax.vjp(lambda *a: _dn_local(*a, known=known)[:6], *[r[rows(j), :] for r in refs[:5]])
                grads = vjp(tuple(r[at_(j), :] for r, at_ in zip(refs[6:11], out_rows)) + (refs[11][j],))
                for o_r, o in zip(refs[12:], grads):
                    o_r[rows(j), :] = o
        return _pcall(body, name=f"{name}_local_bwd", grid=(n // per,), in_specs=[row] * 5 + [qkb] + [row] * 4 + [qkb, eg],
                      out_specs=[row] * 5, out_shape=[jax.ShapeDtypeStruct((S, W), F32)] * 5,
                      compiler_params=_params(("parallel",)))(*ins, *cts)

    def scan_call(ins, saved=None, do=None):
        S = ins[0].shape[0]
        n = S // DN_CHUNK
        at = (lambda i: i) if do is None else (lambda i: n - 1 - i)
        row = pl.BlockSpec((DN_CHUNK, W), lambda i: (at(i), 0))
        qkb = pl.BlockSpec((QK, DN_BLOCK), lambda i: (at(i), 0))
        eg = pl.BlockSpec((None, 8, W), lambda i: (at(i), 0, 0))
        st = pl.BlockSpec((None, W, LANES), lambda i: (at(i), 0, 0))

        if do is None:
            def body(*refs):
                o_r, st_r, s_s = refs[6:]

                @pl.when(pl.program_id(0) == 0)
                def _():
                    s_s[...] = jnp.zeros_like(s_s)
                st_r[...] = s_s[...]
                o_r[...], s_s[...] = _dn_scan(*[r[...] for r in refs[:6]], s_s[...])
            return _pcall(body, name=f"{name}_scan_fwd", grid=(n,), in_specs=[row] * 4 + [qkb, eg], out_specs=[row, st],
                          out_shape=[jax.ShapeDtypeStruct((S, W), F32), jax.ShapeDtypeStruct((n, W, LANES), F32)],
                          scratch_shapes=[pltpu.VMEM((W, LANES), F32)], compiler_params=_params(("arbitrary",)))(*ins)

        def body(*refs):
            st_r, do_r = refs[6:8]
            outs, ds_s = refs[8:14], refs[14]

            @pl.when(pl.program_id(0) == 0)
            def _():
                ds_s[...] = jnp.zeros_like(ds_s)
            _, vjp = jax.vjp(_dn_scan, *[r[...] for r in refs[:6]], st_r[...])
            *grads, ds = vjp((do_r[...], ds_s[...]))
            for o_r, gval in zip(outs, grads):
                o_r[...] = gval
            ds_s[...] = ds
        return _pcall(body, name=f"{name}_scan_bwd", grid=(n,), in_specs=[row] * 4 + [qkb, eg, st, row], out_specs=[row] * 4 + [qkb, eg],
                      out_shape=[jax.ShapeDtypeStruct((S, W), F32)] * 4
                      + [jax.ShapeDtypeStruct((n * QK, DN_BLOCK), F32), jax.ShapeDtypeStruct((n, 8, W), F32)],
                      scratch_shapes=[pltpu.VMEM((W, LANES), F32)], compiler_params=_params(("arbitrary",)))(*ins, saved, do)

    @jax.custom_vjp
    def local(q, k, v, g, b):
        return tuple(local_call((q, k, v, g, b))[:6])

    def local_fwd(*a):
        *outs, inverses = local_call(a)
        return tuple(outs), (*a, inverses)

    local.defvjp(local_fwd, lambda res, cts: tuple(local_call(res, tuple(cts))))

    @jax.custom_vjp
    def scan(u, w, qe, kd, qk, egl):
        return scan_call((u, w, qe, kd, qk, egl))[0]

    def scan_fwd(*a):
        o, states = scan_call(a)
        return o, (a, states)

    scan.defvjp(scan_fwd, lambda res, do: tuple(scan_call(res[0], res[1], do)))
    return lambda q, k, v, g, b: scan(*local(q, k, v, g, b))


def _loss_call(y, target):
    S, D = y.shape
    t = min(S, 512)
    n = S // t
    row = pl.BlockSpec((t, D), lambda i: (i, 0))

    def body(y_r, t_r, loss_r, dy_r, acc_s):
        i = pl.program_id(0)

        @pl.when(i == 0)
        def _():
            acc_s[...] = jnp.zeros_like(acc_s)
        err = y_r[...] - t_r[...]
        dy_r[...] = err * (1.0 / D)
        acc_s[...] += jnp.sum(err * err, axis=0, keepdims=True)

        @pl.when(i == n - 1)
        def _():
            loss_r[...] = jnp.broadcast_to(jnp.sum(acc_s[...], axis=1, keepdims=True) * (0.5 / D), loss_r.shape)

    return _pcall(body, name="loss_head", grid=(n,), in_specs=[row, row],
                  out_specs=[pl.BlockSpec((8, LANES), lambda i: (0, 0)), row],
                  out_shape=[jax.ShapeDtypeStruct((8, LANES), F32), jax.ShapeDtypeStruct((S, D), F32)],
                  scratch_shapes=[pltpu.VMEM((1, D), F32)], compiler_params=_params(("arbitrary",)))(y, target)


def _adamw_layer(name, part, w, m, v, layer, earlier=None, rows=128):
    L, R, C = w.shape
    t = _tile(R, rows, 8)
    row = pl.BlockSpec((None, t, C), lambda i: (layer, i, 0))

    def body(p_r, w_r, m_r, v_r, *rest):
        g_r, d_r, nm_r, nv_r = rest[-4:]
        g = p_r[0].astype(F32)
        for s in range(1, N_DEV):
            g = g + p_r[s].astype(F32)
        m_new = ADAM_B1 * m_r[...] + (1.0 - ADAM_B1) * g
        v_new = ADAM_B2 * v_r[...] + (1.0 - ADAM_B2) * (g * g)
        m_hat = m_new / (1.0 - ADAM_B1 ** ADAM_STEP)
        v_hat = v_new / (1.0 - ADAM_B2 ** ADAM_STEP)
        g_r[...] = g
        d_r[...] = -ADAM_LR * (m_hat / (jnp.sqrt(v_hat) + ADAM_EPS) + ADAM_WD * w_r[...])
        nm_r[...] = m_new
        nv_r[...] = v_new

    extra = [] if earlier is None else list(earlier)
    return _pcall(body, name=name, grid=(R // t,),
                  in_specs=[pl.BlockSpec((N_DEV, t, C), lambda i: (0, i, 0)), row, row, row] + [pl.BlockSpec(memory_space=pl.ANY)] * len(extra),
                  out_specs=[row] * 4, out_shape=[jax.ShapeDtypeStruct((L, R, C), F32)] * 4,
                  input_output_aliases={4 + k: k for k in range(len(extra))},
                  compiler_params=_params(("parallel",)))(part, w, m, v, *extra)


def _my_place():
    x, y, c = lax.axis_index("x"), lax.axis_index("y"), lax.axis_index("c")
    return x, y, c


def _index(x, y, c):
    return 4 * x + 2 * y + c


def _all_gather(vs):
    n = len(vs)

    def body(*refs):
        v_refs, out_refs = refs[:n], refs[n:2 * n]
        send_sems, recv_sems, local_sems = refs[2 * n:]
        x, y, c = _my_place()
        me, sibling = (x, y, c), (x, y, 1 - c)
        chips = [(1 - x, y), (x, 1 - y), (1 - x, 1 - y)]

        def copy(a, k, block, to, src=None):
            rows = out_refs[a].at[_index(*block)]
            return pltpu.make_async_remote_copy(src_ref=rows if src is None else src, dst_ref=rows, send_sem=send_sems.at[a, k],
                                                recv_sem=recv_sems.at[a, k], device_id=to, device_id_type=MESH)

        mine = [pltpu.make_async_copy(v_refs[a], out_refs[a].at[_index(*me)], local_sems.at[a]) for a in range(n)]
        first, passed = [], []
        for a in range(n):
            mine[a].start()
            first += [copy(a, 0, me, sibling, src=v_refs[a])]
            first += [copy(a, 1 + j, me, (*chip, c), src=v_refs[a]) for j, chip in enumerate(chips)]
        for cp in first:
            cp.start()
        for j, chip in enumerate(chips):
            for a in range(n):
                copy(a, 1 + j, (*chip, c), me).wait_recv()
                passed.append(copy(a, 4 + j, (*chip, c), sibling))
                passed[-1].start()
        for a in range(n):
            copy(a, 0, sibling, me).wait_recv()
            for j, chip in enumerate(chips):
                copy(a, 4 + j, (*chip, 1 - c), me).wait_recv()
        for cp in first + passed:
            cp.wait_send()
        for a in range(n):
            mine[a].wait()

    any_ = pl.BlockSpec(memory_space=pl.ANY)
    return _pcall(body, name="gather_weights", in_specs=[any_] * n, out_specs=[any_] * n,
                  out_shape=[jax.ShapeDtypeStruct((N_DEV,) + v.shape, v.dtype) for v in vs],
                  scratch_shapes=[pltpu.SemaphoreType.DMA((n, 7)), pltpu.SemaphoreType.DMA((n, 7)), pltpu.SemaphoreType.DMA((n,))])(*vs)


def _all_to_all(vs, after):
    n = len(vs)

    def body(*refs):
        v_refs, out_refs = refs[:n], refs[n + 1:2 * n + 1]
        send_sems, recv_sems, local_sems = refs[2 * n + 1:]
        x, y, c = _my_place()
        me = _index(x, y, c)
        mine = [pltpu.make_async_copy(v_refs[a].at[me], out_refs[a].at[me], local_sems.at[a]) for a in range(n)]
        copies = []
        for a in range(n):
            mine[a].start()
        for k in range(1, N_DEV):
            px = 1 - x if k & 4 else x
            py = 1 - y if k & 2 else y
            pc = 1 - c if k & 1 else c
            for a in range(n):
                cp = pltpu.make_async_remote_copy(src_ref=v_refs[a].at[_index(px, py, pc)], dst_ref=out_refs[a].at[me],
                                                  send_sem=send_sems.at[a, k - 1], recv_sem=recv_sems.at[a, k - 1],
                                                  device_id=(px, py, pc), device_id_type=MESH)
                cp.start()
                copies.append(cp)
        for cp in copies:
            cp.wait()
        for a in range(n):
            mine[a].wait()

    any_ = pl.BlockSpec(memory_space=pl.ANY)
    return _pcall(body, name="exchange_vectors", in_specs=[any_] * (n + 1), out_specs=[any_] * n,
                  out_shape=[jax.ShapeDtypeStruct(v.shape, v.dtype) for v in vs],
                  scratch_shapes=[pltpu.SemaphoreType.DMA((n, 7)), pltpu.SemaphoreType.DMA((n, 7)), pltpu.SemaphoreType.DMA((n,))])(*vs, after)


_HBM = pl.BlockSpec(memory_space=pltpu.HBM)
_SEM = pl.BlockSpec(memory_space=pltpu.SEMAPHORE)
_EFFECT = pltpu.SideEffectType.DATAFLOW_SIDE_EFFECTING


def _direct_copies(gather, v_refs, land_refs, send_sems, recv_sems, local_sems):
    x, y, c = _my_place()
    me = _index(x, y, c)
    local, remote = [], []
    for a, (v_ref, land_ref) in enumerate(zip(v_refs, land_refs)):
        local.append(pltpu.make_async_copy(v_ref if gather else v_ref.at[me], land_ref.at[me], local_sems.at[a]))
    for k in range(1, N_DEV):
        px = 1 - x if k & 4 else x
        py = 1 - y if k & 2 else y
        pc = 1 - c if k & 1 else c
        for a, (v_ref, land_ref) in enumerate(zip(v_refs, land_refs)):
            sem = a * (N_DEV - 1) + k - 1
            remote.append(pltpu.make_async_remote_copy(
                src_ref=v_ref if gather else v_ref.at[_index(px, py, pc)], dst_ref=land_ref.at[me], send_sem=send_sems.at[sem],
                recv_sem=recv_sems.at[sem], device_id=(px, py, pc), device_id_type=MESH))
    return local, remote


def _exchange_start(name, vs, gather, thru):
    n = len(vs)
    lands = [lax.empty((N_DEV,) + v.shape if gather else v.shape, v.dtype) for v in vs]

    def body(*refs):
        v_refs, land_refs = refs[:n], refs[n:2 * n]
        send_sems, recv_sems, local_sems = refs[2 * n + 1:2 * n + 4]
        local, remote = _direct_copies(gather, v_refs, land_refs, send_sems, recv_sems, local_sems)
        for cp in local + remote:
            cp.start()

    hbm = lambda a: pltpu.HBM(a.shape, a.dtype)
    res = _pcall(body, name=name,
                 out_shape=(pltpu.SemaphoreType.DMA((n * (N_DEV - 1),)), pltpu.SemaphoreType.DMA((n * (N_DEV - 1),)), pltpu.SemaphoreType.DMA((n,)),
                            *[hbm(a) for a in (*vs, *lands, thru)]),
                 in_specs=[_HBM] * (2 * n + 1), out_specs=(_SEM, _SEM, _SEM, *[_HBM] * (2 * n + 1)),
                 input_output_aliases={i: 3 + i for i in range(2 * n + 1)},
                 compiler_params=pltpu.CompilerParams(has_side_effects=_EFFECT))(
        *[pltpu.with_memory_space_constraint(a, pltpu.HBM) for a in (*vs, *lands, thru)])
    return (gather, res[:3], res[3:3 + n], res[3 + n:3 + 2 * n]), res[3 + 2 * n]


def _exchange_wait(name, started, after):
    gather, sems, vs, lands = started
    n = len(vs)

    def body(*refs):
        v_refs, land_refs = refs[:n], refs[n:2 * n]
        send_sems, recv_sems, local_sems = refs[2 * n:2 * n + 3]
        local, remote = _direct_copies(gather, v_refs, land_refs, send_sems, recv_sems, local_sems)
        for cp in local:
            cp.wait()
        for cp in remote:
            cp.wait_send()
            cp.wait_recv()

    hbm = lambda a: pltpu.HBM(a.shape, a.dtype)
    res = _pcall(body, name=name, out_shape=tuple(hbm(a) for a in (*vs, *lands)),
                 in_specs=[_HBM] * (2 * n) + [_SEM] * 3 + [pl.BlockSpec(memory_space=pl.ANY)], out_specs=tuple([_HBM] * (2 * n)),
                 input_output_aliases={i: i for i in range(2 * n)},
                 compiler_params=pltpu.CompilerParams(has_side_effects=_EFFECT))(*vs, *lands, *sems, after)
    return list(res[n:])


W_IN_SHARD = IN_WIDTH // N_DEV
SEG_ORDER = ("q_lat", "c_kv", "k_pe", "z_a", "dn_qkv", "dn_ab", "z_b", "dil_qkv", "z_c", "gate")
SEG_WIDTH = (384, 256, LANES, 512, 1536, LANES, 512, 4608, 512, 3072)


def _w_in_plan():
    plan = []

    def add(seg, c0, c1, dst):
        while c0 < c1:
            d = c0 // W_IN_SHARD
            e = min(c1, (d + 1) * W_IN_SHARD)
            plan.append((seg, dst, d, c0 - d * W_IN_SHARD, e - c0))
            dst += e - c0
            c0 = e

    half = MLA_ROPE // 2
    for i, name in enumerate(SEG_ORDER):
        if name == "k_pe":
            o = _SEG["k_pe"][0]
            add(i, o, o + half, 0)
            add(i, o + half, o + 2 * half, LANES // 2)
        elif name == "dn_ab":
            o = _SEG["dn_a"][0]
            add(i, o, o + 2 * DN_HEADS, 0)
        else:
            o, w = _SEG[name]
            add(i, o, o + w, 0)
    return plan


def _make_w_in_segments(name):
    plan = _w_in_plan()
    nseg = len(SEG_ORDER)
    t = 256

    def fwd_call(g):
        L = g.shape[1]

        def body(g_ref, *o_refs):
            for i in (SEG_ORDER.index("k_pe"), SEG_ORDER.index("dn_ab")):
                o_refs[i][...] = jnp.zeros_like(o_refs[i])
            for seg, dst, d, src, n in plan:
                o_refs[seg][:, dst:dst + n] = g_ref[d, :, src:src + n]

        return _pcall(body, name=f"{name}_fwd", grid=(L, D_MODEL // t),
                      in_specs=[pl.BlockSpec((N_DEV, None, t, W_IN_SHARD), lambda l, i: (0, l, i, 0))],
                      out_specs=[pl.BlockSpec((None, t, w), lambda l, i: (l, i, 0)) for w in SEG_WIDTH],
                      out_shape=[jax.ShapeDtypeStruct((L, D_MODEL, w), g.dtype) for w in SEG_WIDTH],
                      compiler_params=_params(("parallel", "parallel")))(g)

    def bwd_call(ds):
        L = ds[0].shape[0]

        def body(*refs):
            d_refs, g_ref = refs[:nseg], refs[nseg]
            for seg, dst, d, src, n in plan:
                g_ref[d, :, src:src + n] = d_refs[seg][:, dst:dst + n]

        return _pcall(body, name=f"{name}_bwd", grid=(L, D_MODEL // t),
                      in_specs=[pl.BlockSpec((None, t, w), lambda l, i: (l, i, 0)) for w in SEG_WIDTH],
                      out_specs=pl.BlockSpec((N_DEV, None, t, W_IN_SHARD), lambda l, i: (0, l, i, 0)),
                      out_shape=jax.ShapeDtypeStruct((N_DEV, L, D_MODEL, W_IN_SHARD), ds[0].dtype),
                      compiler_params=_params(("parallel", "parallel")))(*ds)

    @jax.custom_vjp
    def op(g):
        return tuple(fwd_call(g))

    op.defvjp(lambda g: (op(g), None), lambda _, ds: (bwd_call(tuple(ds)),))
    return op


def _pe_pad(a):
    h = MLA_ROPE // 2
    z = jnp.zeros(a.shape[:-1] + (h,), a.dtype)
    return jnp.concatenate([a[..., :h], z, a[..., h:], z], axis=-1)


def _layer_norm(tag, x, norm_g):
    return _make_rowwise(f"{tag}_norm", _f_norm, 512)((x,), (), (norm_g[None, :],), ())[0]


def _layer(tag, x, tables, W):
    h = _layer_norm(tag, x, W["norm_g"])
    return _layer_tail(tag, x, _make_multi_linear(f"{tag}_inproj", 10, INPROJ_DTYPE)(h, W["w_in_segments"]), tables, W)


def _layer_tail(tag, x, segments, tables, W):
    cos_p, sin_p, cos_h, sin_h = tables
    row = lambda a: a[None, :]
    q_lat, c_kv, kpe, z_a, dn_qkv, ab, z_b, dil_qkv, z_c, gl = segments

    qn_lat, ckvn, kp = _make_rowwise(f"{tag}_mla_a", _f_mla_a, 512)(
        (q_lat, c_kv, kpe), (cos_p, sin_p),
        (row(W["mla_q_a_norm_g"]), row(W["mla_kv_a_norm_g"]), row(_pe_pad(W["mla_k_norm_g"][LANES:]))), ())
    wq = W["mla_w_q_b"].reshape(MLA_Q_RANK, MLA_HEADS, MLA_QK)
    wq = jnp.concatenate([wq[:, :, :LANES].reshape(MLA_Q_RANK, -1), _pe_pad(wq[:, :, LANES:]).reshape(MLA_Q_RANK, -1)], axis=1)
    wkv = W["mla_w_kv_b"].reshape(MLA_KV_RANK, MLA_HEADS, 2 * LANES)
    (q8,) = _make_multi_linear(f"{tag}_qb", 1)(qn_lat, (wq,))
    kn_raw, v_mla = _make_multi_linear(f"{tag}_kvb", 2)(
        ckvn, (wkv[:, :, :LANES].reshape(MLA_KV_RANK, -1), wkv[:, :, LANES:].reshape(MLA_KV_RANK, -1)))
    qn, qp, kn = _make_rowwise(f"{tag}_mla_b", _f_mla_b, 512)(
        (q8, kn_raw), (cos_p, sin_p),
        (row(W["mla_q_norm_g"][:LANES]), row(_pe_pad(W["mla_q_norm_g"][LANES:])), row(W["mla_k_norm_g"][:LANES])), ())
    y_a = _make_mla_attn(f"{tag}_mla")(qn, qp, kn, kp, v_mla)

    mixed = _make_conv(f"{tag}_conv")(dn_qkv, W["dn_conv_w"])
    lane_head = jnp.arange(DN_HEADS * LANES) // LANES
    e_a = (jnp.arange(LANES)[:, None] == lane_head[None, :]).astype(F32)
    e_b = (jnp.arange(LANES)[:, None] == lane_head[None, :] + DN_HEADS).astype(F32)
    q_dn, k_dn, v_dn, g_dn, b_dn = _make_rowwise(f"{tag}_dn_pre", _f_dn_pre, 512)(
        (mixed, ab), (), (row(jnp.repeat(W["dn_a_log"], LANES)), row(jnp.repeat(W["dn_dt_bias"], LANES))), (e_a, e_b))
    o_dn = _make_delta_rule(f"{tag}_dn")(q_dn, k_dn, v_dn, g_dn, b_dn)

    qkv_dil = _make_rowwise(f"{tag}_dil_pre", _f_dil_pre, 256)(
        (dil_qkv,), (cos_h, sin_h), (row(W["dil_q_norm_g"]), row(W["dil_k_norm_g"])), ())
    n_groups = len(DIL_DILATIONS)
    o_lse = [_make_dil_attn(f"{tag}_dil{g}", d, DIL_TILE_ROWS[g])(qkv_dil[g], qkv_dil[n_groups + g], qkv_dil[2 * n_groups + g])
             for g, d in enumerate(DIL_DILATIONS)]

    ya, yb, yc = _make_rowwise(f"{tag}_merge_a", _f_merge_a, 256)(
        (y_a, z_a, o_dn, z_b, *[o for o, _ in o_lse], *[l for _, l in o_lse], z_c), (), (row(W["dn_out_norm_g"]),), ())
    (b0,) = _make_multi_linear(f"{tag}_br0", 1)(ya, (W["w_branch"][0],))
    (b1,) = _make_multi_linear(f"{tag}_br1", 1)(yb, (W["w_branch"][1],))
    (b2,) = _make_multi_linear(f"{tag}_br2", 1)(yc, (W["w_branch"][2],))
    (mix,) = _make_rowwise(f"{tag}_merge_b", _f_merge_b, 256)((b0, b1, b2, gl), (), (), ())
    return _make_resid_linear(f"{tag}_out")(x, mix, W["w_out"])


SHARDED = (("w_in", (D_MODEL, W_IN_SHARD)), ("mla_w_q_b", (MLA_Q_RANK, MLA_HEADS * MLA_QK // N_DEV)),
           ("mla_w_kv_b", (MLA_KV_RANK, MLA_HEADS * 2 * LANES // N_DEV)), ("w_branch", (3 * BRANCH_W, D_MODEL // N_DEV)),
           ("w_out", (D_MODEL // N_DEV, D_MODEL)), ("dn_conv_w", (DN_CONV, 3 * DN_HEADS * LANES // N_DEV)))
SMALL = (("norm_g", D_MODEL), ("mla_q_a_norm_g", MLA_Q_RANK), ("mla_kv_a_norm_g", MLA_KV_RANK), ("mla_q_norm_g", MLA_QK),
         ("mla_k_norm_g", MLA_QK), ("dn_a_log", DN_HEADS), ("dn_dt_bias", DN_HEADS), ("dn_out_norm_g", LANES),
         ("dil_q_norm_g", LANES), ("dil_k_norm_g", LANES))
WEIGHTS = ("norm_g", "w_in", "mla_q_a_norm_g", "mla_w_q_b", "mla_kv_a_norm_g", "mla_w_kv_b", "mla_q_norm_g", "mla_k_norm_g",
           "dn_conv_w", "dn_a_log", "dn_dt_bias", "dn_out_norm_g", "dil_q_norm_g", "dil_k_norm_g", "w_branch", "w_out")


def _round_up(n, m):
    return -(-n // m) * m


def _pack_vectors(pieces):
    return jnp.concatenate([jnp.pad(p, (0, _round_up(p.shape[0], LANES) - p.shape[0])) for p in pieces]).reshape(-1, LANES)


def _unpack_vectors(flat, sizes):
    out, off = [], 0
    flat = flat.reshape(-1)
    for n in sizes:
        out.append(flat[off:off + n])
        off += _round_up(n, LANES)
    return out


def _whole_weights(g, small):
    W = dict(small)
    W["mla_w_q_b"] = g["mla_w_q_b"].transpose(1, 0, 2).reshape(MLA_Q_RANK, -1)
    W["mla_w_kv_b"] = g["mla_w_kv_b"].transpose(1, 0, 2).reshape(MLA_KV_RANK, -1)
    W["w_branch"] = g["w_branch"].reshape(N_DEV, 3, BRANCH_W, -1).transpose(1, 2, 0, 3).reshape(3, BRANCH_W, D_MODEL)
    W["w_out"] = g["w_out"].reshape(D_MODEL, D_MODEL)
    W["dn_conv_w"] = g["dn_conv_w"].transpose(1, 0, 2).reshape(DN_CONV, -1)
    return W


def kernel(x, positions, norm_g, w_in, mla_q_a_norm_g, mla_w_q_b, mla_kv_a_norm_g, mla_w_kv_b, mla_q_norm_g, mla_k_norm_g, dn_conv_w, dn_a_log, dn_dt_bias, dn_out_norm_g, dil_q_norm_g, dil_k_norm_g, w_branch, w_out, loss_target, m_norm_g, m_w_in, m_mla_q_a_norm_g, m_mla_w_q_b, m_mla_kv_a_norm_g, m_mla_w_kv_b, m_mla_q_norm_g, m_mla_k_norm_g, m_dn_conv_w, m_dn_a_log, m_dn_dt_bias, m_dn_out_norm_g, m_dil_q_norm_g, m_dil_k_norm_g, m_w_branch, m_w_out, v_norm_g, v_w_in, v_mla_q_a_norm_g, v_mla_w_q_b, v_mla_kv_a_norm_g, v_mla_w_kv_b, v_mla_q_norm_g, v_mla_k_norm_g, v_dn_conv_w, v_dn_a_log, v_dn_dt_bias, v_dn_out_norm_g, v_dil_q_norm_g, v_dil_k_norm_g, v_w_branch, v_w_out):
    w = dict(norm_g=norm_g, w_in=w_in, mla_q_a_norm_g=mla_q_a_norm_g, mla_w_q_b=mla_w_q_b, mla_kv_a_norm_g=mla_kv_a_norm_g,
             mla_w_kv_b=mla_w_kv_b, mla_q_norm_g=mla_q_norm_g, mla_k_norm_g=mla_k_norm_g, dn_conv_w=dn_conv_w, dn_a_log=dn_a_log,
             dn_dt_bias=dn_dt_bias, dn_out_norm_g=dn_out_norm_g, dil_q_norm_g=dil_q_norm_g, dil_k_norm_g=dil_k_norm_g,
             w_branch=w_branch, w_out=w_out)
    m = dict(norm_g=m_norm_g, w_in=m_w_in, mla_q_a_norm_g=m_mla_q_a_norm_g, mla_w_q_b=m_mla_w_q_b, mla_kv_a_norm_g=m_mla_kv_a_norm_g,
             mla_w_kv_b=m_mla_w_kv_b, mla_q_norm_g=m_mla_q_norm_g, mla_k_norm_g=m_mla_k_norm_g, dn_conv_w=m_dn_conv_w,
             dn_a_log=m_dn_a_log, dn_dt_bias=m_dn_dt_bias, dn_out_norm_g=m_dn_out_norm_g, dil_q_norm_g=m_dil_q_norm_g,
             dil_k_norm_g=m_dil_k_norm_g, w_branch=m_w_branch, w_out=m_w_out)
    v = dict(norm_g=v_norm_g, w_in=v_w_in, mla_q_a_norm_g=v_mla_q_a_norm_g, mla_w_q_b=v_mla_w_q_b, mla_kv_a_norm_g=v_mla_kv_a_norm_g,
             mla_w_kv_b=v_mla_w_kv_b, mla_q_norm_g=v_mla_q_norm_g, mla_k_norm_g=v_mla_k_norm_g, dn_conv_w=v_dn_conv_w,
             dn_a_log=v_dn_a_log, dn_dt_bias=v_dn_dt_bias, dn_out_norm_g=v_dn_out_norm_g, dil_q_norm_g=v_dil_q_norm_g,
             dil_k_norm_g=v_dil_k_norm_g, w_branch=v_w_branch, w_out=v_w_out)
    x2, target = x[0], loss_target[0]
    pos = positions[0][:, None]

    names = [n for n, _ in SHARDED]
    view = lambda t, n, s: t[n].reshape((DEPTH,) + s)
    shards = [[(view(w, n, s) if n == "dn_conv_w" else view(w, n, s).astype(BF16))[l] for n, s in SHARDED] for l in range(DEPTH)]
    small = [{n: w[n][l] for n, _ in SMALL} for l in range(DEPTH)]
    (w_in0,) = _all_gather(shards[0][:1])
    gathering0, w_in0 = _exchange_start("gather_layer0_others_start", shards[0][1:], True, w_in0)
    gathering1, w_in0 = _exchange_start("gather_layer1_start", shards[1], True, w_in0)
    tables = _rope_tables(pos, _rope_consts())

    def layer(l, g, small_l, x_l, pending=None):
        tag = f"l{l}"
        seg_op = _make_w_in_segments(f"{tag}_w_in_segments")
        w_segs, vjp_segs = jax.vjp(lambda gw: tuple(s[0] for s in seg_op(gw[:, None])), g["w_in"])
        h, vjp_norm = jax.vjp(lambda x_, ng: _layer_norm(tag, x_, ng), x_l, small_l["norm_g"])
        wide = [i for i in range(len(w_segs)) if i not in INPROJ_NARROW]
        segs = dict(zip(INPROJ_NARROW, _mm_nn_many(f"{tag}_inproj_fwd_narrow", h, [w_segs[i] for i in INPROJ_NARROW], INPROJ_DTYPE)))
        segs.update({i: _mm(f"{tag}_inproj_fwd{i}", h, w_segs[i], "nn", out_dtype=INPROJ_DTYPE, **INPROJ_TILES["nn"]) for i in wide})
        segs = tuple(segs[i] for i in range(len(w_segs)))
        if pending is not None:
            g = dict(g, **dict(zip(names[1:], _exchange_wait(f"gather_{tag}_others_wait", pending, segs[7]))))
        rest_g = {n: a for n, a in g.items() if n != "w_in"}
        rest_s = {n: a for n, a in small_l.items() if n != "norm_g"}
        y, vjp_tail = jax.vjp(lambda sg, x_, gg, ss: _layer_tail(tag, x_, sg, tables, _whole_weights(gg, ss)), segs, x_l, rest_g, rest_s)

        def backward(dy):
            dsegs, dx_skip, d_rest_g, d_rest_s = vjp_tail(dy)
            others, big = _exchange_start(f"exchange_{tag}_others_start", [d_rest_g[n] for n in names if n != "w_in"], False, dsegs[7])
            dsegs = tuple(dsegs[:7]) + (big,) + tuple(dsegs[8:])
            dws = dict(zip(INPROJ_NARROW, _mm_tn_many(f"{tag}_inproj_dw_narrow", h, [dsegs[i] for i in INPROJ_NARROW], w_segs[0].dtype)))
            dws.update({i: _mm(f"{tag}_inproj_dw{i}", h, dsegs[i], "tn", out_dtype=w_segs[i].dtype, **INPROJ_TILES["tn"]) for i in wide})
            dws = tuple(dws[i] for i in range(len(w_segs)))
            projection, first =_exchange_start(f"exchange_{tag}_w_in_start", [vjp_segs(dws)[0]], False, dsegs[0])
            dsegs = (first,) + tuple(dsegs[1:])
            dh = None
            for i, group in enumerate(INPROJ_DH_GROUPS):
                dh = _mm_nt_sum(f"{tag}_inproj_dh{i}", [(dsegs[s], w_segs[s]) for s in group], dh,
                                h.dtype if i == len(INPROJ_DH_GROUPS) - 1 else F32)
            dx_norm, d_norm_g = vjp_norm(dh)
            return (projection, others), dx_skip + dx_norm, dict(d_rest_s, norm_g=d_norm_g)

        def landed(exchanging, after):
            projection, others = (_exchange_wait(f"exchange_{tag}_{k}_wait", e, after) for k, e in zip(("w_in", "others"), exchanging))
            return projection + others
        return y, backward, landed

    y0, backward0, landed0 = layer(0, {"w_in": w_in0}, small[0], x2, gathering0)
    gathered1 = dict(zip(names, _exchange_wait("gather_layer1_wait", gathering1, y0)))
    y1, backward1, landed1 = layer(1, gathered1, small[1], y0)
    loss_splat, dy = _loss_call(y1, target)
    loss = lax.psum(loss_splat[0, 0], ("x", "y", "c"))
    exchanging1, d_y0, g_small1 = backward1(dy)
    exchanging0, g_x, g_small0 = backward0(d_y0)

    state = lambda n, s: (view(w, n, s), view(m, n, s), view(v, n, s))
    parts1 = landed1(exchanging1, g_x)
    updated = {n: _adamw_layer(f"adamw_l1_{n}", parts1[i], *state(n, s), 1) for i, (n, s) in enumerate(SHARDED)}
    g_small = (g_small0, g_small1)
    sizes = [k for _ in range(DEPTH) for _, k in SMALL]
    g_vec = _pack_vectors([g_small[l][n] for l in range(DEPTH) for n, _ in SMALL])
    (parts_vec,) = _all_to_all([jnp.broadcast_to(g_vec[None], (N_DEV,) + g_vec.shape)], updated["w_in"][0])
    parts0 = landed0(exchanging0, parts_vec)

    vec = lambda t: _pack_vectors([t[n][l] for l in range(DEPTH) for n, _ in SMALL])[None]
    outs = {}
    for i, (n, s) in enumerate(SHARDED):
        res = _adamw_layer(f"adamw_l0_{n}", parts0[i], *state(n, s), 0, earlier=updated[n])
        outs[n] = [o.reshape(w[n].shape) for o in res]
    vec_outs = [_unpack_vectors(o, sizes) for o in _adamw_layer("adamw_vectors", parts_vec, vec(w), vec(m), vec(v), 0)]
    for i, (n, _) in enumerate(SMALL):
        outs[n] = [jnp.stack([o[l * len(SMALL) + i] for l in range(DEPTH)]) for o in vec_outs]
    return (loss, g_x[None], *[outs[n][k] for k in range(4) for n in WEIGHTS])
```

```python
import functools
import math

import jax
import jax.numpy as jnp
from jax import lax
from jax.experimental import pallas as pl
from jax.experimental.pallas import tpu as pltpu

F32 = jnp.float32
BF16 = jnp.bfloat16
HI = lax.Precision.HIGHEST
MESH = pl.DeviceIdType.MESH

N_DEV = 8
D_MODEL = 1024
DEPTH = 2
RMS_EPS = 1e-6
ROPE_THETA = 10000.0
LANES = 128
MLA_HEADS = 4
MLA_ROPE = 64
MLA_QK = 192
MLA_Q_RANK = 384
MLA_KV_RANK = 256
DN_HEADS = 4
DN_CHUNK = 64
DN_CONV = 4
DIL_HEADS = 12
DIL_GROUP_HEADS = 4
DIL_DILATIONS = (1, 4, 16)
DIL_BLOCK = 128
BRANCH_W = 512
IN_WIDTH = 11464
NEG = -1e30
VMEM_LIMIT = 56 * 1024 * 1024

ADAM_LR, ADAM_B1, ADAM_B2, ADAM_EPS, ADAM_WD, ADAM_STEP = 0.001, 0.9, 0.999, 1e-08, 0.01, 10

_SEG = {}
_off = 0
for _n, _w in (("q_lat", 384), ("c_kv", 256), ("k_pe", 64), ("z_a", 512), ("dn_qkv", 1536), ("dn_a", 4), ("dn_b", 4),
               ("z_b", 512), ("dil_qkv", 4608), ("z_c", 512), ("gate", 3072)):
    _SEG[_n] = (_off, _w)
    _off += _w
assert _off == IN_WIDTH


def _pcall(body, **kw):
    return pl.pallas_call(body, **kw)


def _params(sem=None):
    return pltpu.CompilerParams(dimension_semantics=sem, vmem_limit_bytes=VMEM_LIMIT)


def _tile(n, target, mult):
    t = (min(n, target) // mult) * mult
    while t >= mult:
        if n % t == 0:
            return t
        t -= mult
    return n


def _mm(name, a, b, mode, out_dtype=F32, acc=None, tm=1024, tn=512, tk=1024):
    if mode == "nn":
        (M, K), (_, N) = a.shape, b.shape
    elif mode == "nt":
        (M, K), (N, _) = a.shape, b.shape
    else:
        (K, M), (_, N) = a.shape, b.shape
    tm, tn, tk = _tile(M, tm, LANES), _tile(N, tn, LANES), _tile(K, tk, LANES)
    nk = K // tk
    dims = {"nn": (((1,), (0,)), ((), ())), "nt": (((1,), (1,)), ((), ())), "tn": (((0,), (0,)), ((), ()))}[mode]
    a_spec = pl.BlockSpec((tk, tm), lambda i, j, k: (k, i)) if mode == "tn" else pl.BlockSpec((tm, tk), lambda i, j, k: (i, k))
    b_spec = pl.BlockSpec((tn, tk), lambda i, j, k: (j, k)) if mode == "nt" else pl.BlockSpec((tk, tn), lambda i, j, k: (k, j))
    o_spec = pl.BlockSpec((tm, tn), lambda i, j, k: (i, j))
    has_acc = acc is not None

    def body(*refs):
        a_ref, b_ref = refs[:2]
        c_ref = refs[2] if has_acc else None
        o_ref = refs[3] if has_acc else refs[2]
        prod = lax.dot_general(a_ref[...].astype(BF16), b_ref[...].astype(BF16), dims, preferred_element_type=F32)
        if nk == 1:
            o_ref[...] = (prod + c_ref[...].astype(F32) if has_acc else prod).astype(out_dtype)
            return
        acc_ref = refs[-1]
        k = pl.program_id(2)

        @pl.when(k == 0)
        def _():
            acc_ref[...] = prod + c_ref[...].astype(F32) if has_acc else prod

        @pl.when(k > 0)
        def _():
            acc_ref[...] += prod

        @pl.when(k == nk - 1)
        def _():
            o_ref[...] = acc_ref[...].astype(out_dtype)

    ins = [a, b] + ([acc] if has_acc else [])
    in_specs = [a_spec, b_spec] + ([o_spec] if has_acc else [])
    return _pcall(body, name=name, grid=(M // tm, N // tn, nk), in_specs=in_specs, out_specs=o_spec,
                  out_shape=jax.ShapeDtypeStruct((M, N), out_dtype), scratch_shapes=[pltpu.VMEM((tm, tn), F32)] if nk > 1 else [],
                  compiler_params=_params(("parallel", "parallel", "arbitrary")))(*ins)


def _mm_nt_sum(name, pairs, acc, out_dtype, tm=512, tn=1024):
    M, N = pairs[0][0].shape[0], pairs[0][1].shape[0]
    tm, tn = _tile(M, tm, LANES), _tile(N, tn, LANES)
    n = len(pairs)
    has_acc = acc is not None
    o_spec = pl.BlockSpec((tm, tn), lambda i, j: (i, j))

    def body(*refs):
        total = refs[2 * n][...].astype(F32) if has_acc else None
        for a_ref, b_ref in zip(refs[:n], refs[n:2 * n]):
            prod = lax.dot_general(a_ref[...].astype(BF16), b_ref[...].astype(BF16), (((1,), (1,)), ((), ())), preferred_element_type=F32)
            total = prod if total is None else total + prod
        refs[-1][...] = total.astype(out_dtype)

    in_specs = ([pl.BlockSpec((tm, a.shape[1]), lambda i, j: (i, 0)) for a, _ in pairs]
                + [pl.BlockSpec((tn, b.shape[1]), lambda i, j: (j, 0)) for _, b in pairs] + ([o_spec] if has_acc else []))
    return _pcall(body, name=name, grid=(M // tm, N // tn), in_specs=in_specs, out_specs=o_spec,
                  out_shape=jax.ShapeDtypeStruct((M, N), out_dtype), compiler_params=_params(("parallel", "parallel")))(
        *[a for a, _ in pairs], *[b for _, b in pairs], *([acc] if has_acc else []))


def _mm_nn_many(name, a, bs, out_dtype, tm=1024):
    M, K = a.shape
    tm = _tile(M, tm, LANES)
    n = len(bs)

    def body(*refs):
        a_t = refs[0][...].astype(BF16)
        for b_ref, o_ref in zip(refs[1:1 + n], refs[1 + n:]):
            o_ref[...] = jnp.dot(a_t, b_ref[...].astype(BF16), preferred_element_type=F32).astype(out_dtype)

    return _pcall(body, name=name, grid=(M // tm,),
                  in_specs=[pl.BlockSpec((tm, K), lambda i: (i, 0))] + [pl.BlockSpec(b.shape, lambda i: (0, 0)) for b in bs],
                  out_specs=[pl.BlockSpec((tm, b.shape[1]), lambda i: (i, 0)) for b in bs],
                  out_shape=[jax.ShapeDtypeStruct((M, b.shape[1]), out_dtype) for b in bs],
                  compiler_params=_params(("parallel",)))(a, *bs)


def _mm_tn_many(name, a, bs, out_dtype, tk=1024):
    K, M = a.shape
    tk = _tile(K, tk, LANES)
    nk, n = K // tk, len(bs)

    def body(*refs):
        a_t = refs[0][...].astype(BF16)
        k = pl.program_id(0)
        for b_ref, o_ref, acc in zip(refs[1:1 + n], refs[1 + n:1 + 2 * n], refs[1 + 2 * n:]):
            prod = lax.dot_general(a_t, b_ref[...].astype(BF16), (((0,), (0,)), ((), ())), preferred_element_type=F32)

            @pl.when(k == 0)
            def _(acc=acc, prod=prod):
                acc[...] = prod

            @pl.when(k > 0)
            def _(acc=acc, prod=prod):
                acc[...] += prod

            @pl.when(k == nk - 1)
            def _(acc=acc, o_ref=o_ref):
                o_ref[...] = acc[...].astype(out_dtype)

    return _pcall(body, name=name, grid=(nk,),
                  in_specs=[pl.BlockSpec((tk, M), lambda k: (k, 0))] + [pl.BlockSpec((tk, b.shape[1]), lambda k: (k, 0)) for b in bs],
                  out_specs=[pl.BlockSpec((M, b.shape[1]), lambda k: (0, 0)) for b in bs],
                  out_shape=[jax.ShapeDtypeStruct((M, b.shape[1]), out_dtype) for b in bs],
                  scratch_shapes=[pltpu.VMEM((M, b.shape[1]), F32) for b in bs],
                  compiler_params=_params(("arbitrary",)))(a, *bs)


INPROJ_DH_GROUPS = ((0, 1, 2, 3, 5, 6, 8), (4,), (9,), (7,))
INPROJ_NARROW = INPROJ_DH_GROUPS[0]

INPROJ_DTYPE = BF16

INPROJ_TILES = {"nn": dict(tm=1024, tn=1536, tk=1024), "tn": dict(tm=1024, tn=768, tk=2048)}


def _make_multi_linear(name, n, out_dtype=F32):
    @jax.custom_vjp
    def op(h, ws):
        return tuple(_mm(f"{name}_fwd{i}", h, w, "nn", out_dtype=out_dtype) for i, w in enumerate(ws))

    def fwd(h, ws):
        return op(h, ws), (h, ws)

    def bwd(res, douts):
        h, ws = res
        dh = None
        for i, (w, d) in enumerate(zip(ws, douts)):
            dh = _mm(f"{name}_dh{i}", d, w, "nt", acc=dh, out_dtype=h.dtype if i == len(ws) - 1 else F32)
        dws = tuple(_mm(f"{name}_dw{i}", h, d, "tn", out_dtype=w.dtype) for i, (w, d) in enumerate(zip(ws, douts)))
        return dh, dws

    op.defvjp(fwd, bwd)
    return op


def _make_resid_linear(name):
    @jax.custom_vjp
    def op(x, a, w):
        return _mm(f"{name}_fwd", a, w, "nn", acc=x)

    def fwd(x, a, w):
        return op(x, a, w), (a, w)

    def bwd(res, dy):
        a, w = res
        return dy, _mm(f"{name}_da", dy, w, "nt", out_dtype=a.dtype), _mm(f"{name}_dw", a, dy, "tn", out_dtype=w.dtype)

    op.defvjp(fwd, bwd)
    return op


def _make_rowwise(name, f, tile):
    def specs(rows, aux, params, consts, t):
        row = [pl.BlockSpec((t, a.shape[1]), lambda i: (i, 0)) for a in (*rows, *aux)]
        full = [pl.BlockSpec(p.shape, lambda i: (0, 0)) for p in (*params, *consts)]
        return row, full

    def fwd_call(rows, aux, params, consts):
        S = rows[0].shape[0]
        t = min(tile, S)
        n_in = len(rows) + len(aux) + len(params) + len(consts)
        shp = lambda a: jax.ShapeDtypeStruct((t, a.shape[1]), a.dtype)
        outs = jax.eval_shape(f, *[shp(a) for a in (*rows, *aux)], *params, *consts)
        row_specs, full_specs = specs(rows, aux, params, consts, t)

        def body(*refs):
            res = f(*[r[...] for r in refs[:n_in]])
            for o_ref, o in zip(refs[n_in:], res):
                o_ref[...] = o

        return _pcall(body, name=f"{name}_fwd", grid=(S // t,), in_specs=row_specs + full_specs,
                      out_specs=[pl.BlockSpec((t, o.shape[1]), lambda i: (i, 0)) for o in outs],
                      out_shape=[jax.ShapeDtypeStruct((S, o.shape[1]), o.dtype) for o in outs],
                      compiler_params=_params(("parallel",)))(*rows, *aux, *params, *consts)

    def bwd_call(rows, aux, params, consts, douts):
        S = rows[0].shape[0]
        t = min(tile, S)
        nr, na, npar, nc, nd = len(rows), len(aux), len(params), len(consts), len(douts)
        row_specs, full_specs = specs(rows, aux, params, consts, t)

        def body(*refs):
            vals = [r[...] for r in refs[:nr + na + npar + nc]]
            rv, av = vals[:nr], vals[nr:nr + na]
            pv, cv = vals[nr + na:nr + na + npar], vals[nr + na + npar:]
            dv = tuple(r[...] for r in refs[nr + na + npar + nc:nr + na + npar + nc + nd])
            out_refs = refs[nr + na + npar + nc + nd:]
            _, vjp = jax.vjp(lambda *rp: f(*rp[:nr], *av, *rp[nr:], *cv), *rv, *pv)
            grads = vjp(dv)
            for o_ref, g in zip(out_refs[:nr], grads[:nr]):
                o_ref[...] = g
            first = pl.program_id(0) == 0
            for o_ref, g in zip(out_refs[nr:], grads[nr:]):
                @pl.when(first)
                def _(o_ref=o_ref):
                    o_ref[...] = jnp.zeros_like(o_ref)
                o_ref[...] += g

        res = _pcall(body, name=f"{name}_bwd", grid=(S // t,),
                     in_specs=row_specs + full_specs + [pl.BlockSpec((t, d.shape[1]), lambda i: (i, 0)) for d in douts],
                     out_specs=[pl.BlockSpec((t, a.shape[1]), lambda i: (i, 0)) for a in rows]
                     + [pl.BlockSpec(p.shape, lambda i: (0, 0)) for p in params],
                     out_shape=[jax.ShapeDtypeStruct(a.shape, a.dtype) for a in (*rows, *params)],
                     compiler_params=_params(("arbitrary",)))(*rows, *aux, *params, *consts, *douts)
        return tuple(res[:nr]), tuple(res[nr:])

    @jax.custom_vjp
    def op(rows, aux, params, consts):
        return tuple(fwd_call(rows, aux, params, consts))

    def fwd(rows, aux, params, consts):
        return op(rows, aux, params, consts), (rows, aux, params, consts)

    def bwd(res, douts):
        rows, aux, params, consts = res
        drows, dparams = bwd_call(rows, aux, params, consts, tuple(douts))
        zeros = lambda xs: tuple(jnp.zeros_like(a) for a in xs)
        return drows, zeros(aux), dparams, zeros(consts)

    op.defvjp(fwd, bwd)
    return op


@jax.custom_vjp
def _swap_halves(x):
    return pltpu.roll(x, LANES // 2, 1)


_swap_halves.defvjp(lambda x: (_swap_halves(x), None), lambda _, g: (_swap_halves(g),))


def _rope(x, cos_t, sin_t):
    return x * cos_t + _swap_halves(x) * sin_t


def _rms(x, g, n=None):
    n = x.shape[-1] if n is None else n
    return x * lax.rsqrt(jnp.sum(x * x, axis=-1, keepdims=True) * (1.0 / n) + RMS_EPS) * g


def _heads(x):
    return [x[:, i * LANES:(i + 1) * LANES] for i in range(x.shape[1] // LANES)]


def _cat(xs):
    return jnp.concatenate(xs, axis=1)


def _silu(x):
    return x * jax.nn.sigmoid(x)


def _f_norm(x, g):
    return (_rms(x, g).astype(BF16),)


def _f_mla_a(q_lat, c_kv, kpe, cos_p, sin_p, qa_g, kva_g, kpe_g):
    q_lat, c_kv, kpe = (t.astype(F32) for t in (q_lat, c_kv, kpe))
    kp =_rope(_rms(kpe, kpe_g, MLA_ROPE), cos_p, sin_p)
    return _rms(q_lat, qa_g).astype(BF16), _rms(c_kv, kva_g).astype(BF16), _cat([kp] * MLA_HEADS)


def _f_mla_b(q8, kn_raw, cos_p, sin_p, qn_g, qp_g, kn_g):
    hs = _heads(q8)
    qn = _cat([_rms(h, qn_g) for h in hs[:MLA_HEADS]])
    qp = _cat([_rope(_rms(h, qp_g, MLA_ROPE), cos_p, sin_p) for h in hs[MLA_HEADS:]])
    kn = _cat([_rms(h, kn_g) for h in _heads(kn_raw)])
    return qn, qp, kn


def _softplus(x):
    return jnp.maximum(x, 0.0) + jnp.log(1.0 + jnp.exp(-jnp.abs(x)))


def _l2n(x):
    return x * lax.rsqrt(jnp.sum(x * x, axis=-1, keepdims=True) + 1e-6)


def _f_dn_pre(mixed, ab, alog_f, dtb_f, e_a, e_b):
    hs = _heads(mixed)
    q = _cat([_l2n(h) * (LANES ** -0.5) for h in hs[:DN_HEADS]])
    k = _cat([_l2n(h) for h in hs[DN_HEADS:2 * DN_HEADS]])
    v = _cat(hs[2 * DN_HEADS:])
    ab = ab.astype(F32)
    a_f = jnp.dot(ab, e_a, precision=HI, preferred_element_type=F32)
    b_f = jnp.dot(ab, e_b, precision=HI, preferred_element_type=F32)
    g = -jnp.exp(alog_f) * _softplus(a_f + dtb_f)
    return q, k, v, g, jax.nn.sigmoid(b_f)


def _f_dil_pre(qkv, cos_h, sin_h, q_g, k_g):
    hs = [h.astype(F32) for h in _heads(qkv)]
    q = [_rope(_rms(h, q_g), cos_h, sin_h) for h in hs[:DIL_HEADS]]
    k = [_rope(_rms(h, k_g), cos_h, sin_h) for h in hs[DIL_HEADS:2 * DIL_HEADS]]
    v = hs[2 * DIL_HEADS:]
    group = lambda xs, g: _cat(xs[g * DIL_GROUP_HEADS:(g + 1) * DIL_GROUP_HEADS])
    return tuple(group(xs, g) for xs in (q, k, v) for g in range(len(DIL_DILATIONS)))


def _f_merge_a(y_a, z_a, o_dn, z_b, o0, o1, o2, l0, l1, l2, z_c, out_g):
    z_a, z_b, z_c = (z.astype(F32) for z in (z_a, z_b, z_c))
    y_b = _cat([_rms(h, out_g) for h in _heads(o_dn)])
    os_, ls = [_heads(o) for o in (o0, o1, o2)], [_heads(l) for l in (l0, l1, l2)]
    y_c = []
    for j in range(DIL_GROUP_HEADS):
        l3 = [ls[g][j] for g in range(3)]
        m = jnp.maximum(jnp.maximum(l3[0], l3[1]), l3[2])
        e3 = [jnp.exp(l - m) for l in l3]
        den = e3[0] + e3[1] + e3[2]
        y_c.append(sum(e3[g] * os_[g][j] for g in range(3)) / den)
    return tuple(t.astype(BF16) for t in (y_a * _silu(z_a), y_b * _silu(z_b), _cat(y_c) * _silu(z_c)))


def _f_merge_b(b0, b1, b2, gl):
    gs = [jax.nn.sigmoid(gl[:, i * D_MODEL:(i + 1) * D_MODEL].astype(F32)) for i in range(3)]
    return ((gs[0] * b0 + gs[1] * b1 + gs[2] * b2).astype(BF16),)


def _rope_tables(pos, inv_sign):
    S = pos.shape[0]
    t = min(S, 1024)

    def body(p_ref, c_ref, cp, sp, ch, sh):
        p = p_ref[...].astype(F32)
        c = c_ref[...]
        ang_p, ang_h = p * c[0:1], p * c[2:3]
        cp[...] = jnp.cos(ang_p) * jnp.abs(c[1:2])
        sp[...] = jnp.sin(ang_p) * c[1:2]
        ch[...] = jnp.cos(ang_h)
        sh[...] = jnp.sin(ang_h) * c[3:4]

    row = pl.BlockSpec((t, LANES), lambda i: (i, 0))
    return _pcall(body, name="rope_tables", grid=(S // t,),
                  in_specs=[pl.BlockSpec((t, 1), lambda i: (i, 0)), pl.BlockSpec((4, LANES), lambda i: (0, 0))],
                  out_specs=[row] * 4, out_shape=[jax.ShapeDtypeStruct((S, LANES), F32)] * 4,
                  compiler_params=_params(("parallel",)))(pos, inv_sign)


def _rope_consts():
    half_p, half_h = MLA_ROPE // 2, LANES // 2
    inv_p = 1.0 / (ROPE_THETA ** (jnp.arange(0, MLA_ROPE, 2, dtype=F32) / MLA_ROPE))
    inv_h = 1.0 / (ROPE_THETA ** (jnp.arange(0, LANES, 2, dtype=F32) / LANES))
    z = jnp.zeros((half_p,), F32)
    o = jnp.ones((half_p,), F32)
    return jnp.stack([jnp.concatenate([inv_p, z, inv_p, z]), jnp.concatenate([-o, z, o, z]),
                      jnp.concatenate([inv_h, inv_h]), jnp.concatenate([-jnp.ones((half_h,), F32), jnp.ones((half_h,), F32)])])


def _shift_rows(x, s, up):
    n = x.shape[0]
    r = lax.broadcasted_iota(jnp.int32, x.shape, 0)
    if up:
        return jnp.where(r < n - s, pltpu.roll(x, n - s, 0), 0.0)
    return jnp.where(r >= s, pltpu.roll(x, s, 0), 0.0)


def _make_shift(s):
    @jax.custom_vjp
    def sh(x):
        return _shift_rows(x, s, False)

    sh.defvjp(lambda x: (sh(x), None), lambda _, g: (_shift_rows(g, s, True),))
    return sh


def _f_conv(x, w):
    x = x.astype(F32)
    y = x * w[DN_CONV - 1:DN_CONV]
    for j in range(DN_CONV - 1):
        y = y + _make_shift(DN_CONV - 1 - j)(x) * w[j:j + 1]
    return _silu(y)


def _make_conv(name):
    def call(x, w, dy=None):
        S, C = x.shape
        col = pl.BlockSpec((S, LANES), lambda i: (0, i))
        wsp = pl.BlockSpec((DN_CONV, LANES), lambda i: (0, i))
        if dy is None:
            def body(x_ref, w_ref, o_ref):
                o_ref[...] = _f_conv(x_ref[...], w_ref[...])
            return _pcall(body, name=f"{name}_fwd", grid=(C // LANES,), in_specs=[col, wsp], out_specs=col,
                          out_shape=jax.ShapeDtypeStruct(x.shape, F32), compiler_params=_params(("parallel",)))(x, w)

        def body(x_ref, w_ref, dy_ref, dx_ref, dw_ref):
            _, vjp = jax.vjp(_f_conv, x_ref[...], w_ref[...])
            dx_ref[...], dw_ref[...] = vjp(dy_ref[...])
        return _pcall(body, name=f"{name}_bwd", grid=(C // LANES,), in_specs=[col, wsp, col], out_specs=[col, wsp],
                      out_shape=[jax.ShapeDtypeStruct(x.shape, x.dtype), jax.ShapeDtypeStruct(w.shape, F32)],
                      compiler_params=_params(("parallel",)))(x, w, dy)

    @jax.custom_vjp
    def op(x, w):
        return call(x, w)

    op.defvjp(lambda x, w: (op(x, w), (x, w)), lambda res, dy: tuple(call(*res, dy)))
    return op


def _dot_nt(a, b):
    return lax.dot_general(a.astype(BF16), b.astype(BF16), (((1,), (1,)), ((), ())), preferred_element_type=F32)


def _dot_nn(a, b):
    return jnp.dot(a.astype(BF16), b.astype(BF16), preferred_element_type=F32)


def _dot_tn(a, b):
    return lax.dot_general(a.astype(BF16), b.astype(BF16), (((0,), (0,)), ((), ())), preferred_element_type=F32)


def _mla_scores(qn, qp, kn, kp, diagonal):
    scale = MLA_QK ** -0.5
    s = _dot_nt(qn * scale, kn) + _dot_nt(qp * scale, kp)
    if diagonal:
        r = lax.broadcasted_iota(jnp.int32, s.shape, 0)
        c = lax.broadcasted_iota(jnp.int32, s.shape, 1)
        s = jnp.where(c <= r, s, NEG)
    return s


def _on_causal_pairs(qi, ki, step):
    @pl.when(ki < qi)
    def _():
        step(False)

    @pl.when(ki == qi)
    def _():
        step(True)


def _causal_pairs(n, t, by_key, heads):
    pairs = [(q, k) for k in range(n) for q in range(k, n)] if by_key else [(q, k) for q in range(n) for k in range(q + 1)]
    qt, kt = (jnp.array([p[i] for p in pairs], jnp.int32) for i in (0, 1))
    return (qt, kt, pl.BlockSpec((t, heads * LANES), lambda h, p, qt_r, kt_r: (qt_r[p], h)),
            pl.BlockSpec((t, heads * LANES), lambda h, p, qt_r, kt_r: (kt_r[p], h)))


MLA_FWD_HEADS = 4
MLA_BWD_HEADS = 2


def _make_mla_attn(name):
    scale = MLA_QK ** -0.5

    def fwd_call(qn, qp, kn, kp, v):
        S = qn.shape[0]
        t = min(S, 512)
        n = S // t
        hp = MLA_FWD_HEADS
        qt, kt, qs, ks = _causal_pairs(n, t, False, hp)
        cols = [slice(j * LANES, (j + 1) * LANES) for j in range(hp)]

        def body(qt_r, kt_r, qn_r, qp_r, kn_r, kp_r, v_r, o_r, lse_r, m_s, l_s, acc_s):
            qi, ki = qt_r[pl.program_id(1)], kt_r[pl.program_id(1)]

            @pl.when(ki == 0)
            def _():
                m_s[...] = jnp.full_like(m_s, NEG)
                l_s[...] = jnp.zeros_like(l_s)
                acc_s[...] = jnp.zeros_like(acc_s)

            def step(diagonal):
                for c in cols:
                    s = _mla_scores(qn_r[:, c], qp_r[:, c], kn_r[:, c], kp_r[:, c], diagonal)
                    m_old = m_s[:, c]
                    m_new = jnp.maximum(m_old, jnp.max(s, axis=-1, keepdims=True))
                    p = jnp.exp(s - m_new[:, :1])
                    alpha = jnp.exp(m_old - m_new)
                    l_s[:, c] = alpha * l_s[:, c] + jnp.sum(p, axis=-1, keepdims=True)
                    acc_s[:, c] = alpha * acc_s[:, c] + _dot_nn(p, v_r[:, c])
                    m_s[:, c] = m_new
            _on_causal_pairs(qi, ki, step)

            @pl.when(ki == qi)
            def _():
                o_r[...] = acc_s[...] / l_s[...]
                lse_r[...] = m_s[...] + jnp.log(l_s[...])

        spec = pltpu.PrefetchScalarGridSpec(num_scalar_prefetch=2, grid=(MLA_HEADS // hp, qt.shape[0]), in_specs=[qs, qs, ks, ks, ks],
                                            out_specs=[qs, qs], scratch_shapes=[pltpu.VMEM((t, hp * LANES), F32)] * 3)
        return _pcall(body, name=f"{name}_fwd", grid_spec=spec, out_shape=[jax.ShapeDtypeStruct((S, MLA_HEADS * LANES), F32)] * 2,
                      compiler_params=_params(("parallel", "arbitrary")))(qt, kt, qn, qp, kn, kp, v)

    def bwd_call(qn, qp, kn, kp, v, o, lse, do):
        S = qn.shape[0]
        t = min(S, 512)
        n = S // t
        hp = MLA_BWD_HEADS
        qt, kt, qs, ks = _causal_pairs(n, t, True, hp)
        head = pl.BlockSpec((S, hp * LANES), lambda h, p, qt_r, kt_r: (0, h))
        n_pairs = qt.shape[0]
        cols = [slice(j * LANES, (j + 1) * LANES) for j in range(hp)]

        def body(qt_r, kt_r, qn_r, qp_r, kn_r, kp_r, v_r, o_r, lse_r, do_r, dqn_r, dqp_r, dkn_r, dkp_r, dv_r, dkn_s, dkp_s, dv_s, dl_s):
            pair = pl.program_id(1)
            qi, ki = qt_r[pair], kt_r[pair]
            rows = pl.ds(pl.multiple_of(qi * t, t), t)

            @pl.when(pair == 0)
            def _():
                dqn_r[...] = jnp.zeros_like(dqn_r)
                dqp_r[...] = jnp.zeros_like(dqp_r)

            @pl.when(ki == 0)
            def _():
                for c in cols:
                    dl_s[rows, c] = jnp.broadcast_to(jnp.sum(do_r[:, c] * o_r[:, c], axis=-1, keepdims=True), (t, LANES))

            @pl.when(qi == ki)
            def _():
                dkn_s[...] = jnp.zeros_like(dkn_s)
                dkp_s[...] = jnp.zeros_like(dkp_s)
                dv_s[...] = jnp.zeros_like(dv_s)

            def step(diagonal):
                for c in cols:
                    qn, qp, kn, kp, do = qn_r[:, c], qp_r[:, c], kn_r[:, c], kp_r[:, c], do_r[:, c]
                    p = jnp.exp(_mla_scores(qn, qp, kn, kp, diagonal) - lse_r[:, c][:, :1])
                    ds = p * (_dot_nt(do, v_r[:, c]) - dl_s[rows, c][:, :1])
                    dv_s[:, c] += _dot_tn(p, do)
                    dkn_s[:, c] += _dot_tn(ds, qn * scale)
                    dkp_s[:, c] += _dot_tn(ds, qp * scale)
                    dqn_r[rows, c] += _dot_nn(ds, kn)
                    dqp_r[rows, c] += _dot_nn(ds, kp)
            _on_causal_pairs(qi, ki, step)

            @pl.when(qi == n - 1)
            def _():
                dkn_r[...] = dkn_s[...]
                dkp_r[...] = dkp_s[...]
                dv_r[...] = dv_s[...]

            @pl.when(pair == n_pairs - 1)
            def _():
                dqn_r[...] = dqn_r[...] * scale
                dqp_r[...] = dqp_r[...] * scale

        spec = pltpu.PrefetchScalarGridSpec(num_scalar_prefetch=2, grid=(MLA_HEADS // hp, n_pairs), in_specs=[qs, qs, ks, ks, ks, qs, qs, qs],
                                            out_specs=[head, head, ks, ks, ks],
                                            scratch_shapes=[pltpu.VMEM((t, hp * LANES), F32)] * 3 + [pltpu.VMEM((S, hp * LANES), F32)])
        return _pcall(body, name=f"{name}_bwd", grid_spec=spec, out_shape=[jax.ShapeDtypeStruct((S, MLA_HEADS * LANES), F32)] * 5,
                      compiler_params=_params(("parallel", "arbitrary")))(qt, kt, qn, qp, kn, kp, v, o, lse, do)

    @jax.custom_vjp
    def op(qn, qp, kn, kp, v):
        return fwd_call(qn, qp, kn, kp, v)[0]

    def fwd(qn, qp, kn, kp, v):
        o, lse = fwd_call(qn, qp, kn, kp, v)
        return o, (qn, qp, kn, kp, v, o, lse)

    def bwd(res, do):
        return tuple(bwd_call(*res, do))

    op.defvjp(fwd, bwd)
    return op


def _dil_block(q, kp, kc, vp, vc, has_prev):
    scale = LANES ** -0.5
    r = lax.broadcasted_iota(jnp.int32, (DIL_BLOCK, 2 * DIL_BLOCK), 0)
    c = lax.broadcasted_iota(jnp.int32, (DIL_BLOCK, 2 * DIL_BLOCK), 1)
    valid = ((c < DIL_BLOCK) & (c >= r) & has_prev) | ((c >= DIL_BLOCK) & (c - DIL_BLOCK <= r))
    s = jnp.where(valid, _dot_nt(q * scale, jnp.concatenate([kp, kc], axis=0)), NEG)
    m = jnp.max(s, axis=-1, keepdims=True)
    e = jnp.exp(s - m)
    den = jnp.sum(e, axis=-1, keepdims=True)
    o = _dot_nn(e, jnp.concatenate([vp, vc], axis=0)) / den
    return o, jnp.broadcast_to(m + jnp.log(den), o.shape)


DIL_TILE_ROWS = (1024, 1024, 2048)
DIL_STEP_HEADS = 2


def _make_dil_attn(name, d, tile_rows):
    def call(q, k, v, cts=None):
        S = q.shape[0]
        span = DIL_BLOCK * d
        G = max(1, min(tile_rows, S) // span)
        n = S // (G * span)
        at = (lambda i: i) if cts is None else (lambda i: n - 1 - i)
        step_heads = DIL_STEP_HEADS if d == 1 else 1
        width = step_heads * LANES
        tile = pl.BlockSpec((G * span, width), lambda h, i: (at(i), h))
        before = pl.BlockSpec((span, width), lambda h, i: (jnp.maximum(at(i) * G - 1, 0), h))
        heads = [slice(h * LANES, (h + 1) * LANES) for h in range(step_heads)]

        def rows(r, j):
            return pl.ds(j * DIL_BLOCK, DIL_BLOCK) if d == 1 else pl.ds(r + j * span, DIL_BLOCK, stride=d)

        def over_residues(fn):
            if d == 1:
                fn(0)
            else:
                lax.fori_loop(0, d, lambda r, c: (fn(r), c)[1], 0)

        def block_inputs(r, j, c, q_r, kb_r, k_r, vb_r, v_r):
            kp = kb_r[rows(r, 0), c] if j == 0 else k_r[rows(r, j - 1), c]
            vp = vb_r[rows(r, 0), c] if j == 0 else v_r[rows(r, j - 1), c]
            return q_r[rows(r, j), c], kp, k_r[rows(r, j), c], vp, v_r[rows(r, j), c]

        if cts is None:
            def body(q_r, kb_r, k_r, vb_r, v_r, o_r, lse_r):
                first = at(pl.program_id(1)) * G

                def residue(r):
                    for j in range(G):
                        for c in heads:
                            o_r[rows(r, j), c], lse_r[rows(r, j), c] = _dil_block(
                                *block_inputs(r, j, c, q_r, kb_r, k_r, vb_r, v_r), first + j > 0)
                over_residues(residue)
            return _pcall(body, name=f"{name}_fwd", grid=(DIL_GROUP_HEADS // step_heads, n), in_specs=[tile, before, tile, before, tile],
                          out_specs=[tile, tile], out_shape=[jax.ShapeDtypeStruct(q.shape, F32)] * 2,
                          compiler_params=_params(("parallel", "parallel")))(q, k, k, v, v)

        def body(q_r, kb_r, k_r, vb_r, v_r, do_r, dl_r, dq_r, dk_r, dv_r, ck_s, cv_s):
            first = at(pl.program_id(1)) * G

            @pl.when(pl.program_id(1) == 0)
            def _():
                ck_s[...] = jnp.zeros_like(ck_s)
                cv_s[...] = jnp.zeros_like(cv_s)

            def residue(r):
                for c in heads:
                    owed = None
                    for j in range(G):
                        hp = first + j > 0
                        _, vjp = jax.vjp(lambda *a: _dil_block(*a, hp), *block_inputs(r, j, c, q_r, kb_r, k_r, vb_r, v_r))
                        dq, dkp, dkc, dvp, dvc = vjp((do_r[rows(r, j), c], dl_r[rows(r, j), c]))
                        dq_r[rows(r, j), c] = dq
                        if j == G - 1:
                            dkc, dvc = dkc + ck_s[rows(r, 0), c], dvc + cv_s[rows(r, 0), c]
                        dk_r[rows(r, j), c], dv_r[rows(r, j), c] = dkc, dvc
                        if j == 0:
                            owed = (dkp, dvp)
                        else:
                            dk_r[rows(r, j - 1), c] += dkp
                            dv_r[rows(r, j - 1), c] += dvp
                    ck_s[rows(r, 0), c], cv_s[rows(r, 0), c] = owed
            over_residues(residue)
        return _pcall(body, name=f"{name}_bwd", grid=(DIL_GROUP_HEADS // step_heads, n), in_specs=[tile, before, tile, before, tile, tile, tile],
                      out_specs=[tile] * 3, out_shape=[jax.ShapeDtypeStruct(q.shape, F32)] * 3,
                      scratch_shapes=[pltpu.VMEM((span, width), F32)] * 2,
                      compiler_params=_params(("parallel", "arbitrary")))(q, k, k, v, v, *cts)

    @jax.custom_vjp
    def op(q, k, v):
        return tuple(call(q, k, v))

    op.defvjp(lambda q, k, v: (op(q, k, v), (q, k, v)), lambda res, cts: tuple(call(*res, cts=cts)))
    return op


def _pdot(a, b, dims):
    return lax.dot_general(a, b, (dims, ((), ())), precision=lax.Precision.HIGH, preferred_element_type=F32)


DN_LOCAL_CHUNKS = 4
DN_SCAN_CHUNKS = 4


DN_BLOCK_HEADS = 4
DN_BLOCK = DN_BLOCK_HEADS * DN_CHUNK


def _inverse_cotangent(inv, d):
    return -_pdot(inv, _pdot(d, inv, ((1,), (1,))), ((0,), (0,)))


@jax.custom_vjp
def _unit_lower_inverse(a):
    n = a.shape[0]
    eye = (lax.broadcasted_iota(jnp.int32, (n, n), 0) == lax.broadcasted_iota(jnp.int32, (n, n), 1)).astype(F32)
    inv, pw = eye - a, a
    for level in range(5):
        dot = _pdot if level < 2 else (lambda x, y, dims: lax.dot_general(x, y, (dims, ((), ())), preferred_element_type=F32))
        pw = dot(pw, pw, ((1,), (0,)))
        inv = inv + dot(inv, pw, ((1,), (0,)))
    return inv


def _unit_lower_inverse_fwd(a):
    inv = _unit_lower_inverse(a)
    return inv, inv


_unit_lower_inverse.defvjp(_unit_lower_inverse_fwd, lambda inv, d: (_inverse_cotangent(inv, d),))


@jax.custom_vjp
def _known_inverse(a, inv):
    return inv


_known_inverse.defvjp(lambda a, inv: (inv, inv), lambda inv, d: (_inverse_cotangent(inv, d), jnp.zeros_like(inv)))


def _dn_local(q, k, v, g, b, known=None):
    C, R = DN_CHUNK, DN_BLOCK
    r = lax.broadcasted_iota(jnp.int32, (R, R), 0)
    c = lax.broadcasted_iota(jnp.int32, (R, R), 1)
    same_head = (r // C) == (c // C)
    incl, strict = same_head & (r >= c), same_head & (r > c)
    avg = jnp.full((R, LANES), 1.0 / LANES, F32)
    rc = lax.broadcasted_iota(jnp.int32, (C, C), 0) >= lax.broadcasted_iota(jnp.int32, (C, C), 1)
    gc_lanes = _pdot(rc.astype(F32), g, ((1,), (0,)))
    us, ws, qes, kds, qks, invs = [], [], [], [], [], []
    for first in range(0, DN_HEADS, DN_BLOCK_HEADS):
        stack = lambda x: jnp.concatenate(_heads(x)[first:first + DN_BLOCK_HEADS], axis=0)
        unstack = lambda x: [x[p * C:(p + 1) * C] for p in range(DN_BLOCK_HEADS)]
        gc, q_s, k_s, v_s, b_s = (stack(x) for x in (gc_lanes, q, k, v, b))
        gc_j = _pdot(avg, gc, ((1,), (1,)))
        decay = jnp.exp(jnp.where(incl, _cat([gc] * (R // LANES)) - gc_j, NEG))
        kb = k_s * b_s
        kk = _pdot(jnp.concatenate([kb, q_s], axis=0), k_s, ((1,), (1,)))
        a = jnp.where(strict, kk[:R] * decay, 0.0)
        inv = _unit_lower_inverse(a) if known is None else _known_inverse(a, known[len(invs) * R:(len(invs) + 1) * R])
        invs.append(inv)
        eg = jnp.exp(gc)
        uw = _pdot(inv, _cat([v_s * b_s, kb * eg]), ((1,), (0,)))
        g_last = jnp.concatenate([jnp.broadcast_to(x[C - 1:C], (C, LANES)) for x in unstack(gc)], axis=0)
        us += unstack(uw[:, :LANES])
        ws += unstack(uw[:, LANES:])
        qes += unstack(q_s * eg)
        kds += unstack(k_s * jnp.exp(g_last - gc))
        qks.append(kk[R:] * decay)
    egl = jnp.broadcast_to(jnp.exp(gc_lanes[C - 1:C]), (8, DN_HEADS * LANES))
    return _cat(us), _cat(ws), _cat(qes), _cat(kds), jnp.concatenate(qks, axis=0), egl, jnp.concatenate(invs, axis=0)


def _dn_scan(u, w, qe, kd, qk, egl, state):
    C = DN_CHUNK
    heads = [slice(h * LANES, (h + 1) * LANES) for h in range(DN_HEADS)]
    ws = [_pdot(jnp.concatenate([w[:, sl], qe[:, sl]], axis=0), state[sl, :], ((1,), (0,))) for sl in heads]
    v_new = [u[:, sl] - x[:C] for sl, x in zip(heads, ws)]
    local = []
    for i, first in enumerate(range(0, DN_HEADS, DN_BLOCK_HEADS)):
        y = _pdot(qk[i * DN_BLOCK:(i + 1) * DN_BLOCK], jnp.concatenate(v_new[first:first + DN_BLOCK_HEADS], axis=0), ((1,), (0,)))
        local += [y[p * C:(p + 1) * C] for p in range(DN_BLOCK_HEADS)]
    o = _cat([x[C:] + y for x, y in zip(ws, local)])
    states = [state[sl, :] * egl[0:1, sl] + _pdot(kd[:, sl], vn, ((0,), (0,))) for sl, vn in zip(heads, v_new)]
    return o, jnp.concatenate(states, axis=0)


def _make_delta_rule(name):
    W = DN_HEADS * LANES
    QK = DN_HEADS * DN_CHUNK

    def local_call(ins, cts=None):
        S = ins[0].shape[0]
        n = S // DN_CHUNK
        per = math.gcd(DN_LOCAL_CHUNKS, n)
        row = pl.BlockSpec((per * DN_CHUNK, W), lambda i: (i, 0))
        qkb = pl.BlockSpec((per * QK, DN_BLOCK), lambda i: (i, 0))
        eg = pl.BlockSpec((per, 8, W), lambda i: (i, 0, 0))
        rows = lambda j: slice(j * DN_CHUNK, (j + 1) * DN_CHUNK)
        qk_rows = lambda j: slice(j * QK, (j + 1) * QK)
        out_rows = [rows, rows, rows, rows, qk_rows]

        if cts is None:
            def body(*refs):
                for j in range(per):
                    res = _dn_local(*[r[rows(j), :] for r in refs[:5]])
                    for o_r, o, at_ in zip(refs[5:10], res[:5], out_rows):
                        o_r[at_(j), :] = o
                    refs[10][j] = res[5]
                    refs[11][qk_rows(j), :] = res[6]
            blockdiag = jax.ShapeDtypeStruct((n * QK, DN_BLOCK), F32)
            return _pcall(body, name=f"{name}_local_fwd", grid=(n // per,), in_specs=[row] * 5, out_specs=[row] * 4 + [qkb, eg, qkb],
                          out_shape=[jax.ShapeDtypeStruct((S, W), F32)] * 4 + [blockdiag, jax.ShapeDtypeStruct((n, 8, W), F32), blockdiag],
                          compiler_params=_params(("parallel",)))(*ins)

        def body(*refs):
            for j in range(per):
                known = refs[5][qk_rows(j), :]
                _, vjp = jax.vjp(lambda *a: _dn_local(*a, known=known)[:6], *[r[rows(j), :] for r in refs[:5]])
                grads = vjp(tuple(r[at_(j), :] for r, at_ in zip(refs[6:11], out_rows)) + (refs[11][j],))
                for o_r, o in zip(refs[12:], grads):
                    o_r[rows(j), :] = o
        return _pcall(body, name=f"{name}_local_bwd", grid=(n // per,), in_specs=[row] * 5 + [qkb] + [row] * 4 + [qkb, eg],
                      out_specs=[row] * 5, out_shape=[jax.ShapeDtypeStruct((S, W), F32)] * 5,
                      compiler_params=_params(("parallel",)))(*ins, *cts)

    def scan_call(ins, saved=None, do=None):
        S = ins[0].shape[0]
        n = S // DN_CHUNK
        per = math.gcd(DN_SCAN_CHUNKS, n)
        steps = n // per
        at = (lambda i: i) if do is None else (lambda i: steps - 1 - i)
        row = pl.BlockSpec((per * DN_CHUNK, W), lambda i: (at(i), 0))
        qkb = pl.BlockSpec((per * QK, DN_BLOCK), lambda i: (at(i), 0))
        eg = pl.BlockSpec((per, 8, W), lambda i: (at(i), 0, 0))
        st = pl.BlockSpec((per, W, LANES), lambda i: (at(i), 0, 0))
        rows = lambda j: slice(j * DN_CHUNK, (j + 1) * DN_CHUNK)
        qk_rows = lambda j: slice(j * QK, (j + 1) * QK)
        chunk = lambda refs, j: [r[rows(j), :] for r in refs[:4]] + [refs[4][qk_rows(j), :], refs[5][j]]

        if do is None:
            def body(*refs):
                o_r, st_r, s_s = refs[6:]

                @pl.when(pl.program_id(0) == 0)
                def _():
                    s_s[...] = jnp.zeros_like(s_s)
                state = s_s[...]
                for j in range(per):
                    st_r[j] = state
                    o_r[rows(j), :], state = _dn_scan(*chunk(refs, j), state)
                s_s[...] = state
            return _pcall(body, name=f"{name}_scan_fwd", grid=(steps,), in_specs=[row] * 4 + [qkb, eg], out_specs=[row, st],
                          out_shape=[jax.ShapeDtypeStruct((S, W), F32), jax.ShapeDtypeStruct((n, W, LANES), F32)],
                          scratch_shapes=[pltpu.VMEM((W, LANES), F32)], compiler_params=_params(("arbitrary",)))(*ins)

        def body(*refs):
            st_r, do_r = refs[6:8]
            outs, ds_s = refs[8:14], refs[14]

            @pl.when(pl.program_id(0) == 0)
            def _():
                ds_s[...] = jnp.zeros_like(ds_s)
            ds = ds_s[...]
            for j in reversed(range(per)):
                _, vjp = jax.vjp(_dn_scan, *chunk(refs, j), st_r[j])
                *grads, ds = vjp((do_r[rows(j), :], ds))
                for o_r, gval in zip(outs[:4], grads[:4]):
                    o_r[rows(j), :] = gval
                outs[4][qk_rows(j), :] = grads[4]
                outs[5][j] = grads[5]
            ds_s[...] = ds
        return _pcall(body, name=f"{name}_scan_bwd", grid=(steps,), in_specs=[row] * 4 + [qkb, eg, st, row], out_specs=[row] * 4 + [qkb, eg],
                      out_shape=[jax.ShapeDtypeStruct((S, W), F32)] * 4
                      + [jax.ShapeDtypeStruct((n * QK, DN_BLOCK), F32), jax.ShapeDtypeStruct((n, 8, W), F32)],
                      scratch_shapes=[pltpu.VMEM((W, LANES), F32)], compiler_params=_params(("arbitrary",)))(*ins, saved, do)

    @jax.custom_vjp
    def local(q, k, v, g, b):
        return tuple(local_call((q, k, v, g, b))[:6])

    def local_fwd(*a):
        *outs, inverses = local_call(a)
        return tuple(outs), (*a, inverses)

    local.defvjp(local_fwd, lambda res, cts: tuple(local_call(res, tuple(cts))))

    @jax.custom_vjp
    def scan(u, w, qe, kd, qk, egl):
        return scan_call((u, w, qe, kd, qk, egl))[0]

    def scan_fwd(*a):
        o, states = scan_call(a)
        return o, (a, states)

    scan.defvjp(scan_fwd, lambda res, do: tuple(scan_call(res[0], res[1], do)))
    return lambda q, k, v, g, b: scan(*local(q, k, v, g, b))


def _loss_call(y, target):
    S, D = y.shape
    t = min(S, 512)
    n = S // t
    row = pl.BlockSpec((t, D), lambda i: (i, 0))

    def body(y_r, t_r, loss_r, dy_r, acc_s):
        i = pl.program_id(0)

        @pl.when(i == 0)
        def _():
            acc_s[...] = jnp.zeros_like(acc_s)
        err = y_r[...] - t_r[...]
        dy_r[...] = err * (1.0 / D)
        acc_s[...] += jnp.sum(err * err, axis=0, keepdims=True)

        @pl.when(i == n - 1)
        def _():
            loss_r[...] = jnp.broadcast_to(jnp.sum(acc_s[...], axis=1, keepdims=True) * (0.5 / D), loss_r.shape)

    return _pcall(body, name="loss_head", grid=(n,), in_specs=[row, row],
                  out_specs=[pl.BlockSpec((8, LANES), lambda i: (0, 0)), row],
                  out_shape=[jax.ShapeDtypeStruct((8, LANES), F32), jax.ShapeDtypeStruct((S, D), F32)],
                  scratch_shapes=[pltpu.VMEM((1, D), F32)], compiler_params=_params(("arbitrary",)))(y, target)


def _adamw_layer(name, part, w, m, v, layer, earlier=None, rows=128):
    L, R, C = w.shape
    t = _tile(R, rows, 8)
    row = pl.BlockSpec((None, t, C), lambda i: (layer, i, 0))

    def body(p_r, w_r, m_r, v_r, *rest):
        g_r, d_r, nm_r, nv_r = rest[-4:]
        g = p_r[0].astype(F32)
        for s in range(1, N_DEV):
            g = g + p_r[s].astype(F32)
        m_new = ADAM_B1 * m_r[...] + (1.0 - ADAM_B1) * g
        v_new = ADAM_B2 * v_r[...] + (1.0 - ADAM_B2) * (g * g)
        m_hat = m_new / (1.0 - ADAM_B1 ** ADAM_STEP)
        v_hat = v_new / (1.0 - ADAM_B2 ** ADAM_STEP)
        g_r[...] = g
        d_r[...] = -ADAM_LR * (m_hat / (jnp.sqrt(v_hat) + ADAM_EPS) + ADAM_WD * w_r[...])
        nm_r[...] = m_new
        nv_r[...] = v_new

    extra = [] if earlier is None else list(earlier)
    return _pcall(body, name=name, grid=(R // t,),
                  in_specs=[pl.BlockSpec((N_DEV, t, C), lambda i: (0, i, 0)), row, row, row] + [pl.BlockSpec(memory_space=pl.ANY)] * len(extra),
                  out_specs=[row] * 4, out_shape=[jax.ShapeDtypeStruct((L, R, C), F32)] * 4,
                  input_output_aliases={4 + k: k for k in range(len(extra))},
                  compiler_params=_params(("parallel",)))(part, w, m, v, *extra)


def _my_place():
    x, y, c = lax.axis_index("x"), lax.axis_index("y"), lax.axis_index("c")
    return x, y, c


def _index(x, y, c):
    return 4 * x + 2 * y + c


def _all_gather(vs):
    n = len(vs)

    def body(*refs):
        v_refs, out_refs = refs[:n], refs[n:2 * n]
        send_sems, recv_sems, local_sems = refs[2 * n:]
        x, y, c = _my_place()
        me, sibling = (x, y, c), (x, y, 1 - c)
        chips = [(1 - x, y), (x, 1 - y), (1 - x, 1 - y)]

        def copy(a, k, block, to, src=None):
            rows = out_refs[a].at[_index(*block)]
            return pltpu.make_async_remote_copy(src_ref=rows if src is None else src, dst_ref=rows, send_sem=send_sems.at[a, k],
                                                recv_sem=recv_sems.at[a, k], device_id=to, device_id_type=MESH)

        mine = [pltpu.make_async_copy(v_refs[a], out_refs[a].at[_index(*me)], local_sems.at[a]) for a in range(n)]
        first, passed = [], []
        for a in range(n):
            mine[a].start()
            first += [copy(a, 0, me, sibling, src=v_refs[a])]
            first += [copy(a, 1 + j, me, (*chip, c), src=v_refs[a]) for j, chip in enumerate(chips)]
        for cp in first:
            cp.start()
        for j, chip in enumerate(chips):
            for a in range(n):
                copy(a, 1 + j, (*chip, c), me).wait_recv()
                passed.append(copy(a, 4 + j, (*chip, c), sibling))
                passed[-1].start()
        for a in range(n):
            copy(a, 0, sibling, me).wait_recv()
            for j, chip in enumerate(chips):
                copy(a, 4 + j, (*chip, 1 - c), me).wait_recv()
        for cp in first + passed:
            cp.wait_send()
        for a in range(n):
            mine[a].wait()

    any_ = pl.BlockSpec(memory_space=pl.ANY)
    return _pcall(body, name="gather_weights", in_specs=[any_] * n, out_specs=[any_] * n,
                  out_shape=[jax.ShapeDtypeStruct((N_DEV,) + v.shape, v.dtype) for v in vs],
                  scratch_shapes=[pltpu.SemaphoreType.DMA((n, 7)), pltpu.SemaphoreType.DMA((n, 7)), pltpu.SemaphoreType.DMA((n,))])(*vs)


def _all_to_all(vs, after):
    n = len(vs)

    def body(*refs):
        v_refs, out_refs = refs[:n], refs[n + 1:2 * n + 1]
        send_sems, recv_sems, local_sems = refs[2 * n + 1:]
        x, y, c = _my_place()
        me = _index(x, y, c)
        mine = [pltpu.make_async_copy(v_refs[a].at[me], out_refs[a].at[me], local_sems.at[a]) for a in range(n)]
        copies = []
        for a in range(n):
            mine[a].start()
        for k in range(1, N_DEV):
            px = 1 - x if k & 4 else x
            py = 1 - y if k & 2 else y
            pc = 1 - c if k & 1 else c
            for a in range(n):
                cp = pltpu.make_async_remote_copy(src_ref=v_refs[a].at[_index(px, py, pc)], dst_ref=out_refs[a].at[me],
                                                  send_sem=send_sems.at[a, k - 1], recv_sem=recv_sems.at[a, k - 1],
                                                  device_id=(px, py, pc), device_id_type=MESH)
                cp.start()
                copies.append(cp)
        for cp in copies:
            cp.wait()
        for a in range(n):
            mine[a].wait()

    any_ = pl.BlockSpec(memory_space=pl.ANY)
    return _pcall(body, name="exchange_vectors", in_specs=[any_] * (n + 1), out_specs=[any_] * n,
                  out_shape=[jax.ShapeDtypeStruct(v.shape, v.dtype) for v in vs],
                  scratch_shapes=[pltpu.SemaphoreType.DMA((n, 7)), pltpu.SemaphoreType.DMA((n, 7)), pltpu.SemaphoreType.DMA((n,))])(*vs, after)


_HBM = pl.BlockSpec(memory_space=pltpu.HBM)
_SEM = pl.BlockSpec(memory_space=pltpu.SEMAPHORE)
_EFFECT = pltpu.SideEffectType.DATAFLOW_SIDE_EFFECTING


def _direct_copies(gather, v_refs, land_refs, send_sems, recv_sems, local_sems):
    x, y, c = _my_place()
    me = _index(x, y, c)
    local, remote = [], []
    for a, (v_ref, land_ref) in enumerate(zip(v_refs, land_refs)):
        local.append(pltpu.make_async_copy(v_ref if gather else v_ref.at[me], land_ref.at[me], local_sems.at[a]))
    for k in range(1, N_DEV):
        px = 1 - x if k & 4 else x
        py = 1 - y if k & 2 else y
        pc = 1 - c if k & 1 else c
        for a, (v_ref, land_ref) in enumerate(zip(v_refs, land_refs)):
            sem = a * (N_DEV - 1) + k - 1
            remote.append(pltpu.make_async_remote_copy(
                src_ref=v_ref if gather else v_ref.at[_index(px, py, pc)], dst_ref=land_ref.at[me], send_sem=send_sems.at[sem],
                recv_sem=recv_sems.at[sem], device_id=(px, py, pc), device_id_type=MESH))
    return local, remote


def _exchange_start(name, vs, gather, thru):
    n = len(vs)
    lands = [lax.empty((N_DEV,) + v.shape if gather else v.shape, v.dtype) for v in vs]

    def body(*refs):
        v_refs, land_refs = refs[:n], refs[n:2 * n]
        send_sems, recv_sems, local_sems = refs[2 * n + 1:2 * n + 4]
        local, remote = _direct_copies(gather, v_refs, land_refs, send_sems, recv_sems, local_sems)
        for cp in local + remote:
            cp.start()

    hbm = lambda a: pltpu.HBM(a.shape, a.dtype)
    res = _pcall(body, name=name,
                 out_shape=(pltpu.SemaphoreType.DMA((n * (N_DEV - 1),)), pltpu.SemaphoreType.DMA((n * (N_DEV - 1),)), pltpu.SemaphoreType.DMA((n,)),
                            *[hbm(a) for a in (*vs, *lands, thru)]),
                 in_specs=[_HBM] * (2 * n + 1), out_specs=(_SEM, _SEM, _SEM, *[_HBM] * (2 * n + 1)),
                 input_output_aliases={i: 3 + i for i in range(2 * n + 1)},
                 compiler_params=pltpu.CompilerParams(has_side_effects=_EFFECT))(
        *[pltpu.with_memory_space_constraint(a, pltpu.HBM) for a in (*vs, *lands, thru)])
    return (gather, res[:3], res[3:3 + n], res[3 + n:3 + 2 * n]), res[3 + 2 * n]


def _exchange_wait(name, started, after):
    gather, sems, vs, lands = started
    n = len(vs)

    def body(*refs):
        v_refs, land_refs = refs[:n], refs[n:2 * n]
        send_sems, recv_sems, local_sems = refs[2 * n:2 * n + 3]
        local, remote = _direct_copies(gather, v_refs, land_refs, send_sems, recv_sems, local_sems)
        for cp in local:
            cp.wait()
        for cp in remote:
            cp.wait_send()
            cp.wait_recv()

    hbm = lambda a: pltpu.HBM(a.shape, a.dtype)
    res = _pcall(body, name=name, out_shape=tuple(hbm(a) for a in (*vs, *lands)),
                 in_specs=[_HBM] * (2 * n) + [_SEM] * 3 + [pl.BlockSpec(memory_space=pl.ANY)], out_specs=tuple([_HBM] * (2 * n)),
                 input_output_aliases={i: i for i in range(2 * n)},
                 compiler_params=pltpu.CompilerParams(has_side_effects=_EFFECT))(*vs, *lands, *sems, after)
    return list(res[n:])


W_IN_SHARD = IN_WIDTH // N_DEV
SEG_ORDER = ("q_lat", "c_kv", "k_pe", "z_a", "dn_qkv", "dn_ab", "z_b", "dil_qkv", "z_c", "gate")
SEG_WIDTH = (384, 256, LANES, 512, 1536, LANES, 512, 4608, 512, 3072)


def _w_in_plan():
    plan = []

    def add(seg, c0, c1, dst):
        while c0 < c1:
            d = c0 // W_IN_SHARD
            e = min(c1, (d + 1) * W_IN_SHARD)
            plan.append((seg, dst, d, c0 - d * W_IN_SHARD, e - c0))
            dst += e - c0
            c0 = e

    half = MLA_ROPE // 2
    for i, name in enumerate(SEG_ORDER):
        if name == "k_pe":
            o = _SEG["k_pe"][0]
            add(i, o, o + half, 0)
            add(i, o + half, o + 2 * half, LANES // 2)
        elif name == "dn_ab":
            o = _SEG["dn_a"][0]
            add(i, o, o + 2 * DN_HEADS, 0)
        else:
            o, w = _SEG[name]
            add(i, o, o + w, 0)
    return plan


def _make_w_in_segments(name):
    plan = _w_in_plan()
    nseg = len(SEG_ORDER)
    t = 256

    def fwd_call(g):
        L = g.shape[1]

        def body(g_ref, *o_refs):
            for i in (SEG_ORDER.index("k_pe"), SEG_ORDER.index("dn_ab")):
                o_refs[i][...] = jnp.zeros_like(o_refs[i])
            for seg, dst, d, src, n in plan:
                o_refs[seg][:, dst:dst + n] = g_ref[d, :, src:src + n]

        return _pcall(body, name=f"{name}_fwd", grid=(L, D_MODEL // t),
                      in_specs=[pl.BlockSpec((N_DEV, None, t, W_IN_SHARD), lambda l, i: (0, l, i, 0))],
                      out_specs=[pl.BlockSpec((None, t, w), lambda l, i: (l, i, 0)) for w in SEG_WIDTH],
                      out_shape=[jax.ShapeDtypeStruct((L, D_MODEL, w), g.dtype) for w in SEG_WIDTH],
                      compiler_params=_params(("parallel", "parallel")))(g)

    def bwd_call(ds):
        L = ds[0].shape[0]

        def body(*refs):
            d_refs, g_ref = refs[:nseg], refs[nseg]
            for seg, dst, d, src, n in plan:
                g_ref[d, :, src:src + n] = d_refs[seg][:, dst:dst + n]

        return _pcall(body, name=f"{name}_bwd", grid=(L, D_MODEL // t),
                      in_specs=[pl.BlockSpec((None, t, w), lambda l, i: (l, i, 0)) for w in SEG_WIDTH],
                      out_specs=pl.BlockSpec((N_DEV, None, t, W_IN_SHARD), lambda l, i: (0, l, i, 0)),
                      out_shape=jax.ShapeDtypeStruct((N_DEV, L, D_MODEL, W_IN_SHARD), ds[0].dtype),
                      compiler_params=_params(("parallel", "parallel")))(*ds)

    @jax.custom_vjp
    def op(g):
        return tuple(fwd_call(g))

    op.defvjp(lambda g: (op(g), None), lambda _, ds: (bwd_call(tuple(ds)),))
    return op


def _pe_pad(a):
    h = MLA_ROPE // 2
    z = jnp.zeros(a.shape[:-1] + (h,), a.dtype)
    return jnp.concatenate([a[..., :h], z, a[..., h:], z], axis=-1)


def _layer_norm(tag, x, norm_g):
    return _make_rowwise(f"{tag}_norm", _f_norm, 512)((x,), (), (norm_g[None, :],), ())[0]


def _layer(tag, x, tables, W):
    h = _layer_norm(tag, x, W["norm_g"])
    return _layer_tail(tag, x, _make_multi_linear(f"{tag}_inproj", 10, INPROJ_DTYPE)(h, W["w_in_segments"]), tables, W)


def _layer_tail(tag, x, segments, tables, W):
    cos_p, sin_p, cos_h, sin_h = tables
    row = lambda a: a[None, :]
    q_lat, c_kv, kpe, z_a, dn_qkv, ab, z_b, dil_qkv, z_c, gl = segments

    qn_lat, ckvn, kp = _make_rowwise(f"{tag}_mla_a", _f_mla_a, 512)(
        (q_lat, c_kv, kpe), (cos_p, sin_p),
        (row(W["mla_q_a_norm_g"]), row(W["mla_kv_a_norm_g"]), row(_pe_pad(W["mla_k_norm_g"][LANES:]))), ())
    wq = W["mla_w_q_b"].reshape(MLA_Q_RANK, MLA_HEADS, MLA_QK)
    wq = jnp.concatenate([wq[:, :, :LANES].reshape(MLA_Q_RANK, -1), _pe_pad(wq[:, :, LANES:]).reshape(MLA_Q_RANK, -1)], axis=1)
    wkv = W["mla_w_kv_b"].reshape(MLA_KV_RANK, MLA_HEADS, 2 * LANES)
    (q8,) = _make_multi_linear(f"{tag}_qb", 1)(qn_lat, (wq,))
    kn_raw, v_mla = _make_multi_linear(f"{tag}_kvb", 2)(
        ckvn, (wkv[:, :, :LANES].reshape(MLA_KV_RANK, -1), wkv[:, :, LANES:].reshape(MLA_KV_RANK, -1)))
    qn, qp, kn = _make_rowwise(f"{tag}_mla_b", _f_mla_b, 512)(
        (q8, kn_raw), (cos_p, sin_p),
        (row(W["mla_q_norm_g"][:LANES]), row(_pe_pad(W["mla_q_norm_g"][LANES:])), row(W["mla_k_norm_g"][:LANES])), ())
    y_a = _make_mla_attn(f"{tag}_mla")(qn, qp, kn, kp, v_mla)

    mixed = _make_conv(f"{tag}_conv")(dn_qkv, W["dn_conv_w"])
    lane_head = jnp.arange(DN_HEADS * LANES) // LANES
    e_a = (jnp.arange(LANES)[:, None] == lane_head[None, :]).astype(F32)
    e_b = (jnp.arange(LANES)[:, None] == lane_head[None, :] + DN_HEADS).astype(F32)
    q_dn, k_dn, v_dn, g_dn, b_dn = _make_rowwise(f"{tag}_dn_pre", _f_dn_pre, 512)(
        (mixed, ab), (), (row(jnp.repeat(W["dn_a_log"], LANES)), row(jnp.repeat(W["dn_dt_bias"], LANES))), (e_a, e_b))
    o_dn = _make_delta_rule(f"{tag}_dn")(q_dn, k_dn, v_dn, g_dn, b_dn)

    qkv_dil = _make_rowwise(f"{tag}_dil_pre", _f_dil_pre, 256)(
        (dil_qkv,), (cos_h, sin_h), (row(W["dil_q_norm_g"]), row(W["dil_k_norm_g"])), ())
    n_groups = len(DIL_DILATIONS)
    o_lse = [_make_dil_attn(f"{tag}_dil{g}", d, DIL_TILE_ROWS[g])(qkv_dil[g], qkv_dil[n_groups + g], qkv_dil[2 * n_groups + g])
             for g, d in enumerate(DIL_DILATIONS)]

    ya, yb, yc = _make_rowwise(f"{tag}_merge_a", _f_merge_a, 256)(
        (y_a, z_a, o_dn, z_b, *[o for o, _ in o_lse], *[l for _, l in o_lse], z_c), (), (row(W["dn_out_norm_g"]),), ())
    (b0,) = _make_multi_linear(f"{tag}_br0", 1)(ya, (W["w_branch"][0],))
    (b1,) = _make_multi_linear(f"{tag}_br1", 1)(yb, (W["w_branch"][1],))
    (b2,) = _make_multi_linear(f"{tag}_br2", 1)(yc, (W["w_branch"][2],))
    (mix,) = _make_rowwise(f"{tag}_merge_b", _f_merge_b, 256)((b0, b1, b2, gl), (), (), ())
    return _make_resid_linear(f"{tag}_out")(x, mix, W["w_out"])


SHARDED = (("w_in", (D_MODEL, W_IN_SHARD)), ("mla_w_q_b", (MLA_Q_RANK, MLA_HEADS * MLA_QK // N_DEV)),
           ("mla_w_kv_b", (MLA_KV_RANK, MLA_HEADS * 2 * LANES // N_DEV)), ("w_branch", (3 * BRANCH_W, D_MODEL // N_DEV)),
           ("w_out", (D_MODEL // N_DEV, D_MODEL)), ("dn_conv_w", (DN_CONV, 3 * DN_HEADS * LANES // N_DEV)))
SMALL = (("norm_g", D_MODEL), ("mla_q_a_norm_g", MLA_Q_RANK), ("mla_kv_a_norm_g", MLA_KV_RANK), ("mla_q_norm_g", MLA_QK),
         ("mla_k_norm_g", MLA_QK), ("dn_a_log", DN_HEADS), ("dn_dt_bias", DN_HEADS), ("dn_out_norm_g", LANES),
         ("dil_q_norm_g", LANES), ("dil_k_norm_g", LANES))
WEIGHTS = ("norm_g", "w_in", "mla_q_a_norm_g", "mla_w_q_b", "mla_kv_a_norm_g", "mla_w_kv_b", "mla_q_norm_g", "mla_k_norm_g",
           "dn_conv_w", "dn_a_log", "dn_dt_bias", "dn_out_norm_g", "dil_q_norm_g", "dil_k_norm_g", "w_branch", "w_out")


def _round_up(n, m):
    return -(-n // m) * m


def _pack_vectors(pieces):
    return jnp.concatenate([jnp.pad(p, (0, _round_up(p.shape[0], LANES) - p.shape[0])) for p in pieces]).reshape(-1, LANES)


def _unpack_vectors(flat, sizes):
    out, off = [], 0
    flat = flat.reshape(-1)
    for n in sizes:
        out.append(flat[off:off + n])
        off += _round_up(n, LANES)
    return out


def _whole_weights(g, small):
    W = dict(small)
    W["mla_w_q_b"] = g["mla_w_q_b"].transpose(1, 0, 2).reshape(MLA_Q_RANK, -1)
    W["mla_w_kv_b"] = g["mla_w_kv_b"].transpose(1, 0, 2).reshape(MLA_KV_RANK, -1)
    W["w_branch"] = g["w_branch"].reshape(N_DEV, 3, BRANCH_W, -1).transpose(1, 2, 0, 3).reshape(3, BRANCH_W, D_MODEL)
    W["w_out"] = g["w_out"].reshape(D_MODEL, D_MODEL)
    W["dn_conv_w"] = g["dn_conv_w"].transpose(1, 0, 2).reshape(DN_CONV, -1)
    return W


def kernel(x, positions, norm_g, w_in, mla_q_a_norm_g, mla_w_q_b, mla_kv_a_norm_g, mla_w_kv_b, mla_q_norm_g, mla_k_norm_g, dn_conv_w, dn_a_log, dn_dt_bias, dn_out_norm_g, dil_q_norm_g, dil_k_norm_g, w_branch, w_out, loss_target, m_norm_g, m_w_in, m_mla_q_a_norm_g, m_mla_w_q_b, m_mla_kv_a_norm_g, m_mla_w_kv_b, m_mla_q_norm_g, m_mla_k_norm_g, m_dn_conv_w, m_dn_a_log, m_dn_dt_bias, m_dn_out_norm_g, m_dil_q_norm_g, m_dil_k_norm_g, m_w_branch, m_w_out, v_norm_g, v_w_in, v_mla_q_a_norm_g, v_mla_w_q_b, v_mla_kv_a_norm_g, v_mla_w_kv_b, v_mla_q_norm_g, v_mla_k_norm_g, v_dn_conv_w, v_dn_a_log, v_dn_dt_bias, v_dn_out_norm_g, v_dil_q_norm_g, v_dil_k_norm_g, v_w_branch, v_w_out):
    w = dict(norm_g=norm_g, w_in=w_in, mla_q_a_norm_g=mla_q_a_norm_g, mla_w_q_b=mla_w_q_b, mla_kv_a_norm_g=mla_kv_a_norm_g,
             mla_w_kv_b=mla_w_kv_b, mla_q_norm_g=mla_q_norm_g, mla_k_norm_g=mla_k_norm_g, dn_conv_w=dn_conv_w, dn_a_log=dn_a_log,
             dn_dt_bias=dn_dt_bias, dn_out_norm_g=dn_out_norm_g, dil_q_norm_g=dil_q_norm_g, dil_k_norm_g=dil_k_norm_g,
             w_branch=w_branch, w_out=w_out)
    m = dict(norm_g=m_norm_g, w_in=m_w_in, mla_q_a_norm_g=m_mla_q_a_norm_g, mla_w_q_b=m_mla_w_q_b, mla_kv_a_norm_g=m_mla_kv_a_norm_g,
             mla_w_kv_b=m_mla_w_kv_b, mla_q_norm_g=m_mla_q_norm_g, mla_k_norm_g=m_mla_k_norm_g, dn_conv_w=m_dn_conv_w,
             dn_a_log=m_dn_a_log, dn_dt_bias=m_dn_dt_bias, dn_out_norm_g=m_dn_out_norm_g, dil_q_norm_g=m_dil_q_norm_g,
             dil_k_norm_g=m_dil_k_norm_g, w_branch=m_w_branch, w_out=m_w_out)
    v = dict(norm_g=v_norm_g, w_in=v_w_in, mla_q_a_norm_g=v_mla_q_a_norm_g, mla_w_q_b=v_mla_w_q_b, mla_kv_a_norm_g=v_mla_kv_a_norm_g,
             mla_w_kv_b=v_mla_w_kv_b, mla_q_norm_g=v_mla_q_norm_g, mla_k_norm_g=v_mla_k_norm_g, dn_conv_w=v_dn_conv_w,
             dn_a_log=v_dn_a_log, dn_dt_bias=v_dn_dt_bias, dn_out_norm_g=v_dn_out_norm_g, dil_q_norm_g=v_dil_q_norm_g,
             dil_k_norm_g=v_dil_k_norm_g, w_branch=v_w_branch, w_out=v_w_out)
    x2, target = x[0], loss_target[0]
    pos = positions[0][:, None]

    names = [n for n, _ in SHARDED]
    view = lambda t, n, s: t[n].reshape((DEPTH,) + s)
    shards = [[(view(w, n, s) if n == "dn_conv_w" else view(w, n, s).astype(BF16))[l] for n, s in SHARDED] for l in range(DEPTH)]
    small = [{n: w[n][l] for n, _ in SMALL} for l in range(DEPTH)]
    (w_in0,) = _all_gather(shards[0][:1])
    gathering0, w_in0 = _exchange_start("gather_layer0_others_start", shards[0][1:], True, w_in0)
    gathering1, w_in0 = _exchange_start("gather_layer1_start", shards[1], True, w_in0)
    tables = _rope_tables(pos, _rope_consts())

    def layer(l, g, small_l, x_l, pending=None):
        tag = f"l{l}"
        seg_op = _make_w_in_segments(f"{tag}_w_in_segments")
        w_segs, vjp_segs = jax.vjp(lambda gw: tuple(s[0] for s in seg_op(gw[:, None])), g["w_in"])
        h, vjp_norm = jax.vjp(lambda x_, ng: _layer_norm(tag, x_, ng), x_l, small_l["norm_g"])
        wide = [i for i in range(len(w_segs)) if i not in INPROJ_NARROW]
        segs = dict(zip(INPROJ_NARROW, _mm_nn_many(f"{tag}_inproj_fwd_narrow", h, [w_segs[i] for i in INPROJ_NARROW], INPROJ_DTYPE)))
        segs.update({i: _mm(f"{tag}_inproj_fwd{i}", h, w_segs[i], "nn", out_dtype=INPROJ_DTYPE, **INPROJ_TILES["nn"]) for i in wide})
        segs = tuple(segs[i] for i in range(len(w_segs)))
        if pending is not None:
            g = dict(g, **dict(zip(names[1:], _exchange_wait(f"gather_{tag}_others_wait", pending, segs[7]))))
        rest_g = {n: a for n, a in g.items() if n != "w_in"}
        rest_s = {n: a for n, a in small_l.items() if n != "norm_g"}
        y, vjp_tail = jax.vjp(lambda sg, x_, gg, ss: _layer_tail(tag, x_, sg, tables, _whole_weights(gg, ss)), segs, x_l, rest_g, rest_s)

        def backward(dy):
            dsegs, dx_skip, d_rest_g, d_rest_s = vjp_tail(dy)
            others, big = _exchange_start(f"exchange_{tag}_others_start", [d_rest_g[n] for n in names if n != "w_in"], False, dsegs[7])
            dsegs = tuple(dsegs[:7]) + (big,) + tuple(dsegs[8:])
            dws = dict(zip(INPROJ_NARROW, _mm_tn_many(f"{tag}_inproj_dw_narrow", h, [dsegs[i] for i in INPROJ_NARROW], w_segs[0].dtype)))
            dws.update({i: _mm(f"{tag}_inproj_dw{i}", h, dsegs[i], "tn", out_dtype=w_segs[i].dtype, **INPROJ_TILES["tn"]) for i in wide})
            dws = tuple(dws[i] for i in range(len(w_segs)))
            projection, first =_exchange_start(f"exchange_{tag}_w_in_start", [vjp_segs(dws)[0]], False, dsegs[0])
            dsegs = (first,) + tuple(dsegs[1:])
            dh = None
            for i, group in enumerate(INPROJ_DH_GROUPS):
                dh = _mm_nt_sum(f"{tag}_inproj_dh{i}", [(dsegs[s], w_segs[s]) for s in group], dh,
                                h.dtype if i == len(INPROJ_DH_GROUPS) - 1 else F32)
            dx_norm, d_norm_g = vjp_norm(dh)
            return (projection, others), dx_skip + dx_norm, dict(d_rest_s, norm_g=d_norm_g)

        def landed(exchanging, after):
            projection, others = (_exchange_wait(f"exchange_{tag}_{k}_wait", e, after) for k, e in zip(("w_in", "others"), exchanging))
            return projection + others
        return y, backward, landed

    y0, backward0, landed0 = layer(0, {"w_in": w_in0}, small[0], x2, gathering0)
    gathered1 = dict(zip(names, _exchange_wait("gather_layer1_wait", gathering1, y0)))
    y1, backward1, landed1 = layer(1, gathered1, small[1], y0)
    loss_splat, dy = _loss_call(y1, target)
    loss = lax.psum(loss_splat[0, 0], ("x", "y", "c"))
    exchanging1, d_y0, g_small1 = backward1(dy)
    exchanging0, g_x, g_small0 = backward0(d_y0)

    state = lambda n, s: (view(w, n, s), view(m, n, s), view(v, n, s))
    parts1 = landed1(exchanging1, g_x)
    updated = {n: _adamw_layer(f"adamw_l1_{n}", parts1[i], *state(n, s), 1) for i, (n, s) in enumerate(SHARDED)}
    g_small = (g_small0, g_small1)
    sizes = [k for _ in range(DEPTH) for _, k in SMALL]
    g_vec = _pack_vectors([g_small[l][n] for l in range(DEPTH) for n, _ in SMALL])
    (parts_vec,) = _all_to_all([jnp.broadcast_to(g_vec[None], (N_DEV,) + g_vec.shape)], updated["w_in"][0])
    parts0 = landed0(exchanging0, parts_vec)

    vec = lambda t: _pack_vectors([t[n][l] for l in range(DEPTH) for n, _ in SMALL])[None]
    outs = {}
    for i, (n, s) in enumerate(SHARDED):
        res = _adamw_layer(f"adamw_l0_{n}", parts0[i], *state(n, s), 0, earlier=updated[n])
        outs[n] = [o.reshape(w[n].shape) for o in res]
    vec_outs = [_unpack_vectors(o, sizes) for o in _adamw_layer("adamw_vectors", parts_vec, vec(w), vec(m), vec(v), 0)]
    for i, (n, _) in enumerate(SMALL):
        outs[n] = [jnp.stack([o[l * len(SMALL) + i] for l in range(DEPTH)]) for o in vec_outs]
    return (loss, g_x[None], *[outs[n][k] for k in range(4) for n in WEIGHTS])
```

```python
import functools
import math

import jax
import jax.numpy as jnp
from jax import lax
from jax.experimental import pallas as pl
from jax.experimental.pallas import tpu as pltpu

F32 = jnp.float32
BF16 = jnp.bfloat16
HI = lax.Precision.HIGHEST
MESH = pl.DeviceIdType.MESH

N_DEV = 8
D_MODEL = 1024
DEPTH = 2
RMS_EPS = 1e-6
ROPE_THETA = 10000.0
LANES = 128
MLA_HEADS = 4
MLA_ROPE = 64
MLA_QK = 192
MLA_Q_RANK = 384
MLA_KV_RANK = 256
DN_HEADS = 4
DN_CHUNK = 64
DN_CONV = 4
DIL_HEADS = 12
DIL_GROUP_HEADS = 4
DIL_DILATIONS = (1, 4, 16)
DIL_BLOCK = 128
BRANCH_W = 512
IN_WIDTH = 11464
NEG = -1e30
VMEM_LIMIT = 56 * 1024 * 1024

ADAM_LR, ADAM_B1, ADAM_B2, ADAM_EPS, ADAM_WD, ADAM_STEP = 0.001, 0.9, 0.999, 1e-08, 0.01, 10

_SEG = {}
_off = 0
for _n, _w in (("q_lat", 384), ("c_kv", 256), ("k_pe", 64), ("z_a", 512), ("dn_qkv", 1536), ("dn_a", 4), ("dn_b", 4),
               ("z_b", 512), ("dil_qkv", 4608), ("z_c", 512), ("gate", 3072)):
    _SEG[_n] = (_off, _w)
    _off += _w
assert _off == IN_WIDTH


def _pcall(body, **kw):
    return pl.pallas_call(body, **kw)


def _params(sem=None):
    return pltpu.CompilerParams(dimension_semantics=sem, vmem_limit_bytes=VMEM_LIMIT)


def _tile(n, target, mult):
    t = (min(n, target) // mult) * mult
    while t >= mult:
        if n % t == 0:
            return t
        t -= mult
    return n


def _mm(name, a, b, mode, out_dtype=F32, acc=None, tm=1024, tn=512, tk=1024):
    if mode == "nn":
        (M, K), (_, N) = a.shape, b.shape
    elif mode == "nt":
        (M, K), (N, _) = a.shape, b.shape
    else:
        (K, M), (_, N) = a.shape, b.shape
    tm, tn, tk = _tile(M, tm, LANES), _tile(N, tn, LANES), _tile(K, tk, LANES)
    nk = K // tk
    dims = {"nn": (((1,), (0,)), ((), ())), "nt": (((1,), (1,)), ((), ())), "tn": (((0,), (0,)), ((), ()))}[mode]
    a_spec = pl.BlockSpec((tk, tm), lambda i, j, k: (k, i)) if mode == "tn" else pl.BlockSpec((tm, tk), lambda i, j, k: (i, k))
    b_spec = pl.BlockSpec((tn, tk), lambda i, j, k: (j, k)) if mode == "nt" else pl.BlockSpec((tk, tn), lambda i, j, k: (k, j))
    o_spec = pl.BlockSpec((tm, tn), lambda i, j, k: (i, j))
    has_acc = acc is not None

    def body(*refs):
        a_ref, b_ref = refs[:2]
        c_ref = refs[2] if has_acc else None
        o_ref = refs[3] if has_acc else refs[2]
        prod = lax.dot_general(a_ref[...].astype(BF16), b_ref[...].astype(BF16), dims, preferred_element_type=F32)
        if nk == 1:
            o_ref[...] = (prod + c_ref[...].astype(F32) if has_acc else prod).astype(out_dtype)
            return
        acc_ref = refs[-1]
        k = pl.program_id(2)

        @pl.when(k == 0)
        def _():
            acc_ref[...] = prod + c_ref[...].astype(F32) if has_acc else prod

        @pl.when(k > 0)
        def _():
            acc_ref[...] += prod

        @pl.when(k == nk - 1)
        def _():
            o_ref[...] = acc_ref[...].astype(out_dtype)

    ins = [a, b] + ([acc] if has_acc else [])
    in_specs = [a_spec, b_spec] + ([o_spec] if has_acc else [])
    return _pcall(body, name=name, grid=(M // tm, N // tn, nk), in_specs=in_specs, out_specs=o_spec,
                  out_shape=jax.ShapeDtypeStruct((M, N), out_dtype), scratch_shapes=[pltpu.VMEM((tm, tn), F32)] if nk > 1 else [],
                  compiler_params=_params(("parallel", "parallel", "arbitrary")))(*ins)


def _mm_nt_sum(name, pairs, acc, out_dtype, tm=512, tn=1024):
    M, N = pairs[0][0].shape[0], pairs[0][1].shape[0]
    tm, tn = _tile(M, tm, LANES), _tile(N, tn, LANES)
    n = len(pairs)
    has_acc = acc is not None
    o_spec = pl.BlockSpec((tm, tn), lambda i, j: (i, j))

    def body(*refs):
        total = refs[2 * n][...].astype(F32) if has_acc else None
        for a_ref, b_ref in zip(refs[:n], refs[n:2 * n]):
            prod = lax.dot_general(a_ref[...].astype(BF16), b_ref[...].astype(BF16), (((1,), (1,)), ((), ())), preferred_element_type=F32)
            total = prod if total is None else total + prod
        refs[-1][...] = total.astype(out_dtype)

    in_specs = ([pl.BlockSpec((tm, a.shape[1]), lambda i, j: (i, 0)) for a, _ in pairs]
                + [pl.BlockSpec((tn, b.shape[1]), lambda i, j: (j, 0)) for _, b in pairs] + ([o_spec] if has_acc else []))
    return _pcall(body, name=name, grid=(M // tm, N // tn), in_specs=in_specs, out_specs=o_spec,
                  out_shape=jax.ShapeDtypeStruct((M, N), out_dtype), compiler_params=_params(("parallel", "parallel")))(
        *[a for a, _ in pairs], *[b for _, b in pairs], *([acc] if has_acc else []))


def _mm_nn_many(name, a, bs, out_dtype, tm=1024):
    M, K = a.shape
    tm = _tile(M, tm, LANES)
    n = len(bs)

    def body(*refs):
        a_t = refs[0][...].astype(BF16)
        for b_ref, o_ref in zip(refs[1:1 + n], refs[1 + n:]):
            o_ref[...] = jnp.dot(a_t, b_ref[...].astype(BF16), preferred_element_type=F32).astype(out_dtype)

    return _pcall(body, name=name, grid=(M // tm,),
                  in_specs=[pl.BlockSpec((tm, K), lambda i: (i, 0))] + [pl.BlockSpec(b.shape, lambda i: (0, 0)) for b in bs],
                  out_specs=[pl.BlockSpec((tm, b.shape[1]), lambda i: (i, 0)) for b in bs],
                  out_shape=[jax.ShapeDtypeStruct((M, b.shape[1]), out_dtype) for b in bs],
                  compiler_params=_params(("parallel",)))(a, *bs)


def _mm_tn_many(name, a, bs, out_dtype, tk=1024):
    K, M = a.shape
    tk = _tile(K, tk, LANES)
    nk, n = K // tk, len(bs)

    def body(*refs):
        a_t = refs[0][...].astype(BF16)
        k = pl.program_id(0)
        for b_ref, o_ref, acc in zip(refs[1:1 + n], refs[1 + n:1 + 2 * n], refs[1 + 2 * n:]):
            prod = lax.dot_general(a_t, b_ref[...].astype(BF16), (((0,), (0,)), ((), ())), preferred_element_type=F32)

            @pl.when(k == 0)
            def _(acc=acc, prod=prod):
                acc[...] = prod

            @pl.when(k > 0)
            def _(acc=acc, prod=prod):
                acc[...] += prod

            @pl.when(k == nk - 1)
            def _(acc=acc, o_ref=o_ref):
                o_ref[...] = acc[...].astype(out_dtype)

    return _pcall(body, name=name, grid=(nk,),
                  in_specs=[pl.BlockSpec((tk, M), lambda k: (k, 0))] + [pl.BlockSpec((tk, b.shape[1]), lambda k: (k, 0)) for b in bs],
                  out_specs=[pl.BlockSpec((M, b.shape[1]), lambda k: (0, 0)) for b in bs],
                  out_shape=[jax.ShapeDtypeStruct((M, b.shape[1]), out_dtype) for b in bs],
                  scratch_shapes=[pltpu.VMEM((M, b.shape[1]), F32) for b in bs],
                  compiler_params=_params(("arbitrary",)))(a, *bs)


INPROJ_DH_GROUPS = ((0, 1, 2, 3, 5, 6, 8), (4,), (9,), (7,))
INPROJ_NARROW = INPROJ_DH_GROUPS[0]

INPROJ_DTYPE = BF16

INPROJ_TILES = {"nn": dict(tm=1024, tn=1536, tk=1024), "tn": dict(tm=1024, tn=768, tk=2048)}


def _make_multi_linear(name, n, out_dtype=F32):
    @jax.custom_vjp
    def op(h, ws):
        return tuple(_mm(f"{name}_fwd{i}", h, w, "nn", out_dtype=out_dtype) for i, w in enumerate(ws))

    def fwd(h, ws):
        return op(h, ws), (h, ws)

    def bwd(res, douts):
        h, ws = res
        dh = None
        for i, (w, d) in enumerate(zip(ws, douts)):
            dh = _mm(f"{name}_dh{i}", d, w, "nt", acc=dh, out_dtype=h.dtype if i == len(ws) - 1 else F32)
        dws = tuple(_mm(f"{name}_dw{i}", h, d, "tn", out_dtype=w.dtype) for i, (w, d) in enumerate(zip(ws, douts)))
        return dh, dws

    op.defvjp(fwd, bwd)
    return op


def _make_resid_linear(name):
    @jax.custom_vjp
    def op(x, a, w):
        return _mm(f"{name}_fwd", a, w, "nn", acc=x)

    def fwd(x, a, w):
        return op(x, a, w), (a, w)

    def bwd(res, dy):
        a, w = res
        return dy, _mm(f"{name}_da", dy, w, "nt", out_dtype=a.dtype), _mm(f"{name}_dw", a, dy, "tn", out_dtype=w.dtype)

    op.defvjp(fwd, bwd)
    return op


def _make_rowwise(name, f, tile):
    def specs(rows, aux, params, consts, t):
        row = [pl.BlockSpec((t, a.shape[1]), lambda i: (i, 0)) for a in (*rows, *aux)]
        full = [pl.BlockSpec(p.shape, lambda i: (0, 0)) for p in (*params, *consts)]
        return row, full

    def fwd_call(rows, aux, params, consts):
        S = rows[0].shape[0]
        t = min(tile, S)
        n_in = len(rows) + len(aux) + len(params) + len(consts)
        shp = lambda a: jax.ShapeDtypeStruct((t, a.shape[1]), a.dtype)
        outs = jax.eval_shape(f, *[shp(a) for a in (*rows, *aux)], *params, *consts)
        row_specs, full_specs = specs(rows, aux, params, consts, t)

        def body(*refs):
            res = f(*[r[...] for r in refs[:n_in]])
            for o_ref, o in zip(refs[n_in:], res):
                o_ref[...] = o

        return _pcall(body, name=f"{name}_fwd", grid=(S // t,), in_specs=row_specs + full_specs,
                      out_specs=[pl.BlockSpec((t, o.shape[1]), lambda i: (i, 0)) for o in outs],
                      out_shape=[jax.ShapeDtypeStruct((S, o.shape[1]), o.dtype) for o in outs],
                      compiler_params=_params(("parallel",)))(*rows, *aux, *params, *consts)

    def bwd_call(rows, aux, params, consts, douts):
        S = rows[0].shape[0]
        t = min(tile, S)
        nr, na, npar, nc, nd = len(rows), len(aux), len(params), len(consts), len(douts)
        row_specs, full_specs = specs(rows, aux, params, consts, t)

        def body(*refs):
            vals = [r[...] for r in refs[:nr + na + npar + nc]]
            rv, av = vals[:nr], vals[nr:nr + na]
            pv, cv = vals[nr + na:nr + na + npar], vals[nr + na + npar:]
            dv = tuple(r[...] for r in refs[nr + na + npar + nc:nr + na + npar + nc + nd])
            out_refs = refs[nr + na + npar + nc + nd:]
            _, vjp = jax.vjp(lambda *rp: f(*rp[:nr], *av, *rp[nr:], *cv), *rv, *pv)
            grads = vjp(dv)
            for o_ref, g in zip(out_refs[:nr], grads[:nr]):
                o_ref[...] = g
            first = pl.program_id(0) == 0
            for o_ref, g in zip(out_refs[nr:], grads[nr:]):
                @pl.when(first)
                def _(o_ref=o_ref):
                    o_ref[...] = jnp.zeros_like(o_ref)
                o_ref[...] += g

        res = _pcall(body, name=f"{name}_bwd", grid=(S // t,),
                     in_specs=row_specs + full_specs + [pl.BlockSpec((t, d.shape[1]), lambda i: (i, 0)) for d in douts],
                     out_specs=[pl.BlockSpec((t, a.shape[1]), lambda i: (i, 0)) for a in rows]
                     + [pl.BlockSpec(p.shape, lambda i: (0, 0)) for p in params],
                     out_shape=[jax.ShapeDtypeStruct(a.shape, a.dtype) for a in (*rows, *params)],
                     compiler_params=_params(("arbitrary",)))(*rows, *aux, *params, *consts, *douts)
        return tuple(res[:nr]), tuple(res[nr:])

    @jax.custom_vjp
    def op(rows, aux, params, consts):
        return tuple(fwd_call(rows, aux, params, consts))

    def fwd(rows, aux, params, consts):
        return op(rows, aux, params, consts), (rows, aux, params, consts)

    def bwd(res, douts):
        rows, aux, params, consts = res
        drows, dparams = bwd_call(rows, aux, params, consts, tuple(douts))
        zeros = lambda xs: tuple(jnp.zeros_like(a) for a in xs)
        return drows, zeros(aux), dparams, zeros(consts)

    op.defvjp(fwd, bwd)
    return op


@jax.custom_vjp
def _swap_halves(x):
    return pltpu.roll(x, LANES // 2, 1)


_swap_halves.defvjp(lambda x: (_swap_halves(x), None), lambda _, g: (_swap_halves(g),))


def _rope(x, cos_t, sin_t):
    return x * cos_t + _swap_halves(x) * sin_t


def _rms(x, g, n=None):
    n = x.shape[-1] if n is None else n
    return x * lax.rsqrt(jnp.sum(x * x, axis=-1, keepdims=True) * (1.0 / n) + RMS_EPS) * g


def _heads(x):
    return [x[:, i * LANES:(i + 1) * LANES] for i in range(x.shape[1] // LANES)]


def _cat(xs):
    return jnp.concatenate(xs, axis=1)


def _silu(x):
    return x * jax.nn.sigmoid(x)


def _f_norm(x, g):
    return (_rms(x, g).astype(BF16),)


def _f_mla_a(q_lat, c_kv, kpe, cos_p, sin_p, qa_g, kva_g, kpe_g):
    q_lat, c_kv, kpe = (t.astype(F32) for t in (q_lat, c_kv, kpe))
    kp =_rope(_rms(kpe, kpe_g, MLA_ROPE), cos_p, sin_p)
    return _rms(q_lat, qa_g).astype(BF16), _rms(c_kv, kva_g).astype(BF16), _cat([kp] * MLA_HEADS)


def _f_mla_b(q8, kn_raw, cos_p, sin_p, qn_g, qp_g, kn_g):
    hs = _heads(q8)
    qn = _cat([_rms(h, qn_g) for h in hs[:MLA_HEADS]])
    qp = _cat([_rope(_rms(h, qp_g, MLA_ROPE), cos_p, sin_p) for h in hs[MLA_HEADS:]])
    kn = _cat([_rms(h, kn_g) for h in _heads(kn_raw)])
    return qn, qp, kn


def _softplus(x):
    return jnp.maximum(x, 0.0) + jnp.log(1.0 + jnp.exp(-jnp.abs(x)))


def _l2n(x):
    return x * lax.rsqrt(jnp.sum(x * x, axis=-1, keepdims=True) + 1e-6)


def _f_dn_pre(mixed, ab, alog_f, dtb_f, e_a, e_b):
    hs = _heads(mixed)
    q = _cat([_l2n(h) * (LANES ** -0.5) for h in hs[:DN_HEADS]])
    k = _cat([_l2n(h) for h in hs[DN_HEADS:2 * DN_HEADS]])
    v = _cat(hs[2 * DN_HEADS:])
    ab = ab.astype(F32)
    a_f = jnp.dot(ab, e_a, precision=HI, preferred_element_type=F32)
    b_f = jnp.dot(ab, e_b, precision=HI, preferred_element_type=F32)
    g = -jnp.exp(alog_f) * _softplus(a_f + dtb_f)
    return q, k, v, g, jax.nn.sigmoid(b_f)


def _f_dil_pre(qkv, cos_h, sin_h, q_g, k_g):
    hs = [h.astype(F32) for h in _heads(qkv)]
    q = [_rope(_rms(h, q_g), cos_h, sin_h) for h in hs[:DIL_HEADS]]
    k = [_rope(_rms(h, k_g), cos_h, sin_h) for h in hs[DIL_HEADS:2 * DIL_HEADS]]
    v = hs[2 * DIL_HEADS:]
    group = lambda xs, g: _cat(xs[g * DIL_GROUP_HEADS:(g + 1) * DIL_GROUP_HEADS])
    return tuple(group(xs, g) for xs in (q, k, v) for g in range(len(DIL_DILATIONS)))


def _f_merge_a(y_a, z_a, o_dn, z_b, o0, o1, o2, l0, l1, l2, z_c, out_g):
    z_a, z_b, z_c = (z.astype(F32) for z in (z_a, z_b, z_c))
    y_b = _cat([_rms(h, out_g) for h in _heads(o_dn)])
    os_, ls = [_heads(o) for o in (o0, o1, o2)], [_heads(l) for l in (l0, l1, l2)]
    y_c = []
    for j in range(DIL_GROUP_HEADS):
        l3 = [ls[g][j] for g in range(3)]
        m = jnp.maximum(jnp.maximum(l3[0], l3[1]), l3[2])
        e3 = [jnp.exp(l - m) for l in l3]
        den = e3[0] + e3[1] + e3[2]
        y_c.append(sum(e3[g] * os_[g][j] for g in range(3)) / den)
    return tuple(t.astype(BF16) for t in (y_a * _silu(z_a), y_b * _silu(z_b), _cat(y_c) * _silu(z_c)))


def _f_merge_b(b0, b1, b2, gl):
    gs = [jax.nn.sigmoid(gl[:, i * D_MODEL:(i + 1) * D_MODEL].astype(F32)) for i in range(3)]
    return ((gs[0] * b0 + gs[1] * b1 + gs[2] * b2).astype(BF16),)


def _rope_tables(pos, inv_sign):
    S = pos.shape[0]
    t = min(S, 1024)

    def body(p_ref, c_ref, cp, sp, ch, sh):
        p = p_ref[...].astype(F32)
        c = c_ref[...]
        ang_p, ang_h = p * c[0:1], p * c[2:3]
        cp[...] = jnp.cos(ang_p) * jnp.abs(c[1:2])
        sp[...] = jnp.sin(ang_p) * c[1:2]
        ch[...] = jnp.cos(ang_h)
        sh[...] = jnp.sin(ang_h) * c[3:4]

    row = pl.BlockSpec((t, LANES), lambda i: (i, 0))
    return _pcall(body, name="rope_tables", grid=(S // t,),
                  in_specs=[pl.BlockSpec((t, 1), lambda i: (i, 0)), pl.BlockSpec((4, LANES), lambda i: (0, 0))],
                  out_specs=[row] * 4, out_shape=[jax.ShapeDtypeStruct((S, LANES), F32)] * 4,
                  compiler_params=_params(("parallel",)))(pos, inv_sign)


def _rope_consts():
    half_p, half_h = MLA_ROPE // 2, LANES // 2
    inv_p = 1.0 / (ROPE_THETA ** (jnp.arange(0, MLA_ROPE, 2, dtype=F32) / MLA_ROPE))
    inv_h = 1.0 / (ROPE_THETA ** (jnp.arange(0, LANES, 2, dtype=F32) / LANES))
    z = jnp.zeros((half_p,), F32)
    o = jnp.ones((half_p,), F32)
    return jnp.stack([jnp.concatenate([inv_p, z, inv_p, z]), jnp.concatenate([-o, z, o, z]),
                      jnp.concatenate([inv_h, inv_h]), jnp.concatenate([-jnp.ones((half_h,), F32), jnp.ones((half_h,), F32)])])


def _shift_rows(x, s, up):
    n = x.shape[0]
    r = lax.broadcasted_iota(jnp.int32, x.shape, 0)
    if up:
        return jnp.where(r < n - s, pltpu.roll(x, n - s, 0), 0.0)
    return jnp.where(r >= s, pltpu.roll(x, s, 0), 0.0)


def _make_shift(s):
    @jax.custom_vjp
    def sh(x):
        return _shift_rows(x, s, False)

    sh.defvjp(lambda x: (sh(x), None), lambda _, g: (_shift_rows(g, s, True),))
    return sh


def _f_conv(x, w):
    x = x.astype(F32)
    y = x * w[DN_CONV - 1:DN_CONV]
    for j in range(DN_CONV - 1):
        y = y + _make_shift(DN_CONV - 1 - j)(x) * w[j:j + 1]
    return _silu(y)


def _make_conv(name):
    def call(x, w, dy=None):
        S, C = x.shape
        col = pl.BlockSpec((S, LANES), lambda i: (0, i))
        wsp = pl.BlockSpec((DN_CONV, LANES), lambda i: (0, i))
        if dy is None:
            def body(x_ref, w_ref, o_ref):
                o_ref[...] = _f_conv(x_ref[...], w_ref[...])
            return _pcall(body, name=f"{name}_fwd", grid=(C // LANES,), in_specs=[col, wsp], out_specs=col,
                          out_shape=jax.ShapeDtypeStruct(x.shape, F32), compiler_params=_params(("parallel",)))(x, w)

        def body(x_ref, w_ref, dy_ref, dx_ref, dw_ref):
            _, vjp = jax.vjp(_f_conv, x_ref[...], w_ref[...])
            dx_ref[...], dw_ref[...] = vjp(dy_ref[...])
        return _pcall(body, name=f"{name}_bwd", grid=(C // LANES,), in_specs=[col, wsp, col], out_specs=[col, wsp],
                      out_shape=[jax.ShapeDtypeStruct(x.shape, x.dtype), jax.ShapeDtypeStruct(w.shape, F32)],
                      compiler_params=_params(("parallel",)))(x, w, dy)

    @jax.custom_vjp
    def op(x, w):
        return call(x, w)

    op.defvjp(lambda x, w: (op(x, w), (x, w)), lambda res, dy: tuple(call(*res, dy)))
    return op


def _dot_nt(a, b):
    return lax.dot_general(a.astype(BF16), b.astype(BF16), (((1,), (1,)), ((), ())), preferred_element_type=F32)


def _dot_nn(a, b):
    return jnp.dot(a.astype(BF16), b.astype(BF16), preferred_element_type=F32)


def _dot_tn(a, b):
    return lax.dot_general(a.astype(BF16), b.astype(BF16), (((0,), (0,)), ((), ())), preferred_element_type=F32)


def _mla_scores(qn, qp, kn, kp, diagonal):
    scale = MLA_QK ** -0.5
    s = _dot_nt(qn * scale, kn) + _dot_nt(qp * scale, kp)
    if diagonal:
        r = lax.broadcasted_iota(jnp.int32, s.shape, 0)
        c = lax.broadcasted_iota(jnp.int32, s.shape, 1)
        s = jnp.where(c <= r, s, NEG)
    return s


def _on_causal_pairs(qi, ki, step):
    @pl.when(ki < qi)
    def _():
        step(False)

    @pl.when(ki == qi)
    def _():
        step(True)


def _causal_pairs(n, t, by_key, heads):
    pairs = [(q, k) for k in range(n) for q in range(k, n)] if by_key else [(q, k) for q in range(n) for k in range(q + 1)]
    qt, kt = (jnp.array([p[i] for p in pairs], jnp.int32) for i in (0, 1))
    return (qt, kt, pl.BlockSpec((t, heads * LANES), lambda h, p, qt_r, kt_r: (qt_r[p], h)),
            pl.BlockSpec((t, heads * LANES), lambda h, p, qt_r, kt_r: (kt_r[p], h)))


MLA_FWD_HEADS = 4
MLA_BWD_HEADS = 2


def _make_mla_attn(name):
    scale = MLA_QK ** -0.5

    def fwd_call(qn, qp, kn, kp, v):
        S = qn.shape[0]
        t = min(S, 512)
        n = S // t
        hp = MLA_FWD_HEADS
        qt, kt, qs, ks = _causal_pairs(n, t, False, hp)
        cols = [slice(j * LANES, (j + 1) * LANES) for j in range(hp)]

        def body(qt_r, kt_r, qn_r, qp_r, kn_r, kp_r, v_r, o_r, lse_r, m_s, l_s, acc_s):
            qi, ki = qt_r[pl.program_id(1)], kt_r[pl.program_id(1)]

            @pl.when(ki == 0)
            def _():
                m_s[...] = jnp.full_like(m_s, NEG)
                l_s[...] = jnp.zeros_like(l_s)
                acc_s[...] = jnp.zeros_like(acc_s)

            def step(diagonal):
                for c in cols:
                    s = _mla_scores(qn_r[:, c], qp_r[:, c], kn_r[:, c], kp_r[:, c], diagonal)
                    m_old = m_s[:, c]
                    m_new = jnp.maximum(m_old, jnp.max(s, axis=-1, keepdims=True))
                    p = jnp.exp(s - m_new[:, :1])
                    alpha = jnp.exp(m_old - m_new)
                    l_s[:, c] = alpha * l_s[:, c] + jnp.sum(p, axis=-1, keepdims=True)
                    acc_s[:, c] = alpha * acc_s[:, c] + _dot_nn(p, v_r[:, c])
                    m_s[:, c] = m_new
            _on_causal_pairs(qi, ki, step)

            @pl.when(ki == qi)
            def _():
                o_r[...] = acc_s[...] / l_s[...]
                lse_r[...] = m_s[...] + jnp.log(l_s[...])

        spec = pltpu.PrefetchScalarGridSpec(num_scalar_prefetch=2, grid=(MLA_HEADS // hp, qt.shape[0]), in_specs=[qs, qs, ks, ks, ks],
                                            out_specs=[qs, qs], scratch_shapes=[pltpu.VMEM((t, hp * LANES), F32)] * 3)
        return _pcall(body, name=f"{name}_fwd", grid_spec=spec, out_shape=[jax.ShapeDtypeStruct((S, MLA_HEADS * LANES), F32)] * 2,
                      compiler_params=_params(("parallel", "arbitrary")))(qt, kt, qn, qp, kn, kp, v)

    def bwd_call(qn, qp, kn, kp, v, o, lse, do):
        S = qn.shape[0]
        t = min(S, 512)
        n = S // t
        hp = MLA_BWD_HEADS
        qt, kt, qs, ks = _causal_pairs(n, t, True, hp)
        head = pl.BlockSpec((S, hp * LANES), lambda h, p, qt_r, kt_r: (0, h))
        n_pairs = qt.shape[0]
        cols = [slice(j * LANES, (j + 1) * LANES) for j in range(hp)]

        def body(qt_r, kt_r, qn_r, qp_r, kn_r, kp_r, v_r, o_r, lse_r, do_r, dqn_r, dqp_r, dkn_r, dkp_r, dv_r, dkn_s, dkp_s, dv_s, dl_s):
            pair = pl.program_id(1)
            qi, ki = qt_r[pair], kt_r[pair]
            rows = pl.ds(pl.multiple_of(qi * t, t), t)

            @pl.when(pair == 0)
            def _():
                dqn_r[...] = jnp.zeros_like(dqn_r)
                dqp_r[...] = jnp.zeros_like(dqp_r)

            @pl.when(ki == 0)
            def _():
                for c in cols:
                    dl_s[rows, c] = jnp.broadcast_to(jnp.sum(do_r[:, c] * o_r[:, c], axis=-1, keepdims=True), (t, LANES))

            @pl.when(qi == ki)
            def _():
                dkn_s[...] = jnp.zeros_like(dkn_s)
                dkp_s[...] = jnp.zeros_like(dkp_s)
                dv_s[...] = jnp.zeros_like(dv_s)

            def step(diagonal):
                for c in cols:
                    qn, qp, kn, kp, do = qn_r[:, c], qp_r[:, c], kn_r[:, c], kp_r[:, c], do_r[:, c]
                    p = jnp.exp(_mla_scores(qn, qp, kn, kp, diagonal) - lse_r[:, c][:, :1])
                    ds = p * (_dot_nt(do, v_r[:, c]) - dl_s[rows, c][:, :1])
                    dv_s[:, c] += _dot_tn(p, do)
                    dkn_s[:, c] += _dot_tn(ds, qn * scale)
                    dkp_s[:, c] += _dot_tn(ds, qp * scale)
                    dqn_r[rows, c] += _dot_nn(ds, kn)
                    dqp_r[rows, c] += _dot_nn(ds, kp)
            _on_causal_pairs(qi, ki, step)

            @pl.when(qi == n - 1)
            def _():
                dkn_r[...] = dkn_s[...]
                dkp_r[...] = dkp_s[...]
                dv_r[...] = dv_s[...]

            @pl.when(pair == n_pairs - 1)
            def _():
                dqn_r[...] = dqn_r[...] * scale
                dqp_r[...] = dqp_r[...] * scale

        spec = pltpu.PrefetchScalarGridSpec(num_scalar_prefetch=2, grid=(MLA_HEADS // hp, n_pairs), in_specs=[qs, qs, ks, ks, ks, qs, qs, qs],
                                            out_specs=[head, head, ks, ks, ks],
                                            scratch_shapes=[pltpu.VMEM((t, hp * LANES), F32)] * 3 + [pltpu.VMEM((S, hp * LANES), F32)])
        return _pcall(body, name=f"{name}_bwd", grid_spec=spec, out_shape=[jax.ShapeDtypeStruct((S, MLA_HEADS * LANES), F32)] * 5,
                      compiler_params=_params(("parallel", "arbitrary")))(qt, kt, qn, qp, kn, kp, v, o, lse, do)

    @jax.custom_vjp
    def op(qn, qp, kn, kp, v):
        return fwd_call(qn, qp, kn, kp, v)[0]

    def fwd(qn, qp, kn, kp, v):
        o, lse = fwd_call(qn, qp, kn, kp, v)
        return o, (qn, qp, kn, kp, v, o, lse)

    def bwd(res, do):
        return tuple(bwd_call(*res, do))

    op.defvjp(fwd, bwd)
    return op


def _dil_block(q, kp, kc, vp, vc, has_prev):
    scale = LANES ** -0.5
    r = lax.broadcasted_iota(jnp.int32, (DIL_BLOCK, 2 * DIL_BLOCK), 0)
    c = lax.broadcasted_iota(jnp.int32, (DIL_BLOCK, 2 * DIL_BLOCK), 1)
    valid = ((c < DIL_BLOCK) & (c >= r) & has_prev) | ((c >= DIL_BLOCK) & (c - DIL_BLOCK <= r))
    s = jnp.where(valid, _dot_nt(q * scale, jnp.concatenate([kp, kc], axis=0)), NEG)
    m = jnp.max(s, axis=-1, keepdims=True)
    e = jnp.exp(s - m)
    den = jnp.sum(e, axis=-1, keepdims=True)
    o = _dot_nn(e, jnp.concatenate([vp, vc], axis=0)) / den
    return o, jnp.broadcast_to(m + jnp.log(den), o.shape)


DIL_TILE_ROWS = (1024, 1024, 2048)
DIL_STEP_HEADS = 2


def _make_dil_attn(name, d, tile_rows):
    def call(q, k, v, cts=None):
        S = q.shape[0]
        span = DIL_BLOCK * d
        G = max(1, min(tile_rows, S) // span)
        n = S // (G * span)
        at = (lambda i: i) if cts is None else (lambda i: n - 1 - i)
        step_heads = DIL_STEP_HEADS if d == 1 else 1
        width = step_heads * LANES
        tile = pl.BlockSpec((G * span, width), lambda h, i: (at(i), h))
        before = pl.BlockSpec((span, width), lambda h, i: (jnp.maximum(at(i) * G - 1, 0), h))
        heads = [slice(h * LANES, (h + 1) * LANES) for h in range(step_heads)]

        def rows(r, j):
            return pl.ds(j * DIL_BLOCK, DIL_BLOCK) if d == 1 else pl.ds(r + j * span, DIL_BLOCK, stride=d)

        def over_residues(fn):
            if d == 1:
                fn(0)
            else:
                lax.fori_loop(0, d, lambda r, c: (fn(r), c)[1], 0)

        def block_inputs(r, j, c, q_r, kb_r, k_r, vb_r, v_r):
            kp = kb_r[rows(r, 0), c] if j == 0 else k_r[rows(r, j - 1), c]
            vp = vb_r[rows(r, 0), c] if j == 0 else v_r[rows(r, j - 1), c]
            return q_r[rows(r, j), c], kp, k_r[rows(r, j), c], vp, v_r[rows(r, j), c]

        if cts is None:
            def body(q_r, kb_r, k_r, vb_r, v_r, o_r, lse_r):
                first = at(pl.program_id(1)) * G

                def residue(r):
                    for j in range(G):
                        for c in heads:
                            o_r[rows(r, j), c], lse_r[rows(r, j), c] = _dil_block(
                                *block_inputs(r, j, c, q_r, kb_r, k_r, vb_r, v_r), first + j > 0)
                over_residues(residue)
            return _pcall(body, name=f"{name}_fwd", grid=(DIL_GROUP_HEADS // step_heads, n), in_specs=[tile, before, tile, before, tile],
                          out_specs=[tile, tile], out_shape=[jax.ShapeDtypeStruct(q.shape, F32)] * 2,
                          compiler_params=_params(("parallel", "parallel")))(q, k, k, v, v)

        def body(q_r, kb_r, k_r, vb_r, v_r, do_r, dl_r, dq_r, dk_r, dv_r, ck_s, cv_s):
            first = at(pl.program_id(1)) * G

            @pl.when(pl.program_id(1) == 0)
            def _():
                ck_s[...] = jnp.zeros_like(ck_s)
                cv_s[...] = jnp.zeros_like(cv_s)

            def residue(r):
                for c in heads:
                    owed = None
                    for j in range(G):
                        hp = first + j > 0
                        _, vjp = jax.vjp(lambda *a: _dil_block(*a, hp), *block_inputs(r, j, c, q_r, kb_r, k_r, vb_r, v_r))
                        dq, dkp, dkc, dvp, dvc = vjp((do_r[rows(r, j), c], dl_r[rows(r, j), c]))
                        dq_r[rows(r, j), c] = dq
                        if j == G - 1:
                            dkc, dvc = dkc + ck_s[rows(r, 0), c], dvc + cv_s[rows(r, 0), c]
                        dk_r[rows(r, j), c], dv_r[rows(r, j), c] = dkc, dvc
                        if j == 0:
                            owed = (dkp, dvp)
                        else:
                            dk_r[rows(r, j - 1), c] += dkp
                            dv_r[rows(r, j - 1), c] += dvp
                    ck_s[rows(r, 0), c], cv_s[rows(r, 0), c] = owed
            over_residues(residue)
        return _pcall(body, name=f"{name}_bwd", grid=(DIL_GROUP_HEADS // step_heads, n), in_specs=[tile, before, tile, before, tile, tile, tile],
                      out_specs=[tile] * 3, out_shape=[jax.ShapeDtypeStruct(q.shape, F32)] * 3,
                      scratch_shapes=[pltpu.VMEM((span, width), F32)] * 2,
                      compiler_params=_params(("parallel", "arbitrary")))(q, k, k, v, v, *cts)

    @jax.custom_vjp
    def op(q, k, v):
        return tuple(call(q, k, v))

    op.defvjp(lambda q, k, v: (op(q, k, v), (q, k, v)), lambda res, cts: tuple(call(*res, cts=cts)))
    return op


def _pdot(a, b, dims):
    return lax.dot_general(a, b, (dims, ((), ())), precision=lax.Precision.HIGH, preferred_element_type=F32)


DN_LOCAL_CHUNKS = 4
DN_SCAN_CHUNKS = 8


DN_BLOCK_HEADS = 4
DN_BLOCK = DN_BLOCK_HEADS * DN_CHUNK


def _inverse_cotangent(inv, d):
    return -_pdot(inv, _pdot(d, inv, ((1,), (1,))), ((0,), (0,)))


@jax.custom_vjp
def _unit_lower_inverse(a):
    n = a.shape[0]
    eye = (lax.broadcasted_iota(jnp.int32, (n, n), 0) == lax.broadcasted_iota(jnp.int32, (n, n), 1)).astype(F32)
    inv, pw = eye - a, a
    for level in range(5):
        dot = _pdot if level < 2 else (lambda x, y, dims: lax.dot_general(x, y, (dims, ((), ())), preferred_element_type=F32))
        pw = dot(pw, pw, ((1,), (0,)))
        inv = inv + dot(inv, pw, ((1,), (0,)))
    return inv


def _unit_lower_inverse_fwd(a):
    inv = _unit_lower_inverse(a)
    return inv, inv


_unit_lower_inverse.defvjp(_unit_lower_inverse_fwd, lambda inv, d: (_inverse_cotangent(inv, d),))


@jax.custom_vjp
def _known_inverse(a, inv):
    return inv


_known_inverse.defvjp(lambda a, inv: (inv, inv), lambda inv, d: (_inverse_cotangent(inv, d), jnp.zeros_like(inv)))


def _dn_local(q, k, v, g, b, known=None):
    C, R = DN_CHUNK, DN_BLOCK
    r = lax.broadcasted_iota(jnp.int32, (R, R), 0)
    c = lax.broadcasted_iota(jnp.int32, (R, R), 1)
    same_head = (r // C) == (c // C)
    incl, strict = same_head & (r >= c), same_head & (r > c)
    avg = jnp.full((R, LANES), 1.0 / LANES, F32)
    rc = lax.broadcasted_iota(jnp.int32, (C, C), 0) >= lax.broadcasted_iota(jnp.int32, (C, C), 1)
    gc_lanes = _pdot(rc.astype(F32), g, ((1,), (0,)))
    us, ws, qes, kds, qks, invs = [], [], [], [], [], []
    for first in range(0, DN_HEADS, DN_BLOCK_HEADS):
        stack = lambda x: jnp.concatenate(_heads(x)[first:first + DN_BLOCK_HEADS], axis=0)
        unstack = lambda x: [x[p * C:(p + 1) * C] for p in range(DN_BLOCK_HEADS)]
        gc, q_s, k_s, v_s, b_s = (stack(x) for x in (gc_lanes, q, k, v, b))
        gc_j = _pdot(avg, gc, ((1,), (1,)))
        decay = jnp.exp(jnp.where(incl, _cat([gc] * (R // LANES)) - gc_j, NEG))
        kb = k_s * b_s
        kk = _pdot(jnp.concatenate([kb, q_s], axis=0), k_s, ((1,), (1,)))
        a = jnp.where(strict, kk[:R] * decay, 0.0)
        inv = _unit_lower_inverse(a) if known is None else _known_inverse(a, known[len(invs) * R:(len(invs) + 1) * R])
        invs.append(inv)
        eg = jnp.exp(gc)
        uw = _pdot(inv, _cat([v_s * b_s, kb * eg]), ((1,), (0,)))
        g_last = jnp.concatenate([jnp.broadcast_to(x[C - 1:C], (C, LANES)) for x in unstack(gc)], axis=0)
        us += unstack(uw[:, :LANES])
        ws += unstack(uw[:, LANES:])
        qes += unstack(q_s * eg)
        kds += unstack(k_s * jnp.exp(g_last - gc))
        qks.append(kk[R:] * decay)
    egl = jnp.broadcast_to(jnp.exp(gc_lanes[C - 1:C]), (8, DN_HEADS * LANES))
    return _cat(us), _cat(ws), _cat(qes), _cat(kds), jnp.concatenate(qks, axis=0), egl, jnp.concatenate(invs, axis=0)


def _dn_scan(u, w, qe, kd, qk, egl, state):
    C = DN_CHUNK
    heads = [slice(h * LANES, (h + 1) * LANES) for h in range(DN_HEADS)]
    ws = [_pdot(jnp.concatenate([w[:, sl], qe[:, sl]], axis=0), state[sl, :], ((1,), (0,))) for sl in heads]
    v_new = [u[:, sl] - x[:C] for sl, x in zip(heads, ws)]
    local = []
    for i, first in enumerate(range(0, DN_HEADS, DN_BLOCK_HEADS)):
        y = _pdot(qk[i * DN_BLOCK:(i + 1) * DN_BLOCK], jnp.concatenate(v_new[first:first + DN_BLOCK_HEADS], axis=0), ((1,), (0,)))
        local += [y[p * C:(p + 1) * C] for p in range(DN_BLOCK_HEADS)]
    o = _cat([x[C:] + y for x, y in zip(ws, local)])
    states = [state[sl, :] * egl[0:1, sl] + _pdot(kd[:, sl], vn, ((0,), (0,))) for sl, vn in zip(heads, v_new)]
    return o, jnp.concatenate(states, axis=0)


def _make_delta_rule(name):
    W = DN_HEADS * LANES
    QK = DN_HEADS * DN_CHUNK

    def local_call(ins, cts=None):
        S = ins[0].shape[0]
        n = S // DN_CHUNK
        per = math.gcd(DN_LOCAL_CHUNKS, n)
        row = pl.BlockSpec((per * DN_CHUNK, W), lambda i: (i, 0))
        qkb = pl.BlockSpec((per * QK, DN_BLOCK), lambda i: (i, 0))
        eg = pl.BlockSpec((per, 8, W), lambda i: (i, 0, 0))
        rows = lambda j: slice(j * DN_CHUNK, (j + 1) * DN_CHUNK)
        qk_rows = lambda j: slice(j * QK, (j + 1) * QK)
        out_rows = [rows, rows, rows, rows, qk_rows]

        if cts is None:
            def body(*refs):
                for j in range(per):
                    res = _dn_local(*[r[rows(j), :] for r in refs[:5]])
                    for o_r, o, at_ in zip(refs[5:10], res[:5], out_rows):
                        o_r[at_(j), :] = o
                    refs[10][j] = res[5]
                    refs[11][qk_rows(j), :] = res[6]
            blockdiag = jax.ShapeDtypeStruct((n * QK, DN_BLOCK), F32)
            return _pcall(body, name=f"{name}_local_fwd", grid=(n // per,), in_specs=[row] * 5, out_specs=[row] * 4 + [qkb, eg, qkb],
                          out_shape=[jax.ShapeDtypeStruct((S, W), F32)] * 4 + [blockdiag, jax.ShapeDtypeStruct((n, 8, W), F32), blockdiag],
                          compiler_params=_params(("parallel",)))(*ins)

        def body(*refs):
            for j in range(per):
                known = refs[5][qk_rows(j), :]
                _, vjp = jax.vjp(lambda *a: _dn_local(*a, known=known)[:6], *[r[rows(j), :] for r in refs[:5]])
                grads = vjp(tuple(r[at_(j), :] for r, at_ in zip(refs[6:11], out_rows)) + (refs[11][j],))
                for o_r, o in zip(refs[12:], grads):
                    o_r[rows(j), :] = o
        return _pcall(body, name=f"{name}_local_bwd", grid=(n // per,), in_specs=[row] * 5 + [qkb] + [row] * 4 + [qkb, eg],
                      out_specs=[row] * 5, out_shape=[jax.ShapeDtypeStruct((S, W), F32)] * 5,
                      compiler_params=_params(("parallel",)))(*ins, *cts)

    def scan_call(ins, saved=None, do=None):
        S = ins[0].shape[0]
        n = S // DN_CHUNK
        per = math.gcd(DN_SCAN_CHUNKS, n)
        steps = n // per
        at = (lambda i: i) if do is None else (lambda i: steps - 1 - i)
        row = pl.BlockSpec((per * DN_CHUNK, W), lambda i: (at(i), 0))
        qkb = pl.BlockSpec((per * QK, DN_BLOCK), lambda i: (at(i), 0))
        eg = pl.BlockSpec((per, 8, W), lambda i: (at(i), 0, 0))
        st = pl.BlockSpec((per, W, LANES), lambda i: (at(i), 0, 0))
        rows = lambda j: slice(j * DN_CHUNK, (j + 1) * DN_CHUNK)
        qk_rows = lambda j: slice(j * QK, (j + 1) * QK)
        chunk = lambda refs, j: [r[rows(j), :] for r in refs[:4]] + [refs[4][qk_rows(j), :], refs[5][j]]

        if do is None:
            def body(*refs):
                o_r, st_r, s_s = refs[6:]

                @pl.when(pl.program_id(0) == 0)
                def _():
                    s_s[...] = jnp.zeros_like(s_s)
                state = s_s[...]
                for j in range(per):
                    st_r[j] = state
                    o_r[rows(j), :], state = _dn_scan(*chunk(refs, j), state)
                s_s[...] = state
            return _pcall(body, name=f"{name}_scan_fwd", grid=(steps,), in_specs=[row] * 4 + [qkb, eg], out_specs=[row, st],
                          out_shape=[jax.ShapeDtypeStruct((S, W), F32), jax.ShapeDtypeStruct((n, W, LANES), F32)],
                          scratch_shapes=[pltpu.VMEM((W, LANES), F32)], compiler_params=_params(("arbitrary",)))(*ins)

        def body(*refs):
            st_r, do_r = refs[6:8]
            outs, ds_s = refs[8:14], refs[14]

            @pl.when(pl.program_id(0) == 0)
            def _():
                ds_s[...] = jnp.zeros_like(ds_s)
            ds = ds_s[...]
            for j in reversed(range(per)):
                _, vjp = jax.vjp(_dn_scan, *chunk(refs, j), st_r[j])
                *grads, ds = vjp((do_r[rows(j), :], ds))
                for o_r, gval in zip(outs[:4], grads[:4]):
                    o_r[rows(j), :] = gval
                outs[4][qk_rows(j), :] = grads[4]
                outs[5][j] = grads[5]
            ds_s[...] = ds
        return _pcall(body, name=f"{name}_scan_bwd", grid=(steps,), in_specs=[row] * 4 + [qkb, eg, st, row], out_specs=[row] * 4 + [qkb, eg],
                      out_shape=[jax.ShapeDtypeStruct((S, W), F32)] * 4
                      + [jax.ShapeDtypeStruct((n * QK, DN_BLOCK), F32), jax.ShapeDtypeStruct((n, 8, W), F32)],
                      scratch_shapes=[pltpu.VMEM((W, LANES), F32)], compiler_params=_params(("arbitrary",)))(*ins, saved, do)

    @jax.custom_vjp
    def local(q, k, v, g, b):
        return tuple(local_call((q, k, v, g, b))[:6])

    def local_fwd(*a):
        *outs, inverses = local_call(a)
        return tuple(outs), (*a, inverses)

    local.defvjp(local_fwd, lambda res, cts: tuple(local_call(res, tuple(cts))))

    @jax.custom_vjp
    def scan(u, w, qe, kd, qk, egl):
        return scan_call((u, w, qe, kd, qk, egl))[0]

    def scan_fwd(*a):
        o, states = scan_call(a)
        return o, (a, states)

    scan.defvjp(scan_fwd, lambda res, do: tuple(scan_call(res[0], res[1], do)))
    return lambda q, k, v, g, b: scan(*local(q, k, v, g, b))


def _loss_call(y, target):
    S, D = y.shape
    t = min(S, 512)
    n = S // t
    row = pl.BlockSpec((t, D), lambda i: (i, 0))

    def body(y_r, t_r, loss_r, dy_r, acc_s):
        i = pl.program_id(0)

        @pl.when(i == 0)
        def _():
            acc_s[...] = jnp.zeros_like(acc_s)
        err = y_r[...] - t_r[...]
        dy_r[...] = err * (1.0 / D)
        acc_s[...] += jnp.sum(err * err, axis=0, keepdims=True)

        @pl.when(i == n - 1)
        def _():
            loss_r[...] = jnp.broadcast_to(jnp.sum(acc_s[...], axis=1, keepdims=True) * (0.5 / D), loss_r.shape)

    return _pcall(body, name="loss_head", grid=(n,), in_specs=[row, row],
                  out_specs=[pl.BlockSpec((8, LANES), lambda i: (0, 0)), row],
                  out_shape=[jax.ShapeDtypeStruct((8, LANES), F32), jax.ShapeDtypeStruct((S, D), F32)],
                  scratch_shapes=[pltpu.VMEM((1, D), F32)], compiler_params=_params(("arbitrary",)))(y, target)


def _adamw_layer(name, part, w, m, v, layer, earlier=None, rows=128):
    L, R, C = w.shape
    t = _tile(R, rows, 8)
    row = pl.BlockSpec((None, t, C), lambda i: (layer, i, 0))

    def body(p_r, w_r, m_r, v_r, *rest):
        g_r, d_r, nm_r, nv_r = rest[-4:]
        g = p_r[0].astype(F32)
        for s in range(1, N_DEV):
            g = g + p_r[s].astype(F32)
        m_new = ADAM_B1 * m_r[...] + (1.0 - ADAM_B1) * g
        v_new = ADAM_B2 * v_r[...] + (1.0 - ADAM_B2) * (g * g)
        m_hat = m_new / (1.0 - ADAM_B1 ** ADAM_STEP)
        v_hat = v_new / (1.0 - ADAM_B2 ** ADAM_STEP)
        g_r[...] = g
        d_r[...] = -ADAM_LR * (m_hat / (jnp.sqrt(v_hat) + ADAM_EPS) + ADAM_WD * w_r[...])
        nm_r[...] = m_new
        nv_r[...] = v_new

    extra = [] if earlier is None else list(earlier)
    return _pcall(body, name=name, grid=(R // t,),
                  in_specs=[pl.BlockSpec((N_DEV, t, C), lambda i: (0, i, 0)), row, row, row] + [pl.BlockSpec(memory_space=pl.ANY)] * len(extra),
                  out_specs=[row] * 4, out_shape=[jax.ShapeDtypeStruct((L, R, C), F32)] * 4,
                  input_output_aliases={4 + k: k for k in range(len(extra))},
                  compiler_params=_params(("parallel",)))(part, w, m, v, *extra)


def _my_place():
    x, y, c = lax.axis_index("x"), lax.axis_index("y"), lax.axis_index("c")
    return x, y, c


def _index(x, y, c):
    return 4 * x + 2 * y + c


def _all_gather(vs):
    n = len(vs)

    def body(*refs):
        v_refs, out_refs = refs[:n], refs[n:2 * n]
        send_sems, recv_sems, local_sems = refs[2 * n:]
        x, y, c = _my_place()
        me, sibling = (x, y, c), (x, y, 1 - c)
        chips = [(1 - x, y), (x, 1 - y), (1 - x, 1 - y)]

        def copy(a, k, block, to, src=None):
            rows = out_refs[a].at[_index(*block)]
            return pltpu.make_async_remote_copy(src_ref=rows if src is None else src, dst_ref=rows, send_sem=send_sems.at[a, k],
                                                recv_sem=recv_sems.at[a, k], device_id=to, device_id_type=MESH)

        mine = [pltpu.make_async_copy(v_refs[a], out_refs[a].at[_index(*me)], local_sems.at[a]) for a in range(n)]
        first, passed = [], []
        for a in range(n):
            mine[a].start()
            first += [copy(a, 0, me, sibling, src=v_refs[a])]
            first += [copy(a, 1 + j, me, (*chip, c), src=v_refs[a]) for j, chip in enumerate(chips)]
        for cp in first:
            cp.start()
        for j, chip in enumerate(chips):
            for a in range(n):
                copy(a, 1 + j, (*chip, c), me).wait_recv()
                passed.append(copy(a, 4 + j, (*chip, c), sibling))
                passed[-1].start()
        for a in range(n):
            copy(a, 0, sibling, me).wait_recv()
            for j, chip in enumerate(chips):
                copy(a, 4 + j, (*chip, 1 - c), me).wait_recv()
        for cp in first + passed:
            cp.wait_send()
        for a in range(n):
            mine[a].wait()

    any_ = pl.BlockSpec(memory_space=pl.ANY)
    return _pcall(body, name="gather_weights", in_specs=[any_] * n, out_specs=[any_] * n,
                  out_shape=[jax.ShapeDtypeStruct((N_DEV,) + v.shape, v.dtype) for v in vs],
                  scratch_shapes=[pltpu.SemaphoreType.DMA((n, 7)), pltpu.SemaphoreType.DMA((n, 7)), pltpu.SemaphoreType.DMA((n,))])(*vs)


def _all_to_all(vs, after):
    n = len(vs)

    def body(*refs):
        v_refs, out_refs = refs[:n], refs[n + 1:2 * n + 1]
        send_sems, recv_sems, local_sems = refs[2 * n + 1:]
        x, y, c = _my_place()
        me = _index(x, y, c)
        mine = [pltpu.make_async_copy(v_refs[a].at[me], out_refs[a].at[me], local_sems.at[a]) for a in range(n)]
        copies = []
        for a in range(n):
            mine[a].start()
        for k in range(1, N_DEV):
            px = 1 - x if k & 4 else x
            py = 1 - y if k & 2 else y
            pc = 1 - c if k & 1 else c
            for a in range(n):
                cp = pltpu.make_async_remote_copy(src_ref=v_refs[a].at[_index(px, py, pc)], dst_ref=out_refs[a].at[me],
                                                  send_sem=send_sems.at[a, k - 1], recv_sem=recv_sems.at[a, k - 1],
                                                  device_id=(px, py, pc), device_id_type=MESH)
                cp.start()
                copies.append(cp)
        for cp in copies:
            cp.wait()
        for a in range(n):
            mine[a].wait()

    any_ = pl.BlockSpec(memory_space=pl.ANY)
    return _pcall(body, name="exchange_vectors", in_specs=[any_] * (n + 1), out_specs=[any_] * n,
                  out_shape=[jax.ShapeDtypeStruct(v.shape, v.dtype) for v in vs],
                  scratch_shapes=[pltpu.SemaphoreType.DMA((n, 7)), pltpu.SemaphoreType.DMA((n, 7)), pltpu.SemaphoreType.DMA((n,))])(*vs, after)


_HBM = pl.BlockSpec(memory_space=pltpu.HBM)
_SEM = pl.BlockSpec(memory_space=pltpu.SEMAPHORE)
_EFFECT = pltpu.SideEffectType.DATAFLOW_SIDE_EFFECTING


def _direct_copies(gather, v_refs, land_refs, send_sems, recv_sems, local_sems):
    x, y, c = _my_place()
    me = _index(x, y, c)
    local, remote = [], []
    for a, (v_ref, land_ref) in enumerate(zip(v_refs, land_refs)):
        local.append(pltpu.make_async_copy(v_ref if gather else v_ref.at[me], land_ref.at[me], local_sems.at[a]))
    for k in range(1, N_DEV):
        px = 1 - x if k & 4 else x
        py = 1 - y if k & 2 else y
        pc = 1 - c if k & 1 else c
        for a, (v_ref, land_ref) in enumerate(zip(v_refs, land_refs)):
            sem = a * (N_DEV - 1) + k - 1
            remote.append(pltpu.make_async_remote_copy(
                src_ref=v_ref if gather else v_ref.at[_index(px, py, pc)], dst_ref=land_ref.at[me], send_sem=send_sems.at[sem],
                recv_sem=recv_sems.at[sem], device_id=(px, py, pc), device_id_type=MESH))
    return local, remote


def _exchange_start(name, vs, gather, thru):
    n = len(vs)
    lands = [lax.empty((N_DEV,) + v.shape if gather else v.shape, v.dtype) for v in vs]

    def body(*refs):
        v_refs, land_refs = refs[:n], refs[n:2 * n]
        send_sems, recv_sems, local_sems = refs[2 * n + 1:2 * n + 4]
        local, remote = _direct_copies(gather, v_refs, land_refs, send_sems, recv_sems, local_sems)
        for cp in local + remote:
            cp.start()

    hbm = lambda a: pltpu.HBM(a.shape, a.dtype)
    res = _pcall(body, name=name,
                 out_shape=(pltpu.SemaphoreType.DMA((n * (N_DEV - 1),)), pltpu.SemaphoreType.DMA((n * (N_DEV - 1),)), pltpu.SemaphoreType.DMA((n,)),
                            *[hbm(a) for a in (*vs, *lands, thru)]),
                 in_specs=[_HBM] * (2 * n + 1), out_specs=(_SEM, _SEM, _SEM, *[_HBM] * (2 * n + 1)),
                 input_output_aliases={i: 3 + i for i in range(2 * n + 1)},
                 compiler_params=pltpu.CompilerParams(has_side_effects=_EFFECT))(
        *[pltpu.with_memory_space_constraint(a, pltpu.HBM) for a in (*vs, *lands, thru)])
    return (gather, res[:3], res[3:3 + n], res[3 + n:3 + 2 * n]), res[3 + 2 * n]


def _exchange_wait(name, started, after):
    gather, sems, vs, lands = started
    n = len(vs)

    def body(*refs):
        v_refs, land_refs = refs[:n], refs[n:2 * n]
        send_sems, recv_sems, local_sems = refs[2 * n:2 * n + 3]
        local, remote = _direct_copies(gather, v_refs, land_refs, send_sems, recv_sems, local_sems)
        for cp in local:
            cp.wait()
        for cp in remote:
            cp.wait_send()
            cp.wait_recv()

    hbm = lambda a: pltpu.HBM(a.shape, a.dtype)
    res = _pcall(body, name=name, out_shape=tuple(hbm(a) for a in (*vs, *lands)),
                 in_specs=[_HBM] * (2 * n) + [_SEM] * 3 + [pl.BlockSpec(memory_space=pl.ANY)], out_specs=tuple([_HBM] * (2 * n)),
                 input_output_aliases={i: i for i in range(2 * n)},
                 compiler_params=pltpu.CompilerParams(has_side_effects=_EFFECT))(*vs, *lands, *sems, after)
    return list(res[n:])


W_IN_SHARD = IN_WIDTH // N_DEV
SEG_ORDER = ("q_lat", "c_kv", "k_pe", "z_a", "dn_qkv", "dn_ab", "z_b", "dil_qkv", "z_c", "gate")
SEG_WIDTH = (384, 256, LANES, 512, 1536, LANES, 512, 4608, 512, 3072)


def _w_in_plan():
    plan = []

    def add(seg, c0, c1, dst):
        while c0 < c1:
            d = c0 // W_IN_SHARD
            e = min(c1, (d + 1) * W_IN_SHARD)
            plan.append((seg, dst, d, c0 - d * W_IN_SHARD, e - c0))
            dst += e - c0
            c0 = e

    half = MLA_ROPE // 2
    for i, name in enumerate(SEG_ORDER):
        if name == "k_pe":
            o = _SEG["k_pe"][0]
            add(i, o, o + half, 0)
            add(i, o + half, o + 2 * half, LANES // 2)
        elif name == "dn_ab":
            o = _SEG["dn_a"][0]
            add(i, o, o + 2 * DN_HEADS, 0)
        else:
            o, w = _SEG[name]
            add(i, o, o + w, 0)
    return plan


def _make_w_in_segments(name):
    plan = _w_in_plan()
    nseg = len(SEG_ORDER)
    t = 256

    def fwd_call(g):
        L = g.shape[1]

        def body(g_ref, *o_refs):
            for i in (SEG_ORDER.index("k_pe"), SEG_ORDER.index("dn_ab")):
                o_refs[i][...] = jnp.zeros_like(o_refs[i])
            for seg, dst, d, src, n in plan:
                o_refs[seg][:, dst:dst + n] = g_ref[d, :, src:src + n]

        return _pcall(body, name=f"{name}_fwd", grid=(L, D_MODEL // t),
                      in_specs=[pl.BlockSpec((N_DEV, None, t, W_IN_SHARD), lambda l, i: (0, l, i, 0))],
                      out_specs=[pl.BlockSpec((None, t, w), lambda l, i: (l, i, 0)) for w in SEG_WIDTH],
                      out_shape=[jax.ShapeDtypeStruct((L, D_MODEL, w), g.dtype) for w in SEG_WIDTH],
                      compiler_params=_params(("parallel", "parallel")))(g)

    def bwd_call(ds):
        L = ds[0].shape[0]

        def body(*refs):
            d_refs, g_ref = refs[:nseg], refs[nseg]
            for seg, dst, d, src, n in plan:
                g_ref[d, :, src:src + n] = d_refs[seg][:, dst:dst + n]

        return _pcall(body, name=f"{name}_bwd", grid=(L, D_MODEL // t),
                      in_specs=[pl.BlockSpec((None, t, w), lambda l, i: (l, i, 0)) for w in SEG_WIDTH],
                      out_specs=pl.BlockSpec((N_DEV, None, t, W_IN_SHARD), lambda l, i: (0, l, i, 0)),
                      out_shape=jax.ShapeDtypeStruct((N_DEV, L, D_MODEL, W_IN_SHARD), ds[0].dtype),
                      compiler_params=_params(("parallel", "parallel")))(*ds)

    @jax.custom_vjp
    def op(g):
        return tuple(fwd_call(g))

    op.defvjp(lambda g: (op(g), None), lambda _, ds: (bwd_call(tuple(ds)),))
    return op


def _pe_pad(a):
    h = MLA_ROPE // 2
    z = jnp.zeros(a.shape[:-1] + (h,), a.dtype)
    return jnp.concatenate([a[..., :h], z, a[..., h:], z], axis=-1)


def _layer_norm(tag, x, norm_g):
    return _make_rowwise(f"{tag}_norm", _f_norm, 512)((x,), (), (norm_g[None, :],), ())[0]


def _layer(tag, x, tables, W):
    h = _layer_norm(tag, x, W["norm_g"])
    return _layer_tail(tag, x, _make_multi_linear(f"{tag}_inproj", 10, INPROJ_DTYPE)(h, W["w_in_segments"]), tables, W)


def _layer_tail(tag, x, segments, tables, W):
    cos_p, sin_p, cos_h, sin_h = tables
    row = lambda a: a[None, :]
    q_lat, c_kv, kpe, z_a, dn_qkv, ab, z_b, dil_qkv, z_c, gl = segments

    qn_lat, ckvn, kp = _make_rowwise(f"{tag}_mla_a", _f_mla_a, 512)(
        (q_lat, c_kv, kpe), (cos_p, sin_p),
        (row(W["mla_q_a_norm_g"]), row(W["mla_kv_a_norm_g"]), row(_pe_pad(W["mla_k_norm_g"][LANES:]))), ())
    wq = W["mla_w_q_b"].reshape(MLA_Q_RANK, MLA_HEADS, MLA_QK)
    wq = jnp.concatenate([wq[:, :, :LANES].reshape(MLA_Q_RANK, -1), _pe_pad(wq[:, :, LANES:]).reshape(MLA_Q_RANK, -1)], axis=1)
    wkv = W["mla_w_kv_b"].reshape(MLA_KV_RANK, MLA_HEADS, 2 * LANES)
    (q8,) = _make_multi_linear(f"{tag}_qb", 1)(qn_lat, (wq,))
    kn_raw, v_mla = _make_multi_linear(f"{tag}_kvb", 2)(
        ckvn, (wkv[:, :, :LANES].reshape(MLA_KV_RANK, -1), wkv[:, :, LANES:].reshape(MLA_KV_RANK, -1)))
    qn, qp, kn = _make_rowwise(f"{tag}_mla_b", _f_mla_b, 512)(
        (q8, kn_raw), (cos_p, sin_p),
        (row(W["mla_q_norm_g"][:LANES]), row(_pe_pad(W["mla_q_norm_g"][LANES:])), row(W["mla_k_norm_g"][:LANES])), ())
    y_a = _make_mla_attn(f"{tag}_mla")(qn, qp, kn, kp, v_mla)

    mixed = _make_conv(f"{tag}_conv")(dn_qkv, W["dn_conv_w"])
    lane_head = jnp.arange(DN_HEADS * LANES) // LANES
    e_a = (jnp.arange(LANES)[:, None] == lane_head[None, :]).astype(F32)
    e_b = (jnp.arange(LANES)[:, None] == lane_head[None, :] + DN_HEADS).astype(F32)
    q_dn, k_dn, v_dn, g_dn, b_dn = _make_rowwise(f"{tag}_dn_pre", _f_dn_pre, 512)(
        (mixed, ab), (), (row(jnp.repeat(W["dn_a_log"], LANES)), row(jnp.repeat(W["dn_dt_bias"], LANES))), (e_a, e_b))
    o_dn = _make_delta_rule(f"{tag}_dn")(q_dn, k_dn, v_dn, g_dn, b_dn)

    qkv_dil = _make_rowwise(f"{tag}_dil_pre", _f_dil_pre, 256)(
        (dil_qkv,), (cos_h, sin_h), (row(W["dil_q_norm_g"]), row(W["dil_k_norm_g"])), ())
    n_groups = len(DIL_DILATIONS)
    o_lse = [_make_dil_attn(f"{tag}_dil{g}", d, DIL_TILE_ROWS[g])(qkv_dil[g], qkv_dil[n_groups + g], qkv_dil[2 * n_groups + g])
             for g, d in enumerate(DIL_DILATIONS)]

    ya, yb, yc = _make_rowwise(f"{tag}_merge_a", _f_merge_a, 256)(
        (y_a, z_a, o_dn, z_b, *[o for o, _ in o_lse], *[l for _, l in o_lse], z_c), (), (row(W["dn_out_norm_g"]),), ())
    (b0,) = _make_multi_linear(f"{tag}_br0", 1)(ya, (W["w_branch"][0],))
    (b1,) = _make_multi_linear(f"{tag}_br1", 1)(yb, (W["w_branch"][1],))
    (b2,) = _make_multi_linear(f"{tag}_br2", 1)(yc, (W["w_branch"][2],))
    (mix,) = _make_rowwise(f"{tag}_merge_b", _f_merge_b, 256)((b0, b1, b2, gl), (), (), ())
    return _make_resid_linear(f"{tag}_out")(x, mix, W["w_out"])


SHARDED = (("w_in", (D_MODEL, W_IN_SHARD)), ("mla_w_q_b", (MLA_Q_RANK, MLA_HEADS * MLA_QK // N_DEV)),
           ("mla_w_kv_b", (MLA_KV_RANK, MLA_HEADS * 2 * LANES // N_DEV)), ("w_branch", (3 * BRANCH_W, D_MODEL // N_DEV)),
           ("w_out", (D_MODEL // N_DEV, D_MODEL)), ("dn_conv_w", (DN_CONV, 3 * DN_HEADS * LANES // N_DEV)))
SMALL = (("norm_g", D_MODEL), ("mla_q_a_norm_g", MLA_Q_RANK), ("mla_kv_a_norm_g", MLA_KV_RANK), ("mla_q_norm_g", MLA_QK),
         ("mla_k_norm_g", MLA_QK), ("dn_a_log", DN_HEADS), ("dn_dt_bias", DN_HEADS), ("dn_out_norm_g", LANES),
         ("dil_q_norm_g", LANES), ("dil_k_norm_g", LANES))
WEIGHTS = ("norm_g", "w_in", "mla_q_a_norm_g", "mla_w_q_b", "mla_kv_a_norm_g", "mla_w_kv_b", "mla_q_norm_g", "mla_k_norm_g",
           "dn_conv_w", "dn_a_log", "dn_dt_bias", "dn_out_norm_g", "dil_q_norm_g", "dil_k_norm_g", "w_branch", "w_out")


def _round_up(n, m):
    return -(-n // m) * m


def _pack_vectors(pieces):
    return jnp.concatenate([jnp.pad(p, (0, _round_up(p.shape[0], LANES) - p.shape[0])) for p in pieces]).reshape(-1, LANES)


def _unpack_vectors(flat, sizes):
    out, off = [], 0
    flat = flat.reshape(-1)
    for n in sizes:
        out.append(flat[off:off + n])
        off += _round_up(n, LANES)
    return out


def _whole_weights(g, small):
    W = dict(small)
    W["mla_w_q_b"] = g["mla_w_q_b"].transpose(1, 0, 2).reshape(MLA_Q_RANK, -1)
    W["mla_w_kv_b"] = g["mla_w_kv_b"].transpose(1, 0, 2).reshape(MLA_KV_RANK, -1)
    W["w_branch"] = g["w_branch"].reshape(N_DEV, 3, BRANCH_W, -1).transpose(1, 2, 0, 3).reshape(3, BRANCH_W, D_MODEL)
    W["w_out"] = g["w_out"].reshape(D_MODEL, D_MODEL)
    W["dn_conv_w"] = g["dn_conv_w"].transpose(1, 0, 2).reshape(DN_CONV, -1)
    return W


def kernel(x, positions, norm_g, w_in, mla_q_a_norm_g, mla_w_q_b, mla_kv_a_norm_g, mla_w_kv_b, mla_q_norm_g, mla_k_norm_g, dn_conv_w, dn_a_log, dn_dt_bias, dn_out_norm_g, dil_q_norm_g, dil_k_norm_g, w_branch, w_out, loss_target, m_norm_g, m_w_in, m_mla_q_a_norm_g, m_mla_w_q_b, m_mla_kv_a_norm_g, m_mla_w_kv_b, m_mla_q_norm_g, m_mla_k_norm_g, m_dn_conv_w, m_dn_a_log, m_dn_dt_bias, m_dn_out_norm_g, m_dil_q_norm_g, m_dil_k_norm_g, m_w_branch, m_w_out, v_norm_g, v_w_in, v_mla_q_a_norm_g, v_mla_w_q_b, v_mla_kv_a_norm_g, v_mla_w_kv_b, v_mla_q_norm_g, v_mla_k_norm_g, v_dn_conv_w, v_dn_a_log, v_dn_dt_bias, v_dn_out_norm_g, v_dil_q_norm_g, v_dil_k_norm_g, v_w_branch, v_w_out):
    w = dict(norm_g=norm_g, w_in=w_in, mla_q_a_norm_g=mla_q_a_norm_g, mla_w_q_b=mla_w_q_b, mla_kv_a_norm_g=mla_kv_a_norm_g,
             mla_w_kv_b=mla_w_kv_b, mla_q_norm_g=mla_q_norm_g, mla_k_norm_g=mla_k_norm_g, dn_conv_w=dn_conv_w, dn_a_log=dn_a_log,
             dn_dt_bias=dn_dt_bias, dn_out_norm_g=dn_out_norm_g, dil_q_norm_g=dil_q_norm_g, dil_k_norm_g=dil_k_norm_g,
             w_branch=w_branch, w_out=w_out)
    m = dict(norm_g=m_norm_g, w_in=m_w_in, mla_q_a_norm_g=m_mla_q_a_norm_g, mla_w_q_b=m_mla_w_q_b, mla_kv_a_norm_g=m_mla_kv_a_norm_g,
             mla_w_kv_b=m_mla_w_kv_b, mla_q_norm_g=m_mla_q_norm_g, mla_k_norm_g=m_mla_k_norm_g, dn_conv_w=m_dn_conv_w,
             dn_a_log=m_dn_a_log, dn_dt_bias=m_dn_dt_bias, dn_out_norm_g=m_dn_out_norm_g, dil_q_norm_g=m_dil_q_norm_g,
             dil_k_norm_g=m_dil_k_norm_g, w_branch=m_w_branch, w_out=m_w_out)
    v = dict(norm_g=v_norm_g, w_in=v_w_in, mla_q_a_norm_g=v_mla_q_a_norm_g, mla_w_q_b=v_mla_w_q_b, mla_kv_a_norm_g=v_mla_kv_a_norm_g,
             mla_w_kv_b=v_mla_w_kv_b, mla_q_norm_g=v_mla_q_norm_g, mla_k_norm_g=v_mla_k_norm_g, dn_conv_w=v_dn_conv_w,
             dn_a_log=v_dn_a_log, dn_dt_bias=v_dn_dt_bias, dn_out_norm_g=v_dn_out_norm_g, dil_q_norm_g=v_dil_q_norm_g,
             dil_k_norm_g=v_dil_k_norm_g, w_branch=v_w_branch, w_out=v_w_out)
    x2, target = x[0], loss_target[0]
    pos = positions[0][:, None]

    names = [n for n, _ in SHARDED]
    view = lambda t, n, s: t[n].reshape((DEPTH,) + s)
    shards = [[(view(w, n, s) if n == "dn_conv_w" else view(w, n, s).astype(BF16))[l] for n, s in SHARDED] for l in range(DEPTH)]
    small = [{n: w[n][l] for n, _ in SMALL} for l in range(DEPTH)]
    (w_in0,) = _all_gather(shards[0][:1])
    gathering0, w_in0 = _exchange_start("gather_layer0_others_start", shards[0][1:], True, w_in0)
    gathering1, w_in0 = _exchange_start("gather_layer1_start", shards[1], True, w_in0)
    tables = _rope_tables(pos, _rope_consts())

    def layer(l, g, small_l, x_l, pending=None):
        tag = f"l{l}"
        seg_op = _make_w_in_segments(f"{tag}_w_in_segments")
        w_segs, vjp_segs = jax.vjp(lambda gw: tuple(s[0] for s in seg_op(gw[:, None])), g["w_in"])
        h, vjp_norm = jax.vjp(lambda x_, ng: _layer_norm(tag, x_, ng), x_l, small_l["norm_g"])
        wide = [i for i in range(len(w_segs)) if i not in INPROJ_NARROW]
        segs = dict(zip(INPROJ_NARROW, _mm_nn_many(f"{tag}_inproj_fwd_narrow", h, [w_segs[i] for i in INPROJ_NARROW], INPROJ_DTYPE)))
        segs.update({i: _mm(f"{tag}_inproj_fwd{i}", h, w_segs[i], "nn", out_dtype=INPROJ_DTYPE, **INPROJ_TILES["nn"]) for i in wide})
        segs = tuple(segs[i] for i in range(len(w_segs)))
        if pending is not None:
            g = dict(g, **dict(zip(names[1:], _exchange_wait(f"gather_{tag}_others_wait", pending, segs[7]))))
        rest_g = {n: a for n, a in g.items() if n != "w_in"}
        rest_s = {n: a for n, a in small_l.items() if n != "norm_g"}
        y, vjp_tail = jax.vjp(lambda sg, x_, gg, ss: _layer_tail(tag, x_, sg, tables, _whole_weights(gg, ss)), segs, x_l, rest_g, rest_s)

        def backward(dy):
            dsegs, dx_skip, d_rest_g, d_rest_s = vjp_tail(dy)
            others, big = _exchange_start(f"exchange_{tag}_others_start", [d_rest_g[n] for n in names if n != "w_in"], False, dsegs[7])
            dsegs = tuple(dsegs[:7]) + (big,) + tuple(dsegs[8:])
            dws = dict(zip(INPROJ_NARROW, _mm_tn_many(f"{tag}_inproj_dw_narrow", h, [dsegs[i] for i in INPROJ_NARROW], w_segs[0].dtype)))
            dws.update({i: _mm(f"{tag}_inproj_dw{i}", h, dsegs[i], "tn", out_dtype=w_segs[i].dtype, **INPROJ_TILES["tn"]) for i in wide})
            dws = tuple(dws[i] for i in range(len(w_segs)))
            projection, first =_exchange_start(f"exchange_{tag}_w_in_start", [vjp_segs(dws)[0]], False, dsegs[0])
            dsegs = (first,) + tuple(dsegs[1:])
            dh = None
            for i, group in enumerate(INPROJ_DH_GROUPS):
                dh = _mm_nt_sum(f"{tag}_inproj_dh{i}", [(dsegs[s], w_segs[s]) for s in group], dh,
                                h.dtype if i == len(INPROJ_DH_GROUPS) - 1 else F32)
            dx_norm, d_norm_g = vjp_norm(dh)
            return (projection, others), dx_skip + dx_norm, dict(d_rest_s, norm_g=d_norm_g)

        def landed(exchanging, after):
            projection, others = (_exchange_wait(f"exchange_{tag}_{k}_wait", e, after) for k, e in zip(("w_in", "others"), exchanging))
            return projection + others
        return y, backward, landed

    y0, backward0, landed0 = layer(0, {"w_in": w_in0}, small[0], x2, gathering0)
    gathered1 = dict(zip(names, _exchange_wait("gather_layer1_wait", gathering1, y0)))
    y1, backward1, landed1 = layer(1, gathered1, small[1], y0)
    loss_splat, dy = _loss_call(y1, target)
    loss = lax.psum(loss_splat[0, 0], ("x", "y", "c"))
    exchanging1, d_y0, g_small1 = backward1(dy)
    exchanging0, g_x, g_small0 = backward0(d_y0)

    state = lambda n, s: (view(w, n, s), view(m, n, s), view(v, n, s))
    parts1 = landed1(exchanging1, g_x)
    updated = {n: _adamw_layer(f"adamw_l1_{n}", parts1[i], *state(n, s), 1) for i, (n, s) in enumerate(SHARDED)}
    g_small = (g_small0, g_small1)
    sizes = [k for _ in range(DEPTH) for _, k in SMALL]
    g_vec = _pack_vectors([g_small[l][n] for l in range(DEPTH) for n, _ in SMALL])
    (parts_vec,) = _all_to_all([jnp.broadcast_to(g_vec[None], (N_DEV,) + g_vec.shape)], updated["w_in"][0])
    parts0 = landed0(exchanging0, parts_vec)

    vec = lambda t: _pack_vectors([t[n][l] for l in range(DEPTH) for n, _ in SMALL])[None]
    outs = {}
    for i, (n, s) in enumerate(SHARDED):
        res = _adamw_layer(f"adamw_l0_{n}", parts0[i], *state(n, s), 0, earlier=updated[n])
        outs[n] = [o.reshape(w[n].shape) for o in res]
    vec_outs = [_unpack_vectors(o, sizes) for o in _adamw_layer("adamw_vectors", parts_vec, vec(w), vec(m), vec(v), 0)]
    for i, (n, _) in enumerate(SMALL):
        outs[n] = [jnp.stack([o[l * len(SMALL) + i] for l in range(DEPTH)]) for o in vec_outs]
    return (loss, g_x[None], *[outs[n][k] for k in range(4) for n in WEIGHTS])
```

```python
import functools
import math

import jax
import jax.numpy as jnp
from jax import lax
from jax.experimental import pallas as pl
from jax.experimental.pallas import tpu as pltpu

F32 = jnp.float32
BF16 = jnp.bfloat16
HI = lax.Precision.HIGHEST
MESH = pl.DeviceIdType.MESH

N_DEV = 8
D_MODEL = 1024
DEPTH = 2
RMS_EPS = 1e-6
ROPE_THETA = 10000.0
LANES = 128
MLA_HEADS = 4
MLA_ROPE = 64
MLA_QK = 192
MLA_Q_RANK = 384
MLA_KV_RANK = 256
DN_HEADS = 4
DN_CHUNK = 64
DN_CONV = 4
DIL_HEADS = 12
DIL_GROUP_HEADS = 4
DIL_DILATIONS = (1, 4, 16)
DIL_BLOCK = 128
BRANCH_W = 512
IN_WIDTH = 11464
NEG = -1e30
VMEM_LIMIT = 56 * 1024 * 1024

ADAM_LR, ADAM_B1, ADAM_B2, ADAM_EPS, ADAM_WD, ADAM_STEP = 0.001, 0.9, 0.999, 1e-08, 0.01, 10

_SEG = {}
_off = 0
for _n, _w in (("q_lat", 384), ("c_kv", 256), ("k_pe", 64), ("z_a", 512), ("dn_qkv", 1536), ("dn_a", 4), ("dn_b", 4),
               ("z_b", 512), ("dil_qkv", 4608), ("z_c", 512), ("gate", 3072)):
    _SEG[_n] = (_off, _w)
    _off += _w
assert _off == IN_WIDTH


def _pcall(body, **kw):
    return pl.pallas_call(body, **kw)


def _params(sem=None):
    return pltpu.CompilerParams(dimension_semantics=sem, vmem_limit_bytes=VMEM_LIMIT)


def _tile(n, target, mult):
    t = (min(n, target) // mult) * mult
    while t >= mult:
        if n % t == 0:
            return t
        t -= mult
    return n


def _mm(name, a, b, mode, out_dtype=F32, acc=None, tm=1024, tn=512, tk=1024):
    if mode == "nn":
        (M, K), (_, N) = a.shape, b.shape
    elif mode == "nt":
        (M, K), (N, _) = a.shape, b.shape
    else:
        (K, M), (_, N) = a.shape, b.shape
    tm, tn, tk = _tile(M, tm, LANES), _tile(N, tn, LANES), _tile(K, tk, LANES)
    nk = K // tk
    dims = {"nn": (((1,), (0,)), ((), ())), "nt": (((1,), (1,)), ((), ())), "tn": (((0,), (0,)), ((), ()))}[mode]
    a_spec = pl.BlockSpec((tk, tm), lambda i, j, k: (k, i)) if mode == "tn" else pl.BlockSpec((tm, tk), lambda i, j, k: (i, k))
    b_spec = pl.BlockSpec((tn, tk), lambda i, j, k: (j, k)) if mode == "nt" else pl.BlockSpec((tk, tn), lambda i, j, k: (k, j))
    o_spec = pl.BlockSpec((tm, tn), lambda i, j, k: (i, j))
    has_acc = acc is not None

    def body(*refs):
        a_ref, b_ref = refs[:2]
        c_ref = refs[2] if has_acc else None
        o_ref = refs[3] if has_acc else refs[2]
        prod = lax.dot_general(a_ref[...].astype(BF16), b_ref[...].astype(BF16), dims, preferred_element_type=F32)
        if nk == 1:
            o_ref[...] = (prod + c_ref[...].astype(F32) if has_acc else prod).astype(out_dtype)
            return
        acc_ref = refs[-1]
        k = pl.program_id(2)

        @pl.when(k == 0)
        def _():
            acc_ref[...] = prod + c_ref[...].astype(F32) if has_acc else prod

        @pl.when(k > 0)
        def _():
            acc_ref[...] += prod

        @pl.when(k == nk - 1)
        def _():
            o_ref[...] = acc_ref[...].astype(out_dtype)

    ins = [a, b] + ([acc] if has_acc else [])
    in_specs = [a_spec, b_spec] + ([o_spec] if has_acc else [])
    return _pcall(body, name=name, grid=(M // tm, N // tn, nk), in_specs=in_specs, out_specs=o_spec,
                  out_shape=jax.ShapeDtypeStruct((M, N), out_dtype), scratch_shapes=[pltpu.VMEM((tm, tn), F32)] if nk > 1 else [],
                  compiler_params=_params(("parallel", "parallel", "arbitrary")))(*ins)


def _mm_nt_sum(name, pairs, acc, out_dtype, tm=512, tn=1024):
    M, N = pairs[0][0].shape[0], pairs[0][1].shape[0]
    tm, tn = _tile(M, tm, LANES), _tile(N, tn, LANES)
    n = len(pairs)
    has_acc = acc is not None
    o_spec = pl.BlockSpec((tm, tn), lambda i, j: (i, j))

    def body(*refs):
        total = refs[2 * n][...].astype(F32) if has_acc else None
        for a_ref, b_ref in zip(refs[:n], refs[n:2 * n]):
            prod = lax.dot_general(a_ref[...].astype(BF16), b_ref[...].astype(BF16), (((1,), (1,)), ((), ())), preferred_element_type=F32)
            total = prod if total is None else total + prod
        refs[-1][...] = total.astype(out_dtype)

    in_specs = ([pl.BlockSpec((tm, a.shape[1]), lambda i, j: (i, 0)) for a, _ in pairs]
                + [pl.BlockSpec((tn, b.shape[1]), lambda i, j: (j, 0)) for _, b in pairs] + ([o_spec] if has_acc else []))
    return _pcall(body, name=name, grid=(M // tm, N // tn), in_specs=in_specs, out_specs=o_spec,
                  out_shape=jax.ShapeDtypeStruct((M, N), out_dtype), compiler_params=_params(("parallel", "parallel")))(
        *[a for a, _ in pairs], *[b for _, b in pairs], *([acc] if has_acc else []))


def _mm_nn_many(name, a, bs, out_dtype, tm=1024):
    M, K = a.shape
    tm = _tile(M, tm, LANES)
    n = len(bs)

    def body(*refs):
        a_t = refs[0][...].astype(BF16)
        for b_ref, o_ref in zip(refs[1:1 + n], refs[1 + n:]):
            o_ref[...] = jnp.dot(a_t, b_ref[...].astype(BF16), preferred_element_type=F32).astype(out_dtype)

    return _pcall(body, name=name, grid=(M // tm,),
                  in_specs=[pl.BlockSpec((tm, K), lambda i: (i, 0))] + [pl.BlockSpec(b.shape, lambda i: (0, 0)) for b in bs],
                  out_specs=[pl.BlockSpec((tm, b.shape[1]), lambda i: (i, 0)) for b in bs],
                  out_shape=[jax.ShapeDtypeStruct((M, b.shape[1]), out_dtype) for b in bs],
                  compiler_params=_params(("parallel",)))(a, *bs)


def _mm_tn_many(name, a, bs, out_dtype, tk=1024):
    K, M = a.shape
    tk = _tile(K, tk, LANES)
    nk, n = K // tk, len(bs)

    def body(*refs):
        a_t = refs[0][...].astype(BF16)
        k = pl.program_id(0)
        for b_ref, o_ref, acc in zip(refs[1:1 + n], refs[1 + n:1 + 2 * n], refs[1 + 2 * n:]):
            prod = lax.dot_general(a_t, b_ref[...].astype(BF16), (((0,), (0,)), ((), ())), preferred_element_type=F32)

            @pl.when(k == 0)
            def _(acc=acc, prod=prod):
                acc[...] = prod

            @pl.when(k > 0)
            def _(acc=acc, prod=prod):
                acc[...] += prod

            @pl.when(k == nk - 1)
            def _(acc=acc, o_ref=o_ref):
                o_ref[...] = acc[...].astype(out_dtype)

    return _pcall(body, name=name, grid=(nk,),
                  in_specs=[pl.BlockSpec((tk, M), lambda k: (k, 0))] + [pl.BlockSpec((tk, b.shape[1]), lambda k: (k, 0)) for b in bs],
                  out_specs=[pl.BlockSpec((M, b.shape[1]), lambda k: (0, 0)) for b in bs],
                  out_shape=[jax.ShapeDtypeStruct((M, b.shape[1]), out_dtype) for b in bs],
                  scratch_shapes=[pltpu.VMEM((M, b.shape[1]), F32) for b in bs],
                  compiler_params=_params(("arbitrary",)))(a, *bs)


INPROJ_DH_GROUPS = ((0, 1, 2, 3, 5, 6, 8), (4,), (9,), (7,))
INPROJ_NARROW = INPROJ_DH_GROUPS[0]

INPROJ_DTYPE = BF16

INPROJ_TILES = {"nn": dict(tm=1024, tn=1536, tk=1024), "tn": dict(tm=1024, tn=768, tk=2048)}


def _make_multi_linear(name, n, out_dtype=F32):
    @jax.custom_vjp
    def op(h, ws):
        return tuple(_mm(f"{name}_fwd{i}", h, w, "nn", out_dtype=out_dtype) for i, w in enumerate(ws))

    def fwd(h, ws):
        return op(h, ws), (h, ws)

    def bwd(res, douts):
        h, ws = res
        dh = None
        for i, (w, d) in enumerate(zip(ws, douts)):
            dh = _mm(f"{name}_dh{i}", d, w, "nt", acc=dh, out_dtype=h.dtype if i == len(ws) - 1 else F32)
        dws = tuple(_mm(f"{name}_dw{i}", h, d, "tn", out_dtype=w.dtype) for i, (w, d) in enumerate(zip(ws, douts)))
        return dh, dws

    op.defvjp(fwd, bwd)
    return op


def _make_resid_linear(name):
    @jax.custom_vjp
    def op(x, a, w):
        return _mm(f"{name}_fwd", a, w, "nn", acc=x)

    def fwd(x, a, w):
        return op(x, a, w), (a, w)

    def bwd(res, dy):
        a, w = res
        return dy, _mm(f"{name}_da", dy, w, "nt", out_dtype=a.dtype), _mm(f"{name}_dw", a, dy, "tn", out_dtype=w.dtype)

    op.defvjp(fwd, bwd)
    return op


def _make_rowwise(name, f, tile):
    def specs(rows, aux, params, consts, t):
        row = [pl.BlockSpec((t, a.shape[1]), lambda i: (i, 0)) for a in (*rows, *aux)]
        full = [pl.BlockSpec(p.shape, lambda i: (0, 0)) for p in (*params, *consts)]
        return row, full

    def fwd_call(rows, aux, params, consts):
        S = rows[0].shape[0]
        t = min(tile, S)
        n_in = len(rows) + len(aux) + len(params) + len(consts)
        shp = lambda a: jax.ShapeDtypeStruct((t, a.shape[1]), a.dtype)
        outs = jax.eval_shape(f, *[shp(a) for a in (*rows, *aux)], *params, *consts)
        row_specs, full_specs = specs(rows, aux, params, consts, t)

        def body(*refs):
            res = f(*[r[...] for r in refs[:n_in]])
            for o_ref, o in zip(refs[n_in:], res):
                o_ref[...] = o

        return _pcall(body, name=f"{name}_fwd", grid=(S // t,), in_specs=row_specs + full_specs,
                      out_specs=[pl.BlockSpec((t, o.shape[1]), lambda i: (i, 0)) for o in outs],
                      out_shape=[jax.ShapeDtypeStruct((S, o.shape[1]), o.dtype) for o in outs],
                      compiler_params=_params(("parallel",)))(*rows, *aux, *params, *consts)

    def bwd_call(rows, aux, params, consts, douts):
        S = rows[0].shape[0]
        t = min(tile, S)
        nr, na, npar, nc, nd = len(rows), len(aux), len(params), len(consts), len(douts)
        row_specs, full_specs = specs(rows, aux, params, consts, t)

        def body(*refs):
            vals = [r[...] for r in refs[:nr + na + npar + nc]]
            rv, av = vals[:nr], vals[nr:nr + na]
            pv, cv = vals[nr + na:nr + na + npar], vals[nr + na + npar:]
            dv = tuple(r[...] for r in refs[nr + na + npar + nc:nr + na + npar + nc + nd])
            out_refs = refs[nr + na + npar + nc + nd:]
            _, vjp = jax.vjp(lambda *rp: f(*rp[:nr], *av, *rp[nr:], *cv), *rv, *pv)
            grads = vjp(dv)
            for o_ref, g in zip(out_refs[:nr], grads[:nr]):
                o_ref[...] = g
            first = pl.program_id(0) == 0
            for o_ref, g in zip(out_refs[nr:], grads[nr:]):
                @pl.when(first)
                def _(o_ref=o_ref):
                    o_ref[...] = jnp.zeros_like(o_ref)
                o_ref[...] += g

        res = _pcall(body, name=f"{name}_bwd", grid=(S // t,),
                     in_specs=row_specs + full_specs + [pl.BlockSpec((t, d.shape[1]), lambda i: (i, 0)) for d in douts],
                     out_specs=[pl.BlockSpec((t, a.shape[1]), lambda i: (i, 0)) for a in rows]
                     + [pl.BlockSpec(p.shape, lambda i: (0, 0)) for p in params],
                     out_shape=[jax.ShapeDtypeStruct(a.shape, a.dtype) for a in (*rows, *params)],
                     compiler_params=_params(("arbitrary",)))(*rows, *aux, *params, *consts, *douts)
        return tuple(res[:nr]), tuple(res[nr:])

    @jax.custom_vjp
    def op(rows, aux, params, consts):
        return tuple(fwd_call(rows, aux, params, consts))

    def fwd(rows, aux, params, consts):
        return op(rows, aux, params, consts), (rows, aux, params, consts)

    def bwd(res, douts):
        rows, aux, params, consts = res
        drows, dparams = bwd_call(rows, aux, params, consts, tuple(douts))
        zeros = lambda xs: tuple(jnp.zeros_like(a) for a in xs)
        return drows, zeros(aux), dparams, zeros(consts)

    op.defvjp(fwd, bwd)
    return op


@jax.custom_vjp
def _swap_halves(x):
    return pltpu.roll(x, LANES // 2, 1)


_swap_halves.defvjp(lambda x: (_swap_halves(x), None), lambda _, g: (_swap_halves(g),))


def _rope(x, cos_t, sin_t):
    return x * cos_t + _swap_halves(x) * sin_t


def _rms(x, g, n=None):
    n = x.shape[-1] if n is None else n
    return x * lax.rsqrt(jnp.sum(x * x, axis=-1, keepdims=True) * (1.0 / n) + RMS_EPS) * g


def _heads(x):
    return [x[:, i * LANES:(i + 1) * LANES] for i in range(x.shape[1] // LANES)]


def _cat(xs):
    return jnp.concatenate(xs, axis=1)


def _silu(x):
    return x * jax.nn.sigmoid(x)


def _f_norm(x, g):
    return (_rms(x, g).astype(BF16),)


def _f_mla_a(q_lat, c_kv, kpe, cos_p, sin_p, qa_g, kva_g, kpe_g):
    q_lat, c_kv, kpe = (t.astype(F32) for t in (q_lat, c_kv, kpe))
    kp =_rope(_rms(kpe, kpe_g, MLA_ROPE), cos_p, sin_p)
    return _rms(q_lat, qa_g).astype(BF16), _rms(c_kv, kva_g).astype(BF16), _cat([kp] * MLA_HEADS)


def _f_mla_b(q8, kn_raw, cos_p, sin_p, qn_g, qp_g, kn_g):
    hs = _heads(q8)
    qn = _cat([_rms(h, qn_g) for h in hs[:MLA_HEADS]])
    qp = _cat([_rope(_rms(h, qp_g, MLA_ROPE), cos_p, sin_p) for h in hs[MLA_HEADS:]])
    kn = _cat([_rms(h, kn_g) for h in _heads(kn_raw)])
    return qn, qp, kn


def _softplus(x):
    return jnp.maximum(x, 0.0) + jnp.log(1.0 + jnp.exp(-jnp.abs(x)))


def _l2n(x):
    return x * lax.rsqrt(jnp.sum(x * x, axis=-1, keepdims=True) + 1e-6)


def _f_dn_pre(mixed, ab, alog_f, dtb_f, e_a, e_b):
    hs = _heads(mixed)
    q = _cat([_l2n(h) * (LANES ** -0.5) for h in hs[:DN_HEADS]])
    k = _cat([_l2n(h) for h in hs[DN_HEADS:2 * DN_HEADS]])
    v = _cat(hs[2 * DN_HEADS:])
    ab = ab.astype(F32)
    a_f = jnp.dot(ab, e_a, precision=HI, preferred_element_type=F32)
    b_f = jnp.dot(ab, e_b, precision=HI, preferred_element_type=F32)
    g = -jnp.exp(alog_f) * _softplus(a_f + dtb_f)
    return q, k, v, g, jax.nn.sigmoid(b_f)


def _f_dil_pre(qkv, cos_h, sin_h, q_g, k_g):
    hs = [h.astype(F32) for h in _heads(qkv)]
    q = [_rope(_rms(h, q_g), cos_h, sin_h) for h in hs[:DIL_HEADS]]
    k = [_rope(_rms(h, k_g), cos_h, sin_h) for h in hs[DIL_HEADS:2 * DIL_HEADS]]
    v = hs[2 * DIL_HEADS:]
    group = lambda xs, g: _cat(xs[g * DIL_GROUP_HEADS:(g + 1) * DIL_GROUP_HEADS])
    return tuple(group(xs, g) for xs in (q, k, v) for g in range(len(DIL_DILATIONS)))


def _f_merge_a(y_a, z_a, o_dn, z_b, o0, o1, o2, l0, l1, l2, z_c, out_g):
    z_a, z_b, z_c = (z.astype(F32) for z in (z_a, z_b, z_c))
    y_b = _cat([_rms(h, out_g) for h in _heads(o_dn)])
    os_, ls = [_heads(o) for o in (o0, o1, o2)], [_heads(l) for l in (l0, l1, l2)]
    y_c = []
    for j in range(DIL_GROUP_HEADS):
        l3 = [ls[g][j] for g in range(3)]
        m = jnp.maximum(jnp.maximum(l3[0], l3[1]), l3[2])
        e3 = [jnp.exp(l - m) for l in l3]
        den = e3[0] + e3[1] + e3[2]
        y_c.append(sum(e3[g] * os_[g][j] for g in range(3)) / den)
    return tuple(t.astype(BF16) for t in (y_a * _silu(z_a), y_b * _silu(z_b), _cat(y_c) * _silu(z_c)))


def _f_merge_b(b0, b1, b2, gl):
    gs = [jax.nn.sigmoid(gl[:, i * D_MODEL:(i + 1) * D_MODEL].astype(F32)) for i in range(3)]
    return ((gs[0] * b0 + gs[1] * b1 + gs[2] * b2).astype(BF16),)


def _rope_tables(pos, inv_sign):
    S = pos.shape[0]
    t = min(S, 1024)

    def body(p_ref, c_ref, cp, sp, ch, sh):
        p = p_ref[...].astype(F32)
        c = c_ref[...]
        ang_p, ang_h = p * c[0:1], p * c[2:3]
        cp[...] = jnp.cos(ang_p) * jnp.abs(c[1:2])
        sp[...] = jnp.sin(ang_p) * c[1:2]
        ch[...] = jnp.cos(ang_h)
        sh[...] = jnp.sin(ang_h) * c[3:4]

    row = pl.BlockSpec((t, LANES), lambda i: (i, 0))
    return _pcall(body, name="rope_tables", grid=(S // t,),
                  in_specs=[pl.BlockSpec((t, 1), lambda i: (i, 0)), pl.BlockSpec((4, LANES), lambda i: (0, 0))],
                  out_specs=[row] * 4, out_shape=[jax.ShapeDtypeStruct((S, LANES), F32)] * 4,
                  compiler_params=_params(("parallel",)))(pos, inv_sign)


def _rope_consts():
    half_p, half_h = MLA_ROPE // 2, LANES // 2
    inv_p = 1.0 / (ROPE_THETA ** (jnp.arange(0, MLA_ROPE, 2, dtype=F32) / MLA_ROPE))
    inv_h = 1.0 / (ROPE_THETA ** (jnp.arange(0, LANES, 2, dtype=F32) / LANES))
    z = jnp.zeros((half_p,), F32)
    o = jnp.ones((half_p,), F32)
    return jnp.stack([jnp.concatenate([inv_p, z, inv_p, z]), jnp.concatenate([-o, z, o, z]),
                      jnp.concatenate([inv_h, inv_h]), jnp.concatenate([-jnp.ones((half_h,), F32), jnp.ones((half_h,), F32)])])


def _shift_rows(x, s, up):
    n = x.shape[0]
    r = lax.broadcasted_iota(jnp.int32, x.shape, 0)
    if up:
        return jnp.where(r < n - s, pltpu.roll(x, n - s, 0), 0.0)
    return jnp.where(r >= s, pltpu.roll(x, s, 0), 0.0)


def _make_shift(s):
    @jax.custom_vjp
    def sh(x):
        return _shift_rows(x, s, False)

    sh.defvjp(lambda x: (sh(x), None), lambda _, g: (_shift_rows(g, s, True),))
    return sh


def _f_conv(x, w):
    x = x.astype(F32)
    y = x * w[DN_CONV - 1:DN_CONV]
    for j in range(DN_CONV - 1):
        y = y + _make_shift(DN_CONV - 1 - j)(x) * w[j:j + 1]
    return _silu(y)


def _make_conv(name):
    def call(x, w, dy=None):
        S, C = x.shape
        col = pl.BlockSpec((S, LANES), lambda i: (0, i))
        wsp = pl.BlockSpec((DN_CONV, LANES), lambda i: (0, i))
        if dy is None:
            def body(x_ref, w_ref, o_ref):
                o_ref[...] = _f_conv(x_ref[...], w_ref[...])
            return _pcall(body, name=f"{name}_fwd", grid=(C // LANES,), in_specs=[col, wsp], out_specs=col,
                          out_shape=jax.ShapeDtypeStruct(x.shape, F32), compiler_params=_params(("parallel",)))(x, w)

        def body(x_ref, w_ref, dy_ref, dx_ref, dw_ref):
            _, vjp = jax.vjp(_f_conv, x_ref[...], w_ref[...])
            dx_ref[...], dw_ref[...] = vjp(dy_ref[...])
        return _pcall(body, name=f"{name}_bwd", grid=(C // LANES,), in_specs=[col, wsp, col], out_specs=[col, wsp],
                      out_shape=[jax.ShapeDtypeStruct(x.shape, x.dtype), jax.ShapeDtypeStruct(w.shape, F32)],
                      compiler_params=_params(("parallel",)))(x, w, dy)

    @jax.custom_vjp
    def op(x, w):
        return call(x, w)

    op.defvjp(lambda x, w: (op(x, w), (x, w)), lambda res, dy: tuple(call(*res, dy)))
    return op


def _dot_nt(a, b):
    return lax.dot_general(a.astype(BF16), b.astype(BF16), (((1,), (1,)), ((), ())), preferred_element_type=F32)


def _dot_nn(a, b):
    return jnp.dot(a.astype(BF16), b.astype(BF16), preferred_element_type=F32)


def _dot_tn(a, b):
    return lax.dot_general(a.astype(BF16), b.astype(BF16), (((0,), (0,)), ((), ())), preferred_element_type=F32)


def _mla_scores(qn, qp, kn, kp, diagonal):
    scale = MLA_QK ** -0.5
    s = _dot_nt(qn * scale, kn) + _dot_nt(qp * scale, kp)
    if diagonal:
        r = lax.broadcasted_iota(jnp.int32, s.shape, 0)
        c = lax.broadcasted_iota(jnp.int32, s.shape, 1)
        s = jnp.where(c <= r, s, NEG)
    return s


def _on_causal_pairs(qi, ki, step):
    @pl.when(ki < qi)
    def _():
        step(False)

    @pl.when(ki == qi)
    def _():
        step(True)


def _causal_pairs(n, t, by_key, heads):
    pairs = [(q, k) for k in range(n) for q in range(k, n)] if by_key else [(q, k) for q in range(n) for k in range(q + 1)]
    qt, kt = (jnp.array([p[i] for p in pairs], jnp.int32) for i in (0, 1))
    return (qt, kt, pl.BlockSpec((t, heads * LANES), lambda h, p, qt_r, kt_r: (qt_r[p], h)),
            pl.BlockSpec((t, heads * LANES), lambda h, p, qt_r, kt_r: (kt_r[p], h)))


MLA_FWD_HEADS = 4
MLA_BWD_HEADS = 2


def _make_mla_attn(name):
    scale = MLA_QK ** -0.5

    def fwd_call(qn, qp, kn, kp, v):
        S = qn.shape[0]
        t = min(S, 512)
        n = S // t
        hp = MLA_FWD_HEADS
        qt, kt, qs, ks = _causal_pairs(n, t, False, hp)
        cols = [slice(j * LANES, (j + 1) * LANES) for j in range(hp)]

        def body(qt_r, kt_r, qn_r, qp_r, kn_r, kp_r, v_r, o_r, lse_r, m_s, l_s, acc_s):
            qi, ki = qt_r[pl.program_id(1)], kt_r[pl.program_id(1)]

            @pl.when(ki == 0)
            def _():
                m_s[...] = jnp.full_like(m_s, NEG)
                l_s[...] = jnp.zeros_like(l_s)
                acc_s[...] = jnp.zeros_like(acc_s)

            def step(diagonal):
                for c in cols:
                    s = _mla_scores(qn_r[:, c], qp_r[:, c], kn_r[:, c], kp_r[:, c], diagonal)
                    m_old = m_s[:, c]
                    m_new = jnp.maximum(m_old, jnp.max(s, axis=-1, keepdims=True))
                    p = jnp.exp(s - m_new[:, :1])
                    alpha = jnp.exp(m_old - m_new)
                    l_s[:, c] = alpha * l_s[:, c] + jnp.sum(p, axis=-1, keepdims=True)
                    acc_s[:, c] = alpha * acc_s[:, c] + _dot_nn(p, v_r[:, c])
                    m_s[:, c] = m_new
            _on_causal_pairs(qi, ki, step)

            @pl.when(ki == qi)
            def _():
                o_r[...] = acc_s[...] / l_s[...]
                lse_r[...] = m_s[...] + jnp.log(l_s[...])

        spec = pltpu.PrefetchScalarGridSpec(num_scalar_prefetch=2, grid=(MLA_HEADS // hp, qt.shape[0]), in_specs=[qs, qs, ks, ks, ks],
                                            out_specs=[qs, qs], scratch_shapes=[pltpu.VMEM((t, hp * LANES), F32)] * 3)
        return _pcall(body, name=f"{name}_fwd", grid_spec=spec, out_shape=[jax.ShapeDtypeStruct((S, MLA_HEADS * LANES), F32)] * 2,
                      compiler_params=_params(("parallel", "arbitrary")))(qt, kt, qn, qp, kn, kp, v)

    def bwd_call(qn, qp, kn, kp, v, o, lse, do):
        S = qn.shape[0]
        t = min(S, 512)
        n = S // t
        hp = MLA_BWD_HEADS
        qt, kt, qs, ks = _causal_pairs(n, t, True, hp)
        head = pl.BlockSpec((S, hp * LANES), lambda h, p, qt_r, kt_r: (0, h))
        n_pairs = qt.shape[0]
        cols = [slice(j * LANES, (j + 1) * LANES) for j in range(hp)]

        def body(qt_r, kt_r, qn_r, qp_r, kn_r, kp_r, v_r, o_r, lse_r, do_r, dqn_r, dqp_r, dkn_r, dkp_r, dv_r, dkn_s, dkp_s, dv_s, dl_s):
            pair = pl.program_id(1)
            qi, ki = qt_r[pair], kt_r[pair]
            rows = pl.ds(pl.multiple_of(qi * t, t), t)

            @pl.when(pair == 0)
            def _():
                dqn_r[...] = jnp.zeros_like(dqn_r)
                dqp_r[...] = jnp.zeros_like(dqp_r)

            @pl.when(ki == 0)
            def _():
                for c in cols:
                    dl_s[rows, c] = jnp.broadcast_to(jnp.sum(do_r[:, c] * o_r[:, c], axis=-1, keepdims=True), (t, LANES))

            @pl.when(qi == ki)
            def _():
                dkn_s[...] = jnp.zeros_like(dkn_s)
                dkp_s[...] = jnp.zeros_like(dkp_s)
                dv_s[...] = jnp.zeros_like(dv_s)

            def step(diagonal):
                for c in cols:
                    qn, qp, kn, kp, do = qn_r[:, c], qp_r[:, c], kn_r[:, c], kp_r[:, c], do_r[:, c]
                    p = jnp.exp(_mla_scores(qn, qp, kn, kp, diagonal) - lse_r[:, c][:, :1])
                    ds = p * (_dot_nt(do, v_r[:, c]) - dl_s[rows, c][:, :1])
                    dv_s[:, c] += _dot_tn(p, do)
                    dkn_s[:, c] += _dot_tn(ds, qn * scale)
                    dkp_s[:, c] += _dot_tn(ds, qp * scale)
                    dqn_r[rows, c] += _dot_nn(ds, kn)
                    dqp_r[rows, c] += _dot_nn(ds, kp)
            _on_causal_pairs(qi, ki, step)

            @pl.when(qi == n - 1)
            def _():
                dkn_r[...] = dkn_s[...]
                dkp_r[...] = dkp_s[...]
                dv_r[...] = dv_s[...]

            @pl.when(pair == n_pairs - 1)
            def _():
                dqn_r[...] = dqn_r[...] * scale
                dqp_r[...] = dqp_r[...] * scale

        spec = pltpu.PrefetchScalarGridSpec(num_scalar_prefetch=2, grid=(MLA_HEADS // hp, n_pairs), in_specs=[qs, qs, ks, ks, ks, qs, qs, qs],
                                            out_specs=[head, head, ks, ks, ks],
                                            scratch_shapes=[pltpu.VMEM((t, hp * LANES), F32)] * 3 + [pltpu.VMEM((S, hp * LANES), F32)])
        return _pcall(body, name=f"{name}_bwd", grid_spec=spec, out_shape=[jax.ShapeDtypeStruct((S, MLA_HEADS * LANES), F32)] * 5,
                      compiler_params=_params(("parallel", "arbitrary")))(qt, kt, qn, qp, kn, kp, v, o, lse, do)

    @jax.custom_vjp
    def op(qn, qp, kn, kp, v):
        return fwd_call(qn, qp, kn, kp, v)[0]

    def fwd(qn, qp, kn, kp, v):
        o, lse = fwd_call(qn, qp, kn, kp, v)
        return o, (qn, qp, kn, kp, v, o, lse)

    def bwd(res, do):
        return tuple(bwd_call(*res, do))

    op.defvjp(fwd, bwd)
    return op


def _dil_block(q, kp, kc, vp, vc, has_prev):
    scale = LANES ** -0.5
    r = lax.broadcasted_iota(jnp.int32, (DIL_BLOCK, 2 * DIL_BLOCK), 0)
    c = lax.broadcasted_iota(jnp.int32, (DIL_BLOCK, 2 * DIL_BLOCK), 1)
    valid = ((c < DIL_BLOCK) & (c >= r) & has_prev) | ((c >= DIL_BLOCK) & (c - DIL_BLOCK <= r))
    s = jnp.where(valid, _dot_nt(q * scale, jnp.concatenate([kp, kc], axis=0)), NEG)
    m = jnp.max(s, axis=-1, keepdims=True)
    e = jnp.exp(s - m)
    den = jnp.sum(e, axis=-1, keepdims=True)
    o = _dot_nn(e, jnp.concatenate([vp, vc], axis=0)) / den
    return o, jnp.broadcast_to(m + jnp.log(den), o.shape)


DIL_TILE_ROWS = (1024, 1024, 2048)
DIL_STEP_HEADS = 2


def _make_dil_attn(name, d, tile_rows):
    def call(q, k, v, cts=None):
        S = q.shape[0]
        span = DIL_BLOCK * d
        G = max(1, min(tile_rows, S) // span)
        n = S // (G * span)
        at = (lambda i: i) if cts is None else (lambda i: n - 1 - i)
        step_heads = DIL_STEP_HEADS if d == 1 else 1
        width = step_heads * LANES
        tile = pl.BlockSpec((G * span, width), lambda h, i: (at(i), h))
        before = pl.BlockSpec((span, width), lambda h, i: (jnp.maximum(at(i) * G - 1, 0), h))
        heads = [slice(h * LANES, (h + 1) * LANES) for h in range(step_heads)]

        def rows(r, j):
            return pl.ds(j * DIL_BLOCK, DIL_BLOCK) if d == 1 else pl.ds(r + j * span, DIL_BLOCK, stride=d)

        def over_residues(fn):
            if d == 1:
                fn(0)
            else:
                lax.fori_loop(0, d, lambda r, c: (fn(r), c)[1], 0)

        def block_inputs(r, j, c, q_r, kb_r, k_r, vb_r, v_r):
            kp = kb_r[rows(r, 0), c] if j == 0 else k_r[rows(r, j - 1), c]
            vp = vb_r[rows(r, 0), c] if j == 0 else v_r[rows(r, j - 1), c]
            return q_r[rows(r, j), c], kp, k_r[rows(r, j), c], vp, v_r[rows(r, j), c]

        if cts is None:
            def body(q_r, kb_r, k_r, vb_r, v_r, o_r, lse_r):
                first = at(pl.program_id(1)) * G

                def residue(r):
                    for j in range(G):
                        for c in heads:
                            o_r[rows(r, j), c], lse_r[rows(r, j), c] = _dil_block(
                                *block_inputs(r, j, c, q_r, kb_r, k_r, vb_r, v_r), first + j > 0)
                over_residues(residue)
            return _pcall(body, name=f"{name}_fwd", grid=(DIL_GROUP_HEADS // step_heads, n), in_specs=[tile, before, tile, before, tile],
                          out_specs=[tile, tile], out_shape=[jax.ShapeDtypeStruct(q.shape, F32)] * 2,
                          compiler_params=_params(("parallel", "parallel")))(q, k, k, v, v)

        def body(q_r, kb_r, k_r, vb_r, v_r, do_r, dl_r, dq_r, dk_r, dv_r, ck_s, cv_s):
            first = at(pl.program_id(1)) * G

            @pl.when(pl.program_id(1) == 0)
            def _():
                ck_s[...] = jnp.zeros_like(ck_s)
                cv_s[...] = jnp.zeros_like(cv_s)

            def residue(r):
                for c in heads:
                    owed = None
                    for j in range(G):
                        hp = first + j > 0
                        _, vjp = jax.vjp(lambda *a: _dil_block(*a, hp), *block_inputs(r, j, c, q_r, kb_r, k_r, vb_r, v_r))
                        dq, dkp, dkc, dvp, dvc = vjp((do_r[rows(r, j), c], dl_r[rows(r, j), c]))
                        dq_r[rows(r, j), c] = dq
                        if j == G - 1:
                            dkc, dvc = dkc + ck_s[rows(r, 0), c], dvc + cv_s[rows(r, 0), c]
                        dk_r[rows(r, j), c], dv_r[rows(r, j), c] = dkc, dvc
                        if j == 0:
                            owed = (dkp, dvp)
                        else:
                            dk_r[rows(r, j - 1), c] += dkp
                            dv_r[rows(r, j - 1), c] += dvp
                    ck_s[rows(r, 0), c], cv_s[rows(r, 0), c] = owed
            over_residues(residue)
        return _pcall(body, name=f"{name}_bwd", grid=(DIL_GROUP_HEADS // step_heads, n), in_specs=[tile, before, tile, before, tile, tile, tile],
                      out_specs=[tile] * 3, out_shape=[jax.ShapeDtypeStruct(q.shape, F32)] * 3,
                      scratch_shapes=[pltpu.VMEM((span, width), F32)] * 2,
                      compiler_params=_params(("parallel", "arbitrary")))(q, k, k, v, v, *cts)

    @jax.custom_vjp
    def op(q, k, v):
        return tuple(call(q, k, v))

    op.defvjp(lambda q, k, v: (op(q, k, v), (q, k, v)), lambda res, cts: tuple(call(*res, cts=cts)))
    return op


def _pdot(a, b, dims):
    return lax.dot_general(a, b, (dims, ((), ())), precision=lax.Precision.HIGH, preferred_element_type=F32)


DN_LOCAL_CHUNKS = 8
DN_SCAN_CHUNKS = 8


DN_BLOCK_HEADS = 4
DN_BLOCK = DN_BLOCK_HEADS * DN_CHUNK


def _inverse_cotangent(inv, d):
    return -_pdot(inv, _pdot(d, inv, ((1,), (1,))), ((0,), (0,)))


@jax.custom_vjp
def _unit_lower_inverse(a):
    n = a.shape[0]
    eye = (lax.broadcasted_iota(jnp.int32, (n, n), 0) == lax.broadcasted_iota(jnp.int32, (n, n), 1)).astype(F32)
    inv, pw = eye - a, a
    for level in range(5):
        dot = _pdot if level < 2 else (lambda x, y, dims: lax.dot_general(x, y, (dims, ((), ())), preferred_element_type=F32))
        pw = dot(pw, pw, ((1,), (0,)))
        inv = inv + dot(inv, pw, ((1,), (0,)))
    return inv


def _unit_lower_inverse_fwd(a):
    inv = _unit_lower_inverse(a)
    return inv, inv


_unit_lower_inverse.defvjp(_unit_lower_inverse_fwd, lambda inv, d: (_inverse_cotangent(inv, d),))


@jax.custom_vjp
def _known_inverse(a, inv):
    return inv


_known_inverse.defvjp(lambda a, inv: (inv, inv), lambda inv, d: (_inverse_cotangent(inv, d), jnp.zeros_like(inv)))


def _dn_local(q, k, v, g, b, known=None):
    C, R = DN_CHUNK, DN_BLOCK
    r = lax.broadcasted_iota(jnp.int32, (R, R), 0)
    c = lax.broadcasted_iota(jnp.int32, (R, R), 1)
    same_head = (r // C) == (c // C)
    incl, strict = same_head & (r >= c), same_head & (r > c)
    avg = jnp.full((R, LANES), 1.0 / LANES, F32)
    rc = lax.broadcasted_iota(jnp.int32, (C, C), 0) >= lax.broadcasted_iota(jnp.int32, (C, C), 1)
    gc_lanes = _pdot(rc.astype(F32), g, ((1,), (0,)))
    us, ws, qes, kds, qks, invs = [], [], [], [], [], []
    for first in range(0, DN_HEADS, DN_BLOCK_HEADS):
        stack = lambda x: jnp.concatenate(_heads(x)[first:first + DN_BLOCK_HEADS], axis=0)
        unstack = lambda x: [x[p * C:(p + 1) * C] for p in range(DN_BLOCK_HEADS)]
        gc, q_s, k_s, v_s, b_s = (stack(x) for x in (gc_lanes, q, k, v, b))
        gc_j = _pdot(avg, gc, ((1,), (1,)))
        decay = jnp.exp(jnp.where(incl, _cat([gc] * (R // LANES)) - gc_j, NEG))
        kb = k_s * b_s
        kk = _pdot(jnp.concatenate([kb, q_s], axis=0), k_s, ((1,), (1,)))
        a = jnp.where(strict, kk[:R] * decay, 0.0)
        inv = _unit_lower_inverse(a) if known is None else _known_inverse(a, known[len(invs) * R:(len(invs) + 1) * R])
        invs.append(inv)
        eg = jnp.exp(gc)
        uw = _pdot(inv, _cat([v_s * b_s, kb * eg]), ((1,), (0,)))
        g_last = jnp.concatenate([jnp.broadcast_to(x[C - 1:C], (C, LANES)) for x in unstack(gc)], axis=0)
        us += unstack(uw[:, :LANES])
        ws += unstack(uw[:, LANES:])
        qes += unstack(q_s * eg)
        kds += unstack(k_s * jnp.exp(g_last - gc))
        qks.append(kk[R:] * decay)
    egl = jnp.broadcast_to(jnp.exp(gc_lanes[C - 1:C]), (8, DN_HEADS * LANES))
    return _cat(us), _cat(ws), _cat(qes), _cat(kds), jnp.concatenate(qks, axis=0), egl, jnp.concatenate(invs, axis=0)


def _dn_scan(u, w, qe, kd, qk, egl, state):
    C = DN_CHUNK
    heads = [slice(h * LANES, (h + 1) * LANES) for h in range(DN_HEADS)]
    ws = [_pdot(jnp.concatenate([w[:, sl], qe[:, sl]], axis=0), state[sl, :], ((1,), (0,))) for sl in heads]
    v_new = [u[:, sl] - x[:C] for sl, x in zip(heads, ws)]
    local = []
    for i, first in enumerate(range(0, DN_HEADS, DN_BLOCK_HEADS)):
        y = _pdot(qk[i * DN_BLOCK:(i + 1) * DN_BLOCK], jnp.concatenate(v_new[first:first + DN_BLOCK_HEADS], axis=0), ((1,), (0,)))
        local += [y[p * C:(p + 1) * C] for p in range(DN_BLOCK_HEADS)]
    o = _cat([x[C:] + y for x, y in zip(ws, local)])
    states = [state[sl, :] * egl[0:1, sl] + _pdot(kd[:, sl], vn, ((0,), (0,))) for sl, vn in zip(heads, v_new)]
    return o, jnp.concatenate(states, axis=0)


def _make_delta_rule(name):
    W = DN_HEADS * LANES
    QK = DN_HEADS * DN_CHUNK

    def local_call(ins, cts=None):
        S = ins[0].shape[0]
        n = S // DN_CHUNK
        per = math.gcd(DN_LOCAL_CHUNKS, n)
        row = pl.BlockSpec((per * DN_CHUNK, W), lambda i: (i, 0))
        qkb = pl.BlockSpec((per * QK, DN_BLOCK), lambda i: (i, 0))
        eg = pl.BlockSpec((per, 8, W), lambda i: (i, 0, 0))
        rows = lambda j: slice(j * DN_CHUNK, (j + 1) * DN_CHUNK)
        qk_rows = lambda j: slice(j * QK, (j + 1) * QK)
        out_rows = [rows, rows, rows, rows, qk_rows]

        if cts is None:
            def body(*refs):
                for j in range(per):
                    res = _dn_local(*[r[rows(j), :] for r in refs[:5]])
                    for o_r, o, at_ in zip(refs[5:10], res[:5], out_rows):
                        o_r[at_(j), :] = o
                    refs[10][j] = res[5]
                    refs[11][qk_rows(j), :] = res[6]
            blockdiag = jax.ShapeDtypeStruct((n * QK, DN_BLOCK), F32)
            return _pcall(body, name=f"{name}_local_fwd", grid=(n // per,), in_specs=[row] * 5, out_specs=[row] * 4 + [qkb, eg, qkb],
                          out_shape=[jax.ShapeDtypeStruct((S, W), F32)] * 4 + [blockdiag, jax.ShapeDtypeStruct((n, 8, W), F32), blockdiag],
                          compiler_params=_params(("parallel",)))(*ins)

        def body(*refs):
            for j in range(per):
                known = refs[5][qk_rows(j), :]
                _, vjp = jax.vjp(lambda *a: _dn_local(*a, known=known)[:6], *[r[rows(j), :] for r in refs[:5]])
                grads = vjp(tuple(r[at_(j), :] for r, at_ in zip(refs[6:11], out_rows)) + (refs[11][j],))
                for o_r, o in zip(refs[12:], grads):
                    o_r[rows(j), :] = o
        return _pcall(body, name=f"{name}_local_bwd", grid=(n // per,), in_specs=[row] * 5 + [qkb] + [row] * 4 + [qkb, eg],
                      out_specs=[row] * 5, out_shape=[jax.ShapeDtypeStruct((S, W), F32)] * 5,
                      compiler_params=_params(("parallel",)))(*ins, *cts)

    def scan_call(ins, saved=None, do=None):
        S = ins[0].shape[0]
        n = S // DN_CHUNK
        per = math.gcd(DN_SCAN_CHUNKS, n)
        steps = n // per
        at = (lambda i: i) if do is None else (lambda i: steps - 1 - i)
        row = pl.BlockSpec((per * DN_CHUNK, W), lambda i: (at(i), 0))
        qkb = pl.BlockSpec((per * QK, DN_BLOCK), lambda i: (at(i), 0))
        eg = pl.BlockSpec((per, 8, W), lambda i: (at(i), 0, 0))
        st = pl.BlockSpec((per, W, LANES), lambda i: (at(i), 0, 0))
        rows = lambda j: slice(j * DN_CHUNK, (j + 1) * DN_CHUNK)
        qk_rows = lambda j: slice(j * QK, (j + 1) * QK)
        chunk = lambda refs, j: [r[rows(j), :] for r in refs[:4]] + [refs[4][qk_rows(j), :], refs[5][j]]

        if do is None:
            def body(*refs):
                o_r, st_r, s_s = refs[6:]

                @pl.when(pl.program_id(0) == 0)
                def _():
                    s_s[...] = jnp.zeros_like(s_s)
                state = s_s[...]
                for j in range(per):
                    st_r[j] = state
                    o_r[rows(j), :], state = _dn_scan(*chunk(refs, j), state)
                s_s[...] = state
            return _pcall(body, name=f"{name}_scan_fwd", grid=(steps,), in_specs=[row] * 4 + [qkb, eg], out_specs=[row, st],
                          out_shape=[jax.ShapeDtypeStruct((S, W), F32), jax.ShapeDtypeStruct((n, W, LANES), F32)],
                          scratch_shapes=[pltpu.VMEM((W, LANES), F32)], compiler_params=_params(("arbitrary",)))(*ins)

        def body(*refs):
            st_r, do_r = refs[6:8]
            outs, ds_s = refs[8:14], refs[14]

            @pl.when(pl.program_id(0) == 0)
            def _():
                ds_s[...] = jnp.zeros_like(ds_s)
            ds = ds_s[...]
            for j in reversed(range(per)):
                _, vjp = jax.vjp(_dn_scan, *chunk(refs, j), st_r[j])
                *grads, ds = vjp((do_r[rows(j), :], ds))
                for o_r, gval in zip(outs[:4], grads[:4]):
                    o_r[rows(j), :] = gval
                outs[4][qk_rows(j), :] = grads[4]
                outs[5][j] = grads[5]
            ds_s[...] = ds
        return _pcall(body, name=f"{name}_scan_bwd", grid=(steps,), in_specs=[row] * 4 + [qkb, eg, st, row], out_specs=[row] * 4 + [qkb, eg],
                      out_shape=[jax.ShapeDtypeStruct((S, W), F32)] * 4
                      + [jax.ShapeDtypeStruct((n * QK, DN_BLOCK), F32), jax.ShapeDtypeStruct((n, 8, W), F32)],
                      scratch_shapes=[pltpu.VMEM((W, LANES), F32)], compiler_params=_params(("arbitrary",)))(*ins, saved, do)

    @jax.custom_vjp
    def local(q, k, v, g, b):
        return tuple(local_call((q, k, v, g, b))[:6])

    def local_fwd(*a):
        *outs, inverses = local_call(a)
        return tuple(outs), (*a, inverses)

    local.defvjp(local_fwd, lambda res, cts: tuple(local_call(res, tuple(cts))))

    @jax.custom_vjp
    def scan(u, w, qe, kd, qk, egl):
        return scan_call((u, w, qe, kd, qk, egl))[0]

    def scan_fwd(*a):
        o, states = scan_call(a)
        return o, (a, states)

    scan.defvjp(scan_fwd, lambda res, do: tuple(scan_call(res[0], res[1], do)))
    return lambda q, k, v, g, b: scan(*local(q, k, v, g, b))


def _loss_call(y, target):
    S, D = y.shape
    t = min(S, 512)
    n = S // t
    row = pl.BlockSpec((t, D), lambda i: (i, 0))

    def body(y_r, t_r, loss_r, dy_r, acc_s):
        i = pl.program_id(0)

        @pl.when(i == 0)
        def _():
            acc_s[...] = jnp.zeros_like(acc_s)
        err = y_r[...] - t_r[...]
        dy_r[...] = err * (1.0 / D)
        acc_s[...] += jnp.sum(err * err, axis=0, keepdims=True)

        @pl.when(i == n - 1)
        def _():
            loss_r[...] = jnp.broadcast_to(jnp.sum(acc_s[...], axis=1, keepdims=True) * (0.5 / D), loss_r.shape)

    return _pcall(body, name="loss_head", grid=(n,), in_specs=[row, row],
                  out_specs=[pl.BlockSpec((8, LANES), lambda i: (0, 0)), row],
                  out_shape=[jax.ShapeDtypeStruct((8, LANES), F32), jax.ShapeDtypeStruct((S, D), F32)],
                  scratch_shapes=[pltpu.VMEM((1, D), F32)], compiler_params=_params(("arbitrary",)))(y, target)


def _adamw_layer(name, part, w, m, v, layer, earlier=None, rows=128):
    L, R, C = w.shape
    t = _tile(R, rows, 8)
    row = pl.BlockSpec((None, t, C), lambda i: (layer, i, 0))

    def body(p_r, w_r, m_r, v_r, *rest):
        g_r, d_r, nm_r, nv_r = rest[-4:]
        g = p_r[0].astype(F32)
        for s in range(1, N_DEV):
            g = g + p_r[s].astype(F32)
        m_new = ADAM_B1 * m_r[...] + (1.0 - ADAM_B1) * g
        v_new = ADAM_B2 * v_r[...] + (1.0 - ADAM_B2) * (g * g)
        m_hat = m_new / (1.0 - ADAM_B1 ** ADAM_STEP)
        v_hat = v_new / (1.0 - ADAM_B2 ** ADAM_STEP)
        g_r[...] = g
        d_r[...] = -ADAM_LR * (m_hat / (jnp.sqrt(v_hat) + ADAM_EPS) + ADAM_WD * w_r[...])
        nm_r[...] = m_new
        nv_r[...] = v_new

    extra = [] if earlier is None else list(earlier)
    return _pcall(body, name=name, grid=(R // t,),
                  in_specs=[pl.BlockSpec((N_DEV, t, C), lambda i: (0, i, 0)), row, row, row] + [pl.BlockSpec(memory_space=pl.ANY)] * len(extra),
                  out_specs=[row] * 4, out_shape=[jax.ShapeDtypeStruct((L, R, C), F32)] * 4,
                  input_output_aliases={4 + k: k for k in range(len(extra))},
                  compiler_params=_params(("parallel",)))(part, w, m, v, *extra)


def _my_place():
    x, y, c = lax.axis_index("x"), lax.axis_index("y"), lax.axis_index("c")
    return x, y, c


def _index(x, y, c):
    return 4 * x + 2 * y + c


def _all_gather(vs):
    n = len(vs)

    def body(*refs):
        v_refs, out_refs = refs[:n], refs[n:2 * n]
        send_sems, recv_sems, local_sems = refs[2 * n:]
        x, y, c = _my_place()
        me, sibling = (x, y, c), (x, y, 1 - c)
        chips = [(1 - x, y), (x, 1 - y), (1 - x, 1 - y)]

        def copy(a, k, block, to, src=None):
            rows = out_refs[a].at[_index(*block)]
            return pltpu.make_async_remote_copy(src_ref=rows if src is None else src, dst_ref=rows, send_sem=send_sems.at[a, k],
                                                recv_sem=recv_sems.at[a, k], device_id=to, device_id_type=MESH)

        mine = [pltpu.make_async_copy(v_refs[a], out_refs[a].at[_index(*me)], local_sems.at[a]) for a in range(n)]
        first, passed = [], []
        for a in range(n):
            mine[a].start()
            first += [copy(a, 0, me, sibling, src=v_refs[a])]
            first += [copy(a, 1 + j, me, (*chip, c), src=v_refs[a]) for j, chip in enumerate(chips)]
        for cp in first:
            cp.start()
        for j, chip in enumerate(chips):
            for a in range(n):
                copy(a, 1 + j, (*chip, c), me).wait_recv()
                passed.append(copy(a, 4 + j, (*chip, c), sibling))
                passed[-1].start()
        for a in range(n):
            copy(a, 0, sibling, me).wait_recv()
            for j, chip in enumerate(chips):
                copy(a, 4 + j, (*chip, 1 - c), me).wait_recv()
        for cp in first + passed:
            cp.wait_send()
        for a in range(n):
            mine[a].wait()

    any_ = pl.BlockSpec(memory_space=pl.ANY)
    return _pcall(body, name="gather_weights", in_specs=[any_] * n, out_specs=[any_] * n,
                  out_shape=[jax.ShapeDtypeStruct((N_DEV,) + v.shape, v.dtype) for v in vs],
                  scratch_shapes=[pltpu.SemaphoreType.DMA((n, 7)), pltpu.SemaphoreType.DMA((n, 7)), pltpu.SemaphoreType.DMA((n,))])(*vs)


def _all_to_all(vs, after):
    n = len(vs)

    def body(*refs):
        v_refs, out_refs = refs[:n], refs[n + 1:2 * n + 1]
        send_sems, recv_sems, local_sems = refs[2 * n + 1:]
        x, y, c = _my_place()
        me = _index(x, y, c)
        mine = [pltpu.make_async_copy(v_refs[a].at[me], out_refs[a].at[me], local_sems.at[a]) for a in range(n)]
        copies = []
        for a in range(n):
            mine[a].start()
        for k in range(1, N_DEV):
            px = 1 - x if k & 4 else x
            py = 1 - y if k & 2 else y
            pc = 1 - c if k & 1 else c
            for a in range(n):
                cp = pltpu.make_async_remote_copy(src_ref=v_refs[a].at[_index(px, py, pc)], dst_ref=out_refs[a].at[me],
                                                  send_sem=send_sems.at[a, k - 1], recv_sem=recv_sems.at[a, k - 1],
                                                  device_id=(px, py, pc), device_id_type=MESH)
                cp.start()
                copies.append(cp)
        for cp in copies:
            cp.wait()
        for a in range(n):
            mine[a].wait()

    any_ = pl.BlockSpec(memory_space=pl.ANY)
    return _pcall(body, name="exchange_vectors", in_specs=[any_] * (n + 1), out_specs=[any_] * n,
                  out_shape=[jax.ShapeDtypeStruct(v.shape, v.dtype) for v in vs],
                  scratch_shapes=[pltpu.SemaphoreType.DMA((n, 7)), pltpu.SemaphoreType.DMA((n, 7)), pltpu.SemaphoreType.DMA((n,))])(*vs, after)


_HBM = pl.BlockSpec(memory_space=pltpu.HBM)
_SEM = pl.BlockSpec(memory_space=pltpu.SEMAPHORE)
_EFFECT = pltpu.SideEffectType.DATAFLOW_SIDE_EFFECTING


def _direct_copies(gather, v_refs, land_refs, send_sems, recv_sems, local_sems):
    x, y, c = _my_place()
    me = _index(x, y, c)
    local, remote = [], []
    for a, (v_ref, land_ref) in enumerate(zip(v_refs, land_refs)):
        local.append(pltpu.make_async_copy(v_ref if gather else v_ref.at[me], land_ref.at[me], local_sems.at[a]))
    for k in range(1, N_DEV):
        px = 1 - x if k & 4 else x
        py = 1 - y if k & 2 else y
        pc = 1 - c if k & 1 else c
        for a, (v_ref, land_ref) in enumerate(zip(v_refs, land_refs)):
            sem = a * (N_DEV - 1) + k - 1
            remote.append(pltpu.make_async_remote_copy(
                src_ref=v_ref if gather else v_ref.at[_index(px, py, pc)], dst_ref=land_ref.at[me], send_sem=send_sems.at[sem],
                recv_sem=recv_sems.at[sem], device_id=(px, py, pc), device_id_type=MESH))
    return local, remote


def _exchange_start(name, vs, gather, thru):
    n = len(vs)
    lands = [lax.empty((N_DEV,) + v.shape if gather else v.shape, v.dtype) for v in vs]

    def body(*refs):
        v_refs, land_refs = refs[:n], refs[n:2 * n]
        send_sems, recv_sems, local_sems = refs[2 * n + 1:2 * n + 4]
        local, remote = _direct_copies(gather, v_refs, land_refs, send_sems, recv_sems, local_sems)
        for cp in local + remote:
            cp.start()

    hbm = lambda a: pltpu.HBM(a.shape, a.dtype)
    res = _pcall(body, name=name,
                 out_shape=(pltpu.SemaphoreType.DMA((n * (N_DEV - 1),)), pltpu.SemaphoreType.DMA((n * (N_DEV - 1),)), pltpu.SemaphoreType.DMA((n,)),
                            *[hbm(a) for a in (*vs, *lands, thru)]),
                 in_specs=[_HBM] * (2 * n + 1), out_specs=(_SEM, _SEM, _SEM, *[_HBM] * (2 * n + 1)),
                 input_output_aliases={i: 3 + i for i in range(2 * n + 1)},
                 compiler_params=pltpu.CompilerParams(has_side_effects=_EFFECT))(
        *[pltpu.with_memory_space_constraint(a, pltpu.HBM) for a in (*vs, *lands, thru)])
    return (gather, res[:3], res[3:3 + n], res[3 + n:3 + 2 * n]), res[3 + 2 * n]


def _exchange_wait(name, started, after):
    gather, sems, vs, lands = started
    n = len(vs)

    def body(*refs):
        v_refs, land_refs = refs[:n], refs[n:2 * n]
        send_sems, recv_sems, local_sems = refs[2 * n:2 * n + 3]
        local, remote = _direct_copies(gather, v_refs, land_refs, send_sems, recv_sems, local_sems)
        for cp in local:
            cp.wait()
        for cp in remote:
            cp.wait_send()
            cp.wait_recv()

    hbm = lambda a: pltpu.HBM(a.shape, a.dtype)
    res = _pcall(body, name=name, out_shape=tuple(hbm(a) for a in (*vs, *lands)),
                 in_specs=[_HBM] * (2 * n) + [_SEM] * 3 + [pl.BlockSpec(memory_space=pl.ANY)], out_specs=tuple([_HBM] * (2 * n)),
                 input_output_aliases={i: i for i in range(2 * n)},
                 compiler_params=pltpu.CompilerParams(has_side_effects=_EFFECT))(*vs, *lands, *sems, after)
    return list(res[n:])


W_IN_SHARD = IN_WIDTH // N_DEV
SEG_ORDER = ("q_lat", "c_kv", "k_pe", "z_a", "dn_qkv", "dn_ab", "z_b", "dil_qkv", "z_c", "gate")
SEG_WIDTH = (384, 256, LANES, 512, 1536, LANES, 512, 4608, 512, 3072)


def _w_in_plan():
    plan = []

    def add(seg, c0, c1, dst):
        while c0 < c1:
            d = c0 // W_IN_SHARD
            e = min(c1, (d + 1) * W_IN_SHARD)
            plan.append((seg, dst, d, c0 - d * W_IN_SHARD, e - c0))
            dst += e - c0
            c0 = e

    half = MLA_ROPE // 2
    for i, name in enumerate(SEG_ORDER):
        if name == "k_pe":
            o = _SEG["k_pe"][0]
            add(i, o, o + half, 0)
            add(i, o + half, o + 2 * half, LANES // 2)
        elif name == "dn_ab":
            o = _SEG["dn_a"][0]
            add(i, o, o + 2 * DN_HEADS, 0)
        else:
            o, w = _SEG[name]
            add(i, o, o + w, 0)
    return plan


def _make_w_in_segments(name):
    plan = _w_in_plan()
    nseg = len(SEG_ORDER)
    t = 256

    def fwd_call(g):
        L = g.shape[1]

        def body(g_ref, *o_refs):
            for i in (SEG_ORDER.index("k_pe"), SEG_ORDER.index("dn_ab")):
                o_refs[i][...] = jnp.zeros_like(o_refs[i])
            for seg, dst, d, src, n in plan:
                o_refs[seg][:, dst:dst + n] = g_ref[d, :, src:src + n]

        return _pcall(body, name=f"{name}_fwd", grid=(L, D_MODEL // t),
                      in_specs=[pl.BlockSpec((N_DEV, None, t, W_IN_SHARD), lambda l, i: (0, l, i, 0))],
                      out_specs=[pl.BlockSpec((None, t, w), lambda l, i: (l, i, 0)) for w in SEG_WIDTH],
                      out_shape=[jax.ShapeDtypeStruct((L, D_MODEL, w), g.dtype) for w in SEG_WIDTH],
                      compiler_params=_params(("parallel", "parallel")))(g)

    def bwd_call(ds):
        L = ds[0].shape[0]

        def body(*refs):
            d_refs, g_ref = refs[:nseg], refs[nseg]
            for seg, dst, d, src, n in plan:
                g_ref[d, :, src:src + n] = d_refs[seg][:, dst:dst + n]

        return _pcall(body, name=f"{name}_bwd", grid=(L, D_MODEL // t),
                      in_specs=[pl.BlockSpec((None, t, w), lambda l, i: (l, i, 0)) for w in SEG_WIDTH],
                      out_specs=pl.BlockSpec((N_DEV, None, t, W_IN_SHARD), lambda l, i: (0, l, i, 0)),
                      out_shape=jax.ShapeDtypeStruct((N_DEV, L, D_MODEL, W_IN_SHARD), ds[0].dtype),
                      compiler_params=_params(("parallel", "parallel")))(*ds)

    @jax.custom_vjp
    def op(g):
        return tuple(fwd_call(g))

    op.defvjp(lambda g: (op(g), None), lambda _, ds: (bwd_call(tuple(ds)),))
    return op


def _pe_pad(a):
    h = MLA_ROPE // 2
    z = jnp.zeros(a.shape[:-1] + (h,), a.dtype)
    return jnp.concatenate([a[..., :h], z, a[..., h:], z], axis=-1)


def _layer_norm(tag, x, norm_g):
    return _make_rowwise(f"{tag}_norm", _f_norm, 512)((x,), (), (norm_g[None, :],), ())[0]


def _layer(tag, x, tables, W):
    h = _layer_norm(tag, x, W["norm_g"])
    return _layer_tail(tag, x, _make_multi_linear(f"{tag}_inproj", 10, INPROJ_DTYPE)(h, W["w_in_segments"]), tables, W)


def _layer_tail(tag, x, segments, tables, W):
    cos_p, sin_p, cos_h, sin_h = tables
    row = lambda a: a[None, :]
    q_lat, c_kv, kpe, z_a, dn_qkv, ab, z_b, dil_qkv, z_c, gl = segments

    qn_lat, ckvn, kp = _make_rowwise(f"{tag}_mla_a", _f_mla_a, 512)(
        (q_lat, c_kv, kpe), (cos_p, sin_p),
        (row(W["mla_q_a_norm_g"]), row(W["mla_kv_a_norm_g"]), row(_pe_pad(W["mla_k_norm_g"][LANES:]))), ())
    wq = W["mla_w_q_b"].reshape(MLA_Q_RANK, MLA_HEADS, MLA_QK)
    wq = jnp.concatenate([wq[:, :, :LANES].reshape(MLA_Q_RANK, -1), _pe_pad(wq[:, :, LANES:]).reshape(MLA_Q_RANK, -1)], axis=1)
    wkv = W["mla_w_kv_b"].reshape(MLA_KV_RANK, MLA_HEADS, 2 * LANES)
    (q8,) = _make_multi_linear(f"{tag}_qb", 1)(qn_lat, (wq,))
    kn_raw, v_mla = _make_multi_linear(f"{tag}_kvb", 2)(
        ckvn, (wkv[:, :, :LANES].reshape(MLA_KV_RANK, -1), wkv[:, :, LANES:].reshape(MLA_KV_RANK, -1)))
    qn, qp, kn = _make_rowwise(f"{tag}_mla_b", _f_mla_b, 512)(
        (q8, kn_raw), (cos_p, sin_p),
        (row(W["mla_q_norm_g"][:LANES]), row(_pe_pad(W["mla_q_norm_g"][LANES:])), row(W["mla_k_norm_g"][:LANES])), ())
    y_a = _make_mla_attn(f"{tag}_mla")(qn, qp, kn, kp, v_mla)

    mixed = _make_conv(f"{tag}_conv")(dn_qkv, W["dn_conv_w"])
    lane_head = jnp.arange(DN_HEADS * LANES) // LANES
    e_a = (jnp.arange(LANES)[:, None] == lane_head[None, :]).astype(F32)
    e_b = (jnp.arange(LANES)[:, None] == lane_head[None, :] + DN_HEADS).astype(F32)
    q_dn, k_dn, v_dn, g_dn, b_dn = _make_rowwise(f"{tag}_dn_pre", _f_dn_pre, 512)(
        (mixed, ab), (), (row(jnp.repeat(W["dn_a_log"], LANES)), row(jnp.repeat(W["dn_dt_bias"], LANES))), (e_a, e_b))
    o_dn = _make_delta_rule(f"{tag}_dn")(q_dn, k_dn, v_dn, g_dn, b_dn)

    qkv_dil = _make_rowwise(f"{tag}_dil_pre", _f_dil_pre, 256)(
        (dil_qkv,), (cos_h, sin_h), (row(W["dil_q_norm_g"]), row(W["dil_k_norm_g"])), ())
    n_groups = len(DIL_DILATIONS)
    o_lse = [_make_dil_attn(f"{tag}_dil{g}", d, DIL_TILE_ROWS[g])(qkv_dil[g], qkv_dil[n_groups + g], qkv_dil[2 * n_groups + g])
             for g, d in enumerate(DIL_DILATIONS)]

    ya, yb, yc = _make_rowwise(f"{tag}_merge_a", _f_merge_a, 256)(
        (y_a, z_a, o_dn, z_b, *[o for o, _ in o_lse], *[l for _, l in o_lse], z_c), (), (row(W["dn_out_norm_g"]),), ())
    (b0,) = _make_multi_linear(f"{tag}_br0", 1)(ya, (W["w_branch"][0],))
    (b1,) = _make_multi_linear(f"{tag}_br1", 1)(yb, (W["w_branch"][1],))
    (b2,) = _make_multi_linear(f"{tag}_br2", 1)(yc, (W["w_branch"][2],))
    (mix,) = _make_rowwise(f"{tag}_merge_b", _f_merge_b, 256)((b0, b1, b2, gl), (), (), ())
    return _make_resid_linear(f"{tag}_out")(x, mix, W["w_out"])


SHARDED = (("w_in", (D_MODEL, W_IN_SHARD)), ("mla_w_q_b", (MLA_Q_RANK, MLA_HEADS * MLA_QK // N_DEV)),
           ("mla_w_kv_b", (MLA_KV_RANK, MLA_HEADS * 2 * LANES // N_DEV)), ("w_branch", (3 * BRANCH_W, D_MODEL // N_DEV)),
           ("w_out", (D_MODEL // N_DEV, D_MODEL)), ("dn_conv_w", (DN_CONV, 3 * DN_HEADS * LANES // N_DEV)))
SMALL = (("norm_g", D_MODEL), ("mla_q_a_norm_g", MLA_Q_RANK), ("mla_kv_a_norm_g", MLA_KV_RANK), ("mla_q_norm_g", MLA_QK),
         ("mla_k_norm_g", MLA_QK), ("dn_a_log", DN_HEADS), ("dn_dt_bias", DN_HEADS), ("dn_out_norm_g", LANES),
         ("dil_q_norm_g", LANES), ("dil_k_norm_g", LANES))
WEIGHTS = ("norm_g", "w_in", "mla_q_a_norm_g", "mla_w_q_b", "mla_kv_a_norm_g", "mla_w_kv_b", "mla_q_norm_g", "mla_k_norm_g",
           "dn_conv_w", "dn_a_log", "dn_dt_bias", "dn_out_norm_g", "dil_q_norm_g", "dil_k_norm_g", "w_branch", "w_out")


def _round_up(n, m):
    return -(-n // m) * m


def _pack_vectors(pieces):
    return jnp.concatenate([jnp.pad(p, (0, _round_up(p.shape[0], LANES) - p.shape[0])) for p in pieces]).reshape(-1, LANES)


def _unpack_vectors(flat, sizes):
    out, off = [], 0
    flat = flat.reshape(-1)
    for n in sizes:
        out.append(flat[off:off + n])
        off += _round_up(n, LANES)
    return out


def _whole_weights(g, small):
    W = dict(small)
    W["mla_w_q_b"] = g["mla_w_q_b"].transpose(1, 0, 2).reshape(MLA_Q_RANK, -1)
    W["mla_w_kv_b"] = g["mla_w_kv_b"].transpose(1, 0, 2).reshape(MLA_KV_RANK, -1)
    W["w_branch"] = g["w_branch"].reshape(N_DEV, 3, BRANCH_W, -1).transpose(1, 2, 0, 3).reshape(3, BRANCH_W, D_MODEL)
    W["w_out"] = g["w_out"].reshape(D_MODEL, D_MODEL)
    W["dn_conv_w"] = g["dn_conv_w"].transpose(1, 0, 2).reshape(DN_CONV, -1)
    return W


def kernel(x, positions, norm_g, w_in, mla_q_a_norm_g, mla_w_q_b, mla_kv_a_norm_g, mla_w_kv_b, mla_q_norm_g, mla_k_norm_g, dn_conv_w, dn_a_log, dn_dt_bias, dn_out_norm_g, dil_q_norm_g, dil_k_norm_g, w_branch, w_out, loss_target, m_norm_g, m_w_in, m_mla_q_a_norm_g, m_mla_w_q_b, m_mla_kv_a_norm_g, m_mla_w_kv_b, m_mla_q_norm_g, m_mla_k_norm_g, m_dn_conv_w, m_dn_a_log, m_dn_dt_bias, m_dn_out_norm_g, m_dil_q_norm_g, m_dil_k_norm_g, m_w_branch, m_w_out, v_norm_g, v_w_in, v_mla_q_a_norm_g, v_mla_w_q_b, v_mla_kv_a_norm_g, v_mla_w_kv_b, v_mla_q_norm_g, v_mla_k_norm_g, v_dn_conv_w, v_dn_a_log, v_dn_dt_bias, v_dn_out_norm_g, v_dil_q_norm_g, v_dil_k_norm_g, v_w_branch, v_w_out):
    w = dict(norm_g=norm_g, w_in=w_in, mla_q_a_norm_g=mla_q_a_norm_g, mla_w_q_b=mla_w_q_b, mla_kv_a_norm_g=mla_kv_a_norm_g,
             mla_w_kv_b=mla_w_kv_b, mla_q_norm_g=mla_q_norm_g, mla_k_norm_g=mla_k_norm_g, dn_conv_w=dn_conv_w, dn_a_log=dn_a_log,
             dn_dt_bias=dn_dt_bias, dn_out_norm_g=dn_out_norm_g, dil_q_norm_g=dil_q_norm_g, dil_k_norm_g=dil_k_norm_g,
             w_branch=w_branch, w_out=w_out)
    m = dict(norm_g=m_norm_g, w_in=m_w_in, mla_q_a_norm_g=m_mla_q_a_norm_g, mla_w_q_b=m_mla_w_q_b, mla_kv_a_norm_g=m_mla_kv_a_norm_g,
             mla_w_kv_b=m_mla_w_kv_b, mla_q_norm_g=m_mla_q_norm_g, mla_k_norm_g=m_mla_k_norm_g, dn_conv_w=m_dn_conv_w,
             dn_a_log=m_dn_a_log, dn_dt_bias=m_dn_dt_bias, dn_out_norm_g=m_dn_out_norm_g, dil_q_norm_g=m_dil_q_norm_g,
             dil_k_norm_g=m_dil_k_norm_g, w_branch=m_w_branch, w_out=m_w_out)
    v = dict(norm_g=v_norm_g, w_in=v_w_in, mla_q_a_norm_g=v_mla_q_a_norm_g, mla_w_q_b=v_mla_w_q_b, mla_kv_a_norm_g=v_mla_kv_a_norm_g,
             mla_w_kv_b=v_mla_w_kv_b, mla_q_norm_g=v_mla_q_norm_g, mla_k_norm_g=v_mla_k_norm_g, dn_conv_w=v_dn_conv_w,
             dn_a_log=v_dn_a_log, dn_dt_bias=v_dn_dt_bias, dn_out_norm_g=v_dn_out_norm_g, dil_q_norm_g=v_dil_q_norm_g,
             dil_k_norm_g=v_dil_k_norm_g, w_branch=v_w_branch, w_out=v_w_out)
    x2, target = x[0], loss_target[0]
    pos = positions[0][:, None]

    names = [n for n, _ in SHARDED]
    view = lambda t, n, s: t[n].reshape((DEPTH,) + s)
    shards = [[(view(w, n, s) if n == "dn_conv_w" else view(w, n, s).astype(BF16))[l] for n, s in SHARDED] for l in range(DEPTH)]
    small = [{n: w[n][l] for n, _ in SMALL} for l in range(DEPTH)]
    (w_in0,) = _all_gather(shards[0][:1])
    gathering0, w_in0 = _exchange_start("gather_layer0_others_start", shards[0][1:], True, w_in0)
    gathering1, w_in0 = _exchange_start("gather_layer1_start", shards[1], True, w_in0)
    tables = _rope_tables(pos, _rope_consts())

    def layer(l, g, small_l, x_l, pending=None):
        tag = f"l{l}"
        seg_op = _make_w_in_segments(f"{tag}_w_in_segments")
        w_segs, vjp_segs = jax.vjp(lambda gw: tuple(s[0] for s in seg_op(gw[:, None])), g["w_in"])
        h, vjp_norm = jax.vjp(lambda x_, ng: _layer_norm(tag, x_, ng), x_l, small_l["norm_g"])
        wide = [i for i in range(len(w_segs)) if i not in INPROJ_NARROW]
        segs = dict(zip(INPROJ_NARROW, _mm_nn_many(f"{tag}_inproj_fwd_narrow", h, [w_segs[i] for i in INPROJ_NARROW], INPROJ_DTYPE)))
        segs.update({i: _mm(f"{tag}_inproj_fwd{i}", h, w_segs[i], "nn", out_dtype=INPROJ_DTYPE, **INPROJ_TILES["nn"]) for i in wide})
        segs = tuple(segs[i] for i in range(len(w_segs)))
        if pending is not None:
            g = dict(g, **dict(zip(names[1:], _exchange_wait(f"gather_{tag}_others_wait", pending, segs[7]))))
        rest_g = {n: a for n, a in g.items() if n != "w_in"}
        rest_s = {n: a for n, a in small_l.items() if n != "norm_g"}
        y, vjp_tail = jax.vjp(lambda sg, x_, gg, ss: _layer_tail(tag, x_, sg, tables, _whole_weights(gg, ss)), segs, x_l, rest_g, rest_s)

        def backward(dy):
            dsegs, dx_skip, d_rest_g, d_rest_s = vjp_tail(dy)
            others, big = _exchange_start(f"exchange_{tag}_others_start", [d_rest_g[n] for n in names if n != "w_in"], False, dsegs[7])
            dsegs = tuple(dsegs[:7]) + (big,) + tuple(dsegs[8:])
            dws = dict(zip(INPROJ_NARROW, _mm_tn_many(f"{tag}_inproj_dw_narrow", h, [dsegs[i] for i in INPROJ_NARROW], w_segs[0].dtype)))
            dws.update({i: _mm(f"{tag}_inproj_dw{i}", h, dsegs[i], "tn", out_dtype=w_segs[i].dtype, **INPROJ_TILES["tn"]) for i in wide})
            dws = tuple(dws[i] for i in range(len(w_segs)))
            projection, first =_exchange_start(f"exchange_{tag}_w_in_start", [vjp_segs(dws)[0]], False, dsegs[0])
            dsegs = (first,) + tuple(dsegs[1:])
            dh = None
            for i, group in enumerate(INPROJ_DH_GROUPS):
                dh = _mm_nt_sum(f"{tag}_inproj_dh{i}", [(dsegs[s], w_segs[s]) for s in group], dh,
                                h.dtype if i == len(INPROJ_DH_GROUPS) - 1 else F32)
            dx_norm, d_norm_g = vjp_norm(dh)
            return (projection, others), dx_skip + dx_norm, dict(d_rest_s, norm_g=d_norm_g)

        def landed(exchanging, after):
            projection, others = (_exchange_wait(f"exchange_{tag}_{k}_wait", e, after) for k, e in zip(("w_in", "others"), exchanging))
            return projection + others
        return y, backward, landed

    y0, backward0, landed0 = layer(0, {"w_in": w_in0}, small[0], x2, gathering0)
    gathered1 = dict(zip(names, _exchange_wait("gather_layer1_wait", gathering1, y0)))
    y1, backward1, landed1 = layer(1, gathered1, small[1], y0)
    loss_splat, dy = _loss_call(y1, target)
    loss = lax.psum(loss_splat[0, 0], ("x", "y", "c"))
    exchanging1, d_y0, g_small1 = backward1(dy)
    exchanging0, g_x, g_small0 = backward0(d_y0)

    state = lambda n, s: (view(w, n, s), view(m, n, s), view(v, n, s))
    parts1 = landed1(exchanging1, g_x)
    updated = {n: _adamw_layer(f"adamw_l1_{n}", parts1[i], *state(n, s), 1) for i, (n, s) in enumerate(SHARDED)}
    g_small = (g_small0, g_small1)
    sizes = [k for _ in range(DEPTH) for _, k in SMALL]
    g_vec = _pack_vectors([g_small[l][n] for l in range(DEPTH) for n, _ in SMALL])
    (parts_vec,) = _all_to_all([jnp.broadcast_to(g_vec[None], (N_DEV,) + g_vec.shape)], updated["w_in"][0])
    parts0 = landed0(exchanging0, parts_vec)

    vec = lambda t: _pack_vectors([t[n][l] for l in range(DEPTH) for n, _ in SMALL])[None]
    outs = {}
    for i, (n, s) in enumerate(SHARDED):
        res = _adamw_layer(f"adamw_l0_{n}", parts0[i], *state(n, s), 0, earlier=updated[n])
        outs[n] = [o.reshape(w[n].shape) for o in res]
    vec_outs = [_unpack_vectors(o, sizes) for o in _adamw_layer("adamw_vectors", parts_vec, vec(w), vec(m), vec(v), 0)]
    for i, (n, _) in enumerate(SMALL):
        outs[n] = [jnp.stack([o[l * len(SMALL) + i] for l in range(DEPTH)]) for o in vec_outs]
    return (loss, g_x[None], *[outs[n][k] for k in range(4) for n in WEIGHTS])
```

```python
import functools
import math

import jax
import jax.numpy as jnp
from jax import lax
from jax.experimental import pallas as pl
from jax.experimental.pallas import tpu as pltpu

F32 = jnp.float32
BF16 = jnp.bfloat16
HI = lax.Precision.HIGHEST
MESH = pl.DeviceIdType.MESH

N_DEV = 8
D_MODEL = 1024
DEPTH = 2
RMS_EPS = 1e-6
ROPE_THETA = 10000.0
LANES = 128
MLA_HEADS = 4
MLA_ROPE = 64
MLA_QK = 192
MLA_Q_RANK = 384
MLA_KV_RANK = 256
DN_HEADS = 4
DN_CHUNK = 64
DN_CONV = 4
DIL_HEADS = 12
DIL_GROUP_HEADS = 4
DIL_DILATIONS = (1, 4, 16)
DIL_BLOCK = 128
BRANCH_W = 512
IN_WIDTH = 11464
NEG = -1e30
VMEM_LIMIT = 56 * 1024 * 1024

ADAM_LR, ADAM_B1, ADAM_B2, ADAM_EPS, ADAM_WD, ADAM_STEP = 0.001, 0.9, 0.999, 1e-08, 0.01, 10

_SEG = {}
_off = 0
for _n, _w in (("q_lat", 384), ("c_kv", 256), ("k_pe", 64), ("z_a", 512), ("dn_qkv", 1536), ("dn_a", 4), ("dn_b", 4),
               ("z_b", 512), ("dil_qkv", 4608), ("z_c", 512), ("gate", 3072)):
    _SEG[_n] = (_off, _w)
    _off += _w
assert _off == IN_WIDTH


def _pcall(body, **kw):
    return pl.pallas_call(body, **kw)


def _params(sem=None):
    return pltpu.CompilerParams(dimension_semantics=sem, vmem_limit_bytes=VMEM_LIMIT)


def _tile(n, target, mult):
    t = (min(n, target) // mult) * mult
    while t >= mult:
        if n % t == 0:
            return t
        t -= mult
    return n


def _mm(name, a, b, mode, out_dtype=F32, acc=None, tm=1024, tn=512, tk=1024):
    if mode == "nn":
        (M, K), (_, N) = a.shape, b.shape
    elif mode == "nt":
        (M, K), (N, _) = a.shape, b.shape
    else:
        (K, M), (_, N) = a.shape, b.shape
    tm, tn, tk = _tile(M, tm, LANES), _tile(N, tn, LANES), _tile(K, tk, LANES)
    nk = K // tk
    dims = {"nn": (((1,), (0,)), ((), ())), "nt": (((1,), (1,)), ((), ())), "tn": (((0,), (0,)), ((), ()))}[mode]
    a_spec = pl.BlockSpec((tk, tm), lambda i, j, k: (k, i)) if mode == "tn" else pl.BlockSpec((tm, tk), lambda i, j, k: (i, k))
    b_spec = pl.BlockSpec((tn, tk), lambda i, j, k: (j, k)) if mode == "nt" else pl.BlockSpec((tk, tn), lambda i, j, k: (k, j))
    o_spec = pl.BlockSpec((tm, tn), lambda i, j, k: (i, j))
    has_acc = acc is not None

    def body(*refs):
        a_ref, b_ref = refs[:2]
        c_ref = refs[2] if has_acc else None
        o_ref = refs[3] if has_acc else refs[2]
        prod = lax.dot_general(a_ref[...].astype(BF16), b_ref[...].astype(BF16), dims, preferred_element_type=F32)
        if nk == 1:
            o_ref[...] = (prod + c_ref[...].astype(F32) if has_acc else prod).astype(out_dtype)
            return
        acc_ref = refs[-1]
        k = pl.program_id(2)

        @pl.when(k == 0)
        def _():
            acc_ref[...] = prod + c_ref[...].astype(F32) if has_acc else prod

        @pl.when(k > 0)
        def _():
            acc_ref[...] += prod

        @pl.when(k == nk - 1)
        def _():
            o_ref[...] = acc_ref[...].astype(out_dtype)

    ins = [a, b] + ([acc] if has_acc else [])
    in_specs = [a_spec, b_spec] + ([o_spec] if has_acc else [])
    return _pcall(body, name=name, grid=(M // tm, N // tn, nk), in_specs=in_specs, out_specs=o_spec,
                  out_shape=jax.ShapeDtypeStruct((M, N), out_dtype), scratch_shapes=[pltpu.VMEM((tm, tn), F32)] if nk > 1 else [],
                  compiler_params=_params(("parallel", "parallel", "arbitrary")))(*ins)


def _mm_nt_sum(name, pairs, acc, out_dtype, tm=512, tn=1024):
    M, N = pairs[0][0].shape[0], pairs[0][1].shape[0]
    tm, tn = _tile(M, tm, LANES), _tile(N, tn, LANES)
    n = len(pairs)
    has_acc = acc is not None
    o_spec = pl.BlockSpec((tm, tn), lambda i, j: (i, j))

    def body(*refs):
        total = refs[2 * n][...].astype(F32) if has_acc else None
        for a_ref, b_ref in zip(refs[:n], refs[n:2 * n]):
            prod = lax.dot_general(a_ref[...].astype(BF16), b_ref[...].astype(BF16), (((1,), (1,)), ((), ())), preferred_element_type=F32)
            total = prod if total is None else total + prod
        refs[-1][...] = total.astype(out_dtype)

    in_specs = ([pl.BlockSpec((tm, a.shape[1]), lambda i, j: (i, 0)) for a, _ in pairs]
                + [pl.BlockSpec((tn, b.shape[1]), lambda i, j: (j, 0)) for _, b in pairs] + ([o_spec] if has_acc else []))
    return _pcall(body, name=name, grid=(M // tm, N // tn), in_specs=in_specs, out_specs=o_spec,
                  out_shape=jax.ShapeDtypeStruct((M, N), out_dtype), compiler_params=_params(("parallel", "parallel")))(
        *[a for a, _ in pairs], *[b for _, b in pairs], *([acc] if has_acc else []))


def _mm_nn_many(name, a, bs, out_dtype, tm=1024):
    M, K = a.shape
    tm = _tile(M, tm, LANES)
    n = len(bs)

    def body(*refs):
        a_t = refs[0][...].astype(BF16)
        for b_ref, o_ref in zip(refs[1:1 + n], refs[1 + n:]):
            o_ref[...] = jnp.dot(a_t, b_ref[...].astype(BF16), preferred_element_type=F32).astype(out_dtype)

    return _pcall(body, name=name, grid=(M // tm,),
                  in_specs=[pl.BlockSpec((tm, K), lambda i: (i, 0))] + [pl.BlockSpec(b.shape, lambda i: (0, 0)) for b in bs],
                  out_specs=[pl.BlockSpec((tm, b.shape[1]), lambda i: (i, 0)) for b in bs],
                  out_shape=[jax.ShapeDtypeStruct((M, b.shape[1]), out_dtype) for b in bs],
                  compiler_params=_params(("parallel",)))(a, *bs)


def _mm_tn_many(name, a, bs, out_dtype, tk=1024):
    K, M = a.shape
    tk = _tile(K, tk, LANES)
    nk, n = K // tk, len(bs)

    def body(*refs):
        a_t = refs[0][...].astype(BF16)
        k = pl.program_id(0)
        for b_ref, o_ref, acc in zip(refs[1:1 + n], refs[1 + n:1 + 2 * n], refs[1 + 2 * n:]):
            prod = lax.dot_general(a_t, b_ref[...].astype(BF16), (((0,), (0,)), ((), ())), preferred_element_type=F32)

            @pl.when(k == 0)
            def _(acc=acc, prod=prod):
                acc[...] = prod

            @pl.when(k > 0)
            def _(acc=acc, prod=prod):
                acc[...] += prod

            @pl.when(k == nk - 1)
            def _(acc=acc, o_ref=o_ref):
                o_ref[...] = acc[...].astype(out_dtype)

    return _pcall(body, name=name, grid=(nk,),
                  in_specs=[pl.BlockSpec((tk, M), lambda k: (k, 0))] + [pl.BlockSpec((tk, b.shape[1]), lambda k: (k, 0)) for b in bs],
                  out_specs=[pl.BlockSpec((M, b.shape[1]), lambda k: (0, 0)) for b in bs],
                  out_shape=[jax.ShapeDtypeStruct((M, b.shape[1]), out_dtype) for b in bs],
                  scratch_shapes=[pltpu.VMEM((M, b.shape[1]), F32) for b in bs],
                  compiler_params=_params(("arbitrary",)))(a, *bs)


INPROJ_DH_GROUPS = ((0, 1, 2, 3, 5, 6, 8), (4,), (9,), (7,))
INPROJ_NARROW = INPROJ_DH_GROUPS[0]

INPROJ_DTYPE = BF16

INPROJ_TILES = {"nn": dict(tm=1024, tn=1536, tk=1024), "tn": dict(tm=1024, tn=768, tk=2048)}


def _make_multi_linear(name, n, out_dtype=F32):
    @jax.custom_vjp
    def op(h, ws):
        return tuple(_mm(f"{name}_fwd{i}", h, w, "nn", out_dtype=out_dtype) for i, w in enumerate(ws))

    def fwd(h, ws):
        return op(h, ws), (h, ws)

    def bwd(res, douts):
        h, ws = res
        dh = None
        for i, (w, d) in enumerate(zip(ws, douts)):
            dh = _mm(f"{name}_dh{i}", d, w, "nt", acc=dh, out_dtype=h.dtype if i == len(ws) - 1 else F32)
        dws = tuple(_mm(f"{name}_dw{i}", h, d, "tn", out_dtype=w.dtype) for i, (w, d) in enumerate(zip(ws, douts)))
        return dh, dws

    op.defvjp(fwd, bwd)
    return op


def _make_resid_linear(name):
    @jax.custom_vjp
    def op(x, a, w):
        return _mm(f"{name}_fwd", a, w, "nn", acc=x)

    def fwd(x, a, w):
        return op(x, a, w), (a, w)

    def bwd(res, dy):
        a, w = res
        return dy, _mm(f"{name}_da", dy, w, "nt", out_dtype=a.dtype), _mm(f"{name}_dw", a, dy, "tn", out_dtype=w.dtype)

    op.defvjp(fwd, bwd)
    return op


def _make_rowwise(name, f, tile):
    def specs(rows, aux, params, consts, t):
        row = [pl.BlockSpec((t, a.shape[1]), lambda i: (i, 0)) for a in (*rows, *aux)]
        full = [pl.BlockSpec(p.shape, lambda i: (0, 0)) for p in (*params, *consts)]
        return row, full

    def fwd_call(rows, aux, params, consts):
        S = rows[0].shape[0]
        t = min(tile, S)
        n_in = len(rows) + len(aux) + len(params) + len(consts)
        shp = lambda a: jax.ShapeDtypeStruct((t, a.shape[1]), a.dtype)
        outs = jax.eval_shape(f, *[shp(a) for a in (*rows, *aux)], *params, *consts)
        row_specs, full_specs = specs(rows, aux, params, consts, t)

        def body(*refs):
            res = f(*[r[...] for r in refs[:n_in]])
            for o_ref, o in zip(refs[n_in:], res):
                o_ref[...] = o

        return _pcall(body, name=f"{name}_fwd", grid=(S // t,), in_specs=row_specs + full_specs,
                      out_specs=[pl.BlockSpec((t, o.shape[1]), lambda i: (i, 0)) for o in outs],
                      out_shape=[jax.ShapeDtypeStruct((S, o.shape[1]), o.dtype) for o in outs],
                      compiler_params=_params(("parallel",)))(*rows, *aux, *params, *consts)

    def bwd_call(rows, aux, params, consts, douts):
        S = rows[0].shape[0]
        t = min(tile, S)
        nr, na, npar, nc, nd = len(rows), len(aux), len(params), len(consts), len(douts)
        row_specs, full_specs = specs(rows, aux, params, consts, t)

        def body(*refs):
            vals = [r[...] for r in refs[:nr + na + npar + nc]]
            rv, av = vals[:nr], vals[nr:nr + na]
            pv, cv = vals[nr + na:nr + na + npar], vals[nr + na + npar:]
            dv = tuple(r[...] for r in refs[nr + na + npar + nc:nr + na + npar + nc + nd])
            out_refs = refs[nr + na + npar + nc + nd:]
            _, vjp = jax.vjp(lambda *rp: f(*rp[:nr], *av, *rp[nr:], *cv), *rv, *pv)
            grads = vjp(dv)
            for o_ref, g in zip(out_refs[:nr], grads[:nr]):
                o_ref[...] = g
            first = pl.program_id(0) == 0
            for o_ref, g in zip(out_refs[nr:], grads[nr:]):
                @pl.when(first)
                def _(o_ref=o_ref):
                    o_ref[...] = jnp.zeros_like(o_ref)
                o_ref[...] += g

        res = _pcall(body, name=f"{name}_bwd", grid=(S // t,),
                     in_specs=row_specs + full_specs + [pl.BlockSpec((t, d.shape[1]), lambda i: (i, 0)) for d in douts],
                     out_specs=[pl.BlockSpec((t, a.shape[1]), lambda i: (i, 0)) for a in rows]
                     + [pl.BlockSpec(p.shape, lambda i: (0, 0)) for p in params],
                     out_shape=[jax.ShapeDtypeStruct(a.shape, a.dtype) for a in (*rows, *params)],
                     compiler_params=_params(("arbitrary",)))(*rows, *aux, *params, *consts, *douts)
        return tuple(res[:nr]), tuple(res[nr:])

    @jax.custom_vjp
    def op(rows, aux, params, consts):
        return tuple(fwd_call(rows, aux, params, consts))

    def fwd(rows, aux, params, consts):
        return op(rows, aux, params, consts), (rows, aux, params, consts)

    def bwd(res, douts):
        rows, aux, params, consts = res
        drows, dparams = bwd_call(rows, aux, params, consts, tuple(douts))
        zeros = lambda xs: tuple(jnp.zeros_like(a) for a in xs)
        return drows, zeros(aux), dparams, zeros(consts)

    op.defvjp(fwd, bwd)
    return op


@jax.custom_vjp
def _swap_halves(x):
    return pltpu.roll(x, LANES // 2, 1)


_swap_halves.defvjp(lambda x: (_swap_halves(x), None), lambda _, g: (_swap_halves(g),))


def _rope(x, cos_t, sin_t):
    return x * cos_t + _swap_halves(x) * sin_t


def _rms(x, g, n=None):
    n = x.shape[-1] if n is None else n
    return x * lax.rsqrt(jnp.sum(x * x, axis=-1, keepdims=True) * (1.0 / n) + RMS_EPS) * g


def _heads(x):
    return [x[:, i * LANES:(i + 1) * LANES] for i in range(x.shape[1] // LANES)]


def _cat(xs):
    return jnp.concatenate(xs, axis=1)


def _silu(x):
    return x * jax.nn.sigmoid(x)


def _f_norm(x, g):
    return _rms(x, g).astype(BF16), x


def _f_mla_a(q_lat, c_kv, kpe, cos_p, sin_p, qa_g, kva_g, kpe_g):
    q_lat, c_kv, kpe = (t.astype(F32) for t in (q_lat, c_kv, kpe))
    kp =_rope(_rms(kpe, kpe_g, MLA_ROPE), cos_p, sin_p)
    return _rms(q_lat, qa_g).astype(BF16), _rms(c_kv, kva_g).astype(BF16), _cat([kp] * MLA_HEADS)


def _f_mla_b(q8, kn_raw, cos_p, sin_p, qn_g, qp_g, kn_g):
    hs = _heads(q8)
    qn = _cat([_rms(h, qn_g) for h in hs[:MLA_HEADS]])
    qp = _cat([_rope(_rms(h, qp_g, MLA_ROPE), cos_p, sin_p) for h in hs[MLA_HEADS:]])
    kn = _cat([_rms(h, kn_g) for h in _heads(kn_raw)])
    return qn, qp, kn


def _softplus(x):
    return jnp.maximum(x, 0.0) + jnp.log(1.0 + jnp.exp(-jnp.abs(x)))


def _l2n(x):
    return x * lax.rsqrt(jnp.sum(x * x, axis=-1, keepdims=True) + 1e-6)


def _f_dn_pre(mixed, ab, alog_f, dtb_f, e_a, e_b):
    hs = _heads(mixed)
    q = _cat([_l2n(h) * (LANES ** -0.5) for h in hs[:DN_HEADS]])
    k = _cat([_l2n(h) for h in hs[DN_HEADS:2 * DN_HEADS]])
    v = _cat(hs[2 * DN_HEADS:])
    ab = ab.astype(F32)
    a_f = jnp.dot(ab, e_a, precision=HI, preferred_element_type=F32)
    b_f = jnp.dot(ab, e_b, precision=HI, preferred_element_type=F32)
    g = -jnp.exp(alog_f) * _softplus(a_f + dtb_f)
    return q, k, v, g, jax.nn.sigmoid(b_f)


def _f_dil_pre(qkv, cos_h, sin_h, q_g, k_g):
    hs = [h.astype(F32) for h in _heads(qkv)]
    q = [_rope(_rms(h, q_g), cos_h, sin_h) for h in hs[:DIL_HEADS]]
    k = [_rope(_rms(h, k_g), cos_h, sin_h) for h in hs[DIL_HEADS:2 * DIL_HEADS]]
    v = hs[2 * DIL_HEADS:]
    group = lambda xs, g: _cat(xs[g * DIL_GROUP_HEADS:(g + 1) * DIL_GROUP_HEADS])
    return tuple(group(xs, g) for xs in (q, k, v) for g in range(len(DIL_DILATIONS)))


def _f_merge_a(y_a, z_a, o_dn, z_b, o0, o1, o2, l0, l1, l2, z_c, out_g):
    z_a, z_b, z_c = (z.astype(F32) for z in (z_a, z_b, z_c))
    y_b = _cat([_rms(h, out_g) for h in _heads(o_dn)])
    os_, ls = [_heads(o) for o in (o0, o1, o2)], [_heads(l) for l in (l0, l1, l2)]
    y_c = []
    for j in range(DIL_GROUP_HEADS):
        l3 = [ls[g][j] for g in range(3)]
        m = jnp.maximum(jnp.maximum(l3[0], l3[1]), l3[2])
        e3 = [jnp.exp(l - m) for l in l3]
        den = e3[0] + e3[1] + e3[2]
        y_c.append(sum(e3[g] * os_[g][j] for g in range(3)) / den)
    return tuple(t.astype(BF16) for t in (y_a * _silu(z_a), y_b * _silu(z_b), _cat(y_c) * _silu(z_c)))


def _f_merge_b(b0, b1, b2, gl):
    gs = [jax.nn.sigmoid(gl[:, i * D_MODEL:(i + 1) * D_MODEL].astype(F32)) for i in range(3)]
    return ((gs[0] * b0 + gs[1] * b1 + gs[2] * b2).astype(BF16),)


def _rope_tables(pos, inv_sign):
    S = pos.shape[0]
    t = min(S, 1024)

    def body(p_ref, c_ref, cp, sp, ch, sh):
        p = p_ref[...].astype(F32)
        c = c_ref[...]
        ang_p, ang_h = p * c[0:1], p * c[2:3]
        cp[...] = jnp.cos(ang_p) * jnp.abs(c[1:2])
        sp[...] = jnp.sin(ang_p) * c[1:2]
        ch[...] = jnp.cos(ang_h)
        sh[...] = jnp.sin(ang_h) * c[3:4]

    row = pl.BlockSpec((t, LANES), lambda i: (i, 0))
    return _pcall(body, name="rope_tables", grid=(S // t,),
                  in_specs=[pl.BlockSpec((t, 1), lambda i: (i, 0)), pl.BlockSpec((4, LANES), lambda i: (0, 0))],
                  out_specs=[row] * 4, out_shape=[jax.ShapeDtypeStruct((S, LANES), F32)] * 4,
                  compiler_params=_params(("parallel",)))(pos, inv_sign)


def _rope_consts():
    half_p, half_h = MLA_ROPE // 2, LANES // 2
    inv_p = 1.0 / (ROPE_THETA ** (jnp.arange(0, MLA_ROPE, 2, dtype=F32) / MLA_ROPE))
    inv_h = 1.0 / (ROPE_THETA ** (jnp.arange(0, LANES, 2, dtype=F32) / LANES))
    z = jnp.zeros((half_p,), F32)
    o = jnp.ones((half_p,), F32)
    return jnp.stack([jnp.concatenate([inv_p, z, inv_p, z]), jnp.concatenate([-o, z, o, z]),
                      jnp.concatenate([inv_h, inv_h]), jnp.concatenate([-jnp.ones((half_h,), F32), jnp.ones((half_h,), F32)])])


def _shift_rows(x, s, up):
    n = x.shape[0]
    r = lax.broadcasted_iota(jnp.int32, x.shape, 0)
    if up:
        return jnp.where(r < n - s, pltpu.roll(x, n - s, 0), 0.0)
    return jnp.where(r >= s, pltpu.roll(x, s, 0), 0.0)


def _make_shift(s):
    @jax.custom_vjp
    def sh(x):
        return _shift_rows(x, s, False)

    sh.defvjp(lambda x: (sh(x), None), lambda _, g: (_shift_rows(g, s, True),))
    return sh


def _f_conv(x, w):
    x = x.astype(F32)
    y = x * w[DN_CONV - 1:DN_CONV]
    for j in range(DN_CONV - 1):
        y = y + _make_shift(DN_CONV - 1 - j)(x) * w[j:j + 1]
    return _silu(y)


def _make_conv(name):
    def call(x, w, dy=None):
        S, C = x.shape
        col = pl.BlockSpec((S, LANES), lambda i: (0, i))
        wsp = pl.BlockSpec((DN_CONV, LANES), lambda i: (0, i))
        if dy is None:
            def body(x_ref, w_ref, o_ref):
                o_ref[...] = _f_conv(x_ref[...], w_ref[...])
            return _pcall(body, name=f"{name}_fwd", grid=(C // LANES,), in_specs=[col, wsp], out_specs=col,
                          out_shape=jax.ShapeDtypeStruct(x.shape, F32), compiler_params=_params(("parallel",)))(x, w)

        def body(x_ref, w_ref, dy_ref, dx_ref, dw_ref):
            _, vjp = jax.vjp(_f_conv, x_ref[...], w_ref[...])
            dx_ref[...], dw_ref[...] = vjp(dy_ref[...])
        return _pcall(body, name=f"{name}_bwd", grid=(C // LANES,), in_specs=[col, wsp, col], out_specs=[col, wsp],
                      out_shape=[jax.ShapeDtypeStruct(x.shape, x.dtype), jax.ShapeDtypeStruct(w.shape, F32)],
                      compiler_params=_params(("parallel",)))(x, w, dy)

    @jax.custom_vjp
    def op(x, w):
        return call(x, w)

    op.defvjp(lambda x, w: (op(x, w), (x, w)), lambda res, dy: tuple(call(*res, dy)))
    return op


def _dot_nt(a, b):
    return lax.dot_general(a.astype(BF16), b.astype(BF16), (((1,), (1,)), ((), ())), preferred_element_type=F32)


def _dot_nn(a, b):
    return jnp.dot(a.astype(BF16), b.astype(BF16), preferred_element_type=F32)


def _dot_tn(a, b):
    return lax.dot_general(a.astype(BF16), b.astype(BF16), (((0,), (0,)), ((), ())), preferred_element_type=F32)


def _mla_scores(qn, qp, kn, kp, diagonal):
    scale = MLA_QK ** -0.5
    s = _dot_nt(qn * scale, kn) + _dot_nt(qp * scale, kp)
    if diagonal:
        r = lax.broadcasted_iota(jnp.int32, s.shape, 0)
        c = lax.broadcasted_iota(jnp.int32, s.shape, 1)
        s = jnp.where(c <= r, s, NEG)
    return s


def _on_causal_pairs(qi, ki, step):
    @pl.when(ki < qi)
    def _():
        step(False)

    @pl.when(ki == qi)
    def _():
        step(True)


def _causal_pairs(n, t, by_key, heads):
    pairs = [(q, k) for k in range(n) for q in range(k, n)] if by_key else [(q, k) for q in range(n) for k in range(q + 1)]
    qt, kt = (jnp.array([p[i] for p in pairs], jnp.int32) for i in (0, 1))
    return (qt, kt, pl.BlockSpec((t, heads * LANES), lambda h, p, qt_r, kt_r: (qt_r[p], h)),
            pl.BlockSpec((t, heads * LANES), lambda h, p, qt_r, kt_r: (kt_r[p], h)))


MLA_FWD_HEADS = 4
MLA_BWD_HEADS = 2


def _make_mla_attn(name):
    scale = MLA_QK ** -0.5

    def fwd_call(qn, qp, kn, kp, v):
        S = qn.shape[0]
        t = min(S, 512)
        n = S // t
        hp = MLA_FWD_HEADS
        qt, kt, qs, ks = _causal_pairs(n, t, False, hp)
        cols = [slice(j * LANES, (j + 1) * LANES) for j in range(hp)]

        def body(qt_r, kt_r, qn_r, qp_r, kn_r, kp_r, v_r, o_r, lse_r, m_s, l_s, acc_s):
            qi, ki = qt_r[pl.program_id(1)], kt_r[pl.program_id(1)]

            @pl.when(ki == 0)
            def _():
                m_s[...] = jnp.full_like(m_s, NEG)
                l_s[...] = jnp.zeros_like(l_s)
                acc_s[...] = jnp.zeros_like(acc_s)

            def step(diagonal):
                for c in cols:
                    s = _mla_scores(qn_r[:, c], qp_r[:, c], kn_r[:, c], kp_r[:, c], diagonal)
                    m_old = m_s[:, c]
                    m_new = jnp.maximum(m_old, jnp.max(s, axis=-1, keepdims=True))
                    p = jnp.exp(s - m_new[:, :1])
                    alpha = jnp.exp(m_old - m_new)
                    l_s[:, c] = alpha * l_s[:, c] + jnp.sum(p, axis=-1, keepdims=True)
                    acc_s[:, c] = alpha * acc_s[:, c] + _dot_nn(p, v_r[:, c])
                    m_s[:, c] = m_new
            _on_causal_pairs(qi, ki, step)

            @pl.when(ki == qi)
            def _():
                o_r[...] = acc_s[...] / l_s[...]
                lse_r[...] = m_s[...] + jnp.log(l_s[...])

        spec = pltpu.PrefetchScalarGridSpec(num_scalar_prefetch=2, grid=(MLA_HEADS // hp, qt.shape[0]), in_specs=[qs, qs, ks, ks, ks],
                                            out_specs=[qs, qs], scratch_shapes=[pltpu.VMEM((t, hp * LANES), F32)] * 3)
        return _pcall(body, name=f"{name}_fwd", grid_spec=spec, out_shape=[jax.ShapeDtypeStruct((S, MLA_HEADS * LANES), F32)] * 2,
                      compiler_params=_params(("parallel", "arbitrary")))(qt, kt, qn, qp, kn, kp, v)

    def bwd_call(qn, qp, kn, kp, v, o, lse, do):
        S = qn.shape[0]
        t = min(S, 512)
        n = S // t
        hp = MLA_BWD_HEADS
        qt, kt, qs, ks = _causal_pairs(n, t, True, hp)
        head = pl.BlockSpec((S, hp * LANES), lambda h, p, qt_r, kt_r: (0, h))
        n_pairs = qt.shape[0]
        cols = [slice(j * LANES, (j + 1) * LANES) for j in range(hp)]

        def body(qt_r, kt_r, qn_r, qp_r, kn_r, kp_r, v_r, o_r, lse_r, do_r, dqn_r, dqp_r, dkn_r, dkp_r, dv_r, dkn_s, dkp_s, dv_s, dl_s):
            pair = pl.program_id(1)
            qi, ki = qt_r[pair], kt_r[pair]
            rows = pl.ds(pl.multiple_of(qi * t, t), t)

            @pl.when(pair == 0)
            def _():
                dqn_r[...] = jnp.zeros_like(dqn_r)
                dqp_r[...] = jnp.zeros_like(dqp_r)

            @pl.when(ki == 0)
            def _():
                for c in cols:
                    dl_s[rows, c] = jnp.broadcast_to(jnp.sum(do_r[:, c] * o_r[:, c], axis=-1, keepdims=True), (t, LANES))

            @pl.when(qi == ki)
            def _():
                dkn_s[...] = jnp.zeros_like(dkn_s)
                dkp_s[...] = jnp.zeros_like(dkp_s)
                dv_s[...] = jnp.zeros_like(dv_s)

            def step(diagonal):
                for c in cols:
                    qn, qp, kn, kp, do = qn_r[:, c], qp_r[:, c], kn_r[:, c], kp_r[:, c], do_r[:, c]
                    p = jnp.exp(_mla_scores(qn, qp, kn, kp, diagonal) - lse_r[:, c][:, :1])
                    ds = p * (_dot_nt(do, v_r[:, c]) - dl_s[rows, c][:, :1])
                    dv_s[:, c] += _dot_tn(p, do)
                    dkn_s[:, c] += _dot_tn(ds, qn * scale)
                    dkp_s[:, c] += _dot_tn(ds, qp * scale)
                    dqn_r[rows, c] += _dot_nn(ds, kn)
                    dqp_r[rows, c] += _dot_nn(ds, kp)
            _on_causal_pairs(qi, ki, step)

            @pl.when(qi == n - 1)
            def _():
                dkn_r[...] = dkn_s[...]
                dkp_r[...] = dkp_s[...]
                dv_r[...] = dv_s[...]

            @pl.when(pair == n_pairs - 1)
            def _():
                dqn_r[...] = dqn_r[...] * scale
                dqp_r[...] = dqp_r[...] * scale

        spec = pltpu.PrefetchScalarGridSpec(num_scalar_prefetch=2, grid=(MLA_HEADS // hp, n_pairs), in_specs=[qs, qs, ks, ks, ks, qs, qs, qs],
                                            out_specs=[head, head, ks, ks, ks],
                                            scratch_shapes=[pltpu.VMEM((t, hp * LANES), F32)] * 3 + [pltpu.VMEM((S, hp * LANES), F32)])
        return _pcall(body, name=f"{name}_bwd", grid_spec=spec, out_shape=[jax.ShapeDtypeStruct((S, MLA_HEADS * LANES), F32)] * 5,
                      compiler_params=_params(("parallel", "arbitrary")))(qt, kt, qn, qp, kn, kp, v, o, lse, do)

    @jax.custom_vjp
    def op(qn, qp, kn, kp, v):
        return fwd_call(qn, qp, kn, kp, v)[0]

    def fwd(qn, qp, kn, kp, v):
        o, lse = fwd_call(qn, qp, kn, kp, v)
        return o, (qn, qp, kn, kp, v, o, lse)

    def bwd(res, do):
        return tuple(bwd_call(*res, do))

    op.defvjp(fwd, bwd)
    return op


def _dil_block(q, kp, kc, vp, vc, has_prev):
    scale = LANES ** -0.5
    r = lax.broadcasted_iota(jnp.int32, (DIL_BLOCK, 2 * DIL_BLOCK), 0)
    c = lax.broadcasted_iota(jnp.int32, (DIL_BLOCK, 2 * DIL_BLOCK), 1)
    valid = ((c < DIL_BLOCK) & (c >= r) & has_prev) | ((c >= DIL_BLOCK) & (c - DIL_BLOCK <= r))
    s = jnp.where(valid, _dot_nt(q * scale, jnp.concatenate([kp, kc], axis=0)), NEG)
    m = jnp.max(s, axis=-1, keepdims=True)
    e = jnp.exp(s - m)
    den = jnp.sum(e, axis=-1, keepdims=True)
    o = _dot_nn(e, jnp.concatenate([vp, vc], axis=0)) / den
    return o, jnp.broadcast_to(m + jnp.log(den), o.shape)


DIL_TILE_ROWS = (1024, 1024, 2048)
DIL_STEP_HEADS = 2


def _make_dil_attn(name, d, tile_rows):
    def call(q, k, v, cts=None):
        S = q.shape[0]
        span = DIL_BLOCK * d
        G = max(1, min(tile_rows, S) // span)
        n = S // (G * span)
        at = (lambda i: i) if cts is None else (lambda i: n - 1 - i)
        step_heads = DIL_STEP_HEADS if d == 1 else 1
        width = step_heads * LANES
        tile = pl.BlockSpec((G * span, width), lambda h, i: (at(i), h))
        before = pl.BlockSpec((span, width), lambda h, i: (jnp.maximum(at(i) * G - 1, 0), h))
        heads = [slice(h * LANES, (h + 1) * LANES) for h in range(step_heads)]

        def rows(r, j):
            return pl.ds(j * DIL_BLOCK, DIL_BLOCK) if d == 1 else pl.ds(r + j * span, DIL_BLOCK, stride=d)

        def over_residues(fn):
            if d == 1:
                fn(0)
            else:
                lax.fori_loop(0, d, lambda r, c: (fn(r), c)[1], 0)

        def block_inputs(r, j, c, q_r, kb_r, k_r, vb_r, v_r):
            kp = kb_r[rows(r, 0), c] if j == 0 else k_r[rows(r, j - 1), c]
            vp = vb_r[rows(r, 0), c] if j == 0 else v_r[rows(r, j - 1), c]
            return q_r[rows(r, j), c], kp, k_r[rows(r, j), c], vp, v_r[rows(r, j), c]

        if cts is None:
            def body(q_r, kb_r, k_r, vb_r, v_r, o_r, lse_r):
                first = at(pl.program_id(1)) * G

                def residue(r):
                    for j in range(G):
                        for c in heads:
                            o_r[rows(r, j), c], lse_r[rows(r, j), c] = _dil_block(
                                *block_inputs(r, j, c, q_r, kb_r, k_r, vb_r, v_r), first + j > 0)
                over_residues(residue)
            return _pcall(body, name=f"{name}_fwd", grid=(DIL_GROUP_HEADS // step_heads, n), in_specs=[tile, before, tile, before, tile],
                          out_specs=[tile, tile], out_shape=[jax.ShapeDtypeStruct(q.shape, F32)] * 2,
                          compiler_params=_params(("parallel", "parallel")))(q, k, k, v, v)

        def body(q_r, kb_r, k_r, vb_r, v_r, do_r, dl_r, dq_r, dk_r, dv_r, ck_s, cv_s):
            first = at(pl.program_id(1)) * G

            @pl.when(pl.program_id(1) == 0)
            def _():
                ck_s[...] = jnp.zeros_like(ck_s)
                cv_s[...] = jnp.zeros_like(cv_s)

            def residue(r):
                for c in heads:
                    owed = None
                    for j in range(G):
                        hp = first + j > 0
                        _, vjp = jax.vjp(lambda *a: _dil_block(*a, hp), *block_inputs(r, j, c, q_r, kb_r, k_r, vb_r, v_r))
                        dq, dkp, dkc, dvp, dvc = vjp((do_r[rows(r, j), c], dl_r[rows(r, j), c]))
                        dq_r[rows(r, j), c] = dq
                        if j == G - 1:
                            dkc, dvc = dkc + ck_s[rows(r, 0), c], dvc + cv_s[rows(r, 0), c]
                        dk_r[rows(r, j), c], dv_r[rows(r, j), c] = dkc, dvc
                        if j == 0:
                            owed = (dkp, dvp)
                        else:
                            dk_r[rows(r, j - 1), c] += dkp
                            dv_r[rows(r, j - 1), c] += dvp
                    ck_s[rows(r, 0), c], cv_s[rows(r, 0), c] = owed
            over_residues(residue)
        return _pcall(body, name=f"{name}_bwd", grid=(DIL_GROUP_HEADS // step_heads, n), in_specs=[tile, before, tile, before, tile, tile, tile],
                      out_specs=[tile] * 3, out_shape=[jax.ShapeDtypeStruct(q.shape, F32)] * 3,
                      scratch_shapes=[pltpu.VMEM((span, width), F32)] * 2,
                      compiler_params=_params(("parallel", "arbitrary")))(q, k, k, v, v, *cts)

    @jax.custom_vjp
    def op(q, k, v):
        return tuple(call(q, k, v))

    op.defvjp(lambda q, k, v: (op(q, k, v), (q, k, v)), lambda res, cts: tuple(call(*res, cts=cts)))
    return op


def _pdot(a, b, dims):
    return lax.dot_general(a, b, (dims, ((), ())), precision=lax.Precision.HIGH, preferred_element_type=F32)


DN_LOCAL_CHUNKS = 8
DN_SCAN_CHUNKS = 8


DN_BLOCK_HEADS = 4
DN_BLOCK = DN_BLOCK_HEADS * DN_CHUNK


def _inverse_cotangent(inv, d):
    return -_pdot(inv, _pdot(d, inv, ((1,), (1,))), ((0,), (0,)))


@jax.custom_vjp
def _unit_lower_inverse(a):
    n = a.shape[0]
    eye = (lax.broadcasted_iota(jnp.int32, (n, n), 0) == lax.broadcasted_iota(jnp.int32, (n, n), 1)).astype(F32)
    inv, pw = eye - a, a
    for level in range(5):
        dot = _pdot if level < 2 else (lambda x, y, dims: lax.dot_general(x, y, (dims, ((), ())), preferred_element_type=F32))
        pw = dot(pw, pw, ((1,), (0,)))
        inv = inv + dot(inv, pw, ((1,), (0,)))
    return inv


def _unit_lower_inverse_fwd(a):
    inv = _unit_lower_inverse(a)
    return inv, inv


_unit_lower_inverse.defvjp(_unit_lower_inverse_fwd, lambda inv, d: (_inverse_cotangent(inv, d),))


@jax.custom_vjp
def _known_inverse(a, inv):
    return inv


_known_inverse.defvjp(lambda a, inv: (inv, inv), lambda inv, d: (_inverse_cotangent(inv, d), jnp.zeros_like(inv)))


def _dn_local(q, k, v, g, b, known=None):
    C, R = DN_CHUNK, DN_BLOCK
    r = lax.broadcasted_iota(jnp.int32, (R, R), 0)
    c = lax.broadcasted_iota(jnp.int32, (R, R), 1)
    same_head = (r // C) == (c // C)
    incl, strict = same_head & (r >= c), same_head & (r > c)
    avg = jnp.full((R, LANES), 1.0 / LANES, F32)
    rc = lax.broadcasted_iota(jnp.int32, (C, C), 0) >= lax.broadcasted_iota(jnp.int32, (C, C), 1)
    gc_lanes = _pdot(rc.astype(F32), g, ((1,), (0,)))
    us, ws, qes, kds, qks, invs = [], [], [], [], [], []
    for first in range(0, DN_HEADS, DN_BLOCK_HEADS):
        stack = lambda x: jnp.concatenate(_heads(x)[first:first + DN_BLOCK_HEADS], axis=0)
        unstack = lambda x: [x[p * C:(p + 1) * C] for p in range(DN_BLOCK_HEADS)]
        gc, q_s, k_s, v_s, b_s = (stack(x) for x in (gc_lanes, q, k, v, b))
        gc_j = _pdot(avg, gc, ((1,), (1,)))
        decay = jnp.exp(jnp.where(incl, _cat([gc] * (R // LANES)) - gc_j, NEG))
        kb = k_s * b_s
        kk = _pdot(jnp.concatenate([kb, q_s], axis=0), k_s, ((1,), (1,)))
        a = jnp.where(strict, kk[:R] * decay, 0.0)
        inv = _unit_lower_inverse(a) if known is None else _known_inverse(a, known[len(invs) * R:(len(invs) + 1) * R])
        invs.append(inv)
        eg = jnp.exp(gc)
        uw = _pdot(inv, _cat([v_s * b_s, kb * eg]), ((1,), (0,)))
        g_last = jnp.concatenate([jnp.broadcast_to(x[C - 1:C], (C, LANES)) for x in unstack(gc)], axis=0)
        us += unstack(uw[:, :LANES])
        ws += unstack(uw[:, LANES:])
        qes += unstack(q_s * eg)
        kds += unstack(k_s * jnp.exp(g_last - gc))
        qks.append(kk[R:] * decay)
    egl = jnp.broadcast_to(jnp.exp(gc_lanes[C - 1:C]), (8, DN_HEADS * LANES))
    return _cat(us), _cat(ws), _cat(qes), _cat(kds), jnp.concatenate(qks, axis=0), egl, jnp.concatenate(invs, axis=0)


def _dn_scan(u, w, qe, kd, qk, egl, state):
    C = DN_CHUNK
    heads = [slice(h * LANES, (h + 1) * LANES) for h in range(DN_HEADS)]
    ws = [_pdot(jnp.concatenate([w[:, sl], qe[:, sl]], axis=0), state[sl, :], ((1,), (0,))) for sl in heads]
    v_new = [u[:, sl] - x[:C] for sl, x in zip(heads, ws)]
    local = []
    for i, first in enumerate(range(0, DN_HEADS, DN_BLOCK_HEADS)):
        y = _pdot(qk[i * DN_BLOCK:(i + 1) * DN_BLOCK], jnp.concatenate(v_new[first:first + DN_BLOCK_HEADS], axis=0), ((1,), (0,)))
        local += [y[p * C:(p + 1) * C] for p in range(DN_BLOCK_HEADS)]
    o = _cat([x[C:] + y for x, y in zip(ws, local)])
    states = [state[sl, :] * egl[0:1, sl] + _pdot(kd[:, sl], vn, ((0,), (0,))) for sl, vn in zip(heads, v_new)]
    return o, jnp.concatenate(states, axis=0)


def _make_delta_rule(name):
    W = DN_HEADS * LANES
    QK = DN_HEADS * DN_CHUNK

    def local_call(ins, cts=None):
        S = ins[0].shape[0]
        n = S // DN_CHUNK
        per = math.gcd(DN_LOCAL_CHUNKS, n)
        row = pl.BlockSpec((per * DN_CHUNK, W), lambda i: (i, 0))
        qkb = pl.BlockSpec((per * QK, DN_BLOCK), lambda i: (i, 0))
        eg = pl.BlockSpec((per, 8, W), lambda i: (i, 0, 0))
        rows = lambda j: slice(j * DN_CHUNK, (j + 1) * DN_CHUNK)
        qk_rows = lambda j: slice(j * QK, (j + 1) * QK)
        out_rows = [rows, rows, rows, rows, qk_rows]

        if cts is None:
            def body(*refs):
                for j in range(per):
                    res = _dn_local(*[r[rows(j), :] for r in refs[:5]])
                    for o_r, o, at_ in zip(refs[5:10], res[:5], out_rows):
                        o_r[at_(j), :] = o
                    refs[10][j] = res[5]
                    refs[11][qk_rows(j), :] = res[6]
            blockdiag = jax.ShapeDtypeStruct((n * QK, DN_BLOCK), F32)
            return _pcall(body, name=f"{name}_local_fwd", grid=(n // per,), in_specs=[row] * 5, out_specs=[row] * 4 + [qkb, eg, qkb],
                          out_shape=[jax.ShapeDtypeStruct((S, W), F32)] * 4 + [blockdiag, jax.ShapeDtypeStruct((n, 8, W), F32), blockdiag],
                          compiler_params=_params(("parallel",)))(*ins)

        def body(*refs):
            for j in range(per):
                known = refs[5][qk_rows(j), :]
                _, vjp = jax.vjp(lambda *a: _dn_local(*a, known=known)[:6], *[r[rows(j), :] for r in refs[:5]])
                grads = vjp(tuple(r[at_(j), :] for r, at_ in zip(refs[6:11], out_rows)) + (refs[11][j],))
                for o_r, o in zip(refs[12:], grads):
                    o_r[rows(j), :] = o
        return _pcall(body, name=f"{name}_local_bwd", grid=(n // per,), in_specs=[row] * 5 + [qkb] + [row] * 4 + [qkb, eg],
                      out_specs=[row] * 5, out_shape=[jax.ShapeDtypeStruct((S, W), F32)] * 5,
                      compiler_params=_params(("parallel",)))(*ins, *cts)

    def scan_call(ins, saved=None, do=None):
        S = ins[0].shape[0]
        n = S // DN_CHUNK
        per = math.gcd(DN_SCAN_CHUNKS, n)
        steps = n // per
        at = (lambda i: i) if do is None else (lambda i: steps - 1 - i)
        row = pl.BlockSpec((per * DN_CHUNK, W), lambda i: (at(i), 0))
        qkb = pl.BlockSpec((per * QK, DN_BLOCK), lambda i: (at(i), 0))
        eg = pl.BlockSpec((per, 8, W), lambda i: (at(i), 0, 0))
        st = pl.BlockSpec((per, W, LANES), lambda i: (at(i), 0, 0))
        rows = lambda j: slice(j * DN_CHUNK, (j + 1) * DN_CHUNK)
        qk_rows = lambda j: slice(j * QK, (j + 1) * QK)
        chunk = lambda refs, j: [r[rows(j), :] for r in refs[:4]] + [refs[4][qk_rows(j), :], refs[5][j]]

        if do is None:
            def body(*refs):
                o_r, st_r, s_s = refs[6:]

                @pl.when(pl.program_id(0) == 0)
                def _():
                    s_s[...] = jnp.zeros_like(s_s)
                state = s_s[...]
                for j in range(per):
                    st_r[j] = state
                    o_r[rows(j), :], state = _dn_scan(*chunk(refs, j), state)
                s_s[...] = state
            return _pcall(body, name=f"{name}_scan_fwd", grid=(steps,), in_specs=[row] * 4 + [qkb, eg], out_specs=[row, st],
                          out_shape=[jax.ShapeDtypeStruct((S, W), F32), jax.ShapeDtypeStruct((n, W, LANES), F32)],
                          scratch_shapes=[pltpu.VMEM((W, LANES), F32)], compiler_params=_params(("arbitrary",)))(*ins)

        def body(*refs):
            st_r, do_r = refs[6:8]
            outs, ds_s = refs[8:14], refs[14]

            @pl.when(pl.program_id(0) == 0)
            def _():
                ds_s[...] = jnp.zeros_like(ds_s)
            ds = ds_s[...]
            for j in reversed(range(per)):
                _, vjp = jax.vjp(_dn_scan, *chunk(refs, j), st_r[j])
                *grads, ds = vjp((do_r[rows(j), :], ds))
                for o_r, gval in zip(outs[:4], grads[:4]):
                    o_r[rows(j), :] = gval
                outs[4][qk_rows(j), :] = grads[4]
                outs[5][j] = grads[5]
            ds_s[...] = ds
        return _pcall(body, name=f"{name}_scan_bwd", grid=(steps,), in_specs=[row] * 4 + [qkb, eg, st, row], out_specs=[row] * 4 + [qkb, eg],
                      out_shape=[jax.ShapeDtypeStruct((S, W), F32)] * 4
                      + [jax.ShapeDtypeStruct((n * QK, DN_BLOCK), F32), jax.ShapeDtypeStruct((n, 8, W), F32)],
                      scratch_shapes=[pltpu.VMEM((W, LANES), F32)], compiler_params=_params(("arbitrary",)))(*ins, saved, do)

    @jax.custom_vjp
    def local(q, k, v, g, b):
        return tuple(local_call((q, k, v, g, b))[:6])

    def local_fwd(*a):
        *outs, inverses = local_call(a)
        return tuple(outs), (*a, inverses)

    local.defvjp(local_fwd, lambda res, cts: tuple(local_call(res, tuple(cts))))

    @jax.custom_vjp
    def scan(u, w, qe, kd, qk, egl):
        return scan_call((u, w, qe, kd, qk, egl))[0]

    def scan_fwd(*a):
        o, states = scan_call(a)
        return o, (a, states)

    scan.defvjp(scan_fwd, lambda res, do: tuple(scan_call(res[0], res[1], do)))
    return lambda q, k, v, g, b: scan(*local(q, k, v, g, b))


def _loss_call(y, target):
    S, D = y.shape
    t = min(S, 512)
    n = S // t
    row = pl.BlockSpec((t, D), lambda i: (i, 0))

    def body(y_r, t_r, loss_r, dy_r, acc_s):
        i = pl.program_id(0)

        @pl.when(i == 0)
        def _():
            acc_s[...] = jnp.zeros_like(acc_s)
        err = y_r[...] - t_r[...]
        dy_r[...] = err * (1.0 / D)
        acc_s[...] += jnp.sum(err * err, axis=0, keepdims=True)

        @pl.when(i == n - 1)
        def _():
            loss_r[...] = jnp.broadcast_to(jnp.sum(acc_s[...], axis=1, keepdims=True) * (0.5 / D), loss_r.shape)

    return _pcall(body, name="loss_head", grid=(n,), in_specs=[row, row],
                  out_specs=[pl.BlockSpec((8, LANES), lambda i: (0, 0)), row],
                  out_shape=[jax.ShapeDtypeStruct((8, LANES), F32), jax.ShapeDtypeStruct((S, D), F32)],
                  scratch_shapes=[pltpu.VMEM((1, D), F32)], compiler_params=_params(("arbitrary",)))(y, target)


def _adamw_layer(name, part, w, m, v, layer, earlier=None, rows=128):
    L, R, C = w.shape
    t = _tile(R, rows, 8)
    row = pl.BlockSpec((None, t, C), lambda i: (layer, i, 0))

    def body(p_r, w_r, m_r, v_r, *rest):
        g_r, d_r, nm_r, nv_r = rest[-4:]
        g = p_r[0].astype(F32)
        for s in range(1, N_DEV):
            g = g + p_r[s].astype(F32)
        m_new = ADAM_B1 * m_r[...] + (1.0 - ADAM_B1) * g
        v_new = ADAM_B2 * v_r[...] + (1.0 - ADAM_B2) * (g * g)
        m_hat = m_new / (1.0 - ADAM_B1 ** ADAM_STEP)
        v_hat = v_new / (1.0 - ADAM_B2 ** ADAM_STEP)
        g_r[...] = g
        d_r[...] = -ADAM_LR * (m_hat / (jnp.sqrt(v_hat) + ADAM_EPS) + ADAM_WD * w_r[...])
        nm_r[...] = m_new
        nv_r[...] = v_new

    extra = [] if earlier is None else list(earlier)
    return _pcall(body, name=name, grid=(R // t,),
                  in_specs=[pl.BlockSpec((N_DEV, t, C), lambda i: (0, i, 0)), row, row, row] + [pl.BlockSpec(memory_space=pl.ANY)] * len(extra),
                  out_specs=[row] * 4, out_shape=[jax.ShapeDtypeStruct((L, R, C), F32)] * 4,
                  input_output_aliases={4 + k: k for k in range(len(extra))},
                  compiler_params=_params(("parallel",)))(part, w, m, v, *extra)


def _my_place():
    x, y, c = lax.axis_index("x"), lax.axis_index("y"), lax.axis_index("c")
    return x, y, c


def _index(x, y, c):
    return 4 * x + 2 * y + c


def _all_gather(vs):
    n = len(vs)

    def body(*refs):
        v_refs, out_refs = refs[:n], refs[n:2 * n]
        send_sems, recv_sems, local_sems = refs[2 * n:]
        x, y, c = _my_place()
        me, sibling = (x, y, c), (x, y, 1 - c)
        chips = [(1 - x, y), (x, 1 - y), (1 - x, 1 - y)]

        def copy(a, k, block, to, src=None):
            rows = out_refs[a].at[_index(*block)]
            return pltpu.make_async_remote_copy(src_ref=rows if src is None else src, dst_ref=rows, send_sem=send_sems.at[a, k],
                                                recv_sem=recv_sems.at[a, k], device_id=to, device_id_type=MESH)

        mine = [pltpu.make_async_copy(v_refs[a], out_refs[a].at[_index(*me)], local_sems.at[a]) for a in range(n)]
        first, passed = [], []
        for a in range(n):
            mine[a].start()
            first += [copy(a, 0, me, sibling, src=v_refs[a])]
            first += [copy(a, 1 + j, me, (*chip, c), src=v_refs[a]) for j, chip in enumerate(chips)]
        for cp in first:
            cp.start()
        for j, chip in enumerate(chips):
            for a in range(n):
                copy(a, 1 + j, (*chip, c), me).wait_recv()
                passed.append(copy(a, 4 + j, (*chip, c), sibling))
                passed[-1].start()
        for a in range(n):
            copy(a, 0, sibling, me).wait_recv()
            for j, chip in enumerate(chips):
                copy(a, 4 + j, (*chip, 1 - c), me).wait_recv()
        for cp in first + passed:
            cp.wait_send()
        for a in range(n):
            mine[a].wait()

    any_ = pl.BlockSpec(memory_space=pl.ANY)
    return _pcall(body, name="gather_weights", in_specs=[any_] * n, out_specs=[any_] * n,
                  out_shape=[jax.ShapeDtypeStruct((N_DEV,) + v.shape, v.dtype) for v in vs],
                  scratch_shapes=[pltpu.SemaphoreType.DMA((n, 7)), pltpu.SemaphoreType.DMA((n, 7)), pltpu.SemaphoreType.DMA((n,))])(*vs)


def _all_to_all(vs, after):
    n = len(vs)

    def body(*refs):
        v_refs, out_refs = refs[:n], refs[n + 1:2 * n + 1]
        send_sems, recv_sems, local_sems = refs[2 * n + 1:]
        x, y, c = _my_place()
        me = _index(x, y, c)
        mine = [pltpu.make_async_copy(v_refs[a].at[me], out_refs[a].at[me], local_sems.at[a]) for a in range(n)]
        copies = []
        for a in range(n):
            mine[a].start()
        for k in range(1, N_DEV):
            px = 1 - x if k & 4 else x
            py = 1 - y if k & 2 else y
            pc = 1 - c if k & 1 else c
            for a in range(n):
                cp = pltpu.make_async_remote_copy(src_ref=v_refs[a].at[_index(px, py, pc)], dst_ref=out_refs[a].at[me],
                                                  send_sem=send_sems.at[a, k - 1], recv_sem=recv_sems.at[a, k - 1],
                                                  device_id=(px, py, pc), device_id_type=MESH)
                cp.start()
                copies.append(cp)
        for cp in copies:
            cp.wait()
        for a in range(n):
            mine[a].wait()

    any_ = pl.BlockSpec(memory_space=pl.ANY)
    return _pcall(body, name="exchange_vectors", in_specs=[any_] * (n + 1), out_specs=[any_] * n,
                  out_shape=[jax.ShapeDtypeStruct(v.shape, v.dtype) for v in vs],
                  scratch_shapes=[pltpu.SemaphoreType.DMA((n, 7)), pltpu.SemaphoreType.DMA((n, 7)), pltpu.SemaphoreType.DMA((n,))])(*vs, after)


_HBM = pl.BlockSpec(memory_space=pltpu.HBM)
_SEM = pl.BlockSpec(memory_space=pltpu.SEMAPHORE)
_EFFECT = pltpu.SideEffectType.DATAFLOW_SIDE_EFFECTING


def _direct_copies(gather, v_refs, land_refs, send_sems, recv_sems, local_sems):
    x, y, c = _my_place()
    me = _index(x, y, c)
    local, remote = [], []
    for a, (v_ref, land_ref) in enumerate(zip(v_refs, land_refs)):
        local.append(pltpu.make_async_copy(v_ref if gather else v_ref.at[me], land_ref.at[me], local_sems.at[a]))
    for k in range(1, N_DEV):
        px = 1 - x if k & 4 else x
        py = 1 - y if k & 2 else y
        pc = 1 - c if k & 1 else c
        for a, (v_ref, land_ref) in enumerate(zip(v_refs, land_refs)):
            sem = a * (N_DEV - 1) + k - 1
            remote.append(pltpu.make_async_remote_copy(
                src_ref=v_ref if gather else v_ref.at[_index(px, py, pc)], dst_ref=land_ref.at[me], send_sem=send_sems.at[sem],
                recv_sem=recv_sems.at[sem], device_id=(px, py, pc), device_id_type=MESH))
    return local, remote


def _exchange_start(name, vs, gather, thru):
    n = len(vs)
    lands = [lax.empty((N_DEV,) + v.shape if gather else v.shape, v.dtype) for v in vs]

    def body(*refs):
        v_refs, land_refs = refs[:n], refs[n:2 * n]
        send_sems, recv_sems, local_sems = refs[2 * n + 1:2 * n + 4]
        local, remote = _direct_copies(gather, v_refs, land_refs, send_sems, recv_sems, local_sems)
        for cp in local + remote:
            cp.start()

    hbm = lambda a: pltpu.HBM(a.shape, a.dtype)
    res = _pcall(body, name=name,
                 out_shape=(pltpu.SemaphoreType.DMA((n * (N_DEV - 1),)), pltpu.SemaphoreType.DMA((n * (N_DEV - 1),)), pltpu.SemaphoreType.DMA((n,)),
                            *[hbm(a) for a in (*vs, *lands, thru)]),
                 in_specs=[_HBM] * (2 * n + 1), out_specs=(_SEM, _SEM, _SEM, *[_HBM] * (2 * n + 1)),
                 input_output_aliases={i: 3 + i for i in range(2 * n + 1)},
                 compiler_params=pltpu.CompilerParams(has_side_effects=_EFFECT))(
        *[pltpu.with_memory_space_constraint(a, pltpu.HBM) for a in (*vs, *lands, thru)])
    return (gather, res[:3], res[3:3 + n], res[3 + n:3 + 2 * n]), res[3 + 2 * n]


def _exchange_wait(name, started, after):
    gather, sems, vs, lands = started
    n = len(vs)

    def body(*refs):
        v_refs, land_refs = refs[:n], refs[n:2 * n]
        send_sems, recv_sems, local_sems = refs[2 * n:2 * n + 3]
        local, remote = _direct_copies(gather, v_refs, land_refs, send_sems, recv_sems, local_sems)
        for cp in local:
            cp.wait()
        for cp in remote:
            cp.wait_send()
            cp.wait_recv()

    hbm = lambda a: pltpu.HBM(a.shape, a.dtype)
    res = _pcall(body, name=name, out_shape=tuple(hbm(a) for a in (*vs, *lands)),
                 in_specs=[_HBM] * (2 * n) + [_SEM] * 3 + [pl.BlockSpec(memory_space=pl.ANY)], out_specs=tuple([_HBM] * (2 * n)),
                 input_output_aliases={i: i for i in range(2 * n)},
                 compiler_params=pltpu.CompilerParams(has_side_effects=_EFFECT))(*vs, *lands, *sems, after)
    return list(res[n:])


W_IN_SHARD = IN_WIDTH // N_DEV
SEG_ORDER = ("q_lat", "c_kv", "k_pe", "z_a", "dn_qkv", "dn_ab", "z_b", "dil_qkv", "z_c", "gate")
SEG_WIDTH = (384, 256, LANES, 512, 1536, LANES, 512, 4608, 512, 3072)


def _w_in_plan():
    plan = []

    def add(seg, c0, c1, dst):
        while c0 < c1:
            d = c0 // W_IN_SHARD
            e = min(c1, (d + 1) * W_IN_SHARD)
            plan.append((seg, dst, d, c0 - d * W_IN_SHARD, e - c0))
            dst += e - c0
            c0 = e

    half = MLA_ROPE // 2
    for i, name in enumerate(SEG_ORDER):
        if name == "k_pe":
            o = _SEG["k_pe"][0]
            add(i, o, o + half, 0)
            add(i, o + half, o + 2 * half, LANES // 2)
        elif name == "dn_ab":
            o = _SEG["dn_a"][0]
            add(i, o, o + 2 * DN_HEADS, 0)
        else:
            o, w = _SEG[name]
            add(i, o, o + w, 0)
    return plan


def _make_w_in_segments(name):
    plan = _w_in_plan()
    nseg = len(SEG_ORDER)
    t = 256

    def fwd_call(g):
        L = g.shape[1]

        def body(g_ref, *o_refs):
            for i in (SEG_ORDER.index("k_pe"), SEG_ORDER.index("dn_ab")):
                o_refs[i][...] = jnp.zeros_like(o_refs[i])
            for seg, dst, d, src, n in plan:
                o_refs[seg][:, dst:dst + n] = g_ref[d, :, src:src + n]

        return _pcall(body, name=f"{name}_fwd", grid=(L, D_MODEL // t),
                      in_specs=[pl.BlockSpec((N_DEV, None, t, W_IN_SHARD), lambda l, i: (0, l, i, 0))],
                      out_specs=[pl.BlockSpec((None, t, w), lambda l, i: (l, i, 0)) for w in SEG_WIDTH],
                      out_shape=[jax.ShapeDtypeStruct((L, D_MODEL, w), g.dtype) for w in SEG_WIDTH],
                      compiler_params=_params(("parallel", "parallel")))(g)

    def bwd_call(ds):
        L = ds[0].shape[0]

        def body(*refs):
            d_refs, g_ref = refs[:nseg], refs[nseg]
            for seg, dst, d, src, n in plan:
                g_ref[d, :, src:src + n] = d_refs[seg][:, dst:dst + n]

        return _pcall(body, name=f"{name}_bwd", grid=(L, D_MODEL // t),
                      in_specs=[pl.BlockSpec((None, t, w), lambda l, i: (l, i, 0)) for w in SEG_WIDTH],
                      out_specs=pl.BlockSpec((N_DEV, None, t, W_IN_SHARD), lambda l, i: (0, l, i, 0)),
                      out_shape=jax.ShapeDtypeStruct((N_DEV, L, D_MODEL, W_IN_SHARD), ds[0].dtype),
                      compiler_params=_params(("parallel", "parallel")))(*ds)

    @jax.custom_vjp
    def op(g):
        return tuple(fwd_call(g))

    op.defvjp(lambda g: (op(g), None), lambda _, ds: (bwd_call(tuple(ds)),))
    return op


def _pe_pad(a):
    h = MLA_ROPE // 2
    z = jnp.zeros(a.shape[:-1] + (h,), a.dtype)
    return jnp.concatenate([a[..., :h], z, a[..., h:], z], axis=-1)


def _layer_norm(tag, x, norm_g):
    return _make_rowwise(f"{tag}_norm", _f_norm, 512)((x,), (), (norm_g[None, :],), ())


def _layer(tag, x, tables, W):
    h, x = _layer_norm(tag, x, W["norm_g"])
    return _layer_tail(tag, x, _make_multi_linear(f"{tag}_inproj", 10, INPROJ_DTYPE)(h, W["w_in_segments"]), tables, W)


def _layer_tail(tag, x, segments, tables, W):
    cos_p, sin_p, cos_h, sin_h = tables
    row = lambda a: a[None, :]
    q_lat, c_kv, kpe, z_a, dn_qkv, ab, z_b, dil_qkv, z_c, gl = segments

    qn_lat, ckvn, kp = _make_rowwise(f"{tag}_mla_a", _f_mla_a, 512)(
        (q_lat, c_kv, kpe), (cos_p, sin_p),
        (row(W["mla_q_a_norm_g"]), row(W["mla_kv_a_norm_g"]), row(_pe_pad(W["mla_k_norm_g"][LANES:]))), ())
    wq = W["mla_w_q_b"].reshape(MLA_Q_RANK, MLA_HEADS, MLA_QK)
    wq = jnp.concatenate([wq[:, :, :LANES].reshape(MLA_Q_RANK, -1), _pe_pad(wq[:, :, LANES:]).reshape(MLA_Q_RANK, -1)], axis=1)
    wkv = W["mla_w_kv_b"].reshape(MLA_KV_RANK, MLA_HEADS, 2 * LANES)
    (q8,) = _make_multi_linear(f"{tag}_qb", 1)(qn_lat, (wq,))
    kn_raw, v_mla = _make_multi_linear(f"{tag}_kvb", 2)(
        ckvn, (wkv[:, :, :LANES].reshape(MLA_KV_RANK, -1), wkv[:, :, LANES:].reshape(MLA_KV_RANK, -1)))
    qn, qp, kn = _make_rowwise(f"{tag}_mla_b", _f_mla_b, 512)(
        (q8, kn_raw), (cos_p, sin_p),
        (row(W["mla_q_norm_g"][:LANES]), row(_pe_pad(W["mla_q_norm_g"][LANES:])), row(W["mla_k_norm_g"][:LANES])), ())
    y_a = _make_mla_attn(f"{tag}_mla")(qn, qp, kn, kp, v_mla)

    mixed = _make_conv(f"{tag}_conv")(dn_qkv, W["dn_conv_w"])
    lane_head = jnp.arange(DN_HEADS * LANES) // LANES
    e_a = (jnp.arange(LANES)[:, None] == lane_head[None, :]).astype(F32)
    e_b = (jnp.arange(LANES)[:, None] == lane_head[None, :] + DN_HEADS).astype(F32)
    q_dn, k_dn, v_dn, g_dn, b_dn = _make_rowwise(f"{tag}_dn_pre", _f_dn_pre, 512)(
        (mixed, ab), (), (row(jnp.repeat(W["dn_a_log"], LANES)), row(jnp.repeat(W["dn_dt_bias"], LANES))), (e_a, e_b))
    o_dn = _make_delta_rule(f"{tag}_dn")(q_dn, k_dn, v_dn, g_dn, b_dn)

    qkv_dil = _make_rowwise(f"{tag}_dil_pre", _f_dil_pre, 256)(
        (dil_qkv,), (cos_h, sin_h), (row(W["dil_q_norm_g"]), row(W["dil_k_norm_g"])), ())
    n_groups = len(DIL_DILATIONS)
    o_lse = [_make_dil_attn(f"{tag}_dil{g}", d, DIL_TILE_ROWS[g])(qkv_dil[g], qkv_dil[n_groups + g], qkv_dil[2 * n_groups + g])
             for g, d in enumerate(DIL_DILATIONS)]

    ya, yb, yc = _make_rowwise(f"{tag}_merge_a", _f_merge_a, 256)(
        (y_a, z_a, o_dn, z_b, *[o for o, _ in o_lse], *[l for _, l in o_lse], z_c), (), (row(W["dn_out_norm_g"]),), ())
    (b0,) = _make_multi_linear(f"{tag}_br0", 1)(ya, (W["w_branch"][0],))
    (b1,) = _make_multi_linear(f"{tag}_br1", 1)(yb, (W["w_branch"][1],))
    (b2,) = _make_multi_linear(f"{tag}_br2", 1)(yc, (W["w_branch"][2],))
    (mix,) = _make_rowwise(f"{tag}_merge_b", _f_merge_b, 256)((b0, b1, b2, gl), (), (), ())
    return _make_resid_linear(f"{tag}_out")(x, mix, W["w_out"])


SHARDED = (("w_in", (D_MODEL, W_IN_SHARD)), ("mla_w_q_b", (MLA_Q_RANK, MLA_HEADS * MLA_QK // N_DEV)),
           ("mla_w_kv_b", (MLA_KV_RANK, MLA_HEADS * 2 * LANES // N_DEV)), ("w_branch", (3 * BRANCH_W, D_MODEL // N_DEV)),
           ("w_out", (D_MODEL // N_DEV, D_MODEL)), ("dn_conv_w", (DN_CONV, 3 * DN_HEADS * LANES // N_DEV)))
SMALL = (("norm_g", D_MODEL), ("mla_q_a_norm_g", MLA_Q_RANK), ("mla_kv_a_norm_g", MLA_KV_RANK), ("mla_q_norm_g", MLA_QK),
         ("mla_k_norm_g", MLA_QK), ("dn_a_log", DN_HEADS), ("dn_dt_bias", DN_HEADS), ("dn_out_norm_g", LANES),
         ("dil_q_norm_g", LANES), ("dil_k_norm_g", LANES))
WEIGHTS = ("norm_g", "w_in", "mla_q_a_norm_g", "mla_w_q_b", "mla_kv_a_norm_g", "mla_w_kv_b", "mla_q_norm_g", "mla_k_norm_g",
           "dn_conv_w", "dn_a_log", "dn_dt_bias", "dn_out_norm_g", "dil_q_norm_g", "dil_k_norm_g", "w_branch", "w_out")


def _round_up(n, m):
    return -(-n // m) * m


def _pack_vectors(pieces):
    return jnp.concatenate([jnp.pad(p, (0, _round_up(p.shape[0], LANES) - p.shape[0])) for p in pieces]).reshape(-1, LANES)


def _unpack_vectors(flat, sizes):
    out, off = [], 0
    flat = flat.reshape(-1)
    for n in sizes:
        out.append(flat[off:off + n])
        off += _round_up(n, LANES)
    return out


def _whole_weights(g, small):
    W = dict(small)
    W["mla_w_q_b"] = g["mla_w_q_b"].transpose(1, 0, 2).reshape(MLA_Q_RANK, -1)
    W["mla_w_kv_b"] = g["mla_w_kv_b"].transpose(1, 0, 2).reshape(MLA_KV_RANK, -1)
    W["w_branch"] = g["w_branch"].reshape(N_DEV, 3, BRANCH_W, -1).transpose(1, 2, 0, 3).reshape(3, BRANCH_W, D_MODEL)
    W["w_out"] = g["w_out"].reshape(D_MODEL, D_MODEL)
    W["dn_conv_w"] = g["dn_conv_w"].transpose(1, 0, 2).reshape(DN_CONV, -1)
    return W


def kernel(x, positions, norm_g, w_in, mla_q_a_norm_g, mla_w_q_b, mla_kv_a_norm_g, mla_w_kv_b, mla_q_norm_g, mla_k_norm_g, dn_conv_w, dn_a_log, dn_dt_bias, dn_out_norm_g, dil_q_norm_g, dil_k_norm_g, w_branch, w_out, loss_target, m_norm_g, m_w_in, m_mla_q_a_norm_g, m_mla_w_q_b, m_mla_kv_a_norm_g, m_mla_w_kv_b, m_mla_q_norm_g, m_mla_k_norm_g, m_dn_conv_w, m_dn_a_log, m_dn_dt_bias, m_dn_out_norm_g, m_dil_q_norm_g, m_dil_k_norm_g, m_w_branch, m_w_out, v_norm_g, v_w_in, v_mla_q_a_norm_g, v_mla_w_q_b, v_mla_kv_a_norm_g, v_mla_w_kv_b, v_mla_q_norm_g, v_mla_k_norm_g, v_dn_conv_w, v_dn_a_log, v_dn_dt_bias, v_dn_out_norm_g, v_dil_q_norm_g, v_dil_k_norm_g, v_w_branch, v_w_out):
    w = dict(norm_g=norm_g, w_in=w_in, mla_q_a_norm_g=mla_q_a_norm_g, mla_w_q_b=mla_w_q_b, mla_kv_a_norm_g=mla_kv_a_norm_g,
             mla_w_kv_b=mla_w_kv_b, mla_q_norm_g=mla_q_norm_g, mla_k_norm_g=mla_k_norm_g, dn_conv_w=dn_conv_w, dn_a_log=dn_a_log,
             dn_dt_bias=dn_dt_bias, dn_out_norm_g=dn_out_norm_g, dil_q_norm_g=dil_q_norm_g, dil_k_norm_g=dil_k_norm_g,
             w_branch=w_branch, w_out=w_out)
    m = dict(norm_g=m_norm_g, w_in=m_w_in, mla_q_a_norm_g=m_mla_q_a_norm_g, mla_w_q_b=m_mla_w_q_b, mla_kv_a_norm_g=m_mla_kv_a_norm_g,
             mla_w_kv_b=m_mla_w_kv_b, mla_q_norm_g=m_mla_q_norm_g, mla_k_norm_g=m_mla_k_norm_g, dn_conv_w=m_dn_conv_w,
             dn_a_log=m_dn_a_log, dn_dt_bias=m_dn_dt_bias, dn_out_norm_g=m_dn_out_norm_g, dil_q_norm_g=m_dil_q_norm_g,
             dil_k_norm_g=m_dil_k_norm_g, w_branch=m_w_branch, w_out=m_w_out)
    v = dict(norm_g=v_norm_g, w_in=v_w_in, mla_q_a_norm_g=v_mla_q_a_norm_g, mla_w_q_b=v_mla_w_q_b, mla_kv_a_norm_g=v_mla_kv_a_norm_g,
             mla_w_kv_b=v_mla_w_kv_b, mla_q_norm_g=v_mla_q_norm_g, mla_k_norm_g=v_mla_k_norm_g, dn_conv_w=v_dn_conv_w,
             dn_a_log=v_dn_a_log, dn_dt_bias=v_dn_dt_bias, dn_out_norm_g=v_dn_out_norm_g, dil_q_norm_g=v_dil_q_norm_g,
             dil_k_norm_g=v_dil_k_norm_g, w_branch=v_w_branch, w_out=v_w_out)
    x2, target = x[0], loss_target[0]
    pos = positions[0][:, None]

    names = [n for n, _ in SHARDED]
    view = lambda t, n, s: t[n].reshape((DEPTH,) + s)
    shards = [[(view(w, n, s) if n == "dn_conv_w" else view(w, n, s).astype(BF16))[l] for n, s in SHARDED] for l in range(DEPTH)]
    small = [{n: w[n][l] for n, _ in SMALL} for l in range(DEPTH)]
    (w_in0,) = _all_gather(shards[0][:1])
    gathering0, w_in0 = _exchange_start("gather_layer0_others_start", shards[0][1:], True, w_in0)
    gathering1, w_in0 = _exchange_start("gather_layer1_start", shards[1], True, w_in0)
    tables = _rope_tables(pos, _rope_consts())

    def layer(l, g, small_l, x_l, pending=None):
        tag = f"l{l}"
        seg_op = _make_w_in_segments(f"{tag}_w_in_segments")
        w_segs, vjp_segs = jax.vjp(lambda gw: tuple(s[0] for s in seg_op(gw[:, None])), g["w_in"])
        (h, x_skip), vjp_norm = jax.vjp(lambda x_, ng: _layer_norm(tag, x_, ng), x_l, small_l["norm_g"])
        wide = [i for i in range(len(w_segs)) if i not in INPROJ_NARROW]
        segs = dict(zip(INPROJ_NARROW, _mm_nn_many(f"{tag}_inproj_fwd_narrow", h, [w_segs[i] for i in INPROJ_NARROW], INPROJ_DTYPE)))
        segs.update({i: _mm(f"{tag}_inproj_fwd{i}", h, w_segs[i], "nn", out_dtype=INPROJ_DTYPE, **INPROJ_TILES["nn"]) for i in wide})
        segs = tuple(segs[i] for i in range(len(w_segs)))
        if pending is not None:
            g = dict(g, **dict(zip(names[1:], _exchange_wait(f"gather_{tag}_others_wait", pending, segs[7]))))
        rest_g = {n: a for n, a in g.items() if n != "w_in"}
        rest_s = {n: a for n, a in small_l.items() if n != "norm_g"}
        y, vjp_tail = jax.vjp(lambda sg, x_, gg, ss: _layer_tail(tag, x_, sg, tables, _whole_weights(gg, ss)), segs, x_skip, rest_g, rest_s)

        def backward(dy):
            dsegs, dx_skip, d_rest_g, d_rest_s = vjp_tail(dy)
            others, big = _exchange_start(f"exchange_{tag}_others_start", [d_rest_g[n] for n in names if n != "w_in"], False, dsegs[7])
            dsegs = tuple(dsegs[:7]) + (big,) + tuple(dsegs[8:])
            dws = dict(zip(INPROJ_NARROW, _mm_tn_many(f"{tag}_inproj_dw_narrow", h, [dsegs[i] for i in INPROJ_NARROW], w_segs[0].dtype)))
            dws.update({i: _mm(f"{tag}_inproj_dw{i}", h, dsegs[i], "tn", out_dtype=w_segs[i].dtype, **INPROJ_TILES["tn"]) for i in wide})
            dws = tuple(dws[i] for i in range(len(w_segs)))
            projection, first =_exchange_start(f"exchange_{tag}_w_in_start", [vjp_segs(dws)[0]], False, dsegs[0])
            dsegs = (first,) + tuple(dsegs[1:])
            dh = None
            for i, group in enumerate(INPROJ_DH_GROUPS):
                dh = _mm_nt_sum(f"{tag}_inproj_dh{i}", [(dsegs[s], w_segs[s]) for s in group], dh,
                                h.dtype if i == len(INPROJ_DH_GROUPS) - 1 else F32)
            dx, d_norm_g = vjp_norm((dh, dx_skip))
            return (projection, others), dx, dict(d_rest_s, norm_g=d_norm_g)

        def landed(exchanging, after):
            projection, others = (_exchange_wait(f"exchange_{tag}_{k}_wait", e, after) for k, e in zip(("w_in", "others"), exchanging))
            return projection + others
        return y, backward, landed

    y0, backward0, landed0 = layer(0, {"w_in": w_in0}, small[0], x2, gathering0)
    gathered1 = dict(zip(names, _exchange_wait("gather_layer1_wait", gathering1, y0)))
    y1, backward1, landed1 = layer(1, gathered1, small[1], y0)
    loss_splat, dy = _loss_call(y1, target)
    loss = lax.psum(loss_splat[0, 0], ("x", "y", "c"))
    exchanging1, d_y0, g_small1 = backward1(dy)
    exchanging0, g_x, g_small0 = backward0(d_y0)

    state = lambda n, s: (view(w, n, s), view(m, n, s), view(v, n, s))
    parts1 = landed1(exchanging1, g_x)
    updated = {n: _adamw_layer(f"adamw_l1_{n}", parts1[i], *state(n, s), 1) for i, (n, s) in enumerate(SHARDED)}
    g_small = (g_small0, g_small1)
    sizes = [k for _ in range(DEPTH) for _, k in SMALL]
    g_vec = _pack_vectors([g_small[l][n] for l in range(DEPTH) for n, _ in SMALL])
    (parts_vec,) = _all_to_all([jnp.broadcast_to(g_vec[None], (N_DEV,) + g_vec.shape)], updated["w_in"][0])
    parts0 = landed0(exchanging0, parts_vec)

    vec = lambda t: _pack_vectors([t[n][l] for l in range(DEPTH) for n, _ in SMALL])[None]
    outs = {}
    for i, (n, s) in enumerate(SHARDED):
        res = _adamw_layer(f"adamw_l0_{n}", parts0[i], *state(n, s), 0, earlier=updated[n])
        outs[n] = [o.reshape(w[n].shape) for o in res]
    vec_outs = [_unpack_vectors(o, sizes) for o in _adamw_layer("adamw_vectors", parts_vec, vec(w), vec(m), vec(v), 0)]
    for i, (n, _) in enumerate(SMALL):
        outs[n] = [jnp.stack([o[l * len(SMALL) + i] for l in range(DEPTH)]) for o in vec_outs]
    return (loss, g_x[None], *[outs[n][k] for k in range(4) for n in WEIGHTS])
```
